```python
import math
import jax, jax.numpy as jnp
from jax import lax
import numpy as np

D_MODEL = 1024
BATCH = 8
SEQ = 8192
DEPTH = 4

N_MIXERS = 2
N_A_LAYERS = (DEPTH + 1) // 2
N_B_LAYERS = DEPTH // 2

A_HEADS = 16
A_HEAD_DIM = 64
A_WIDTH = A_HEADS * A_HEAD_DIM
DILATION_PAIRS = ((128, 1), (512, 4), (2048, 16))
N_DIL = len(DILATION_PAIRS)
A_QBLOCK = 128
A_IN_COLS = N_DIL * 3 * A_WIDTH + A_WIDTH

SSM_INNER = 2 * D_MODEL
SSM_HEAD_DIM = 64
SSM_HEADS = SSM_INNER // SSM_HEAD_DIM
SSM_STATE = 128
SSM_GROUPS = 4
SSM_CONV = 5
SSM_CHUNK = 128
SSM_CONV_DIM = SSM_INNER + 2 * SSM_GROUPS * SSM_STATE
SSM_IN_COLS = SSM_INNER + SSM_CONV_DIM + 2 * SSM_HEADS

DEEPNORM_ALPHA = (2 * DEPTH) ** 0.25
DEEPNORM_BETA = (8 * DEPTH) ** -0.25
LN_EPS = 1e-5
RMS_EPS = 1e-5

kernel_name = "hybrid_dilated_attn_ssd_encoder"


def _layer_norm(x, g, b):
    xf = x.astype(jnp.float32)
    mu = jnp.mean(xf, axis=-1, keepdims=True)
    var = jnp.mean(jnp.square(xf - mu), axis=-1, keepdims=True)
    y = (xf - mu) * lax.rsqrt(var + LN_EPS) * g.astype(jnp.float32) + b.astype(jnp.float32)
    return y.astype(x.dtype)


def _rms_norm(x, g):
    xf = x.astype(jnp.float32)
    y = xf * lax.rsqrt(jnp.mean(jnp.square(xf), axis=-1, keepdims=True) + RMS_EPS)
    return (y * g.astype(jnp.float32)).astype(x.dtype)


def _alibi_slopes(n):
    return jnp.asarray(2.0 ** (-8.0 * (np.arange(n, dtype=np.float32) + 1.0) / n), dtype=jnp.float32)


def _dilated_group(q, k, v, window, dil, slopes):
    bsz, s, h, dh = q.shape
    length = s // dil
    radius = window // (2 * dil)
    qb = math.gcd(length, A_QBLOCK)
    nblk = length // qb
    span = qb + 2 * radius

    def to_sub(t):
        return t.reshape(bsz, length, dil, h, dh).transpose(0, 2, 3, 1, 4)

    qs = to_sub(q).reshape(bsz, dil, h, nblk, qb, dh)
    pad = ((0, 0), (0, 0), (0, 0), (radius, radius), (0, 0))
    kp = jnp.pad(to_sub(k), pad)
    vp = jnp.pad(to_sub(v), pad)
    idx = np.arange(nblk)[:, None] * qb + np.arange(span)[None, :]
    kb = kp[:, :, :, idx]
    vb = vp[:, :, :, idx]

    scores = jnp.einsum('bdhnqe,bdhnke->bdhnqk', qs, kb).astype(jnp.float32) / math.sqrt(dh)
    delta = np.arange(span)[None, :] - radius - np.arange(qb)[:, None]
    keypos = np.arange(nblk)[:, None, None] * qb + np.arange(span)[None, None, :] - radius
    valid = (np.abs(delta)[None] <= radius) & (keypos >= 0) & (keypos < length)
    dist = jnp.asarray(np.abs(delta) * dil, dtype=jnp.float32)
    alibi = -slopes[:, None, None] * dist[None]
    scores = jnp.where(valid, scores + alibi[None, None, :, None], -jnp.inf)

    m = jnp.max(scores, axis=-1, keepdims=True)
    p = jnp.exp(scores - m)
    z = jnp.sum(p, axis=-1, keepdims=True)
    o = jnp.einsum('bdhnqk,bdhnke->bdhnqe', p.astype(v.dtype), vb) / z.astype(v.dtype)
    lse = (m + jnp.log(z))[..., 0]

    o = o.reshape(bsz, dil, h, length, dh).transpose(0, 3, 1, 2, 4).reshape(bsz, s, h, dh)
    lse = lse.reshape(bsz, dil, h, length).transpose(0, 3, 1, 2).reshape(bsz, s, h)
    return o, lse


def _dilated_attention_mixer(h, w_in, w_out):
    bsz, s, _ = h.shape
    proj = h @ w_in
    qkv = proj[..., :N_DIL * 3 * A_WIDTH].reshape(bsz, s, N_DIL, 3, A_HEADS, A_HEAD_DIM)
    gate = proj[..., N_DIL * 3 * A_WIDTH:]
    slopes = _alibi_slopes(A_HEADS)
    outs, lses = [], []
    for g, (window, dil) in enumerate(DILATION_PAIRS):
        o, l = _dilated_group(qkv[:, :, g, 0], qkv[:, :, g, 1], qkv[:, :, g, 2], window, dil, slopes)
        outs.append(o)
        lses.append(l)
    wts = jax.nn.softmax(jnp.stack(lses, axis=-1), axis=-1).astype(h.dtype)
    o = jnp.einsum('bshg,bshgd->bshd', wts, jnp.stack(outs, axis=3))
    y = o.reshape(bsz, s, A_WIDTH) * jax.nn.silu(gate)
    return y @ w_out


def _segsum_exp(a_cum):
    t = a_cum.shape[-1]
    mask = np.tril(np.ones((t, t), dtype=bool))
    diff = a_cum[..., :, None] - a_cum[..., None, :]
    return jnp.where(mask, jnp.exp(jnp.where(mask, diff, 0.0)), 0.0)


def _ssd_scan(x, dt, a, bm, cm):
    bsz, s, h, p = x.shape
    g, n = bm.shape[-2:]
    rep = h // g
    nc = s // SSM_CHUNK
    xf = x.astype(jnp.float32)
    xg = (xf * dt[..., None]).reshape(bsz, nc, SSM_CHUNK, g, rep, p)
    bc = bm.astype(jnp.float32).reshape(bsz, nc, SSM_CHUNK, g, n)
    cc = cm.astype(jnp.float32).reshape(bsz, nc, SSM_CHUNK, g, n)
    ac = (dt * a).reshape(bsz, nc, SSM_CHUNK, h).transpose(0, 3, 1, 2)
    a_cum = jnp.cumsum(ac, axis=-1)

    lmat = _segsum_exp(a_cum).reshape(bsz, g, rep, nc, SSM_CHUNK, SSM_CHUNK)
    cb = jnp.einsum('bclgn,bcsgn->bgcls', cc, bc)
    y_diag = jnp.einsum('bgrcls,bcsgrp->bclgrp', cb[:, :, None] * lmat, xg)

    decay_states = jnp.exp(a_cum[..., -1:] - a_cum).reshape(bsz, g, rep, nc, SSM_CHUNK)
    states = jnp.einsum('bcsgn,bgrcs,bcsgrp->bcgrpn', bc, decay_states, xg)

    chunk_decay = jnp.moveaxis(jnp.exp(a_cum[..., -1]).reshape(bsz, g, rep, nc), 3, 0)

    def step(hstate, inp):
        dec, st = inp
        return hstate * dec[..., None, None] + st, hstate

    h0 = jnp.zeros((bsz, g, rep, p, n), jnp.float32)
    _, h_prev = lax.scan(step, h0, (chunk_decay, jnp.moveaxis(states, 1, 0)))
    h_prev = jnp.moveaxis(h_prev, 0, 1)

    out_decay = jnp.exp(a_cum).reshape(bsz, g, rep, nc, SSM_CHUNK)
    y_off = jnp.einsum('bclgn,bcgrpn,bgrcl->bclgrp', cc, h_prev, out_decay)
    return (y_diag + y_off).reshape(bsz, s, h, p)


def _ssd_mixer(h, w_in, conv_w, conv_b, dt_bias, a_log, d_skip, norm_w, w_out):
    bsz, s, _ = h.shape
    proj = h @ w_in
    z = proj[..., :SSM_INNER]
    xbc = proj[..., SSM_INNER:SSM_INNER + SSM_CONV_DIM]
    dt_raw = proj[..., SSM_INNER + SSM_CONV_DIM:].astype(jnp.float32)

    half = SSM_CONV // 2
    xbc = lax.conv_general_dilated(
        xbc, conv_w[:, None, :].astype(xbc.dtype), window_strides=(1,), padding=[(half, half)],
        dimension_numbers=('NWC', 'WIO', 'NWC'), feature_group_count=SSM_CONV_DIM)
    xbc = jax.nn.silu(xbc + conv_b)
    xs = xbc[..., :SSM_INNER].reshape(bsz, s, SSM_HEADS, SSM_HEAD_DIM)
    bm = xbc[..., SSM_INNER:SSM_INNER + SSM_GROUPS * SSM_STATE].reshape(bsz, s, SSM_GROUPS, SSM_STATE)
    cm = xbc[..., SSM_INNER + SSM_GROUPS * SSM_STATE:].reshape(bsz, s, SSM_GROUPS, SSM_STATE)

    dtb = dt_bias.astype(jnp.float32)
    dt_f = jax.nn.softplus(dt_raw[..., :SSM_HEADS] + dtb[0])
    dt_b = jax.nn.softplus(dt_raw[..., SSM_HEADS:] + dtb[1])
    a = -jnp.exp(a_log.astype(jnp.float32))

    y_f = _ssd_scan(xs, dt_f, a[0], bm, cm)
    flip = lambda t: jnp.flip(t, axis=1)
    y_b = flip(_ssd_scan(flip(xs), flip(dt_b), a[1], flip(bm), flip(cm)))
    y = y_f + y_b + d_skip.astype(jnp.float32)[:, None] * xs.astype(jnp.float32)
    y = y.astype(h.dtype).reshape(bsz, s, SSM_INNER) * jax.nn.silu(z)
    y = _rms_norm(y, norm_w)
    return y @ w_out


def _fwd_setup_inputs(seed: int = 0) -> dict:
    key = jax.random.key(seed)
    ks = jax.random.split(key, 20)
    f32 = jnp.float32
    nrm = lambda k, shape, scale: jax.random.normal(k, shape, f32) * scale

    x = jax.random.normal(ks[0], (BATCH, SEQ, D_MODEL), f32)
    c = jax.random.normal(ks[1], (BATCH, D_MODEL), f32)
    ada_w = nrm(ks[2], (DEPTH, D_MODEL, 3 * D_MODEL), D_MODEL ** -0.5)
    ada_b = nrm(ks[3], (DEPTH, 3 * D_MODEL), 0.02)
    ln_g = 1.0 + nrm(ks[4], (DEPTH, D_MODEL), 0.02)
    ln_b = nrm(ks[5], (DEPTH, D_MODEL), 0.02)

    a_w_in = nrm(ks[6], (N_A_LAYERS, D_MODEL, A_IN_COLS), D_MODEL ** -0.5)
    a_w_out = nrm(ks[7], (N_A_LAYERS, A_WIDTH, D_MODEL), DEEPNORM_BETA * A_WIDTH ** -0.5)

    b_w_in = nrm(ks[8], (N_B_LAYERS, D_MODEL, SSM_IN_COLS), D_MODEL ** -0.5)
    b_conv_w = nrm(ks[9], (N_B_LAYERS, SSM_CONV, SSM_CONV_DIM), SSM_CONV ** -0.5)
    b_conv_b = nrm(ks[10], (N_B_LAYERS, SSM_CONV_DIM), 0.02)
    dt0 = jnp.exp(jax.random.uniform(ks[11], (N_B_LAYERS, 2, SSM_HEADS), f32,
                                     math.log(1e-3), math.log(1e-1)))
    b_dt_bias = dt0 + jnp.log(-jnp.expm1(-dt0))
    b_a_log = jnp.log(jax.random.uniform(ks[12], (N_B_LAYERS, 2, SSM_HEADS), f32, 1.0, 16.0))
    b_d = 1.0 + nrm(ks[13], (N_B_LAYERS, SSM_HEADS), 0.02)
    b_norm_w = 1.0 + nrm(ks[14], (N_B_LAYERS, SSM_INNER), 0.02)
    b_w_out = nrm(ks[15], (N_B_LAYERS, SSM_INNER, D_MODEL), DEEPNORM_BETA * SSM_INNER ** -0.5)

    return {"x": x, "c": c, "ada_w": ada_w, "ada_b": ada_b, "ln_g": ln_g, "ln_b": ln_b,
            "a_w_in": a_w_in, "a_w_out": a_w_out,
            "b_w_in": b_w_in, "b_conv_w": b_conv_w, "b_conv_b": b_conv_b,
            "b_dt_bias": b_dt_bias, "b_a_log": b_a_log, "b_d": b_d,
            "b_norm_w": b_norm_w, "b_w_out": b_w_out}


def _fwd_reference(x, c, ada_w, ada_b, ln_g, ln_b, a_w_in, a_w_out,
              b_w_in, b_conv_w, b_conv_b, b_dt_bias, b_a_log, b_d, b_norm_w, b_w_out):
    cond = jax.nn.silu(c)
    for i in range(DEPTH):
        mod = cond @ ada_w[i] + ada_b[i]
        shift, scale, gate = jnp.split(mod, 3, axis=-1)
        h = x * (1.0 + scale[:, None, :]) + shift[:, None, :]
        j = i // N_MIXERS
        if i % N_MIXERS == 0:
            y = _dilated_attention_mixer(h, a_w_in[j], a_w_out[j])
        else:
            y = _ssd_mixer(h, b_w_in[j], b_conv_w[j], b_conv_b[j], b_dt_bias[j],
                           b_a_log[j], b_d[j], b_norm_w[j], b_w_out[j])
        x = _layer_norm(DEEPNORM_ALPHA * x + gate[:, None, :] * y, ln_g[i], ln_b[i])
    return x


import jax as _jax
import jax.numpy as _jnp

TWIN_FORMAT = 'train_step'
FWD_PARAMS = ['x', 'c', 'ada_w', 'ada_b', 'ln_g', 'ln_b', 'a_w_in', 'a_w_out', 'b_w_in', 'b_conv_w', 'b_conv_b', 'b_dt_bias', 'b_a_log', 'b_d', 'b_norm_w', 'b_w_out']
TWIN_WEIGHTS = ['ada_w', 'ada_b', 'ln_g', 'ln_b', 'a_w_in', 'a_w_out', 'b_w_in', 'b_conv_w', 'b_conv_b', 'b_dt_bias', 'b_a_log', 'b_d', 'b_norm_w', 'b_w_out']
TWIN_DIFF_INPUT = 'x'
TWIN_INPUTS = ['x', 'c', 'ada_w', 'ada_b', 'ln_g', 'ln_b', 'a_w_in', 'a_w_out', 'b_w_in', 'b_conv_w', 'b_conv_b', 'b_dt_bias', 'b_a_log', 'b_d', 'b_norm_w', 'b_w_out', 'loss_target', 'm_ada_w', 'm_ada_b', 'm_ln_g', 'm_ln_b', 'm_a_w_in', 'm_a_w_out', 'm_b_w_in', 'm_b_conv_w', 'm_b_conv_b', 'm_b_dt_bias', 'm_b_a_log', 'm_b_d', 'm_b_norm_w', 'm_b_w_out', 'v_ada_w', 'v_ada_b', 'v_ln_g', 'v_ln_b', 'v_a_w_in', 'v_a_w_out', 'v_b_w_in', 'v_b_conv_w', 'v_b_conv_b', 'v_b_dt_bias', 'v_b_a_log', 'v_b_d', 'v_b_norm_w', 'v_b_w_out']
TWIN_OUTPUTS = ['loss', 'grad_x', 'grad_ada_w', 'grad_ada_b', 'grad_ln_g', 'grad_ln_b', 'grad_a_w_in', 'grad_a_w_out', 'grad_b_w_in', 'grad_b_conv_w', 'grad_b_conv_b', 'grad_b_dt_bias', 'grad_b_a_log', 'grad_b_d', 'grad_b_norm_w', 'grad_b_w_out', 'delta_ada_w', 'delta_ada_b', 'delta_ln_g', 'delta_ln_b', 'delta_a_w_in', 'delta_a_w_out', 'delta_b_w_in', 'delta_b_conv_w', 'delta_b_conv_b', 'delta_b_dt_bias', 'delta_b_a_log', 'delta_b_d', 'delta_b_norm_w', 'delta_b_w_out', 'new_m_ada_w', 'new_m_ada_b', 'new_m_ln_g', 'new_m_ln_b', 'new_m_a_w_in', 'new_m_a_w_out', 'new_m_b_w_in', 'new_m_b_conv_w', 'new_m_b_conv_b', 'new_m_b_dt_bias', 'new_m_b_a_log', 'new_m_b_d', 'new_m_b_norm_w', 'new_m_b_w_out', 'new_v_ada_w', 'new_v_ada_b', 'new_v_ln_g', 'new_v_ln_b', 'new_v_a_w_in', 'new_v_a_w_out', 'new_v_b_w_in', 'new_v_b_conv_w', 'new_v_b_conv_b', 'new_v_b_dt_bias', 'new_v_b_a_log', 'new_v_b_d', 'new_v_b_norm_w', 'new_v_b_w_out']
TWIN_LEAF_KINDS = {'loss': 'loss', 'grad_x': 'grad_x', 'grad_ada_w': 'grad_w', 'grad_ada_b': 'grad_w', 'grad_ln_g': 'grad_w', 'grad_ln_b': 'grad_w', 'grad_a_w_in': 'grad_w', 'grad_a_w_out': 'grad_w', 'grad_b_w_in': 'grad_w', 'grad_b_conv_w': 'grad_w', 'grad_b_conv_b': 'grad_w', 'grad_b_dt_bias': 'grad_w', 'grad_b_a_log': 'grad_w', 'grad_b_d': 'grad_w', 'grad_b_norm_w': 'grad_w', 'grad_b_w_out': 'grad_w', 'delta_ada_w': 'delta_w', 'delta_ada_b': 'delta_w', 'delta_ln_g': 'delta_w', 'delta_ln_b': 'delta_w', 'delta_a_w_in': 'delta_w', 'delta_a_w_out': 'delta_w', 'delta_b_w_in': 'delta_w', 'delta_b_conv_w': 'delta_w', 'delta_b_conv_b': 'delta_w', 'delta_b_dt_bias': 'delta_w', 'delta_b_a_log': 'delta_w', 'delta_b_d': 'delta_w', 'delta_b_norm_w': 'delta_w', 'delta_b_w_out': 'delta_w', 'new_m_ada_w': 'new_m', 'new_m_ada_b': 'new_m', 'new_m_ln_g': 'new_m', 'new_m_ln_b': 'new_m', 'new_m_a_w_in': 'new_m', 'new_m_a_w_out': 'new_m', 'new_m_b_w_in': 'new_m', 'new_m_b_conv_w': 'new_m', 'new_m_b_conv_b': 'new_m', 'new_m_b_dt_bias': 'new_m', 'new_m_b_a_log': 'new_m', 'new_m_b_d': 'new_m', 'new_m_b_norm_w': 'new_m', 'new_m_b_w_out': 'new_m', 'new_v_ada_w': 'new_v', 'new_v_ada_b': 'new_v', 'new_v_ln_g': 'new_v', 'new_v_ln_b': 'new_v', 'new_v_a_w_in': 'new_v', 'new_v_a_w_out': 'new_v', 'new_v_b_w_in': 'new_v', 'new_v_b_conv_w': 'new_v', 'new_v_b_conv_b': 'new_v', 'new_v_b_dt_bias': 'new_v', 'new_v_b_a_log': 'new_v', 'new_v_b_d': 'new_v', 'new_v_b_norm_w': 'new_v', 'new_v_b_w_out': 'new_v'}


def _forward(args):
    return _fwd_reference(*[args[k] for k in FWD_PARAMS])


def _output_shape():
    def fwd():
        inp = _fwd_setup_inputs(0)
        return _fwd_reference(*[inp[k] for k in FWD_PARAMS])
    out = _jax.eval_shape(fwd)
    return out.shape, out.dtype

N_MICROBATCH = 1
ADAM_LR = 0.001
ADAM_B1 = 0.9
ADAM_B2 = 0.999
ADAM_EPS = 1e-08
ADAM_WD = 0.01
ADAM_STEP = 10
PER_EXAMPLE_BATCH_AXIS = {'x': 0, 'c': 0, 'loss_target': 0}
SHARED_INPUTS = []
_WEIGHT_DTYPES = {'ada_w': _jnp.float32, 'ada_b': _jnp.float32, 'ln_g': _jnp.float32, 'ln_b': _jnp.float32, 'a_w_in': _jnp.float32, 'a_w_out': _jnp.float32, 'b_w_in': _jnp.float32, 'b_conv_w': _jnp.float32, 'b_conv_b': _jnp.float32, 'b_dt_bias': _jnp.float32, 'b_a_log': _jnp.float32, 'b_d': _jnp.float32, 'b_norm_w': _jnp.float32, 'b_w_out': _jnp.float32}
MOMENT_SCALE = {'ada_w': 2.868974e-02, 'ada_b': 5.123073e-02, 'ln_g': 3.213031e+01, 'ln_b': 1.756392e+00, 'a_w_in': 1.191373e-02, 'a_w_out': 4.873894e-02, 'b_w_in': 2.675991e-02, 'b_conv_w': 2.508821e-02, 'b_conv_b': 2.758461e-02, 'b_dt_bias': 5.184367e-02, 'b_a_log': 9.121411e-02, 'b_d': 5.010540e-02, 'b_norm_w': 2.863824e-02, 'b_w_out': 9.794032e-02}


def _to_microbatches(a, axis):
    t = _jnp.moveaxis(a, axis, 0)
    t = t.reshape((N_MICROBATCH, t.shape[0] // N_MICROBATCH) + t.shape[1:])
    return _jnp.moveaxis(t, 1, axis + 1)


def setup_inputs(seed: int = 0) -> dict:
    inp = _fwd_setup_inputs(seed)
    key = _jax.random.fold_in(_jax.random.key(seed), 7919)
    shape, _ = _output_shape()
    out = dict(inp)
    out["loss_target"] = _jax.random.normal(_jax.random.fold_in(key, 0), shape, _jnp.float32)
    for i, name in enumerate(TWIN_WEIGHTS):
        w = inp[name].astype(_jnp.float32)
        if MOMENT_SCALE is None:
            s = _jnp.sqrt(_jnp.mean(_jnp.square(w)) + 1e-30)
        else:
            s = MOMENT_SCALE[name]
        km, kv = _jax.random.split(_jax.random.fold_in(key, i + 1))
        out[name] = w
        out["m_" + name] = s * _jax.random.normal(km, w.shape, _jnp.float32)
        out["v_" + name] = (s * s) * _jax.random.uniform(kv, w.shape, _jnp.float32, 0.5, 1.5)
    if N_MICROBATCH > 1:
        for name, axis in PER_EXAMPLE_BATCH_AXIS.items():
            out[name] = _to_microbatches(out[name], axis)
    return {'x': out['x'], 'c': out['c'], 'ada_w': out['ada_w'], 'ada_b': out['ada_b'], 'ln_g': out['ln_g'], 'ln_b': out['ln_b'], 'a_w_in': out['a_w_in'], 'a_w_out': out['a_w_out'], 'b_w_in': out['b_w_in'], 'b_conv_w': out['b_conv_w'], 'b_conv_b': out['b_conv_b'], 'b_dt_bias': out['b_dt_bias'], 'b_a_log': out['b_a_log'], 'b_d': out['b_d'], 'b_norm_w': out['b_norm_w'], 'b_w_out': out['b_w_out'], 'loss_target': out['loss_target'], 'm_ada_w': out['m_ada_w'], 'm_ada_b': out['m_ada_b'], 'm_ln_g': out['m_ln_g'], 'm_ln_b': out['m_ln_b'], 'm_a_w_in': out['m_a_w_in'], 'm_a_w_out': out['m_a_w_out'], 'm_b_w_in': out['m_b_w_in'], 'm_b_conv_w': out['m_b_conv_w'], 'm_b_conv_b': out['m_b_conv_b'], 'm_b_dt_bias': out['m_b_dt_bias'], 'm_b_a_log': out['m_b_a_log'], 'm_b_d': out['m_b_d'], 'm_b_norm_w': out['m_b_norm_w'], 'm_b_w_out': out['m_b_w_out'], 'v_ada_w': out['v_ada_w'], 'v_ada_b': out['v_ada_b'], 'v_ln_g': out['v_ln_g'], 'v_ln_b': out['v_ln_b'], 'v_a_w_in': out['v_a_w_in'], 'v_a_w_out': out['v_a_w_out'], 'v_b_w_in': out['v_b_w_in'], 'v_b_conv_w': out['v_b_conv_w'], 'v_b_conv_b': out['v_b_conv_b'], 'v_b_dt_bias': out['v_b_dt_bias'], 'v_b_a_log': out['v_b_a_log'], 'v_b_d': out['v_b_d'], 'v_b_norm_w': out['v_b_norm_w'], 'v_b_w_out': out['v_b_w_out']}


def _loss(weights, diff, rest, loss_target):
    with _jax.named_scope("forward"):
        args = {**rest, TWIN_DIFF_INPUT: diff, **{k: w.astype(_WEIGHT_DTYPES[k]) for k, w in weights.items()}}
        y = _forward(args)
    with _jax.named_scope("loss_head"):
        err = _jnp.square(y.astype(_jnp.float32) - loss_target)
        return 0.5 * _jnp.sum(_jnp.mean(err, axis=-1)) if err.ndim else 0.5 * err


def _adamw(w, g, m, v):
    m = ADAM_B1 * m + (1.0 - ADAM_B1) * g
    v = ADAM_B2 * v + (1.0 - ADAM_B2) * _jnp.square(g)
    m_hat = m / (1.0 - ADAM_B1 ** ADAM_STEP)
    v_hat = v / (1.0 - ADAM_B2 ** ADAM_STEP)
    delta = -ADAM_LR * (m_hat / (_jnp.sqrt(v_hat) + ADAM_EPS) + ADAM_WD * w)
    return delta, m, v


def reference(x, c, ada_w, ada_b, ln_g, ln_b, a_w_in, a_w_out, b_w_in, b_conv_w, b_conv_b, b_dt_bias, b_a_log, b_d, b_norm_w, b_w_out, loss_target, m_ada_w, m_ada_b, m_ln_g, m_ln_b, m_a_w_in, m_a_w_out, m_b_w_in, m_b_conv_w, m_b_conv_b, m_b_dt_bias, m_b_a_log, m_b_d, m_b_norm_w, m_b_w_out, v_ada_w, v_ada_b, v_ln_g, v_ln_b, v_a_w_in, v_a_w_out, v_b_w_in, v_b_conv_w, v_b_conv_b, v_b_dt_bias, v_b_a_log, v_b_d, v_b_norm_w, v_b_w_out):
    given = dict(x=x, c=c, ada_w=ada_w, ada_b=ada_b, ln_g=ln_g, ln_b=ln_b, a_w_in=a_w_in, a_w_out=a_w_out, b_w_in=b_w_in, b_conv_w=b_conv_w, b_conv_b=b_conv_b, b_dt_bias=b_dt_bias, b_a_log=b_a_log, b_d=b_d, b_norm_w=b_norm_w, b_w_out=b_w_out, loss_target=loss_target, m_ada_w=m_ada_w, m_ada_b=m_ada_b, m_ln_g=m_ln_g, m_ln_b=m_ln_b, m_a_w_in=m_a_w_in, m_a_w_out=m_a_w_out, m_b_w_in=m_b_w_in, m_b_conv_w=m_b_conv_w, m_b_conv_b=m_b_conv_b, m_b_dt_bias=m_b_dt_bias, m_b_a_log=m_b_a_log, m_b_d=m_b_d, m_b_norm_w=m_b_norm_w, m_b_w_out=m_b_w_out, v_ada_w=v_ada_w, v_ada_b=v_ada_b, v_ln_g=v_ln_g, v_ln_b=v_ln_b, v_a_w_in=v_a_w_in, v_a_w_out=v_a_w_out, v_b_w_in=v_b_w_in, v_b_conv_w=v_b_conv_w, v_b_conv_b=v_b_conv_b, v_b_dt_bias=v_b_dt_bias, v_b_a_log=v_b_a_log, v_b_d=v_b_d, v_b_norm_w=v_b_norm_w, v_b_w_out=v_b_w_out)
    weights = {n: given[n] for n in TWIN_WEIGHTS}
    shared = {n: given[n] for n in SHARED_INPUTS}
    per_example = {n: given[n] for n in ['x', 'c']}
    grad_fn = _jax.value_and_grad(_loss, argnums=(0, 1))

    def one_microbatch(ex, loss_target):
        ex = dict(ex)
        diff = ex.pop(TWIN_DIFF_INPUT)
        return grad_fn(weights, diff, {**shared, **ex}, loss_target)

    if N_MICROBATCH == 1:
        loss, (grad_w, grad_x) = one_microbatch(per_example, given["loss_target"])
    else:
        def body(carry, xs):
            loss_sum, grad_sum = carry
            l_k, (gw_k, gx_k) = one_microbatch(xs[0], xs[1])
            with _jax.named_scope("update"):
                return (loss_sum + l_k, _jax.tree.map(_jnp.add, grad_sum, gw_k)), gx_k

        init = (_jnp.zeros((), _jnp.float32), _jax.tree.map(_jnp.zeros_like, weights))
        (loss, grad_w), grad_x = _jax.lax.scan(body, init, (per_example, given["loss_target"]))
    with _jax.named_scope("update"):
        delta_w, new_m, new_v = {}, {}, {}
        for n in TWIN_WEIGHTS:
            delta_w[n], new_m[n], new_v[n] = _adamw(weights[n], grad_w[n], given["m_" + n], given["v_" + n])
    return (loss, grad_x, *[grad_w[n] for n in TWIN_WEIGHTS], *[delta_w[n] for n in TWIN_WEIGHTS],
            *[new_m[n] for n in TWIN_WEIGHTS], *[new_v[n] for n in TWIN_WEIGHTS])
```

```python
import functools
import math

import jax
import jax.numpy as jnp
import numpy as np
from jax import lax
from jax.experimental import pallas as pl
from jax.experimental.pallas import tpu as pltpu

F32 = jnp.float32
BF16 = jnp.bfloat16

DEPTH = 4
A_HEADS = 16
A_HEAD_DIM = 64
A_WIDTH = A_HEADS * A_HEAD_DIM
DILATIONS = (1, 4, 16)
A_RADIUS = 64
A_QBLOCK = 128
SSM_HEADS = 32
SSM_HEAD_DIM = 64
SSM_STATE = 128
SSM_GROUPS = 4
SSM_REP = SSM_HEADS // SSM_GROUPS
SSM_CONV = 5
SSM_CHUNK = 128
DEEPNORM_ALPHA = (2 * DEPTH) ** 0.25
LN_EPS = 1e-5
RMS_EPS = 1e-5
ADAM_LR, ADAM_B1, ADAM_B2, ADAM_EPS, ADAM_WD, ADAM_STEP = 0.001, 0.9, 0.999, 1e-08, 0.01, 10
NEG_BIG = -1e30
VMEM_LIMIT = 56 * 1024 * 1024
LANE = 128


def _cp(*sem):
    return pltpu.CompilerParams(dimension_semantics=sem, vmem_limit_bytes=VMEM_LIMIT)


def _tile(dim, target):
    if dim <= target:
        return dim
    t = (target // LANE) * LANE
    while dim % t:
        t -= LANE
    return t


def _sigmoid(x):
    return 1.0 / (1.0 + jnp.exp(-x))


def _silu(x):
    return x * _sigmoid(x)


def _dsilu(x):
    s = _sigmoid(x)
    return s * (1.0 + x * (1.0 - s))


def _split3(x):
    a = x.astype(BF16)
    r = x - a.astype(F32)
    b = r.astype(BF16)
    c = (r - b.astype(F32)).astype(BF16)
    return a, b, c


def _dot(a, b, ca=1, cb=0):
    return lax.dot_general(a, b, (((ca,), (cb,)), ((), ())), preferred_element_type=F32)


def _dot_exact(m01, x):
    a, b, c = _split3(x)
    return _dot(m01, a) + _dot(m01, b) + _dot(m01, c)


def _mm(a, b, *, ta=False, tb=False, add=None, out_dtype=F32, name, tm=1024, tn=1024, tk=512):
    m, k = (a.shape[1], a.shape[0]) if ta else a.shape
    n = b.shape[0] if tb else b.shape[1]
    assert (b.shape[1] if tb else b.shape[0]) == k
    tm, tn, tk = _tile(m, tm), _tile(n, tn), _tile(k, tk)
    nk = k // tk
    has_add = add is not None

    def body(*refs):
        if has_add:
            a_ref, b_ref, c_ref, o_ref, acc = refs
        else:
            a_ref, b_ref, o_ref, acc = refs
        kk = pl.program_id(2)

        @pl.when(kk == 0)
        def _():
            acc[...] = jnp.zeros_like(acc)

        av = a_ref[...].astype(BF16)
        bv = b_ref[...].astype(BF16)
        acc[...] += _dot(av, bv, 0 if ta else 1, 1 if tb else 0)

        @pl.when(kk == nk - 1)
        def _():
            r = acc[...]
            if has_add:
                r = r + c_ref[...]
            o_ref[...] = r.astype(o_ref.dtype)

    a_spec = pl.BlockSpec((tk, tm), lambda i, j, kk: (kk, i)) if ta else pl.BlockSpec((tm, tk), lambda i, j, kk: (i, kk))
    b_spec = pl.BlockSpec((tn, tk), lambda i, j, kk: (j, kk)) if tb else pl.BlockSpec((tk, tn), lambda i, j, kk: (kk, j))
    in_specs = [a_spec, b_spec]
    args = [a, b]
    if has_add:
        in_specs.append(pl.BlockSpec((tm, tn), lambda i, j, kk: (i, j)))
        args.append(add)
    return pl.pallas_call(
        body, name=name, grid=(m // tm, n // tn, nk), in_specs=in_specs,
        out_specs=pl.BlockSpec((tm, tn), lambda i, j, kk: (i, j)),
        out_shape=jax.ShapeDtypeStruct((m, n), out_dtype),
        scratch_shapes=[pltpu.VMEM((tm, tn), F32)],
        compiler_params=_cp("parallel", "parallel", "arbitrary"),
    )(*args)


ROWS = 512


def _row_spec(tm, d):
    return pl.BlockSpec((tm, d), lambda i: (i, 0))


def _vec_spec(d, rows=1):
    return pl.BlockSpec((rows, d), lambda i: (0, 0))


def _modulate(x, scale, shift, name):
    s, d = x.shape
    tm = min(ROWS, s)

    def body(x_ref, sc_ref, sh_ref, o_ref):
        o_ref[...] = (x_ref[...] * (1.0 + sc_ref[...]) + sh_ref[...]).astype(BF16)

    return pl.pallas_call(
        body, name=name, grid=(s // tm,), in_specs=[_row_spec(tm, d), _vec_spec(d), _vec_spec(d)],
        out_specs=_row_spec(tm, d), out_shape=jax.ShapeDtypeStruct((s, d), BF16), compiler_params=_cp("parallel"),
    )(x, scale, shift)


def _resid_ln(x, y, gate, g, b, name):
    s, d = x.shape
    tm = min(ROWS, s)

    def body(x_ref, y_ref, gt_ref, g_ref, b_ref, o_ref):
        u = DEEPNORM_ALPHA * x_ref[...] + gt_ref[...] * y_ref[...]
        mu = jnp.mean(u, axis=1, keepdims=True)
        uc = u - mu
        var = jnp.mean(uc * uc, axis=1, keepdims=True)
        o_ref[...] = uc * lax.rsqrt(var + LN_EPS) * g_ref[...] + b_ref[...]

    return pl.pallas_call(
        body, name=name, grid=(s // tm,),
        in_specs=[_row_spec(tm, d), _row_spec(tm, d), _vec_spec(d), _vec_spec(d), _vec_spec(d)],
        out_specs=_row_spec(tm, d), out_shape=jax.ShapeDtypeStruct((s, d), F32), compiler_params=_cp("parallel"),
    )(x, y, gate, g, b)


def _resid_ln_bwd(x, y, dxn, gate, g, name):
    s, d = x.shape
    tm = min(ROWS, s)

    def body(x_ref, y_ref, dxn_ref, gt_ref, g_ref, du_ref, dy_ref, red_ref):
        @pl.when(pl.program_id(0) == 0)
        def _():
            red_ref[...] = jnp.zeros_like(red_ref)

        yv = y_ref[...]
        u = DEEPNORM_ALPHA * x_ref[...] + gt_ref[...] * yv
        mu = jnp.mean(u, axis=1, keepdims=True)
        uc = u - mu
        var = jnp.mean(uc * uc, axis=1, keepdims=True)
        rstd = lax.rsqrt(var + LN_EPS)
        xhat = uc * rstd
        dxnv = dxn_ref[...]
        dxh = dxnv * g_ref[...]
        du = rstd * (dxh - jnp.mean(dxh, axis=1, keepdims=True) - xhat * jnp.mean(dxh * xhat, axis=1, keepdims=True))
        du_ref[...] = du
        dy_ref[...] = (du * gt_ref[...]).astype(BF16)
        red_ref[0:1, :] += jnp.sum(du * yv, axis=0, keepdims=True)
        red_ref[1:2, :] += jnp.sum(dxnv * xhat, axis=0, keepdims=True)
        red_ref[2:3, :] += jnp.sum(dxnv, axis=0, keepdims=True)

    return pl.pallas_call(
        body, name=name, grid=(s // tm,),
        in_specs=[_row_spec(tm, d), _row_spec(tm, d), _row_spec(tm, d), _vec_spec(d), _vec_spec(d)],
        out_specs=[_row_spec(tm, d), _row_spec(tm, d), _vec_spec(d, 8)],
        out_shape=[jax.ShapeDtypeStruct((s, d), F32), jax.ShapeDtypeStruct((s, d), BF16), jax.ShapeDtypeStruct((8, d), F32)],
        compiler_params=_cp("arbitrary"),
    )(x, y, dxn, gate, g)


def _modulate_bwd(du, dh, x, scale, name):
    s, d = x.shape
    tm = min(ROWS, s)

    def body(du_ref, dh_ref, x_ref, sc_ref, dx_ref, red_ref):
        @pl.when(pl.program_id(0) == 0)
        def _():
            red_ref[...] = jnp.zeros_like(red_ref)

        dhv = dh_ref[...]
        dx_ref[...] = DEEPNORM_ALPHA * du_ref[...] + dhv * (1.0 + sc_ref[...])
        red_ref[0:1, :] += jnp.sum(dhv * x_ref[...], axis=0, keepdims=True)
        red_ref[1:2, :] += jnp.sum(dhv, axis=0, keepdims=True)

    return pl.pallas_call(
        body, name=name, grid=(s // tm,),
        in_specs=[_row_spec(tm, d), _row_spec(tm, d), _row_spec(tm, d), _vec_spec(d)],
        out_specs=[_row_spec(tm, d), _vec_spec(d, 8)],
        out_shape=[jax.ShapeDtypeStruct((s, d), F32), jax.ShapeDtypeStruct((8, d), F32)],
        compiler_params=_cp("arbitrary"),
    )(du, dh, x, scale)


def _loss_grad(xf, tgt, name):
    s, d = xf.shape
    tm = min(ROWS, s)

    def body(x_ref, t_ref, dx_ref, red_ref):
        @pl.when(pl.program_id(0) == 0)
        def _():
            red_ref[...] = jnp.zeros_like(red_ref)

        e = x_ref[...] - t_ref[...]
        dx_ref[...] = e * (1.0 / d)
        red_ref[0:1, :] += jnp.sum(e * e, axis=0, keepdims=True)

    return pl.pallas_call(
        body, name=name, grid=(s // tm,), in_specs=[_row_spec(tm, d), _row_spec(tm, d)],
        out_specs=[_row_spec(tm, d), _vec_spec(d, 8)],
        out_shape=[jax.ShapeDtypeStruct((s, d), F32), jax.ShapeDtypeStruct((8, d), F32)],
        compiler_params=_cp("arbitrary"),
    )(xf, tgt)


QKV_COLS = 3 * 3 * A_WIDTH
HP = A_WIDTH // LANE


def _slope_lanes():
    sl = 2.0 ** (-8.0 * (np.arange(A_HEADS, dtype=np.float32) + 1.0) / A_HEADS)
    return jnp.asarray(np.repeat(sl, A_HEAD_DIM).reshape(HP, 1, LANE), dtype=F32)


def _band(n, length, dil, span_rows):
    shape = (2 * A_QBLOCK, A_QBLOCK) if span_rows else (A_QBLOCK, 2 * A_QBLOCK)
    r = lax.broadcasted_iota(jnp.int32, shape, 0)
    c = lax.broadcasted_iota(jnp.int32, shape, 1)
    sp, ce = (r, c) if span_rows else (c, r)
    delta = sp - A_RADIUS - ce
    pos = n * A_QBLOCK - A_RADIUS + sp
    valid = (jnp.abs(delta) <= A_RADIUS) & (pos >= 0) & (pos < length)
    dist = jnp.abs(delta).astype(F32) * float(dil)
    return valid, dist


def _span_specs(colblock, nb64):
    def mk(i):
        return pl.BlockSpec((64, LANE), lambda r, n, hp: (jnp.clip(2 * n - 1 + i, 0, nb64 - 1), colblock(r, hp)))
    return [mk(i) for i in range(4)]


def _attn_fwd(qkv, g, name):
    s = qkv.shape[0]
    dil = DILATIONS[g]
    length = s // dil
    nblk = length // A_QBLOCK
    cpr = QKV_COLS // LANE
    base = g * 3 * HP
    view = qkv.reshape(length, dil * QKV_COLS)

    def body(sl_ref, q_ref, k0, k1, k2, k3, v0, v1, v2, v3, o_ref, l_ref):
        n = pl.program_id(1)
        q = q_ref[...]
        kk = jnp.concatenate([k0[...], k1[...], k2[...], k3[...]], axis=0)
        vv = jnp.concatenate([v0[...], v1[...], v2[...], v3[...]], axis=0)
        valid, dist = _band(n, length, dil, False)
        for hh in range(2):
            cs = slice(hh * 64, hh * 64 + 64)
            sc = _dot(q[:, cs], kk[:, cs], 1, 1) * 0.125
            slope = sl_ref[0:1, hh * 64:hh * 64 + 1]
            sc = jnp.where(valid, sc - slope * dist, NEG_BIG)
            m = jnp.max(sc, axis=1, keepdims=True)
            p = jnp.exp(sc - m)
            z = jnp.sum(p, axis=1, keepdims=True)
            o = _dot(p.astype(BF16), vv[:, cs]) / z
            o_ref[:, cs] = o
            l_ref[:, cs] = jnp.broadcast_to(m + jnp.log(z), (A_QBLOCK, 64))

    qspec = pl.BlockSpec((A_QBLOCK, LANE), lambda r, n, hp: (n, r * cpr + base + hp))
    kspecs = _span_specs(lambda r, hp: r * cpr + base + HP + hp, 2 * nblk)
    vspecs = _span_specs(lambda r, hp: r * cpr + base + 2 * HP + hp, 2 * nblk)
    ospec = pl.BlockSpec((A_QBLOCK, LANE), lambda r, n, hp: (n, r * HP + hp))
    o, l = pl.pallas_call(
        body, name=name, grid=(dil, nblk, HP),
        in_specs=[pl.BlockSpec((None, 1, LANE), lambda r, n, hp: (hp, 0, 0)), qspec] + kspecs + vspecs,
        out_specs=[ospec, ospec],
        out_shape=[jax.ShapeDtypeStruct((length, dil * A_WIDTH), F32)] * 2,
        compiler_params=_cp("parallel", "parallel", "parallel"),
    )(_slope_lanes(), view, *([view] * 8))
    return o.reshape(s, A_WIDTH), l.reshape(s, A_WIDTH)


def _attn_merge(os_, ls_, gate, name):
    s, w = gate.shape
    tm = min(ROWS, s)

    def body(o0, o1, o2, l0, l1, l2, g_ref, y_ref, o_ref, l_ref):
        a, b, c = l0[...], l1[...], l2[...]
        m = jnp.maximum(jnp.maximum(a, b), c)
        ea, eb, ec = jnp.exp(a - m), jnp.exp(b - m), jnp.exp(c - m)
        z = ea + eb + ec
        o = (ea * o0[...] + eb * o1[...] + ec * o2[...]) / z
        o_ref[...] = o
        l_ref[...] = m + jnp.log(z)
        y_ref[...] = (o * _silu(g_ref[...])).astype(BF16)

    rs = _row_spec(tm, w)
    return pl.pallas_call(
        body, name=name, grid=(s // tm,), in_specs=[rs] * 7, out_specs=[rs] * 3,
        out_shape=[jax.ShapeDtypeStruct((s, w), BF16), jax.ShapeDtypeStruct((s, w), F32), jax.ShapeDtypeStruct((s, w), F32)],
        compiler_params=_cp("parallel"),
    )(*os_, *ls_, gate)


def _attn_gate_bwd(dyy, o, gate, name):
    s, w = gate.shape
    tm = min(ROWS, s)

    def body(dy_ref, o_ref, g_ref, do_ref, dg_ref, dl_ref):
        dyv, ov, gv = dy_ref[...], o_ref[...], g_ref[...]
        do = dyv * _silu(gv)
        do_ref[...] = do
        dg_ref[...] = (dyv * ov * _dsilu(gv)).astype(BF16)
        prod = do * ov
        for h in range(A_HEADS):
            cs = slice(h * 64, h * 64 + 64)
            dl_ref[:, cs] = jnp.broadcast_to(jnp.sum(prod[:, cs], axis=1, keepdims=True), (tm, 64))

    rs = _row_spec(tm, w)
    return pl.pallas_call(
        body, name=name, grid=(s // tm,), in_specs=[rs] * 3, out_specs=[rs] * 3,
        out_shape=[jax.ShapeDtypeStruct((s, w), F32), jax.ShapeDtypeStruct((s, w), BF16), jax.ShapeDtypeStruct((s, w), F32)],
        compiler_params=_cp("parallel"),
    )(dyy, o, gate)


def _attn_dq(qkv, do, lse, delta, g, name):
    s = qkv.shape[0]
    dil = DILATIONS[g]
    length = s // dil
    nblk = length // A_QBLOCK
    cpr = QKV_COLS // LANE
    base = g * 3 * HP
    view = qkv.reshape(length, dil * QKV_COLS)
    wide = lambda t: t.reshape(length, dil * A_WIDTH)

    def body(sl_ref, q_ref, k0, k1, k2, k3, v0, v1, v2, v3, do_ref, l_ref, dl_ref, dq_ref):
        n = pl.program_id(1)
        q = q_ref[...]
        kk = jnp.concatenate([k0[...], k1[...], k2[...], k3[...]], axis=0)
        vv = jnp.concatenate([v0[...], v1[...], v2[...], v3[...]], axis=0)
        valid, dist = _band(n, length, dil, False)
        for hh in range(2):
            cs = slice(hh * 64, hh * 64 + 64)
            sc = _dot(q[:, cs], kk[:, cs], 1, 1) * 0.125
            slope = sl_ref[0:1, hh * 64:hh * 64 + 1]
            sc = jnp.where(valid, sc - slope * dist, NEG_BIG)
            p = jnp.exp(sc - l_ref[:, hh * 64:hh * 64 + 1])
            dp = _dot(do_ref[:, cs].astype(BF16), vv[:, cs], 1, 1)
            ds = p * (dp - dl_ref[:, hh * 64:hh * 64 + 1])
            dq_ref[:, cs] = (_dot(ds.astype(BF16), kk[:, cs]) * 0.125).astype(BF16)

    qspec = pl.BlockSpec((A_QBLOCK, LANE), lambda r, n, hp: (n, r * cpr + base + hp))
    kspecs = _span_specs(lambda r, hp: r * cpr + base + HP + hp, 2 * nblk)
    vspecs = _span_specs(lambda r, hp: r * cpr + base + 2 * HP + hp, 2 * nblk)
    ospec = pl.BlockSpec((A_QBLOCK, LANE), lambda r, n, hp: (n, r * HP + hp))
    dq = pl.pallas_call(
        body, name=name, grid=(dil, nblk, HP),
        in_specs=[pl.BlockSpec((None, 1, LANE), lambda r, n, hp: (hp, 0, 0)), qspec] + kspecs + vspecs + [ospec] * 3,
        out_specs=ospec, out_shape=jax.ShapeDtypeStruct((length, dil * A_WIDTH), BF16),
        compiler_params=_cp("parallel", "parallel", "parallel"),
    )(_slope_lanes(), view, *([view] * 8), wide(do), wide(lse), wide(delta))
    return dq.reshape(s, A_WIDTH)


def _attn_dkv(qkv, do, lse, delta, g, name):
    s = qkv.shape[0]
    dil = DILATIONS[g]
    length = s // dil
    nblk = length // A_QBLOCK
    cpr = QKV_COLS // LANE
    base = g * 3 * HP
    view = qkv.reshape(length, dil * QKV_COLS)
    wide = lambda t: t.reshape(length, dil * A_WIDTH)

    def body(sl_ref, k_ref, v_ref, q0, q1, q2, q3, d0, d1, d2, d3, l0, l1, l2, l3, e0, e1, e2, e3, dk_ref, dv_ref):
        n = pl.program_id(1)
        k = k_ref[...]
        v = v_ref[...]
        qq = jnp.concatenate([q0[...], q1[...], q2[...], q3[...]], axis=0)
        dd = jnp.concatenate([d0[...], d1[...], d2[...], d3[...]], axis=0)
        ll = jnp.concatenate([l0[...], l1[...], l2[...], l3[...]], axis=0)
        ee = jnp.concatenate([e0[...], e1[...], e2[...], e3[...]], axis=0)
        valid, dist = _band(n, length, dil, True)
        for hh in range(2):
            cs = slice(hh * 64, hh * 64 + 64)
            sc = _dot(qq[:, cs], k[:, cs], 1, 1) * 0.125
            slope = sl_ref[0:1, hh * 64:hh * 64 + 1]
            sc = jnp.where(valid, sc - slope * dist, NEG_BIG)
            p = jnp.exp(sc - ll[:, hh * 64:hh * 64 + 1])
            dob = dd[:, cs].astype(BF16)
            dv_ref[:, cs] = _dot(p.astype(BF16), dob, 0, 0).astype(BF16)
            dp = _dot(dob, v[:, cs], 1, 1)
            ds = p * (dp - ee[:, hh * 64:hh * 64 + 1])
            dk_ref[:, cs] = (_dot(ds.astype(BF16), qq[:, cs], 0, 0) * 0.125).astype(BF16)

    kspec = pl.BlockSpec((A_QBLOCK, LANE), lambda r, n, hp: (n, r * cpr + base + HP + hp))
    vspec = pl.BlockSpec((A_QBLOCK, LANE), lambda r, n, hp: (n, r * cpr + base + 2 * HP + hp))
    qspecs = _span_specs(lambda r, hp: r * cpr + base + hp, 2 * nblk)
    wspecs = _span_specs(lambda r, hp: r * HP + hp, 2 * nblk)
    ospec = pl.BlockSpec((A_QBLOCK, LANE), lambda r, n, hp: (n, r * HP + hp))
    dk, dv = pl.pallas_call(
        body, name=name, grid=(dil, nblk, HP),
        in_specs=[pl.BlockSpec((None, 1, LANE), lambda r, n, hp: (hp, 0, 0)), kspec, vspec] + qspecs + wspecs * 3,
        out_specs=[ospec, ospec], out_shape=[jax.ShapeDtypeStruct((length, dil * A_WIDTH), BF16)] * 2,
        compiler_params=_cp("parallel", "parallel", "parallel"),
    )(_slope_lanes(), view, view, *([view] * 4), *([wide(do)] * 4), *([wide(lse)] * 4), *([wide(delta)] * 4))
    return dk.reshape(s, A_WIDTH), dv.reshape(s, A_WIDTH)


def _attn_layer_fwd(h, w_qkv, w_gate, w_out, li):
    nm = lambda t: f"a{li}_{t}"
    qkv = _mm(h, w_qkv, out_dtype=BF16, name=nm("qkv"))
    gate = _mm(h, w_gate, name=nm("gate"))
    os_, ls_ = [], []
    for g in range(3):
        o, l = _attn_fwd(qkv, g, nm(f"attn{g}"))
        os_.append(o)
        ls_.append(l)
    y, o, lse = _attn_merge(os_, ls_, gate, nm("merge"))
    out = _mm(y, w_out, name=nm("out"))
    return out, (qkv, gate, y, o, lse)


def _attn_layer_bwd(dy, h, saved, w_qkv, w_gate, w_out, li):
    nm = lambda t: f"a{li}_{t}"
    qkv, gate, y, o, lse = saved
    g_w_out = _mm(y, dy, ta=True, name=nm("dwout"))
    dyy = _mm(dy, w_out, tb=True, name=nm("dyy"))
    do, dgate, delta = _attn_gate_bwd(dyy, o, gate, nm("gatebwd"))
    parts = []
    for g in range(3):
        dq = _attn_dq(qkv, do, lse, delta, g, nm(f"dq{g}"))
        dk, dv = _attn_dkv(qkv, do, lse, delta, g, nm(f"dkv{g}"))
        parts += [dq, dk, dv]
    dqkv = jnp.concatenate(parts, axis=1)
    dh = _mm(dgate, w_gate, tb=True, name=nm("dh_gate"))
    dh = _mm(dqkv, w_qkv, tb=True, add=dh, name=nm("dh_qkv"))
    g_w_in = jnp.concatenate([_mm(h, dqkv, ta=True, name=nm("dwqkv")), _mm(h, dgate, ta=True, name=nm("dwgate"))], axis=1)
    return dh, g_w_in, g_w_out


SSM_INNER = SSM_HEADS * SSM_HEAD_DIM
SSM_BC = SSM_GROUPS * SSM_STATE
SSM_CONV_DIM = SSM_INNER + 2 * SSM_BC
GW = SSM_REP * SSM_HEAD_DIM
T = SSM_CHUNK
HALO = 8


def _conv_specs(tm, tn, s):
    nb8 = s // HALO
    cur = pl.BlockSpec((tm, tn), lambda j, i: (i, j))
    prev = pl.BlockSpec((HALO, tn), lambda j, i: (jnp.maximum(i * (tm // HALO) - 1, 0), j))
    nxt = pl.BlockSpec((HALO, tn), lambda j, i: (jnp.minimum((i + 1) * (tm // HALO), nb8 - 1), j))
    return prev, cur, nxt


def _extend(prev_ref, cur_ref, nxt_ref, i, nrow):
    p = jnp.where(i == 0, 0.0, prev_ref[...])
    n = jnp.where(i == nrow - 1, 0.0, nxt_ref[...])
    return jnp.concatenate([p, cur_ref[...], n], axis=0)


def _shift_rows(ext, off, tm):
    rows = ext.shape[0]
    return pltpu.roll(ext, (-off) % rows, 0)[HALO:HALO + tm]


def _conv_fwd(xraw, w, b, name):
    s, cdim = xraw.shape
    tm, tn = min(256, s), 1024
    nrow = s // tm

    def body(p_ref, c_ref, n_ref, w_ref, b_ref, pre_ref, act_ref):
        ext = _extend(p_ref, c_ref, n_ref, pl.program_id(1), nrow)
        acc = jnp.broadcast_to(b_ref[...], (tm, tn))
        for k in range(SSM_CONV):
            acc = acc + w_ref[k:k + 1, :] * _shift_rows(ext, k - SSM_CONV // 2, tm)
        pre_ref[...] = acc
        act_ref[...] = _silu(acc)

    prev, cur, nxt = _conv_specs(tm, tn, s)
    return pl.pallas_call(
        body, name=name, grid=(cdim // tn, nrow),
        in_specs=[prev, cur, nxt, pl.BlockSpec((SSM_CONV, tn), lambda j, i: (0, j)), pl.BlockSpec((1, tn), lambda j, i: (0, j))],
        out_specs=[cur, cur], out_shape=[jax.ShapeDtypeStruct((s, cdim), F32)] * 2,
        compiler_params=_cp("parallel", "parallel"),
    )(xraw, xraw, xraw, w, b)


def _conv_bwd(dact, pre, xraw, w, name):
    s, cdim = xraw.shape
    tm, tn = min(256, s), 1024
    nrow = s // tm

    def body(dp, dc, dn, pp, pc, pn, xp, xc, xn, w_ref, dx_ref, red_ref):
        i = pl.program_id(1)

        @pl.when(i == 0)
        def _():
            red_ref[...] = jnp.zeros_like(red_ref)

        dpre = _extend(dp, dc, dn, i, nrow) * _dsilu(_extend(pp, pc, pn, i, nrow))
        xext = _extend(xp, xc, xn, i, nrow)
        dcur = dpre[HALO:HALO + tm]
        acc = jnp.zeros((tm, tn), F32)
        for k in range(SSM_CONV):
            off = k - SSM_CONV // 2
            acc = acc + w_ref[k:k + 1, :] * _shift_rows(dpre, -off, tm)
            red_ref[k:k + 1, :] += jnp.sum(dcur * _shift_rows(xext, off, tm), axis=0, keepdims=True)
        red_ref[SSM_CONV:SSM_CONV + 1, :] += jnp.sum(dcur, axis=0, keepdims=True)
        dx_ref[...] = acc.astype(BF16)

    prev, cur, nxt = _conv_specs(tm, tn, s)
    return pl.pallas_call(
        body, name=name, grid=(cdim // tn, nrow),
        in_specs=[prev, cur, nxt] * 3 + [pl.BlockSpec((SSM_CONV, tn), lambda j, i: (0, j))],
        out_specs=[cur, pl.BlockSpec((8, tn), lambda j, i: (0, j))],
        out_shape=[jax.ShapeDtypeStruct((s, cdim), BF16), jax.ShapeDtypeStruct((8, cdim), F32)],
        compiler_params=_cp("parallel", "arbitrary"),
    )(dact, dact, dact, pre, pre, pre, xraw, xraw, xraw, w)


def _tri(lower):
    r = lax.broadcasted_iota(jnp.int32, (T, T), 0)
    c = lax.broadcasted_iota(jnp.int32, (T, T), 1)
    return (r >= c) if lower else (r <= c)


def _softplus(x):
    return jnp.maximum(x, 0.0) + jnp.log(1.0 + jnp.exp(-jnp.abs(x)))


def _dt_prep(dt_raw, bias, a_log, name):
    s = dt_raw.shape[0]
    nc = s // T

    def body(r_ref, b_ref, a_ref, dt_ref, cum_ref, cumt_ref):
        dt = _softplus(r_ref[...] + b_ref[...])
        da = dt * (-jnp.exp(a_ref[...]))
        pre = _dot_exact(_tri(True).astype(BF16), da)
        suf = _dot_exact(_tri(False).astype(BF16), da)
        lane = lax.broadcasted_iota(jnp.int32, (T, LANE), 1)
        cum = jnp.where(lane < SSM_HEADS, pre, suf)
        dt_ref[...] = dt
        cum_ref[...] = cum
        cumt_ref[...] = cum.T

    blk = pl.BlockSpec((T, LANE), lambda c: (c, 0))
    vec = pl.BlockSpec((1, LANE), lambda c: (0, 0))
    return pl.pallas_call(
        body, name=name, grid=(nc,), in_specs=[blk, vec, vec],
        out_specs=[blk, blk, pl.BlockSpec((None, LANE, T), lambda c: (c, 0, 0))],
        out_shape=[jax.ShapeDtypeStruct((s, LANE), F32), jax.ShapeDtypeStruct((s, LANE), F32), jax.ShapeDtypeStruct((nc, LANE, T), F32)],
        compiler_params=_cp("parallel"),
    )(dt_raw, bias, a_log)


def _by_group(t):
    s = t.shape[0]
    return t[:, :2 * SSM_HEADS].reshape(s, 2 * SSM_GROUPS, SSM_REP).transpose(1, 0, 2)


def _from_group(tf, tb):
    s = tf.shape[1]
    t = jnp.concatenate([tf, tb], axis=0).transpose(1, 0, 2).reshape(s, 2 * SSM_HEADS)
    return jnp.pad(t, ((0, 0), (0, LANE - 2 * SSM_HEADS)))


def _decay_mats(acol, arow, rev):
    after = _tri(not rev)
    return jnp.where(after, jnp.exp(jnp.where(after, acol - arow, 0.0)), 0.0)


def _ssd_scan(xbc, dtk, cumk, cumtk, rev, name):
    s = xbc.shape[0]
    nc = s // T
    last = 0 if rev else T - 1
    kofs = SSM_GROUPS if rev else 0
    ci = (lambda c: nc - 1 - c) if rev else (lambda c: c)

    def body(x_ref, b_ref, c_ref, dt_ref, cum_ref, cumt_ref, y_ref, st_ref, state):
        @pl.when(pl.program_id(1) == 0)
        def _():
            state[...] = jnp.zeros_like(state)

        bm = b_ref[...]
        cm = c_ref[...].astype(BF16)
        cb = _dot(cm, bm.astype(BF16), 1, 1)
        bt = bm.T.astype(BF16)
        for r in range(SSM_REP):
            cs = slice(r * SSM_HEAD_DIM, (r + 1) * SSM_HEAD_DIM)
            acol = cum_ref[:, r:r + 1]
            arow = cumt_ref[r:r + 1, :]
            lm = _decay_mats(acol, arow, rev)
            u = x_ref[:, cs] * dt_ref[:, r:r + 1]
            st = state[r]
            st_ref[r] = st
            yd = _dot((cb * lm).astype(BF16), u.astype(BF16))
            yo = jnp.exp(acol) * _dot(cm, st.astype(BF16))
            y_ref[:, cs] = yd + yo
            tot = cum_ref[last:last + 1, r:r + 1]
            dec = jnp.exp(tot - acol)
            state[r] = jnp.exp(tot) * st + _dot(bt, (dec * u).astype(BF16))

    nxb = SSM_INNER // LANE
    return pl.pallas_call(
        body, name=name, grid=(SSM_GROUPS, nc),
        in_specs=[
            pl.BlockSpec((T, GW), lambda g, c: (ci(c), g)),
            pl.BlockSpec((T, LANE), lambda g, c: (ci(c), nxb + g)),
            pl.BlockSpec((T, LANE), lambda g, c: (ci(c), nxb + SSM_GROUPS + g)),
            pl.BlockSpec((None, T, SSM_REP), lambda g, c: (kofs + g, ci(c), 0)),
            pl.BlockSpec((None, T, SSM_REP), lambda g, c: (kofs + g, ci(c), 0)),
            pl.BlockSpec((None, None, SSM_REP, T), lambda g, c: (kofs + g, ci(c), 0, 0)),
        ],
        out_specs=[
            pl.BlockSpec((T, GW), lambda g, c: (ci(c), g)),
            pl.BlockSpec((None, SSM_REP, SSM_STATE, SSM_HEAD_DIM), lambda g, c: (ci(c), g, 0, 0)),
        ],
        out_shape=[jax.ShapeDtypeStruct((s, SSM_INNER), F32), jax.ShapeDtypeStruct((nc, SSM_HEADS, SSM_STATE, SSM_HEAD_DIM), F32)],
        scratch_shapes=[pltpu.VMEM((SSM_REP, SSM_STATE, SSM_HEAD_DIM), F32)],
        compiler_params=_cp("parallel", "arbitrary"),
    )(xbc, xbc, xbc, dtk, cumk, cumtk)


def _ssd_scan_bwd(xbc, dtk, cumk, cumtk, dy, states, dvec, prev, rev, name):
    s = xbc.shape[0]
    nc = s // T
    last = 0 if rev else T - 1
    kofs = SSM_GROUPS if rev else 0
    ci = (lambda c: c) if rev else (lambda c: nc - 1 - c)
    has_prev = prev is not None

    def body(*refs):
        x_ref, b_ref, c_ref, dt_ref, cum_ref, cumt_ref, dy_ref, st_ref, dv_ref = refs[:9]
        refs = refs[9:]
        if has_prev:
            pdx, pdb, pdc = refs[:3]
            refs = refs[3:]
        dx_ref, db_ref, dc_ref, ddt_ref, dda_ref, dstate, rs_buf, in_buf, k_buf = refs

        @pl.when(pl.program_id(1) == 0)
        def _():
            dstate[...] = jnp.zeros_like(dstate)

        rs_buf[...] = jnp.zeros_like(rs_buf)
        in_buf[...] = jnp.zeros_like(in_buf)
        k_buf[...] = jnp.zeros_like(k_buf)
        bm = b_ref[...].astype(BF16)
        cm = c_ref[...].astype(BF16)
        cbt = _dot(bm, cm, 1, 1)
        cb = _dot(cm, bm, 1, 1)
        ct = c_ref[...].T.astype(BF16)
        after = _tri(not rev)
        before = _tri(rev)
        from_k = before.astype(BF16)
        ri = lax.broadcasted_iota(jnp.int32, (T, T), 0)
        cj = lax.broadcasted_iota(jnp.int32, (T, T), 1)
        strictly_before = (cj > ri) if rev else (cj < ri)
        dcb = jnp.zeros((T, T), F32)
        dc_acc = jnp.zeros((T, SSM_STATE), F32)
        db_acc = jnp.zeros((T, SSM_STATE), F32)
        for r in range(SSM_REP):
            cs = slice(r * SSM_HEAD_DIM, (r + 1) * SSM_HEAD_DIM)
            acol = cum_ref[:, r:r + 1]
            arow = cumt_ref[r:r + 1, :]
            lm = jnp.where(after, jnp.exp(jnp.where(after, acol - arow, 0.0)), 0.0)
            lmt = jnp.where(before, jnp.exp(jnp.where(before, arow - acol, 0.0)), 0.0)
            dtc = dt_ref[:, r:r + 1]
            xh = x_ref[:, cs]
            u = xh * dtc
            ub = u.astype(BF16)
            dyh = dy_ref[:, cs]
            dyb = dyh.astype(BF16)
            st = st_ref[r]
            dst = dstate[r]
            tot = cum_ref[last:last + 1, r:r + 1]
            dec = jnp.exp(tot - acol)
            eac = jnp.exp(acol)
            du_off = dec * _dot(bm, dst.astype(BF16))
            du = _dot((cbt * lmt).astype(BF16), dyb) + du_off
            gl = _dot(dyb, ub, 1, 1) * lm
            dcb = dcb + gl
            dc_acc = dc_acc + eac * _dot(dyb, st.astype(BF16), 1, 1)
            db_acc = db_acc + dec * _dot(ub, dst.astype(BF16), 1, 1)
            crossing = _dot(from_k, (gl * cb).astype(BF16))
            in_buf[:, r:r + 1] = jnp.sum(jnp.where(strictly_before, crossing, 0.0), axis=1, keepdims=True)
            y_off = eac * _dot(cm, st.astype(BF16))
            rs_buf[:, r:r + 1] = jnp.sum(dyh * y_off, axis=1, keepdims=True) - jnp.sum(u * du_off, axis=1, keepdims=True)
            kk = jnp.sum(dst * (jnp.exp(tot) * st), keepdims=True) + jnp.sum(u * du_off, keepdims=True)
            k_buf[0:1, r:r + 1] = kk
            ddt_ref[:, r:r + 1] = jnp.sum(du * xh, axis=1, keepdims=True)
            dx = du * dtc
            if has_prev:
                dx = dx + pdx[:, cs]
            else:
                dx = dx + dyh * dv_ref[:, cs]
            dx_ref[:, cs] = dx
            dstate[r] = jnp.exp(tot) * dst + _dot(ct, (eac * dyh).astype(BF16))
        dda = in_buf[...] + _dot_exact(from_k, rs_buf[...]) + k_buf[0:1, :]
        dda_ref[...] = dda[:, :SSM_REP]
        dcbb = dcb.astype(BF16)
        dc = dc_acc + _dot(dcbb, bm)
        db = db_acc + _dot(dcbb, cm, 0, 0)
        if has_prev:
            dc = dc + pdc[...]
            db = db + pdb[...]
        dc_ref[...] = dc
        db_ref[...] = db

    nxb = SSM_INNER // LANE
    xspec = pl.BlockSpec((T, GW), lambda g, c: (ci(c), g))
    gspec = pl.BlockSpec((T, LANE), lambda g, c: (ci(c), g))
    kspec = pl.BlockSpec((None, T, SSM_REP), lambda g, c: (kofs + g, ci(c), 0))
    in_specs = [
        xspec,
        pl.BlockSpec((T, LANE), lambda g, c: (ci(c), nxb + g)),
        pl.BlockSpec((T, LANE), lambda g, c: (ci(c), nxb + SSM_GROUPS + g)),
        kspec, kspec,
        pl.BlockSpec((None, None, SSM_REP, T), lambda g, c: (kofs + g, ci(c), 0, 0)),
        xspec,
        pl.BlockSpec((None, SSM_REP, SSM_STATE, SSM_HEAD_DIM), lambda g, c: (ci(c), g, 0, 0)),
        pl.BlockSpec((1, GW), lambda g, c: (0, g)),
    ]
    args = [xbc, xbc, xbc, dtk, cumk, cumtk, dy, states, dvec]
    if has_prev:
        in_specs += [xspec, gspec, gspec]
        args += list(prev)
    ospec8 = pl.BlockSpec((None, T, SSM_REP), lambda g, c: (g, ci(c), 0))
    return pl.pallas_call(
        body, name=name, grid=(SSM_GROUPS, nc), in_specs=in_specs,
        out_specs=[xspec, gspec, gspec, ospec8, ospec8],
        out_shape=[jax.ShapeDtypeStruct((s, SSM_INNER), F32), jax.ShapeDtypeStruct((s, SSM_BC), F32), jax.ShapeDtypeStruct((s, SSM_BC), F32),
                   jax.ShapeDtypeStruct((SSM_GROUPS, s, SSM_REP), F32), jax.ShapeDtypeStruct((SSM_GROUPS, s, SSM_REP), F32)],
        scratch_shapes=[pltpu.VMEM((SSM_REP, SSM_STATE, SSM_HEAD_DIM), F32), pltpu.VMEM((T, LANE), F32), pltpu.VMEM((T, LANE), F32),
                        pltpu.VMEM((8, LANE), F32)],
        compiler_params=_cp("parallel", "arbitrary"),
    )(*args)


def _ssd_post(yf, yb, xbc, z, dvec, nw, name):
    s = z.shape[0]
    tm = min(256, s)

    def body(yf_ref, yb_ref, x_ref, z_ref, dv_ref, nw_ref, o_ref):
        ys = yf_ref[...] + yb_ref[...] + dv_ref[...] * x_ref[...]
        yg = ys * _silu(z_ref[...])
        ms = jnp.mean(yg * yg, axis=1, keepdims=True)
        o_ref[...] = (yg * lax.rsqrt(ms + RMS_EPS) * nw_ref[...]).astype(BF16)

    rs = _row_spec(tm, SSM_INNER)
    vs = _vec_spec(SSM_INNER)
    return pl.pallas_call(
        body, name=name, grid=(s // tm,), in_specs=[rs, rs, rs, rs, vs, vs], out_specs=rs,
        out_shape=jax.ShapeDtypeStruct((s, SSM_INNER), BF16), compiler_params=_cp("parallel"),
    )(yf, yb, xbc, z, dvec, nw)


def _ssd_post_bwd(dyn, yf, yb, xbc, z, dvec, nw, name):
    s = z.shape[0]
    tm = min(256, s)

    def body(dyn_ref, yf_ref, yb_ref, x_ref, z_ref, dv_ref, nw_ref, dys_ref, dz_ref, red_ref):
        @pl.when(pl.program_id(0) == 0)
        def _():
            red_ref[...] = jnp.zeros_like(red_ref)

        xv, zv = x_ref[...], z_ref[...]
        ys = yf_ref[...] + yb_ref[...] + dv_ref[...] * xv
        sz = _silu(zv)
        yg = ys * sz
        rstd = lax.rsqrt(jnp.mean(yg * yg, axis=1, keepdims=True) + RMS_EPS)
        yhat = yg * rstd
        dynv = dyn_ref[...]
        dyh = dynv * nw_ref[...]
        dyg = rstd * (dyh - yhat * jnp.mean(dyh * yhat, axis=1, keepdims=True))
        dys = dyg * sz
        dys_ref[...] = dys
        dz_ref[...] = (dyg * ys * _dsilu(zv)).astype(BF16)
        red_ref[0:1, :] += jnp.sum(dynv * yhat, axis=0, keepdims=True)
        red_ref[1:2, :] += jnp.sum(dys * xv, axis=0, keepdims=True)

    rs = _row_spec(tm, SSM_INNER)
    vs = _vec_spec(SSM_INNER)
    return pl.pallas_call(
        body, name=name, grid=(s // tm,), in_specs=[rs, rs, rs, rs, rs, vs, vs],
        out_specs=[rs, rs, _vec_spec(SSM_INNER, 8)],
        out_shape=[jax.ShapeDtypeStruct((s, SSM_INNER), F32), jax.ShapeDtypeStruct((s, SSM_INNER), BF16), jax.ShapeDtypeStruct((8, SSM_INNER), F32)],
        compiler_params=_cp("arbitrary"),
    )(dyn, yf, yb, xbc, z, dvec, nw)


def _dt_bwd(dt_raw, bias, a_log, dt, ddt, dda, name):
    s = dt_raw.shape[0]
    tm = min(1024, s)

    def body(r_ref, b_ref, a_ref, dt_ref, ddt_ref, dda_ref, o_ref, red_ref):
        @pl.when(pl.program_id(0) == 0)
        def _():
            red_ref[...] = jnp.zeros_like(red_ref)

        a = -jnp.exp(a_ref[...])
        ddav = dda_ref[...]
        draw = (ddt_ref[...] + a * ddav) * _sigmoid(r_ref[...] + b_ref[...])
        o_ref[...] = draw.astype(BF16)
        red_ref[0:1, :] += jnp.sum(draw, axis=0, keepdims=True)
        red_ref[1:2, :] += a * jnp.sum(ddav * dt_ref[...], axis=0, keepdims=True)

    rs = _row_spec(tm, LANE)
    vs = _vec_spec(LANE)
    return pl.pallas_call(
        body, name=name, grid=(s // tm,), in_specs=[rs, vs, vs, rs, rs, rs], out_specs=[rs, _vec_spec(LANE, 8)],
        out_shape=[jax.ShapeDtypeStruct((s, LANE), BF16), jax.ShapeDtypeStruct((8, LANE), F32)],
        compiler_params=_cp("arbitrary"),
    )(dt_raw, bias, a_log, dt, ddt, dda)


def _pad_lanes(v):
    v = v.reshape(1, -1)
    return jnp.pad(v, ((0, 0), (0, LANE - v.shape[1])))


def _ssd_prep_weights(w_in, conv_w, conv_b, dt_bias, a_log, d_skip, norm_w, w_out):
    return dict(
        w_z=w_in[:, :SSM_INNER].astype(BF16),
        w_xbc=w_in[:, SSM_INNER:SSM_INNER + SSM_CONV_DIM].astype(BF16),
        w_dt=jnp.pad(w_in[:, SSM_INNER + SSM_CONV_DIM:], ((0, 0), (0, LANE - 2 * SSM_HEADS))).astype(BF16),
        conv_w=conv_w, conv_b=conv_b.reshape(1, -1), bias=_pad_lanes(dt_bias), a_log=_pad_lanes(a_log),
        dvec=jnp.repeat(d_skip, SSM_HEAD_DIM).reshape(1, -1), nw=norm_w.reshape(1, -1), w_out=w_out.astype(BF16),
    )


def _ssd_layer_fwd(h, w, li):
    nm = lambda t: f"b{li}_{t}"
    z = _mm(h, w["w_z"], name=nm("z"))
    xraw = _mm(h, w["w_xbc"], name=nm("xbc"))
    dt_raw = _mm(h, w["w_dt"], name=nm("dt"))
    pre, xbc = _conv_fwd(xraw, w["conv_w"], w["conv_b"], nm("conv"))
    dt, cum, cumt = _dt_prep(dt_raw, w["bias"], w["a_log"], nm("dtprep"))
    nc = cumt.shape[0]
    dtk, cumk = _by_group(dt), _by_group(cum)
    cumtk = cumt[:, :2 * SSM_HEADS].reshape(nc, 2 * SSM_GROUPS, SSM_REP, T).transpose(1, 0, 2, 3)
    yf, stf = _ssd_scan(xbc, dtk, cumk, cumtk, False, nm("scan_f"))
    yb, stb = _ssd_scan(xbc, dtk, cumk, cumtk, True, nm("scan_b"))
    yn = _ssd_post(yf, yb, xbc, z, w["dvec"], w["nw"], nm("post"))
    out = _mm(yn, w["w_out"], name=nm("out"))
    return out, (z, xraw, dt_raw, pre, xbc, dt, dtk, cumk, cumtk, yf, stf, yb, stb, yn)


def _ssd_layer_bwd(dy, h, saved, w, li):
    nm = lambda t: f"b{li}_{t}"
    z, xraw, dt_raw, pre, xbc, dt, dtk, cumk, cumtk, yf, stf, yb, stb, yn = saved
    g_w_out = _mm(yn, dy, ta=True, name=nm("dwout"))
    dyn = _mm(dy, w["w_out"], tb=True, name=nm("dyn"))
    dys, dz, pred = _ssd_post_bwd(dyn, yf, yb, xbc, z, w["dvec"], w["nw"], nm("postbwd"))
    dx1, db1, dc1, ddt_f, dda_f = _ssd_scan_bwd(xbc, dtk, cumk, cumtk, dys, stf, w["dvec"], None, False, nm("scanbwd_f"))
    dx, db, dc, ddt_b, dda_b = _ssd_scan_bwd(xbc, dtk, cumk, cumtk, dys, stb, w["dvec"], (dx1, db1, dc1), True, nm("scanbwd_b"))
    dact = jnp.concatenate([dx, db, dc], axis=1)
    dxraw, cred = _conv_bwd(dact, pre, xraw, w["conv_w"], nm("convbwd"))
    draw, dred = _dt_bwd(dt_raw, w["bias"], w["a_log"], dt, _from_group(ddt_f, ddt_b), _from_group(dda_f, dda_b), nm("dtbwd"))
    dh = _mm(dz, w["w_z"], tb=True, name=nm("dh_z"))
    dh = _mm(dxraw, w["w_xbc"], tb=True, add=dh, name=nm("dh_xbc"))
    dh = _mm(draw, w["w_dt"], tb=True, add=dh, name=nm("dh_dt"))
    g_w_in = jnp.concatenate([_mm(h, dz, ta=True, name=nm("dwz")), _mm(h, dxraw, ta=True, name=nm("dwxbc")),
                              _mm(h, draw, ta=True, name=nm("dwdt"))[:, :2 * SSM_HEADS]], axis=1)
    grads = (g_w_in, cred[:SSM_CONV], cred[SSM_CONV], dred[0, :2 * SSM_HEADS].reshape(2, SSM_HEADS),
             dred[1, :2 * SSM_HEADS].reshape(2, SSM_HEADS), pred[1].reshape(SSM_HEADS, SSM_HEAD_DIM).sum(axis=1), pred[0], g_w_out)
    return dh, grads


B_GRAD_NAMES = ("b_w_in", "b_conv_w", "b_conv_b", "b_dt_bias", "b_a_log", "b_d", "b_norm_w", "b_w_out")


def _local_step(x, tgt, mod, w):
    d = x.shape[1]
    qkv_cols = QKV_COLS
    layers = []
    for i in range(DEPTH):
        j = i // 2
        if i % 2 == 0:
            layers.append((w["a_w_in"][j][:, :qkv_cols].astype(BF16), w["a_w_in"][j][:, qkv_cols:].astype(BF16), w["a_w_out"][j].astype(BF16)))
        else:
            layers.append(_ssd_prep_weights(w["b_w_in"][j], w["b_conv_w"][j], w["b_conv_b"][j], w["b_dt_bias"][j], w["b_a_log"][j],
                                            w["b_d"][j], w["b_norm_w"][j], w["b_w_out"][j]))
    saved = []
    for i in range(DEPTH):
        shift, scale, gate = mod[i:i + 1, :d], mod[i:i + 1, d:2 * d], mod[i:i + 1, 2 * d:]
        h = _modulate(x, scale, shift, f"l{i}_mod")
        if i % 2 == 0:
            out, sv = _attn_layer_fwd(h, *layers[i], i)
        else:
            out, sv = _ssd_layer_fwd(h, layers[i], i)
        xn = _resid_ln(x, out, gate, w["ln_g"][i:i + 1], w["ln_b"][i:i + 1], f"l{i}_ln")
        saved.append((x, h, out, sv))
        x = xn
    dx, lred = _loss_grad(x, tgt, "loss")
    loss = 0.5 * jnp.sum(lred[0]) / d
    dmod, g_ln_g, g_ln_b = [None] * DEPTH, [None] * DEPTH, [None] * DEPTH
    ga_in, ga_out = [None, None], [None, None]
    gb = [None, None]
    for i in reversed(range(DEPTH)):
        j = i // 2
        xi, h, out, sv = saved[i]
        scale, gate = mod[i:i + 1, d:2 * d], mod[i:i + 1, 2 * d:]
        du, dy, red = _resid_ln_bwd(xi, out, dx, gate, w["ln_g"][i:i + 1], f"l{i}_lnbwd")
        g_ln_g[i], g_ln_b[i] = red[1], red[2]
        if i % 2 == 0:
            dh, ga_in[j], ga_out[j] = _attn_layer_bwd(dy, h, sv, *layers[i], i)
        else:
            dh, gb[j] = _ssd_layer_bwd(dy, h, sv, layers[i], i)
        dx, red2 = _modulate_bwd(du, dh, xi, scale, f"l{i}_modbwd")
        dmod[i] = jnp.concatenate([red2[1], red2[0], red[0]])
    grads = {"ln_g": jnp.stack(g_ln_g), "ln_b": jnp.stack(g_ln_b), "a_w_in": jnp.stack(ga_in), "a_w_out": jnp.stack(ga_out)}
    for k, n in enumerate(B_GRAD_NAMES):
        grads[n] = jnp.stack([gb[0][k], gb[1][k]])
    return loss, dx, jnp.stack(dmod), grads


MESH = pl.DeviceIdType.MESH
ANY = pl.BlockSpec(memory_space=pl.ANY)
N_DEV = 8
N_SHARD = 4


def _flip(v, bit):
    return 1 - v if bit else v


def _all_gather8(v, name):
    def body(v_ref, o_ref, send_sems, recv_sems, local_sem):
        x, y, c = lax.axis_index("x"), lax.axis_index("y"), lax.axis_index("c")
        me = 4 * x + 2 * y + c
        local = pltpu.make_async_copy(v_ref, o_ref.at[me], local_sem)
        local.start()
        copies = []
        for k in range(1, N_DEV):
            peer = (_flip(x, k & 4), _flip(y, k & 2), _flip(c, k & 1))
            copies.append(pltpu.make_async_remote_copy(
                src_ref=v_ref, dst_ref=o_ref.at[me], send_sem=send_sems.at[k - 1], recv_sem=recv_sems.at[k - 1],
                device_id=peer, device_id_type=MESH))
        for cp in copies:
            cp.start()
        for cp in copies:
            cp.wait()
        local.wait()

    return pl.pallas_call(
        body, name=name, in_specs=[ANY], out_specs=ANY, out_shape=jax.ShapeDtypeStruct((N_DEV,) + v.shape, v.dtype),
        scratch_shapes=[pltpu.SemaphoreType.DMA((N_DEV - 1,)), pltpu.SemaphoreType.DMA((N_DEV - 1,)), pltpu.SemaphoreType.DMA],
    )(v)


def _exchange4(src, same, name):
    shape = src.shape if same else src.shape[1:]

    def body(s_ref, o_ref, send_sems, recv_sems, local_sem):
        x, y, c = lax.axis_index("x"), lax.axis_index("y"), lax.axis_index("c")
        m = 2 * x + y
        pick = (lambda j: s_ref) if same else (lambda j: s_ref.at[j])
        local = pltpu.make_async_copy(pick(m), o_ref.at[m], local_sem)
        local.start()
        copies = []
        for k in range(1, N_SHARD):
            px, py = _flip(x, k & 2), _flip(y, k & 1)
            copies.append(pltpu.make_async_remote_copy(
                src_ref=pick(2 * px + py), dst_ref=o_ref.at[m], send_sem=send_sems.at[k - 1], recv_sem=recv_sems.at[k - 1],
                device_id=(px, py, c), device_id_type=MESH))
        for cp in copies:
            cp.start()
        for cp in copies:
            cp.wait()
        local.wait()

    return pl.pallas_call(
        body, name=name, in_specs=[ANY], out_specs=ANY, out_shape=jax.ShapeDtypeStruct((N_SHARD,) + shape, src.dtype),
        scratch_shapes=[pltpu.SemaphoreType.DMA((N_SHARD - 1,)), pltpu.SemaphoreType.DMA((N_SHARD - 1,)), pltpu.SemaphoreType.DMA],
    )(src)


def _swap_sibling(v, name):
    def body(v_ref, o_ref, send_sem, recv_sem):
        x, y, c = lax.axis_index("x"), lax.axis_index("y"), lax.axis_index("c")
        cp = pltpu.make_async_remote_copy(src_ref=v_ref, dst_ref=o_ref, send_sem=send_sem, recv_sem=recv_sem,
                                          device_id=(x, y, 1 - c), device_id_type=MESH)
        cp.start()
        cp.wait()

    return pl.pallas_call(
        body, name=name, in_specs=[ANY], out_specs=ANY, out_shape=jax.ShapeDtypeStruct(v.shape, v.dtype),
        scratch_shapes=[pltpu.SemaphoreType.DMA, pltpu.SemaphoreType.DMA],
    )(v)


def _sum_slots(a, name):
    n, r, cdim = a.shape
    tm = 8 if r % 256 else 256

    def body(a_ref, o_ref):
        acc = a_ref[0]
        for k in range(1, n):
            acc = acc + a_ref[k]
        o_ref[...] = acc

    return pl.pallas_call(
        body, name=name, grid=(r // tm,), in_specs=[pl.BlockSpec((n, tm, cdim), lambda i: (0, i, 0))],
        out_specs=pl.BlockSpec((tm, cdim), lambda i: (i, 0)), out_shape=jax.ShapeDtypeStruct((r, cdim), F32),
        compiler_params=_cp("parallel"),
    )(a)


def _silu_rows(v, name):
    def body(v_ref, o_ref):
        o_ref[...] = _silu(v_ref[...])

    return pl.pallas_call(body, name=name, out_shape=jax.ShapeDtypeStruct(v.shape, F32))(v)


PACK_COLS = 1024
PACK_TILE = 256


def _adamw(w, g1, g2, m, v, name):
    r = w.shape[0]
    c1 = 1.0 / (1.0 - ADAM_B1 ** ADAM_STEP)
    c2 = 1.0 / (1.0 - ADAM_B2 ** ADAM_STEP)

    def body(w_ref, g1_ref, g2_ref, m_ref, v_ref, g_ref, d_ref, nm_ref, nv_ref):
        g = g1_ref[...] + g2_ref[...]
        mn = ADAM_B1 * m_ref[...] + (1.0 - ADAM_B1) * g
        vn = ADAM_B2 * v_ref[...] + (1.0 - ADAM_B2) * (g * g)
        g_ref[...] = g
        nm_ref[...] = mn
        nv_ref[...] = vn
        d_ref[...] = -ADAM_LR * ((mn * c1) / (jnp.sqrt(vn * c2) + ADAM_EPS) + ADAM_WD * w_ref[...])

    spec = pl.BlockSpec((PACK_TILE, PACK_COLS), lambda i: (i, 0))
    return pl.pallas_call(
        body, name=name, grid=(r // PACK_TILE,), in_specs=[spec] * 5, out_specs=[spec] * 4,
        out_shape=[jax.ShapeDtypeStruct(w.shape, F32)] * 4, compiler_params=_cp("parallel"),
    )(w, g1, g2, m, v)


def _rows(a):
    f = a.reshape(-1)
    pad = (-f.shape[0]) % PACK_COLS
    if pad:
        f = jnp.pad(f, (0, pad))
    return f.reshape(-1, PACK_COLS)


def _nrows(shape):
    return -(-int(np.prod(shape)) // PACK_COLS)


def _pack(parts, total_rows=None):
    p = jnp.concatenate([_rows(a) for a in parts], axis=0)
    if total_rows is not None and total_rows > p.shape[0]:
        p = jnp.pad(p, ((0, total_rows - p.shape[0]), (0, 0)))
    return p


def _unpack(p, shapes):
    out, r0 = [], 0
    for shp in shapes:
        n = int(np.prod(shp))
        nr = _nrows(shp)
        out.append(p[r0:r0 + nr].reshape(-1)[:n].reshape(shp))
        r0 += nr
    return out


def _unshard_cols(g):
    return jnp.concatenate([g[k] for k in range(N_SHARD)], axis=-1)


def _shard_cols(a):
    n = a.shape[-1] // N_SHARD
    return jnp.stack([a[..., k * n:(k + 1) * n] for k in range(N_SHARD)])


def _unshard_rows(g):
    return jnp.concatenate([g[k] for k in range(N_SHARD)], axis=1)


def _shard_rows(a):
    n = a.shape[1] // N_SHARD
    return jnp.stack([a[:, k * n:(k + 1) * n] for k in range(N_SHARD)])


W_NAMES = ("ada_w", "ada_b", "ln_g", "ln_b", "a_w_in", "a_w_out", "b_w_in", "b_conv_w", "b_conv_b", "b_dt_bias", "b_a_log", "b_d",
           "b_norm_w", "b_w_out")
BIG = ("a_w_in", "a_w_out", "b_w_in", "b_w_out")
PACK_ORDER = BIG + ("ada_w", "ada_b", "ln_g", "ln_b", "b_conv_w", "b_conv_b", "b_dt_bias", "b_a_log", "b_d", "b_norm_w")


def kernel(x, c, ada_w, ada_b, ln_g, ln_b, a_w_in, a_w_out, b_w_in, b_conv_w, b_conv_b, b_dt_bias, b_a_log, b_d, b_norm_w, b_w_out, loss_target, m_ada_w, m_ada_b, m_ln_g, m_ln_b, m_a_w_in, m_a_w_out, m_b_w_in, m_b_conv_w, m_b_conv_b, m_b_dt_bias, m_b_a_log, m_b_d, m_b_norm_w, m_b_w_out, v_ada_w, v_ada_b, v_ln_g, v_ln_b, v_a_w_in, v_a_w_out, v_b_w_in, v_b_conv_w, v_b_conv_b, v_b_dt_bias, v_b_a_log, v_b_d, v_b_norm_w, v_b_w_out):
    w = dict(ada_w=ada_w, ada_b=ada_b, ln_g=ln_g, ln_b=ln_b, a_w_in=a_w_in, a_w_out=a_w_out, b_w_in=b_w_in, b_conv_w=b_conv_w,
             b_conv_b=b_conv_b, b_dt_bias=b_dt_bias, b_a_log=b_a_log, b_d=b_d, b_norm_w=b_norm_w, b_w_out=b_w_out)
    mom = dict(ada_w=m_ada_w, ada_b=m_ada_b, ln_g=m_ln_g, ln_b=m_ln_b, a_w_in=m_a_w_in, a_w_out=m_a_w_out, b_w_in=m_b_w_in,
               b_conv_w=m_b_conv_w, b_conv_b=m_b_conv_b, b_dt_bias=m_b_dt_bias, b_a_log=m_b_a_log, b_d=m_b_d, b_norm_w=m_b_norm_w,
               b_w_out=m_b_w_out)
    var = dict(ada_w=v_ada_w, ada_b=v_ada_b, ln_g=v_ln_g, ln_b=v_ln_b, a_w_in=v_a_w_in, a_w_out=v_a_w_out, b_w_in=v_b_w_in,
               b_conv_w=v_b_conv_w, b_conv_b=v_b_conv_b, b_dt_bias=v_b_dt_bias, b_a_log=v_b_a_log, b_d=v_b_d, b_norm_w=v_b_norm_w,
               b_w_out=v_b_w_out)
    ax, ay, ac = lax.axis_index("x"), lax.axis_index("y"), lax.axis_index("c")
    me = 4 * ax + 2 * ay + ac
    shard = 2 * ax + ay
    d = x.shape[-1]
    dsh = ada_w.shape[-1]

    small_in = (c, b_conv_w, b_conv_b, b_norm_w)
    g0 = _all_gather8(_pack(small_in).reshape(-1, LANE), "gather_small_in").reshape(N_DEV, -1, PACK_COLS)
    per_dev = [_unpack(g0[k], [a.shape for a in small_in]) for k in range(N_DEV)]
    c_all = jnp.concatenate([p[0] for p in per_dev], axis=0)
    conv_w_full, conv_b_full, norm_w_full = (_unshard_cols([per_dev[2 * k][t] for k in range(N_SHARD)]) for t in (1, 2, 3))

    cond = _silu_rows(jnp.pad(c_all, ((0, 8), (0, 0))), "cond")
    bias = lax.dynamic_slice_in_dim(ada_b, shard * dsh, dsh, axis=1)
    part = jnp.stack([_mm(cond, ada_w[i], add=jnp.broadcast_to(bias[i], (16, dsh)), name=f"mod{i}")[:N_DEV] for i in range(DEPTH)])
    g1 = _all_gather8(part.reshape(-1, LANE), "gather_mod").reshape(N_DEV, DEPTH, N_DEV, dsh)
    mod_all = _unshard_cols([g1[2 * k] for k in range(N_SHARD)])
    mod = lax.dynamic_index_in_dim(mod_all, me, axis=1, keepdims=False)

    gw = _exchange4(_pack([w[n] for n in BIG]).astype(BF16), True, "gather_weights")
    big_sh = [_unpack(gw[k], [w[n].shape for n in BIG]) for k in range(N_SHARD)]
    full = dict(
        ln_g=ln_g, ln_b=ln_b, b_dt_bias=b_dt_bias, b_a_log=b_a_log, b_d=b_d,
        b_conv_w=conv_w_full, b_conv_b=conv_b_full, b_norm_w=norm_w_full,
        a_w_in=_unshard_cols([s[0] for s in big_sh]), a_w_out=_unshard_rows([s[1] for s in big_sh]),
        b_w_in=_unshard_cols([s[2] for s in big_sh]), b_w_out=_unshard_rows([s[3] for s in big_sh]),
    )

    loss, grad_x, dmod, g = _local_step(x[0], loss_target[0], mod, full)

    gsh = (_shard_cols(g["a_w_in"]), _shard_rows(g["a_w_out"]), _shard_cols(g["b_w_in"]), _shard_rows(g["b_w_out"]))
    to_send = jnp.stack([_pack([t[k] for t in gsh]) for k in range(N_SHARD)])
    mine = _sum_slots(_exchange4(to_send, False, "scatter_grads"), "sum_shards")
    theirs = _swap_sibling(mine, "swap_grads")

    small_g = (dmod, g["ln_g"], g["ln_b"], g["b_dt_bias"], g["b_a_log"], g["b_d"], g["b_conv_w"], g["b_conv_b"], g["b_norm_w"],
               loss.reshape(1))
    g2 = _all_gather8(_pack(small_g).reshape(-1, LANE), "gather_small_grads")
    tot = _unpack(_sum_slots(g2, "sum_small").reshape(-1, PACK_COLS), [a.shape for a in small_g])
    g_ada_b, g_ln_g, g_ln_b, g_dt_bias, g_a_log, g_d, g_conv_w, g_conv_b, g_norm_w, loss_sum = tot
    dmod_all = g2.reshape(N_DEV, -1)[:, :dmod.size].reshape(N_DEV, DEPTH, 3 * d)
    dmod_mine = lax.dynamic_slice_in_dim(dmod_all, shard * dsh, dsh, axis=2)
    g_ada_w = jnp.stack([_mm(cond, jnp.pad(dmod_mine[:, i], ((0, 8), (0, 0))), ta=True, name=f"dada{i}") for i in range(DEPTH)])
    csh = g_conv_w.shape[-1] // N_SHARD
    nsh = g_norm_w.shape[-1] // N_SHARD
    small_grads = dict(
        ada_w=g_ada_w, ada_b=g_ada_b, ln_g=g_ln_g, ln_b=g_ln_b, b_dt_bias=g_dt_bias, b_a_log=g_a_log, b_d=g_d,
        b_conv_w=lax.dynamic_slice_in_dim(g_conv_w, shard * csh, csh, axis=2),
        b_conv_b=lax.dynamic_slice_in_dim(g_conv_b, shard * csh, csh, axis=1),
        b_norm_w=lax.dynamic_slice_in_dim(g_norm_w, shard * nsh, nsh, axis=1),
    )

    rest = PACK_ORDER[len(BIG):]
    rows = sum(_nrows(w[n].shape) for n in PACK_ORDER)
    rows = -(-rows // PACK_TILE) * PACK_TILE
    pw, pm, pv = (_pack([t[n] for n in PACK_ORDER], rows) for t in (w, mom, var))
    pg1 = _pack([mine] + [small_grads[n] for n in rest], rows)
    pg2 = _pack([theirs], rows)
    outs = [_unpack(p, [w[n].shape for n in PACK_ORDER]) for p in _adamw(pw, pg1, pg2, pm, pv, "adamw")]
    by_name = [dict(zip(PACK_ORDER, o)) for o in outs]
    return (loss_sum.reshape(()), grad_x[None], *[t[n] for t in by_name for n in W_NAMES])
```

```python
import jax
import jax.numpy as jnp
import numpy as np
from jax import lax
from jax.experimental import pallas as pl
from jax.experimental.pallas import tpu as pltpu

F32 = jnp.float32
BF16 = jnp.bfloat16

DEPTH = 4
A_HEADS = 16
A_HEAD_DIM = 64
A_WIDTH = A_HEADS * A_HEAD_DIM
DILATIONS = (1, 4, 16)
A_RADIUS = 64
A_QBLOCK = 128
SSM_HEADS = 32
SSM_HEAD_DIM = 64
SSM_STATE = 128
SSM_GROUPS = 4
SSM_REP = SSM_HEADS // SSM_GROUPS
SSM_CONV = 5
SSM_CHUNK = 128
DEEPNORM_ALPHA = (2 * DEPTH) ** 0.25
LN_EPS = 1e-5
RMS_EPS = 1e-5
ADAM_LR, ADAM_B1, ADAM_B2, ADAM_EPS, ADAM_WD, ADAM_STEP = 0.001, 0.9, 0.999, 1e-08, 0.01, 10
VMEM_LIMIT = 56 * 1024 * 1024
LANE = 128


def _cp(*sem):
    return pltpu.CompilerParams(dimension_semantics=sem, vmem_limit_bytes=VMEM_LIMIT)


def _tile(dim, target):
    if dim <= target:
        return dim
    t = (target // LANE) * LANE
    while dim % t:
        t -= LANE
    return t


def _sigmoid(x):
    return 1.0 / (1.0 + jnp.exp(-x))


def _silu(x):
    return x * _sigmoid(x)


def _dsilu(x):
    s = _sigmoid(x)
    return s * (1.0 + x * (1.0 - s))


def _split3(x):
    a = x.astype(BF16)
    r = x - a.astype(F32)
    b = r.astype(BF16)
    c = (r - b.astype(F32)).astype(BF16)
    return a, b, c


def _dot(a, b, ca=1, cb=0):
    return lax.dot_general(a, b, (((ca,), (cb,)), ((), ())), preferred_element_type=F32)


def _dot_exact(m01, x):
    a, b, c = _split3(x)
    return _dot(m01, a) + _dot(m01, b) + _dot(m01, c)


def _mm(a, b, *, ta=False, tb=False, add=None, out_dtype=F32, name, tm=1024, tn=1024, tk=512):
    m, k = (a.shape[1], a.shape[0]) if ta else a.shape
    n = b.shape[0] if tb else b.shape[1]
    assert (b.shape[1] if tb else b.shape[0]) == k
    tm, tn, tk = _tile(m, tm), _tile(n, tn), _tile(k, tk)
    nk = k // tk
    has_add = add is not None

    def body(*refs):
        if has_add:
            a_ref, b_ref, c_ref, o_ref, acc = refs
        else:
            a_ref, b_ref, o_ref, acc = refs
        kk = pl.program_id(2)

        @pl.when(kk == 0)
        def _():
            acc[...] = jnp.zeros_like(acc)

        av = a_ref[...].astype(BF16)
        bv = b_ref[...].astype(BF16)
        acc[...] += _dot(av, bv, 0 if ta else 1, 1 if tb else 0)

        @pl.when(kk == nk - 1)
        def _():
            r = acc[...]
            if has_add:
                r = r + c_ref[...]
            o_ref[...] = r.astype(o_ref.dtype)

    a_spec = pl.BlockSpec((tk, tm), lambda i, j, kk: (kk, i)) if ta else pl.BlockSpec((tm, tk), lambda i, j, kk: (i, kk))
    b_spec = pl.BlockSpec((tn, tk), lambda i, j, kk: (j, kk)) if tb else pl.BlockSpec((tk, tn), lambda i, j, kk: (kk, j))
    in_specs = [a_spec, b_spec]
    args = [a, b]
    if has_add:
        in_specs.append(pl.BlockSpec((tm, tn), lambda i, j, kk: (i, j)))
        args.append(add)
    return pl.pallas_call(
        body, name=name, grid=(m // tm, n // tn, nk), in_specs=in_specs,
        out_specs=pl.BlockSpec((tm, tn), lambda i, j, kk: (i, j)),
        out_shape=jax.ShapeDtypeStruct((m, n), out_dtype),
        scratch_shapes=[pltpu.VMEM((tm, tn), F32)],
        compiler_params=_cp("parallel", "parallel", "arbitrary"),
    )(*args)


ROWS = 512


def _row_spec(tm, d):
    return pl.BlockSpec((tm, d), lambda i: (i, 0))


def _vec_spec(d, rows=1):
    return pl.BlockSpec((rows, d), lambda i: (0, 0))


def _modulate(x, scale, shift, name):
    s, d = x.shape
    tm = min(ROWS, s)

    def body(x_ref, sc_ref, sh_ref, o_ref):
        o_ref[...] = (x_ref[...] * (1.0 + sc_ref[...]) + sh_ref[...]).astype(BF16)

    return pl.pallas_call(
        body, name=name, grid=(s // tm,), in_specs=[_row_spec(tm, d), _vec_spec(d), _vec_spec(d)],
        out_specs=_row_spec(tm, d), out_shape=jax.ShapeDtypeStruct((s, d), BF16), compiler_params=_cp("parallel"),
    )(x, scale, shift)


def _resid_ln(x, y, gate, g, b, name):
    s, d = x.shape
    tm = min(ROWS, s)

    def body(x_ref, y_ref, gt_ref, g_ref, b_ref, o_ref):
        u = DEEPNORM_ALPHA * x_ref[...] + gt_ref[...] * y_ref[...]
        mu = jnp.mean(u, axis=1, keepdims=True)
        uc = u - mu
        var = jnp.mean(uc * uc, axis=1, keepdims=True)
        o_ref[...] = uc * lax.rsqrt(var + LN_EPS) * g_ref[...] + b_ref[...]

    return pl.pallas_call(
        body, name=name, grid=(s // tm,),
        in_specs=[_row_spec(tm, d), _row_spec(tm, d), _vec_spec(d), _vec_spec(d), _vec_spec(d)],
        out_specs=_row_spec(tm, d), out_shape=jax.ShapeDtypeStruct((s, d), F32), compiler_params=_cp("parallel"),
    )(x, y, gate, g, b)


def _resid_ln_bwd(x, y, dxn, gate, g, name):
    s, d = x.shape
    tm = min(ROWS, s)

    def body(x_ref, y_ref, dxn_ref, gt_ref, g_ref, du_ref, dy_ref, red_ref):
        @pl.when(pl.program_id(0) == 0)
        def _():
            red_ref[...] = jnp.zeros_like(red_ref)

        yv = y_ref[...]
        u = DEEPNORM_ALPHA * x_ref[...] + gt_ref[...] * yv
        mu = jnp.mean(u, axis=1, keepdims=True)
        uc = u - mu
        var = jnp.mean(uc * uc, axis=1, keepdims=True)
        rstd = lax.rsqrt(var + LN_EPS)
        xhat = uc * rstd
        dxnv = dxn_ref[...]
        dxh = dxnv * g_ref[...]
        du = rstd * (dxh - jnp.mean(dxh, axis=1, keepdims=True) - xhat * jnp.mean(dxh * xhat, axis=1, keepdims=True))
        du_ref[...] = du
        dy_ref[...] = (du * gt_ref[...]).astype(BF16)
        red_ref[0:1, :] += jnp.sum(du * yv, axis=0, keepdims=True)
        red_ref[1:2, :] += jnp.sum(dxnv * xhat, axis=0, keepdims=True)
        red_ref[2:3, :] += jnp.sum(dxnv, axis=0, keepdims=True)

    return pl.pallas_call(
        body, name=name, grid=(s // tm,),
        in_specs=[_row_spec(tm, d), _row_spec(tm, d), _row_spec(tm, d), _vec_spec(d), _vec_spec(d)],
        out_specs=[_row_spec(tm, d), _row_spec(tm, d), _vec_spec(d, 8)],
        out_shape=[jax.ShapeDtypeStruct((s, d), F32), jax.ShapeDtypeStruct((s, d), BF16), jax.ShapeDtypeStruct((8, d), F32)],
        compiler_params=_cp("arbitrary"),
    )(x, y, dxn, gate, g)


def _modulate_bwd(du, dh, x, scale, name):
    s, d = x.shape
    tm = min(ROWS, s)

    def body(du_ref, dh_ref, x_ref, sc_ref, dx_ref, red_ref):
        @pl.when(pl.program_id(0) == 0)
        def _():
            red_ref[...] = jnp.zeros_like(red_ref)

        dhv = dh_ref[...]
        dx_ref[...] = DEEPNORM_ALPHA * du_ref[...] + dhv * (1.0 + sc_ref[...])
        red_ref[0:1, :] += jnp.sum(dhv * x_ref[...], axis=0, keepdims=True)
        red_ref[1:2, :] += jnp.sum(dhv, axis=0, keepdims=True)

    return pl.pallas_call(
        body, name=name, grid=(s // tm,),
        in_specs=[_row_spec(tm, d), _row_spec(tm, d), _row_spec(tm, d), _vec_spec(d)],
        out_specs=[_row_spec(tm, d), _vec_spec(d, 8)],
        out_shape=[jax.ShapeDtypeStruct((s, d), F32), jax.ShapeDtypeStruct((8, d), F32)],
        compiler_params=_cp("arbitrary"),
    )(du, dh, x, scale)


def _loss_grad(xf, tgt, name):
    s, d = xf.shape
    tm = min(ROWS, s)

    def body(x_ref, t_ref, dx_ref, red_ref):
        @pl.when(pl.program_id(0) == 0)
        def _():
            red_ref[...] = jnp.zeros_like(red_ref)

        e = x_ref[...] - t_ref[...]
        dx_ref[...] = e * (1.0 / d)
        red_ref[0:1, :] += jnp.sum(e * e, axis=0, keepdims=True)

    return pl.pallas_call(
        body, name=name, grid=(s // tm,), in_specs=[_row_spec(tm, d), _row_spec(tm, d)],
        out_specs=[_row_spec(tm, d), _vec_spec(d, 8)],
        out_shape=[jax.ShapeDtypeStruct((s, d), F32), jax.ShapeDtypeStruct((8, d), F32)],
        compiler_params=_cp("arbitrary"),
    )(xf, tgt)


QKV_COLS = 3 * 3 * A_WIDTH


SLOPES = tuple(float(2.0 ** (-8.0 * (h + 1.0) / A_HEADS)) for h in range(A_HEADS))
FAR = 1e30
HEAD_COLS = tuple(slice(h * A_HEAD_DIM, (h + 1) * A_HEAD_DIM) for h in range(A_HEADS))


def _band_dist(n, length, dil, span_rows):
    shape = (2 * A_QBLOCK, A_QBLOCK) if span_rows else (A_QBLOCK, 2 * A_QBLOCK)
    r = lax.broadcasted_iota(jnp.int32, shape, 0)
    c = lax.broadcasted_iota(jnp.int32, shape, 1)
    sp, ce = (r, c) if span_rows else (c, r)
    delta = sp - A_RADIUS - ce
    pos = n * A_QBLOCK - A_RADIUS + sp
    valid = (jnp.abs(delta) <= A_RADIUS) & (pos >= 0) & (pos < length)
    return jnp.where(valid, jnp.abs(delta).astype(F32) * float(dil), FAR)


def _span_specs(colblock, nb64):
    def mk(i):
        return pl.BlockSpec((64, A_WIDTH), lambda r, n: (jnp.clip(2 * n - 1 + i, 0, nb64 - 1), colblock(r)))
    return [mk(i) for i in range(4)]


def _cat(refs):
    return jnp.concatenate([t[...] for t in refs], axis=0)


def _attn_fwd(qkv, g, name):
    s = qkv.shape[0]
    dil = DILATIONS[g]
    length = s // dil
    nblk = length // A_QBLOCK
    cpr = QKV_COLS // A_WIDTH
    base = g * 3
    view = qkv.reshape(length, dil * QKV_COLS)

    def body(q_ref, k0, k1, k2, k3, v0, v1, v2, v3, o_ref, l_ref):
        dist = _band_dist(pl.program_id(1), length, dil, False)
        kk = _cat((k0, k1, k2, k3))
        vv = _cat((v0, v1, v2, v3))
        for h, cs in enumerate(HEAD_COLS):
            sc = _dot(q_ref[:, cs], kk[:, cs], 1, 1) * 0.125 - SLOPES[h] * dist
            m = jnp.max(sc, axis=1, keepdims=True)
            p = jnp.exp(sc - m)
            z = jnp.sum(p, axis=1, keepdims=True)
            o_ref[:, cs] = _dot(p.astype(BF16), vv[:, cs]) / z
            l_ref[:, cs] = jnp.broadcast_to(m + jnp.log(z), (A_QBLOCK, A_HEAD_DIM))

    qspec = pl.BlockSpec((A_QBLOCK, A_WIDTH), lambda r, n: (n, r * cpr + base))
    kspecs = _span_specs(lambda r: r * cpr + base + 1, 2 * nblk)
    vspecs = _span_specs(lambda r: r * cpr + base + 2, 2 * nblk)
    ospec = pl.BlockSpec((A_QBLOCK, A_WIDTH), lambda r, n: (n, r))
    o, l = pl.pallas_call(
        body, name=name, grid=(dil, nblk), in_specs=[qspec] + kspecs + vspecs, out_specs=[ospec, ospec],
        out_shape=[jax.ShapeDtypeStruct((length, dil * A_WIDTH), F32)] * 2,
        compiler_params=_cp("parallel", "parallel"),
    )(view, *([view] * 8))
    return o.reshape(s, A_WIDTH), l.reshape(s, A_WIDTH)


def _attn_merge(os_, ls_, gate, name):
    s, w = gate.shape
    tm = min(ROWS, s)

    def body(o0, o1, o2, l0, l1, l2, g_ref, y_ref, o_ref, l_ref):
        a, b, c = l0[...], l1[...], l2[...]
        m = jnp.maximum(jnp.maximum(a, b), c)
        ea, eb, ec = jnp.exp(a - m), jnp.exp(b - m), jnp.exp(c - m)
        z = ea + eb + ec
        o = (ea * o0[...] + eb * o1[...] + ec * o2[...]) / z
        o_ref[...] = o
        l_ref[...] = m + jnp.log(z)
        y_ref[...] = (o * _silu(g_ref[...])).astype(BF16)

    rs = _row_spec(tm, w)
    return pl.pallas_call(
        body, name=name, grid=(s // tm,), in_specs=[rs] * 7, out_specs=[rs] * 3,
        out_shape=[jax.ShapeDtypeStruct((s, w), BF16), jax.ShapeDtypeStruct((s, w), F32), jax.ShapeDtypeStruct((s, w), F32)],
        compiler_params=_cp("parallel"),
    )(*os_, *ls_, gate)


def _attn_gate_bwd(dyy, o, gate, name):
    s, w = gate.shape
    tm = min(ROWS, s)

    def body(dy_ref, o_ref, g_ref, do_ref, dg_ref, dl_ref):
        dyv, ov, gv = dy_ref[...], o_ref[...], g_ref[...]
        do = dyv * _silu(gv)
        do_ref[...] = do
        dg_ref[...] = (dyv * ov * _dsilu(gv)).astype(BF16)
        prod = do * ov
        for h in range(A_HEADS):
            cs = slice(h * 64, h * 64 + 64)
            dl_ref[:, cs] = jnp.broadcast_to(jnp.sum(prod[:, cs], axis=1, keepdims=True), (tm, 64))

    rs = _row_spec(tm, w)
    return pl.pallas_call(
        body, name=name, grid=(s // tm,), in_specs=[rs] * 3, out_specs=[rs] * 3,
        out_shape=[jax.ShapeDtypeStruct((s, w), F32), jax.ShapeDtypeStruct((s, w), BF16), jax.ShapeDtypeStruct((s, w), F32)],
        compiler_params=_cp("parallel"),
    )(dyy, o, gate)


def _attn_dq(qkv, do, lse, delta, g, name):
    s = qkv.shape[0]
    dil = DILATIONS[g]
    length = s // dil
    nblk = length // A_QBLOCK
    cpr = QKV_COLS // A_WIDTH
    base = g * 3
    view = qkv.reshape(length, dil * QKV_COLS)
    wide = lambda t: t.reshape(length, dil * A_WIDTH)

    def body(q_ref, k0, k1, k2, k3, v0, v1, v2, v3, do_ref, l_ref, dl_ref, dq_ref):
        dist = _band_dist(pl.program_id(1), length, dil, False)
        kk = _cat((k0, k1, k2, k3))
        vv = _cat((v0, v1, v2, v3))
        for h, cs in enumerate(HEAD_COLS):
            c0 = h * A_HEAD_DIM
            sc = _dot(q_ref[:, cs], kk[:, cs], 1, 1) * 0.125 - SLOPES[h] * dist
            p = jnp.exp(sc - l_ref[:, c0:c0 + 1])
            dp = _dot(do_ref[:, cs].astype(BF16), vv[:, cs], 1, 1)
            ds = p * (dp - dl_ref[:, c0:c0 + 1])
            dq_ref[:, cs] = (_dot(ds.astype(BF16), kk[:, cs]) * 0.125).astype(BF16)

    qspec = pl.BlockSpec((A_QBLOCK, A_WIDTH), lambda r, n: (n, r * cpr + base))
    kspecs = _span_specs(lambda r: r * cpr + base + 1, 2 * nblk)
    vspecs = _span_specs(lambda r: r * cpr + base + 2, 2 * nblk)
    ospec = pl.BlockSpec((A_QBLOCK, A_WIDTH), lambda r, n: (n, r))
    dq = pl.pallas_call(
        body, name=name, grid=(dil, nblk), in_specs=[qspec] + kspecs + vspecs + [ospec] * 3,
        out_specs=ospec, out_shape=jax.ShapeDtypeStruct((length, dil * A_WIDTH), BF16),
        compiler_params=_cp("parallel", "parallel"),
    )(view, *([view] * 8), wide(do), wide(lse), wide(delta))
    return dq.reshape(s, A_WIDTH)


def _attn_dkv(qkv, do, lse, delta, g, name):
    s = qkv.shape[0]
    dil = DILATIONS[g]
    length = s // dil
    nblk = length // A_QBLOCK
    cpr = QKV_COLS // A_WIDTH
    base = g * 3
    view = qkv.reshape(length, dil * QKV_COLS)
    wide = lambda t: t.reshape(length, dil * A_WIDTH)

    def by_residue(t):
        c = t[:, ::A_HEAD_DIM].reshape(length, dil, A_HEADS).transpose(1, 2, 0)
        return jnp.pad(c, ((0, 0), (0, 0), (A_RADIUS, A_RADIUS)))

    def body(k_ref, v_ref, q0, q1, q2, q3, d0, d1, d2, d3, la, lb, ea, eb, dk_ref, dv_ref):
        dist = _band_dist(pl.program_id(1), length, dil, False)
        qq = _cat((q0, q1, q2, q3))
        dd = _cat((d0, d1, d2, d3)).astype(BF16)
        lse = jnp.concatenate([la[...], lb[...]], axis=1)
        dlt = jnp.concatenate([ea[...], eb[...]], axis=1)
        for h, cs in enumerate(HEAD_COLS):
            sc = _dot(k_ref[:, cs], qq[:, cs], 1, 1) * 0.125 - SLOPES[h] * dist
            p = jnp.exp(sc - lse[h:h + 1, :])
            dv_ref[:, cs] = _dot(p.astype(BF16), dd[:, cs]).astype(BF16)
            dp = _dot(v_ref[:, cs], dd[:, cs], 1, 1)
            ds = p * (dp - dlt[h:h + 1, :])
            dk_ref[:, cs] = (_dot(ds.astype(BF16), qq[:, cs]) * 0.125).astype(BF16)

    kspec = pl.BlockSpec((A_QBLOCK, A_WIDTH), lambda r, n: (n, r * cpr + base + 1))
    vspec = pl.BlockSpec((A_QBLOCK, A_WIDTH), lambda r, n: (n, r * cpr + base + 2))
    qspecs = _span_specs(lambda r: r * cpr + base, 2 * nblk)
    wspecs = _span_specs(lambda r: r, 2 * nblk)
    rspecs = [pl.BlockSpec((None, A_HEADS, A_QBLOCK), lambda r, n: (r, 0, n)), pl.BlockSpec((None, A_HEADS, A_QBLOCK), lambda r, n: (r, 0, n + 1))]
    ospec = pl.BlockSpec((A_QBLOCK, A_WIDTH), lambda r, n: (n, r))
    lse_r, dlt_r = by_residue(lse), by_residue(delta)
    dk, dv = pl.pallas_call(
        body, name=name, grid=(dil, nblk), in_specs=[kspec, vspec] + qspecs + wspecs + rspecs * 2,
        out_specs=[ospec, ospec], out_shape=[jax.ShapeDtypeStruct((length, dil * A_WIDTH), BF16)] * 2,
        compiler_params=_cp("parallel", "parallel"),
    )(view, view, *([view] * 4), *([wide(do)] * 4), lse_r, lse_r, dlt_r, dlt_r)
    return dk.reshape(s, A_WIDTH), dv.reshape(s, A_WIDTH)


def _attn_layer_fwd(h, w_qkv, w_gate, w_out, li):
    nm = lambda t: f"a{li}_{t}"
    qkv = _mm(h, w_qkv, out_dtype=BF16, name=nm("qkv"))
    gate = _mm(h, w_gate, name=nm("gate"))
    os_, ls_ = [], []
    for g in range(3):
        o, l = _attn_fwd(qkv, g, nm(f"attn{g}"))
        os_.append(o)
        ls_.append(l)
    y, o, lse = _attn_merge(os_, ls_, gate, nm("merge"))
    out = _mm(y, w_out, name=nm("out"))
    return out, (qkv, gate, y, o, lse)


def _attn_layer_bwd(dy, h, saved, w_qkv, w_gate, w_out, li):
    nm = lambda t: f"a{li}_{t}"
    qkv, gate, y, o, lse = saved
    g_w_out = _mm(y, dy, ta=True, name=nm("dwout"))
    dyy = _mm(dy, w_out, tb=True, name=nm("dyy"))
    do, dgate, delta = _attn_gate_bwd(dyy, o, gate, nm("gatebwd"))
    parts = []
    for g in range(3):
        dq = _attn_dq(qkv, do, lse, delta, g, nm(f"dq{g}"))
        dk, dv = _attn_dkv(qkv, do, lse, delta, g, nm(f"dkv{g}"))
        parts += [dq, dk, dv]
    dqkv = jnp.concatenate(parts, axis=1)
    dh = _mm(dgate, w_gate, tb=True, name=nm("dh_gate"))
    dh = _mm(dqkv, w_qkv, tb=True, add=dh, name=nm("dh_qkv"))
    g_w_in = jnp.concatenate([_mm(h, dqkv, ta=True, name=nm("dwqkv")), _mm(h, dgate, ta=True, name=nm("dwgate"))], axis=1)
    return dh, g_w_in, g_w_out


SSM_INNER = SSM_HEADS * SSM_HEAD_DIM
SSM_BC = SSM_GROUPS * SSM_STATE
SSM_CONV_DIM = SSM_INNER + 2 * SSM_BC
GW = SSM_REP * SSM_HEAD_DIM
T = SSM_CHUNK
HALO = 8


def _conv_specs(tm, tn, s):
    nb8 = s // HALO
    cur = pl.BlockSpec((tm, tn), lambda j, i: (i, j))
    prev = pl.BlockSpec((HALO, tn), lambda j, i: (jnp.maximum(i * (tm // HALO) - 1, 0), j))
    nxt = pl.BlockSpec((HALO, tn), lambda j, i: (jnp.minimum((i + 1) * (tm // HALO), nb8 - 1), j))
    return prev, cur, nxt


def _extend(prev_ref, cur_ref, nxt_ref, i, nrow):
    p = jnp.where(i == 0, 0.0, prev_ref[...])
    n = jnp.where(i == nrow - 1, 0.0, nxt_ref[...])
    return jnp.concatenate([p, cur_ref[...], n], axis=0)


def _shift_rows(ext, off, tm):
    rows = ext.shape[0]
    return pltpu.roll(ext, (-off) % rows, 0)[HALO:HALO + tm]


def _conv_fwd(xraw, w, b, name):
    s, cdim = xraw.shape
    tm, tn = min(256, s), 1024
    nrow = s // tm

    def body(p_ref, c_ref, n_ref, w_ref, b_ref, pre_ref, act_ref):
        ext = _extend(p_ref, c_ref, n_ref, pl.program_id(1), nrow)
        acc = jnp.broadcast_to(b_ref[...], (tm, tn))
        for k in range(SSM_CONV):
            acc = acc + w_ref[k:k + 1, :] * _shift_rows(ext, k - SSM_CONV // 2, tm)
        pre_ref[...] = acc
        act_ref[...] = _silu(acc)

    prev, cur, nxt = _conv_specs(tm, tn, s)
    return pl.pallas_call(
        body, name=name, grid=(cdim // tn, nrow),
        in_specs=[prev, cur, nxt, pl.BlockSpec((SSM_CONV, tn), lambda j, i: (0, j)), pl.BlockSpec((1, tn), lambda j, i: (0, j))],
        out_specs=[cur, cur], out_shape=[jax.ShapeDtypeStruct((s, cdim), F32)] * 2,
        compiler_params=_cp("parallel", "parallel"),
    )(xraw, xraw, xraw, w, b)


def _conv_bwd(dact, pre, xraw, w, name):
    s, cdim = xraw.shape
    tm, tn = min(256, s), 1024
    nrow = s // tm

    def body(dp, dc, dn, pp, pc, pn, xp, xc, xn, w_ref, dx_ref, red_ref):
        i = pl.program_id(1)

        @pl.when(i == 0)
        def _():
            red_ref[...] = jnp.zeros_like(red_ref)

        dpre = _extend(dp, dc, dn, i, nrow) * _dsilu(_extend(pp, pc, pn, i, nrow))
        xext = _extend(xp, xc, xn, i, nrow)
        dcur = dpre[HALO:HALO + tm]
        acc = jnp.zeros((tm, tn), F32)
        for k in range(SSM_CONV):
            off = k - SSM_CONV // 2
            acc = acc + w_ref[k:k + 1, :] * _shift_rows(dpre, -off, tm)
            red_ref[k:k + 1, :] += jnp.sum(dcur * _shift_rows(xext, off, tm), axis=0, keepdims=True)
        red_ref[SSM_CONV:SSM_CONV + 1, :] += jnp.sum(dcur, axis=0, keepdims=True)
        dx_ref[...] = acc.astype(BF16)

    prev, cur, nxt = _conv_specs(tm, tn, s)
    return pl.pallas_call(
        body, name=name, grid=(cdim // tn, nrow),
        in_specs=[prev, cur, nxt] * 3 + [pl.BlockSpec((SSM_CONV, tn), lambda j, i: (0, j))],
        out_specs=[cur, pl.BlockSpec((8, tn), lambda j, i: (0, j))],
        out_shape=[jax.ShapeDtypeStruct((s, cdim), BF16), jax.ShapeDtypeStruct((8, cdim), F32)],
        compiler_params=_cp("parallel", "arbitrary"),
    )(dact, dact, dact, pre, pre, pre, xraw, xraw, xraw, w)


def _tri(lower):
    r = lax.broadcasted_iota(jnp.int32, (T, T), 0)
    c = lax.broadcasted_iota(jnp.int32, (T, T), 1)
    return (r >= c) if lower else (r <= c)


def _softplus(x):
    return jnp.maximum(x, 0.0) + jnp.log(1.0 + jnp.exp(-jnp.abs(x)))


def _dt_prep(dt_raw, bias, a_log, name):
    s = dt_raw.shape[0]
    nc = s // T

    def body(r_ref, b_ref, a_ref, dt_ref, cum_ref, cumt_ref):
        dt = _softplus(r_ref[...] + b_ref[...])
        da = dt * (-jnp.exp(a_ref[...]))
        pre = _dot_exact(_tri(True).astype(BF16), da)
        suf = _dot_exact(_tri(False).astype(BF16), da)
        lane = lax.broadcasted_iota(jnp.int32, (T, LANE), 1)
        cum = jnp.where(lane < SSM_HEADS, pre, suf)
        dt_ref[...] = dt
        cum_ref[...] = cum
        cumt_ref[...] = cum.T

    blk = pl.BlockSpec((T, LANE), lambda c: (c, 0))
    vec = pl.BlockSpec((1, LANE), lambda c: (0, 0))
    return pl.pallas_call(
        body, name=name, grid=(nc,), in_specs=[blk, vec, vec],
        out_specs=[blk, blk, pl.BlockSpec((None, LANE, T), lambda c: (c, 0, 0))],
        out_shape=[jax.ShapeDtypeStruct((s, LANE), F32), jax.ShapeDtypeStruct((s, LANE), F32), jax.ShapeDtypeStruct((nc, LANE, T), F32)],
        compiler_params=_cp("parallel"),
    )(dt_raw, bias, a_log)


def _by_group(t):
    s = t.shape[0]
    return t[:, :2 * SSM_HEADS].reshape(s, 2 * SSM_GROUPS, SSM_REP).transpose(1, 0, 2)


def _from_group(tf, tb):
    s = tf.shape[1]
    t = jnp.concatenate([tf, tb], axis=0).transpose(1, 0, 2).reshape(s, 2 * SSM_HEADS)
    return jnp.pad(t, ((0, 0), (0, LANE - 2 * SSM_HEADS)))


def _decay_mats(acol, arow, rev):
    after = _tri(not rev)
    return jnp.where(after, jnp.exp(jnp.where(after, acol - arow, 0.0)), 0.0)


def _ssd_scan(xbc, dtk, cumk, cumtk, rev, name):
    s = xbc.shape[0]
    nc = s // T
    last = 0 if rev else T - 1
    kofs = SSM_GROUPS if rev else 0
    ci = (lambda c: nc - 1 - c) if rev else (lambda c: c)

    def body(x_ref, b_ref, c_ref, dt_ref, cum_ref, cumt_ref, y_ref, st_ref, state):
        @pl.when(pl.program_id(1) == 0)
        def _():
            state[...] = jnp.zeros_like(state)

        bm = b_ref[...]
        cm = c_ref[...].astype(BF16)
        cb = _dot(cm, bm.astype(BF16), 1, 1)
        bt = bm.T.astype(BF16)
        for r in range(SSM_REP):
            cs = slice(r * SSM_HEAD_DIM, (r + 1) * SSM_HEAD_DIM)
            acol = cum_ref[:, r:r + 1]
            arow = cumt_ref[r:r + 1, :]
            lm = _decay_mats(acol, arow, rev)
            u = x_ref[:, cs] * dt_ref[:, r:r + 1]
            st = state[r]
            st_ref[r] = st
            yd = _dot((cb * lm).astype(BF16), u.astype(BF16))
            yo = jnp.exp(acol) * _dot(cm, st.astype(BF16))
            y_ref[:, cs] = yd + yo
            tot = cum_ref[last:last + 1, r:r + 1]
            dec = jnp.exp(tot - acol)
            state[r] = jnp.exp(tot) * st + _dot(bt, (dec * u).astype(BF16))

    nxb = SSM_INNER // LANE
    return pl.pallas_call(
        body, name=name, grid=(SSM_GROUPS, nc),
        in_specs=[
            pl.BlockSpec((T, GW), lambda g, c: (ci(c), g)),
            pl.BlockSpec((T, LANE), lambda g, c: (ci(c), nxb + g)),
            pl.BlockSpec((T, LANE), lambda g, c: (ci(c), nxb + SSM_GROUPS + g)),
            pl.BlockSpec((None, T, SSM_REP), lambda g, c: (kofs + g, ci(c), 0)),
            pl.BlockSpec((None, T, SSM_REP), lambda g, c: (kofs + g, ci(c), 0)),
            pl.BlockSpec((None, None, SSM_REP, T), lambda g, c: (kofs + g, ci(c), 0, 0)),
        ],
        out_specs=[
            pl.BlockSpec((T, GW), lambda g, c: (ci(c), g)),
            pl.BlockSpec((None, SSM_REP, SSM_STATE, SSM_HEAD_DIM), lambda g, c: (ci(c), g, 0, 0)),
        ],
        out_shape=[jax.ShapeDtypeStruct((s, SSM_INNER), F32), jax.ShapeDtypeStruct((nc, SSM_HEADS, SSM_STATE, SSM_HEAD_DIM), F32)],
        scratch_shapes=[pltpu.VMEM((SSM_REP, SSM_STATE, SSM_HEAD_DIM), F32)],
        compiler_params=_cp("parallel", "arbitrary"),
    )(xbc, xbc, xbc, dtk, cumk, cumtk)


def _ssd_scan_bwd(xbc, dtk, cumk, cumtk, dy, states, dvec, prev, rev, name):
    s = xbc.shape[0]
    nc = s // T
    last = 0 if rev else T - 1
    kofs = SSM_GROUPS if rev else 0
    ci = (lambda c: c) if rev else (lambda c: nc - 1 - c)
    has_prev = prev is not None

    def body(*refs):
        x_ref, b_ref, c_ref, dt_ref, cum_ref, cumt_ref, dy_ref, st_ref, dv_ref = refs[:9]
        refs = refs[9:]
        if has_prev:
            pdx, pdb, pdc = refs[:3]
            refs = refs[3:]
        dx_ref, db_ref, dc_ref, ddt_ref, dda_ref, dstate, rs_buf, in_buf, k_buf = refs

        @pl.when(pl.program_id(1) == 0)
        def _():
            dstate[...] = jnp.zeros_like(dstate)

        rs_buf[...] = jnp.zeros_like(rs_buf)
        in_buf[...] = jnp.zeros_like(in_buf)
        k_buf[...] = jnp.zeros_like(k_buf)
        bm = b_ref[...].astype(BF16)
        cm = c_ref[...].astype(BF16)
        cbt = _dot(bm, cm, 1, 1)
        cb = _dot(cm, bm, 1, 1)
        ct = c_ref[...].T.astype(BF16)
        after = _tri(not rev)
        before = _tri(rev)
        from_k = before.astype(BF16)
        ri = lax.broadcasted_iota(jnp.int32, (T, T), 0)
        cj = lax.broadcasted_iota(jnp.int32, (T, T), 1)
        strictly_before = (cj > ri) if rev else (cj < ri)
        dcb = jnp.zeros((T, T), F32)
        dc_acc = jnp.zeros((T, SSM_STATE), F32)
        db_acc = jnp.zeros((T, SSM_STATE), F32)
        for r in range(SSM_REP):
            cs = slice(r * SSM_HEAD_DIM, (r + 1) * SSM_HEAD_DIM)
            acol = cum_ref[:, r:r + 1]
            arow = cumt_ref[r:r + 1, :]
            lm = jnp.where(after, jnp.exp(jnp.where(after, acol - arow, 0.0)), 0.0)
            lmt = jnp.where(before, jnp.exp(jnp.where(before, arow - acol, 0.0)), 0.0)
            dtc = dt_ref[:, r:r + 1]
            xh = x_ref[:, cs]
            u = xh * dtc
            ub = u.astype(BF16)
            dyh = dy_ref[:, cs]
            dyb = dyh.astype(BF16)
            st = st_ref[r]
            dst = dstate[r]
            tot = cum_ref[last:last + 1, r:r + 1]
            dec = jnp.exp(tot - acol)
            eac = jnp.exp(acol)
            du_off = dec * _dot(bm, dst.astype(BF16))
            du = _dot((cbt * lmt).astype(BF16), dyb) + du_off
            gl = _dot(dyb, ub, 1, 1) * lm
            dcb = dcb + gl
            dc_acc = dc_acc + eac * _dot(dyb, st.astype(BF16), 1, 1)
            db_acc = db_acc + dec * _dot(ub, dst.astype(BF16), 1, 1)
            crossing = _dot(from_k, (gl * cb).astype(BF16))
            in_buf[:, r:r + 1] = jnp.sum(jnp.where(strictly_before, crossing, 0.0), axis=1, keepdims=True)
            y_off = eac * _dot(cm, st.astype(BF16))
            rs_buf[:, r:r + 1] = jnp.sum(dyh * y_off, axis=1, keepdims=True) - jnp.sum(u * du_off, axis=1, keepdims=True)
            kk = jnp.sum(dst * (jnp.exp(tot) * st), keepdims=True) + jnp.sum(u * du_off, keepdims=True)
            k_buf[0:1, r:r + 1] = kk
            ddt_ref[:, r:r + 1] = jnp.sum(du * xh, axis=1, keepdims=True)
            dx = du * dtc
            if has_prev:
                dx = dx + pdx[:, cs]
            else:
                dx = dx + dyh * dv_ref[:, cs]
            dx_ref[:, cs] = dx
            dstate[r] = jnp.exp(tot) * dst + _dot(ct, (eac * dyh).astype(BF16))
        dda = in_buf[...] + _dot_exact(from_k, rs_buf[...]) + k_buf[0:1, :]
        dda_ref[...] = dda[:, :SSM_REP]
        dcbb = dcb.astype(BF16)
        dc = dc_acc + _dot(dcbb, bm)
        db = db_acc + _dot(dcbb, cm, 0, 0)
        if has_prev:
            dc = dc + pdc[...]
            db = db + pdb[...]
        dc_ref[...] = dc
        db_ref[...] = db

    nxb = SSM_INNER // LANE
    xspec = pl.BlockSpec((T, GW), lambda g, c: (ci(c), g))
    gspec = pl.BlockSpec((T, LANE), lambda g, c: (ci(c), g))
    kspec = pl.BlockSpec((None, T, SSM_REP), lambda g, c: (kofs + g, ci(c), 0))
    in_specs = [
        xspec,
        pl.BlockSpec((T, LANE), lambda g, c: (ci(c), nxb + g)),
        pl.BlockSpec((T, LANE), lambda g, c: (ci(c), nxb + SSM_GROUPS + g)),
        kspec, kspec,
        pl.BlockSpec((None, None, SSM_REP, T), lambda g, c: (kofs + g, ci(c), 0, 0)),
        xspec,
        pl.BlockSpec((None, SSM_REP, SSM_STATE, SSM_HEAD_DIM), lambda g, c: (ci(c), g, 0, 0)),
        pl.BlockSpec((1, GW), lambda g, c: (0, g)),
    ]
    args = [xbc, xbc, xbc, dtk, cumk, cumtk, dy, states, dvec]
    if has_prev:
        in_specs += [xspec, gspec, gspec]
        args += list(prev)
    ospec8 = pl.BlockSpec((None, T, SSM_REP), lambda g, c: (g, ci(c), 0))
    return pl.pallas_call(
        body, name=name, grid=(SSM_GROUPS, nc), in_specs=in_specs,
        out_specs=[xspec, gspec, gspec, ospec8, ospec8],
        out_shape=[jax.ShapeDtypeStruct((s, SSM_INNER), F32), jax.ShapeDtypeStruct((s, SSM_BC), F32), jax.ShapeDtypeStruct((s, SSM_BC), F32),
                   jax.ShapeDtypeStruct((SSM_GROUPS, s, SSM_REP), F32), jax.ShapeDtypeStruct((SSM_GROUPS, s, SSM_REP), F32)],
        scratch_shapes=[pltpu.VMEM((SSM_REP, SSM_STATE, SSM_HEAD_DIM), F32), pltpu.VMEM((T, LANE), F32), pltpu.VMEM((T, LANE), F32),
                        pltpu.VMEM((8, LANE), F32)],
        compiler_params=_cp("parallel", "arbitrary"),
    )(*args)


def _ssd_post(yf, yb, xbc, z, dvec, nw, name):
    s = z.shape[0]
    tm = min(256, s)

    def body(yf_ref, yb_ref, x_ref, z_ref, dv_ref, nw_ref, o_ref):
        ys = yf_ref[...] + yb_ref[...] + dv_ref[...] * x_ref[...]
        yg = ys * _silu(z_ref[...])
        ms = jnp.mean(yg * yg, axis=1, keepdims=True)
        o_ref[...] = (yg * lax.rsqrt(ms + RMS_EPS) * nw_ref[...]).astype(BF16)

    rs = _row_spec(tm, SSM_INNER)
    vs = _vec_spec(SSM_INNER)
    return pl.pallas_call(
        body, name=name, grid=(s // tm,), in_specs=[rs, rs, rs, rs, vs, vs], out_specs=rs,
        out_shape=jax.ShapeDtypeStruct((s, SSM_INNER), BF16), compiler_params=_cp("parallel"),
    )(yf, yb, xbc, z, dvec, nw)


def _ssd_post_bwd(dyn, yf, yb, xbc, z, dvec, nw, name):
    s = z.shape[0]
    tm = min(256, s)

    def body(dyn_ref, yf_ref, yb_ref, x_ref, z_ref, dv_ref, nw_ref, dys_ref, dz_ref, red_ref):
        @pl.when(pl.program_id(0) == 0)
        def _():
            red_ref[...] = jnp.zeros_like(red_ref)

        xv, zv = x_ref[...], z_ref[...]
        ys = yf_ref[...] + yb_ref[...] + dv_ref[...] * xv
        sz = _silu(zv)
        yg = ys * sz
        rstd = lax.rsqrt(jnp.mean(yg * yg, axis=1, keepdims=True) + RMS_EPS)
        yhat = yg * rstd
        dynv = dyn_ref[...]
        dyh = dynv * nw_ref[...]
        dyg = rstd * (dyh - yhat * jnp.mean(dyh * yhat, axis=1, keepdims=True))
        dys = dyg * sz
        dys_ref[...] = dys
        dz_ref[...] = (dyg * ys * _dsilu(zv)).astype(BF16)
        red_ref[0:1, :] += jnp.sum(dynv * yhat, axis=0, keepdims=True)
        red_ref[1:2, :] += jnp.sum(dys * xv, axis=0, keepdims=True)

    rs = _row_spec(tm, SSM_INNER)
    vs = _vec_spec(SSM_INNER)
    return pl.pallas_call(
        body, name=name, grid=(s // tm,), in_specs=[rs, rs, rs, rs, rs, vs, vs],
        out_specs=[rs, rs, _vec_spec(SSM_INNER, 8)],
        out_shape=[jax.ShapeDtypeStruct((s, SSM_INNER), F32), jax.ShapeDtypeStruct((s, SSM_INNER), BF16), jax.ShapeDtypeStruct((8, SSM_INNER), F32)],
        compiler_params=_cp("arbitrary"),
    )(dyn, yf, yb, xbc, z, dvec, nw)


def _dt_bwd(dt_raw, bias, a_log, dt, ddt, dda, name):
    s = dt_raw.shape[0]
    tm = min(1024, s)

    def body(r_ref, b_ref, a_ref, dt_ref, ddt_ref, dda_ref, o_ref, red_ref):
        @pl.when(pl.program_id(0) == 0)
        def _():
            red_ref[...] = jnp.zeros_like(red_ref)

        a = -jnp.exp(a_ref[...])
        ddav = dda_ref[...]
        draw = (ddt_ref[...] + a * ddav) * _sigmoid(r_ref[...] + b_ref[...])
        o_ref[...] = draw.astype(BF16)
        red_ref[0:1, :] += jnp.sum(draw, axis=0, keepdims=True)
        red_ref[1:2, :] += a * jnp.sum(ddav * dt_ref[...], axis=0, keepdims=True)

    rs = _row_spec(tm, LANE)
    vs = _vec_spec(LANE)
    return pl.pallas_call(
        body, name=name, grid=(s // tm,), in_specs=[rs, vs, vs, rs, rs, rs], out_specs=[rs, _vec_spec(LANE, 8)],
        out_shape=[jax.ShapeDtypeStruct((s, LANE), BF16), jax.ShapeDtypeStruct((8, LANE), F32)],
        compiler_params=_cp("arbitrary"),
    )(dt_raw, bias, a_log, dt, ddt, dda)


def _pad_lanes(v):
    v = v.reshape(1, -1)
    return jnp.pad(v, ((0, 0), (0, LANE - v.shape[1])))


def _ssd_prep_weights(w_in, conv_w, conv_b, dt_bias, a_log, d_skip, norm_w, w_out):
    return dict(
        w_z=w_in[:, :SSM_INNER].astype(BF16),
        w_xbc=w_in[:, SSM_INNER:SSM_INNER + SSM_CONV_DIM].astype(BF16),
        w_dt=jnp.pad(w_in[:, SSM_INNER + SSM_CONV_DIM:], ((0, 0), (0, LANE - 2 * SSM_HEADS))).astype(BF16),
        conv_w=conv_w, conv_b=conv_b.reshape(1, -1), bias=_pad_lanes(dt_bias), a_log=_pad_lanes(a_log),
        dvec=jnp.repeat(d_skip, SSM_HEAD_DIM).reshape(1, -1), nw=norm_w.reshape(1, -1), w_out=w_out.astype(BF16),
    )


def _ssd_layer_fwd(h, w, li):
    nm = lambda t: f"b{li}_{t}"
    z = _mm(h, w["w_z"], name=nm("z"))
    xraw = _mm(h, w["w_xbc"], name=nm("xbc"))
    dt_raw = _mm(h, w["w_dt"], name=nm("dt"))
    pre, xbc = _conv_fwd(xraw, w["conv_w"], w["conv_b"], nm("conv"))
    dt, cum, cumt = _dt_prep(dt_raw, w["bias"], w["a_log"], nm("dtprep"))
    nc = cumt.shape[0]
    dtk, cumk = _by_group(dt), _by_group(cum)
    cumtk = cumt[:, :2 * SSM_HEADS].reshape(nc, 2 * SSM_GROUPS, SSM_REP, T).transpose(1, 0, 2, 3)
    yf, stf = _ssd_scan(xbc, dtk, cumk, cumtk, False, nm("scan_f"))
    yb, stb = _ssd_scan(xbc, dtk, cumk, cumtk, True, nm("scan_b"))
    yn = _ssd_post(yf, yb, xbc, z, w["dvec"], w["nw"], nm("post"))
    out = _mm(yn, w["w_out"], name=nm("out"))
    return out, (z, xraw, dt_raw, pre, xbc, dt, dtk, cumk, cumtk, yf, stf, yb, stb, yn)


def _ssd_layer_bwd(dy, h, saved, w, li):
    nm = lambda t: f"b{li}_{t}"
    z, xraw, dt_raw, pre, xbc, dt, dtk, cumk, cumtk, yf, stf, yb, stb, yn = saved
    g_w_out = _mm(yn, dy, ta=True, name=nm("dwout"))
    dyn = _mm(dy, w["w_out"], tb=True, name=nm("dyn"))
    dys, dz, pred = _ssd_post_bwd(dyn, yf, yb, xbc, z, w["dvec"], w["nw"], nm("postbwd"))
    dx1, db1, dc1, ddt_f, dda_f = _ssd_scan_bwd(xbc, dtk, cumk, cumtk, dys, stf, w["dvec"], None, False, nm("scanbwd_f"))
    dx, db, dc, ddt_b, dda_b = _ssd_scan_bwd(xbc, dtk, cumk, cumtk, dys, stb, w["dvec"], (dx1, db1, dc1), True, nm("scanbwd_b"))
    dact = jnp.concatenate([dx, db, dc], axis=1)
    dxraw, cred = _conv_bwd(dact, pre, xraw, w["conv_w"], nm("convbwd"))
    draw, dred = _dt_bwd(dt_raw, w["bias"], w["a_log"], dt, _from_group(ddt_f, ddt_b), _from_group(dda_f, dda_b), nm("dtbwd"))
    dh = _mm(dz, w["w_z"], tb=True, name=nm("dh_z"))
    dh = _mm(dxraw, w["w_xbc"], tb=True, add=dh, name=nm("dh_xbc"))
    dh = _mm(draw, w["w_dt"], tb=True, add=dh, name=nm("dh_dt"))
    g_w_in = jnp.concatenate([_mm(h, dz, ta=True, name=nm("dwz")), _mm(h, dxraw, ta=True, name=nm("dwxbc")),
                              _mm(h, draw, ta=True, name=nm("dwdt"))[:, :2 * SSM_HEADS]], axis=1)
    grads = (g_w_in, cred[:SSM_CONV], cred[SSM_CONV], dred[0, :2 * SSM_HEADS].reshape(2, SSM_HEADS),
             dred[1, :2 * SSM_HEADS].reshape(2, SSM_HEADS), pred[1].reshape(SSM_HEADS, SSM_HEAD_DIM).sum(axis=1), pred[0], g_w_out)
    return dh, grads


B_GRAD_NAMES = ("b_w_in", "b_conv_w", "b_conv_b", "b_dt_bias", "b_a_log", "b_d", "b_norm_w", "b_w_out")


def _local_step(x, tgt, mod, w):
    d = x.shape[1]
    qkv_cols = QKV_COLS
    layers = []
    for i in range(DEPTH):
        j = i // 2
        if i % 2 == 0:
            layers.append((w["a_w_in"][j][:, :qkv_cols].astype(BF16), w["a_w_in"][j][:, qkv_cols:].astype(BF16), w["a_w_out"][j].astype(BF16)))
        else:
            layers.append(_ssd_prep_weights(w["b_w_in"][j], w["b_conv_w"][j], w["b_conv_b"][j], w["b_dt_bias"][j], w["b_a_log"][j],
                                            w["b_d"][j], w["b_norm_w"][j], w["b_w_out"][j]))
    saved = []
    for i in range(DEPTH):
        shift, scale, gate = mod[i:i + 1, :d], mod[i:i + 1, d:2 * d], mod[i:i + 1, 2 * d:]
        h = _modulate(x, scale, shift, f"l{i}_mod")
        if i % 2 == 0:
            out, sv = _attn_layer_fwd(h, *layers[i], i)
        else:
            out, sv = _ssd_layer_fwd(h, layers[i], i)
        xn = _resid_ln(x, out, gate, w["ln_g"][i:i + 1], w["ln_b"][i:i + 1], f"l{i}_ln")
        saved.append((x, h, out, sv))
        x = xn
    dx, lred = _loss_grad(x, tgt, "loss")
    loss = 0.5 * jnp.sum(lred[0]) / d
    dmod, g_ln_g, g_ln_b = [None] * DEPTH, [None] * DEPTH, [None] * DEPTH
    ga_in, ga_out = [None, None], [None, None]
    gb = [None, None]
    for i in reversed(range(DEPTH)):
        j = i // 2
        xi, h, out, sv = saved[i]
        scale, gate = mod[i:i + 1, d:2 * d], mod[i:i + 1, 2 * d:]
        du, dy, red = _resid_ln_bwd(xi, out, dx, gate, w["ln_g"][i:i + 1], f"l{i}_lnbwd")
        g_ln_g[i], g_ln_b[i] = red[1], red[2]
        if i % 2 == 0:
            dh, ga_in[j], ga_out[j] = _attn_layer_bwd(dy, h, sv, *layers[i], i)
        else:
            dh, gb[j] = _ssd_layer_bwd(dy, h, sv, layers[i], i)
        dx, red2 = _modulate_bwd(du, dh, xi, scale, f"l{i}_modbwd")
        dmod[i] = jnp.concatenate([red2[1], red2[0], red[0]])
    grads = {"ln_g": jnp.stack(g_ln_g), "ln_b": jnp.stack(g_ln_b), "a_w_in": jnp.stack(ga_in), "a_w_out": jnp.stack(ga_out)}
    for k, n in enumerate(B_GRAD_NAMES):
        grads[n] = jnp.stack([gb[0][k], gb[1][k]])
    return loss, dx, jnp.stack(dmod), grads


MESH = pl.DeviceIdType.MESH
ANY = pl.BlockSpec(memory_space=pl.ANY)
N_DEV = 8
N_SHARD = 4


def _flip(v, bit):
    return 1 - v if bit else v


def _all_gather8(v, name):
    def body(v_ref, o_ref, send_sems, recv_sems, local_sem):
        x, y, c = lax.axis_index("x"), lax.axis_index("y"), lax.axis_index("c")
        me = 4 * x + 2 * y + c
        local = pltpu.make_async_copy(v_ref, o_ref.at[me], local_sem)
        local.start()
        copies = []
        for k in range(1, N_DEV):
            peer = (_flip(x, k & 4), _flip(y, k & 2), _flip(c, k & 1))
            copies.append(pltpu.make_async_remote_copy(
                src_ref=v_ref, dst_ref=o_ref.at[me], send_sem=send_sems.at[k - 1], recv_sem=recv_sems.at[k - 1],
                device_id=peer, device_id_type=MESH))
        for cp in copies:
            cp.start()
        for cp in copies:
            cp.wait()
        local.wait()

    return pl.pallas_call(
        body, name=name, in_specs=[ANY], out_specs=ANY, out_shape=jax.ShapeDtypeStruct((N_DEV,) + v.shape, v.dtype),
        scratch_shapes=[pltpu.SemaphoreType.DMA((N_DEV - 1,)), pltpu.SemaphoreType.DMA((N_DEV - 1,)), pltpu.SemaphoreType.DMA],
    )(v)


def _exchange4(src, same, name):
    shape = src.shape if same else src.shape[1:]

    def body(s_ref, o_ref, send_sems, recv_sems, local_sem):
        x, y, c = lax.axis_index("x"), lax.axis_index("y"), lax.axis_index("c")
        m = 2 * x + y
        pick = (lambda j: s_ref) if same else (lambda j: s_ref.at[j])
        local = pltpu.make_async_copy(pick(m), o_ref.at[m], local_sem)
        local.start()
        copies = []
        for k in range(1, N_SHARD):
            px, py = _flip(x, k & 2), _flip(y, k & 1)
            copies.append(pltpu.make_async_remote_copy(
                src_ref=pick(2 * px + py), dst_ref=o_ref.at[m], send_sem=send_sems.at[k - 1], recv_sem=recv_sems.at[k - 1],
                device_id=(px, py, c), device_id_type=MESH))
        for cp in copies:
            cp.start()
        for cp in copies:
            cp.wait()
        local.wait()

    return pl.pallas_call(
        body, name=name, in_specs=[ANY], out_specs=ANY, out_shape=jax.ShapeDtypeStruct((N_SHARD,) + shape, src.dtype),
        scratch_shapes=[pltpu.SemaphoreType.DMA((N_SHARD - 1,)), pltpu.SemaphoreType.DMA((N_SHARD - 1,)), pltpu.SemaphoreType.DMA],
    )(src)


def _swap_sibling(v, name):
    def body(v_ref, o_ref, send_sem, recv_sem):
        x, y, c = lax.axis_index("x"), lax.axis_index("y"), lax.axis_index("c")
        cp = pltpu.make_async_remote_copy(src_ref=v_ref, dst_ref=o_ref, send_sem=send_sem, recv_sem=recv_sem,
                                          device_id=(x, y, 1 - c), device_id_type=MESH)
        cp.start()
        cp.wait()

    return pl.pallas_call(
        body, name=name, in_specs=[ANY], out_specs=ANY, out_shape=jax.ShapeDtypeStruct(v.shape, v.dtype),
        scratch_shapes=[pltpu.SemaphoreType.DMA, pltpu.SemaphoreType.DMA],
    )(v)


def _sum_slots(a, name):
    n, r, cdim = a.shape
    tm = max(t for t in range(16, 641, 16) if r % t == 0)

    def body(a_ref, o_ref):
        acc = a_ref[0].astype(F32)
        for k in range(1, n):
            acc = acc + a_ref[k].astype(F32)
        o_ref[...] = acc

    return pl.pallas_call(
        body, name=name, grid=(r // tm,), in_specs=[pl.BlockSpec((n, tm, cdim), lambda i: (0, i, 0))],
        out_specs=pl.BlockSpec((tm, cdim), lambda i: (i, 0)), out_shape=jax.ShapeDtypeStruct((r, cdim), F32),
        compiler_params=_cp("parallel"),
    )(a)


def _silu_rows(v, name):
    def body(v_ref, o_ref):
        o_ref[...] = _silu(v_ref[...])

    return pl.pallas_call(body, name=name, out_shape=jax.ShapeDtypeStruct(v.shape, F32))(v)


PACK_COLS = 1024
PACK_TILE = 256


def _adamw(w, g1, g2, m, v, name):
    r = w.shape[0]
    c1 = 1.0 / (1.0 - ADAM_B1 ** ADAM_STEP)
    c2 = 1.0 / (1.0 - ADAM_B2 ** ADAM_STEP)

    def body(w_ref, g1_ref, g2_ref, m_ref, v_ref, g_ref, d_ref, nm_ref, nv_ref):
        g = g1_ref[...] + g2_ref[...]
        mn = ADAM_B1 * m_ref[...] + (1.0 - ADAM_B1) * g
        vn = ADAM_B2 * v_ref[...] + (1.0 - ADAM_B2) * (g * g)
        g_ref[...] = g
        nm_ref[...] = mn
        nv_ref[...] = vn
        d_ref[...] = -ADAM_LR * ((mn * c1) / (jnp.sqrt(vn * c2) + ADAM_EPS) + ADAM_WD * w_ref[...])

    spec = pl.BlockSpec((PACK_TILE, PACK_COLS), lambda i: (i, 0))
    return pl.pallas_call(
        body, name=name, grid=(r // PACK_TILE,), in_specs=[spec] * 5, out_specs=[spec] * 4,
        out_shape=[jax.ShapeDtypeStruct(w.shape, F32)] * 4, compiler_params=_cp("parallel"),
    )(w, g1, g2, m, v)


def _rows(a):
    f = a.reshape(-1)
    pad = (-f.shape[0]) % PACK_COLS
    if pad:
        f = jnp.pad(f, (0, pad))
    return f.reshape(-1, PACK_COLS)


def _nrows(shape):
    return -(-int(np.prod(shape)) // PACK_COLS)


def _pack(parts, total_rows=None):
    p = jnp.concatenate([_rows(a) for a in parts], axis=0)
    if total_rows is not None and total_rows > p.shape[0]:
        p = jnp.pad(p, ((0, total_rows - p.shape[0]), (0, 0)))
    return p


def _unpack(p, shapes):
    out, r0 = [], 0
    for shp in shapes:
        n = int(np.prod(shp))
        nr = _nrows(shp)
        out.append(p[r0:r0 + nr].reshape(-1)[:n].reshape(shp))
        r0 += nr
    return out


def _unshard_cols(g):
    return jnp.concatenate([g[k] for k in range(N_SHARD)], axis=-1)


def _shard_cols(a):
    n = a.shape[-1] // N_SHARD
    return jnp.stack([a[..., k * n:(k + 1) * n] for k in range(N_SHARD)])


def _unshard_rows(g):
    return jnp.concatenate([g[k] for k in range(N_SHARD)], axis=1)


def _shard_rows(a):
    n = a.shape[1] // N_SHARD
    return jnp.stack([a[:, k * n:(k + 1) * n] for k in range(N_SHARD)])


W_NAMES = ("ada_w", "ada_b", "ln_g", "ln_b", "a_w_in", "a_w_out", "b_w_in", "b_conv_w", "b_conv_b", "b_dt_bias", "b_a_log", "b_d",
           "b_norm_w", "b_w_out")
BIG = ("a_w_in", "a_w_out", "b_w_in", "b_w_out")
PACK_ORDER = BIG + ("ada_w", "ada_b", "ln_g", "ln_b", "b_conv_w", "b_conv_b", "b_dt_bias", "b_a_log", "b_d", "b_norm_w")


def kernel(x, c, ada_w, ada_b, ln_g, ln_b, a_w_in, a_w_out, b_w_in, b_conv_w, b_conv_b, b_dt_bias, b_a_log, b_d, b_norm_w, b_w_out, loss_target, m_ada_w, m_ada_b, m_ln_g, m_ln_b, m_a_w_in, m_a_w_out, m_b_w_in, m_b_conv_w, m_b_conv_b, m_b_dt_bias, m_b_a_log, m_b_d, m_b_norm_w, m_b_w_out, v_ada_w, v_ada_b, v_ln_g, v_ln_b, v_a_w_in, v_a_w_out, v_b_w_in, v_b_conv_w, v_b_conv_b, v_b_dt_bias, v_b_a_log, v_b_d, v_b_norm_w, v_b_w_out):
    w = dict(ada_w=ada_w, ada_b=ada_b, ln_g=ln_g, ln_b=ln_b, a_w_in=a_w_in, a_w_out=a_w_out, b_w_in=b_w_in, b_conv_w=b_conv_w,
             b_conv_b=b_conv_b, b_dt_bias=b_dt_bias, b_a_log=b_a_log, b_d=b_d, b_norm_w=b_norm_w, b_w_out=b_w_out)
    mom = dict(ada_w=m_ada_w, ada_b=m_ada_b, ln_g=m_ln_g, ln_b=m_ln_b, a_w_in=m_a_w_in, a_w_out=m_a_w_out, b_w_in=m_b_w_in,
               b_conv_w=m_b_conv_w, b_conv_b=m_b_conv_b, b_dt_bias=m_b_dt_bias, b_a_log=m_b_a_log, b_d=m_b_d, b_norm_w=m_b_norm_w,
               b_w_out=m_b_w_out)
    var = dict(ada_w=v_ada_w, ada_b=v_ada_b, ln_g=v_ln_g, ln_b=v_ln_b, a_w_in=v_a_w_in, a_w_out=v_a_w_out, b_w_in=v_b_w_in,
               b_conv_w=v_b_conv_w, b_conv_b=v_b_conv_b, b_dt_bias=v_b_dt_bias, b_a_log=v_b_a_log, b_d=v_b_d, b_norm_w=v_b_norm_w,
               b_w_out=v_b_w_out)
    ax, ay, ac = lax.axis_index("x"), lax.axis_index("y"), lax.axis_index("c")
    me = 4 * ax + 2 * ay + ac
    shard = 2 * ax + ay
    d = x.shape[-1]
    dsh = ada_w.shape[-1]

    small_in = (c, b_conv_w, b_conv_b, b_norm_w)
    g0 = _all_gather8(_pack(small_in).reshape(-1, LANE), "gather_small_in").reshape(N_DEV, -1, PACK_COLS)
    per_dev = [_unpack(g0[k], [a.shape for a in small_in]) for k in range(N_DEV)]
    c_all = jnp.concatenate([p[0] for p in per_dev], axis=0)
    conv_w_full, conv_b_full, norm_w_full = (_unshard_cols([per_dev[2 * k][t] for k in range(N_SHARD)]) for t in (1, 2, 3))

    cond = _silu_rows(jnp.pad(c_all, ((0, 8), (0, 0))), "cond")
    bias = lax.dynamic_slice_in_dim(ada_b, shard * dsh, dsh, axis=1)
    part = jnp.stack([_mm(cond, ada_w[i], add=jnp.broadcast_to(bias[i], (16, dsh)), name=f"mod{i}")[:N_DEV] for i in range(DEPTH)])
    g1 = _all_gather8(part.reshape(-1, LANE), "gather_mod").reshape(N_DEV, DEPTH, N_DEV, dsh)
    mod_all = _unshard_cols([g1[2 * k] for k in range(N_SHARD)])
    mod = lax.dynamic_index_in_dim(mod_all, me, axis=1, keepdims=False)

    gw = _exchange4(_pack([w[n] for n in BIG]).astype(BF16), True, "gather_weights")
    big_sh = [_unpack(gw[k], [w[n].shape for n in BIG]) for k in range(N_SHARD)]
    full = dict(
        ln_g=ln_g, ln_b=ln_b, b_dt_bias=b_dt_bias, b_a_log=b_a_log, b_d=b_d,
        b_conv_w=conv_w_full, b_conv_b=conv_b_full, b_norm_w=norm_w_full,
        a_w_in=_unshard_cols([s[0] for s in big_sh]), a_w_out=_unshard_rows([s[1] for s in big_sh]),
        b_w_in=_unshard_cols([s[2] for s in big_sh]), b_w_out=_unshard_rows([s[3] for s in big_sh]),
    )

    loss, grad_x, dmod, g = _local_step(x[0], loss_target[0], mod, full)

    gsh = (_shard_cols(g["a_w_in"]), _shard_rows(g["a_w_out"]), _shard_cols(g["b_w_in"]), _shard_rows(g["b_w_out"]))
    to_send = jnp.stack([_pack([t[k] for t in gsh]) for k in range(N_SHARD)]).astype(BF16)
    mine = _sum_slots(_exchange4(to_send, False, "scatter_grads"), "sum_shards")
    theirs = _swap_sibling(mine, "swap_grads")

    small_g = (dmod, g["ln_g"], g["ln_b"], g["b_dt_bias"], g["b_a_log"], g["b_d"], g["b_conv_w"], g["b_conv_b"], g["b_norm_w"],
               loss.reshape(1))
    g2 = _all_gather8(_pack(small_g).reshape(-1, LANE), "gather_small_grads")
    tot = _unpack(_sum_slots(g2, "sum_small").reshape(-1, PACK_COLS), [a.shape for a in small_g])
    g_ada_b, g_ln_g, g_ln_b, g_dt_bias, g_a_log, g_d, g_conv_w, g_conv_b, g_norm_w, loss_sum = tot
    dmod_all = g2.reshape(N_DEV, -1)[:, :dmod.size].reshape(N_DEV, DEPTH, 3 * d)
    dmod_mine = lax.dynamic_slice_in_dim(dmod_all, shard * dsh, dsh, axis=2)
    g_ada_w = jnp.stack([_mm(cond, jnp.pad(dmod_mine[:, i], ((0, 8), (0, 0))), ta=True, name=f"dada{i}") for i in range(DEPTH)])
    csh = g_conv_w.shape[-1] // N_SHARD
    nsh = g_norm_w.shape[-1] // N_SHARD
    small_grads = dict(
        ada_w=g_ada_w, ada_b=g_ada_b, ln_g=g_ln_g, ln_b=g_ln_b, b_dt_bias=g_dt_bias, b_a_log=g_a_log, b_d=g_d,
        b_conv_w=lax.dynamic_slice_in_dim(g_conv_w, shard * csh, csh, axis=2),
        b_conv_b=lax.dynamic_slice_in_dim(g_conv_b, shard * csh, csh, axis=1),
        b_norm_w=lax.dynamic_slice_in_dim(g_norm_w, shard * nsh, nsh, axis=1),
    )

    rest = PACK_ORDER[len(BIG):]
    rows = sum(_nrows(w[n].shape) for n in PACK_ORDER)
    rows = -(-rows // PACK_TILE) * PACK_TILE
    pw, pm, pv = (_pack([t[n] for n in PACK_ORDER], rows) for t in (w, mom, var))
    pg1 = _pack([mine] + [small_grads[n] for n in rest], rows)
    pg2 = _pack([theirs], rows)
    outs = [_unpack(p, [w[n].shape for n in PACK_ORDER]) for p in _adamw(pw, pg1, pg2, pm, pv, "adamw")]
    by_name = [dict(zip(PACK_ORDER, o)) for o in outs]
    return (loss_sum.reshape(()), grad_x[None], *[t[n] for t in by_name for n in W_NAMES])
```

```python
import jax
import jax.numpy as jnp
import numpy as np
from jax import lax
from jax.experimental import pallas as pl
from jax.experimental.pallas import tpu as pltpu

F32 = jnp.float32
BF16 = jnp.bfloat16

DEPTH = 4
A_HEADS = 16
A_HEAD_DIM = 64
A_WIDTH = A_HEADS * A_HEAD_DIM
DILATIONS = (1, 4, 16)
A_RADIUS = 64
A_QBLOCK = 128
SSM_HEADS = 32
SSM_HEAD_DIM = 64
SSM_STATE = 128
SSM_GROUPS = 4
SSM_REP = SSM_HEADS // SSM_GROUPS
SSM_CONV = 5
SSM_CHUNK = 128
DEEPNORM_ALPHA = (2 * DEPTH) ** 0.25
LN_EPS = 1e-5
RMS_EPS = 1e-5
ADAM_LR, ADAM_B1, ADAM_B2, ADAM_EPS, ADAM_WD, ADAM_STEP = 0.001, 0.9, 0.999, 1e-08, 0.01, 10
VMEM_LIMIT = 56 * 1024 * 1024
LANE = 128


def _cp(*sem):
    return pltpu.CompilerParams(dimension_semantics=sem, vmem_limit_bytes=VMEM_LIMIT)


def _tile(dim, target):
    if dim <= target:
        return dim
    t = (target // LANE) * LANE
    while dim % t:
        t -= LANE
    return t


def _sigmoid(x):
    return 1.0 / (1.0 + jnp.exp(-x))


def _silu(x):
    return x * _sigmoid(x)


def _dsilu(x):
    s = _sigmoid(x)
    return s * (1.0 + x * (1.0 - s))


def _split3(x):
    a = x.astype(BF16)
    r = x - a.astype(F32)
    b = r.astype(BF16)
    c = (r - b.astype(F32)).astype(BF16)
    return a, b, c


def _dot(a, b, ca=1, cb=0):
    return lax.dot_general(a, b, (((ca,), (cb,)), ((), ())), preferred_element_type=F32)


def _dot_exact(m01, x):
    a, b, c = _split3(x)
    return _dot(m01, a) + _dot(m01, b) + _dot(m01, c)


def _mm(a, b, *, ta=False, tb=False, add=None, out_dtype=F32, name, tm=1024, tn=1024, tk=1024):
    m, k = (a.shape[1], a.shape[0]) if ta else a.shape
    n = b.shape[0] if tb else b.shape[1]
    assert (b.shape[1] if tb else b.shape[0]) == k
    tm, tn, tk = _tile(m, tm), _tile(n, tn), _tile(k, tk)
    nk = k // tk
    has_add = add is not None

    def body(*refs):
        if has_add:
            a_ref, b_ref, c_ref, o_ref, acc = refs
        else:
            a_ref, b_ref, o_ref, acc = refs
        kk = pl.program_id(2)
        part = _dot(a_ref[...].astype(BF16), b_ref[...].astype(BF16), 0 if ta else 1, 1 if tb else 0)

        def finish(r):
            if has_add:
                r = r + c_ref[...]
            o_ref[...] = r.astype(o_ref.dtype)

        if nk == 1:
            finish(part)
            return

        @pl.when(kk == 0)
        def _():
            acc[...] = part

        @pl.when((kk > 0) & (kk < nk - 1))
        def _():
            acc[...] += part

        @pl.when(kk == nk - 1)
        def _():
            finish(acc[...] + part)

    a_spec = pl.BlockSpec((tk, tm), lambda i, j, kk: (kk, i)) if ta else pl.BlockSpec((tm, tk), lambda i, j, kk: (i, kk))
    b_spec = pl.BlockSpec((tn, tk), lambda i, j, kk: (j, kk)) if tb else pl.BlockSpec((tk, tn), lambda i, j, kk: (kk, j))
    in_specs = [a_spec, b_spec]
    args = [a, b]
    if has_add:
        in_specs.append(pl.BlockSpec((tm, tn), lambda i, j, kk: (i, j)))
        args.append(add)
    return pl.pallas_call(
        body, name=name, grid=(m // tm, n // tn, nk), in_specs=in_specs,
        out_specs=pl.BlockSpec((tm, tn), lambda i, j, kk: (i, j)),
        out_shape=jax.ShapeDtypeStruct((m, n), out_dtype),
        scratch_shapes=[pltpu.VMEM((tm, tn) if nk > 1 else (8, LANE), F32)],
        compiler_params=_cp("parallel", "parallel", "arbitrary"),
    )(*args)


ROWS = 512


def _row_spec(tm, d):
    return pl.BlockSpec((tm, d), lambda i: (i, 0))


def _vec_spec(d, rows=1):
    return pl.BlockSpec((rows, d), lambda i: (0, 0))


def _modulate(x, scale, shift, name):
    s, d = x.shape
    tm = min(ROWS, s)

    def body(x_ref, sc_ref, sh_ref, o_ref):
        o_ref[...] = (x_ref[...] * (1.0 + sc_ref[...]) + sh_ref[...]).astype(BF16)

    return pl.pallas_call(
        body, name=name, grid=(s // tm,), in_specs=[_row_spec(tm, d), _vec_spec(d), _vec_spec(d)],
        out_specs=_row_spec(tm, d), out_shape=jax.ShapeDtypeStruct((s, d), BF16), compiler_params=_cp("parallel"),
    )(x, scale, shift)


def _resid_ln(x, y, gate, g, b, name):
    s, d = x.shape
    tm = min(ROWS, s)

    def body(x_ref, y_ref, gt_ref, g_ref, b_ref, o_ref):
        u = DEEPNORM_ALPHA * x_ref[...] + gt_ref[...] * y_ref[...]
        mu = jnp.mean(u, axis=1, keepdims=True)
        uc = u - mu
        var = jnp.mean(uc * uc, axis=1, keepdims=True)
        o_ref[...] = uc * lax.rsqrt(var + LN_EPS) * g_ref[...] + b_ref[...]

    return pl.pallas_call(
        body, name=name, grid=(s // tm,),
        in_specs=[_row_spec(tm, d), _row_spec(tm, d), _vec_spec(d), _vec_spec(d), _vec_spec(d)],
        out_specs=_row_spec(tm, d), out_shape=jax.ShapeDtypeStruct((s, d), F32), compiler_params=_cp("parallel"),
    )(x, y, gate, g, b)


def _resid_ln_bwd(x, y, dxn, gate, g, name):
    s, d = x.shape
    tm = min(ROWS, s)

    def body(x_ref, y_ref, dxn_ref, gt_ref, g_ref, du_ref, dy_ref, red_ref):
        @pl.when(pl.program_id(0) == 0)
        def _():
            red_ref[...] = jnp.zeros_like(red_ref)

        yv = y_ref[...]
        u = DEEPNORM_ALPHA * x_ref[...] + gt_ref[...] * yv
        mu = jnp.mean(u, axis=1, keepdims=True)
        uc = u - mu
        var = jnp.mean(uc * uc, axis=1, keepdims=True)
        rstd = lax.rsqrt(var + LN_EPS)
        xhat = uc * rstd
        dxnv = dxn_ref[...]
        dxh = dxnv * g_ref[...]
        du = rstd * (dxh - jnp.mean(dxh, axis=1, keepdims=True) - xhat * jnp.mean(dxh * xhat, axis=1, keepdims=True))
        du_ref[...] = du
        dy_ref[...] = (du * gt_ref[...]).astype(BF16)
        red_ref[0:1, :] += jnp.sum(du * yv, axis=0, keepdims=True)
        red_ref[1:2, :] += jnp.sum(dxnv * xhat, axis=0, keepdims=True)
        red_ref[2:3, :] += jnp.sum(dxnv, axis=0, keepdims=True)

    return pl.pallas_call(
        body, name=name, grid=(s // tm,),
        in_specs=[_row_spec(tm, d), _row_spec(tm, d), _row_spec(tm, d), _vec_spec(d), _vec_spec(d)],
        out_specs=[_row_spec(tm, d), _row_spec(tm, d), _vec_spec(d, 8)],
        out_shape=[jax.ShapeDtypeStruct((s, d), F32), jax.ShapeDtypeStruct((s, d), BF16), jax.ShapeDtypeStruct((8, d), F32)],
        compiler_params=_cp("arbitrary"),
    )(x, y, dxn, gate, g)


def _modulate_bwd(du, dh, x, scale, name):
    s, d = x.shape
    tm = min(ROWS, s)

    def body(du_ref, dh_ref, x_ref, sc_ref, dx_ref, red_ref):
        @pl.when(pl.program_id(0) == 0)
        def _():
            red_ref[...] = jnp.zeros_like(red_ref)

        dhv = dh_ref[...]
        dx_ref[...] = DEEPNORM_ALPHA * du_ref[...] + dhv * (1.0 + sc_ref[...])
        red_ref[0:1, :] += jnp.sum(dhv * x_ref[...], axis=0, keepdims=True)
        red_ref[1:2, :] += jnp.sum(dhv, axis=0, keepdims=True)

    return pl.pallas_call(
        body, name=name, grid=(s // tm,),
        in_specs=[_row_spec(tm, d), _row_spec(tm, d), _row_spec(tm, d), _vec_spec(d)],
        out_specs=[_row_spec(tm, d), _vec_spec(d, 8)],
        out_shape=[jax.ShapeDtypeStruct((s, d), F32), jax.ShapeDtypeStruct((8, d), F32)],
        compiler_params=_cp("arbitrary"),
    )(du, dh, x, scale)


def _loss_grad(xf, tgt, name):
    s, d = xf.shape
    tm = min(ROWS, s)

    def body(x_ref, t_ref, dx_ref, red_ref):
        @pl.when(pl.program_id(0) == 0)
        def _():
            red_ref[...] = jnp.zeros_like(red_ref)

        e = x_ref[...] - t_ref[...]
        dx_ref[...] = e * (1.0 / d)
        red_ref[0:1, :] += jnp.sum(e * e, axis=0, keepdims=True)

    return pl.pallas_call(
        body, name=name, grid=(s // tm,), in_specs=[_row_spec(tm, d), _row_spec(tm, d)],
        out_specs=[_row_spec(tm, d), _vec_spec(d, 8)],
        out_shape=[jax.ShapeDtypeStruct((s, d), F32), jax.ShapeDtypeStruct((8, d), F32)],
        compiler_params=_cp("arbitrary"),
    )(xf, tgt)


QKV_COLS = 3 * 3 * A_WIDTH


SLOPES = tuple(float(2.0 ** (-8.0 * (h + 1.0) / A_HEADS)) for h in range(A_HEADS))
FAR = 1e30
HEAD_COLS = tuple(slice(h * A_HEAD_DIM, (h + 1) * A_HEAD_DIM) for h in range(A_HEADS))


def _band_dist(n, length, dil, span_rows):
    shape = (2 * A_QBLOCK, A_QBLOCK) if span_rows else (A_QBLOCK, 2 * A_QBLOCK)
    r = lax.broadcasted_iota(jnp.int32, shape, 0)
    c = lax.broadcasted_iota(jnp.int32, shape, 1)
    sp, ce = (r, c) if span_rows else (c, r)
    delta = sp - A_RADIUS - ce
    pos = n * A_QBLOCK - A_RADIUS + sp
    valid = (jnp.abs(delta) <= A_RADIUS) & (pos >= 0) & (pos < length)
    return jnp.where(valid, jnp.abs(delta).astype(F32) * float(dil), FAR)


def _span_specs(colblock, nb64):
    def mk(i):
        return pl.BlockSpec((64, A_WIDTH), lambda r, n: (jnp.clip(2 * n - 1 + i, 0, nb64 - 1), colblock(r)))
    return [mk(i) for i in range(4)]


def _cat(refs):
    return jnp.concatenate([t[...] for t in refs], axis=0)


def _head_expander():
    r = lax.broadcasted_iota(jnp.int32, (A_HEADS, A_WIDTH), 0)
    c = lax.broadcasted_iota(jnp.int32, (A_HEADS, A_WIDTH), 1)
    return ((c >= r * A_HEAD_DIM) & (c < (r + 1) * A_HEAD_DIM)).astype(BF16)


def _to_lanes(x16, e):
    a, b, c = _split3(x16)
    return _dot(a, e) + _dot(b, e) + _dot(c, e)


def _per_head_sum(x, e):
    a, b, c = _split3(x)
    return _dot(a, e, 1, 1) + _dot(b, e, 1, 1) + _dot(c, e, 1, 1)


def _residue_major(t16, dil):
    return t16.reshape(-1, dil, A_HEADS).transpose(1, 0, 2)


def _attn_fwd(qkv, g, name):
    s = qkv.shape[0]
    dil = DILATIONS[g]
    length = s // dil
    nblk = length // A_QBLOCK
    cpr = QKV_COLS // A_WIDTH
    base = g * 3
    view = qkv.reshape(length, dil * QKV_COLS)

    def body(q_ref, k0, k1, k2, k3, v0, v1, v2, v3, o_ref, l_ref):
        dist = _band_dist(pl.program_id(1), length, dil, False)
        kk = _cat((k0, k1, k2, k3))
        vv = _cat((v0, v1, v2, v3))
        for h, cs in enumerate(HEAD_COLS):
            sc = _dot(q_ref[:, cs], kk[:, cs], 1, 1) * 0.125 - SLOPES[h] * dist
            m = jnp.max(sc, axis=1, keepdims=True)
            p = jnp.exp(sc - m)
            z = jnp.sum(p, axis=1, keepdims=True)
            o_ref[:, cs] = _dot(p.astype(BF16), vv[:, cs]) / z
            l_ref[:, h:h + 1] = m + jnp.log(z)

    qspec = pl.BlockSpec((A_QBLOCK, A_WIDTH), lambda r, n: (n, r * cpr + base))
    kspecs = _span_specs(lambda r: r * cpr + base + 1, 2 * nblk)
    vspecs = _span_specs(lambda r: r * cpr + base + 2, 2 * nblk)
    ospec = pl.BlockSpec((A_QBLOCK, A_WIDTH), lambda r, n: (n, r))
    o, l = pl.pallas_call(
        body, name=name, grid=(dil, nblk), in_specs=[qspec] + kspecs + vspecs,
        out_specs=[ospec, pl.BlockSpec((None, A_QBLOCK, A_HEADS), lambda r, n: (r, n, 0))],
        out_shape=[jax.ShapeDtypeStruct((length, dil * A_WIDTH), F32), jax.ShapeDtypeStruct((dil, length, A_HEADS), F32)],
        compiler_params=_cp("parallel", "parallel"),
    )(view, *([view] * 8))
    return o.reshape(s, A_WIDTH), l.transpose(1, 0, 2).reshape(s, A_HEADS)


def _attn_merge(os_, ls_, gate, name):
    s, w = gate.shape
    tm = min(ROWS, s)

    def body(o0, o1, o2, l0, l1, l2, g_ref, y_ref, o_ref, l_ref):
        a, b, c = l0[...], l1[...], l2[...]
        m = jnp.maximum(jnp.maximum(a, b), c)
        ea, eb, ec = jnp.exp(a - m), jnp.exp(b - m), jnp.exp(c - m)
        z = ea + eb + ec
        l_ref[...] = m + jnp.log(z)
        e = _head_expander()
        o = _to_lanes(ea / z, e) * o0[...] + _to_lanes(eb / z, e) * o1[...] + _to_lanes(ec / z, e) * o2[...]
        o_ref[...] = o
        y_ref[...] = (o * _silu(g_ref[...])).astype(BF16)

    rs = _row_spec(tm, w)
    ls = _row_spec(tm, A_HEADS)
    return pl.pallas_call(
        body, name=name, grid=(s // tm,), in_specs=[rs] * 3 + [ls] * 3 + [rs], out_specs=[rs, rs, ls],
        out_shape=[jax.ShapeDtypeStruct((s, w), BF16), jax.ShapeDtypeStruct((s, w), F32), jax.ShapeDtypeStruct((s, A_HEADS), F32)],
        compiler_params=_cp("parallel"),
    )(*os_, *ls_, gate)


def _attn_gate_bwd(dyy, o, gate, name):
    s, w = gate.shape
    tm = min(ROWS, s)

    def body(dy_ref, o_ref, g_ref, do_ref, dg_ref, dl_ref):
        dyv, ov, gv = dy_ref[...], o_ref[...], g_ref[...]
        do = dyv * _silu(gv)
        do_ref[...] = do.astype(BF16)
        dg_ref[...] = (dyv * ov * _dsilu(gv)).astype(BF16)
        dl_ref[...] = _per_head_sum(do * ov, _head_expander())

    rs = _row_spec(tm, w)
    return pl.pallas_call(
        body, name=name, grid=(s // tm,), in_specs=[rs] * 3, out_specs=[rs, rs, _row_spec(tm, A_HEADS)],
        out_shape=[jax.ShapeDtypeStruct((s, w), BF16), jax.ShapeDtypeStruct((s, w), BF16), jax.ShapeDtypeStruct((s, A_HEADS), F32)],
        compiler_params=_cp("parallel"),
    )(dyy, o, gate)


def _attn_dq(qkv, do, lse, delta, g, name):
    s = qkv.shape[0]
    dil = DILATIONS[g]
    length = s // dil
    nblk = length // A_QBLOCK
    cpr = QKV_COLS // A_WIDTH
    base = g * 3
    view = qkv.reshape(length, dil * QKV_COLS)

    def body(q_ref, k0, k1, k2, k3, v0, v1, v2, v3, do_ref, l_ref, dl_ref, dq_ref):
        dist = _band_dist(pl.program_id(1), length, dil, False)
        kk = _cat((k0, k1, k2, k3))
        vv = _cat((v0, v1, v2, v3))
        for h, cs in enumerate(HEAD_COLS):
            sc = _dot(q_ref[:, cs], kk[:, cs], 1, 1) * 0.125 - SLOPES[h] * dist
            p = jnp.exp(sc - l_ref[:, h:h + 1])
            dp = _dot(do_ref[:, cs], vv[:, cs], 1, 1)
            ds = p * (dp - dl_ref[:, h:h + 1])
            dq_ref[:, cs] = (_dot(ds.astype(BF16), kk[:, cs]) * 0.125).astype(BF16)

    qspec = pl.BlockSpec((A_QBLOCK, A_WIDTH), lambda r, n: (n, r * cpr + base))
    kspecs = _span_specs(lambda r: r * cpr + base + 1, 2 * nblk)
    vspecs = _span_specs(lambda r: r * cpr + base + 2, 2 * nblk)
    ospec = pl.BlockSpec((A_QBLOCK, A_WIDTH), lambda r, n: (n, r))
    cspec = pl.BlockSpec((None, A_QBLOCK, A_HEADS), lambda r, n: (r, n, 0))
    dq = pl.pallas_call(
        body, name=name, grid=(dil, nblk), in_specs=[qspec] + kspecs + vspecs + [ospec, cspec, cspec],
        out_specs=ospec, out_shape=jax.ShapeDtypeStruct((length, dil * A_WIDTH), BF16),
        compiler_params=_cp("parallel", "parallel"),
    )(view, *([view] * 8), do.reshape(length, dil * A_WIDTH), _residue_major(lse, dil), _residue_major(delta, dil))
    return dq.reshape(s, A_WIDTH)


def _attn_dkv(qkv, do, lse, delta, g, name):
    s = qkv.shape[0]
    dil = DILATIONS[g]
    length = s // dil
    nblk = length // A_QBLOCK
    cpr = QKV_COLS // A_WIDTH
    base = g * 3
    view = qkv.reshape(length, dil * QKV_COLS)
    wide = lambda t: t.reshape(length, dil * A_WIDTH)

    def by_residue(t16):
        return jnp.pad(t16.reshape(length, dil, A_HEADS).transpose(1, 2, 0), ((0, 0), (0, 0), (A_RADIUS, A_RADIUS)))

    def body(k_ref, v_ref, q0, q1, q2, q3, d0, d1, d2, d3, la, lb, ea, eb, dk_ref, dv_ref):
        dist = _band_dist(pl.program_id(1), length, dil, False)
        qq = _cat((q0, q1, q2, q3))
        dd = _cat((d0, d1, d2, d3))
        lse = jnp.concatenate([la[...], lb[...]], axis=1)
        dlt = jnp.concatenate([ea[...], eb[...]], axis=1)
        for h, cs in enumerate(HEAD_COLS):
            sc = _dot(k_ref[:, cs], qq[:, cs], 1, 1) * 0.125 - SLOPES[h] * dist
            p = jnp.exp(sc - lse[h:h + 1, :])
            dv_ref[:, cs] = _dot(p.astype(BF16), dd[:, cs]).astype(BF16)
            dp = _dot(v_ref[:, cs], dd[:, cs], 1, 1)
            ds = p * (dp - dlt[h:h + 1, :])
            dk_ref[:, cs] = (_dot(ds.astype(BF16), qq[:, cs]) * 0.125).astype(BF16)

    kspec = pl.BlockSpec((A_QBLOCK, A_WIDTH), lambda r, n: (n, r * cpr + base + 1))
    vspec = pl.BlockSpec((A_QBLOCK, A_WIDTH), lambda r, n: (n, r * cpr + base + 2))
    qspecs = _span_specs(lambda r: r * cpr + base, 2 * nblk)
    wspecs = _span_specs(lambda r: r, 2 * nblk)
    rspecs = [pl.BlockSpec((None, A_HEADS, A_QBLOCK), lambda r, n: (r, 0, n)), pl.BlockSpec((None, A_HEADS, A_QBLOCK), lambda r, n: (r, 0, n + 1))]
    ospec = pl.BlockSpec((A_QBLOCK, A_WIDTH), lambda r, n: (n, r))
    lse_r, dlt_r = by_residue(lse), by_residue(delta)
    dk, dv = pl.pallas_call(
        body, name=name, grid=(dil, nblk), in_specs=[kspec, vspec] + qspecs + wspecs + rspecs * 2,
        out_specs=[ospec, ospec], out_shape=[jax.ShapeDtypeStruct((length, dil * A_WIDTH), BF16)] * 2,
        compiler_params=_cp("parallel", "parallel"),
    )(view, view, *([view] * 4), *([wide(do)] * 4), lse_r, lse_r, dlt_r, dlt_r)
    return dk.reshape(s, A_WIDTH), dv.reshape(s, A_WIDTH)


def _attn_layer_fwd(h, w_qkv, w_gate, w_out, li):
    nm = lambda t: f"a{li}_{t}"
    qkv = _mm(h, w_qkv, out_dtype=BF16, name=nm("qkv"))
    gate = _mm(h, w_gate, name=nm("gate"))
    os_, ls_ = [], []
    for g in range(3):
        o, l = _attn_fwd(qkv, g, nm(f"attn{g}"))
        os_.append(o)
        ls_.append(l)
    y, o, lse = _attn_merge(os_, ls_, gate, nm("merge"))
    out = _mm(y, w_out, name=nm("out"))
    return out, (qkv, gate, y, o, lse)


def _attn_layer_bwd(dy, h, saved, w_qkv, w_gate, w_out, li):
    nm = lambda t: f"a{li}_{t}"
    qkv, gate, y, o, lse = saved
    g_w_out = _mm(y, dy, ta=True, out_dtype=BF16, name=nm("dwout"))
    dyy = _mm(dy, w_out, tb=True, name=nm("dyy"))
    do, dgate, delta = _attn_gate_bwd(dyy, o, gate, nm("gatebwd"))
    parts = []
    for g in range(3):
        dq = _attn_dq(qkv, do, lse, delta, g, nm(f"dq{g}"))
        dk, dv = _attn_dkv(qkv, do, lse, delta, g, nm(f"dkv{g}"))
        parts += [dq, dk, dv]
    dqkv = jnp.concatenate(parts, axis=1)
    dh = _mm(dgate, w_gate, tb=True, name=nm("dh_gate"))
    dh = _mm(dqkv, w_qkv, tb=True, add=dh, name=nm("dh_qkv"))
    g_w_in = jnp.concatenate([_mm(h, dqkv, ta=True, out_dtype=BF16, name=nm("dwqkv")),
                              _mm(h, dgate, ta=True, out_dtype=BF16, name=nm("dwgate"))], axis=1)
    return dh, g_w_in, g_w_out


SSM_INNER = SSM_HEADS * SSM_HEAD_DIM
SSM_BC = SSM_GROUPS * SSM_STATE
SSM_CONV_DIM = SSM_INNER + 2 * SSM_BC
GW = SSM_REP * SSM_HEAD_DIM
T = SSM_CHUNK
HALO = 8


def _conv_specs(tm, tn, s):
    nb8 = s // HALO
    cur = pl.BlockSpec((tm, tn), lambda j, i: (i, j))
    prev = pl.BlockSpec((HALO, tn), lambda j, i: (jnp.maximum(i * (tm // HALO) - 1, 0), j))
    nxt = pl.BlockSpec((HALO, tn), lambda j, i: (jnp.minimum((i + 1) * (tm // HALO), nb8 - 1), j))
    return prev, cur, nxt


def _extend(prev_ref, cur_ref, nxt_ref, i, nrow):
    p = jnp.where(i == 0, 0.0, prev_ref[...])
    n = jnp.where(i == nrow - 1, 0.0, nxt_ref[...])
    return jnp.concatenate([p, cur_ref[...], n], axis=0)


def _shift_rows(ext, off, tm):
    rows = ext.shape[0]
    return pltpu.roll(ext, (-off) % rows, 0)[HALO:HALO + tm]


def _conv_fwd(xraw, w, b, name):
    s, cdim = xraw.shape
    tm, tn = min(256, s), 1024
    nrow = s // tm

    def body(p_ref, c_ref, n_ref, w_ref, b_ref, pre_ref, act_ref):
        ext = _extend(p_ref, c_ref, n_ref, pl.program_id(1), nrow)
        acc = jnp.broadcast_to(b_ref[...], (tm, tn))
        for k in range(SSM_CONV):
            acc = acc + w_ref[k:k + 1, :] * _shift_rows(ext, k - SSM_CONV // 2, tm)
        pre_ref[...] = acc
        act_ref[...] = _silu(acc)

    prev, cur, nxt = _conv_specs(tm, tn, s)
    return pl.pallas_call(
        body, name=name, grid=(cdim // tn, nrow),
        in_specs=[prev, cur, nxt, pl.BlockSpec((SSM_CONV, tn), lambda j, i: (0, j)), pl.BlockSpec((1, tn), lambda j, i: (0, j))],
        out_specs=[cur, cur], out_shape=[jax.ShapeDtypeStruct((s, cdim), F32)] * 2,
        compiler_params=_cp("parallel", "parallel"),
    )(xraw, xraw, xraw, w, b)


def _conv_bwd(dact, pre, xraw, w, name):
    s, cdim = xraw.shape
    tm, tn = min(256, s), 1024
    nrow = s // tm

    def body(dp, dc, dn, pp, pc, pn, xp, xc, xn, w_ref, dx_ref, red_ref):
        i = pl.program_id(1)

        @pl.when(i == 0)
        def _():
            red_ref[...] = jnp.zeros_like(red_ref)

        dpre = _extend(dp, dc, dn, i, nrow) * _dsilu(_extend(pp, pc, pn, i, nrow))
        xext = _extend(xp, xc, xn, i, nrow)
        dcur = dpre[HALO:HALO + tm]
        acc = jnp.zeros((tm, tn), F32)
        for k in range(SSM_CONV):
            off = k - SSM_CONV // 2
            acc = acc + w_ref[k:k + 1, :] * _shift_rows(dpre, -off, tm)
            red_ref[k:k + 1, :] += jnp.sum(dcur * _shift_rows(xext, off, tm), axis=0, keepdims=True)
        red_ref[SSM_CONV:SSM_CONV + 1, :] += jnp.sum(dcur, axis=0, keepdims=True)
        dx_ref[...] = acc.astype(BF16)

    prev, cur, nxt = _conv_specs(tm, tn, s)
    return pl.pallas_call(
        body, name=name, grid=(cdim // tn, nrow),
        in_specs=[prev, cur, nxt] * 3 + [pl.BlockSpec((SSM_CONV, tn), lambda j, i: (0, j))],
        out_specs=[cur, pl.BlockSpec((8, tn), lambda j, i: (0, j))],
        out_shape=[jax.ShapeDtypeStruct((s, cdim), BF16), jax.ShapeDtypeStruct((8, cdim), F32)],
        compiler_params=_cp("parallel", "arbitrary"),
    )(dact, dact, dact, pre, pre, pre, xraw, xraw, xraw, w)


def _tri(lower):
    r = lax.broadcasted_iota(jnp.int32, (T, T), 0)
    c = lax.broadcasted_iota(jnp.int32, (T, T), 1)
    return (r >= c) if lower else (r <= c)


def _softplus(x):
    return jnp.maximum(x, 0.0) + jnp.log(1.0 + jnp.exp(-jnp.abs(x)))


def _dt_prep(dt_raw, bias, a_log, name):
    s = dt_raw.shape[0]
    nc = s // T

    def body(r_ref, b_ref, a_ref, dt_ref, cum_ref, cumt_ref):
        dt = _softplus(r_ref[...] + b_ref[...])
        da = dt * (-jnp.exp(a_ref[...]))
        pre = _dot_exact(_tri(True).astype(BF16), da)
        suf = _dot_exact(_tri(False).astype(BF16), da)
        lane = lax.broadcasted_iota(jnp.int32, (T, LANE), 1)
        cum = jnp.where(lane < SSM_HEADS, pre, suf)
        dt_ref[...] = dt
        cum_ref[...] = cum
        cumt_ref[...] = cum.T

    blk = pl.BlockSpec((T, LANE), lambda c: (c, 0))
    vec = pl.BlockSpec((1, LANE), lambda c: (0, 0))
    return pl.pallas_call(
        body, name=name, grid=(nc,), in_specs=[blk, vec, vec],
        out_specs=[blk, blk, pl.BlockSpec((None, LANE, T), lambda c: (c, 0, 0))],
        out_shape=[jax.ShapeDtypeStruct((s, LANE), F32), jax.ShapeDtypeStruct((s, LANE), F32), jax.ShapeDtypeStruct((nc, LANE, T), F32)],
        compiler_params=_cp("parallel"),
    )(dt_raw, bias, a_log)


def _by_group(t):
    s = t.shape[0]
    return t[:, :2 * SSM_HEADS].reshape(s, 2 * SSM_GROUPS, SSM_REP).transpose(1, 0, 2)


def _from_group(tf, tb):
    s = tf.shape[1]
    t = jnp.concatenate([tf, tb], axis=0).transpose(1, 0, 2).reshape(s, 2 * SSM_HEADS)
    return jnp.pad(t, ((0, 0), (0, LANE - 2 * SSM_HEADS)))


def _decay_mats(acol, arow, rev):
    after = _tri(not rev)
    return jnp.where(after, jnp.exp(jnp.where(after, acol - arow, 0.0)), 0.0)


def _ssd_scan(xbc, dtk, cumk, cumtk, rev, name):
    s = xbc.shape[0]
    nc = s // T
    last = 0 if rev else T - 1
    kofs = SSM_GROUPS if rev else 0
    ci = (lambda c: nc - 1 - c) if rev else (lambda c: c)

    def body(x_ref, b_ref, c_ref, dt_ref, cum_ref, cumt_ref, y_ref, st_ref, state):
        @pl.when(pl.program_id(1) == 0)
        def _():
            state[...] = jnp.zeros_like(state)

        bm = b_ref[...]
        cm = c_ref[...].astype(BF16)
        cb = _dot(cm, bm.astype(BF16), 1, 1)
        bt = bm.T.astype(BF16)
        for r in range(SSM_REP):
            cs = slice(r * SSM_HEAD_DIM, (r + 1) * SSM_HEAD_DIM)
            acol = cum_ref[:, r:r + 1]
            arow = cumt_ref[r:r + 1, :]
            lm = _decay_mats(acol, arow, rev)
            u = x_ref[:, cs] * dt_ref[:, r:r + 1]
            st = state[r]
            st_ref[r] = st
            yd = _dot((cb * lm).astype(BF16), u.astype(BF16))
            yo = jnp.exp(acol) * _dot(cm, st.astype(BF16))
            y_ref[:, cs] = yd + yo
            tot = cum_ref[last:last + 1, r:r + 1]
            dec = jnp.exp(tot - acol)
            state[r] = jnp.exp(tot) * st + _dot(bt, (dec * u).astype(BF16))

    nxb = SSM_INNER // LANE
    return pl.pallas_call(
        body, name=name, grid=(SSM_GROUPS, nc),
        in_specs=[
            pl.BlockSpec((T, GW), lambda g, c: (ci(c), g)),
            pl.BlockSpec((T, LANE), lambda g, c: (ci(c), nxb + g)),
            pl.BlockSpec((T, LANE), lambda g, c: (ci(c), nxb + SSM_GROUPS + g)),
            pl.BlockSpec((None, T, SSM_REP), lambda g, c: (kofs + g, ci(c), 0)),
            pl.BlockSpec((None, T, SSM_REP), lambda g, c: (kofs + g, ci(c), 0)),
            pl.BlockSpec((None, None, SSM_REP, T), lambda g, c: (kofs + g, ci(c), 0, 0)),
        ],
        out_specs=[
            pl.BlockSpec((T, GW), lambda g, c: (ci(c), g)),
            pl.BlockSpec((None, SSM_REP, SSM_STATE, SSM_HEAD_DIM), lambda g, c: (ci(c), g, 0, 0)),
        ],
        out_shape=[jax.ShapeDtypeStruct((s, SSM_INNER), F32), jax.ShapeDtypeStruct((nc, SSM_HEADS, SSM_STATE, SSM_HEAD_DIM), F32)],
        scratch_shapes=[pltpu.VMEM((SSM_REP, SSM_STATE, SSM_HEAD_DIM), F32)],
        compiler_params=_cp("parallel", "arbitrary"),
    )(xbc, xbc, xbc, dtk, cumk, cumtk)


def _ssd_scan_bwd(xbc, dtk, cumk, cumtk, dy, states, dvec, prev, rev, name):
    s = xbc.shape[0]
    nc = s // T
    last = 0 if rev else T - 1
    kofs = SSM_GROUPS if rev else 0
    ci = (lambda c: c) if rev else (lambda c: nc - 1 - c)
    has_prev = prev is not None

    def body(*refs):
        x_ref, b_ref, c_ref, dt_ref, cum_ref, cumt_ref, dy_ref, st_ref, dv_ref = refs[:9]
        refs = refs[9:]
        if has_prev:
            pdx, pdb, pdc = refs[:3]
            refs = refs[3:]
        dx_ref, db_ref, dc_ref, ddt_ref, dda_ref, dstate, rs_buf, in_buf, k_buf = refs

        @pl.when(pl.program_id(1) == 0)
        def _():
            dstate[...] = jnp.zeros_like(dstate)

        rs_buf[...] = jnp.zeros_like(rs_buf)
        in_buf[...] = jnp.zeros_like(in_buf)
        k_buf[...] = jnp.zeros_like(k_buf)
        bm = b_ref[...].astype(BF16)
        cm = c_ref[...].astype(BF16)
        cbt = _dot(bm, cm, 1, 1)
        cb = _dot(cm, bm, 1, 1)
        ct = c_ref[...].T.astype(BF16)
        after = _tri(not rev)
        before = _tri(rev)
        from_k = before.astype(BF16)
        ri = lax.broadcasted_iota(jnp.int32, (T, T), 0)
        cj = lax.broadcasted_iota(jnp.int32, (T, T), 1)
        strictly_before = (cj > ri) if rev else (cj < ri)
        dcb = jnp.zeros((T, T), F32)
        dc_acc = jnp.zeros((T, SSM_STATE), F32)
        db_acc = jnp.zeros((T, SSM_STATE), F32)
        for r in range(SSM_REP):
            cs = slice(r * SSM_HEAD_DIM, (r + 1) * SSM_HEAD_DIM)
            acol = cum_ref[:, r:r + 1]
            arow = cumt_ref[r:r + 1, :]
            lm = jnp.where(after, jnp.exp(jnp.where(after, acol - arow, 0.0)), 0.0)
            lmt = jnp.where(before, jnp.exp(jnp.where(before, arow - acol, 0.0)), 0.0)
            dtc = dt_ref[:, r:r + 1]
            xh = x_ref[:, cs]
            u = xh * dtc
            ub = u.astype(BF16)
            dyh = dy_ref[:, cs]
            dyb = dyh.astype(BF16)
            st = st_ref[r]
            dst = dstate[r]
            tot = cum_ref[last:last + 1, r:r + 1]
            dec = jnp.exp(tot - acol)
            eac = jnp.exp(acol)
            du_off = dec * _dot(bm, dst.astype(BF16))
            du = _dot((cbt * lmt).astype(BF16), dyb) + du_off
            gl = _dot(dyb, ub, 1, 1) * lm
            dcb = dcb + gl
            dc_acc = dc_acc + eac * _dot(dyb, st.astype(BF16), 1, 1)
            db_acc = db_acc + dec * _dot(ub, dst.astype(BF16), 1, 1)
            crossing = _dot(from_k, (gl * cb).astype(BF16))
            in_buf[:, r:r + 1] = jnp.sum(jnp.where(strictly_before, crossing, 0.0), axis=1, keepdims=True)
            y_off = eac * _dot(cm, st.astype(BF16))
            udu = u * du_off
            rs_buf[:, r:r + 1] = jnp.sum(dyh * y_off - udu, axis=1, keepdims=True)
            kk = jnp.sum(dst * (jnp.exp(tot) * st) + udu, keepdims=True)
            k_buf[0:1, r:r + 1] = kk
            ddt_ref[:, r:r + 1] = jnp.sum(du * xh, axis=1, keepdims=True)
            dx = du * dtc
            if has_prev:
                dx = dx + pdx[:, cs]
            else:
                dx = dx + dyh * dv_ref[:, cs]
            dx_ref[:, cs] = dx
            dstate[r] = jnp.exp(tot) * dst + _dot(ct, (eac * dyh).astype(BF16))
        dda = in_buf[...] + _dot_exact(from_k, rs_buf[...]) + k_buf[0:1, :]
        dda_ref[...] = dda[:, :SSM_REP]
        dcbb = dcb.astype(BF16)
        dc = dc_acc + _dot(dcbb, bm)
        db = db_acc + _dot(dcbb, cm, 0, 0)
        if has_prev:
            dc = dc + pdc[...]
            db = db + pdb[...]
        dc_ref[...] = dc
        db_ref[...] = db

    nxb = SSM_INNER // LANE
    xspec = pl.BlockSpec((T, GW), lambda g, c: (ci(c), g))
    gspec = pl.BlockSpec((T, LANE), lambda g, c: (ci(c), g))
    kspec = pl.BlockSpec((None, T, SSM_REP), lambda g, c: (kofs + g, ci(c), 0))
    in_specs = [
        xspec,
        pl.BlockSpec((T, LANE), lambda g, c: (ci(c), nxb + g)),
        pl.BlockSpec((T, LANE), lambda g, c: (ci(c), nxb + SSM_GROUPS + g)),
        kspec, kspec,
        pl.BlockSpec((None, None, SSM_REP, T), lambda g, c: (kofs + g, ci(c), 0, 0)),
        xspec,
        pl.BlockSpec((None, SSM_REP, SSM_STATE, SSM_HEAD_DIM), lambda g, c: (ci(c), g, 0, 0)),
        pl.BlockSpec((1, GW), lambda g, c: (0, g)),
    ]
    args = [xbc, xbc, xbc, dtk, cumk, cumtk, dy, states, dvec]
    if has_prev:
        in_specs += [xspec, gspec, gspec]
        args += list(prev)
    ospec8 = pl.BlockSpec((None, T, SSM_REP), lambda g, c: (g, ci(c), 0))
    return pl.pallas_call(
        body, name=name, grid=(SSM_GROUPS, nc), in_specs=in_specs,
        out_specs=[xspec, gspec, gspec, ospec8, ospec8],
        out_shape=[jax.ShapeDtypeStruct((s, SSM_INNER), F32), jax.ShapeDtypeStruct((s, SSM_BC), F32), jax.ShapeDtypeStruct((s, SSM_BC), F32),
                   jax.ShapeDtypeStruct((SSM_GROUPS, s, SSM_REP), F32), jax.ShapeDtypeStruct((SSM_GROUPS, s, SSM_REP), F32)],
        scratch_shapes=[pltpu.VMEM((SSM_REP, SSM_STATE, SSM_HEAD_DIM), F32), pltpu.VMEM((T, LANE), F32), pltpu.VMEM((T, LANE), F32),
                        pltpu.VMEM((8, LANE), F32)],
        compiler_params=_cp("parallel", "arbitrary"),
    )(*args)


def _ssd_post(yf, yb, xbc, z, dvec, nw, name):
    s = z.shape[0]
    tm = min(256, s)

    def body(yf_ref, yb_ref, x_ref, z_ref, dv_ref, nw_ref, o_ref):
        ys = yf_ref[...] + yb_ref[...] + dv_ref[...] * x_ref[...]
        yg = ys * _silu(z_ref[...])
        ms = jnp.mean(yg * yg, axis=1, keepdims=True)
        o_ref[...] = (yg * lax.rsqrt(ms + RMS_EPS) * nw_ref[...]).astype(BF16)

    rs = _row_spec(tm, SSM_INNER)
    vs = _vec_spec(SSM_INNER)
    return pl.pallas_call(
        body, name=name, grid=(s // tm,), in_specs=[rs, rs, rs, rs, vs, vs], out_specs=rs,
        out_shape=jax.ShapeDtypeStruct((s, SSM_INNER), BF16), compiler_params=_cp("parallel"),
    )(yf, yb, xbc, z, dvec, nw)


def _ssd_post_bwd(dyn, yf, yb, xbc, z, dvec, nw, name):
    s = z.shape[0]
    tm = min(256, s)

    def body(dyn_ref, yf_ref, yb_ref, x_ref, z_ref, dv_ref, nw_ref, dys_ref, dz_ref, red_ref):
        @pl.when(pl.program_id(0) == 0)
        def _():
            red_ref[...] = jnp.zeros_like(red_ref)

        xv, zv = x_ref[...], z_ref[...]
        ys = yf_ref[...] + yb_ref[...] + dv_ref[...] * xv
        sz = _silu(zv)
        yg = ys * sz
        rstd = lax.rsqrt(jnp.mean(yg * yg, axis=1, keepdims=True) + RMS_EPS)
        yhat = yg * rstd
        dynv = dyn_ref[...]
        dyh = dynv * nw_ref[...]
        dyg = rstd * (dyh - yhat * jnp.mean(dyh * yhat, axis=1, keepdims=True))
        dys = dyg * sz
        dys_ref[...] = dys
        dz_ref[...] = (dyg * ys * _dsilu(zv)).astype(BF16)
        red_ref[0:1, :] += jnp.sum(dynv * yhat, axis=0, keepdims=True)
        red_ref[1:2, :] += jnp.sum(dys * xv, axis=0, keepdims=True)

    rs = _row_spec(tm, SSM_INNER)
    vs = _vec_spec(SSM_INNER)
    return pl.pallas_call(
        body, name=name, grid=(s // tm,), in_specs=[rs, rs, rs, rs, rs, vs, vs],
        out_specs=[rs, rs, _vec_spec(SSM_INNER, 8)],
        out_shape=[jax.ShapeDtypeStruct((s, SSM_INNER), F32), jax.ShapeDtypeStruct((s, SSM_INNER), BF16), jax.ShapeDtypeStruct((8, SSM_INNER), F32)],
        compiler_params=_cp("arbitrary"),
    )(dyn, yf, yb, xbc, z, dvec, nw)


def _dt_bwd(dt_raw, bias, a_log, dt, ddt, dda, name):
    s = dt_raw.shape[0]
    tm = min(1024, s)

    def body(r_ref, b_ref, a_ref, dt_ref, ddt_ref, dda_ref, o_ref, red_ref):
        @pl.when(pl.program_id(0) == 0)
        def _():
            red_ref[...] = jnp.zeros_like(red_ref)

        a = -jnp.exp(a_ref[...])
        ddav = dda_ref[...]
        draw = (ddt_ref[...] + a * ddav) * _sigmoid(r_ref[...] + b_ref[...])
        o_ref[...] = draw.astype(BF16)
        red_ref[0:1, :] += jnp.sum(draw, axis=0, keepdims=True)
        red_ref[1:2, :] += a * jnp.sum(ddav * dt_ref[...], axis=0, keepdims=True)

    rs = _row_spec(tm, LANE)
    vs = _vec_spec(LANE)
    return pl.pallas_call(
        body, name=name, grid=(s // tm,), in_specs=[rs, vs, vs, rs, rs, rs], out_specs=[rs, _vec_spec(LANE, 8)],
        out_shape=[jax.ShapeDtypeStruct((s, LANE), BF16), jax.ShapeDtypeStruct((8, LANE), F32)],
        compiler_params=_cp("arbitrary"),
    )(dt_raw, bias, a_log, dt, ddt, dda)


def _pad_lanes(v):
    v = v.reshape(1, -1)
    return jnp.pad(v, ((0, 0), (0, LANE - v.shape[1])))


def _ssd_prep_weights(w_in, conv_w, conv_b, dt_bias, a_log, d_skip, norm_w, w_out):
    return dict(
        w_z=w_in[:, :SSM_INNER].astype(BF16),
        w_xbc=w_in[:, SSM_INNER:SSM_INNER + SSM_CONV_DIM].astype(BF16),
        w_dt=jnp.pad(w_in[:, SSM_INNER + SSM_CONV_DIM:], ((0, 0), (0, LANE - 2 * SSM_HEADS))).astype(BF16),
        conv_w=conv_w, conv_b=conv_b.reshape(1, -1), bias=_pad_lanes(dt_bias), a_log=_pad_lanes(a_log),
        dvec=jnp.repeat(d_skip, SSM_HEAD_DIM).reshape(1, -1), nw=norm_w.reshape(1, -1), w_out=w_out.astype(BF16),
    )


def _ssd_layer_fwd(h, w, li):
    nm = lambda t: f"b{li}_{t}"
    z = _mm(h, w["w_z"], name=nm("z"))
    xraw = _mm(h, w["w_xbc"], name=nm("xbc"))
    dt_raw = _mm(h, w["w_dt"], name=nm("dt"))
    pre, xbc = _conv_fwd(xraw, w["conv_w"], w["conv_b"], nm("conv"))
    dt, cum, cumt = _dt_prep(dt_raw, w["bias"], w["a_log"], nm("dtprep"))
    nc = cumt.shape[0]
    dtk, cumk = _by_group(dt), _by_group(cum)
    cumtk = cumt[:, :2 * SSM_HEADS].reshape(nc, 2 * SSM_GROUPS, SSM_REP, T).transpose(1, 0, 2, 3)
    yf, stf = _ssd_scan(xbc, dtk, cumk, cumtk, False, nm("scan_f"))
    yb, stb = _ssd_scan(xbc, dtk, cumk, cumtk, True, nm("scan_b"))
    yn = _ssd_post(yf, yb, xbc, z, w["dvec"], w["nw"], nm("post"))
    out = _mm(yn, w["w_out"], name=nm("out"))
    return out, (z, xraw, dt_raw, pre, xbc, dt, dtk, cumk, cumtk, yf, stf, yb, stb, yn)


def _ssd_layer_bwd(dy, h, saved, w, li):
    nm = lambda t: f"b{li}_{t}"
    z, xraw, dt_raw, pre, xbc, dt, dtk, cumk, cumtk, yf, stf, yb, stb, yn = saved
    g_w_out = _mm(yn, dy, ta=True, out_dtype=BF16, name=nm("dwout"))
    dyn = _mm(dy, w["w_out"], tb=True, name=nm("dyn"))
    dys, dz, pred = _ssd_post_bwd(dyn, yf, yb, xbc, z, w["dvec"], w["nw"], nm("postbwd"))
    dx1, db1, dc1, ddt_f, dda_f = _ssd_scan_bwd(xbc, dtk, cumk, cumtk, dys, stf, w["dvec"], None, False, nm("scanbwd_f"))
    dx, db, dc, ddt_b, dda_b = _ssd_scan_bwd(xbc, dtk, cumk, cumtk, dys, stb, w["dvec"], (dx1, db1, dc1), True, nm("scanbwd_b"))
    dact = jnp.concatenate([dx, db, dc], axis=1)
    dxraw, cred = _conv_bwd(dact, pre, xraw, w["conv_w"], nm("convbwd"))
    draw, dred = _dt_bwd(dt_raw, w["bias"], w["a_log"], dt, _from_group(ddt_f, ddt_b), _from_group(dda_f, dda_b), nm("dtbwd"))
    dh = _mm(dz, w["w_z"], tb=True, name=nm("dh_z"))
    dh = _mm(dxraw, w["w_xbc"], tb=True, add=dh, name=nm("dh_xbc"))
    dh = _mm(draw, w["w_dt"], tb=True, add=dh, name=nm("dh_dt"))
    g_w_in = jnp.concatenate([_mm(h, dz, ta=True, out_dtype=BF16, name=nm("dwz")), _mm(h, dxraw, ta=True, out_dtype=BF16, name=nm("dwxbc")),
                              _mm(h, draw, ta=True, out_dtype=BF16, name=nm("dwdt"))[:, :2 * SSM_HEADS]], axis=1)
    grads = (g_w_in, cred[:SSM_CONV], cred[SSM_CONV], dred[0, :2 * SSM_HEADS].reshape(2, SSM_HEADS),
             dred[1, :2 * SSM_HEADS].reshape(2, SSM_HEADS), pred[1].reshape(SSM_HEADS, SSM_HEAD_DIM).sum(axis=1), pred[0], g_w_out)
    return dh, grads


B_GRAD_NAMES = ("b_w_in", "b_conv_w", "b_conv_b", "b_dt_bias", "b_a_log", "b_d", "b_norm_w", "b_w_out")


def _local_step(x, tgt, mod, w):
    d = x.shape[1]
    qkv_cols = QKV_COLS
    layers = []
    for i in range(DEPTH):
        j = i // 2
        if i % 2 == 0:
            layers.append((w["a_w_in"][j][:, :qkv_cols].astype(BF16), w["a_w_in"][j][:, qkv_cols:].astype(BF16), w["a_w_out"][j].astype(BF16)))
        else:
            layers.append(_ssd_prep_weights(w["b_w_in"][j], w["b_conv_w"][j], w["b_conv_b"][j], w["b_dt_bias"][j], w["b_a_log"][j],
                                            w["b_d"][j], w["b_norm_w"][j], w["b_w_out"][j]))
    saved = []
    for i in range(DEPTH):
        shift, scale, gate = mod[i:i + 1, :d], mod[i:i + 1, d:2 * d], mod[i:i + 1, 2 * d:]
        h = _modulate(x, scale, shift, f"l{i}_mod")
        if i % 2 == 0:
            out, sv = _attn_layer_fwd(h, *layers[i], i)
        else:
            out, sv = _ssd_layer_fwd(h, layers[i], i)
        xn = _resid_ln(x, out, gate, w["ln_g"][i:i + 1], w["ln_b"][i:i + 1], f"l{i}_ln")
        saved.append((x, h, out, sv))
        x = xn
    dx, lred = _loss_grad(x, tgt, "loss")
    loss = 0.5 * jnp.sum(lred[0]) / d
    dmod, g_ln_g, g_ln_b = [None] * DEPTH, [None] * DEPTH, [None] * DEPTH
    ga_in, ga_out = [None, None], [None, None]
    gb = [None, None]
    for i in reversed(range(DEPTH)):
        j = i // 2
        xi, h, out, sv = saved[i]
        scale, gate = mod[i:i + 1, d:2 * d], mod[i:i + 1, 2 * d:]
        du, dy, red = _resid_ln_bwd(xi, out, dx, gate, w["ln_g"][i:i + 1], f"l{i}_lnbwd")
        g_ln_g[i], g_ln_b[i] = red[1], red[2]
        if i % 2 == 0:
            dh, ga_in[j], ga_out[j] = _attn_layer_bwd(dy, h, sv, *layers[i], i)
        else:
            dh, gb[j] = _ssd_layer_bwd(dy, h, sv, layers[i], i)
        dx, red2 = _modulate_bwd(du, dh, xi, scale, f"l{i}_modbwd")
        dmod[i] = jnp.concatenate([red2[1], red2[0], red[0]])
    grads = {"ln_g": jnp.stack(g_ln_g), "ln_b": jnp.stack(g_ln_b), "a_w_in": jnp.stack(ga_in), "a_w_out": jnp.stack(ga_out)}
    for k, n in enumerate(B_GRAD_NAMES):
        grads[n] = jnp.stack([gb[0][k], gb[1][k]])
    return loss, dx, jnp.stack(dmod), grads


MESH = pl.DeviceIdType.MESH
ANY = pl.BlockSpec(memory_space=pl.ANY)
N_DEV = 8
N_SHARD = 4


def _flip(v, bit):
    return 1 - v if bit else v


def _all_gather8(v, name):
    def body(v_ref, o_ref, send_sems, recv_sems, local_sem):
        x, y, c = lax.axis_index("x"), lax.axis_index("y"), lax.axis_index("c")
        me = 4 * x + 2 * y + c
        local = pltpu.make_async_copy(v_ref, o_ref.at[me], local_sem)
        local.start()
        copies = []
        for k in range(1, N_DEV):
            peer = (_flip(x, k & 4), _flip(y, k & 2), _flip(c, k & 1))
            copies.append(pltpu.make_async_remote_copy(
                src_ref=v_ref, dst_ref=o_ref.at[me], send_sem=send_sems.at[k - 1], recv_sem=recv_sems.at[k - 1],
                device_id=peer, device_id_type=MESH))
        for cp in copies:
            cp.start()
        for cp in copies:
            cp.wait()
        local.wait()

    return pl.pallas_call(
        body, name=name, in_specs=[ANY], out_specs=ANY, out_shape=jax.ShapeDtypeStruct((N_DEV,) + v.shape, v.dtype),
        scratch_shapes=[pltpu.SemaphoreType.DMA((N_DEV - 1,)), pltpu.SemaphoreType.DMA((N_DEV - 1,)), pltpu.SemaphoreType.DMA],
    )(v)


def _transpose_shards(src, name):
    def body(s_ref, o_ref, send_sems, recv_sems, local_sem):
        x, y, c = lax.axis_index("x"), lax.axis_index("y"), lax.axis_index("c")
        m = 2 * x + y
        local = pltpu.make_async_copy(s_ref.at[m], o_ref.at[m], local_sem)
        local.start()
        copies = []
        for k in range(1, N_SHARD):
            px, py = _flip(x, k & 2), _flip(y, k & 1)
            copies.append(pltpu.make_async_remote_copy(
                src_ref=s_ref.at[2 * px + py], dst_ref=o_ref.at[m], send_sem=send_sems.at[k - 1], recv_sem=recv_sems.at[k - 1],
                device_id=(px, py, c), device_id_type=MESH))
        for cp in copies:
            cp.start()
        for cp in copies:
            cp.wait()
        local.wait()

    return pl.pallas_call(
        body, name=name, in_specs=[ANY], out_specs=ANY, out_shape=jax.ShapeDtypeStruct(src.shape, src.dtype),
        scratch_shapes=[pltpu.SemaphoreType.DMA((N_SHARD - 1,)), pltpu.SemaphoreType.DMA((N_SHARD - 1,)), pltpu.SemaphoreType.DMA],
    )(src)


def _gather_shards(src, name):
    rows = src.shape[0]
    half = rows // 2
    n_ici = N_SHARD - 1

    def body(s_ref, o_ref, send_sems, recv_sems, local_sem):
        x, y, c = lax.axis_index("x"), lax.axis_index("y"), lax.axis_index("c")
        m = 2 * x + y
        sibling = (x, y, 1 - c)
        my_half = pl.ds(pl.multiple_of(c * half, 16), half)
        its_half = pl.ds(pl.multiple_of((1 - c) * half, 16), half)
        local = pltpu.make_async_copy(s_ref, o_ref.at[m], local_sem)
        local.start()
        chips = [(_flip(x, k & 2), _flip(y, k & 1)) for k in range(1, N_SHARD)]

        def copy(sem, src_ref, dst_ref, to):
            return pltpu.make_async_remote_copy(src_ref=src_ref, dst_ref=dst_ref, send_sem=send_sems.at[sem],
                                                recv_sem=recv_sems.at[sem], device_id=to, device_id_type=MESH)

        first = [copy(i, s_ref.at[my_half], o_ref.at[m, my_half], (px, py, c)) for i, (px, py) in enumerate(chips)]
        for cp in first:
            cp.start()
        passed = []
        for i, (px, py) in enumerate(chips):
            landed = o_ref.at[2 * px + py, my_half]
            copy(i, landed, landed, (px, py, c)).wait_recv()
            passed.append(copy(n_ici + i, landed, landed, sibling))
            passed[-1].start()
        for i, (px, py) in enumerate(chips):
            from_sibling = o_ref.at[2 * px + py, its_half]
            copy(n_ici + i, from_sibling, from_sibling, sibling).wait_recv()
        for cp in first + passed:
            cp.wait_send()
        local.wait()

    return pl.pallas_call(
        body, name=name, in_specs=[ANY], out_specs=ANY, out_shape=jax.ShapeDtypeStruct((N_SHARD,) + src.shape, src.dtype),
        scratch_shapes=[pltpu.SemaphoreType.DMA((2 * n_ici,)), pltpu.SemaphoreType.DMA((2 * n_ici,)), pltpu.SemaphoreType.DMA],
    )(src)


def _swap_sibling(v, name):
    def body(v_ref, o_ref, send_sem, recv_sem):
        x, y, c = lax.axis_index("x"), lax.axis_index("y"), lax.axis_index("c")
        cp = pltpu.make_async_remote_copy(src_ref=v_ref, dst_ref=o_ref, send_sem=send_sem, recv_sem=recv_sem,
                                          device_id=(x, y, 1 - c), device_id_type=MESH)
        cp.start()
        cp.wait()

    return pl.pallas_call(
        body, name=name, in_specs=[ANY], out_specs=ANY, out_shape=jax.ShapeDtypeStruct(v.shape, v.dtype),
        scratch_shapes=[pltpu.SemaphoreType.DMA, pltpu.SemaphoreType.DMA],
    )(v)


def _sum_slots(a, name):
    n, r, cdim = a.shape
    tm = max(t for t in range(16, 641, 16) if r % t == 0)

    def body(a_ref, o_ref):
        acc = a_ref[0].astype(F32)
        for k in range(1, n):
            acc = acc + a_ref[k].astype(F32)
        o_ref[...] = acc

    return pl.pallas_call(
        body, name=name, grid=(r // tm,), in_specs=[pl.BlockSpec((n, tm, cdim), lambda i: (0, i, 0))],
        out_specs=pl.BlockSpec((tm, cdim), lambda i: (i, 0)), out_shape=jax.ShapeDtypeStruct((r, cdim), F32),
        compiler_params=_cp("parallel"),
    )(a)


def _silu_rows(v, name):
    def body(v_ref, o_ref):
        o_ref[...] = _silu(v_ref[...])

    return pl.pallas_call(body, name=name, out_shape=jax.ShapeDtypeStruct(v.shape, F32))(v)


PACK_COLS = 1024
PACK_TILE = 256


def _adamw(w, g1, g2, m, v, name):
    r = w.shape[0]
    c1 = 1.0 / (1.0 - ADAM_B1 ** ADAM_STEP)
    c2 = 1.0 / (1.0 - ADAM_B2 ** ADAM_STEP)

    def body(w_ref, g1_ref, g2_ref, m_ref, v_ref, g_ref, d_ref, nm_ref, nv_ref):
        g = g1_ref[...] + g2_ref[...]
        mn = ADAM_B1 * m_ref[...] + (1.0 - ADAM_B1) * g
        vn = ADAM_B2 * v_ref[...] + (1.0 - ADAM_B2) * (g * g)
        g_ref[...] = g
        nm_ref[...] = mn
        nv_ref[...] = vn
        d_ref[...] = -ADAM_LR * ((mn * c1) / (jnp.sqrt(vn * c2) + ADAM_EPS) + ADAM_WD * w_ref[...])

    spec = pl.BlockSpec((PACK_TILE, PACK_COLS), lambda i: (i, 0))
    return pl.pallas_call(
        body, name=name, grid=(r // PACK_TILE,), in_specs=[spec] * 5, out_specs=[spec] * 4,
        out_shape=[jax.ShapeDtypeStruct(w.shape, F32)] * 4, compiler_params=_cp("parallel"),
    )(w, g1, g2, m, v)


def _rows(a):
    f = a.reshape(-1)
    pad = (-f.shape[0]) % PACK_COLS
    if pad:
        f = jnp.pad(f, (0, pad))
    return f.reshape(-1, PACK_COLS)


def _nrows(shape):
    return -(-int(np.prod(shape)) // PACK_COLS)


def _pack(parts, total_rows=None):
    p = jnp.concatenate([_rows(a) for a in parts], axis=0)
    if total_rows is not None and total_rows > p.shape[0]:
        p = jnp.pad(p, ((0, total_rows - p.shape[0]), (0, 0)))
    return p


def _unpack(p, shapes):
    out, r0 = [], 0
    for shp in shapes:
        n = int(np.prod(shp))
        nr = _nrows(shp)
        out.append(p[r0:r0 + nr].reshape(-1)[:n].reshape(shp))
        r0 += nr
    return out


def _unshard_cols(g):
    return jnp.concatenate([g[k] for k in range(N_SHARD)], axis=-1)


def _shard_cols(a):
    n = a.shape[-1] // N_SHARD
    return jnp.stack([a[..., k * n:(k + 1) * n] for k in range(N_SHARD)])


def _unshard_rows(g):
    return jnp.concatenate([g[k] for k in range(N_SHARD)], axis=1)


def _shard_rows(a):
    n = a.shape[1] // N_SHARD
    return jnp.stack([a[:, k * n:(k + 1) * n] for k in range(N_SHARD)])


W_NAMES = ("ada_w", "ada_b", "ln_g", "ln_b", "a_w_in", "a_w_out", "b_w_in", "b_conv_w", "b_conv_b", "b_dt_bias", "b_a_log", "b_d",
           "b_norm_w", "b_w_out")
BIG = ("a_w_in", "a_w_out", "b_w_in", "b_w_out")
PACK_ORDER = BIG + ("ada_w", "ada_b", "ln_g", "ln_b", "b_conv_w", "b_conv_b", "b_dt_bias", "b_a_log", "b_d", "b_norm_w")


def kernel(x, c, ada_w, ada_b, ln_g, ln_b, a_w_in, a_w_out, b_w_in, b_conv_w, b_conv_b, b_dt_bias, b_a_log, b_d, b_norm_w, b_w_out, loss_target, m_ada_w, m_ada_b, m_ln_g, m_ln_b, m_a_w_in, m_a_w_out, m_b_w_in, m_b_conv_w, m_b_conv_b, m_b_dt_bias, m_b_a_log, m_b_d, m_b_norm_w, m_b_w_out, v_ada_w, v_ada_b, v_ln_g, v_ln_b, v_a_w_in, v_a_w_out, v_b_w_in, v_b_conv_w, v_b_conv_b, v_b_dt_bias, v_b_a_log, v_b_d, v_b_norm_w, v_b_w_out):
    w = dict(ada_w=ada_w, ada_b=ada_b, ln_g=ln_g, ln_b=ln_b, a_w_in=a_w_in, a_w_out=a_w_out, b_w_in=b_w_in, b_conv_w=b_conv_w,
             b_conv_b=b_conv_b, b_dt_bias=b_dt_bias, b_a_log=b_a_log, b_d=b_d, b_norm_w=b_norm_w, b_w_out=b_w_out)
    mom = dict(ada_w=m_ada_w, ada_b=m_ada_b, ln_g=m_ln_g, ln_b=m_ln_b, a_w_in=m_a_w_in, a_w_out=m_a_w_out, b_w_in=m_b_w_in,
               b_conv_w=m_b_conv_w, b_conv_b=m_b_conv_b, b_dt_bias=m_b_dt_bias, b_a_log=m_b_a_log, b_d=m_b_d, b_norm_w=m_b_norm_w,
               b_w_out=m_b_w_out)
    var = dict(ada_w=v_ada_w, ada_b=v_ada_b, ln_g=v_ln_g, ln_b=v_ln_b, a_w_in=v_a_w_in, a_w_out=v_a_w_out, b_w_in=v_b_w_in,
               b_conv_w=v_b_conv_w, b_conv_b=v_b_conv_b, b_dt_bias=v_b_dt_bias, b_a_log=v_b_a_log, b_d=v_b_d, b_norm_w=v_b_norm_w,
               b_w_out=v_b_w_out)
    ax, ay, ac = lax.axis_index("x"), lax.axis_index("y"), lax.axis_index("c")
    me = 4 * ax + 2 * ay + ac
    shard = 2 * ax + ay
    d = x.shape[-1]
    dsh = ada_w.shape[-1]

    small_in = (c, b_conv_w, b_conv_b, b_norm_w)
    g0 = _all_gather8(_pack(small_in).reshape(-1, LANE), "gather_small_in").reshape(N_DEV, -1, PACK_COLS)
    per_dev = [_unpack(g0[k], [a.shape for a in small_in]) for k in range(N_DEV)]
    c_all = jnp.concatenate([p[0] for p in per_dev], axis=0)
    conv_w_full, conv_b_full, norm_w_full = (_unshard_cols([per_dev[2 * k][t] for k in range(N_SHARD)]) for t in (1, 2, 3))

    cond = _silu_rows(jnp.pad(c_all, ((0, 8), (0, 0))), "cond")
    bias = lax.dynamic_slice_in_dim(ada_b, shard * dsh, dsh, axis=1)
    part = jnp.stack([_mm(cond, ada_w[i], add=jnp.broadcast_to(bias[i], (16, dsh)), name=f"mod{i}")[:N_DEV] for i in range(DEPTH)])
    g1 = _all_gather8(part.reshape(-1, LANE), "gather_mod").reshape(N_DEV, DEPTH, N_DEV, dsh)
    mod_all = _unshard_cols([g1[2 * k] for k in range(N_SHARD)])
    mod = lax.dynamic_index_in_dim(mod_all, me, axis=1, keepdims=False)

    gw = _gather_shards(_pack([w[n] for n in BIG]).astype(BF16), "gather_weights")
    big_sh = [_unpack(gw[k], [w[n].shape for n in BIG]) for k in range(N_SHARD)]
    full = dict(
        ln_g=ln_g, ln_b=ln_b, b_dt_bias=b_dt_bias, b_a_log=b_a_log, b_d=b_d,
        b_conv_w=conv_w_full, b_conv_b=conv_b_full, b_norm_w=norm_w_full,
        a_w_in=_unshard_cols([s[0] for s in big_sh]), a_w_out=_unshard_rows([s[1] for s in big_sh]),
        b_w_in=_unshard_cols([s[2] for s in big_sh]), b_w_out=_unshard_rows([s[3] for s in big_sh]),
    )

    loss, grad_x, dmod, g = _local_step(x[0], loss_target[0], mod, full)

    gsh = (_shard_cols(g["a_w_in"]), _shard_rows(g["a_w_out"]), _shard_cols(g["b_w_in"]), _shard_rows(g["b_w_out"]))
    to_send = jnp.stack([_pack([t[k] for t in gsh]) for k in range(N_SHARD)]).astype(BF16)
    mine = _sum_slots(_transpose_shards(to_send, "scatter_grads"), "sum_shards")
    theirs = _swap_sibling(mine, "swap_grads")

    small_g = (dmod, g["ln_g"], g["ln_b"], g["b_dt_bias"], g["b_a_log"], g["b_d"], g["b_conv_w"], g["b_conv_b"], g["b_norm_w"],
               loss.reshape(1))
    g2 = _all_gather8(_pack(small_g).reshape(-1, LANE), "gather_small_grads")
    tot = _unpack(_sum_slots(g2, "sum_small").reshape(-1, PACK_COLS), [a.shape for a in small_g])
    g_ada_b, g_ln_g, g_ln_b, g_dt_bias, g_a_log, g_d, g_conv_w, g_conv_b, g_norm_w, loss_sum = tot
    dmod_all = g2.reshape(N_DEV, -1)[:, :dmod.size].reshape(N_DEV, DEPTH, 3 * d)
    dmod_mine = lax.dynamic_slice_in_dim(dmod_all, shard * dsh, dsh, axis=2)
    g_ada_w = jnp.stack([_mm(cond, jnp.pad(dmod_mine[:, i], ((0, 8), (0, 0))), ta=True, name=f"dada{i}") for i in range(DEPTH)])
    csh = g_conv_w.shape[-1] // N_SHARD
    nsh = g_norm_w.shape[-1] // N_SHARD
    small_grads = dict(
        ada_w=g_ada_w, ada_b=g_ada_b, ln_g=g_ln_g, ln_b=g_ln_b, b_dt_bias=g_dt_bias, b_a_log=g_a_log, b_d=g_d,
        b_conv_w=lax.dynamic_slice_in_dim(g_conv_w, shard * csh, csh, axis=2),
        b_conv_b=lax.dynamic_slice_in_dim(g_conv_b, shard * csh, csh, axis=1),
        b_norm_w=lax.dynamic_slice_in_dim(g_norm_w, shard * nsh, nsh, axis=1),
    )

    rest = PACK_ORDER[len(BIG):]
    rows = sum(_nrows(w[n].shape) for n in PACK_ORDER)
    rows = -(-rows // PACK_TILE) * PACK_TILE
    pw, pm, pv = (_pack([t[n] for n in PACK_ORDER], rows) for t in (w, mom, var))
    pg1 = _pack([mine] + [small_grads[n] for n in rest], rows)
    pg2 = _pack([theirs], rows)
    outs = [_unpack(p, [w[n].shape for n in PACK_ORDER]) for p in _adamw(pw, pg1, pg2, pm, pv, "adamw")]
    by_name = [dict(zip(PACK_ORDER, o)) for o in outs]
    return (loss_sum.reshape(()), grad_x[None], *[t[n] for t in by_name for n in W_NAMES])
```

```python
import jax
import jax.numpy as jnp
import numpy as np
from jax import lax
from jax.experimental import pallas as pl
from jax.experimental.pallas import tpu as pltpu

F32 = jnp.float32
BF16 = jnp.bfloat16

DEPTH = 4
A_HEADS = 16
A_HEAD_DIM = 64
A_WIDTH = A_HEADS * A_HEAD_DIM
DILATIONS = (1, 4, 16)
A_RADIUS = 64
A_QBLOCK = 128
SSM_HEADS = 32
SSM_HEAD_DIM = 64
SSM_STATE = 128
SSM_GROUPS = 4
SSM_REP = SSM_HEADS // SSM_GROUPS
SSM_CONV = 5
SSM_CHUNK = 128
DEEPNORM_ALPHA = (2 * DEPTH) ** 0.25
LN_EPS = 1e-5
RMS_EPS = 1e-5
ADAM_LR, ADAM_B1, ADAM_B2, ADAM_EPS, ADAM_WD, ADAM_STEP = 0.001, 0.9, 0.999, 1e-08, 0.01, 10
VMEM_LIMIT = 56 * 1024 * 1024
LANE = 128


def _cp(*sem):
    return pltpu.CompilerParams(dimension_semantics=sem, vmem_limit_bytes=VMEM_LIMIT)


def _tile(dim, target):
    if dim <= target:
        return dim
    t = (target // LANE) * LANE
    while dim % t:
        t -= LANE
    return t


def _sigmoid(x):
    return 1.0 / (1.0 + jnp.exp(-x))


def _silu(x):
    return x * _sigmoid(x)


def _dsilu(x):
    s = _sigmoid(x)
    return s * (1.0 + x * (1.0 - s))


def _split3(x):
    a = x.astype(BF16)
    r = x - a.astype(F32)
    b = r.astype(BF16)
    c = (r - b.astype(F32)).astype(BF16)
    return a, b, c


def _dot(a, b, ca=1, cb=0):
    return lax.dot_general(a, b, (((ca,), (cb,)), ((), ())), preferred_element_type=F32)


def _dot_exact(m01, x):
    a, b, c = _split3(x)
    return _dot(m01, a) + _dot(m01, b) + _dot(m01, c)


def _mm(a, b, *, ta=False, tb=False, add=None, out_dtype=F32, name, tm=1024, tn=1024, tk=1024):
    m, k = (a.shape[1], a.shape[0]) if ta else a.shape
    n = b.shape[0] if tb else b.shape[1]
    assert (b.shape[1] if tb else b.shape[0]) == k
    tm, tn, tk = _tile(m, tm), _tile(n, tn), _tile(k, tk)
    nk = k // tk
    has_add = add is not None

    def body(*refs):
        if has_add:
            a_ref, b_ref, c_ref, o_ref, acc = refs
        else:
            a_ref, b_ref, o_ref, acc = refs
        kk = pl.program_id(2)
        part = _dot(a_ref[...].astype(BF16), b_ref[...].astype(BF16), 0 if ta else 1, 1 if tb else 0)

        def finish(r):
            if has_add:
                r = r + c_ref[...]
            o_ref[...] = r.astype(o_ref.dtype)

        if nk == 1:
            finish(part)
            return

        @pl.when(kk == 0)
        def _():
            acc[...] = part

        @pl.when((kk > 0) & (kk < nk - 1))
        def _():
            acc[...] += part

        @pl.when(kk == nk - 1)
        def _():
            finish(acc[...] + part)

    a_spec = pl.BlockSpec((tk, tm), lambda i, j, kk: (kk, i)) if ta else pl.BlockSpec((tm, tk), lambda i, j, kk: (i, kk))
    b_spec = pl.BlockSpec((tn, tk), lambda i, j, kk: (j, kk)) if tb else pl.BlockSpec((tk, tn), lambda i, j, kk: (kk, j))
    in_specs = [a_spec, b_spec]
    args = [a, b]
    if has_add:
        in_specs.append(pl.BlockSpec((tm, tn), lambda i, j, kk: (i, j)))
        args.append(add)
    return pl.pallas_call(
        body, name=name, grid=(m // tm, n // tn, nk), in_specs=in_specs,
        out_specs=pl.BlockSpec((tm, tn), lambda i, j, kk: (i, j)),
        out_shape=jax.ShapeDtypeStruct((m, n), out_dtype),
        scratch_shapes=[pltpu.VMEM((tm, tn) if nk > 1 else (8, LANE), F32)],
        compiler_params=_cp("parallel", "parallel", "arbitrary"),
    )(*args)


ROWS = 512


def _row_spec(tm, d):
    return pl.BlockSpec((tm, d), lambda i: (i, 0))


def _vec_spec(d, rows=1):
    return pl.BlockSpec((rows, d), lambda i: (0, 0))


def _modulate(x, scale, shift, name):
    s, d = x.shape
    tm = min(ROWS, s)

    def body(x_ref, sc_ref, sh_ref, o_ref):
        o_ref[...] = (x_ref[...] * (1.0 + sc_ref[...]) + sh_ref[...]).astype(BF16)

    return pl.pallas_call(
        body, name=name, grid=(s // tm,), in_specs=[_row_spec(tm, d), _vec_spec(d), _vec_spec(d)],
        out_specs=_row_spec(tm, d), out_shape=jax.ShapeDtypeStruct((s, d), BF16), compiler_params=_cp("parallel"),
    )(x, scale, shift)


def _resid_ln(x, y, gate, g, b, name):
    s, d = x.shape
    tm = min(ROWS, s)

    def body(x_ref, y_ref, gt_ref, g_ref, b_ref, o_ref):
        u = DEEPNORM_ALPHA * x_ref[...] + gt_ref[...] * y_ref[...]
        mu = jnp.mean(u, axis=1, keepdims=True)
        uc = u - mu
        var = jnp.mean(uc * uc, axis=1, keepdims=True)
        o_ref[...] = uc * lax.rsqrt(var + LN_EPS) * g_ref[...] + b_ref[...]

    return pl.pallas_call(
        body, name=name, grid=(s // tm,),
        in_specs=[_row_spec(tm, d), _row_spec(tm, d), _vec_spec(d), _vec_spec(d), _vec_spec(d)],
        out_specs=_row_spec(tm, d), out_shape=jax.ShapeDtypeStruct((s, d), F32), compiler_params=_cp("parallel"),
    )(x, y, gate, g, b)


def _resid_ln_bwd(x, y, dxn, gate, g, name):
    s, d = x.shape
    tm = min(ROWS, s)

    def body(x_ref, y_ref, dxn_ref, gt_ref, g_ref, du_ref, dy_ref, red_ref):
        @pl.when(pl.program_id(0) == 0)
        def _():
            red_ref[...] = jnp.zeros_like(red_ref)

        yv = y_ref[...]
        u = DEEPNORM_ALPHA * x_ref[...] + gt_ref[...] * yv
        mu = jnp.mean(u, axis=1, keepdims=True)
        uc = u - mu
        var = jnp.mean(uc * uc, axis=1, keepdims=True)
        rstd = lax.rsqrt(var + LN_EPS)
        xhat = uc * rstd
        dxnv = dxn_ref[...]
        dxh = dxnv * g_ref[...]
        du = rstd * (dxh - jnp.mean(dxh, axis=1, keepdims=True) - xhat * jnp.mean(dxh * xhat, axis=1, keepdims=True))
        du_ref[...] = du
        dy_ref[...] = (du * gt_ref[...]).astype(BF16)
        red_ref[0:1, :] += jnp.sum(du * yv, axis=0, keepdims=True)
        red_ref[1:2, :] += jnp.sum(dxnv * xhat, axis=0, keepdims=True)
        red_ref[2:3, :] += jnp.sum(dxnv, axis=0, keepdims=True)

    return pl.pallas_call(
        body, name=name, grid=(s // tm,),
        in_specs=[_row_spec(tm, d), _row_spec(tm, d), _row_spec(tm, d), _vec_spec(d), _vec_spec(d)],
        out_specs=[_row_spec(tm, d), _row_spec(tm, d), _vec_spec(d, 8)],
        out_shape=[jax.ShapeDtypeStruct((s, d), F32), jax.ShapeDtypeStruct((s, d), BF16), jax.ShapeDtypeStruct((8, d), F32)],
        compiler_params=_cp("arbitrary"),
    )(x, y, dxn, gate, g)


def _modulate_bwd(du, dh, x, scale, name):
    s, d = x.shape
    tm = min(ROWS, s)

    def body(du_ref, dh_ref, x_ref, sc_ref, dx_ref, red_ref):
        @pl.when(pl.program_id(0) == 0)
        def _():
            red_ref[...] = jnp.zeros_like(red_ref)

        dhv = dh_ref[...]
        dx_ref[...] = DEEPNORM_ALPHA * du_ref[...] + dhv * (1.0 + sc_ref[...])
        red_ref[0:1, :] += jnp.sum(dhv * x_ref[...], axis=0, keepdims=True)
        red_ref[1:2, :] += jnp.sum(dhv, axis=0, keepdims=True)

    return pl.pallas_call(
        body, name=name, grid=(s // tm,),
        in_specs=[_row_spec(tm, d), _row_spec(tm, d), _row_spec(tm, d), _vec_spec(d)],
        out_specs=[_row_spec(tm, d), _vec_spec(d, 8)],
        out_shape=[jax.ShapeDtypeStruct((s, d), F32), jax.ShapeDtypeStruct((8, d), F32)],
        compiler_params=_cp("arbitrary"),
    )(du, dh, x, scale)


def _loss_grad(xf, tgt, name):
    s, d = xf.shape
    tm = min(ROWS, s)

    def body(x_ref, t_ref, dx_ref, red_ref):
        @pl.when(pl.program_id(0) == 0)
        def _():
            red_ref[...] = jnp.zeros_like(red_ref)

        e = x_ref[...] - t_ref[...]
        dx_ref[...] = e * (1.0 / d)
        red_ref[0:1, :] += jnp.sum(e * e, axis=0, keepdims=True)

    return pl.pallas_call(
        body, name=name, grid=(s // tm,), in_specs=[_row_spec(tm, d), _row_spec(tm, d)],
        out_specs=[_row_spec(tm, d), _vec_spec(d, 8)],
        out_shape=[jax.ShapeDtypeStruct((s, d), F32), jax.ShapeDtypeStruct((8, d), F32)],
        compiler_params=_cp("arbitrary"),
    )(xf, tgt)


QKV_COLS = 3 * 3 * A_WIDTH


SLOPES = tuple(float(2.0 ** (-8.0 * (h + 1.0) / A_HEADS)) for h in range(A_HEADS))
FAR = 1e30
HEAD_COLS = tuple(slice(h * A_HEAD_DIM, (h + 1) * A_HEAD_DIM) for h in range(A_HEADS))


def _band_dist(n, length, dil, span_rows):
    shape = (2 * A_QBLOCK, A_QBLOCK) if span_rows else (A_QBLOCK, 2 * A_QBLOCK)
    r = lax.broadcasted_iota(jnp.int32, shape, 0)
    c = lax.broadcasted_iota(jnp.int32, shape, 1)
    sp, ce = (r, c) if span_rows else (c, r)
    delta = sp - A_RADIUS - ce
    pos = n * A_QBLOCK - A_RADIUS + sp
    valid = (jnp.abs(delta) <= A_RADIUS) & (pos >= 0) & (pos < length)
    return jnp.where(valid, jnp.abs(delta).astype(F32) * float(dil), FAR)


def _span_specs(colblock, nb64):
    def mk(i):
        return pl.BlockSpec((64, A_WIDTH), lambda r, n: (jnp.clip(2 * n - 1 + i, 0, nb64 - 1), colblock(r)))
    return [mk(i) for i in range(4)]


def _cat(refs):
    return jnp.concatenate([t[...] for t in refs], axis=0)


def _head_expander():
    r = lax.broadcasted_iota(jnp.int32, (A_HEADS, A_WIDTH), 0)
    c = lax.broadcasted_iota(jnp.int32, (A_HEADS, A_WIDTH), 1)
    return ((c >= r * A_HEAD_DIM) & (c < (r + 1) * A_HEAD_DIM)).astype(BF16)


def _to_lanes(x16, e):
    a, b, c = _split3(x16)
    return _dot(a, e) + _dot(b, e) + _dot(c, e)


def _per_head_sum(x, e):
    a, b, c = _split3(x)
    return _dot(a, e, 1, 1) + _dot(b, e, 1, 1) + _dot(c, e, 1, 1)


def _residue_major(t16, dil):
    return t16.reshape(-1, dil, A_HEADS).transpose(1, 0, 2)


def _attn_fwd(qkv, g, name):
    s = qkv.shape[0]
    dil = DILATIONS[g]
    length = s // dil
    nblk = length // A_QBLOCK
    cpr = QKV_COLS // A_WIDTH
    base = g * 3
    view = qkv.reshape(length, dil * QKV_COLS)

    def body(q_ref, k0, k1, k2, k3, v0, v1, v2, v3, o_ref, l_ref):
        dist = _band_dist(pl.program_id(1), length, dil, False)
        kk = _cat((k0, k1, k2, k3))
        vv = _cat((v0, v1, v2, v3))
        for h, cs in enumerate(HEAD_COLS):
            sc = _dot(q_ref[:, cs], kk[:, cs], 1, 1) * 0.125 - SLOPES[h] * dist
            m = jnp.max(sc, axis=1, keepdims=True)
            p = jnp.exp(sc - m)
            z = jnp.sum(p, axis=1, keepdims=True)
            o_ref[:, cs] = _dot(p.astype(BF16), vv[:, cs]) / z
            l_ref[:, h:h + 1] = m + jnp.log(z)

    qspec = pl.BlockSpec((A_QBLOCK, A_WIDTH), lambda r, n: (n, r * cpr + base))
    kspecs = _span_specs(lambda r: r * cpr + base + 1, 2 * nblk)
    vspecs = _span_specs(lambda r: r * cpr + base + 2, 2 * nblk)
    ospec = pl.BlockSpec((A_QBLOCK, A_WIDTH), lambda r, n: (n, r))
    o, l = pl.pallas_call(
        body, name=name, grid=(dil, nblk), in_specs=[qspec] + kspecs + vspecs,
        out_specs=[ospec, pl.BlockSpec((None, A_QBLOCK, A_HEADS), lambda r, n: (r, n, 0))],
        out_shape=[jax.ShapeDtypeStruct((length, dil * A_WIDTH), F32), jax.ShapeDtypeStruct((dil, length, A_HEADS), F32)],
        compiler_params=_cp("parallel", "parallel"),
    )(view, *([view] * 8))
    return o.reshape(s, A_WIDTH), l.transpose(1, 0, 2).reshape(s, A_HEADS)


def _attn_merge(os_, ls_, gate, name):
    s, w = gate.shape
    tm = min(ROWS, s)

    def body(o0, o1, o2, l0, l1, l2, g_ref, y_ref, o_ref, l_ref):
        a, b, c = l0[...], l1[...], l2[...]
        m = jnp.maximum(jnp.maximum(a, b), c)
        ea, eb, ec = jnp.exp(a - m), jnp.exp(b - m), jnp.exp(c - m)
        z = ea + eb + ec
        l_ref[...] = m + jnp.log(z)
        e = _head_expander()
        o = _to_lanes(ea / z, e) * o0[...] + _to_lanes(eb / z, e) * o1[...] + _to_lanes(ec / z, e) * o2[...]
        o_ref[...] = o
        y_ref[...] = (o * _silu(g_ref[...])).astype(BF16)

    rs = _row_spec(tm, w)
    ls = _row_spec(tm, A_HEADS)
    return pl.pallas_call(
        body, name=name, grid=(s // tm,), in_specs=[rs] * 3 + [ls] * 3 + [rs], out_specs=[rs, rs, ls],
        out_shape=[jax.ShapeDtypeStruct((s, w), BF16), jax.ShapeDtypeStruct((s, w), F32), jax.ShapeDtypeStruct((s, A_HEADS), F32)],
        compiler_params=_cp("parallel"),
    )(*os_, *ls_, gate)


def _attn_gate_bwd(dyy, o, gate, name):
    s, w = gate.shape
    tm = min(ROWS, s)

    def body(dy_ref, o_ref, g_ref, do_ref, dg_ref, dl_ref):
        dyv, ov, gv = dy_ref[...], o_ref[...], g_ref[...]
        do = dyv * _silu(gv)
        do_ref[...] = do.astype(BF16)
        dg_ref[...] = (dyv * ov * _dsilu(gv)).astype(BF16)
        dl_ref[...] = _per_head_sum(do * ov, _head_expander())

    rs = _row_spec(tm, w)
    return pl.pallas_call(
        body, name=name, grid=(s // tm,), in_specs=[rs] * 3, out_specs=[rs, rs, _row_spec(tm, A_HEADS)],
        out_shape=[jax.ShapeDtypeStruct((s, w), BF16), jax.ShapeDtypeStruct((s, w), BF16), jax.ShapeDtypeStruct((s, A_HEADS), F32)],
        compiler_params=_cp("parallel"),
    )(dyy, o, gate)


def _attn_dq(qkv, do, lse, delta, g, name):
    s = qkv.shape[0]
    dil = DILATIONS[g]
    length = s // dil
    nblk = length // A_QBLOCK
    cpr = QKV_COLS // A_WIDTH
    base = g * 3
    view = qkv.reshape(length, dil * QKV_COLS)

    def body(q_ref, k0, k1, k2, k3, v0, v1, v2, v3, do_ref, l_ref, dl_ref, dq_ref):
        dist = _band_dist(pl.program_id(1), length, dil, False)
        kk = _cat((k0, k1, k2, k3))
        vv = _cat((v0, v1, v2, v3))
        for h, cs in enumerate(HEAD_COLS):
            sc = _dot(q_ref[:, cs], kk[:, cs], 1, 1) * 0.125 - SLOPES[h] * dist
            p = jnp.exp(sc - l_ref[:, h:h + 1])
            dp = _dot(do_ref[:, cs], vv[:, cs], 1, 1)
            ds = p * (dp - dl_ref[:, h:h + 1])
            dq_ref[:, cs] = (_dot(ds.astype(BF16), kk[:, cs]) * 0.125).astype(BF16)

    qspec = pl.BlockSpec((A_QBLOCK, A_WIDTH), lambda r, n: (n, r * cpr + base))
    kspecs = _span_specs(lambda r: r * cpr + base + 1, 2 * nblk)
    vspecs = _span_specs(lambda r: r * cpr + base + 2, 2 * nblk)
    ospec = pl.BlockSpec((A_QBLOCK, A_WIDTH), lambda r, n: (n, r))
    cspec = pl.BlockSpec((None, A_QBLOCK, A_HEADS), lambda r, n: (r, n, 0))
    dq = pl.pallas_call(
        body, name=name, grid=(dil, nblk), in_specs=[qspec] + kspecs + vspecs + [ospec, cspec, cspec],
        out_specs=ospec, out_shape=jax.ShapeDtypeStruct((length, dil * A_WIDTH), BF16),
        compiler_params=_cp("parallel", "parallel"),
    )(view, *([view] * 8), do.reshape(length, dil * A_WIDTH), _residue_major(lse, dil), _residue_major(delta, dil))
    return dq.reshape(s, A_WIDTH)


def _attn_dkv(qkv, do, lse, delta, g, name):
    s = qkv.shape[0]
    dil = DILATIONS[g]
    length = s // dil
    nblk = length // A_QBLOCK
    cpr = QKV_COLS // A_WIDTH
    base = g * 3
    view = qkv.reshape(length, dil * QKV_COLS)
    wide = lambda t: t.reshape(length, dil * A_WIDTH)

    def by_residue(t16):
        return jnp.pad(t16.reshape(length, dil, A_HEADS).transpose(1, 2, 0), ((0, 0), (0, 0), (A_RADIUS, A_RADIUS)))

    def body(k_ref, v_ref, q0, q1, q2, q3, d0, d1, d2, d3, la, lb, ea, eb, dk_ref, dv_ref):
        dist = _band_dist(pl.program_id(1), length, dil, False)
        qq = _cat((q0, q1, q2, q3))
        dd = _cat((d0, d1, d2, d3))
        lse = jnp.concatenate([la[...], lb[...]], axis=1)
        dlt = jnp.concatenate([ea[...], eb[...]], axis=1)
        for h, cs in enumerate(HEAD_COLS):
            sc = _dot(k_ref[:, cs], qq[:, cs], 1, 1) * 0.125 - SLOPES[h] * dist
            p = jnp.exp(sc - lse[h:h + 1, :])
            dv_ref[:, cs] = _dot(p.astype(BF16), dd[:, cs]).astype(BF16)
            dp = _dot(v_ref[:, cs], dd[:, cs], 1, 1)
            ds = p * (dp - dlt[h:h + 1, :])
            dk_ref[:, cs] = (_dot(ds.astype(BF16), qq[:, cs]) * 0.125).astype(BF16)

    kspec = pl.BlockSpec((A_QBLOCK, A_WIDTH), lambda r, n: (n, r * cpr + base + 1))
    vspec = pl.BlockSpec((A_QBLOCK, A_WIDTH), lambda r, n: (n, r * cpr + base + 2))
    qspecs = _span_specs(lambda r: r * cpr + base, 2 * nblk)
    wspecs = _span_specs(lambda r: r, 2 * nblk)
    rspecs = [pl.BlockSpec((None, A_HEADS, A_QBLOCK), lambda r, n: (r, 0, n)), pl.BlockSpec((None, A_HEADS, A_QBLOCK), lambda r, n: (r, 0, n + 1))]
    ospec = pl.BlockSpec((A_QBLOCK, A_WIDTH), lambda r, n: (n, r))
    lse_r, dlt_r = by_residue(lse), by_residue(delta)
    dk, dv = pl.pallas_call(
        body, name=name, grid=(dil, nblk), in_specs=[kspec, vspec] + qspecs + wspecs + rspecs * 2,
        out_specs=[ospec, ospec], out_shape=[jax.ShapeDtypeStruct((length, dil * A_WIDTH), BF16)] * 2,
        compiler_params=_cp("parallel", "parallel"),
    )(view, view, *([view] * 4), *([wide(do)] * 4), lse_r, lse_r, dlt_r, dlt_r)
    return dk.reshape(s, A_WIDTH), dv.reshape(s, A_WIDTH)


def _attn_layer_fwd(h, w_qkv, w_gate, w_out, li):
    nm = lambda t: f"a{li}_{t}"
    qkv = _mm(h, w_qkv, out_dtype=BF16, name=nm("qkv"))
    gate = _mm(h, w_gate, name=nm("gate"))
    os_, ls_ = [], []
    for g in range(3):
        o, l = _attn_fwd(qkv, g, nm(f"attn{g}"))
        os_.append(o)
        ls_.append(l)
    y, o, lse = _attn_merge(os_, ls_, gate, nm("merge"))
    out = _mm(y, w_out, name=nm("out"))
    return out, (qkv, gate, y, o, lse)


def _attn_layer_bwd(dy, h, saved, w_qkv, w_gate, w_out, li):
    nm = lambda t: f"a{li}_{t}"
    qkv, gate, y, o, lse = saved
    g_w_out = _mm(y, dy, ta=True, out_dtype=BF16, name=nm("dwout"))
    dyy = _mm(dy, w_out, tb=True, name=nm("dyy"))
    do, dgate, delta = _attn_gate_bwd(dyy, o, gate, nm("gatebwd"))
    parts = []
    for g in range(3):
        dq = _attn_dq(qkv, do, lse, delta, g, nm(f"dq{g}"))
        dk, dv = _attn_dkv(qkv, do, lse, delta, g, nm(f"dkv{g}"))
        parts += [dq, dk, dv]
    dqkv = jnp.concatenate(parts, axis=1)
    dh = _mm(dgate, w_gate, tb=True, name=nm("dh_gate"))
    dh = _mm(dqkv, w_qkv, tb=True, add=dh, name=nm("dh_qkv"))
    g_w_in = jnp.concatenate([_mm(h, dqkv, ta=True, out_dtype=BF16, name=nm("dwqkv")),
                              _mm(h, dgate, ta=True, out_dtype=BF16, name=nm("dwgate"))], axis=1)
    return dh, g_w_in, g_w_out


SSM_INNER = SSM_HEADS * SSM_HEAD_DIM
SSM_BC = SSM_GROUPS * SSM_STATE
SSM_CONV_DIM = SSM_INNER + 2 * SSM_BC
GW = SSM_REP * SSM_HEAD_DIM
T = SSM_CHUNK
HALO = 8


def _conv_specs(tm, tn, s):
    nb8 = s // HALO
    cur = pl.BlockSpec((tm, tn), lambda j, i: (i, j))
    prev = pl.BlockSpec((HALO, tn), lambda j, i: (jnp.maximum(i * (tm // HALO) - 1, 0), j))
    nxt = pl.BlockSpec((HALO, tn), lambda j, i: (jnp.minimum((i + 1) * (tm // HALO), nb8 - 1), j))
    return prev, cur, nxt


def _extend(prev_ref, cur_ref, nxt_ref, i, nrow):
    p = jnp.where(i == 0, 0.0, prev_ref[...])
    n = jnp.where(i == nrow - 1, 0.0, nxt_ref[...])
    return jnp.concatenate([p, cur_ref[...], n], axis=0)


def _shift_rows(ext, off, tm):
    rows = ext.shape[0]
    return pltpu.roll(ext, (-off) % rows, 0)[HALO:HALO + tm]


def _conv_fwd(xraw, w, b, name):
    s, cdim = xraw.shape
    tm, tn = min(256, s), 1024
    nrow = s // tm

    def body(p_ref, c_ref, n_ref, w_ref, b_ref, pre_ref, act_ref):
        ext = _extend(p_ref, c_ref, n_ref, pl.program_id(1), nrow)
        acc = jnp.broadcast_to(b_ref[...], (tm, tn))
        for k in range(SSM_CONV):
            acc = acc + w_ref[k:k + 1, :] * _shift_rows(ext, k - SSM_CONV // 2, tm)
        pre_ref[...] = acc
        act_ref[...] = _silu(acc)

    prev, cur, nxt = _conv_specs(tm, tn, s)
    return pl.pallas_call(
        body, name=name, grid=(cdim // tn, nrow),
        in_specs=[prev, cur, nxt, pl.BlockSpec((SSM_CONV, tn), lambda j, i: (0, j)), pl.BlockSpec((1, tn), lambda j, i: (0, j))],
        out_specs=[cur, cur], out_shape=[jax.ShapeDtypeStruct((s, cdim), F32)] * 2,
        compiler_params=_cp("parallel", "parallel"),
    )(xraw, xraw, xraw, w, b)


def _conv_bwd(dact, pre, xraw, w, name):
    s, cdim = xraw.shape
    tm, tn = min(256, s), 1024
    nrow = s // tm

    def body(dp, dc, dn, pp, pc, pn, xp, xc, xn, w_ref, dx_ref, red_ref):
        i = pl.program_id(1)

        @pl.when(i == 0)
        def _():
            red_ref[...] = jnp.zeros_like(red_ref)

        dpre = _extend(dp, dc, dn, i, nrow) * _dsilu(_extend(pp, pc, pn, i, nrow))
        xext = _extend(xp, xc, xn, i, nrow)
        dcur = dpre[HALO:HALO + tm]
        acc = jnp.zeros((tm, tn), F32)
        for k in range(SSM_CONV):
            off = k - SSM_CONV // 2
            acc = acc + w_ref[k:k + 1, :] * _shift_rows(dpre, -off, tm)
            red_ref[k:k + 1, :] += jnp.sum(dcur * _shift_rows(xext, off, tm), axis=0, keepdims=True)
        red_ref[SSM_CONV:SSM_CONV + 1, :] += jnp.sum(dcur, axis=0, keepdims=True)
        dx_ref[...] = acc.astype(BF16)

    prev, cur, nxt = _conv_specs(tm, tn, s)
    return pl.pallas_call(
        body, name=name, grid=(cdim // tn, nrow),
        in_specs=[prev, cur, nxt] * 3 + [pl.BlockSpec((SSM_CONV, tn), lambda j, i: (0, j))],
        out_specs=[cur, pl.BlockSpec((8, tn), lambda j, i: (0, j))],
        out_shape=[jax.ShapeDtypeStruct((s, cdim), BF16), jax.ShapeDtypeStruct((8, cdim), F32)],
        compiler_params=_cp("parallel", "arbitrary"),
    )(dact, dact, dact, pre, pre, pre, xraw, xraw, xraw, w)


def _tri(lower):
    r = lax.broadcasted_iota(jnp.int32, (T, T), 0)
    c = lax.broadcasted_iota(jnp.int32, (T, T), 1)
    return (r >= c) if lower else (r <= c)


def _softplus(x):
    return jnp.maximum(x, 0.0) + jnp.log(1.0 + jnp.exp(-jnp.abs(x)))


def _dt_prep(dt_raw, bias, a_log, name):
    s = dt_raw.shape[0]
    nc = s // T

    def body(r_ref, b_ref, a_ref, dt_ref, cum_ref, cumt_ref, dtw_ref, cumw_ref):
        dt = _softplus(r_ref[...] + b_ref[...])
        da = dt * (-jnp.exp(a_ref[...]))
        pre = _dot_exact(_tri(True).astype(BF16), da)
        suf = _dot_exact(_tri(False).astype(BF16), da)
        lane = lax.broadcasted_iota(jnp.int32, (T, LANE), 1)
        cum = jnp.where(lane < SSM_HEADS, pre, suf)
        dt_ref[...] = dt
        cum_ref[...] = cum
        cumt_ref[...] = cum.T
        r = lax.broadcasted_iota(jnp.int32, (LANE, SSM_INNER), 0)
        c = lax.broadcasted_iota(jnp.int32, (LANE, SSM_INNER), 1)
        for d in range(2):
            h = r - d * SSM_HEADS
            e = ((c >= h * SSM_HEAD_DIM) & (c < (h + 1) * SSM_HEAD_DIM)).astype(BF16)
            dtw_ref[d] = _to_lanes(dt, e)
            cumw_ref[d] = _to_lanes(cum, e)

    blk = pl.BlockSpec((T, LANE), lambda c: (c, 0))
    vec = pl.BlockSpec((1, LANE), lambda c: (0, 0))
    wide = pl.BlockSpec((2, T, SSM_INNER), lambda c: (0, c, 0))
    return pl.pallas_call(
        body, name=name, grid=(nc,), in_specs=[blk, vec, vec],
        out_specs=[blk, blk, pl.BlockSpec((None, LANE, T), lambda c: (c, 0, 0)), wide, wide],
        out_shape=[jax.ShapeDtypeStruct((s, LANE), F32), jax.ShapeDtypeStruct((s, LANE), F32), jax.ShapeDtypeStruct((nc, LANE, T), F32),
                   jax.ShapeDtypeStruct((2, s, SSM_INNER), F32), jax.ShapeDtypeStruct((2, s, SSM_INNER), F32)],
        compiler_params=_cp("parallel"),
    )(dt_raw, bias, a_log)


def _by_group(t):
    s = t.shape[0]
    return t[:, :2 * SSM_HEADS].reshape(s, 2 * SSM_GROUPS, SSM_REP).transpose(1, 0, 2)


def _from_group(tf, tb):
    s = tf.shape[1]
    t = jnp.concatenate([tf, tb], axis=0).transpose(1, 0, 2).reshape(s, 2 * SSM_HEADS)
    return jnp.pad(t, ((0, 0), (0, LANE - 2 * SSM_HEADS)))


def _decay_mats(acol, arow, rev):
    after = _tri(not rev)
    return jnp.where(after, jnp.exp(jnp.where(after, acol - arow, 0.0)), 0.0)


PAIRS = SSM_REP // 2


def _low_lanes():
    return lax.broadcasted_iota(jnp.int32, (T, LANE), 1) < SSM_HEAD_DIM


def _block_diag(v, low):
    zero = jnp.zeros_like(v)
    return jnp.concatenate([jnp.where(low, v, zero), jnp.where(low, zero, v)], axis=0)


def _scan_specs(nc, rev, ci):
    nxb = SSM_INNER // LANE
    d = 1 if rev else 0
    kofs = SSM_GROUPS if rev else 0
    return [
        pl.BlockSpec((T, GW), lambda g, c: (ci(c), g)),
        pl.BlockSpec((T, LANE), lambda g, c: (ci(c), nxb + g)),
        pl.BlockSpec((T, LANE), lambda g, c: (ci(c), nxb + SSM_GROUPS + g)),
        pl.BlockSpec((None, T, GW), lambda g, c: (d, ci(c), g)),
        pl.BlockSpec((None, T, GW), lambda g, c: (d, ci(c), g)),
        pl.BlockSpec((None, T, SSM_REP), lambda g, c: (kofs + g, ci(c), 0)),
        pl.BlockSpec((None, None, SSM_REP, T), lambda g, c: (kofs + g, ci(c), 0, 0)),
    ]


def _ssd_scan(xbc, dtw, cumw, cumk, cumtk, rev, name):
    s = xbc.shape[0]
    nc = s // T
    last = 0 if rev else T - 1
    ci = (lambda c: nc - 1 - c) if rev else (lambda c: c)

    def body(x_ref, b_ref, c_ref, dtw_ref, cumw_ref, cum_ref, cumt_ref, y_ref, st_ref, state):
        @pl.when(pl.program_id(1) == 0)
        def _():
            state[...] = jnp.zeros_like(state)

        bm = b_ref[...]
        cm = c_ref[...].astype(BF16)
        cb = _dot(cm, bm.astype(BF16), 1, 1)
        bt = bm.T.astype(BF16)
        low = _low_lanes()
        for p in range(PAIRS):
            ls = slice(p * LANE, (p + 1) * LANE)
            acum = cumw_ref[:, ls]
            u = x_ref[:, ls] * dtw_ref[:, ls]
            tot = cumw_ref[last:last + 1, ls]
            m = [(cb * _decay_mats(cum_ref[:, r:r + 1], cumt_ref[r:r + 1, :], rev)).astype(BF16) for r in (2 * p, 2 * p + 1)]
            st = state[p]
            st_ref[p] = st
            yd = _dot(jnp.concatenate(m, axis=1), _block_diag(u.astype(BF16), low))
            yo = jnp.exp(acum) * _dot(cm, st.astype(BF16))
            y_ref[:, ls] = yd + yo
            state[p] = jnp.exp(tot) * st + _dot(bt, (jnp.exp(tot - acum) * u).astype(BF16))

    return pl.pallas_call(
        body, name=name, grid=(SSM_GROUPS, nc), in_specs=_scan_specs(nc, rev, ci),
        out_specs=[
            pl.BlockSpec((T, GW), lambda g, c: (ci(c), g)),
            pl.BlockSpec((None, PAIRS, SSM_STATE, LANE), lambda g, c: (ci(c), g, 0, 0)),
        ],
        out_shape=[jax.ShapeDtypeStruct((s, SSM_INNER), F32), jax.ShapeDtypeStruct((nc, SSM_HEADS // 2, SSM_STATE, LANE), F32)],
        scratch_shapes=[pltpu.VMEM((PAIRS, SSM_STATE, LANE), F32)],
        compiler_params=_cp("parallel", "arbitrary"),
    )(xbc, xbc, xbc, dtw, cumw, cumk, cumtk)


def _ssd_scan_bwd(xbc, dtw, cumw, cumk, cumtk, dy, states, dvec, prev, rev, name):
    s = xbc.shape[0]
    nc = s // T
    last = 0 if rev else T - 1
    ci = (lambda c: c) if rev else (lambda c: nc - 1 - c)
    has_prev = prev is not None

    def body(*refs):
        x_ref, b_ref, c_ref, dtw_ref, cumw_ref, cum_ref, cumt_ref, dy_ref, st_ref, dv_ref = refs[:10]
        refs = refs[10:]
        if has_prev:
            pdx, pdb, pdc = refs[:3]
            refs = refs[3:]
        dx_ref, db_ref, dc_ref, ddt_ref, dda_ref, dstate, rs_buf, in_buf, k_buf = refs

        @pl.when(pl.program_id(1) == 0)
        def _():
            dstate[...] = jnp.zeros_like(dstate)

        rs_buf[...] = jnp.zeros_like(rs_buf)
        in_buf[...] = jnp.zeros_like(in_buf)
        k_buf[...] = jnp.zeros_like(k_buf)
        bm = b_ref[...].astype(BF16)
        cm = c_ref[...].astype(BF16)
        cbt = _dot(bm, cm, 1, 1)
        cb = _dot(cm, bm, 1, 1)
        ct = c_ref[...].T.astype(BF16)
        after = _tri(not rev)
        before = _tri(rev)
        from_k = before.astype(BF16)
        ri = lax.broadcasted_iota(jnp.int32, (T, T), 0)
        cj = lax.broadcasted_iota(jnp.int32, (T, T), 1)
        strictly_before = (cj > ri) if rev else (cj < ri)
        dcb = jnp.zeros((T, T), F32)
        dc_acc = jnp.zeros((T, SSM_STATE), F32)
        db_acc = jnp.zeros((T, SSM_STATE), F32)
        low = _low_lanes()
        ri2 = lax.broadcasted_iota(jnp.int32, (LANE, LANE), 0)
        cj2 = lax.broadcasted_iota(jnp.int32, (LANE, LANE), 1)
        halves = ((ri2 < SSM_HEAD_DIM) == (cj2 == 0)) & (cj2 < 2)
        halves = halves.astype(BF16)

        def head_sums(v):
            hi = v.astype(BF16)
            lo = (v - hi.astype(F32)).astype(BF16)
            return _dot(hi, halves) + _dot(lo, halves)

        for p in range(PAIRS):
            ls = slice(p * LANE, (p + 1) * LANE)
            c2 = slice(2 * p, 2 * p + 2)
            lm, lmt = [], []
            for r in (2 * p, 2 * p + 1):
                acol = cum_ref[:, r:r + 1]
                arow = cumt_ref[r:r + 1, :]
                lm.append(jnp.where(after, jnp.exp(jnp.where(after, acol - arow, 0.0)), 0.0))
                lmt.append(jnp.where(before, jnp.exp(jnp.where(before, arow - acol, 0.0)), 0.0))
            acum = cumw_ref[:, ls]
            tot = cumw_ref[last:last + 1, ls]
            dtl = dtw_ref[:, ls]
            xl = x_ref[:, ls]
            u = xl * dtl
            ub = u.astype(BF16)
            dyl = dy_ref[:, ls]
            dyb = dyl.astype(BF16)
            st = st_ref[p]
            stb = st.astype(BF16)
            dst = dstate[p]
            dstb = dst.astype(BF16)
            dec = jnp.exp(tot - acum)
            eac = jnp.exp(acum)
            etot = jnp.exp(tot)
            du_off = dec * _dot(bm, dstb)
            mt = jnp.concatenate([(cbt * lmt[0]).astype(BF16), (cbt * lmt[1]).astype(BF16)], axis=1)
            du = _dot(mt, _block_diag(dyb, low)) + du_off
            zero = jnp.zeros_like(dyb)
            gl = [_dot(jnp.where(low, dyb, zero), ub, 1, 1) * lm[0], _dot(jnp.where(low, zero, dyb), ub, 1, 1) * lm[1]]
            dcb = dcb + gl[0] + gl[1]
            dc_acc = dc_acc + _dot((eac * dyl).astype(BF16), stb, 1, 1)
            db_acc = db_acc + _dot((dec * u).astype(BF16), dstb, 1, 1)
            w = jnp.concatenate([(gl[0] * cb).astype(BF16), (gl[1] * cb).astype(BF16)], axis=1)
            crossing = _dot(from_k, w)
            for j in range(2):
                cr = jnp.where(strictly_before, crossing[:, j * T:(j + 1) * T], 0.0)
                in_buf[:, 2 * p + j:2 * p + j + 1] = jnp.sum(cr, axis=1, keepdims=True)
            y_off = eac * _dot(cm, stb)
            udu = u * du_off
            rs_buf[:, c2] = head_sums(dyl * y_off - udu)[:, 0:2]
            col = jnp.sum(dst * (etot * st) + udu, axis=0, keepdims=True)
            k_buf[0:1, c2] = head_sums(jnp.broadcast_to(col, (8, LANE)))[0:1, 0:2]
            ddt_ref[:, c2] = head_sums(du * xl)[:, 0:2]
            dx = du * dtl
            if has_prev:
                dx = dx + pdx[:, ls]
            else:
                dx = dx + dyl * dv_ref[:, ls]
            dx_ref[:, ls] = dx
            dstate[p] = etot * dst + _dot(ct, (eac * dyl).astype(BF16))
        dda = in_buf[...] + _dot_exact(from_k, rs_buf[...]) + k_buf[0:1, :]
        dda_ref[...] = dda[:, :SSM_REP]
        dcbb = dcb.astype(BF16)
        dc = dc_acc + _dot(dcbb, bm)
        db = db_acc + _dot(dcbb, cm, 0, 0)
        if has_prev:
            dc = dc + pdc[...]
            db = db + pdb[...]
        dc_ref[...] = dc
        db_ref[...] = db

    xspec = pl.BlockSpec((T, GW), lambda g, c: (ci(c), g))
    gspec = pl.BlockSpec((T, LANE), lambda g, c: (ci(c), g))
    in_specs = _scan_specs(nc, rev, ci) + [
        xspec,
        pl.BlockSpec((None, PAIRS, SSM_STATE, LANE), lambda g, c: (ci(c), g, 0, 0)),
        pl.BlockSpec((1, GW), lambda g, c: (0, g)),
    ]
    args = [xbc, xbc, xbc, dtw, cumw, cumk, cumtk, dy, states, dvec]
    if has_prev:
        in_specs += [xspec, gspec, gspec]
        args += list(prev)
    ospec8 = pl.BlockSpec((None, T, SSM_REP), lambda g, c: (g, ci(c), 0))
    return pl.pallas_call(
        body, name=name, grid=(SSM_GROUPS, nc), in_specs=in_specs,
        out_specs=[xspec, gspec, gspec, ospec8, ospec8],
        out_shape=[jax.ShapeDtypeStruct((s, SSM_INNER), F32), jax.ShapeDtypeStruct((s, SSM_BC), F32), jax.ShapeDtypeStruct((s, SSM_BC), F32),
                   jax.ShapeDtypeStruct((SSM_GROUPS, s, SSM_REP), F32), jax.ShapeDtypeStruct((SSM_GROUPS, s, SSM_REP), F32)],
        scratch_shapes=[pltpu.VMEM((PAIRS, SSM_STATE, LANE), F32), pltpu.VMEM((T, LANE), F32), pltpu.VMEM((T, LANE), F32),
                        pltpu.VMEM((8, LANE), F32)],
        compiler_params=_cp("parallel", "arbitrary"),
    )(*args)


def _ssd_post(yf, yb, xbc, z, dvec, nw, name):
    s = z.shape[0]
    tm = min(256, s)

    def body(yf_ref, yb_ref, x_ref, z_ref, dv_ref, nw_ref, o_ref):
        ys = yf_ref[...] + yb_ref[...] + dv_ref[...] * x_ref[...]
        yg = ys * _silu(z_ref[...])
        ms = jnp.mean(yg * yg, axis=1, keepdims=True)
        o_ref[...] = (yg * lax.rsqrt(ms + RMS_EPS) * nw_ref[...]).astype(BF16)

    rs = _row_spec(tm, SSM_INNER)
    vs = _vec_spec(SSM_INNER)
    return pl.pallas_call(
        body, name=name, grid=(s // tm,), in_specs=[rs, rs, rs, rs, vs, vs], out_specs=rs,
        out_shape=jax.ShapeDtypeStruct((s, SSM_INNER), BF16), compiler_params=_cp("parallel"),
    )(yf, yb, xbc, z, dvec, nw)


def _ssd_post_bwd(dyn, yf, yb, xbc, z, dvec, nw, name):
    s = z.shape[0]
    tm = min(256, s)

    def body(dyn_ref, yf_ref, yb_ref, x_ref, z_ref, dv_ref, nw_ref, dys_ref, dz_ref, red_ref):
        @pl.when(pl.program_id(0) == 0)
        def _():
            red_ref[...] = jnp.zeros_like(red_ref)

        xv, zv = x_ref[...], z_ref[...]
        ys = yf_ref[...] + yb_ref[...] + dv_ref[...] * xv
        sz = _silu(zv)
        yg = ys * sz
        rstd = lax.rsqrt(jnp.mean(yg * yg, axis=1, keepdims=True) + RMS_EPS)
        yhat = yg * rstd
        dynv = dyn_ref[...]
        dyh = dynv * nw_ref[...]
        dyg = rstd * (dyh - yhat * jnp.mean(dyh * yhat, axis=1, keepdims=True))
        dys = dyg * sz
        dys_ref[...] = dys
        dz_ref[...] = (dyg * ys * _dsilu(zv)).astype(BF16)
        red_ref[0:1, :] += jnp.sum(dynv * yhat, axis=0, keepdims=True)
        red_ref[1:2, :] += jnp.sum(dys * xv, axis=0, keepdims=True)

    rs = _row_spec(tm, SSM_INNER)
    vs = _vec_spec(SSM_INNER)
    return pl.pallas_call(
        body, name=name, grid=(s // tm,), in_specs=[rs, rs, rs, rs, rs, vs, vs],
        out_specs=[rs, rs, _vec_spec(SSM_INNER, 8)],
        out_shape=[jax.ShapeDtypeStruct((s, SSM_INNER), F32), jax.ShapeDtypeStruct((s, SSM_INNER), BF16), jax.ShapeDtypeStruct((8, SSM_INNER), F32)],
        compiler_params=_cp("arbitrary"),
    )(dyn, yf, yb, xbc, z, dvec, nw)


def _dt_bwd(dt_raw, bias, a_log, dt, ddt, dda, name):
    s = dt_raw.shape[0]
    tm = min(1024, s)

    def body(r_ref, b_ref, a_ref, dt_ref, ddt_ref, dda_ref, o_ref, red_ref):
        @pl.when(pl.program_id(0) == 0)
        def _():
            red_ref[...] = jnp.zeros_like(red_ref)

        a = -jnp.exp(a_ref[...])
        ddav = dda_ref[...]
        draw = (ddt_ref[...] + a * ddav) * _sigmoid(r_ref[...] + b_ref[...])
        o_ref[...] = draw.astype(BF16)
        red_ref[0:1, :] += jnp.sum(draw, axis=0, keepdims=True)
        red_ref[1:2, :] += a * jnp.sum(ddav * dt_ref[...], axis=0, keepdims=True)

    rs = _row_spec(tm, LANE)
    vs = _vec_spec(LANE)
    return pl.pallas_call(
        body, name=name, grid=(s // tm,), in_specs=[rs, vs, vs, rs, rs, rs], out_specs=[rs, _vec_spec(LANE, 8)],
        out_shape=[jax.ShapeDtypeStruct((s, LANE), BF16), jax.ShapeDtypeStruct((8, LANE), F32)],
        compiler_params=_cp("arbitrary"),
    )(dt_raw, bias, a_log, dt, ddt, dda)


def _pad_lanes(v):
    v = v.reshape(1, -1)
    return jnp.pad(v, ((0, 0), (0, LANE - v.shape[1])))


def _ssd_prep_weights(w_in, conv_w, conv_b, dt_bias, a_log, d_skip, norm_w, w_out):
    return dict(
        w_z=w_in[:, :SSM_INNER].astype(BF16),
        w_xbc=w_in[:, SSM_INNER:SSM_INNER + SSM_CONV_DIM].astype(BF16),
        w_dt=jnp.pad(w_in[:, SSM_INNER + SSM_CONV_DIM:], ((0, 0), (0, LANE - 2 * SSM_HEADS))).astype(BF16),
        conv_w=conv_w, conv_b=conv_b.reshape(1, -1), bias=_pad_lanes(dt_bias), a_log=_pad_lanes(a_log),
        dvec=jnp.repeat(d_skip, SSM_HEAD_DIM).reshape(1, -1), nw=norm_w.reshape(1, -1), w_out=w_out.astype(BF16),
    )


def _ssd_layer_fwd(h, w, li):
    nm = lambda t: f"b{li}_{t}"
    z = _mm(h, w["w_z"], name=nm("z"))
    xraw = _mm(h, w["w_xbc"], name=nm("xbc"))
    dt_raw = _mm(h, w["w_dt"], name=nm("dt"))
    pre, xbc = _conv_fwd(xraw, w["conv_w"], w["conv_b"], nm("conv"))
    dt, cum, cumt, dtw, cumw = _dt_prep(dt_raw, w["bias"], w["a_log"], nm("dtprep"))
    nc = cumt.shape[0]
    cumk = _by_group(cum)
    cumtk = cumt[:, :2 * SSM_HEADS].reshape(nc, 2 * SSM_GROUPS, SSM_REP, T).transpose(1, 0, 2, 3)
    yf, stf = _ssd_scan(xbc, dtw, cumw, cumk, cumtk, False, nm("scan_f"))
    yb, stb = _ssd_scan(xbc, dtw, cumw, cumk, cumtk, True, nm("scan_b"))
    yn = _ssd_post(yf, yb, xbc, z, w["dvec"], w["nw"], nm("post"))
    out = _mm(yn, w["w_out"], name=nm("out"))
    return out, (z, xraw, dt_raw, pre, xbc, dt, dtw, cumw, cumk, cumtk, yf, stf, yb, stb, yn)


def _ssd_layer_bwd(dy, h, saved, w, li):
    nm = lambda t: f"b{li}_{t}"
    z, xraw, dt_raw, pre, xbc, dt, dtw, cumw, cumk, cumtk, yf, stf, yb, stb, yn = saved
    g_w_out = _mm(yn, dy, ta=True, out_dtype=BF16, name=nm("dwout"))
    dyn = _mm(dy, w["w_out"], tb=True, name=nm("dyn"))
    dys, dz, pred = _ssd_post_bwd(dyn, yf, yb, xbc, z, w["dvec"], w["nw"], nm("postbwd"))
    dx1, db1, dc1, ddt_f, dda_f = _ssd_scan_bwd(xbc, dtw, cumw, cumk, cumtk, dys, stf, w["dvec"], None, False, nm("scanbwd_f"))
    dx, db, dc, ddt_b, dda_b = _ssd_scan_bwd(xbc, dtw, cumw, cumk, cumtk, dys, stb, w["dvec"], (dx1, db1, dc1), True, nm("scanbwd_b"))
    dact = jnp.concatenate([dx, db, dc], axis=1)
    dxraw, cred = _conv_bwd(dact, pre, xraw, w["conv_w"], nm("convbwd"))
    draw, dred = _dt_bwd(dt_raw, w["bias"], w["a_log"], dt, _from_group(ddt_f, ddt_b), _from_group(dda_f, dda_b), nm("dtbwd"))
    dh = _mm(dz, w["w_z"], tb=True, name=nm("dh_z"))
    dh = _mm(dxraw, w["w_xbc"], tb=True, add=dh, name=nm("dh_xbc"))
    dh = _mm(draw, w["w_dt"], tb=True, add=dh, name=nm("dh_dt"))
    g_w_in = jnp.concatenate([_mm(h, dz, ta=True, out_dtype=BF16, name=nm("dwz")), _mm(h, dxraw, ta=True, out_dtype=BF16, name=nm("dwxbc")),
                              _mm(h, draw, ta=True, out_dtype=BF16, name=nm("dwdt"))[:, :2 * SSM_HEADS]], axis=1)
    grads = (g_w_in, cred[:SSM_CONV], cred[SSM_CONV], dred[0, :2 * SSM_HEADS].reshape(2, SSM_HEADS),
             dred[1, :2 * SSM_HEADS].reshape(2, SSM_HEADS), pred[1].reshape(SSM_HEADS, SSM_HEAD_DIM).sum(axis=1), pred[0], g_w_out)
    return dh, grads


B_GRAD_NAMES = ("b_w_in", "b_conv_w", "b_conv_b", "b_dt_bias", "b_a_log", "b_d", "b_norm_w", "b_w_out")


def _local_step(x, tgt, mod, w):
    d = x.shape[1]
    qkv_cols = QKV_COLS
    layers = []
    for i in range(DEPTH):
        j = i // 2
        if i % 2 == 0:
            layers.append((w["a_w_in"][j][:, :qkv_cols].astype(BF16), w["a_w_in"][j][:, qkv_cols:].astype(BF16), w["a_w_out"][j].astype(BF16)))
        else:
            layers.append(_ssd_prep_weights(w["b_w_in"][j], w["b_conv_w"][j], w["b_conv_b"][j], w["b_dt_bias"][j], w["b_a_log"][j],
                                            w["b_d"][j], w["b_norm_w"][j], w["b_w_out"][j]))
    saved = []
    for i in range(DEPTH):
        shift, scale, gate = mod[i:i + 1, :d], mod[i:i + 1, d:2 * d], mod[i:i + 1, 2 * d:]
        h = _modulate(x, scale, shift, f"l{i}_mod")
        if i % 2 == 0:
            out, sv = _attn_layer_fwd(h, *layers[i], i)
        else:
            out, sv = _ssd_layer_fwd(h, layers[i], i)
        xn = _resid_ln(x, out, gate, w["ln_g"][i:i + 1], w["ln_b"][i:i + 1], f"l{i}_ln")
        saved.append((x, h, out, sv))
        x = xn
    dx, lred = _loss_grad(x, tgt, "loss")
    loss = 0.5 * jnp.sum(lred[0]) / d
    dmod, g_ln_g, g_ln_b = [None] * DEPTH, [None] * DEPTH, [None] * DEPTH
    ga_in, ga_out = [None, None], [None, None]
    gb = [None, None]
    for i in reversed(range(DEPTH)):
        j = i // 2
        xi, h, out, sv = saved[i]
        scale, gate = mod[i:i + 1, d:2 * d], mod[i:i + 1, 2 * d:]
        du, dy, red = _resid_ln_bwd(xi, out, dx, gate, w["ln_g"][i:i + 1], f"l{i}_lnbwd")
        g_ln_g[i], g_ln_b[i] = red[1], red[2]
        if i % 2 == 0:
            dh, ga_in[j], ga_out[j] = _attn_layer_bwd(dy, h, sv, *layers[i], i)
        else:
            dh, gb[j] = _ssd_layer_bwd(dy, h, sv, layers[i], i)
        dx, red2 = _modulate_bwd(du, dh, xi, scale, f"l{i}_modbwd")
        dmod[i] = jnp.concatenate([red2[1], red2[0], red[0]])
    grads = {"ln_g": jnp.stack(g_ln_g), "ln_b": jnp.stack(g_ln_b), "a_w_in": jnp.stack(ga_in), "a_w_out": jnp.stack(ga_out)}
    for k, n in enumerate(B_GRAD_NAMES):
        grads[n] = jnp.stack([gb[0][k], gb[1][k]])
    return loss, dx, jnp.stack(dmod), grads


MESH = pl.DeviceIdType.MESH
ANY = pl.BlockSpec(memory_space=pl.ANY)
N_DEV = 8
N_SHARD = 4


def _flip(v, bit):
    return 1 - v if bit else v


def _all_gather8(v, name):
    def body(v_ref, o_ref, send_sems, recv_sems, local_sem):
        x, y, c = lax.axis_index("x"), lax.axis_index("y"), lax.axis_index("c")
        me = 4 * x + 2 * y + c
        local = pltpu.make_async_copy(v_ref, o_ref.at[me], local_sem)
        local.start()
        copies = []
        for k in range(1, N_DEV):
            peer = (_flip(x, k & 4), _flip(y, k & 2), _flip(c, k & 1))
            copies.append(pltpu.make_async_remote_copy(
                src_ref=v_ref, dst_ref=o_ref.at[me], send_sem=send_sems.at[k - 1], recv_sem=recv_sems.at[k - 1],
                device_id=peer, device_id_type=MESH))
        for cp in copies:
            cp.start()
        for cp in copies:
            cp.wait()
        local.wait()

    return pl.pallas_call(
        body, name=name, in_specs=[ANY], out_specs=ANY, out_shape=jax.ShapeDtypeStruct((N_DEV,) + v.shape, v.dtype),
        scratch_shapes=[pltpu.SemaphoreType.DMA((N_DEV - 1,)), pltpu.SemaphoreType.DMA((N_DEV - 1,)), pltpu.SemaphoreType.DMA],
    )(v)


def _transpose_shards(src, name):
    def body(s_ref, o_ref, send_sems, recv_sems, local_sem):
        x, y, c = lax.axis_index("x"), lax.axis_index("y"), lax.axis_index("c")
        m = 2 * x + y
        local = pltpu.make_async_copy(s_ref.at[m], o_ref.at[m], local_sem)
        local.start()
        copies = []
        for k in range(1, N_SHARD):
            px, py = _flip(x, k & 2), _flip(y, k & 1)
            copies.append(pltpu.make_async_remote_copy(
                src_ref=s_ref.at[2 * px + py], dst_ref=o_ref.at[m], send_sem=send_sems.at[k - 1], recv_sem=recv_sems.at[k - 1],
                device_id=(px, py, c), device_id_type=MESH))
        for cp in copies:
            cp.start()
        for cp in copies:
            cp.wait()
        local.wait()

    return pl.pallas_call(
        body, name=name, in_specs=[ANY], out_specs=ANY, out_shape=jax.ShapeDtypeStruct(src.shape, src.dtype),
        scratch_shapes=[pltpu.SemaphoreType.DMA((N_SHARD - 1,)), pltpu.SemaphoreType.DMA((N_SHARD - 1,)), pltpu.SemaphoreType.DMA],
    )(src)


def _gather_shards(src, name):
    rows = src.shape[0]
    half = rows // 2
    n_ici = N_SHARD - 1

    def body(s_ref, o_ref, send_sems, recv_sems, local_sem):
        x, y, c = lax.axis_index("x"), lax.axis_index("y"), lax.axis_index("c")
        m = 2 * x + y
        sibling = (x, y, 1 - c)
        my_half = pl.ds(pl.multiple_of(c * half, 16), half)
        its_half = pl.ds(pl.multiple_of((1 - c) * half, 16), half)
        local = pltpu.make_async_copy(s_ref, o_ref.at[m], local_sem)
        local.start()
        chips = [(_flip(x, k & 2), _flip(y, k & 1)) for k in range(1, N_SHARD)]

        def copy(sem, src_ref, dst_ref, to):
            return pltpu.make_async_remote_copy(src_ref=src_ref, dst_ref=dst_ref, send_sem=send_sems.at[sem],
                                                recv_sem=recv_sems.at[sem], device_id=to, device_id_type=MESH)

        first = [copy(i, s_ref.at[my_half], o_ref.at[m, my_half], (px, py, c)) for i, (px, py) in enumerate(chips)]
        for cp in first:
            cp.start()
        passed = []
        for i, (px, py) in enumerate(chips):
            landed = o_ref.at[2 * px + py, my_half]
            copy(i, landed, landed, (px, py, c)).wait_recv()
            passed.append(copy(n_ici + i, landed, landed, sibling))
            passed[-1].start()
        for i, (px, py) in enumerate(chips):
            from_sibling = o_ref.at[2 * px + py, its_half]
            copy(n_ici + i, from_sibling, from_sibling, sibling).wait_recv()
        for cp in first + passed:
            cp.wait_send()
        local.wait()

    return pl.pallas_call(
        body, name=name, in_specs=[ANY], out_specs=ANY, out_shape=jax.ShapeDtypeStruct((N_SHARD,) + src.shape, src.dtype),
        scratch_shapes=[pltpu.SemaphoreType.DMA((2 * n_ici,)), pltpu.SemaphoreType.DMA((2 * n_ici,)), pltpu.SemaphoreType.DMA],
    )(src)


def _swap_sibling(v, name):
    def body(v_ref, o_ref, send_sem, recv_sem):
        x, y, c = lax.axis_index("x"), lax.axis_index("y"), lax.axis_index("c")
        cp = pltpu.make_async_remote_copy(src_ref=v_ref, dst_ref=o_ref, send_sem=send_sem, recv_sem=recv_sem,
                                          device_id=(x, y, 1 - c), device_id_type=MESH)
        cp.start()
        cp.wait()

    return pl.pallas_call(
        body, name=name, in_specs=[ANY], out_specs=ANY, out_shape=jax.ShapeDtypeStruct(v.shape, v.dtype),
        scratch_shapes=[pltpu.SemaphoreType.DMA, pltpu.SemaphoreType.DMA],
    )(v)


def _sum_slots(a, name):
    n, r, cdim = a.shape
    tm = max(t for t in range(16, 641, 16) if r % t == 0)

    def body(a_ref, o_ref):
        acc = a_ref[0].astype(F32)
        for k in range(1, n):
            acc = acc + a_ref[k].astype(F32)
        o_ref[...] = acc

    return pl.pallas_call(
        body, name=name, grid=(r // tm,), in_specs=[pl.BlockSpec((n, tm, cdim), lambda i: (0, i, 0))],
        out_specs=pl.BlockSpec((tm, cdim), lambda i: (i, 0)), out_shape=jax.ShapeDtypeStruct((r, cdim), F32),
        compiler_params=_cp("parallel"),
    )(a)


def _silu_rows(v, name):
    def body(v_ref, o_ref):
        o_ref[...] = _silu(v_ref[...])

    return pl.pallas_call(body, name=name, out_shape=jax.ShapeDtypeStruct(v.shape, F32))(v)


PACK_COLS = 1024
PACK_TILE = 256


def _adamw(w, g1, g2, m, v, name):
    r = w.shape[0]
    c1 = 1.0 / (1.0 - ADAM_B1 ** ADAM_STEP)
    c2 = 1.0 / (1.0 - ADAM_B2 ** ADAM_STEP)

    def body(w_ref, g1_ref, g2_ref, m_ref, v_ref, g_ref, d_ref, nm_ref, nv_ref):
        g = g1_ref[...] + g2_ref[...]
        mn = ADAM_B1 * m_ref[...] + (1.0 - ADAM_B1) * g
        vn = ADAM_B2 * v_ref[...] + (1.0 - ADAM_B2) * (g * g)
        g_ref[...] = g
        nm_ref[...] = mn
        nv_ref[...] = vn
        d_ref[...] = -ADAM_LR * ((mn * c1) / (jnp.sqrt(vn * c2) + ADAM_EPS) + ADAM_WD * w_ref[...])

    spec = pl.BlockSpec((PACK_TILE, PACK_COLS), lambda i: (i, 0))
    return pl.pallas_call(
        body, name=name, grid=(r // PACK_TILE,), in_specs=[spec] * 5, out_specs=[spec] * 4,
        out_shape=[jax.ShapeDtypeStruct(w.shape, F32)] * 4, compiler_params=_cp("parallel"),
    )(w, g1, g2, m, v)


def _rows(a):
    f = a.reshape(-1)
    pad = (-f.shape[0]) % PACK_COLS
    if pad:
        f = jnp.pad(f, (0, pad))
    return f.reshape(-1, PACK_COLS)


def _nrows(shape):
    return -(-int(np.prod(shape)) // PACK_COLS)


def _pack(parts, total_rows=None):
    p = jnp.concatenate([_rows(a) for a in parts], axis=0)
    if total_rows is not None and total_rows > p.shape[0]:
        p = jnp.pad(p, ((0, total_rows - p.shape[0]), (0, 0)))
    return p


def _unpack(p, shapes):
    out, r0 = [], 0
    for shp in shapes:
        n = int(np.prod(shp))
        nr = _nrows(shp)
        out.append(p[r0:r0 + nr].reshape(-1)[:n].reshape(shp))
        r0 += nr
    return out


def _unshard_cols(g):
    return jnp.concatenate([g[k] for k in range(N_SHARD)], axis=-1)


def _shard_cols(a):
    n = a.shape[-1] // N_SHARD
    return jnp.stack([a[..., k * n:(k + 1) * n] for k in range(N_SHARD)])


def _unshard_rows(g):
    return jnp.concatenate([g[k] for k in range(N_SHARD)], axis=1)


def _shard_rows(a):
    n = a.shape[1] // N_SHARD
    return jnp.stack([a[:, k * n:(k + 1) * n] for k in range(N_SHARD)])


W_NAMES = ("ada_w", "ada_b", "ln_g", "ln_b", "a_w_in", "a_w_out", "b_w_in", "b_conv_w", "b_conv_b", "b_dt_bias", "b_a_log", "b_d",
           "b_norm_w", "b_w_out")
BIG = ("a_w_in", "a_w_out", "b_w_in", "b_w_out")
PACK_ORDER = BIG + ("ada_w", "ada_b", "ln_g", "ln_b", "b_conv_w", "b_conv_b", "b_dt_bias", "b_a_log", "b_d", "b_norm_w")


def kernel(x, c, ada_w, ada_b, ln_g, ln_b, a_w_in, a_w_out, b_w_in, b_conv_w, b_conv_b, b_dt_bias, b_a_log, b_d, b_norm_w, b_w_out, loss_target, m_ada_w, m_ada_b, m_ln_g, m_ln_b, m_a_w_in, m_a_w_out, m_b_w_in, m_b_conv_w, m_b_conv_b, m_b_dt_bias, m_b_a_log, m_b_d, m_b_norm_w, m_b_w_out, v_ada_w, v_ada_b, v_ln_g, v_ln_b, v_a_w_in, v_a_w_out, v_b_w_in, v_b_conv_w, v_b_conv_b, v_b_dt_bias, v_b_a_log, v_b_d, v_b_norm_w, v_b_w_out):
    w = dict(ada_w=ada_w, ada_b=ada_b, ln_g=ln_g, ln_b=ln_b, a_w_in=a_w_in, a_w_out=a_w_out, b_w_in=b_w_in, b_conv_w=b_conv_w,
             b_conv_b=b_conv_b, b_dt_bias=b_dt_bias, b_a_log=b_a_log, b_d=b_d, b_norm_w=b_norm_w, b_w_out=b_w_out)
    mom = dict(ada_w=m_ada_w, ada_b=m_ada_b, ln_g=m_ln_g, ln_b=m_ln_b, a_w_in=m_a_w_in, a_w_out=m_a_w_out, b_w_in=m_b_w_in,
               b_conv_w=m_b_conv_w, b_conv_b=m_b_conv_b, b_dt_bias=m_b_dt_bias, b_a_log=m_b_a_log, b_d=m_b_d, b_norm_w=m_b_norm_w,
               b_w_out=m_b_w_out)
    var = dict(ada_w=v_ada_w, ada_b=v_ada_b, ln_g=v_ln_g, ln_b=v_ln_b, a_w_in=v_a_w_in, a_w_out=v_a_w_out, b_w_in=v_b_w_in,
               b_conv_w=v_b_conv_w, b_conv_b=v_b_conv_b, b_dt_bias=v_b_dt_bias, b_a_log=v_b_a_log, b_d=v_b_d, b_norm_w=v_b_norm_w,
               b_w_out=v_b_w_out)
    ax, ay, ac = lax.axis_index("x"), lax.axis_index("y"), lax.axis_index("c")
    me = 4 * ax + 2 * ay + ac
    shard = 2 * ax + ay
    d = x.shape[-1]
    dsh = ada_w.shape[-1]

    small_in = (c, b_conv_w, b_conv_b, b_norm_w)
    g0 = _all_gather8(_pack(small_in).reshape(-1, LANE), "gather_small_in").reshape(N_DEV, -1, PACK_COLS)
    per_dev = [_unpack(g0[k], [a.shape for a in small_in]) for k in range(N_DEV)]
    c_all = jnp.concatenate([p[0] for p in per_dev], axis=0)
    conv_w_full, conv_b_full, norm_w_full = (_unshard_cols([per_dev[2 * k][t] for k in range(N_SHARD)]) for t in (1, 2, 3))

    cond = _silu_rows(jnp.pad(c_all, ((0, 8), (0, 0))), "cond")
    bias = lax.dynamic_slice_in_dim(ada_b, shard * dsh, dsh, axis=1)
    part = jnp.stack([_mm(cond, ada_w[i], add=jnp.broadcast_to(bias[i], (16, dsh)), name=f"mod{i}")[:N_DEV] for i in range(DEPTH)])
    g1 = _all_gather8(part.reshape(-1, LANE), "gather_mod").reshape(N_DEV, DEPTH, N_DEV, dsh)
    mod_all = _unshard_cols([g1[2 * k] for k in range(N_SHARD)])
    mod = lax.dynamic_index_in_dim(mod_all, me, axis=1, keepdims=False)

    gw = _gather_shards(_pack([w[n] for n in BIG]).astype(BF16), "gather_weights")
    big_sh = [_unpack(gw[k], [w[n].shape for n in BIG]) for k in range(N_SHARD)]
    full = dict(
        ln_g=ln_g, ln_b=ln_b, b_dt_bias=b_dt_bias, b_a_log=b_a_log, b_d=b_d,
        b_conv_w=conv_w_full, b_conv_b=conv_b_full, b_norm_w=norm_w_full,
        a_w_in=_unshard_cols([s[0] for s in big_sh]), a_w_out=_unshard_rows([s[1] for s in big_sh]),
        b_w_in=_unshard_cols([s[2] for s in big_sh]), b_w_out=_unshard_rows([s[3] for s in big_sh]),
    )

    loss, grad_x, dmod, g = _local_step(x[0], loss_target[0], mod, full)

    gsh = (_shard_cols(g["a_w_in"]), _shard_rows(g["a_w_out"]), _shard_cols(g["b_w_in"]), _shard_rows(g["b_w_out"]))
    to_send = jnp.stack([_pack([t[k] for t in gsh]) for k in range(N_SHARD)]).astype(BF16)
    mine = _sum_slots(_transpose_shards(to_send, "scatter_grads"), "sum_shards")
    theirs = _swap_sibling(mine, "swap_grads")

    small_g = (dmod, g["ln_g"], g["ln_b"], g["b_dt_bias"], g["b_a_log"], g["b_d"], g["b_conv_w"], g["b_conv_b"], g["b_norm_w"],
               loss.reshape(1))
    g2 = _all_gather8(_pack(small_g).reshape(-1, LANE), "gather_small_grads")
    tot = _unpack(_sum_slots(g2, "sum_small").reshape(-1, PACK_COLS), [a.shape for a in small_g])
    g_ada_b, g_ln_g, g_ln_b, g_dt_bias, g_a_log, g_d, g_conv_w, g_conv_b, g_norm_w, loss_sum = tot
    dmod_all = g2.reshape(N_DEV, -1)[:, :dmod.size].reshape(N_DEV, DEPTH, 3 * d)
    dmod_mine = lax.dynamic_slice_in_dim(dmod_all, shard * dsh, dsh, axis=2)
    g_ada_w = jnp.stack([_mm(cond, jnp.pad(dmod_mine[:, i], ((0, 8), (0, 0))), ta=True, name=f"dada{i}") for i in range(DEPTH)])
    csh = g_conv_w.shape[-1] // N_SHARD
    nsh = g_norm_w.shape[-1] // N_SHARD
    small_grads = dict(
        ada_w=g_ada_w, ada_b=g_ada_b, ln_g=g_ln_g, ln_b=g_ln_b, b_dt_bias=g_dt_bias, b_a_log=g_a_log, b_d=g_d,
        b_conv_w=lax.dynamic_slice_in_dim(g_conv_w, shard * csh, csh, axis=2),
        b_conv_b=lax.dynamic_slice_in_dim(g_conv_b, shard * csh, csh, axis=1),
        b_norm_w=lax.dynamic_slice_in_dim(g_norm_w, shard * nsh, nsh, axis=1),
    )

    rest = PACK_ORDER[len(BIG):]
    rows = sum(_nrows(w[n].shape) for n in PACK_ORDER)
    rows = -(-rows // PACK_TILE) * PACK_TILE
    pw, pm, pv = (_pack([t[n] for n in PACK_ORDER], rows) for t in (w, mom, var))
    pg1 = _pack([mine] + [small_grads[n] for n in rest], rows)
    pg2 = _pack([theirs], rows)
    outs = [_unpack(p, [w[n].shape for n in PACK_ORDER]) for p in _adamw(pw, pg1, pg2, pm, pv, "adamw")]
    by_name = [dict(zip(PACK_ORDER, o)) for o in outs]
    return (loss_sum.reshape(()), grad_x[None], *[t[n] for t in by_name for n in W_NAMES])
```

```python
import jax
import jax.numpy as jnp
import numpy as np
from jax import lax
from jax.experimental import pallas as pl
from jax.experimental.pallas import tpu as pltpu

F32 = jnp.float32
BF16 = jnp.bfloat16

DEPTH = 4
A_HEADS = 16
A_HEAD_DIM = 64
A_WIDTH = A_HEADS * A_HEAD_DIM
DILATIONS = (1, 4, 16)
A_RADIUS = 64
A_QBLOCK = 128
SSM_HEADS = 32
SSM_HEAD_DIM = 64
SSM_STATE = 128
SSM_GROUPS = 4
SSM_REP = SSM_HEADS // SSM_GROUPS
SSM_CONV = 5
SSM_CHUNK = 128
DEEPNORM_ALPHA = (2 * DEPTH) ** 0.25
LN_EPS = 1e-5
RMS_EPS = 1e-5
ADAM_LR, ADAM_B1, ADAM_B2, ADAM_EPS, ADAM_WD, ADAM_STEP = 0.001, 0.9, 0.999, 1e-08, 0.01, 10
VMEM_LIMIT = 56 * 1024 * 1024
LANE = 128


def _cp(*sem):
    return pltpu.CompilerParams(dimension_semantics=sem, vmem_limit_bytes=VMEM_LIMIT)


def _tile(dim, target):
    if dim <= target:
        return dim
    t = (target // LANE) * LANE
    while dim % t:
        t -= LANE
    return t


def _sigmoid(x):
    return 1.0 / (1.0 + jnp.exp(-x))


def _silu(x):
    return x * _sigmoid(x)


def _dsilu(x):
    s = _sigmoid(x)
    return s * (1.0 + x * (1.0 - s))


def _split3(x):
    a = x.astype(BF16)
    r = x - a.astype(F32)
    b = r.astype(BF16)
    c = (r - b.astype(F32)).astype(BF16)
    return a, b, c


def _dot(a, b, ca=1, cb=0):
    return lax.dot_general(a, b, (((ca,), (cb,)), ((), ())), preferred_element_type=F32)


def _dot_exact(m01, x):
    a, b, c = _split3(x)
    return _dot(m01, a) + _dot(m01, b) + _dot(m01, c)


def _mm(a, b, *, ta=False, tb=False, add=None, out_dtype=F32, name, tm=1024, tn=1024, tk=1024):
    m, k = (a.shape[1], a.shape[0]) if ta else a.shape
    n = b.shape[0] if tb else b.shape[1]
    assert (b.shape[1] if tb else b.shape[0]) == k
    tm, tn, tk = _tile(m, tm), _tile(n, tn), _tile(k, tk)
    nk = k // tk
    has_add = add is not None

    def body(*refs):
        if has_add:
            a_ref, b_ref, c_ref, o_ref, acc = refs
        else:
            a_ref, b_ref, o_ref, acc = refs
        kk = pl.program_id(2)
        part = _dot(a_ref[...].astype(BF16), b_ref[...].astype(BF16), 0 if ta else 1, 1 if tb else 0)

        def finish(r):
            if has_add:
                r = r + c_ref[...]
            o_ref[...] = r.astype(o_ref.dtype)

        if nk == 1:
            finish(part)
            return

        @pl.when(kk == 0)
        def _():
            acc[...] = part

        @pl.when((kk > 0) & (kk < nk - 1))
        def _():
            acc[...] += part

        @pl.when(kk == nk - 1)
        def _():
            finish(acc[...] + part)

    a_spec = pl.BlockSpec((tk, tm), lambda i, j, kk: (kk, i)) if ta else pl.BlockSpec((tm, tk), lambda i, j, kk: (i, kk))
    b_spec = pl.BlockSpec((tn, tk), lambda i, j, kk: (j, kk)) if tb else pl.BlockSpec((tk, tn), lambda i, j, kk: (kk, j))
    in_specs = [a_spec, b_spec]
    args = [a, b]
    if has_add:
        in_specs.append(pl.BlockSpec((tm, tn), lambda i, j, kk: (i, j)))
        args.append(add)
    return pl.pallas_call(
        body, name=name, grid=(m // tm, n // tn, nk), in_specs=in_specs,
        out_specs=pl.BlockSpec((tm, tn), lambda i, j, kk: (i, j)),
        out_shape=jax.ShapeDtypeStruct((m, n), out_dtype),
        scratch_shapes=[pltpu.VMEM((tm, tn) if nk > 1 else (8, LANE), F32)],
        compiler_params=_cp("parallel", "parallel", "arbitrary"),
    )(*args)


ROWS = 512


def _row_spec(tm, d):
    return pl.BlockSpec((tm, d), lambda i: (i, 0))


def _vec_spec(d, rows=1):
    return pl.BlockSpec((rows, d), lambda i: (0, 0))


def _modulate(x, scale, shift, name):
    s, d = x.shape
    tm = min(ROWS, s)

    def body(x_ref, sc_ref, sh_ref, o_ref):
        o_ref[...] = (x_ref[...] * (1.0 + sc_ref[...]) + sh_ref[...]).astype(BF16)

    return pl.pallas_call(
        body, name=name, grid=(s // tm,), in_specs=[_row_spec(tm, d), _vec_spec(d), _vec_spec(d)],
        out_specs=_row_spec(tm, d), out_shape=jax.ShapeDtypeStruct((s, d), BF16), compiler_params=_cp("parallel"),
    )(x, scale, shift)


def _resid_ln(x, y, gate, g, b, name):
    s, d = x.shape
    tm = min(ROWS, s)

    def body(x_ref, y_ref, gt_ref, g_ref, b_ref, o_ref):
        u = DEEPNORM_ALPHA * x_ref[...] + gt_ref[...] * y_ref[...]
        mu = jnp.mean(u, axis=1, keepdims=True)
        uc = u - mu
        var = jnp.mean(uc * uc, axis=1, keepdims=True)
        o_ref[...] = uc * lax.rsqrt(var + LN_EPS) * g_ref[...] + b_ref[...]

    return pl.pallas_call(
        body, name=name, grid=(s // tm,),
        in_specs=[_row_spec(tm, d), _row_spec(tm, d), _vec_spec(d), _vec_spec(d), _vec_spec(d)],
        out_specs=_row_spec(tm, d), out_shape=jax.ShapeDtypeStruct((s, d), F32), compiler_params=_cp("parallel"),
    )(x, y, gate, g, b)


def _resid_ln_bwd(x, y, dxn, gate, g, name):
    s, d = x.shape
    tm = min(ROWS, s)

    def body(x_ref, y_ref, dxn_ref, gt_ref, g_ref, du_ref, dy_ref, red_ref):
        @pl.when(pl.program_id(0) == 0)
        def _():
            red_ref[...] = jnp.zeros_like(red_ref)

        yv = y_ref[...]
        u = DEEPNORM_ALPHA * x_ref[...] + gt_ref[...] * yv
        mu = jnp.mean(u, axis=1, keepdims=True)
        uc = u - mu
        var = jnp.mean(uc * uc, axis=1, keepdims=True)
        rstd = lax.rsqrt(var + LN_EPS)
        xhat = uc * rstd
        dxnv = dxn_ref[...]
        dxh = dxnv * g_ref[...]
        du = rstd * (dxh - jnp.mean(dxh, axis=1, keepdims=True) - xhat * jnp.mean(dxh * xhat, axis=1, keepdims=True))
        du_ref[...] = du
        dy_ref[...] = (du * gt_ref[...]).astype(BF16)
        red_ref[0:1, :] += jnp.sum(du * yv, axis=0, keepdims=True)
        red_ref[1:2, :] += jnp.sum(dxnv * xhat, axis=0, keepdims=True)
        red_ref[2:3, :] += jnp.sum(dxnv, axis=0, keepdims=True)

    return pl.pallas_call(
        body, name=name, grid=(s // tm,),
        in_specs=[_row_spec(tm, d), _row_spec(tm, d), _row_spec(tm, d), _vec_spec(d), _vec_spec(d)],
        out_specs=[_row_spec(tm, d), _row_spec(tm, d), _vec_spec(d, 8)],
        out_shape=[jax.ShapeDtypeStruct((s, d), F32), jax.ShapeDtypeStruct((s, d), BF16), jax.ShapeDtypeStruct((8, d), F32)],
        compiler_params=_cp("arbitrary"),
    )(x, y, dxn, gate, g)


def _modulate_bwd(du, dhs, x, scale, name):
    s, d = x.shape
    tm = min(ROWS, s)
    n = len(dhs)

    def body(*refs):
        du_ref, dh_refs, (x_ref, sc_ref, dx_ref, red_ref) = refs[0], refs[1:1 + n], refs[1 + n:]

        @pl.when(pl.program_id(0) == 0)
        def _():
            red_ref[...] = jnp.zeros_like(red_ref)

        dhv = dh_refs[0][...]
        for t in dh_refs[1:]:
            dhv = dhv + t[...]
        dx_ref[...] = DEEPNORM_ALPHA * du_ref[...] + dhv * (1.0 + sc_ref[...])
        red_ref[0:1, :] += jnp.sum(dhv * x_ref[...], axis=0, keepdims=True)
        red_ref[1:2, :] += jnp.sum(dhv, axis=0, keepdims=True)

    return pl.pallas_call(
        body, name=name, grid=(s // tm,),
        in_specs=[_row_spec(tm, d)] * (n + 2) + [_vec_spec(d)],
        out_specs=[_row_spec(tm, d), _vec_spec(d, 8)],
        out_shape=[jax.ShapeDtypeStruct((s, d), F32), jax.ShapeDtypeStruct((8, d), F32)],
        compiler_params=_cp("arbitrary"),
    )(du, *dhs, x, scale)


def _loss_grad(xf, tgt, name):
    s, d = xf.shape
    tm = min(ROWS, s)

    def body(x_ref, t_ref, dx_ref, red_ref):
        @pl.when(pl.program_id(0) == 0)
        def _():
            red_ref[...] = jnp.zeros_like(red_ref)

        e = x_ref[...] - t_ref[...]
        dx_ref[...] = e * (1.0 / d)
        red_ref[0:1, :] += jnp.sum(e * e, axis=0, keepdims=True)

    return pl.pallas_call(
        body, name=name, grid=(s // tm,), in_specs=[_row_spec(tm, d), _row_spec(tm, d)],
        out_specs=[_row_spec(tm, d), _vec_spec(d, 8)],
        out_shape=[jax.ShapeDtypeStruct((s, d), F32), jax.ShapeDtypeStruct((8, d), F32)],
        compiler_params=_cp("arbitrary"),
    )(xf, tgt)


QKV_COLS = 3 * 3 * A_WIDTH


SLOPES = tuple(float(2.0 ** (-8.0 * (h + 1.0) / A_HEADS)) for h in range(A_HEADS))
FAR = 1e30
HEAD_COLS = tuple(slice(h * A_HEAD_DIM, (h + 1) * A_HEAD_DIM) for h in range(A_HEADS))


def _band_dist(n, length, dil, span_rows):
    shape = (2 * A_QBLOCK, A_QBLOCK) if span_rows else (A_QBLOCK, 2 * A_QBLOCK)
    r = lax.broadcasted_iota(jnp.int32, shape, 0)
    c = lax.broadcasted_iota(jnp.int32, shape, 1)
    sp, ce = (r, c) if span_rows else (c, r)
    delta = sp - A_RADIUS - ce
    pos = n * A_QBLOCK - A_RADIUS + sp
    valid = (jnp.abs(delta) <= A_RADIUS) & (pos >= 0) & (pos < length)
    return jnp.where(valid, jnp.abs(delta).astype(F32) * float(dil), FAR)


def _span_specs(col, nb64):
    def mk(i):
        return pl.BlockSpec((64, A_WIDTH), lambda r, n: (r * nb64 + jnp.clip(2 * n - 1 + i, 0, nb64 - 1), col))
    return [mk(i) for i in range(4)]


def _to_residue(t, dil):
    if dil == 1:
        return t
    s, c = t.shape
    return t.reshape(s // dil, dil, c).transpose(1, 0, 2).reshape(s, c)


def _from_residue(t, dil):
    if dil == 1:
        return t
    s, c = t.shape
    return t.reshape(dil, s // dil, c).transpose(1, 0, 2).reshape(s, c)


def _cat(refs):
    return jnp.concatenate([t[...] for t in refs], axis=0)


def _head_expander():
    r = lax.broadcasted_iota(jnp.int32, (A_HEADS, A_WIDTH), 0)
    c = lax.broadcasted_iota(jnp.int32, (A_HEADS, A_WIDTH), 1)
    return ((c >= r * A_HEAD_DIM) & (c < (r + 1) * A_HEAD_DIM)).astype(BF16)


def _to_lanes(x16, e):
    a, b, c = _split3(x16)
    return _dot(a, e) + _dot(b, e) + _dot(c, e)


def _per_head_sum(x, e):
    a, b, c = _split3(x)
    return _dot(a, e, 1, 1) + _dot(b, e, 1, 1) + _dot(c, e, 1, 1)


def _attn_fwd(qkv, g, name):
    s = qkv.shape[0]
    dil = DILATIONS[g]
    length = s // dil
    nblk = length // A_QBLOCK

    def body(q_ref, k0, k1, k2, k3, v0, v1, v2, v3, o_ref, l_ref):
        dist = _band_dist(pl.program_id(1), length, dil, False)
        kk = _cat((k0, k1, k2, k3))
        vv = _cat((v0, v1, v2, v3))
        for h, cs in enumerate(HEAD_COLS):
            sc = _dot(q_ref[:, cs], kk[:, cs], 1, 1) * 0.125 - SLOPES[h] * dist
            m = jnp.max(sc, axis=1, keepdims=True)
            p = jnp.exp(sc - m)
            z = jnp.sum(p, axis=1, keepdims=True)
            o_ref[:, cs] = _dot(p.astype(BF16), vv[:, cs]) / z
            l_ref[:, h:h + 1] = m + jnp.log(z)

    qspec = pl.BlockSpec((A_QBLOCK, A_WIDTH), lambda r, n: (r * nblk + n, 0))
    lspec = pl.BlockSpec((A_QBLOCK, A_HEADS), lambda r, n: (r * nblk + n, 0))
    return pl.pallas_call(
        body, name=name, grid=(dil, nblk), in_specs=[qspec] + _span_specs(1, 2 * nblk) + _span_specs(2, 2 * nblk),
        out_specs=[qspec, lspec],
        out_shape=[jax.ShapeDtypeStruct((s, A_WIDTH), F32), jax.ShapeDtypeStruct((s, A_HEADS), F32)],
        compiler_params=_cp("parallel", "parallel"),
    )(*([qkv] * 9))


def _attn_merge(os_, ls_, gate, name):
    s, w = gate.shape
    tm = min(ROWS, s)

    def body(o0, o1, o2, l0, l1, l2, g_ref, y_ref, o_ref, l_ref):
        a, b, c = l0[...], l1[...], l2[...]
        m = jnp.maximum(jnp.maximum(a, b), c)
        ea, eb, ec = jnp.exp(a - m), jnp.exp(b - m), jnp.exp(c - m)
        z = ea + eb + ec
        l_ref[...] = m + jnp.log(z)
        e = _head_expander()
        o = _to_lanes(ea / z, e) * o0[...] + _to_lanes(eb / z, e) * o1[...] + _to_lanes(ec / z, e) * o2[...]
        o_ref[...] = o
        y_ref[...] = (o * _silu(g_ref[...])).astype(BF16)

    rs = _row_spec(tm, w)
    ls = _row_spec(tm, A_HEADS)
    return pl.pallas_call(
        body, name=name, grid=(s // tm,), in_specs=[rs] * 3 + [ls] * 3 + [rs], out_specs=[rs, rs, ls],
        out_shape=[jax.ShapeDtypeStruct((s, w), BF16), jax.ShapeDtypeStruct((s, w), F32), jax.ShapeDtypeStruct((s, A_HEADS), F32)],
        compiler_params=_cp("parallel"),
    )(*os_, *ls_, gate)


def _attn_gate_bwd(dyy, o, gate, name):
    s, w = gate.shape
    tm = min(ROWS, s)

    def body(dy_ref, o_ref, g_ref, do_ref, dg_ref, dl_ref):
        dyv, ov, gv = dy_ref[...], o_ref[...], g_ref[...]
        do = dyv * _silu(gv)
        do_ref[...] = do.astype(BF16)
        dg_ref[...] = (dyv * ov * _dsilu(gv)).astype(BF16)
        dl_ref[...] = _per_head_sum(do * ov, _head_expander())

    rs = _row_spec(tm, w)
    return pl.pallas_call(
        body, name=name, grid=(s // tm,), in_specs=[rs] * 3, out_specs=[rs, rs, _row_spec(tm, A_HEADS)],
        out_shape=[jax.ShapeDtypeStruct((s, w), BF16), jax.ShapeDtypeStruct((s, w), BF16), jax.ShapeDtypeStruct((s, A_HEADS), F32)],
        compiler_params=_cp("parallel"),
    )(dyy, o, gate)


def _attn_bwd(qkv, do, lse, delta, g, name):
    s = qkv.shape[0]
    dil = DILATIONS[g]
    length = s // dil
    nblk = length // A_QBLOCK

    def rows(t16):
        return jnp.pad(t16.reshape(dil, length, A_HEADS).transpose(0, 2, 1), ((0, 0), (0, 0), (A_RADIUS, A_RADIUS)))

    def body(q0, q1, q2, q3, k0, k1, k2, k3, v0, v1, v2, v3, d0, d1, d2, d3, lc_ref, ec_ref, la, lb, ea, eb, o_ref):
        dist = _band_dist(pl.program_id(1), length, dil, False)
        qq, kk, vv, dd = _cat((q0, q1, q2, q3)), _cat((k0, k1, k2, k3)), _cat((v0, v1, v2, v3)), _cat((d0, d1, d2, d3))
        qn, kn, vn, dn = _cat((q1, q2)), _cat((k1, k2)), _cat((v1, v2)), _cat((d1, d2))
        lse_r = jnp.concatenate([la[...], lb[...]], axis=1)
        dlt_r = jnp.concatenate([ea[...], eb[...]], axis=1)
        for h, cs in enumerate(HEAD_COLS):
            bias = SLOPES[h] * dist
            p = jnp.exp(_dot(qn[:, cs], kk[:, cs], 1, 1) * 0.125 - bias - lc_ref[:, h:h + 1])
            ds = p * (_dot(dn[:, cs], vv[:, cs], 1, 1) - ec_ref[:, h:h + 1])
            o_ref[:, cs] = (_dot(ds.astype(BF16), kk[:, cs]) * 0.125).astype(BF16)
            pt = jnp.exp(_dot(kn[:, cs], qq[:, cs], 1, 1) * 0.125 - bias - lse_r[h:h + 1, :])
            dst = pt * (_dot(vn[:, cs], dd[:, cs], 1, 1) - dlt_r[h:h + 1, :])
            c0 = h * A_HEAD_DIM
            o_ref[:, A_WIDTH + c0:A_WIDTH + c0 + A_HEAD_DIM] = (_dot(dst.astype(BF16), qq[:, cs]) * 0.125).astype(BF16)
            o_ref[:, 2 * A_WIDTH + c0:2 * A_WIDTH + c0 + A_HEAD_DIM] = _dot(pt.astype(BF16), dd[:, cs]).astype(BF16)

    nb64 = 2 * nblk
    dspecs = _span_specs(0, nb64)
    cspec = pl.BlockSpec((A_QBLOCK, A_HEADS), lambda r, n: (r * nblk + n, 0))
    rspecs = [pl.BlockSpec((None, A_HEADS, A_QBLOCK), lambda r, n: (r, 0, n)), pl.BlockSpec((None, A_HEADS, A_QBLOCK), lambda r, n: (r, 0, n + 1))]
    lse_r, dlt_r = rows(lse), rows(delta)
    return pl.pallas_call(
        body, name=name, grid=(dil, nblk),
        in_specs=_span_specs(0, nb64) + _span_specs(1, nb64) + _span_specs(2, nb64) + dspecs + [cspec, cspec] + rspecs * 2,
        out_specs=pl.BlockSpec((A_QBLOCK, 3 * A_WIDTH), lambda r, n: (r * nblk + n, 0)),
        out_shape=jax.ShapeDtypeStruct((s, 3 * A_WIDTH), BF16),
        compiler_params=_cp("parallel", "parallel"),
    )(*([qkv] * 12), *([do] * 4), lse, delta, lse_r, lse_r, dlt_r, dlt_r)


def _attn_layer_fwd(h, w_qkv, w_gate, w_out, li):
    nm = lambda t: f"a{li}_{t}"
    gate = _mm(h, w_gate, name=nm("gate"))
    hs, qkvs, os_, ls_ = [], [], [], []
    for g, dil in enumerate(DILATIONS):
        hg = _to_residue(h, dil)
        qkv = _mm(hg, w_qkv[:, g * 3 * A_WIDTH:(g + 1) * 3 * A_WIDTH], out_dtype=BF16, name=nm(f"qkv{g}"))
        o, l = _attn_fwd(qkv, g, nm(f"attn{g}"))
        hs.append(hg)
        qkvs.append(qkv)
        os_.append(_from_residue(o, dil))
        ls_.append(_from_residue(l, dil))
    y, o, lse = _attn_merge(os_, ls_, gate, nm("merge"))
    out = _mm(y, w_out, name=nm("out"))
    return out, (hs, qkvs, gate, y, o, lse)


def _attn_layer_bwd(dy, h, saved, w_qkv, w_gate, w_out, li):
    nm = lambda t: f"a{li}_{t}"
    hs, qkvs, gate, y, o, lse = saved
    g_w_out = _mm(y, dy, ta=True, out_dtype=BF16, name=nm("dwout"))
    dyy = _mm(dy, w_out, tb=True, name=nm("dyy"))
    do, dgate, delta = _attn_gate_bwd(dyy, o, gate, nm("gatebwd"))
    dhs, dws = [], []
    for g, dil in enumerate(DILATIONS):
        dqkv = _attn_bwd(qkvs[g], _to_residue(do, dil), _to_residue(lse, dil), _to_residue(delta, dil), g, nm(f"attnbwd{g}"))
        wg = w_qkv[:, g * 3 * A_WIDTH:(g + 1) * 3 * A_WIDTH]
        dws.append(_mm(hs[g], dqkv, ta=True, out_dtype=BF16, name=nm(f"dwqkv{g}")))
        add = _mm(dgate, w_gate, tb=True, name=nm("dh_gate")) if g == 0 else None
        dhs.append(_from_residue(_mm(dqkv, wg, tb=True, add=add, name=nm(f"dh_qkv{g}")), dil))
    g_w_in = jnp.concatenate(dws + [_mm(h, dgate, ta=True, out_dtype=BF16, name=nm("dwgate"))], axis=1)
    return dhs, g_w_in, g_w_out


SSM_INNER = SSM_HEADS * SSM_HEAD_DIM
SSM_BC = SSM_GROUPS * SSM_STATE
SSM_CONV_DIM = SSM_INNER + 2 * SSM_BC
GW = SSM_REP * SSM_HEAD_DIM
T = SSM_CHUNK
HALO = 8


def _conv_specs(tm, tn, s):
    nb8 = s // HALO
    cur = pl.BlockSpec((tm, tn), lambda j, i: (i, j))
    prev = pl.BlockSpec((HALO, tn), lambda j, i: (jnp.maximum(i * (tm // HALO) - 1, 0), j))
    nxt = pl.BlockSpec((HALO, tn), lambda j, i: (jnp.minimum((i + 1) * (tm // HALO), nb8 - 1), j))
    return prev, cur, nxt


def _extend(prev_ref, cur_ref, nxt_ref, i, nrow):
    p = jnp.where(i == 0, 0.0, prev_ref[...])
    n = jnp.where(i == nrow - 1, 0.0, nxt_ref[...])
    return jnp.concatenate([p, cur_ref[...], n], axis=0)


def _shift_rows(ext, off, tm):
    rows = ext.shape[0]
    return pltpu.roll(ext, (-off) % rows, 0)[HALO:HALO + tm]


def _conv_fwd(xraw, w, b, name):
    s, cdim = xraw.shape
    tm, tn = min(256, s), 1024
    nrow = s // tm

    def body(p_ref, c_ref, n_ref, w_ref, b_ref, pre_ref, act_ref):
        ext = _extend(p_ref, c_ref, n_ref, pl.program_id(1), nrow)
        acc = jnp.broadcast_to(b_ref[...], (tm, tn))
        for k in range(SSM_CONV):
            acc = acc + w_ref[k:k + 1, :] * _shift_rows(ext, k - SSM_CONV // 2, tm)
        pre_ref[...] = acc
        act_ref[...] = _silu(acc)

    prev, cur, nxt = _conv_specs(tm, tn, s)
    return pl.pallas_call(
        body, name=name, grid=(cdim // tn, nrow),
        in_specs=[prev, cur, nxt, pl.BlockSpec((SSM_CONV, tn), lambda j, i: (0, j)), pl.BlockSpec((1, tn), lambda j, i: (0, j))],
        out_specs=[cur, cur], out_shape=[jax.ShapeDtypeStruct((s, cdim), F32)] * 2,
        compiler_params=_cp("parallel", "parallel"),
    )(xraw, xraw, xraw, w, b)


def _conv_bwd(dact, pre, xraw, w, name):
    s, cdim = xraw.shape
    tm, tn = min(256, s), 1024
    nrow = s // tm

    def body(dp, dc, dn, pp, pc, pn, xp, xc, xn, w_ref, dx_ref, red_ref):
        i = pl.program_id(1)

        @pl.when(i == 0)
        def _():
            red_ref[...] = jnp.zeros_like(red_ref)

        dpre = _extend(dp, dc, dn, i, nrow) * _dsilu(_extend(pp, pc, pn, i, nrow))
        xext = _extend(xp, xc, xn, i, nrow)
        dcur = dpre[HALO:HALO + tm]
        acc = jnp.zeros((tm, tn), F32)
        for k in range(SSM_CONV):
            off = k - SSM_CONV // 2
            acc = acc + w_ref[k:k + 1, :] * _shift_rows(dpre, -off, tm)
            red_ref[k:k + 1, :] += jnp.sum(dcur * _shift_rows(xext, off, tm), axis=0, keepdims=True)
        red_ref[SSM_CONV:SSM_CONV + 1, :] += jnp.sum(dcur, axis=0, keepdims=True)
        dx_ref[...] = acc.astype(BF16)

    prev, cur, nxt = _conv_specs(tm, tn, s)
    return pl.pallas_call(
        body, name=name, grid=(cdim // tn, nrow),
        in_specs=[prev, cur, nxt] * 3 + [pl.BlockSpec((SSM_CONV, tn), lambda j, i: (0, j))],
        out_specs=[cur, pl.BlockSpec((8, tn), lambda j, i: (0, j))],
        out_shape=[jax.ShapeDtypeStruct((s, cdim), BF16), jax.ShapeDtypeStruct((8, cdim), F32)],
        compiler_params=_cp("parallel", "arbitrary"),
    )(dact, dact, dact, pre, pre, pre, xraw, xraw, xraw, w)


def _tri(lower):
    r = lax.broadcasted_iota(jnp.int32, (T, T), 0)
    c = lax.broadcasted_iota(jnp.int32, (T, T), 1)
    return (r >= c) if lower else (r <= c)


def _softplus(x):
    return jnp.maximum(x, 0.0) + jnp.log(1.0 + jnp.exp(-jnp.abs(x)))


def _dt_prep(dt_raw, bias, a_log, name):
    s = dt_raw.shape[0]
    nc = s // T

    def body(r_ref, b_ref, a_ref, dt_ref, cum_ref, cumt_ref, dtw_ref, cumw_ref):
        dt = _softplus(r_ref[...] + b_ref[...])
        da = dt * (-jnp.exp(a_ref[...]))
        pre = _dot_exact(_tri(True).astype(BF16), da)
        suf = _dot_exact(_tri(False).astype(BF16), da)
        lane = lax.broadcasted_iota(jnp.int32, (T, LANE), 1)
        cum = jnp.where(lane < SSM_HEADS, pre, suf)
        dt_ref[...] = dt
        cum_ref[...] = cum
        cumt_ref[...] = cum.T
        r = lax.broadcasted_iota(jnp.int32, (LANE, SSM_INNER), 0)
        c = lax.broadcasted_iota(jnp.int32, (LANE, SSM_INNER), 1)
        for d in range(2):
            h = r - d * SSM_HEADS
            e = ((c >= h * SSM_HEAD_DIM) & (c < (h + 1) * SSM_HEAD_DIM)).astype(BF16)
            dtw_ref[d] = _to_lanes(dt, e)
            cumw_ref[d] = _to_lanes(cum, e)

    blk = pl.BlockSpec((T, LANE), lambda c: (c, 0))
    vec = pl.BlockSpec((1, LANE), lambda c: (0, 0))
    wide = pl.BlockSpec((2, T, SSM_INNER), lambda c: (0, c, 0))
    return pl.pallas_call(
        body, name=name, grid=(nc,), in_specs=[blk, vec, vec],
        out_specs=[blk, blk, pl.BlockSpec((None, LANE, T), lambda c: (c, 0, 0)), wide, wide],
        out_shape=[jax.ShapeDtypeStruct((s, LANE), F32), jax.ShapeDtypeStruct((s, LANE), F32), jax.ShapeDtypeStruct((nc, LANE, T), F32),
                   jax.ShapeDtypeStruct((2, s, SSM_INNER), F32), jax.ShapeDtypeStruct((2, s, SSM_INNER), F32)],
        compiler_params=_cp("parallel"),
    )(dt_raw, bias, a_log)


def _by_group(t):
    s = t.shape[0]
    return t[:, :2 * SSM_HEADS].reshape(s, 2 * SSM_GROUPS, SSM_REP).transpose(1, 0, 2)


def _from_group(tf, tb):
    s = tf.shape[1]
    t = jnp.concatenate([tf, tb], axis=0).transpose(1, 0, 2).reshape(s, 2 * SSM_HEADS)
    return jnp.pad(t, ((0, 0), (0, LANE - 2 * SSM_HEADS)))


def _decay_mats(acol, arow, rev):
    after = _tri(not rev)
    return jnp.where(after, jnp.exp(jnp.where(after, acol - arow, 0.0)), 0.0)


PAIRS = SSM_REP // 2


def _low_lanes():
    return lax.broadcasted_iota(jnp.int32, (T, LANE), 1) < SSM_HEAD_DIM


def _block_diag(v, low):
    zero = jnp.zeros_like(v)
    return jnp.concatenate([jnp.where(low, v, zero), jnp.where(low, zero, v)], axis=0)


def _scan_specs(nc, rev, ci):
    nxb = SSM_INNER // LANE
    d = 1 if rev else 0
    kofs = SSM_GROUPS if rev else 0
    return [
        pl.BlockSpec((T, GW), lambda g, c: (ci(c), g)),
        pl.BlockSpec((T, LANE), lambda g, c: (ci(c), nxb + g)),
        pl.BlockSpec((T, LANE), lambda g, c: (ci(c), nxb + SSM_GROUPS + g)),
        pl.BlockSpec((None, T, GW), lambda g, c: (d, ci(c), g)),
        pl.BlockSpec((None, T, GW), lambda g, c: (d, ci(c), g)),
        pl.BlockSpec((None, T, SSM_REP), lambda g, c: (kofs + g, ci(c), 0)),
        pl.BlockSpec((None, None, SSM_REP, T), lambda g, c: (kofs + g, ci(c), 0, 0)),
    ]


def _ssd_scan(xbc, dtw, cumw, cumk, cumtk, rev, name):
    s = xbc.shape[0]
    nc = s // T
    last = 0 if rev else T - 1
    ci = (lambda c: nc - 1 - c) if rev else (lambda c: c)

    def body(x_ref, b_ref, c_ref, dtw_ref, cumw_ref, cum_ref, cumt_ref, y_ref, st_ref, state):
        @pl.when(pl.program_id(1) == 0)
        def _():
            state[...] = jnp.zeros_like(state)

        bm = b_ref[...]
        cm = c_ref[...].astype(BF16)
        cb = _dot(cm, bm.astype(BF16), 1, 1)
        bt = bm.T.astype(BF16)
        low = _low_lanes()
        for p in range(PAIRS):
            ls = slice(p * LANE, (p + 1) * LANE)
            acum = cumw_ref[:, ls]
            u = x_ref[:, ls] * dtw_ref[:, ls]
            tot = cumw_ref[last:last + 1, ls]
            m = [(cb * _decay_mats(cum_ref[:, r:r + 1], cumt_ref[r:r + 1, :], rev)).astype(BF16) for r in (2 * p, 2 * p + 1)]
            st = state[p]
            st_ref[p] = st
            yd = _dot(jnp.concatenate(m, axis=1), _block_diag(u.astype(BF16), low))
            yo = jnp.exp(acum) * _dot(cm, st.astype(BF16))
            y_ref[:, ls] = yd + yo
            state[p] = jnp.exp(tot) * st + _dot(bt, (jnp.exp(tot - acum) * u).astype(BF16))

    return pl.pallas_call(
        body, name=name, grid=(SSM_GROUPS, nc), in_specs=_scan_specs(nc, rev, ci),
        out_specs=[
            pl.BlockSpec((T, GW), lambda g, c: (ci(c), g)),
            pl.BlockSpec((None, PAIRS, SSM_STATE, LANE), lambda g, c: (ci(c), g, 0, 0)),
        ],
        out_shape=[jax.ShapeDtypeStruct((s, SSM_INNER), F32), jax.ShapeDtypeStruct((nc, SSM_HEADS // 2, SSM_STATE, LANE), F32)],
        scratch_shapes=[pltpu.VMEM((PAIRS, SSM_STATE, LANE), F32)],
        compiler_params=_cp("parallel", "arbitrary"),
    )(xbc, xbc, xbc, dtw, cumw, cumk, cumtk)


def _ssd_scan_bwd(xbc, dtw, cumw, cumk, cumtk, dy, states, dvec, prev, rev, name):
    s = xbc.shape[0]
    nc = s // T
    last = 0 if rev else T - 1
    ci = (lambda c: c) if rev else (lambda c: nc - 1 - c)
    has_prev = prev is not None

    def body(*refs):
        x_ref, b_ref, c_ref, dtw_ref, cumw_ref, cum_ref, cumt_ref, dy_ref, st_ref, dv_ref = refs[:10]
        refs = refs[10:]
        if has_prev:
            pdx, pdb, pdc = refs[:3]
            refs = refs[3:]
        dx_ref, db_ref, dc_ref, ddt_ref, dda_ref, dstate, rs_buf, in_buf, k_buf = refs

        @pl.when(pl.program_id(1) == 0)
        def _():
            dstate[...] = jnp.zeros_like(dstate)

        rs_buf[...] = jnp.zeros_like(rs_buf)
        in_buf[...] = jnp.zeros_like(in_buf)
        k_buf[...] = jnp.zeros_like(k_buf)
        bm = b_ref[...].astype(BF16)
        cm = c_ref[...].astype(BF16)
        cbt = _dot(bm, cm, 1, 1)
        cb = _dot(cm, bm, 1, 1)
        ct = c_ref[...].T.astype(BF16)
        after = _tri(not rev)
        before = _tri(rev)
        from_k = before.astype(BF16)
        ri = lax.broadcasted_iota(jnp.int32, (T, T), 0)
        cj = lax.broadcasted_iota(jnp.int32, (T, T), 1)
        strictly_before = (cj > ri) if rev else (cj < ri)
        dcb = jnp.zeros((T, T), F32)
        dc_acc = jnp.zeros((T, SSM_STATE), F32)
        db_acc = jnp.zeros((T, SSM_STATE), F32)
        low = _low_lanes()
        ri2 = lax.broadcasted_iota(jnp.int32, (LANE, LANE), 0)
        cj2 = lax.broadcasted_iota(jnp.int32, (LANE, LANE), 1)
        halves = ((ri2 < SSM_HEAD_DIM) == (cj2 == 0)) & (cj2 < 2)
        halves = halves.astype(BF16)

        def head_sums(v):
            hi = v.astype(BF16)
            lo = (v - hi.astype(F32)).astype(BF16)
            return _dot(hi, halves) + _dot(lo, halves)

        for p in range(PAIRS):
            ls = slice(p * LANE, (p + 1) * LANE)
            c2 = slice(2 * p, 2 * p + 2)
            lm, lmt = [], []
            for r in (2 * p, 2 * p + 1):
                acol = cum_ref[:, r:r + 1]
                arow = cumt_ref[r:r + 1, :]
                lm.append(jnp.where(after, jnp.exp(jnp.where(after, acol - arow, 0.0)), 0.0))
                lmt.append(jnp.where(before, jnp.exp(jnp.where(before, arow - acol, 0.0)), 0.0))
            acum = cumw_ref[:, ls]
            tot = cumw_ref[last:last + 1, ls]
            dtl = dtw_ref[:, ls]
            xl = x_ref[:, ls]
            u = xl * dtl
            ub = u.astype(BF16)
            dyl = dy_ref[:, ls]
            dyb = dyl.astype(BF16)
            st = st_ref[p]
            stb = st.astype(BF16)
            dst = dstate[p]
            dstb = dst.astype(BF16)
            dec = jnp.exp(tot - acum)
            eac = jnp.exp(acum)
            etot = jnp.exp(tot)
            du_off = dec * _dot(bm, dstb)
            mt = jnp.concatenate([(cbt * lmt[0]).astype(BF16), (cbt * lmt[1]).astype(BF16)], axis=1)
            du = _dot(mt, _block_diag(dyb, low)) + du_off
            zero = jnp.zeros_like(dyb)
            gl = [_dot(jnp.where(low, dyb, zero), ub, 1, 1) * lm[0], _dot(jnp.where(low, zero, dyb), ub, 1, 1) * lm[1]]
            dcb = dcb + gl[0] + gl[1]
            dc_acc = dc_acc + _dot((eac * dyl).astype(BF16), stb, 1, 1)
            db_acc = db_acc + _dot((dec * u).astype(BF16), dstb, 1, 1)
            w = jnp.concatenate([(gl[0] * cb).astype(BF16), (gl[1] * cb).astype(BF16)], axis=1)
            crossing = _dot(from_k, w)
            for j in range(2):
                cr = jnp.where(strictly_before, crossing[:, j * T:(j + 1) * T], 0.0)
                in_buf[:, 2 * p + j:2 * p + j + 1] = jnp.sum(cr, axis=1, keepdims=True)
            y_off = eac * _dot(cm, stb)
            udu = u * du_off
            rs_buf[:, c2] = head_sums(dyl * y_off - udu)[:, 0:2]
            col = jnp.sum(dst * (etot * st) + udu, axis=0, keepdims=True)
            k_buf[0:1, c2] = head_sums(jnp.broadcast_to(col, (8, LANE)))[0:1, 0:2]
            ddt_ref[:, c2] = head_sums(du * xl)[:, 0:2]
            dx = du * dtl
            if has_prev:
                dx = dx + pdx[:, ls]
            else:
                dx = dx + dyl * dv_ref[:, ls]
            dx_ref[:, ls] = dx
            dstate[p] = etot * dst + _dot(ct, (eac * dyl).astype(BF16))
        dda = in_buf[...] + _dot_exact(from_k, rs_buf[...]) + k_buf[0:1, :]
        dda_ref[...] = dda[:, :SSM_REP]
        dcbb = dcb.astype(BF16)
        dc = dc_acc + _dot(dcbb, bm)
        db = db_acc + _dot(dcbb, cm, 0, 0)
        if has_prev:
            dc = dc + pdc[...]
            db = db + pdb[...]
        dc_ref[...] = dc
        db_ref[...] = db

    xspec = pl.BlockSpec((T, GW), lambda g, c: (ci(c), g))
    gspec = pl.BlockSpec((T, LANE), lambda g, c: (ci(c), g))
    in_specs = _scan_specs(nc, rev, ci) + [
        xspec,
        pl.BlockSpec((None, PAIRS, SSM_STATE, LANE), lambda g, c: (ci(c), g, 0, 0)),
        pl.BlockSpec((1, GW), lambda g, c: (0, g)),
    ]
    args = [xbc, xbc, xbc, dtw, cumw, cumk, cumtk, dy, states, dvec]
    if has_prev:
        in_specs += [xspec, gspec, gspec]
        args += list(prev)
    ospec8 = pl.BlockSpec((None, T, SSM_REP), lambda g, c: (g, ci(c), 0))
    return pl.pallas_call(
        body, name=name, grid=(SSM_GROUPS, nc), in_specs=in_specs,
        out_specs=[xspec, gspec, gspec, ospec8, ospec8],
        out_shape=[jax.ShapeDtypeStruct((s, SSM_INNER), F32), jax.ShapeDtypeStruct((s, SSM_BC), F32), jax.ShapeDtypeStruct((s, SSM_BC), F32),
                   jax.ShapeDtypeStruct((SSM_GROUPS, s, SSM_REP), F32), jax.ShapeDtypeStruct((SSM_GROUPS, s, SSM_REP), F32)],
        scratch_shapes=[pltpu.VMEM((PAIRS, SSM_STATE, LANE), F32), pltpu.VMEM((T, LANE), F32), pltpu.VMEM((T, LANE), F32),
                        pltpu.VMEM((8, LANE), F32)],
        compiler_params=_cp("parallel", "arbitrary"),
    )(*args)


def _ssd_post(yf, yb, xbc, z, dvec, nw, name):
    s = z.shape[0]
    tm = min(256, s)

    def body(yf_ref, yb_ref, x_ref, z_ref, dv_ref, nw_ref, o_ref):
        ys = yf_ref[...] + yb_ref[...] + dv_ref[...] * x_ref[...]
        yg = ys * _silu(z_ref[...])
        ms = jnp.mean(yg * yg, axis=1, keepdims=True)
        o_ref[...] = (yg * lax.rsqrt(ms + RMS_EPS) * nw_ref[...]).astype(BF16)

    rs = _row_spec(tm, SSM_INNER)
    vs = _vec_spec(SSM_INNER)
    return pl.pallas_call(
        body, name=name, grid=(s // tm,), in_specs=[rs, rs, rs, rs, vs, vs], out_specs=rs,
        out_shape=jax.ShapeDtypeStruct((s, SSM_INNER), BF16), compiler_params=_cp("parallel"),
    )(yf, yb, xbc, z, dvec, nw)


def _ssd_post_bwd(dyn, yf, yb, xbc, z, dvec, nw, name):
    s = z.shape[0]
    tm = min(256, s)

    def body(dyn_ref, yf_ref, yb_ref, x_ref, z_ref, dv_ref, nw_ref, dys_ref, dz_ref, red_ref):
        @pl.when(pl.program_id(0) == 0)
        def _():
            red_ref[...] = jnp.zeros_like(red_ref)

        xv, zv = x_ref[...], z_ref[...]
        ys = yf_ref[...] + yb_ref[...] + dv_ref[...] * xv
        sz = _silu(zv)
        yg = ys * sz
        rstd = lax.rsqrt(jnp.mean(yg * yg, axis=1, keepdims=True) + RMS_EPS)
        yhat = yg * rstd
        dynv = dyn_ref[...]
        dyh = dynv * nw_ref[...]
        dyg = rstd * (dyh - yhat * jnp.mean(dyh * yhat, axis=1, keepdims=True))
        dys = dyg * sz
        dys_ref[...] = dys
        dz_ref[...] = (dyg * ys * _dsilu(zv)).astype(BF16)
        red_ref[0:1, :] += jnp.sum(dynv * yhat, axis=0, keepdims=True)
        red_ref[1:2, :] += jnp.sum(dys * xv, axis=0, keepdims=True)

    rs = _row_spec(tm, SSM_INNER)
    vs = _vec_spec(SSM_INNER)
    return pl.pallas_call(
        body, name=name, grid=(s // tm,), in_specs=[rs, rs, rs, rs, rs, vs, vs],
        out_specs=[rs, rs, _vec_spec(SSM_INNER, 8)],
        out_shape=[jax.ShapeDtypeStruct((s, SSM_INNER), F32), jax.ShapeDtypeStruct((s, SSM_INNER), BF16), jax.ShapeDtypeStruct((8, SSM_INNER), F32)],
        compiler_params=_cp("arbitrary"),
    )(dyn, yf, yb, xbc, z, dvec, nw)


def _dt_bwd(dt_raw, bias, a_log, dt, ddt, dda, name):
    s = dt_raw.shape[0]
    tm = min(1024, s)

    def body(r_ref, b_ref, a_ref, dt_ref, ddt_ref, dda_ref, o_ref, red_ref):
        @pl.when(pl.program_id(0) == 0)
        def _():
            red_ref[...] = jnp.zeros_like(red_ref)

        a = -jnp.exp(a_ref[...])
        ddav = dda_ref[...]
        draw = (ddt_ref[...] + a * ddav) * _sigmoid(r_ref[...] + b_ref[...])
        o_ref[...] = draw.astype(BF16)
        red_ref[0:1, :] += jnp.sum(draw, axis=0, keepdims=True)
        red_ref[1:2, :] += a * jnp.sum(ddav * dt_ref[...], axis=0, keepdims=True)

    rs = _row_spec(tm, LANE)
    vs = _vec_spec(LANE)
    return pl.pallas_call(
        body, name=name, grid=(s // tm,), in_specs=[rs, vs, vs, rs, rs, rs], out_specs=[rs, _vec_spec(LANE, 8)],
        out_shape=[jax.ShapeDtypeStruct((s, LANE), BF16), jax.ShapeDtypeStruct((8, LANE), F32)],
        compiler_params=_cp("arbitrary"),
    )(dt_raw, bias, a_log, dt, ddt, dda)


def _pad_lanes(v):
    v = v.reshape(1, -1)
    return jnp.pad(v, ((0, 0), (0, LANE - v.shape[1])))


def _ssd_prep_weights(w_in, conv_w, conv_b, dt_bias, a_log, d_skip, norm_w, w_out):
    return dict(
        w_z=w_in[:, :SSM_INNER].astype(BF16),
        w_xbc=w_in[:, SSM_INNER:SSM_INNER + SSM_CONV_DIM].astype(BF16),
        w_dt=jnp.pad(w_in[:, SSM_INNER + SSM_CONV_DIM:], ((0, 0), (0, LANE - 2 * SSM_HEADS))).astype(BF16),
        conv_w=conv_w, conv_b=conv_b.reshape(1, -1), bias=_pad_lanes(dt_bias), a_log=_pad_lanes(a_log),
        dvec=jnp.repeat(d_skip, SSM_HEAD_DIM).reshape(1, -1), nw=norm_w.reshape(1, -1), w_out=w_out.astype(BF16),
    )


def _ssd_layer_fwd(h, w, li):
    nm = lambda t: f"b{li}_{t}"
    z = _mm(h, w["w_z"], name=nm("z"))
    xraw = _mm(h, w["w_xbc"], name=nm("xbc"))
    dt_raw = _mm(h, w["w_dt"], name=nm("dt"))
    pre, xbc = _conv_fwd(xraw, w["conv_w"], w["conv_b"], nm("conv"))
    dt, cum, cumt, dtw, cumw = _dt_prep(dt_raw, w["bias"], w["a_log"], nm("dtprep"))
    nc = cumt.shape[0]
    cumk = _by_group(cum)
    cumtk = cumt[:, :2 * SSM_HEADS].reshape(nc, 2 * SSM_GROUPS, SSM_REP, T).transpose(1, 0, 2, 3)
    yf, stf = _ssd_scan(xbc, dtw, cumw, cumk, cumtk, False, nm("scan_f"))
    yb, stb = _ssd_scan(xbc, dtw, cumw, cumk, cumtk, True, nm("scan_b"))
    yn = _ssd_post(yf, yb, xbc, z, w["dvec"], w["nw"], nm("post"))
    out = _mm(yn, w["w_out"], name=nm("out"))
    return out, (z, xraw, dt_raw, pre, xbc, dt, dtw, cumw, cumk, cumtk, yf, stf, yb, stb, yn)


def _ssd_layer_bwd(dy, h, saved, w, li):
    nm = lambda t: f"b{li}_{t}"
    z, xraw, dt_raw, pre, xbc, dt, dtw, cumw, cumk, cumtk, yf, stf, yb, stb, yn = saved
    g_w_out = _mm(yn, dy, ta=True, out_dtype=BF16, name=nm("dwout"))
    dyn = _mm(dy, w["w_out"], tb=True, name=nm("dyn"))
    dys, dz, pred = _ssd_post_bwd(dyn, yf, yb, xbc, z, w["dvec"], w["nw"], nm("postbwd"))
    dx1, db1, dc1, ddt_f, dda_f = _ssd_scan_bwd(xbc, dtw, cumw, cumk, cumtk, dys, stf, w["dvec"], None, False, nm("scanbwd_f"))
    dx, db, dc, ddt_b, dda_b = _ssd_scan_bwd(xbc, dtw, cumw, cumk, cumtk, dys, stb, w["dvec"], (dx1, db1, dc1), True, nm("scanbwd_b"))
    dact = jnp.concatenate([dx, db, dc], axis=1)
    dxraw, cred = _conv_bwd(dact, pre, xraw, w["conv_w"], nm("convbwd"))
    draw, dred = _dt_bwd(dt_raw, w["bias"], w["a_log"], dt, _from_group(ddt_f, ddt_b), _from_group(dda_f, dda_b), nm("dtbwd"))
    dh = _mm(dz, w["w_z"], tb=True, name=nm("dh_z"))
    dh = _mm(dxraw, w["w_xbc"], tb=True, add=dh, name=nm("dh_xbc"))
    dh = _mm(draw, w["w_dt"], tb=True, add=dh, name=nm("dh_dt"))
    g_w_in = jnp.concatenate([_mm(h, dz, ta=True, out_dtype=BF16, name=nm("dwz")), _mm(h, dxraw, ta=True, out_dtype=BF16, name=nm("dwxbc")),
                              _mm(h, draw, ta=True, out_dtype=BF16, name=nm("dwdt"))[:, :2 * SSM_HEADS]], axis=1)
    grads = (g_w_in, cred[:SSM_CONV], cred[SSM_CONV], dred[0, :2 * SSM_HEADS].reshape(2, SSM_HEADS),
             dred[1, :2 * SSM_HEADS].reshape(2, SSM_HEADS), pred[1].reshape(SSM_HEADS, SSM_HEAD_DIM).sum(axis=1), pred[0], g_w_out)
    return dh, grads


B_GRAD_NAMES = ("b_w_in", "b_conv_w", "b_conv_b", "b_dt_bias", "b_a_log", "b_d", "b_norm_w", "b_w_out")


def _local_step(x, tgt, mod, w):
    d = x.shape[1]
    qkv_cols = QKV_COLS
    layers = []
    for i in range(DEPTH):
        j = i // 2
        if i % 2 == 0:
            layers.append((w["a_w_in"][j][:, :qkv_cols].astype(BF16), w["a_w_in"][j][:, qkv_cols:].astype(BF16), w["a_w_out"][j].astype(BF16)))
        else:
            layers.append(_ssd_prep_weights(w["b_w_in"][j], w["b_conv_w"][j], w["b_conv_b"][j], w["b_dt_bias"][j], w["b_a_log"][j],
                                            w["b_d"][j], w["b_norm_w"][j], w["b_w_out"][j]))
    saved = []
    for i in range(DEPTH):
        shift, scale, gate = mod[i:i + 1, :d], mod[i:i + 1, d:2 * d], mod[i:i + 1, 2 * d:]
        h = _modulate(x, scale, shift, f"l{i}_mod")
        if i % 2 == 0:
            out, sv = _attn_layer_fwd(h, *layers[i], i)
        else:
            out, sv = _ssd_layer_fwd(h, layers[i], i)
        xn = _resid_ln(x, out, gate, w["ln_g"][i:i + 1], w["ln_b"][i:i + 1], f"l{i}_ln")
        saved.append((x, h, out, sv))
        x = xn
    dx, lred = _loss_grad(x, tgt, "loss")
    loss = 0.5 * jnp.sum(lred[0]) / d
    dmod, g_ln_g, g_ln_b = [None] * DEPTH, [None] * DEPTH, [None] * DEPTH
    ga_in, ga_out = [None, None], [None, None]
    gb = [None, None]
    for i in reversed(range(DEPTH)):
        j = i // 2
        xi, h, out, sv = saved[i]
        scale, gate = mod[i:i + 1, d:2 * d], mod[i:i + 1, 2 * d:]
        du, dy, red = _resid_ln_bwd(xi, out, dx, gate, w["ln_g"][i:i + 1], f"l{i}_lnbwd")
        g_ln_g[i], g_ln_b[i] = red[1], red[2]
        if i % 2 == 0:
            dhs, ga_in[j], ga_out[j] = _attn_layer_bwd(dy, h, sv, *layers[i], i)
        else:
            dh, gb[j] = _ssd_layer_bwd(dy, h, sv, layers[i], i)
            dhs = [dh]
        dx, red2 = _modulate_bwd(du, dhs, xi, scale, f"l{i}_modbwd")
        dmod[i] = jnp.concatenate([red2[1], red2[0], red[0]])
    grads = {"ln_g": jnp.stack(g_ln_g), "ln_b": jnp.stack(g_ln_b), "a_w_in": jnp.stack(ga_in), "a_w_out": jnp.stack(ga_out)}
    for k, n in enumerate(B_GRAD_NAMES):
        grads[n] = jnp.stack([gb[0][k], gb[1][k]])
    return loss, dx, jnp.stack(dmod), grads


MESH = pl.DeviceIdType.MESH
ANY = pl.BlockSpec(memory_space=pl.ANY)
N_DEV = 8
N_SHARD = 4


def _flip(v, bit):
    return 1 - v if bit else v


def _all_gather8(v, name):
    def body(v_ref, o_ref, send_sems, recv_sems, local_sem):
        x, y, c = lax.axis_index("x"), lax.axis_index("y"), lax.axis_index("c")
        me = 4 * x + 2 * y + c
        local = pltpu.make_async_copy(v_ref, o_ref.at[me], local_sem)
        local.start()
        copies = []
        for k in range(1, N_DEV):
            peer = (_flip(x, k & 4), _flip(y, k & 2), _flip(c, k & 1))
            copies.append(pltpu.make_async_remote_copy(
                src_ref=v_ref, dst_ref=o_ref.at[me], send_sem=send_sems.at[k - 1], recv_sem=recv_sems.at[k - 1],
                device_id=peer, device_id_type=MESH))
        for cp in copies:
            cp.start()
        for cp in copies:
            cp.wait()
        local.wait()

    return pl.pallas_call(
        body, name=name, in_specs=[ANY], out_specs=ANY, out_shape=jax.ShapeDtypeStruct((N_DEV,) + v.shape, v.dtype),
        scratch_shapes=[pltpu.SemaphoreType.DMA((N_DEV - 1,)), pltpu.SemaphoreType.DMA((N_DEV - 1,)), pltpu.SemaphoreType.DMA],
    )(v)


def _transpose_shards(src, name):
    def body(s_ref, o_ref, send_sems, recv_sems, local_sem):
        x, y, c = lax.axis_index("x"), lax.axis_index("y"), lax.axis_index("c")
        m = 2 * x + y
        local = pltpu.make_async_copy(s_ref.at[m], o_ref.at[m], local_sem)
        local.start()
        copies = []
        for k in range(1, N_SHARD):
            px, py = _flip(x, k & 2), _flip(y, k & 1)
            copies.append(pltpu.make_async_remote_copy(
                src_ref=s_ref.at[2 * px + py], dst_ref=o_ref.at[m], send_sem=send_sems.at[k - 1], recv_sem=recv_sems.at[k - 1],
                device_id=(px, py, c), device_id_type=MESH))
        for cp in copies:
            cp.start()
        for cp in copies:
            cp.wait()
        local.wait()

    return pl.pallas_call(
        body, name=name, in_specs=[ANY], out_specs=ANY, out_shape=jax.ShapeDtypeStruct(src.shape, src.dtype),
        scratch_shapes=[pltpu.SemaphoreType.DMA((N_SHARD - 1,)), pltpu.SemaphoreType.DMA((N_SHARD - 1,)), pltpu.SemaphoreType.DMA],
    )(src)


def _gather_shards(src, name):
    rows = src.shape[0]
    half = rows // 2
    n_ici = N_SHARD - 1

    def body(s_ref, o_ref, send_sems, recv_sems, local_sem):
        x, y, c = lax.axis_index("x"), lax.axis_index("y"), lax.axis_index("c")
        m = 2 * x + y
        sibling = (x, y, 1 - c)
        my_half = pl.ds(pl.multiple_of(c * half, 16), half)
        its_half = pl.ds(pl.multiple_of((1 - c) * half, 16), half)
        local = pltpu.make_async_copy(s_ref, o_ref.at[m], local_sem)
        local.start()
        chips = [(_flip(x, k & 2), _flip(y, k & 1)) for k in range(1, N_SHARD)]

        def copy(sem, src_ref, dst_ref, to):
            return pltpu.make_async_remote_copy(src_ref=src_ref, dst_ref=dst_ref, send_sem=send_sems.at[sem],
                                                recv_sem=recv_sems.at[sem], device_id=to, device_id_type=MESH)

        first = [copy(i, s_ref.at[my_half], o_ref.at[m, my_half], (px, py, c)) for i, (px, py) in enumerate(chips)]
        for cp in first:
            cp.start()
        passed = []
        for i, (px, py) in enumerate(chips):
            landed = o_ref.at[2 * px + py, my_half]
            copy(i, landed, landed, (px, py, c)).wait_recv()
            passed.append(copy(n_ici + i, landed, landed, sibling))
            passed[-1].start()
        for i, (px, py) in enumerate(chips):
            from_sibling = o_ref.at[2 * px + py, its_half]
            copy(n_ici + i, from_sibling, from_sibling, sibling).wait_recv()
        for cp in first + passed:
            cp.wait_send()
        local.wait()

    return pl.pallas_call(
        body, name=name, in_specs=[ANY], out_specs=ANY, out_shape=jax.ShapeDtypeStruct((N_SHARD,) + src.shape, src.dtype),
        scratch_shapes=[pltpu.SemaphoreType.DMA((2 * n_ici,)), pltpu.SemaphoreType.DMA((2 * n_ici,)), pltpu.SemaphoreType.DMA],
    )(src)


def _swap_sibling(v, name):
    def body(v_ref, o_ref, send_sem, recv_sem):
        x, y, c = lax.axis_index("x"), lax.axis_index("y"), lax.axis_index("c")
        cp = pltpu.make_async_remote_copy(src_ref=v_ref, dst_ref=o_ref, send_sem=send_sem, recv_sem=recv_sem,
                                          device_id=(x, y, 1 - c), device_id_type=MESH)
        cp.start()
        cp.wait()

    return pl.pallas_call(
        body, name=name, in_specs=[ANY], out_specs=ANY, out_shape=jax.ShapeDtypeStruct(v.shape, v.dtype),
        scratch_shapes=[pltpu.SemaphoreType.DMA, pltpu.SemaphoreType.DMA],
    )(v)


def _sum_slots(a, name):
    n, r, cdim = a.shape
    tm = max(t for t in range(16, 641, 16) if r % t == 0)

    def body(a_ref, o_ref):
        acc = a_ref[0].astype(F32)
        for k in range(1, n):
            acc = acc + a_ref[k].astype(F32)
        o_ref[...] = acc

    return pl.pallas_call(
        body, name=name, grid=(r // tm,), in_specs=[pl.BlockSpec((n, tm, cdim), lambda i: (0, i, 0))],
        out_specs=pl.BlockSpec((tm, cdim), lambda i: (i, 0)), out_shape=jax.ShapeDtypeStruct((r, cdim), F32),
        compiler_params=_cp("parallel"),
    )(a)


def _silu_rows(v, name):
    def body(v_ref, o_ref):
        o_ref[...] = _silu(v_ref[...])

    return pl.pallas_call(body, name=name, out_shape=jax.ShapeDtypeStruct(v.shape, F32))(v)


PACK_COLS = 1024
PACK_TILE = 256


def _adamw(w, g1, g2, m, v, name):
    r = w.shape[0]
    c1 = 1.0 / (1.0 - ADAM_B1 ** ADAM_STEP)
    c2 = 1.0 / (1.0 - ADAM_B2 ** ADAM_STEP)

    def body(w_ref, g1_ref, g2_ref, m_ref, v_ref, g_ref, d_ref, nm_ref, nv_ref):
        g = g1_ref[...] + g2_ref[...]
        mn = ADAM_B1 * m_ref[...] + (1.0 - ADAM_B1) * g
        vn = ADAM_B2 * v_ref[...] + (1.0 - ADAM_B2) * (g * g)
        g_ref[...] = g
        nm_ref[...] = mn
        nv_ref[...] = vn
        d_ref[...] = -ADAM_LR * ((mn * c1) / (jnp.sqrt(vn * c2) + ADAM_EPS) + ADAM_WD * w_ref[...])

    spec = pl.BlockSpec((PACK_TILE, PACK_COLS), lambda i: (i, 0))
    return pl.pallas_call(
        body, name=name, grid=(r // PACK_TILE,), in_specs=[spec] * 5, out_specs=[spec] * 4,
        out_shape=[jax.ShapeDtypeStruct(w.shape, F32)] * 4, compiler_params=_cp("parallel"),
    )(w, g1, g2, m, v)


def _rows(a):
    f = a.reshape(-1)
    pad = (-f.shape[0]) % PACK_COLS
    if pad:
        f = jnp.pad(f, (0, pad))
    return f.reshape(-1, PACK_COLS)


def _nrows(shape):
    return -(-int(np.prod(shape)) // PACK_COLS)


def _pack(parts, total_rows=None):
    p = jnp.concatenate([_rows(a) for a in parts], axis=0)
    if total_rows is not None and total_rows > p.shape[0]:
        p = jnp.pad(p, ((0, total_rows - p.shape[0]), (0, 0)))
    return p


def _unpack(p, shapes):
    out, r0 = [], 0
    for shp in shapes:
        n = int(np.prod(shp))
        nr = _nrows(shp)
        out.append(p[r0:r0 + nr].reshape(-1)[:n].reshape(shp))
        r0 += nr
    return out


def _unshard_cols(g):
    return jnp.concatenate([g[k] for k in range(N_SHARD)], axis=-1)


def _shard_cols(a):
    n = a.shape[-1] // N_SHARD
    return jnp.stack([a[..., k * n:(k + 1) * n] for k in range(N_SHARD)])


def _unshard_rows(g):
    return jnp.concatenate([g[k] for k in range(N_SHARD)], axis=1)


def _shard_rows(a):
    n = a.shape[1] // N_SHARD
    return jnp.stack([a[:, k * n:(k + 1) * n] for k in range(N_SHARD)])


W_NAMES = ("ada_w", "ada_b", "ln_g", "ln_b", "a_w_in", "a_w_out", "b_w_in", "b_conv_w", "b_conv_b", "b_dt_bias", "b_a_log", "b_d",
           "b_norm_w", "b_w_out")
BIG = ("a_w_in", "a_w_out", "b_w_in", "b_w_out")
PACK_ORDER = BIG + ("ada_w", "ada_b", "ln_g", "ln_b", "b_conv_w", "b_conv_b", "b_dt_bias", "b_a_log", "b_d", "b_norm_w")


def kernel(x, c, ada_w, ada_b, ln_g, ln_b, a_w_in, a_w_out, b_w_in, b_conv_w, b_conv_b, b_dt_bias, b_a_log, b_d, b_norm_w, b_w_out, loss_target, m_ada_w, m_ada_b, m_ln_g, m_ln_b, m_a_w_in, m_a_w_out, m_b_w_in, m_b_conv_w, m_b_conv_b, m_b_dt_bias, m_b_a_log, m_b_d, m_b_norm_w, m_b_w_out, v_ada_w, v_ada_b, v_ln_g, v_ln_b, v_a_w_in, v_a_w_out, v_b_w_in, v_b_conv_w, v_b_conv_b, v_b_dt_bias, v_b_a_log, v_b_d, v_b_norm_w, v_b_w_out):
    w = dict(ada_w=ada_w, ada_b=ada_b, ln_g=ln_g, ln_b=ln_b, a_w_in=a_w_in, a_w_out=a_w_out, b_w_in=b_w_in, b_conv_w=b_conv_w,
             b_conv_b=b_conv_b, b_dt_bias=b_dt_bias, b_a_log=b_a_log, b_d=b_d, b_norm_w=b_norm_w, b_w_out=b_w_out)
    mom = dict(ada_w=m_ada_w, ada_b=m_ada_b, ln_g=m_ln_g, ln_b=m_ln_b, a_w_in=m_a_w_in, a_w_out=m_a_w_out, b_w_in=m_b_w_in,
               b_conv_w=m_b_conv_w, b_conv_b=m_b_conv_b, b_dt_bias=m_b_dt_bias, b_a_log=m_b_a_log, b_d=m_b_d, b_norm_w=m_b_norm_w,
               b_w_out=m_b_w_out)
    var = dict(ada_w=v_ada_w, ada_b=v_ada_b, ln_g=v_ln_g, ln_b=v_ln_b, a_w_in=v_a_w_in, a_w_out=v_a_w_out, b_w_in=v_b_w_in,
               b_conv_w=v_b_conv_w, b_conv_b=v_b_conv_b, b_dt_bias=v_b_dt_bias, b_a_log=v_b_a_log, b_d=v_b_d, b_norm_w=v_b_norm_w,
               b_w_out=v_b_w_out)
    ax, ay, ac = lax.axis_index("x"), lax.axis_index("y"), lax.axis_index("c")
    me = 4 * ax + 2 * ay + ac
    shard = 2 * ax + ay
    d = x.shape[-1]
    dsh = ada_w.shape[-1]

    small_in = (c, b_conv_w, b_conv_b, b_norm_w)
    g0 = _all_gather8(_pack(small_in).reshape(-1, LANE), "gather_small_in").reshape(N_DEV, -1, PACK_COLS)
    per_dev = [_unpack(g0[k], [a.shape for a in small_in]) for k in range(N_DEV)]
    c_all = jnp.concatenate([p[0] for p in per_dev], axis=0)
    conv_w_full, conv_b_full, norm_w_full = (_unshard_cols([per_dev[2 * k][t] for k in range(N_SHARD)]) for t in (1, 2, 3))

    cond = _silu_rows(jnp.pad(c_all, ((0, 8), (0, 0))), "cond")
    bias = lax.dynamic_slice_in_dim(ada_b, shard * dsh, dsh, axis=1)
    part = jnp.stack([_mm(cond, ada_w[i], add=jnp.broadcast_to(bias[i], (16, dsh)), name=f"mod{i}")[:N_DEV] for i in range(DEPTH)])
    g1 = _all_gather8(part.reshape(-1, LANE), "gather_mod").reshape(N_DEV, DEPTH, N_DEV, dsh)
    mod_all = _unshard_cols([g1[2 * k] for k in range(N_SHARD)])
    mod = lax.dynamic_index_in_dim(mod_all, me, axis=1, keepdims=False)

    gw = _gather_shards(_pack([w[n] for n in BIG]).astype(BF16), "gather_weights")
    big_sh = [_unpack(gw[k], [w[n].shape for n in BIG]) for k in range(N_SHARD)]
    full = dict(
        ln_g=ln_g, ln_b=ln_b, b_dt_bias=b_dt_bias, b_a_log=b_a_log, b_d=b_d,
        b_conv_w=conv_w_full, b_conv_b=conv_b_full, b_norm_w=norm_w_full,
        a_w_in=_unshard_cols([s[0] for s in big_sh]), a_w_out=_unshard_rows([s[1] for s in big_sh]),
        b_w_in=_unshard_cols([s[2] for s in big_sh]), b_w_out=_unshard_rows([s[3] for s in big_sh]),
    )

    loss, grad_x, dmod, g = _local_step(x[0], loss_target[0], mod, full)

    gsh = (_shard_cols(g["a_w_in"]), _shard_rows(g["a_w_out"]), _shard_cols(g["b_w_in"]), _shard_rows(g["b_w_out"]))
    to_send = jnp.stack([_pack([t[k] for t in gsh]) for k in range(N_SHARD)]).astype(BF16)
    mine = _sum_slots(_transpose_shards(to_send, "scatter_grads"), "sum_shards")
    theirs = _swap_sibling(mine, "swap_grads")

    small_g = (dmod, g["ln_g"], g["ln_b"], g["b_dt_bias"], g["b_a_log"], g["b_d"], g["b_conv_w"], g["b_conv_b"], g["b_norm_w"],
               loss.reshape(1))
    g2 = _all_gather8(_pack(small_g).reshape(-1, LANE), "gather_small_grads")
    tot = _unpack(_sum_slots(g2, "sum_small").reshape(-1, PACK_COLS), [a.shape for a in small_g])
    g_ada_b, g_ln_g, g_ln_b, g_dt_bias, g_a_log, g_d, g_conv_w, g_conv_b, g_norm_w, loss_sum = tot
    dmod_all = g2.reshape(N_DEV, -1)[:, :dmod.size].reshape(N_DEV, DEPTH, 3 * d)
    dmod_mine = lax.dynamic_slice_in_dim(dmod_all, shard * dsh, dsh, axis=2)
    g_ada_w = jnp.stack([_mm(cond, jnp.pad(dmod_mine[:, i], ((0, 8), (0, 0))), ta=True, name=f"dada{i}") for i in range(DEPTH)])
    csh = g_conv_w.shape[-1] // N_SHARD
    nsh = g_norm_w.shape[-1] // N_SHARD
    small_grads = dict(
        ada_w=g_ada_w, ada_b=g_ada_b, ln_g=g_ln_g, ln_b=g_ln_b, b_dt_bias=g_dt_bias, b_a_log=g_a_log, b_d=g_d,
        b_conv_w=lax.dynamic_slice_in_dim(g_conv_w, shard * csh, csh, axis=2),
        b_conv_b=lax.dynamic_slice_in_dim(g_conv_b, shard * csh, csh, axis=1),
        b_norm_w=lax.dynamic_slice_in_dim(g_norm_w, shard * nsh, nsh, axis=1),
    )

    rest = PACK_ORDER[len(BIG):]
    rows = sum(_nrows(w[n].shape) for n in PACK_ORDER)
    rows = -(-rows // PACK_TILE) * PACK_TILE
    pw, pm, pv = (_pack([t[n] for n in PACK_ORDER], rows) for t in (w, mom, var))
    pg1 = _pack([mine] + [small_grads[n] for n in rest], rows)
    pg2 = _pack([theirs], rows)
    outs = [_unpack(p, [w[n].shape for n in PACK_ORDER]) for p in _adamw(pw, pg1, pg2, pm, pv, "adamw")]
    by_name = [dict(zip(PACK_ORDER, o)) for o in outs]
    return (loss_sum.reshape(()), grad_x[None], *[t[n] for t in by_name for n in W_NAMES])
```

```python
import jax
import jax.numpy as jnp
import numpy as np
from jax import lax
from jax.experimental import pallas as pl
from jax.experimental.pallas import tpu as pltpu

F32 = jnp.float32
BF16 = jnp.bfloat16

DEPTH = 4
A_HEADS = 16
A_HEAD_DIM = 64
A_WIDTH = A_HEADS * A_HEAD_DIM
DILATIONS = (1, 4, 16)
A_RADIUS = 64
A_QBLOCK = 128
SSM_HEADS = 32
SSM_HEAD_DIM = 64
SSM_STATE = 128
SSM_GROUPS = 4
SSM_REP = SSM_HEADS // SSM_GROUPS
SSM_CONV = 5
SSM_CHUNK = 128
DEEPNORM_ALPHA = (2 * DEPTH) ** 0.25
LN_EPS = 1e-5
RMS_EPS = 1e-5
ADAM_LR, ADAM_B1, ADAM_B2, ADAM_EPS, ADAM_WD, ADAM_STEP = 0.001, 0.9, 0.999, 1e-08, 0.01, 10
VMEM_LIMIT = 56 * 1024 * 1024
LANE = 128


def _cp(*sem):
    return pltpu.CompilerParams(dimension_semantics=sem, vmem_limit_bytes=VMEM_LIMIT)


def _tile(dim, target):
    if dim <= target:
        return dim
    t = (target // LANE) * LANE
    while dim % t:
        t -= LANE
    return t


def _sigmoid(x):
    return 1.0 / (1.0 + jnp.exp(-x))


def _silu(x):
    return x * _sigmoid(x)


def _dsilu(x):
    s = _sigmoid(x)
    return s * (1.0 + x * (1.0 - s))


def _split3(x):
    a = x.astype(BF16)
    r = x - a.astype(F32)
    b = r.astype(BF16)
    c = (r - b.astype(F32)).astype(BF16)
    return a, b, c


def _dot(a, b, ca=1, cb=0):
    return lax.dot_general(a, b, (((ca,), (cb,)), ((), ())), preferred_element_type=F32)


def _dot_exact(m01, x):
    a, b, c = _split3(x)
    return _dot(m01, a) + _dot(m01, b) + _dot(m01, c)


def _mm(a, b, *, ta=False, tb=False, add=None, out_dtype=F32, name, tm=1024, tn=1024, tk=1024):
    m, k = (a.shape[1], a.shape[0]) if ta else a.shape
    n = b.shape[0] if tb else b.shape[1]
    assert (b.shape[1] if tb else b.shape[0]) == k
    tm, tn, tk = _tile(m, tm), _tile(n, tn), _tile(k, tk)
    nk = k // tk
    has_add = add is not None

    def body(*refs):
        if has_add:
            a_ref, b_ref, c_ref, o_ref, acc = refs
        else:
            a_ref, b_ref, o_ref, acc = refs
        kk = pl.program_id(2)
        part = _dot(a_ref[...].astype(BF16), b_ref[...].astype(BF16), 0 if ta else 1, 1 if tb else 0)

        def finish(r):
            if has_add:
                r = r + c_ref[...]
            o_ref[...] = r.astype(o_ref.dtype)

        if nk == 1:
            finish(part)
            return

        @pl.when(kk == 0)
        def _():
            acc[...] = part

        @pl.when((kk > 0) & (kk < nk - 1))
        def _():
            acc[...] += part

        @pl.when(kk == nk - 1)
        def _():
            finish(acc[...] + part)

    a_spec = pl.BlockSpec((tk, tm), lambda i, j, kk: (kk, i)) if ta else pl.BlockSpec((tm, tk), lambda i, j, kk: (i, kk))
    b_spec = pl.BlockSpec((tn, tk), lambda i, j, kk: (j, kk)) if tb else pl.BlockSpec((tk, tn), lambda i, j, kk: (kk, j))
    in_specs = [a_spec, b_spec]
    args = [a, b]
    if has_add:
        in_specs.append(pl.BlockSpec((tm, tn), lambda i, j, kk: (i, j)))
        args.append(add)
    return pl.pallas_call(
        body, name=name, grid=(m // tm, n // tn, nk), in_specs=in_specs,
        out_specs=pl.BlockSpec((tm, tn), lambda i, j, kk: (i, j)),
        out_shape=jax.ShapeDtypeStruct((m, n), out_dtype),
        scratch_shapes=[pltpu.VMEM((tm, tn) if nk > 1 else (8, LANE), F32)],
        compiler_params=_cp("parallel", "parallel", "arbitrary"),
    )(*args)


ROWS = 512


def _row_spec(tm, d):
    return pl.BlockSpec((tm, d), lambda i: (i, 0))


def _vec_spec(d, rows=1):
    return pl.BlockSpec((rows, d), lambda i: (0, 0))


def _modulate(x, scale, shift, name):
    s, d = x.shape
    tm = min(ROWS, s)

    def body(x_ref, sc_ref, sh_ref, o_ref):
        o_ref[...] = (x_ref[...] * (1.0 + sc_ref[...]) + sh_ref[...]).astype(BF16)

    return pl.pallas_call(
        body, name=name, grid=(s // tm,), in_specs=[_row_spec(tm, d), _vec_spec(d), _vec_spec(d)],
        out_specs=_row_spec(tm, d), out_shape=jax.ShapeDtypeStruct((s, d), BF16), compiler_params=_cp("parallel"),
    )(x, scale, shift)


def _resid_ln(x, y, gate, g, b, name):
    s, d = x.shape
    tm = min(ROWS, s)

    def body(x_ref, y_ref, gt_ref, g_ref, b_ref, o_ref):
        u = DEEPNORM_ALPHA * x_ref[...] + gt_ref[...] * y_ref[...]
        mu = jnp.mean(u, axis=1, keepdims=True)
        uc = u - mu
        var = jnp.mean(uc * uc, axis=1, keepdims=True)
        o_ref[...] = uc * lax.rsqrt(var + LN_EPS) * g_ref[...] + b_ref[...]

    return pl.pallas_call(
        body, name=name, grid=(s // tm,),
        in_specs=[_row_spec(tm, d), _row_spec(tm, d), _vec_spec(d), _vec_spec(d), _vec_spec(d)],
        out_specs=_row_spec(tm, d), out_shape=jax.ShapeDtypeStruct((s, d), F32), compiler_params=_cp("parallel"),
    )(x, y, gate, g, b)


def _resid_ln_bwd(x, y, dxn, gate, g, name):
    s, d = x.shape
    tm = min(ROWS, s)

    def body(x_ref, y_ref, dxn_ref, gt_ref, g_ref, du_ref, dy_ref, red_ref):
        @pl.when(pl.program_id(0) == 0)
        def _():
            red_ref[...] = jnp.zeros_like(red_ref)

        yv = y_ref[...]
        u = DEEPNORM_ALPHA * x_ref[...] + gt_ref[...] * yv
        mu = jnp.mean(u, axis=1, keepdims=True)
        uc = u - mu
        var = jnp.mean(uc * uc, axis=1, keepdims=True)
        rstd = lax.rsqrt(var + LN_EPS)
        xhat = uc * rstd
        dxnv = dxn_ref[...]
        dxh = dxnv * g_ref[...]
        du = rstd * (dxh - jnp.mean(dxh, axis=1, keepdims=True) - xhat * jnp.mean(dxh * xhat, axis=1, keepdims=True))
        du_ref[...] = du
        dy_ref[...] = (du * gt_ref[...]).astype(BF16)
        red_ref[0:1, :] += jnp.sum(du * yv, axis=0, keepdims=True)
        red_ref[1:2, :] += jnp.sum(dxnv * xhat, axis=0, keepdims=True)
        red_ref[2:3, :] += jnp.sum(dxnv, axis=0, keepdims=True)

    return pl.pallas_call(
        body, name=name, grid=(s // tm,),
        in_specs=[_row_spec(tm, d), _row_spec(tm, d), _row_spec(tm, d), _vec_spec(d), _vec_spec(d)],
        out_specs=[_row_spec(tm, d), _row_spec(tm, d), _vec_spec(d, 8)],
        out_shape=[jax.ShapeDtypeStruct((s, d), F32), jax.ShapeDtypeStruct((s, d), BF16), jax.ShapeDtypeStruct((8, d), F32)],
        compiler_params=_cp("arbitrary"),
    )(x, y, dxn, gate, g)


def _modulate_bwd(du, dhs, x, scale, name):
    s, d = x.shape
    tm = min(ROWS, s)
    n = len(dhs)

    def body(*refs):
        du_ref, dh_refs, (x_ref, sc_ref, dx_ref, red_ref) = refs[0], refs[1:1 + n], refs[1 + n:]

        @pl.when(pl.program_id(0) == 0)
        def _():
            red_ref[...] = jnp.zeros_like(red_ref)

        dhv = dh_refs[0][...]
        for t in dh_refs[1:]:
            dhv = dhv + t[...]
        dx_ref[...] = DEEPNORM_ALPHA * du_ref[...] + dhv * (1.0 + sc_ref[...])
        red_ref[0:1, :] += jnp.sum(dhv * x_ref[...], axis=0, keepdims=True)
        red_ref[1:2, :] += jnp.sum(dhv, axis=0, keepdims=True)

    return pl.pallas_call(
        body, name=name, grid=(s // tm,),
        in_specs=[_row_spec(tm, d)] * (n + 2) + [_vec_spec(d)],
        out_specs=[_row_spec(tm, d), _vec_spec(d, 8)],
        out_shape=[jax.ShapeDtypeStruct((s, d), F32), jax.ShapeDtypeStruct((8, d), F32)],
        compiler_params=_cp("arbitrary"),
    )(du, *dhs, x, scale)


def _loss_grad(xf, tgt, name):
    s, d = xf.shape
    tm = min(ROWS, s)

    def body(x_ref, t_ref, dx_ref, red_ref):
        @pl.when(pl.program_id(0) == 0)
        def _():
            red_ref[...] = jnp.zeros_like(red_ref)

        e = x_ref[...] - t_ref[...]
        dx_ref[...] = e * (1.0 / d)
        red_ref[0:1, :] += jnp.sum(e * e, axis=0, keepdims=True)

    return pl.pallas_call(
        body, name=name, grid=(s // tm,), in_specs=[_row_spec(tm, d), _row_spec(tm, d)],
        out_specs=[_row_spec(tm, d), _vec_spec(d, 8)],
        out_shape=[jax.ShapeDtypeStruct((s, d), F32), jax.ShapeDtypeStruct((8, d), F32)],
        compiler_params=_cp("arbitrary"),
    )(xf, tgt)


QKV_COLS = 3 * 3 * A_WIDTH


SLOPES = tuple(float(2.0 ** (-8.0 * (h + 1.0) / A_HEADS)) for h in range(A_HEADS))
FAR = 1e30
HEAD_COLS = tuple(slice(h * A_HEAD_DIM, (h + 1) * A_HEAD_DIM) for h in range(A_HEADS))


def _band_dist(n, length, dil, span_rows):
    shape = (2 * A_QBLOCK, A_QBLOCK) if span_rows else (A_QBLOCK, 2 * A_QBLOCK)
    r = lax.broadcasted_iota(jnp.int32, shape, 0)
    c = lax.broadcasted_iota(jnp.int32, shape, 1)
    sp, ce = (r, c) if span_rows else (c, r)
    delta = sp - A_RADIUS - ce
    pos = n * A_QBLOCK - A_RADIUS + sp
    valid = (jnp.abs(delta) <= A_RADIUS) & (pos >= 0) & (pos < length)
    return jnp.where(valid, jnp.abs(delta).astype(F32) * float(dil), FAR)


def _span_specs(col, nb64):
    def mk(i):
        return pl.BlockSpec((64, A_WIDTH), lambda r, n: (r * nb64 + jnp.clip(2 * n - 1 + i, 0, nb64 - 1), col))
    return [mk(i) for i in range(4)]


def _to_residue(t, dil):
    if dil == 1:
        return t
    s, c = t.shape
    return t.reshape(s // dil, dil, c).transpose(1, 0, 2).reshape(s, c)


def _from_residue(t, dil):
    if dil == 1:
        return t
    s, c = t.shape
    return t.reshape(dil, s // dil, c).transpose(1, 0, 2).reshape(s, c)


def _cat(refs):
    return jnp.concatenate([t[...] for t in refs], axis=0)


def _head_expander():
    r = lax.broadcasted_iota(jnp.int32, (A_HEADS, A_WIDTH), 0)
    c = lax.broadcasted_iota(jnp.int32, (A_HEADS, A_WIDTH), 1)
    return ((c >= r * A_HEAD_DIM) & (c < (r + 1) * A_HEAD_DIM)).astype(BF16)


def _to_lanes(x16, e):
    a, b, c = _split3(x16)
    return _dot(a, e) + _dot(b, e) + _dot(c, e)


def _per_head_sum(x, e):
    a, b, c = _split3(x)
    return _dot(a, e, 1, 1) + _dot(b, e, 1, 1) + _dot(c, e, 1, 1)


def _attn_fwd(qkv, g, name):
    s = qkv.shape[0]
    dil = DILATIONS[g]
    length = s // dil
    nblk = length // A_QBLOCK

    def body(q_ref, k0, k1, k2, k3, v0, v1, v2, v3, o_ref, l_ref):
        dist = _band_dist(pl.program_id(1), length, dil, False)
        kk = _cat((k0, k1, k2, k3))
        vv = _cat((v0, v1, v2, v3))
        for h, cs in enumerate(HEAD_COLS):
            sc = _dot(q_ref[:, cs], kk[:, cs], 1, 1) * 0.125 - SLOPES[h] * dist
            m = jnp.max(sc, axis=1, keepdims=True)
            p = jnp.exp(sc - m)
            z = jnp.sum(p, axis=1, keepdims=True)
            o_ref[:, cs] = _dot(p.astype(BF16), vv[:, cs]) / z
            l_ref[:, h:h + 1] = m + jnp.log(z)

    qspec = pl.BlockSpec((A_QBLOCK, A_WIDTH), lambda r, n: (r * nblk + n, 0))
    lspec = pl.BlockSpec((A_QBLOCK, A_HEADS), lambda r, n: (r * nblk + n, 0))
    return pl.pallas_call(
        body, name=name, grid=(dil, nblk), in_specs=[qspec] + _span_specs(1, 2 * nblk) + _span_specs(2, 2 * nblk),
        out_specs=[qspec, lspec],
        out_shape=[jax.ShapeDtypeStruct((s, A_WIDTH), F32), jax.ShapeDtypeStruct((s, A_HEADS), F32)],
        compiler_params=_cp("parallel", "parallel"),
    )(*([qkv] * 9))


def _attn_merge(os_, ls_, gate, name):
    s, w = gate.shape
    tm = min(ROWS, s)

    def body(o0, o1, o2, l0, l1, l2, g_ref, y_ref, o_ref, l_ref):
        a, b, c = l0[...], l1[...], l2[...]
        m = jnp.maximum(jnp.maximum(a, b), c)
        ea, eb, ec = jnp.exp(a - m), jnp.exp(b - m), jnp.exp(c - m)
        z = ea + eb + ec
        l_ref[...] = m + jnp.log(z)
        e = _head_expander()
        o = _to_lanes(ea / z, e) * o0[...] + _to_lanes(eb / z, e) * o1[...] + _to_lanes(ec / z, e) * o2[...]
        o_ref[...] = o
        y_ref[...] = (o * _silu(g_ref[...])).astype(BF16)

    rs = _row_spec(tm, w)
    ls = _row_spec(tm, A_HEADS)
    return pl.pallas_call(
        body, name=name, grid=(s // tm,), in_specs=[rs] * 3 + [ls] * 3 + [rs], out_specs=[rs, rs, ls],
        out_shape=[jax.ShapeDtypeStruct((s, w), BF16), jax.ShapeDtypeStruct((s, w), F32), jax.ShapeDtypeStruct((s, A_HEADS), F32)],
        compiler_params=_cp("parallel"),
    )(*os_, *ls_, gate)


def _attn_gate_bwd(dyy, o, gate, name):
    s, w = gate.shape
    tm = min(ROWS, s)

    def body(dy_ref, o_ref, g_ref, do_ref, dg_ref, dl_ref):
        dyv, ov, gv = dy_ref[...], o_ref[...], g_ref[...]
        do = dyv * _silu(gv)
        do_ref[...] = do.astype(BF16)
        dg_ref[...] = (dyv * ov * _dsilu(gv)).astype(BF16)
        dl_ref[...] = _per_head_sum(do * ov, _head_expander())

    rs = _row_spec(tm, w)
    return pl.pallas_call(
        body, name=name, grid=(s // tm,), in_specs=[rs] * 3, out_specs=[rs, rs, _row_spec(tm, A_HEADS)],
        out_shape=[jax.ShapeDtypeStruct((s, w), BF16), jax.ShapeDtypeStruct((s, w), BF16), jax.ShapeDtypeStruct((s, A_HEADS), F32)],
        compiler_params=_cp("parallel"),
    )(dyy, o, gate)


def _attn_bwd(qkv, do, lse, delta, g, name):
    s = qkv.shape[0]
    dil = DILATIONS[g]
    length = s // dil
    nblk = length // A_QBLOCK

    def rows(t16):
        return jnp.pad(t16.reshape(dil, length, A_HEADS).transpose(0, 2, 1), ((0, 0), (0, 0), (A_RADIUS, A_RADIUS)))

    def body(q0, q1, q2, q3, k0, k1, k2, k3, v0, v1, v2, v3, d0, d1, d2, d3, lc_ref, ec_ref, la, lb, ea, eb, o_ref):
        dist = _band_dist(pl.program_id(1), length, dil, False)
        qq, kk, vv, dd = _cat((q0, q1, q2, q3)), _cat((k0, k1, k2, k3)), _cat((v0, v1, v2, v3)), _cat((d0, d1, d2, d3))
        qn, kn, vn, dn = _cat((q1, q2)), _cat((k1, k2)), _cat((v1, v2)), _cat((d1, d2))
        lse_r = jnp.concatenate([la[...], lb[...]], axis=1)
        dlt_r = jnp.concatenate([ea[...], eb[...]], axis=1)
        for h, cs in enumerate(HEAD_COLS):
            bias = SLOPES[h] * dist
            p = jnp.exp(_dot(qn[:, cs], kk[:, cs], 1, 1) * 0.125 - bias - lc_ref[:, h:h + 1])
            ds = p * (_dot(dn[:, cs], vv[:, cs], 1, 1) - ec_ref[:, h:h + 1])
            o_ref[:, cs] = (_dot(ds.astype(BF16), kk[:, cs]) * 0.125).astype(BF16)
            pt = jnp.exp(_dot(kn[:, cs], qq[:, cs], 1, 1) * 0.125 - bias - lse_r[h:h + 1, :])
            dst = pt * (_dot(vn[:, cs], dd[:, cs], 1, 1) - dlt_r[h:h + 1, :])
            c0 = h * A_HEAD_DIM
            o_ref[:, A_WIDTH + c0:A_WIDTH + c0 + A_HEAD_DIM] = (_dot(dst.astype(BF16), qq[:, cs]) * 0.125).astype(BF16)
            o_ref[:, 2 * A_WIDTH + c0:2 * A_WIDTH + c0 + A_HEAD_DIM] = _dot(pt.astype(BF16), dd[:, cs]).astype(BF16)

    nb64 = 2 * nblk
    dspecs = _span_specs(0, nb64)
    cspec = pl.BlockSpec((A_QBLOCK, A_HEADS), lambda r, n: (r * nblk + n, 0))
    rspecs = [pl.BlockSpec((None, A_HEADS, A_QBLOCK), lambda r, n: (r, 0, n)), pl.BlockSpec((None, A_HEADS, A_QBLOCK), lambda r, n: (r, 0, n + 1))]
    lse_r, dlt_r = rows(lse), rows(delta)
    return pl.pallas_call(
        body, name=name, grid=(dil, nblk),
        in_specs=_span_specs(0, nb64) + _span_specs(1, nb64) + _span_specs(2, nb64) + dspecs + [cspec, cspec] + rspecs * 2,
        out_specs=pl.BlockSpec((A_QBLOCK, 3 * A_WIDTH), lambda r, n: (r * nblk + n, 0)),
        out_shape=jax.ShapeDtypeStruct((s, 3 * A_WIDTH), BF16),
        compiler_params=_cp("parallel", "parallel"),
    )(*([qkv] * 12), *([do] * 4), lse, delta, lse_r, lse_r, dlt_r, dlt_r)


def _attn_layer_fwd(h, w_qkv, w_gate, w_out, li):
    nm = lambda t: f"a{li}_{t}"
    gate = _mm(h, w_gate, name=nm("gate"))
    hs, qkvs, os_, ls_ = [], [], [], []
    for g, dil in enumerate(DILATIONS):
        hg = _to_residue(h, dil)
        qkv = _mm(hg, w_qkv[:, g * 3 * A_WIDTH:(g + 1) * 3 * A_WIDTH], out_dtype=BF16, name=nm(f"qkv{g}"))
        o, l = _attn_fwd(qkv, g, nm(f"attn{g}"))
        hs.append(hg)
        qkvs.append(qkv)
        os_.append(_from_residue(o, dil))
        ls_.append(_from_residue(l, dil))
    y, o, lse = _attn_merge(os_, ls_, gate, nm("merge"))
    out = _mm(y, w_out, name=nm("out"))
    return out, (hs, qkvs, gate, y, o, lse)


def _attn_layer_bwd(dy, h, saved, w_qkv, w_gate, w_out, li):
    nm = lambda t: f"a{li}_{t}"
    hs, qkvs, gate, y, o, lse = saved
    g_w_out = _mm(y, dy, ta=True, out_dtype=BF16, name=nm("dwout"))
    dyy = _mm(dy, w_out, tb=True, name=nm("dyy"))
    do, dgate, delta = _attn_gate_bwd(dyy, o, gate, nm("gatebwd"))
    dhs, dws = [], []
    for g, dil in enumerate(DILATIONS):
        dqkv = _attn_bwd(qkvs[g], _to_residue(do, dil), _to_residue(lse, dil), _to_residue(delta, dil), g, nm(f"attnbwd{g}"))
        wg = w_qkv[:, g * 3 * A_WIDTH:(g + 1) * 3 * A_WIDTH]
        dws.append(_mm(hs[g], dqkv, ta=True, out_dtype=BF16, name=nm(f"dwqkv{g}")))
        add = _mm(dgate, w_gate, tb=True, name=nm("dh_gate")) if g == 0 else None
        dhs.append(_from_residue(_mm(dqkv, wg, tb=True, add=add, name=nm(f"dh_qkv{g}")), dil))
    g_w_in = jnp.concatenate(dws + [_mm(h, dgate, ta=True, out_dtype=BF16, name=nm("dwgate"))], axis=1)
    return dhs, g_w_in, g_w_out


SSM_INNER = SSM_HEADS * SSM_HEAD_DIM
SSM_BC = SSM_GROUPS * SSM_STATE
SSM_CONV_DIM = SSM_INNER + 2 * SSM_BC
GW = SSM_REP * SSM_HEAD_DIM
T = SSM_CHUNK
HALO = 8


def _conv_specs(tm, tn, s):
    nb8 = s // HALO
    cur = pl.BlockSpec((tm, tn), lambda j, i: (i, j))
    prev = pl.BlockSpec((HALO, tn), lambda j, i: (jnp.maximum(i * (tm // HALO) - 1, 0), j))
    nxt = pl.BlockSpec((HALO, tn), lambda j, i: (jnp.minimum((i + 1) * (tm // HALO), nb8 - 1), j))
    return prev, cur, nxt


def _extend(prev_ref, cur_ref, nxt_ref, i, nrow):
    p = jnp.where(i == 0, 0.0, prev_ref[...])
    n = jnp.where(i == nrow - 1, 0.0, nxt_ref[...])
    return jnp.concatenate([p, cur_ref[...], n], axis=0)


def _shift_rows(ext, off, tm):
    rows = ext.shape[0]
    return pltpu.roll(ext, (-off) % rows, 0)[HALO:HALO + tm]


def _conv_fwd(xraw, w, b, name):
    s, cdim = xraw.shape
    tm, tn = min(256, s), 1024
    nrow = s // tm

    def body(p_ref, c_ref, n_ref, w_ref, b_ref, pre_ref, act_ref):
        ext = _extend(p_ref, c_ref, n_ref, pl.program_id(1), nrow)
        acc = jnp.broadcast_to(b_ref[...], (tm, tn))
        for k in range(SSM_CONV):
            acc = acc + w_ref[k:k + 1, :] * _shift_rows(ext, k - SSM_CONV // 2, tm)
        pre_ref[...] = acc
        act_ref[...] = _silu(acc)

    prev, cur, nxt = _conv_specs(tm, tn, s)
    return pl.pallas_call(
        body, name=name, grid=(cdim // tn, nrow),
        in_specs=[prev, cur, nxt, pl.BlockSpec((SSM_CONV, tn), lambda j, i: (0, j)), pl.BlockSpec((1, tn), lambda j, i: (0, j))],
        out_specs=[cur, cur], out_shape=[jax.ShapeDtypeStruct((s, cdim), F32)] * 2,
        compiler_params=_cp("parallel", "parallel"),
    )(xraw, xraw, xraw, w, b)


def _conv_bwd(dact, pre, xraw, w, name):
    s, cdim = xraw.shape
    tm, tn = min(256, s), 1024
    nrow = s // tm

    def body(dp, dc, dn, pp, pc, pn, xp, xc, xn, w_ref, dx_ref, red_ref):
        i = pl.program_id(1)

        @pl.when(i == 0)
        def _():
            red_ref[...] = jnp.zeros_like(red_ref)

        dpre = _extend(dp, dc, dn, i, nrow) * _dsilu(_extend(pp, pc, pn, i, nrow))
        xext = _extend(xp, xc, xn, i, nrow)
        dcur = dpre[HALO:HALO + tm]
        acc = jnp.zeros((tm, tn), F32)
        for k in range(SSM_CONV):
            off = k - SSM_CONV // 2
            acc = acc + w_ref[k:k + 1, :] * _shift_rows(dpre, -off, tm)
            red_ref[k:k + 1, :] += jnp.sum(dcur * _shift_rows(xext, off, tm), axis=0, keepdims=True)
        red_ref[SSM_CONV:SSM_CONV + 1, :] += jnp.sum(dcur, axis=0, keepdims=True)
        dx_ref[...] = acc.astype(BF16)

    prev, cur, nxt = _conv_specs(tm, tn, s)
    return pl.pallas_call(
        body, name=name, grid=(cdim // tn, nrow),
        in_specs=[prev, cur, nxt] * 3 + [pl.BlockSpec((SSM_CONV, tn), lambda j, i: (0, j))],
        out_specs=[cur, pl.BlockSpec((8, tn), lambda j, i: (0, j))],
        out_shape=[jax.ShapeDtypeStruct((s, cdim), BF16), jax.ShapeDtypeStruct((8, cdim), F32)],
        compiler_params=_cp("parallel", "arbitrary"),
    )(dact, dact, dact, pre, pre, pre, xraw, xraw, xraw, w)


def _tri(lower):
    r = lax.broadcasted_iota(jnp.int32, (T, T), 0)
    c = lax.broadcasted_iota(jnp.int32, (T, T), 1)
    return (r >= c) if lower else (r <= c)


def _softplus(x):
    return jnp.maximum(x, 0.0) + jnp.log(1.0 + jnp.exp(-jnp.abs(x)))


def _dt_prep(dt_raw, bias, a_log, name):
    s = dt_raw.shape[0]
    nc = s // T

    def body(r_ref, b_ref, a_ref, dt_ref, cum_ref, cumt_ref, dtw_ref, cumw_ref):
        dt = _softplus(r_ref[...] + b_ref[...])
        da = dt * (-jnp.exp(a_ref[...]))
        pre = _dot_exact(_tri(True).astype(BF16), da)
        suf = _dot_exact(_tri(False).astype(BF16), da)
        lane = lax.broadcasted_iota(jnp.int32, (T, LANE), 1)
        cum = jnp.where(lane < SSM_HEADS, pre, suf)
        dt_ref[...] = dt
        cum_ref[...] = cum
        cumt_ref[...] = cum.T
        r = lax.broadcasted_iota(jnp.int32, (LANE, SSM_INNER), 0)
        c = lax.broadcasted_iota(jnp.int32, (LANE, SSM_INNER), 1)
        for d in range(2):
            h = r - d * SSM_HEADS
            e = ((c >= h * SSM_HEAD_DIM) & (c < (h + 1) * SSM_HEAD_DIM)).astype(BF16)
            dtw_ref[d] = _to_lanes(dt, e)
            cumw_ref[d] = _to_lanes(cum, e)

    blk = pl.BlockSpec((T, LANE), lambda c: (c, 0))
    vec = pl.BlockSpec((1, LANE), lambda c: (0, 0))
    wide = pl.BlockSpec((2, T, SSM_INNER), lambda c: (0, c, 0))
    return pl.pallas_call(
        body, name=name, grid=(nc,), in_specs=[blk, vec, vec],
        out_specs=[blk, blk, pl.BlockSpec((None, LANE, T), lambda c: (c, 0, 0)), wide, wide],
        out_shape=[jax.ShapeDtypeStruct((s, LANE), F32), jax.ShapeDtypeStruct((s, LANE), F32), jax.ShapeDtypeStruct((nc, LANE, T), F32),
                   jax.ShapeDtypeStruct((2, s, SSM_INNER), F32), jax.ShapeDtypeStruct((2, s, SSM_INNER), F32)],
        compiler_params=_cp("parallel"),
    )(dt_raw, bias, a_log)


def _by_group(t):
    s = t.shape[0]
    return t[:, :2 * SSM_HEADS].reshape(s, 2 * SSM_GROUPS, SSM_REP).transpose(1, 0, 2)


def _from_group(tf, tb):
    s = tf.shape[1]
    t = jnp.concatenate([tf, tb], axis=0).transpose(1, 0, 2).reshape(s, 2 * SSM_HEADS)
    return jnp.pad(t, ((0, 0), (0, LANE - 2 * SSM_HEADS)))


def _decay_mats(acol, arow, rev):
    after = _tri(not rev)
    return jnp.where(after, jnp.exp(jnp.where(after, acol - arow, 0.0)), 0.0)


PAIRS = SSM_REP // 2


def _low_lanes():
    return lax.broadcasted_iota(jnp.int32, (T, LANE), 1) < SSM_HEAD_DIM


def _block_diag(v, low):
    zero = jnp.zeros_like(v)
    return jnp.concatenate([jnp.where(low, v, zero), jnp.where(low, zero, v)], axis=0)


def _scan_specs(nc, rev, ci):
    nxb = SSM_INNER // LANE
    d = 1 if rev else 0
    kofs = SSM_GROUPS if rev else 0
    return [
        pl.BlockSpec((T, GW), lambda g, c: (ci(c), g)),
        pl.BlockSpec((T, LANE), lambda g, c: (ci(c), nxb + g)),
        pl.BlockSpec((T, LANE), lambda g, c: (ci(c), nxb + SSM_GROUPS + g)),
        pl.BlockSpec((None, T, GW), lambda g, c: (d, ci(c), g)),
        pl.BlockSpec((None, T, GW), lambda g, c: (d, ci(c), g)),
        pl.BlockSpec((None, T, SSM_REP), lambda g, c: (kofs + g, ci(c), 0)),
        pl.BlockSpec((None, None, SSM_REP, T), lambda g, c: (kofs + g, ci(c), 0, 0)),
    ]


def _ssd_scan(xbc, dtw, cumw, cumk, cumtk, rev, name):
    s = xbc.shape[0]
    nc = s // T
    last = 0 if rev else T - 1
    ci = (lambda c: nc - 1 - c) if rev else (lambda c: c)

    def body(x_ref, b_ref, c_ref, dtw_ref, cumw_ref, cum_ref, cumt_ref, y_ref, st_ref, state):
        @pl.when(pl.program_id(1) == 0)
        def _():
            state[...] = jnp.zeros_like(state)

        bm = b_ref[...]
        cm = c_ref[...].astype(BF16)
        cb = _dot(cm, bm.astype(BF16), 1, 1)
        bt = bm.T.astype(BF16)
        low = _low_lanes()
        for p in range(PAIRS):
            ls = slice(p * LANE, (p + 1) * LANE)
            acum = cumw_ref[:, ls]
            u = x_ref[:, ls] * dtw_ref[:, ls]
            tot = cumw_ref[last:last + 1, ls]
            m = [(cb * _decay_mats(cum_ref[:, r:r + 1], cumt_ref[r:r + 1, :], rev)).astype(BF16) for r in (2 * p, 2 * p + 1)]
            st = state[p]
            st_ref[p] = st
            yd = _dot(jnp.concatenate(m, axis=1), _block_diag(u.astype(BF16), low))
            yo = jnp.exp(acum) * _dot(cm, st.astype(BF16))
            y_ref[:, ls] = yd + yo
            state[p] = jnp.exp(tot) * st + _dot(bt, (jnp.exp(tot - acum) * u).astype(BF16))

    return pl.pallas_call(
        body, name=name, grid=(SSM_GROUPS, nc), in_specs=_scan_specs(nc, rev, ci),
        out_specs=[
            pl.BlockSpec((T, GW), lambda g, c: (ci(c), g)),
            pl.BlockSpec((None, PAIRS, SSM_STATE, LANE), lambda g, c: (ci(c), g, 0, 0)),
        ],
        out_shape=[jax.ShapeDtypeStruct((s, SSM_INNER), F32), jax.ShapeDtypeStruct((nc, SSM_HEADS // 2, SSM_STATE, LANE), F32)],
        scratch_shapes=[pltpu.VMEM((PAIRS, SSM_STATE, LANE), F32)],
        compiler_params=_cp("parallel", "arbitrary"),
    )(xbc, xbc, xbc, dtw, cumw, cumk, cumtk)


def _ssd_scan_bwd(xbc, dtw, cumw, cumk, cumtk, dy, states, dvec, prev, rev, name):
    s = xbc.shape[0]
    nc = s // T
    last = 0 if rev else T - 1
    ci = (lambda c: c) if rev else (lambda c: nc - 1 - c)
    has_prev = prev is not None

    def body(*refs):
        x_ref, b_ref, c_ref, dtw_ref, cumw_ref, cum_ref, cumt_ref, dy_ref, st_ref, dv_ref = refs[:10]
        refs = refs[10:]
        if has_prev:
            pdx, pdb, pdc = refs[:3]
            refs = refs[3:]
        dx_ref, db_ref, dc_ref, ddt_ref, dda_ref, dstate, rs_buf, in_buf, k_buf = refs

        @pl.when(pl.program_id(1) == 0)
        def _():
            dstate[...] = jnp.zeros_like(dstate)

        rs_buf[...] = jnp.zeros_like(rs_buf)
        in_buf[...] = jnp.zeros_like(in_buf)
        k_buf[...] = jnp.zeros_like(k_buf)
        bm = b_ref[...].astype(BF16)
        cm = c_ref[...].astype(BF16)
        cbt = _dot(bm, cm, 1, 1)
        cb = _dot(cm, bm, 1, 1)
        ct = c_ref[...].T.astype(BF16)
        after = _tri(not rev)
        before = _tri(rev)
        from_k = before.astype(BF16)
        ri = lax.broadcasted_iota(jnp.int32, (T, T), 0)
        cj = lax.broadcasted_iota(jnp.int32, (T, T), 1)
        strictly_before = (cj > ri) if rev else (cj < ri)
        dcb = jnp.zeros((T, T), F32)
        dc_acc = jnp.zeros((T, SSM_STATE), F32)
        db_acc = jnp.zeros((T, SSM_STATE), F32)
        low = _low_lanes()
        ri2 = lax.broadcasted_iota(jnp.int32, (LANE, LANE), 0)
        cj2 = lax.broadcasted_iota(jnp.int32, (LANE, LANE), 1)
        halves = ((ri2 < SSM_HEAD_DIM) == (cj2 == 0)) & (cj2 < 2)
        halves = halves.astype(BF16)

        def head_sums(v):
            hi = v.astype(BF16)
            lo = (v - hi.astype(F32)).astype(BF16)
            return _dot(hi, halves) + _dot(lo, halves)

        for p in range(PAIRS):
            ls = slice(p * LANE, (p + 1) * LANE)
            c2 = slice(2 * p, 2 * p + 2)
            lm, lmt = [], []
            for r in (2 * p, 2 * p + 1):
                acol = cum_ref[:, r:r + 1]
                arow = cumt_ref[r:r + 1, :]
                lm.append(jnp.where(after, jnp.exp(jnp.where(after, acol - arow, 0.0)), 0.0))
                lmt.append(jnp.where(before, jnp.exp(jnp.where(before, arow - acol, 0.0)), 0.0))
            acum = cumw_ref[:, ls]
            tot = cumw_ref[last:last + 1, ls]
            dtl = dtw_ref[:, ls]
            xl = x_ref[:, ls]
            u = xl * dtl
            ub = u.astype(BF16)
            dyl = dy_ref[:, ls]
            dyb = dyl.astype(BF16)
            st = st_ref[p]
            stb = st.astype(BF16)
            dst = dstate[p]
            dstb = dst.astype(BF16)
            dec = jnp.exp(tot - acum)
            eac = jnp.exp(acum)
            etot = jnp.exp(tot)
            du_off = dec * _dot(bm, dstb)
            mt = jnp.concatenate([(cbt * lmt[0]).astype(BF16), (cbt * lmt[1]).astype(BF16)], axis=1)
            du = _dot(mt, _block_diag(dyb, low)) + du_off
            zero = jnp.zeros_like(dyb)
            gl = [_dot(jnp.where(low, dyb, zero), ub, 1, 1) * lm[0], _dot(jnp.where(low, zero, dyb), ub, 1, 1) * lm[1]]
            dcb = dcb + gl[0] + gl[1]
            dc_acc = dc_acc + _dot((eac * dyl).astype(BF16), stb, 1, 1)
            db_acc = db_acc + _dot((dec * u).astype(BF16), dstb, 1, 1)
            w = jnp.concatenate([(gl[0] * cb).astype(BF16), (gl[1] * cb).astype(BF16)], axis=1)
            crossing = _dot(from_k, w)
            for j in range(2):
                cr = jnp.where(strictly_before, crossing[:, j * T:(j + 1) * T], 0.0)
                in_buf[:, 2 * p + j:2 * p + j + 1] = jnp.sum(cr, axis=1, keepdims=True)
            y_off = eac * _dot(cm, stb)
            udu = u * du_off
            rs_buf[:, c2] = head_sums(dyl * y_off - udu)[:, 0:2]
            col = jnp.sum(dst * (etot * st) + udu, axis=0, keepdims=True)
            k_buf[0:1, c2] = head_sums(jnp.broadcast_to(col, (8, LANE)))[0:1, 0:2]
            ddt_ref[:, c2] = head_sums(du * xl)[:, 0:2]
            dx = du * dtl
            if has_prev:
                dx = dx + pdx[:, ls]
            else:
                dx = dx + dyl * dv_ref[:, ls]
            dx_ref[:, ls] = dx
            dstate[p] = etot * dst + _dot(ct, (eac * dyl).astype(BF16))
        dda = in_buf[...] + _dot_exact(from_k, rs_buf[...]) + k_buf[0:1, :]
        dda_ref[...] = dda[:, :SSM_REP]
        dcbb = dcb.astype(BF16)
        dc = dc_acc + _dot(dcbb, bm)
        db = db_acc + _dot(dcbb, cm, 0, 0)
        if has_prev:
            dc = dc + pdc[...]
            db = db + pdb[...]
        dc_ref[...] = dc
        db_ref[...] = db

    xspec = pl.BlockSpec((T, GW), lambda g, c: (ci(c), g))
    gspec = pl.BlockSpec((T, LANE), lambda g, c: (ci(c), g))
    in_specs = _scan_specs(nc, rev, ci) + [
        xspec,
        pl.BlockSpec((None, PAIRS, SSM_STATE, LANE), lambda g, c: (ci(c), g, 0, 0)),
        pl.BlockSpec((1, GW), lambda g, c: (0, g)),
    ]
    args = [xbc, xbc, xbc, dtw, cumw, cumk, cumtk, dy, states, dvec]
    if has_prev:
        in_specs += [xspec, gspec, gspec]
        args += list(prev)
    ospec8 = pl.BlockSpec((None, T, SSM_REP), lambda g, c: (g, ci(c), 0))
    return pl.pallas_call(
        body, name=name, grid=(SSM_GROUPS, nc), in_specs=in_specs,
        out_specs=[xspec, gspec, gspec, ospec8, ospec8],
        out_shape=[jax.ShapeDtypeStruct((s, SSM_INNER), F32), jax.ShapeDtypeStruct((s, SSM_BC), F32), jax.ShapeDtypeStruct((s, SSM_BC), F32),
                   jax.ShapeDtypeStruct((SSM_GROUPS, s, SSM_REP), F32), jax.ShapeDtypeStruct((SSM_GROUPS, s, SSM_REP), F32)],
        scratch_shapes=[pltpu.VMEM((PAIRS, SSM_STATE, LANE), F32), pltpu.VMEM((T, LANE), F32), pltpu.VMEM((T, LANE), F32),
                        pltpu.VMEM((8, LANE), F32)],
        compiler_params=_cp("parallel", "arbitrary"),
    )(*args)


def _ssd_post(yf, yb, xbc, z, dvec, nw, name):
    s = z.shape[0]
    tm = min(256, s)

    def body(yf_ref, yb_ref, x_ref, z_ref, dv_ref, nw_ref, o_ref):
        ys = yf_ref[...] + yb_ref[...] + dv_ref[...] * x_ref[...]
        yg = ys * _silu(z_ref[...])
        ms = jnp.mean(yg * yg, axis=1, keepdims=True)
        o_ref[...] = (yg * lax.rsqrt(ms + RMS_EPS) * nw_ref[...]).astype(BF16)

    rs = _row_spec(tm, SSM_INNER)
    vs = _vec_spec(SSM_INNER)
    return pl.pallas_call(
        body, name=name, grid=(s // tm,), in_specs=[rs, rs, rs, rs, vs, vs], out_specs=rs,
        out_shape=jax.ShapeDtypeStruct((s, SSM_INNER), BF16), compiler_params=_cp("parallel"),
    )(yf, yb, xbc, z, dvec, nw)


def _ssd_post_bwd(dyn, yf, yb, xbc, z, dvec, nw, name):
    s = z.shape[0]
    tm = min(256, s)

    def body(dyn_ref, yf_ref, yb_ref, x_ref, z_ref, dv_ref, nw_ref, dys_ref, dz_ref, red_ref):
        @pl.when(pl.program_id(0) == 0)
        def _():
            red_ref[...] = jnp.zeros_like(red_ref)

        xv, zv = x_ref[...], z_ref[...]
        ys = yf_ref[...] + yb_ref[...] + dv_ref[...] * xv
        sz = _silu(zv)
        yg = ys * sz
        rstd = lax.rsqrt(jnp.mean(yg * yg, axis=1, keepdims=True) + RMS_EPS)
        yhat = yg * rstd
        dynv = dyn_ref[...]
        dyh = dynv * nw_ref[...]
        dyg = rstd * (dyh - yhat * jnp.mean(dyh * yhat, axis=1, keepdims=True))
        dys = dyg * sz
        dys_ref[...] = dys
        dz_ref[...] = (dyg * ys * _dsilu(zv)).astype(BF16)
        red_ref[0:1, :] += jnp.sum(dynv * yhat, axis=0, keepdims=True)
        red_ref[1:2, :] += jnp.sum(dys * xv, axis=0, keepdims=True)

    rs = _row_spec(tm, SSM_INNER)
    vs = _vec_spec(SSM_INNER)
    return pl.pallas_call(
        body, name=name, grid=(s // tm,), in_specs=[rs, rs, rs, rs, rs, vs, vs],
        out_specs=[rs, rs, _vec_spec(SSM_INNER, 8)],
        out_shape=[jax.ShapeDtypeStruct((s, SSM_INNER), F32), jax.ShapeDtypeStruct((s, SSM_INNER), BF16), jax.ShapeDtypeStruct((8, SSM_INNER), F32)],
        compiler_params=_cp("arbitrary"),
    )(dyn, yf, yb, xbc, z, dvec, nw)


def _dt_bwd(dt_raw, bias, a_log, dt, ddt, dda, name):
    s = dt_raw.shape[0]
    tm = min(1024, s)

    def body(r_ref, b_ref, a_ref, dt_ref, ddt_ref, dda_ref, o_ref, red_ref):
        @pl.when(pl.program_id(0) == 0)
        def _():
            red_ref[...] = jnp.zeros_like(red_ref)

        a = -jnp.exp(a_ref[...])
        ddav = dda_ref[...]
        draw = (ddt_ref[...] + a * ddav) * _sigmoid(r_ref[...] + b_ref[...])
        o_ref[...] = draw.astype(BF16)
        red_ref[0:1, :] += jnp.sum(draw, axis=0, keepdims=True)
        red_ref[1:2, :] += a * jnp.sum(ddav * dt_ref[...], axis=0, keepdims=True)

    rs = _row_spec(tm, LANE)
    vs = _vec_spec(LANE)
    return pl.pallas_call(
        body, name=name, grid=(s // tm,), in_specs=[rs, vs, vs, rs, rs, rs], out_specs=[rs, _vec_spec(LANE, 8)],
        out_shape=[jax.ShapeDtypeStruct((s, LANE), BF16), jax.ShapeDtypeStruct((8, LANE), F32)],
        compiler_params=_cp("arbitrary"),
    )(dt_raw, bias, a_log, dt, ddt, dda)


def _pad_lanes(v):
    v = v.reshape(1, -1)
    return jnp.pad(v, ((0, 0), (0, LANE - v.shape[1])))


def _ssd_prep_weights(w_in, conv_w, conv_b, dt_bias, a_log, d_skip, norm_w, w_out):
    return dict(
        w_z=w_in[:, :SSM_INNER].astype(BF16),
        w_xbc=w_in[:, SSM_INNER:SSM_INNER + SSM_CONV_DIM].astype(BF16),
        w_dt=jnp.pad(w_in[:, SSM_INNER + SSM_CONV_DIM:], ((0, 0), (0, LANE - 2 * SSM_HEADS))).astype(BF16),
        conv_w=conv_w, conv_b=conv_b.reshape(1, -1), bias=_pad_lanes(dt_bias), a_log=_pad_lanes(a_log),
        dvec=jnp.repeat(d_skip, SSM_HEAD_DIM).reshape(1, -1), nw=norm_w.reshape(1, -1), w_out=w_out.astype(BF16),
    )


def _ssd_layer_fwd(h, w, li):
    nm = lambda t: f"b{li}_{t}"
    z = _mm(h, w["w_z"], name=nm("z"))
    xraw = _mm(h, w["w_xbc"], name=nm("xbc"))
    dt_raw = _mm(h, w["w_dt"], name=nm("dt"))
    pre, xbc = _conv_fwd(xraw, w["conv_w"], w["conv_b"], nm("conv"))
    dt, cum, cumt, dtw, cumw = _dt_prep(dt_raw, w["bias"], w["a_log"], nm("dtprep"))
    nc = cumt.shape[0]
    cumk = _by_group(cum)
    cumtk = cumt[:, :2 * SSM_HEADS].reshape(nc, 2 * SSM_GROUPS, SSM_REP, T).transpose(1, 0, 2, 3)
    yf, stf = _ssd_scan(xbc, dtw, cumw, cumk, cumtk, False, nm("scan_f"))
    yb, stb = _ssd_scan(xbc, dtw, cumw, cumk, cumtk, True, nm("scan_b"))
    yn = _ssd_post(yf, yb, xbc, z, w["dvec"], w["nw"], nm("post"))
    out = _mm(yn, w["w_out"], name=nm("out"))
    return out, (z, xraw, dt_raw, pre, xbc, dt, dtw, cumw, cumk, cumtk, yf, stf, yb, stb, yn)


def _ssd_layer_bwd(dy, h, saved, w, li):
    nm = lambda t: f"b{li}_{t}"
    z, xraw, dt_raw, pre, xbc, dt, dtw, cumw, cumk, cumtk, yf, stf, yb, stb, yn = saved
    g_w_out = _mm(yn, dy, ta=True, out_dtype=BF16, name=nm("dwout"))
    dyn = _mm(dy, w["w_out"], tb=True, name=nm("dyn"))
    dys, dz, pred = _ssd_post_bwd(dyn, yf, yb, xbc, z, w["dvec"], w["nw"], nm("postbwd"))
    dx1, db1, dc1, ddt_f, dda_f = _ssd_scan_bwd(xbc, dtw, cumw, cumk, cumtk, dys, stf, w["dvec"], None, False, nm("scanbwd_f"))
    dx, db, dc, ddt_b, dda_b = _ssd_scan_bwd(xbc, dtw, cumw, cumk, cumtk, dys, stb, w["dvec"], (dx1, db1, dc1), True, nm("scanbwd_b"))
    dact = jnp.concatenate([dx, db, dc], axis=1)
    dxraw, cred = _conv_bwd(dact, pre, xraw, w["conv_w"], nm("convbwd"))
    draw, dred = _dt_bwd(dt_raw, w["bias"], w["a_log"], dt, _from_group(ddt_f, ddt_b), _from_group(dda_f, dda_b), nm("dtbwd"))
    dh = _mm(dz, w["w_z"], tb=True, name=nm("dh_z"))
    dh = _mm(dxraw, w["w_xbc"], tb=True, add=dh, name=nm("dh_xbc"))
    dh = _mm(draw, w["w_dt"], tb=True, add=dh, name=nm("dh_dt"))
    g_w_in = jnp.concatenate([_mm(h, dz, ta=True, out_dtype=BF16, name=nm("dwz")), _mm(h, dxraw, ta=True, out_dtype=BF16, name=nm("dwxbc")),
                              _mm(h, draw, ta=True, out_dtype=BF16, name=nm("dwdt"))[:, :2 * SSM_HEADS]], axis=1)
    grads = (g_w_in, cred[:SSM_CONV], cred[SSM_CONV], dred[0, :2 * SSM_HEADS].reshape(2, SSM_HEADS),
             dred[1, :2 * SSM_HEADS].reshape(2, SSM_HEADS), pred[1].reshape(SSM_HEADS, SSM_HEAD_DIM).sum(axis=1), pred[0], g_w_out)
    return dh, grads


B_GRAD_NAMES = ("b_w_in", "b_conv_w", "b_conv_b", "b_dt_bias", "b_a_log", "b_d", "b_norm_w", "b_w_out")


def _local_step(x, tgt, mod, w):
    d = x.shape[1]
    qkv_cols = QKV_COLS
    layers = []
    for i in range(DEPTH):
        j = i // 2
        if i % 2 == 0:
            layers.append((w["a_w_in"][j][:, :qkv_cols].astype(BF16), w["a_w_in"][j][:, qkv_cols:].astype(BF16), w["a_w_out"][j].astype(BF16)))
        else:
            layers.append(_ssd_prep_weights(w["b_w_in"][j], w["b_conv_w"][j], w["b_conv_b"][j], w["b_dt_bias"][j], w["b_a_log"][j],
                                            w["b_d"][j], w["b_norm_w"][j], w["b_w_out"][j]))
    saved = []
    for i in range(DEPTH):
        shift, scale, gate = mod[i:i + 1, :d], mod[i:i + 1, d:2 * d], mod[i:i + 1, 2 * d:]
        h = _modulate(x, scale, shift, f"l{i}_mod")
        if i % 2 == 0:
            out, sv = _attn_layer_fwd(h, *layers[i], i)
        else:
            out, sv = _ssd_layer_fwd(h, layers[i], i)
        xn = _resid_ln(x, out, gate, w["ln_g"][i:i + 1], w["ln_b"][i:i + 1], f"l{i}_ln")
        saved.append((x, h, out, sv))
        x = xn
    dx, lred = _loss_grad(x, tgt, "loss")
    loss = 0.5 * jnp.sum(lred[0]) / d
    dmod, g_ln_g, g_ln_b = [None] * DEPTH, [None] * DEPTH, [None] * DEPTH
    ga_in, ga_out = [None, None], [None, None]
    gb = [None, None]
    for i in reversed(range(DEPTH)):
        j = i // 2
        xi, h, out, sv = saved[i]
        scale, gate = mod[i:i + 1, d:2 * d], mod[i:i + 1, 2 * d:]
        du, dy, red = _resid_ln_bwd(xi, out, dx, gate, w["ln_g"][i:i + 1], f"l{i}_lnbwd")
        g_ln_g[i], g_ln_b[i] = red[1], red[2]
        if i % 2 == 0:
            dhs, ga_in[j], ga_out[j] = _attn_layer_bwd(dy, h, sv, *layers[i], i)
        else:
            dh, gb[j] = _ssd_layer_bwd(dy, h, sv, layers[i], i)
            dhs = [dh]
        dx, red2 = _modulate_bwd(du, dhs, xi, scale, f"l{i}_modbwd")
        dmod[i] = jnp.concatenate([red2[1], red2[0], red[0]])
    grads = {"ln_g": jnp.stack(g_ln_g), "ln_b": jnp.stack(g_ln_b), "a_w_in": jnp.stack(ga_in), "a_w_out": jnp.stack(ga_out)}
    for k, n in enumerate(B_GRAD_NAMES):
        grads[n] = jnp.stack([gb[0][k], gb[1][k]])
    return loss, dx, jnp.stack(dmod), grads


MESH = pl.DeviceIdType.MESH
ANY = pl.BlockSpec(memory_space=pl.ANY)
N_DEV = 8
N_SHARD = 4


def _flip(v, bit):
    return 1 - v if bit else v


def _all_gather8(v, name):
    def body(v_ref, o_ref, send_sems, recv_sems, local_sem):
        x, y, c = lax.axis_index("x"), lax.axis_index("y"), lax.axis_index("c")
        me = 4 * x + 2 * y + c
        local = pltpu.make_async_copy(v_ref, o_ref.at[me], local_sem)
        local.start()
        copies = []
        for k in range(1, N_DEV):
            peer = (_flip(x, k & 4), _flip(y, k & 2), _flip(c, k & 1))
            copies.append(pltpu.make_async_remote_copy(
                src_ref=v_ref, dst_ref=o_ref.at[me], send_sem=send_sems.at[k - 1], recv_sem=recv_sems.at[k - 1],
                device_id=peer, device_id_type=MESH))
        for cp in copies:
            cp.start()
        for cp in copies:
            cp.wait()
        local.wait()

    return pl.pallas_call(
        body, name=name, in_specs=[ANY], out_specs=ANY, out_shape=jax.ShapeDtypeStruct((N_DEV,) + v.shape, v.dtype),
        scratch_shapes=[pltpu.SemaphoreType.DMA((N_DEV - 1,)), pltpu.SemaphoreType.DMA((N_DEV - 1,)), pltpu.SemaphoreType.DMA],
    )(v)


def _transpose_shards(srcs, name):
    n = len(srcs)
    n_rem = (N_SHARD - 1) * n

    def body(*refs):
        s_refs, o_refs = refs[:n], refs[n:2 * n]
        send_sems, recv_sems, local_sems = refs[2 * n:]
        x, y, c = lax.axis_index("x"), lax.axis_index("y"), lax.axis_index("c")
        m = 2 * x + y
        local = [pltpu.make_async_copy(s_refs[a].at[m], o_refs[a].at[m], local_sems.at[a]) for a in range(n)]
        remote = []
        for k in range(1, N_SHARD):
            px, py = _flip(x, k & 2), _flip(y, k & 1)
            for a in range(n):
                i = (k - 1) * n + a
                remote.append(pltpu.make_async_remote_copy(
                    src_ref=s_refs[a].at[2 * px + py], dst_ref=o_refs[a].at[m], send_sem=send_sems.at[i], recv_sem=recv_sems.at[i],
                    device_id=(px, py, c), device_id_type=MESH))
        for cp in local + remote:
            cp.start()
        for cp in remote + local:
            cp.wait()

    return pl.pallas_call(
        body, name=name, in_specs=[ANY] * n, out_specs=[ANY] * n, out_shape=[jax.ShapeDtypeStruct(s.shape, s.dtype) for s in srcs],
        scratch_shapes=[pltpu.SemaphoreType.DMA((n_rem,)), pltpu.SemaphoreType.DMA((n_rem,)), pltpu.SemaphoreType.DMA((n,))],
    )(*srcs)


def _gather_shards(src, name):
    rows = src.shape[0]
    half = rows // 2
    n_ici = N_SHARD - 1

    def body(s_ref, o_ref, send_sems, recv_sems, local_sem):
        x, y, c = lax.axis_index("x"), lax.axis_index("y"), lax.axis_index("c")
        m = 2 * x + y
        sibling = (x, y, 1 - c)
        my_half = pl.ds(pl.multiple_of(c * half, 16), half)
        its_half = pl.ds(pl.multiple_of((1 - c) * half, 16), half)
        local = pltpu.make_async_copy(s_ref, o_ref.at[m], local_sem)
        local.start()
        chips = [(_flip(x, k & 2), _flip(y, k & 1)) for k in range(1, N_SHARD)]

        def copy(sem, src_ref, dst_ref, to):
            return pltpu.make_async_remote_copy(src_ref=src_ref, dst_ref=dst_ref, send_sem=send_sems.at[sem],
                                                recv_sem=recv_sems.at[sem], device_id=to, device_id_type=MESH)

        first = [copy(i, s_ref.at[my_half], o_ref.at[m, my_half], (px, py, c)) for i, (px, py) in enumerate(chips)]
        for cp in first:
            cp.start()
        passed = []
        for i, (px, py) in enumerate(chips):
            landed = o_ref.at[2 * px + py, my_half]
            copy(i, landed, landed, (px, py, c)).wait_recv()
            passed.append(copy(n_ici + i, landed, landed, sibling))
            passed[-1].start()
        for i, (px, py) in enumerate(chips):
            from_sibling = o_ref.at[2 * px + py, its_half]
            copy(n_ici + i, from_sibling, from_sibling, sibling).wait_recv()
        for cp in first + passed:
            cp.wait_send()
        local.wait()

    return pl.pallas_call(
        body, name=name, in_specs=[ANY], out_specs=ANY, out_shape=jax.ShapeDtypeStruct((N_SHARD,) + src.shape, src.dtype),
        scratch_shapes=[pltpu.SemaphoreType.DMA((2 * n_ici,)), pltpu.SemaphoreType.DMA((2 * n_ici,)), pltpu.SemaphoreType.DMA],
    )(src)


def _swap_sibling(vs, name):
    n = len(vs)

    def body(*refs):
        v_refs, o_refs, (send_sems, recv_sems) = refs[:n], refs[n:2 * n], refs[2 * n:]
        x, y, c = lax.axis_index("x"), lax.axis_index("y"), lax.axis_index("c")
        copies = [pltpu.make_async_remote_copy(src_ref=v_refs[a], dst_ref=o_refs[a], send_sem=send_sems.at[a], recv_sem=recv_sems.at[a],
                                               device_id=(x, y, 1 - c), device_id_type=MESH) for a in range(n)]
        for cp in copies:
            cp.start()
        for cp in copies:
            cp.wait()

    return pl.pallas_call(
        body, name=name, in_specs=[ANY] * n, out_specs=[ANY] * n, out_shape=[jax.ShapeDtypeStruct(v.shape, v.dtype) for v in vs],
        scratch_shapes=[pltpu.SemaphoreType.DMA((n,)), pltpu.SemaphoreType.DMA((n,))],
    )(*vs)


def _row_tile(r, elems, step):
    ok = [t for t in range(step, r + 1, step) if r % t == 0 and t <= elems]
    return max(ok) if ok else r


def _sum_slots(a, name):
    n, r, cdim = a.shape
    tm = _row_tile(r, (4 << 20) // (cdim * 4 * (n + 1)), 16)

    def body(a_ref, o_ref):
        acc = a_ref[0].astype(F32)
        for k in range(1, n):
            acc = acc + a_ref[k].astype(F32)
        o_ref[...] = acc

    return pl.pallas_call(
        body, name=name, grid=(r // tm,), in_specs=[pl.BlockSpec((n, tm, cdim), lambda i: (0, i, 0))],
        out_specs=pl.BlockSpec((tm, cdim), lambda i: (i, 0)), out_shape=jax.ShapeDtypeStruct((r, cdim), F32),
        compiler_params=_cp("parallel"),
    )(a)


def _silu_rows(v, name):
    def body(v_ref, o_ref):
        o_ref[...] = _silu(v_ref[...])

    return pl.pallas_call(body, name=name, out_shape=jax.ShapeDtypeStruct(v.shape, F32))(v)


PACK_COLS = 1024


def _adamw(w, gs, m, v, name):
    r, cdim = w.shape
    tm = _row_tile(r, (1 << 18) // cdim, 8)
    c1 = 1.0 / (1.0 - ADAM_B1 ** ADAM_STEP)
    c2 = 1.0 / (1.0 - ADAM_B2 ** ADAM_STEP)
    ng = len(gs)

    def body(*refs):
        w_ref, g_refs, (m_ref, v_ref, g_ref, d_ref, nm_ref, nv_ref) = refs[0], refs[1:1 + ng], refs[1 + ng:]
        g = g_refs[0][...]
        for t in g_refs[1:]:
            g = g + t[...]
        mn = ADAM_B1 * m_ref[...] + (1.0 - ADAM_B1) * g
        vn = ADAM_B2 * v_ref[...] + (1.0 - ADAM_B2) * (g * g)
        g_ref[...] = g
        nm_ref[...] = mn
        nv_ref[...] = vn
        d_ref[...] = -ADAM_LR * ((mn * c1) / (jnp.sqrt(vn * c2) + ADAM_EPS) + ADAM_WD * w_ref[...])

    spec = pl.BlockSpec((tm, cdim), lambda i: (i, 0))
    return pl.pallas_call(
        body, name=name, grid=(r // tm,), in_specs=[spec] * (3 + ng), out_specs=[spec] * 4,
        out_shape=[jax.ShapeDtypeStruct(w.shape, F32)] * 4, compiler_params=_cp("parallel"),
    )(w, *gs, m, v)


def _rows(a):
    f = a.reshape(-1)
    pad = (-f.shape[0]) % PACK_COLS
    if pad:
        f = jnp.pad(f, (0, pad))
    return f.reshape(-1, PACK_COLS)


def _nrows(shape):
    return -(-int(np.prod(shape)) // PACK_COLS)


def _pack(parts, total_rows=None):
    p = jnp.concatenate([_rows(a) for a in parts], axis=0)
    if total_rows is not None and total_rows > p.shape[0]:
        p = jnp.pad(p, ((0, total_rows - p.shape[0]), (0, 0)))
    return p


def _unpack(p, shapes):
    out, r0 = [], 0
    for shp in shapes:
        n = int(np.prod(shp))
        nr = _nrows(shp)
        out.append(p[r0:r0 + nr].reshape(-1)[:n].reshape(shp))
        r0 += nr
    return out


def _unshard_cols(g):
    return jnp.concatenate([g[k] for k in range(N_SHARD)], axis=-1)


def _shard_cols(a):
    n = a.shape[-1] // N_SHARD
    return jnp.stack([a[..., k * n:(k + 1) * n] for k in range(N_SHARD)])


def _unshard_rows(g):
    return jnp.concatenate([g[k] for k in range(N_SHARD)], axis=1)


def _shard_rows(a):
    n = a.shape[1] // N_SHARD
    return jnp.stack([a[:, k * n:(k + 1) * n] for k in range(N_SHARD)])


W_NAMES = ("ada_w", "ada_b", "ln_g", "ln_b", "a_w_in", "a_w_out", "b_w_in", "b_conv_w", "b_conv_b", "b_dt_bias", "b_a_log", "b_d",
           "b_norm_w", "b_w_out")
BIG = ("a_w_in", "a_w_out", "b_w_in", "b_w_out")
SMALL = ("ada_b", "ln_g", "ln_b", "b_conv_w", "b_conv_b", "b_dt_bias", "b_a_log", "b_d", "b_norm_w")


def kernel(x, c, ada_w, ada_b, ln_g, ln_b, a_w_in, a_w_out, b_w_in, b_conv_w, b_conv_b, b_dt_bias, b_a_log, b_d, b_norm_w, b_w_out, loss_target, m_ada_w, m_ada_b, m_ln_g, m_ln_b, m_a_w_in, m_a_w_out, m_b_w_in, m_b_conv_w, m_b_conv_b, m_b_dt_bias, m_b_a_log, m_b_d, m_b_norm_w, m_b_w_out, v_ada_w, v_ada_b, v_ln_g, v_ln_b, v_a_w_in, v_a_w_out, v_b_w_in, v_b_conv_w, v_b_conv_b, v_b_dt_bias, v_b_a_log, v_b_d, v_b_norm_w, v_b_w_out):
    w = dict(ada_w=ada_w, ada_b=ada_b, ln_g=ln_g, ln_b=ln_b, a_w_in=a_w_in, a_w_out=a_w_out, b_w_in=b_w_in, b_conv_w=b_conv_w,
             b_conv_b=b_conv_b, b_dt_bias=b_dt_bias, b_a_log=b_a_log, b_d=b_d, b_norm_w=b_norm_w, b_w_out=b_w_out)
    mom = dict(ada_w=m_ada_w, ada_b=m_ada_b, ln_g=m_ln_g, ln_b=m_ln_b, a_w_in=m_a_w_in, a_w_out=m_a_w_out, b_w_in=m_b_w_in,
               b_conv_w=m_b_conv_w, b_conv_b=m_b_conv_b, b_dt_bias=m_b_dt_bias, b_a_log=m_b_a_log, b_d=m_b_d, b_norm_w=m_b_norm_w,
               b_w_out=m_b_w_out)
    var = dict(ada_w=v_ada_w, ada_b=v_ada_b, ln_g=v_ln_g, ln_b=v_ln_b, a_w_in=v_a_w_in, a_w_out=v_a_w_out, b_w_in=v_b_w_in,
               b_conv_w=v_b_conv_w, b_conv_b=v_b_conv_b, b_dt_bias=v_b_dt_bias, b_a_log=v_b_a_log, b_d=v_b_d, b_norm_w=v_b_norm_w,
               b_w_out=v_b_w_out)
    ax, ay, ac = lax.axis_index("x"), lax.axis_index("y"), lax.axis_index("c")
    me = 4 * ax + 2 * ay + ac
    shard = 2 * ax + ay
    d = x.shape[-1]
    dsh = ada_w.shape[-1]

    small_in = (c, b_conv_w, b_conv_b, b_norm_w)
    g0 = _all_gather8(_pack(small_in).reshape(-1, LANE), "gather_small_in").reshape(N_DEV, -1, PACK_COLS)
    per_dev = [_unpack(g0[k], [a.shape for a in small_in]) for k in range(N_DEV)]
    c_all = jnp.concatenate([p[0] for p in per_dev], axis=0)
    conv_w_full, conv_b_full, norm_w_full = (_unshard_cols([per_dev[2 * k][t] for k in range(N_SHARD)]) for t in (1, 2, 3))

    cond = _silu_rows(jnp.pad(c_all, ((0, 8), (0, 0))), "cond")
    bias = lax.dynamic_slice_in_dim(ada_b, shard * dsh, dsh, axis=1)
    part = jnp.stack([_mm(cond, ada_w[i], add=jnp.broadcast_to(bias[i], (16, dsh)), name=f"mod{i}")[:N_DEV] for i in range(DEPTH)])
    g1 = _all_gather8(part.reshape(-1, LANE), "gather_mod").reshape(N_DEV, DEPTH, N_DEV, dsh)
    mod_all = _unshard_cols([g1[2 * k] for k in range(N_SHARD)])
    mod = lax.dynamic_index_in_dim(mod_all, me, axis=1, keepdims=False)

    gw = _gather_shards(_pack([w[n] for n in BIG]).astype(BF16), "gather_weights")
    big_sh = [_unpack(gw[k], [w[n].shape for n in BIG]) for k in range(N_SHARD)]
    full = dict(
        ln_g=ln_g, ln_b=ln_b, b_dt_bias=b_dt_bias, b_a_log=b_a_log, b_d=b_d,
        b_conv_w=conv_w_full, b_conv_b=conv_b_full, b_norm_w=norm_w_full,
        a_w_in=_unshard_cols([s[0] for s in big_sh]), a_w_out=_unshard_rows([s[1] for s in big_sh]),
        b_w_in=_unshard_cols([s[2] for s in big_sh]), b_w_out=_unshard_rows([s[3] for s in big_sh]),
    )

    loss, grad_x, dmod, g = _local_step(x[0], loss_target[0], mod, full)

    gsh = (_shard_cols(g["a_w_in"]), _shard_rows(g["a_w_out"]), _shard_cols(g["b_w_in"]), _shard_rows(g["b_w_out"]))
    to_send = [t.reshape(N_SHARD, -1, t.shape[-1]) for t in gsh]
    arrived = _transpose_shards(to_send, "scatter_grads")
    mine = [_sum_slots(t, f"sum_{n}") for n, t in zip(BIG, arrived)]
    theirs = _swap_sibling(mine, "swap_grads")

    small_g = (dmod, g["ln_g"], g["ln_b"], g["b_dt_bias"], g["b_a_log"], g["b_d"], g["b_conv_w"], g["b_conv_b"], g["b_norm_w"],
               loss.reshape(1))
    g2 = _all_gather8(_pack(small_g).reshape(-1, LANE), "gather_small_grads")
    tot = _unpack(_sum_slots(g2, "sum_small").reshape(-1, PACK_COLS), [a.shape for a in small_g])
    g_ada_b, g_ln_g, g_ln_b, g_dt_bias, g_a_log, g_d, g_conv_w, g_conv_b, g_norm_w, loss_sum = tot
    dmod_all = g2.reshape(N_DEV, -1)[:, :dmod.size].reshape(N_DEV, DEPTH, 3 * d)
    dmod_mine = lax.dynamic_slice_in_dim(dmod_all, shard * dsh, dsh, axis=2)
    g_ada_w = jnp.stack([_mm(cond, jnp.pad(dmod_mine[:, i], ((0, 8), (0, 0))), ta=True, name=f"dada{i}") for i in range(DEPTH)])
    csh = g_conv_w.shape[-1] // N_SHARD
    nsh = g_norm_w.shape[-1] // N_SHARD
    small_grads = dict(
        ada_w=g_ada_w, ada_b=g_ada_b, ln_g=g_ln_g, ln_b=g_ln_b, b_dt_bias=g_dt_bias, b_a_log=g_a_log, b_d=g_d,
        b_conv_w=lax.dynamic_slice_in_dim(g_conv_w, shard * csh, csh, axis=2),
        b_conv_b=lax.dynamic_slice_in_dim(g_conv_b, shard * csh, csh, axis=1),
        b_norm_w=lax.dynamic_slice_in_dim(g_norm_w, shard * nsh, nsh, axis=1),
    )

    by_name = [{}, {}, {}, {}]

    def update(n, gs):
        two_d = lambda t: t.reshape(-1, t.shape[-1])
        outs = _adamw(two_d(w[n]), [two_d(t) for t in gs], two_d(mom[n]), two_d(var[n]), f"adamw_{n}")
        for t, o in zip(by_name, outs):
            t[n] = o.reshape(w[n].shape)

    for i, n in enumerate(BIG):
        update(n, [mine[i], theirs[i]])
    update("ada_w", [small_grads["ada_w"]])
    rest = SMALL
    rows = -(-sum(_nrows(w[n].shape) for n in rest) // 8) * 8
    packed = _adamw(_pack([w[n] for n in rest], rows), [_pack([small_grads[n] for n in rest], rows)],
                    _pack([mom[n] for n in rest], rows), _pack([var[n] for n in rest], rows), "adamw_small")
    for t, p in zip(by_name, packed):
        t.update(zip(rest, _unpack(p, [w[n].shape for n in rest])))
    return (loss_sum.reshape(()), grad_x[None], *[t[n] for t in by_name for n in W_NAMES])
```

```python
import jax
import jax.numpy as jnp
import numpy as np
from jax import lax
from jax.experimental import pallas as pl
from jax.experimental.pallas import tpu as pltpu

F32 = jnp.float32
BF16 = jnp.bfloat16

DEPTH = 4
A_HEADS = 16
A_HEAD_DIM = 64
A_WIDTH = A_HEADS * A_HEAD_DIM
DILATIONS = (1, 4, 16)
A_RADIUS = 64
A_QBLOCK = 128
SSM_HEADS = 32
SSM_HEAD_DIM = 64
SSM_STATE = 128
SSM_GROUPS = 4
SSM_REP = SSM_HEADS // SSM_GROUPS
SSM_CONV = 5
SSM_CHUNK = 128
DEEPNORM_ALPHA = (2 * DEPTH) ** 0.25
LN_EPS = 1e-5
RMS_EPS = 1e-5
ADAM_LR, ADAM_B1, ADAM_B2, ADAM_EPS, ADAM_WD, ADAM_STEP = 0.001, 0.9, 0.999, 1e-08, 0.01, 10
VMEM_LIMIT = 56 * 1024 * 1024
LANE = 128


def _cp(*sem):
    return pltpu.CompilerParams(dimension_semantics=sem, vmem_limit_bytes=VMEM_LIMIT)


def _tile(dim, target):
    if dim <= target:
        return dim
    t = (target // LANE) * LANE
    while dim % t:
        t -= LANE
    return t


def _sigmoid(x):
    return 1.0 / (1.0 + jnp.exp(-x))


def _silu(x):
    return x * _sigmoid(x)


def _dsilu(x):
    s = _sigmoid(x)
    return s * (1.0 + x * (1.0 - s))


def _split3(x):
    a = x.astype(BF16)
    r = x - a.astype(F32)
    b = r.astype(BF16)
    c = (r - b.astype(F32)).astype(BF16)
    return a, b, c


def _dot(a, b, ca=1, cb=0):
    return lax.dot_general(a, b, (((ca,), (cb,)), ((), ())), preferred_element_type=F32)


def _dot_exact(m01, x):
    a, b, c = _split3(x)
    return _dot(m01, a) + _dot(m01, b) + _dot(m01, c)


def _mm(a, b, *, ta=False, tb=False, add=None, out_dtype=F32, name, tm=1024, tn=1024, tk=1024):
    m, k = (a.shape[1], a.shape[0]) if ta else a.shape
    n = b.shape[0] if tb else b.shape[1]
    assert (b.shape[1] if tb else b.shape[0]) == k
    tm, tn, tk = _tile(m, tm), _tile(n, tn), _tile(k, tk)
    nk = k // tk
    has_add = add is not None

    def body(*refs):
        if has_add:
            a_ref, b_ref, c_ref, o_ref, acc = refs
        else:
            a_ref, b_ref, o_ref, acc = refs
        kk = pl.program_id(2)
        part = _dot(a_ref[...].astype(BF16), b_ref[...].astype(BF16), 0 if ta else 1, 1 if tb else 0)

        def finish(r):
            if has_add:
                r = r + c_ref[...]
            o_ref[...] = r.astype(o_ref.dtype)

        if nk == 1:
            finish(part)
            return

        @pl.when(kk == 0)
        def _():
            acc[...] = part

        @pl.when((kk > 0) & (kk < nk - 1))
        def _():
            acc[...] += part

        @pl.when(kk == nk - 1)
        def _():
            finish(acc[...] + part)

    a_spec = pl.BlockSpec((tk, tm), lambda i, j, kk: (kk, i)) if ta else pl.BlockSpec((tm, tk), lambda i, j, kk: (i, kk))
    b_spec = pl.BlockSpec((tn, tk), lambda i, j, kk: (j, kk)) if tb else pl.BlockSpec((tk, tn), lambda i, j, kk: (kk, j))
    in_specs = [a_spec, b_spec]
    args = [a, b]
    if has_add:
        in_specs.append(pl.BlockSpec((tm, tn), lambda i, j, kk: (i, j)))
        args.append(add)
    return pl.pallas_call(
        body, name=name, grid=(m // tm, n // tn, nk), in_specs=in_specs,
        out_specs=pl.BlockSpec((tm, tn), lambda i, j, kk: (i, j)),
        out_shape=jax.ShapeDtypeStruct((m, n), out_dtype),
        scratch_shapes=[pltpu.VMEM((tm, tn) if nk > 1 else (8, LANE), F32)],
        compiler_params=_cp("parallel", "parallel", "arbitrary"),
    )(*args)


ROWS = 512


def _row_spec(tm, d):
    return pl.BlockSpec((tm, d), lambda i: (i, 0))


def _vec_spec(d, rows=1):
    return pl.BlockSpec((rows, d), lambda i: (0, 0))


def _modulate(x, scale, shift, name):
    s, d = x.shape
    tm = min(ROWS, s)

    def body(x_ref, sc_ref, sh_ref, o_ref):
        o_ref[...] = (x_ref[...] * (1.0 + sc_ref[...]) + sh_ref[...]).astype(BF16)

    return pl.pallas_call(
        body, name=name, grid=(s // tm,), in_specs=[_row_spec(tm, d), _vec_spec(d), _vec_spec(d)],
        out_specs=_row_spec(tm, d), out_shape=jax.ShapeDtypeStruct((s, d), BF16), compiler_params=_cp("parallel"),
    )(x, scale, shift)


def _resid_ln(x, y, gate, g, b, name):
    s, d = x.shape
    tm = min(ROWS, s)

    def body(x_ref, y_ref, gt_ref, g_ref, b_ref, o_ref):
        u = DEEPNORM_ALPHA * x_ref[...] + gt_ref[...] * y_ref[...]
        mu = jnp.mean(u, axis=1, keepdims=True)
        uc = u - mu
        var = jnp.mean(uc * uc, axis=1, keepdims=True)
        o_ref[...] = uc * lax.rsqrt(var + LN_EPS) * g_ref[...] + b_ref[...]

    return pl.pallas_call(
        body, name=name, grid=(s // tm,),
        in_specs=[_row_spec(tm, d), _row_spec(tm, d), _vec_spec(d), _vec_spec(d), _vec_spec(d)],
        out_specs=_row_spec(tm, d), out_shape=jax.ShapeDtypeStruct((s, d), F32), compiler_params=_cp("parallel"),
    )(x, y, gate, g, b)


def _resid_ln_bwd(x, y, dxn, gate, g, name):
    s, d = x.shape
    tm = min(ROWS, s)

    def body(x_ref, y_ref, dxn_ref, gt_ref, g_ref, du_ref, dy_ref, red_ref):
        @pl.when(pl.program_id(0) == 0)
        def _():
            red_ref[...] = jnp.zeros_like(red_ref)

        yv = y_ref[...]
        u = DEEPNORM_ALPHA * x_ref[...] + gt_ref[...] * yv
        mu = jnp.mean(u, axis=1, keepdims=True)
        uc = u - mu
        var = jnp.mean(uc * uc, axis=1, keepdims=True)
        rstd = lax.rsqrt(var + LN_EPS)
        xhat = uc * rstd
        dxnv = dxn_ref[...]
        dxh = dxnv * g_ref[...]
        du = rstd * (dxh - jnp.mean(dxh, axis=1, keepdims=True) - xhat * jnp.mean(dxh * xhat, axis=1, keepdims=True))
        du_ref[...] = du
        dy_ref[...] = (du * gt_ref[...]).astype(BF16)
        red_ref[0:1, :] += jnp.sum(du * yv, axis=0, keepdims=True)
        red_ref[1:2, :] += jnp.sum(dxnv * xhat, axis=0, keepdims=True)
        red_ref[2:3, :] += jnp.sum(dxnv, axis=0, keepdims=True)

    return pl.pallas_call(
        body, name=name, grid=(s // tm,),
        in_specs=[_row_spec(tm, d), _row_spec(tm, d), _row_spec(tm, d), _vec_spec(d), _vec_spec(d)],
        out_specs=[_row_spec(tm, d), _row_spec(tm, d), _vec_spec(d, 8)],
        out_shape=[jax.ShapeDtypeStruct((s, d), F32), jax.ShapeDtypeStruct((s, d), BF16), jax.ShapeDtypeStruct((8, d), F32)],
        compiler_params=_cp("arbitrary"),
    )(x, y, dxn, gate, g)


def _modulate_bwd(du, dhs, x, scale, name):
    s, d = x.shape
    tm = min(ROWS, s)
    n = len(dhs)

    def body(*refs):
        du_ref, dh_refs, (x_ref, sc_ref, dx_ref, red_ref) = refs[0], refs[1:1 + n], refs[1 + n:]

        @pl.when(pl.program_id(0) == 0)
        def _():
            red_ref[...] = jnp.zeros_like(red_ref)

        dhv = dh_refs[0][...]
        for t in dh_refs[1:]:
            dhv = dhv + t[...]
        dx_ref[...] = DEEPNORM_ALPHA * du_ref[...] + dhv * (1.0 + sc_ref[...])
        red_ref[0:1, :] += jnp.sum(dhv * x_ref[...], axis=0, keepdims=True)
        red_ref[1:2, :] += jnp.sum(dhv, axis=0, keepdims=True)

    return pl.pallas_call(
        body, name=name, grid=(s // tm,),
        in_specs=[_row_spec(tm, d)] * (n + 2) + [_vec_spec(d)],
        out_specs=[_row_spec(tm, d), _vec_spec(d, 8)],
        out_shape=[jax.ShapeDtypeStruct((s, d), F32), jax.ShapeDtypeStruct((8, d), F32)],
        compiler_params=_cp("arbitrary"),
    )(du, *dhs, x, scale)


def _loss_grad(xf, tgt, name):
    s, d = xf.shape
    tm = min(ROWS, s)

    def body(x_ref, t_ref, dx_ref, red_ref):
        @pl.when(pl.program_id(0) == 0)
        def _():
            red_ref[...] = jnp.zeros_like(red_ref)

        e = x_ref[...] - t_ref[...]
        dx_ref[...] = e * (1.0 / d)
        red_ref[0:1, :] += jnp.sum(e * e, axis=0, keepdims=True)

    return pl.pallas_call(
        body, name=name, grid=(s // tm,), in_specs=[_row_spec(tm, d), _row_spec(tm, d)],
        out_specs=[_row_spec(tm, d), _vec_spec(d, 8)],
        out_shape=[jax.ShapeDtypeStruct((s, d), F32), jax.ShapeDtypeStruct((8, d), F32)],
        compiler_params=_cp("arbitrary"),
    )(xf, tgt)


QKV_COLS = 3 * 3 * A_WIDTH


SLOPES = tuple(float(2.0 ** (-8.0 * (h + 1.0) / A_HEADS)) for h in range(A_HEADS))
FAR = 1e30
HEAD_COLS = tuple(slice(h * A_HEAD_DIM, (h + 1) * A_HEAD_DIM) for h in range(A_HEADS))


def _band_dist(n, length, dil, span_rows):
    shape = (2 * A_QBLOCK, A_QBLOCK) if span_rows else (A_QBLOCK, 2 * A_QBLOCK)
    r = lax.broadcasted_iota(jnp.int32, shape, 0)
    c = lax.broadcasted_iota(jnp.int32, shape, 1)
    sp, ce = (r, c) if span_rows else (c, r)
    delta = sp - A_RADIUS - ce
    pos = n * A_QBLOCK - A_RADIUS + sp
    valid = (jnp.abs(delta) <= A_RADIUS) & (pos >= 0) & (pos < length)
    return jnp.where(valid, jnp.abs(delta).astype(F32) * float(dil), FAR)


def _span_specs(col, nb64):
    def mk(i):
        return pl.BlockSpec((64, A_WIDTH), lambda r, n: (r * nb64 + jnp.clip(2 * n - 1 + i, 0, nb64 - 1), col))
    return [mk(i) for i in range(4)]


def _to_residue(t, dil):
    if dil == 1:
        return t
    s, c = t.shape
    return t.reshape(s // dil, dil, c).transpose(1, 0, 2).reshape(s, c)


def _from_residue(t, dil):
    if dil == 1:
        return t
    s, c = t.shape
    return t.reshape(dil, s // dil, c).transpose(1, 0, 2).reshape(s, c)


def _cat(refs):
    return jnp.concatenate([t[...] for t in refs], axis=0)


def _head_expander():
    r = lax.broadcasted_iota(jnp.int32, (A_HEADS, A_WIDTH), 0)
    c = lax.broadcasted_iota(jnp.int32, (A_HEADS, A_WIDTH), 1)
    return ((c >= r * A_HEAD_DIM) & (c < (r + 1) * A_HEAD_DIM)).astype(BF16)


def _to_lanes(x16, e):
    a, b, c = _split3(x16)
    return _dot(a, e) + _dot(b, e) + _dot(c, e)


def _per_head_sum(x, e):
    a, b, c = _split3(x)
    return _dot(a, e, 1, 1) + _dot(b, e, 1, 1) + _dot(c, e, 1, 1)


def _pair_low_lanes():
    return lax.broadcasted_iota(jnp.int32, (A_QBLOCK, LANE), 1) < A_HEAD_DIM


def _one_head(pair, low, j):
    zero = jnp.zeros_like(pair)
    return jnp.where(low, pair, zero) if j == 0 else jnp.where(low, zero, pair)


def _attn_fwd(qkv, g, name):
    s = qkv.shape[0]
    dil = DILATIONS[g]
    length = s // dil
    nblk = length // A_QBLOCK

    def body(q_ref, k0, k1, k2, k3, v0, v1, v2, v3, o_ref, l_ref):
        dist = _band_dist(pl.program_id(1), length, dil, False)
        kk = _cat((k0, k1, k2, k3))
        vv = _cat((v0, v1, v2, v3))
        low = _pair_low_lanes()
        for hp in range(A_HEADS // 2):
            ls = slice(hp * LANE, (hp + 1) * LANE)
            qp, kp, vp = q_ref[:, ls], kk[:, ls], vv[:, ls]
            outs = []
            for j in range(2):
                h = 2 * hp + j
                sc = _dot(_one_head(qp, low, j), kp, 1, 1) * 0.125 - SLOPES[h] * dist
                m = jnp.max(sc, axis=1, keepdims=True)
                p = jnp.exp(sc - m)
                z = jnp.sum(p, axis=1, keepdims=True)
                outs.append(_dot(p.astype(BF16), vp) / z)
                l_ref[:, h:h + 1] = m + jnp.log(z)
            o_ref[:, ls] = jnp.where(low, outs[0], outs[1])

    qspec = pl.BlockSpec((A_QBLOCK, A_WIDTH), lambda r, n: (r * nblk + n, 0))
    lspec = pl.BlockSpec((A_QBLOCK, A_HEADS), lambda r, n: (r * nblk + n, 0))
    return pl.pallas_call(
        body, name=name, grid=(dil, nblk), in_specs=[qspec] + _span_specs(1, 2 * nblk) + _span_specs(2, 2 * nblk),
        out_specs=[qspec, lspec],
        out_shape=[jax.ShapeDtypeStruct((s, A_WIDTH), F32), jax.ShapeDtypeStruct((s, A_HEADS), F32)],
        compiler_params=_cp("parallel", "parallel"),
    )(*([qkv] * 9))


def _attn_merge(os_, ls_, gate, name):
    s, w = gate.shape
    tm = min(ROWS, s)

    def body(o0, o1, o2, l0, l1, l2, g_ref, y_ref, o_ref, l_ref):
        a, b, c = l0[...], l1[...], l2[...]
        m = jnp.maximum(jnp.maximum(a, b), c)
        ea, eb, ec = jnp.exp(a - m), jnp.exp(b - m), jnp.exp(c - m)
        z = ea + eb + ec
        l_ref[...] = m + jnp.log(z)
        e = _head_expander()
        o = _to_lanes(ea / z, e) * o0[...] + _to_lanes(eb / z, e) * o1[...] + _to_lanes(ec / z, e) * o2[...]
        o_ref[...] = o
        y_ref[...] = (o * _silu(g_ref[...])).astype(BF16)

    rs = _row_spec(tm, w)
    ls = _row_spec(tm, A_HEADS)
    return pl.pallas_call(
        body, name=name, grid=(s // tm,), in_specs=[rs] * 3 + [ls] * 3 + [rs], out_specs=[rs, rs, ls],
        out_shape=[jax.ShapeDtypeStruct((s, w), BF16), jax.ShapeDtypeStruct((s, w), F32), jax.ShapeDtypeStruct((s, A_HEADS), F32)],
        compiler_params=_cp("parallel"),
    )(*os_, *ls_, gate)


def _attn_gate_bwd(dyy, o, gate, name):
    s, w = gate.shape
    tm = min(ROWS, s)

    def body(dy_ref, o_ref, g_ref, do_ref, dg_ref, dl_ref):
        dyv, ov, gv = dy_ref[...], o_ref[...], g_ref[...]
        do = dyv * _silu(gv)
        do_ref[...] = do.astype(BF16)
        dg_ref[...] = (dyv * ov * _dsilu(gv)).astype(BF16)
        dl_ref[...] = _per_head_sum(do * ov, _head_expander())

    rs = _row_spec(tm, w)
    return pl.pallas_call(
        body, name=name, grid=(s // tm,), in_specs=[rs] * 3, out_specs=[rs, rs, _row_spec(tm, A_HEADS)],
        out_shape=[jax.ShapeDtypeStruct((s, w), BF16), jax.ShapeDtypeStruct((s, w), BF16), jax.ShapeDtypeStruct((s, A_HEADS), F32)],
        compiler_params=_cp("parallel"),
    )(dyy, o, gate)


def _attn_bwd(qkv, do, lse, delta, g, name):
    s = qkv.shape[0]
    dil = DILATIONS[g]
    length = s // dil
    nblk = length // A_QBLOCK

    def rows(t16):
        return jnp.pad(t16.reshape(dil, length, A_HEADS).transpose(0, 2, 1), ((0, 0), (0, 0), (A_RADIUS, A_RADIUS)))

    def body(q0, q1, q2, q3, k0, k1, k2, k3, v0, v1, v2, v3, d0, d1, d2, d3, lc_ref, ec_ref, la, lb, ea, eb, o_ref):
        dist = _band_dist(pl.program_id(1), length, dil, False)
        qq, kk, vv, dd = _cat((q0, q1, q2, q3)), _cat((k0, k1, k2, k3)), _cat((v0, v1, v2, v3)), _cat((d0, d1, d2, d3))
        lse_r = jnp.concatenate([la[...], lb[...]], axis=1)
        dlt_r = jnp.concatenate([ea[...], eb[...]], axis=1)
        low = _pair_low_lanes()
        centre = slice(A_RADIUS, A_RADIUS + A_QBLOCK)
        for hp in range(A_HEADS // 2):
            ls = slice(hp * LANE, (hp + 1) * LANE)
            qs, ks, vs, ds_ = qq[:, ls], kk[:, ls], vv[:, ls], dd[:, ls]
            qn, kn, vn, dn = qs[centre], ks[centre], vs[centre], ds_[centre]
            dq, dk, dv = [], [], []
            for j in range(2):
                h = 2 * hp + j
                bias = SLOPES[h] * dist
                p = jnp.exp(_dot(_one_head(qn, low, j), ks, 1, 1) * 0.125 - bias - lc_ref[:, h:h + 1])
                dsc = p * (_dot(_one_head(dn, low, j), vs, 1, 1) - ec_ref[:, h:h + 1])
                dq.append(_dot(dsc.astype(BF16), ks))
                pt = jnp.exp(_dot(_one_head(kn, low, j), qs, 1, 1) * 0.125 - bias - lse_r[h:h + 1, :])
                dst = pt * (_dot(_one_head(vn, low, j), ds_, 1, 1) - dlt_r[h:h + 1, :])
                dk.append(_dot(dst.astype(BF16), qs))
                dv.append(_dot(pt.astype(BF16), ds_))
            o_ref[:, ls] = (jnp.where(low, dq[0], dq[1]) * 0.125).astype(BF16)
            o_ref[:, A_WIDTH + hp * LANE:A_WIDTH + (hp + 1) * LANE] = (jnp.where(low, dk[0], dk[1]) * 0.125).astype(BF16)
            o_ref[:, 2 * A_WIDTH + hp * LANE:2 * A_WIDTH + (hp + 1) * LANE] = jnp.where(low, dv[0], dv[1]).astype(BF16)

    nb64 = 2 * nblk
    dspecs = _span_specs(0, nb64)
    cspec = pl.BlockSpec((A_QBLOCK, A_HEADS), lambda r, n: (r * nblk + n, 0))
    rspecs = [pl.BlockSpec((None, A_HEADS, A_QBLOCK), lambda r, n: (r, 0, n)), pl.BlockSpec((None, A_HEADS, A_QBLOCK), lambda r, n: (r, 0, n + 1))]
    lse_r, dlt_r = rows(lse), rows(delta)
    return pl.pallas_call(
        body, name=name, grid=(dil, nblk),
        in_specs=_span_specs(0, nb64) + _span_specs(1, nb64) + _span_specs(2, nb64) + dspecs + [cspec, cspec] + rspecs * 2,
        out_specs=pl.BlockSpec((A_QBLOCK, 3 * A_WIDTH), lambda r, n: (r * nblk + n, 0)),
        out_shape=jax.ShapeDtypeStruct((s, 3 * A_WIDTH), BF16),
        compiler_params=_cp("parallel", "parallel"),
    )(*([qkv] * 12), *([do] * 4), lse, delta, lse_r, lse_r, dlt_r, dlt_r)


def _attn_layer_fwd(h, w_qkv, w_gate, w_out, li):
    nm = lambda t: f"a{li}_{t}"
    gate = _mm(h, w_gate, name=nm("gate"))
    hs, qkvs, os_, ls_ = [], [], [], []
    for g, dil in enumerate(DILATIONS):
        hg = _to_residue(h, dil)
        qkv = _mm(hg, w_qkv[:, g * 3 * A_WIDTH:(g + 1) * 3 * A_WIDTH], out_dtype=BF16, name=nm(f"qkv{g}"))
        o, l = _attn_fwd(qkv, g, nm(f"attn{g}"))
        hs.append(hg)
        qkvs.append(qkv)
        os_.append(_from_residue(o, dil))
        ls_.append(_from_residue(l, dil))
    y, o, lse = _attn_merge(os_, ls_, gate, nm("merge"))
    out = _mm(y, w_out, name=nm("out"))
    return out, (hs, qkvs, gate, y, o, lse)


def _attn_layer_bwd(dy, h, saved, w_qkv, w_gate, w_out, li):
    nm = lambda t: f"a{li}_{t}"
    hs, qkvs, gate, y, o, lse = saved
    g_w_out = _mm(y, dy, ta=True, out_dtype=BF16, name=nm("dwout"))
    dyy = _mm(dy, w_out, tb=True, name=nm("dyy"))
    do, dgate, delta = _attn_gate_bwd(dyy, o, gate, nm("gatebwd"))
    dhs, dws = [], []
    for g, dil in enumerate(DILATIONS):
        dqkv = _attn_bwd(qkvs[g], _to_residue(do, dil), _to_residue(lse, dil), _to_residue(delta, dil), g, nm(f"attnbwd{g}"))
        wg = w_qkv[:, g * 3 * A_WIDTH:(g + 1) * 3 * A_WIDTH]
        dws.append(_mm(hs[g], dqkv, ta=True, out_dtype=BF16, name=nm(f"dwqkv{g}")))
        add = _mm(dgate, w_gate, tb=True, name=nm("dh_gate")) if g == 0 else None
        dhs.append(_from_residue(_mm(dqkv, wg, tb=True, add=add, name=nm(f"dh_qkv{g}")), dil))
    g_w_in = jnp.concatenate(dws + [_mm(h, dgate, ta=True, out_dtype=BF16, name=nm("dwgate"))], axis=1)
    return dhs, g_w_in, g_w_out


SSM_INNER = SSM_HEADS * SSM_HEAD_DIM
SSM_BC = SSM_GROUPS * SSM_STATE
SSM_CONV_DIM = SSM_INNER + 2 * SSM_BC
GW = SSM_REP * SSM_HEAD_DIM
T = SSM_CHUNK
HALO = 8


def _conv_specs(tm, tn, s):
    nb8 = s // HALO
    cur = pl.BlockSpec((tm, tn), lambda j, i: (i, j))
    prev = pl.BlockSpec((HALO, tn), lambda j, i: (jnp.maximum(i * (tm // HALO) - 1, 0), j))
    nxt = pl.BlockSpec((HALO, tn), lambda j, i: (jnp.minimum((i + 1) * (tm // HALO), nb8 - 1), j))
    return prev, cur, nxt


def _extend(prev_ref, cur_ref, nxt_ref, i, nrow):
    p = jnp.where(i == 0, 0.0, prev_ref[...])
    n = jnp.where(i == nrow - 1, 0.0, nxt_ref[...])
    return jnp.concatenate([p, cur_ref[...], n], axis=0)


def _shift_rows(ext, off, tm):
    rows = ext.shape[0]
    return pltpu.roll(ext, (-off) % rows, 0)[HALO:HALO + tm]


def _conv_fwd(xraw, w, b, name):
    s, cdim = xraw.shape
    tm, tn = min(256, s), 1024
    nrow = s // tm

    def body(p_ref, c_ref, n_ref, w_ref, b_ref, pre_ref, act_ref):
        ext = _extend(p_ref, c_ref, n_ref, pl.program_id(1), nrow)
        acc = jnp.broadcast_to(b_ref[...], (tm, tn))
        for k in range(SSM_CONV):
            acc = acc + w_ref[k:k + 1, :] * _shift_rows(ext, k - SSM_CONV // 2, tm)
        pre_ref[...] = acc
        act_ref[...] = _silu(acc)

    prev, cur, nxt = _conv_specs(tm, tn, s)
    return pl.pallas_call(
        body, name=name, grid=(cdim // tn, nrow),
        in_specs=[prev, cur, nxt, pl.BlockSpec((SSM_CONV, tn), lambda j, i: (0, j)), pl.BlockSpec((1, tn), lambda j, i: (0, j))],
        out_specs=[cur, cur], out_shape=[jax.ShapeDtypeStruct((s, cdim), F32)] * 2,
        compiler_params=_cp("parallel", "parallel"),
    )(xraw, xraw, xraw, w, b)


def _conv_bwd(dact, pre, xraw, w, name):
    s, cdim = xraw.shape
    tm, tn = min(256, s), 1024
    nrow = s // tm

    def body(dp, dc, dn, pp, pc, pn, xp, xc, xn, w_ref, dx_ref, red_ref):
        i = pl.program_id(1)

        @pl.when(i == 0)
        def _():
            red_ref[...] = jnp.zeros_like(red_ref)

        dpre = _extend(dp, dc, dn, i, nrow) * _dsilu(_extend(pp, pc, pn, i, nrow))
        xext = _extend(xp, xc, xn, i, nrow)
        dcur = dpre[HALO:HALO + tm]
        acc = jnp.zeros((tm, tn), F32)
        for k in range(SSM_CONV):
            off = k - SSM_CONV // 2
            acc = acc + w_ref[k:k + 1, :] * _shift_rows(dpre, -off, tm)
            red_ref[k:k + 1, :] += jnp.sum(dcur * _shift_rows(xext, off, tm), axis=0, keepdims=True)
        red_ref[SSM_CONV:SSM_CONV + 1, :] += jnp.sum(dcur, axis=0, keepdims=True)
        dx_ref[...] = acc.astype(BF16)

    prev, cur, nxt = _conv_specs(tm, tn, s)
    return pl.pallas_call(
        body, name=name, grid=(cdim // tn, nrow),
        in_specs=[prev, cur, nxt] * 3 + [pl.BlockSpec((SSM_CONV, tn), lambda j, i: (0, j))],
        out_specs=[cur, pl.BlockSpec((8, tn), lambda j, i: (0, j))],
        out_shape=[jax.ShapeDtypeStruct((s, cdim), BF16), jax.ShapeDtypeStruct((8, cdim), F32)],
        compiler_params=_cp("parallel", "arbitrary"),
    )(dact, dact, dact, pre, pre, pre, xraw, xraw, xraw, w)


def _tri(lower):
    r = lax.broadcasted_iota(jnp.int32, (T, T), 0)
    c = lax.broadcasted_iota(jnp.int32, (T, T), 1)
    return (r >= c) if lower else (r <= c)


def _softplus(x):
    return jnp.maximum(x, 0.0) + jnp.log(1.0 + jnp.exp(-jnp.abs(x)))


def _dt_prep(dt_raw, bias, a_log, name):
    s = dt_raw.shape[0]
    nc = s // T

    def body(r_ref, b_ref, a_ref, dt_ref, cum_ref, cumt_ref):
        dt = _softplus(r_ref[...] + b_ref[...])
        da = dt * (-jnp.exp(a_ref[...]))
        pre = _dot_exact(_tri(True).astype(BF16), da)
        suf = _dot_exact(_tri(False).astype(BF16), da)
        lane = lax.broadcasted_iota(jnp.int32, (T, LANE), 1)
        cum = jnp.where(lane < SSM_HEADS, pre, suf)
        dt_ref[...] = dt
        cum_ref[...] = cum
        cumt_ref[...] = cum.T

    blk = pl.BlockSpec((T, LANE), lambda c: (c, 0))
    vec = pl.BlockSpec((1, LANE), lambda c: (0, 0))
    return pl.pallas_call(
        body, name=name, grid=(nc,), in_specs=[blk, vec, vec],
        out_specs=[blk, blk, pl.BlockSpec((None, LANE, T), lambda c: (c, 0, 0))],
        out_shape=[jax.ShapeDtypeStruct((s, LANE), F32), jax.ShapeDtypeStruct((s, LANE), F32), jax.ShapeDtypeStruct((nc, LANE, T), F32)],
        compiler_params=_cp("parallel"),
    )(dt_raw, bias, a_log)


def _by_group(t):
    s = t.shape[0]
    return t[:, :2 * SSM_HEADS].reshape(s, 2 * SSM_GROUPS, SSM_REP).transpose(1, 0, 2)


def _from_group(tf, tb):
    s = tf.shape[1]
    t = jnp.concatenate([tf, tb], axis=0).transpose(1, 0, 2).reshape(s, 2 * SSM_HEADS)
    return jnp.pad(t, ((0, 0), (0, LANE - 2 * SSM_HEADS)))


def _decay_mats(acol, arow, rev):
    after = _tri(not rev)
    return jnp.where(after, jnp.exp(jnp.where(after, acol - arow, 0.0)), 0.0)


PAIRS = SSM_REP // 2


def _low_lanes():
    return lax.broadcasted_iota(jnp.int32, (T, LANE), 1) < SSM_HEAD_DIM


def _block_diag(v, low):
    zero = jnp.zeros_like(v)
    return jnp.concatenate([jnp.where(low, v, zero), jnp.where(low, zero, v)], axis=0)


def _scan_specs(nc, rev, ci):
    nxb = SSM_INNER // LANE
    kofs = SSM_GROUPS if rev else 0
    return [
        pl.BlockSpec((T, GW), lambda g, c: (ci(c), g)),
        pl.BlockSpec((T, LANE), lambda g, c: (ci(c), nxb + g)),
        pl.BlockSpec((T, LANE), lambda g, c: (ci(c), nxb + SSM_GROUPS + g)),
        pl.BlockSpec((None, T, SSM_REP), lambda g, c: (kofs + g, ci(c), 0)),
        pl.BlockSpec((None, T, SSM_REP), lambda g, c: (kofs + g, ci(c), 0)),
        pl.BlockSpec((None, None, SSM_REP, T), lambda g, c: (kofs + g, ci(c), 0, 0)),
    ]


def _pair_lanes(ref, p, low):
    return jnp.where(low, ref[:, 2 * p:2 * p + 1], ref[:, 2 * p + 1:2 * p + 2])


def _ssd_scan(xbc, dtk, cumk, cumtk, rev, name):
    s = xbc.shape[0]
    nc = s // T
    last = 0 if rev else T - 1
    ci = (lambda c: nc - 1 - c) if rev else (lambda c: c)

    def body(x_ref, b_ref, c_ref, dt_ref, cum_ref, cumt_ref, y_ref, st_ref, state):
        @pl.when(pl.program_id(1) == 0)
        def _():
            state[...] = jnp.zeros_like(state)

        bm = b_ref[...]
        cm = c_ref[...].astype(BF16)
        cb = _dot(cm, bm.astype(BF16), 1, 1)
        bt = bm.T.astype(BF16)
        low = _low_lanes()
        for p in range(PAIRS):
            ls = slice(p * LANE, (p + 1) * LANE)
            acum = _pair_lanes(cum_ref, p, low)
            u = x_ref[:, ls] * _pair_lanes(dt_ref, p, low)
            tot = acum[last:last + 1, :]
            m = [(cb * _decay_mats(cum_ref[:, r:r + 1], cumt_ref[r:r + 1, :], rev)).astype(BF16) for r in (2 * p, 2 * p + 1)]
            st = state[p]
            st_ref[p] = st
            yd = _dot(jnp.concatenate(m, axis=1), _block_diag(u.astype(BF16), low))
            yo = jnp.exp(acum) * _dot(cm, st.astype(BF16))
            y_ref[:, ls] = yd + yo
            state[p] = jnp.exp(tot) * st + _dot(bt, (jnp.exp(tot - acum) * u).astype(BF16))

    return pl.pallas_call(
        body, name=name, grid=(SSM_GROUPS, nc), in_specs=_scan_specs(nc, rev, ci),
        out_specs=[
            pl.BlockSpec((T, GW), lambda g, c: (ci(c), g)),
            pl.BlockSpec((None, PAIRS, SSM_STATE, LANE), lambda g, c: (ci(c), g, 0, 0)),
        ],
        out_shape=[jax.ShapeDtypeStruct((s, SSM_INNER), F32), jax.ShapeDtypeStruct((nc, SSM_HEADS // 2, SSM_STATE, LANE), F32)],
        scratch_shapes=[pltpu.VMEM((PAIRS, SSM_STATE, LANE), F32)],
        compiler_params=_cp("parallel", "arbitrary"),
    )(xbc, xbc, xbc, dtk, cumk, cumtk)


def _ssd_scan_bwd(xbc, dtk, cumk, cumtk, dy, states, dvec, prev, rev, name):
    s = xbc.shape[0]
    nc = s // T
    last = 0 if rev else T - 1
    ci = (lambda c: c) if rev else (lambda c: nc - 1 - c)
    has_prev = prev is not None

    def body(*refs):
        x_ref, b_ref, c_ref, dt_ref, cum_ref, cumt_ref, dy_ref, st_ref, dv_ref = refs[:9]
        refs = refs[9:]
        if has_prev:
            pdx, pdb, pdc = refs[:3]
            refs = refs[3:]
        dx_ref, db_ref, dc_ref, ddt_ref, dda_ref, dstate, rs_buf, in_buf, k_buf = refs

        @pl.when(pl.program_id(1) == 0)
        def _():
            dstate[...] = jnp.zeros_like(dstate)

        rs_buf[...] = jnp.zeros_like(rs_buf)
        in_buf[...] = jnp.zeros_like(in_buf)
        k_buf[...] = jnp.zeros_like(k_buf)
        bm = b_ref[...].astype(BF16)
        cm = c_ref[...].astype(BF16)
        cbt = _dot(bm, cm, 1, 1)
        cb = _dot(cm, bm, 1, 1)
        ct = c_ref[...].T.astype(BF16)
        after = _tri(not rev)
        before = _tri(rev)
        from_k = before.astype(BF16)
        ri = lax.broadcasted_iota(jnp.int32, (T, T), 0)
        cj = lax.broadcasted_iota(jnp.int32, (T, T), 1)
        strictly_before = (cj > ri) if rev else (cj < ri)
        dcb = jnp.zeros((T, T), F32)
        dc_acc = jnp.zeros((T, SSM_STATE), F32)
        db_acc = jnp.zeros((T, SSM_STATE), F32)
        low = _low_lanes()
        ri2 = lax.broadcasted_iota(jnp.int32, (LANE, LANE), 0)
        cj2 = lax.broadcasted_iota(jnp.int32, (LANE, LANE), 1)
        halves = ((ri2 < SSM_HEAD_DIM) == (cj2 == 0)) & (cj2 < 2)
        halves = halves.astype(BF16)

        def head_sums(v):
            hi = v.astype(BF16)
            lo = (v - hi.astype(F32)).astype(BF16)
            return _dot(hi, halves) + _dot(lo, halves)

        for p in range(PAIRS):
            ls = slice(p * LANE, (p + 1) * LANE)
            c2 = slice(2 * p, 2 * p + 2)
            lm, lmt = [], []
            for r in (2 * p, 2 * p + 1):
                acol = cum_ref[:, r:r + 1]
                arow = cumt_ref[r:r + 1, :]
                lm.append(jnp.where(after, jnp.exp(jnp.where(after, acol - arow, 0.0)), 0.0))
                lmt.append(jnp.where(before, jnp.exp(jnp.where(before, arow - acol, 0.0)), 0.0))
            acum = _pair_lanes(cum_ref, p, low)
            tot = acum[last:last + 1, :]
            dtl = _pair_lanes(dt_ref, p, low)
            xl = x_ref[:, ls]
            u = xl * dtl
            ub = u.astype(BF16)
            dyl = dy_ref[:, ls]
            dyb = dyl.astype(BF16)
            st = st_ref[p]
            stb = st.astype(BF16)
            dst = dstate[p]
            dstb = dst.astype(BF16)
            dec = jnp.exp(tot - acum)
            eac = jnp.exp(acum)
            etot = jnp.exp(tot)
            du_off = dec * _dot(bm, dstb)
            mt = jnp.concatenate([(cbt * lmt[0]).astype(BF16), (cbt * lmt[1]).astype(BF16)], axis=1)
            du = _dot(mt, _block_diag(dyb, low)) + du_off
            zero = jnp.zeros_like(dyb)
            gl = [_dot(jnp.where(low, dyb, zero), ub, 1, 1) * lm[0], _dot(jnp.where(low, zero, dyb), ub, 1, 1) * lm[1]]
            dcb = dcb + gl[0] + gl[1]
            dc_acc = dc_acc + _dot((eac * dyl).astype(BF16), stb, 1, 1)
            db_acc = db_acc + _dot((dec * u).astype(BF16), dstb, 1, 1)
            w = jnp.concatenate([(gl[0] * cb).astype(BF16), (gl[1] * cb).astype(BF16)], axis=1)
            crossing = _dot(from_k, w)
            for j in range(2):
                cr = jnp.where(strictly_before, crossing[:, j * T:(j + 1) * T], 0.0)
                in_buf[:, 2 * p + j:2 * p + j + 1] = jnp.sum(cr, axis=1, keepdims=True)
            y_off = eac * _dot(cm, stb)
            udu = u * du_off
            rs_buf[:, c2] = head_sums(dyl * y_off - udu)[:, 0:2]
            col = jnp.sum(dst * (etot * st) + udu, axis=0, keepdims=True)
            k_buf[0:1, c2] = head_sums(jnp.broadcast_to(col, (8, LANE)))[0:1, 0:2]
            ddt_ref[:, c2] = head_sums(du * xl)[:, 0:2]
            dx = du * dtl
            if has_prev:
                dx = dx + pdx[:, ls]
            else:
                dx = dx + dyl * dv_ref[:, ls]
            dx_ref[:, ls] = dx
            dstate[p] = etot * dst + _dot(ct, (eac * dyl).astype(BF16))
        dda = in_buf[...] + _dot_exact(from_k, rs_buf[...]) + k_buf[0:1, :]
        dda_ref[...] = dda[:, :SSM_REP]
        dcbb = dcb.astype(BF16)
        dc = dc_acc + _dot(dcbb, bm)
        db = db_acc + _dot(dcbb, cm, 0, 0)
        if has_prev:
            dc = dc + pdc[...]
            db = db + pdb[...]
        dc_ref[...] = dc
        db_ref[...] = db

    xspec = pl.BlockSpec((T, GW), lambda g, c: (ci(c), g))
    gspec = pl.BlockSpec((T, LANE), lambda g, c: (ci(c), g))
    in_specs = _scan_specs(nc, rev, ci) + [
        xspec,
        pl.BlockSpec((None, PAIRS, SSM_STATE, LANE), lambda g, c: (ci(c), g, 0, 0)),
        pl.BlockSpec((1, GW), lambda g, c: (0, g)),
    ]
    args = [xbc, xbc, xbc, dtk, cumk, cumtk, dy, states, dvec]
    if has_prev:
        in_specs += [xspec, gspec, gspec]
        args += list(prev)
    ospec8 = pl.BlockSpec((None, T, SSM_REP), lambda g, c: (g, ci(c), 0))
    return pl.pallas_call(
        body, name=name, grid=(SSM_GROUPS, nc), in_specs=in_specs,
        out_specs=[xspec, gspec, gspec, ospec8, ospec8],
        out_shape=[jax.ShapeDtypeStruct((s, SSM_INNER), F32), jax.ShapeDtypeStruct((s, SSM_BC), F32), jax.ShapeDtypeStruct((s, SSM_BC), F32),
                   jax.ShapeDtypeStruct((SSM_GROUPS, s, SSM_REP), F32), jax.ShapeDtypeStruct((SSM_GROUPS, s, SSM_REP), F32)],
        scratch_shapes=[pltpu.VMEM((PAIRS, SSM_STATE, LANE), F32), pltpu.VMEM((T, LANE), F32), pltpu.VMEM((T, LANE), F32),
                        pltpu.VMEM((8, LANE), F32)],
        compiler_params=_cp("parallel", "arbitrary"),
    )(*args)


def _ssd_post(yf, yb, xbc, z, dvec, nw, name):
    s = z.shape[0]
    tm = min(256, s)

    def body(yf_ref, yb_ref, x_ref, z_ref, dv_ref, nw_ref, o_ref):
        ys = yf_ref[...] + yb_ref[...] + dv_ref[...] * x_ref[...]
        yg = ys * _silu(z_ref[...])
        ms = jnp.mean(yg * yg, axis=1, keepdims=True)
        o_ref[...] = (yg * lax.rsqrt(ms + RMS_EPS) * nw_ref[...]).astype(BF16)

    rs = _row_spec(tm, SSM_INNER)
    vs = _vec_spec(SSM_INNER)
    return pl.pallas_call(
        body, name=name, grid=(s // tm,), in_specs=[rs, rs, rs, rs, vs, vs], out_specs=rs,
        out_shape=jax.ShapeDtypeStruct((s, SSM_INNER), BF16), compiler_params=_cp("parallel"),
    )(yf, yb, xbc, z, dvec, nw)


def _ssd_post_bwd(dyn, yf, yb, xbc, z, dvec, nw, name):
    s = z.shape[0]
    tm = min(256, s)

    def body(dyn_ref, yf_ref, yb_ref, x_ref, z_ref, dv_ref, nw_ref, dys_ref, dz_ref, red_ref):
        @pl.when(pl.program_id(0) == 0)
        def _():
            red_ref[...] = jnp.zeros_like(red_ref)

        xv, zv = x_ref[...], z_ref[...]
        ys = yf_ref[...] + yb_ref[...] + dv_ref[...] * xv
        sz = _silu(zv)
        yg = ys * sz
        rstd = lax.rsqrt(jnp.mean(yg * yg, axis=1, keepdims=True) + RMS_EPS)
        yhat = yg * rstd
        dynv = dyn_ref[...]
        dyh = dynv * nw_ref[...]
        dyg = rstd * (dyh - yhat * jnp.mean(dyh * yhat, axis=1, keepdims=True))
        dys = dyg * sz
        dys_ref[...] = dys
        dz_ref[...] = (dyg * ys * _dsilu(zv)).astype(BF16)
        red_ref[0:1, :] += jnp.sum(dynv * yhat, axis=0, keepdims=True)
        red_ref[1:2, :] += jnp.sum(dys * xv, axis=0, keepdims=True)

    rs = _row_spec(tm, SSM_INNER)
    vs = _vec_spec(SSM_INNER)
    return pl.pallas_call(
        body, name=name, grid=(s // tm,), in_specs=[rs, rs, rs, rs, rs, vs, vs],
        out_specs=[rs, rs, _vec_spec(SSM_INNER, 8)],
        out_shape=[jax.ShapeDtypeStruct((s, SSM_INNER), F32), jax.ShapeDtypeStruct((s, SSM_INNER), BF16), jax.ShapeDtypeStruct((8, SSM_INNER), F32)],
        compiler_params=_cp("arbitrary"),
    )(dyn, yf, yb, xbc, z, dvec, nw)


def _dt_bwd(dt_raw, bias, a_log, dt, ddt, dda, name):
    s = dt_raw.shape[0]
    tm = min(1024, s)

    def body(r_ref, b_ref, a_ref, dt_ref, ddt_ref, dda_ref, o_ref, red_ref):
        @pl.when(pl.program_id(0) == 0)
        def _():
            red_ref[...] = jnp.zeros_like(red_ref)

        a = -jnp.exp(a_ref[...])
        ddav = dda_ref[...]
        draw = (ddt_ref[...] + a * ddav) * _sigmoid(r_ref[...] + b_ref[...])
        o_ref[...] = draw.astype(BF16)
        red_ref[0:1, :] += jnp.sum(draw, axis=0, keepdims=True)
        red_ref[1:2, :] += a * jnp.sum(ddav * dt_ref[...], axis=0, keepdims=True)

    rs = _row_spec(tm, LANE)
    vs = _vec_spec(LANE)
    return pl.pallas_call(
        body, name=name, grid=(s // tm,), in_specs=[rs, vs, vs, rs, rs, rs], out_specs=[rs, _vec_spec(LANE, 8)],
        out_shape=[jax.ShapeDtypeStruct((s, LANE), BF16), jax.ShapeDtypeStruct((8, LANE), F32)],
        compiler_params=_cp("arbitrary"),
    )(dt_raw, bias, a_log, dt, ddt, dda)


def _pad_lanes(v):
    v = v.reshape(1, -1)
    return jnp.pad(v, ((0, 0), (0, LANE - v.shape[1])))


def _ssd_prep_weights(w_in, conv_w, conv_b, dt_bias, a_log, d_skip, norm_w, w_out):
    return dict(
        w_z=w_in[:, :SSM_INNER].astype(BF16),
        w_xbc=w_in[:, SSM_INNER:SSM_INNER + SSM_CONV_DIM].astype(BF16),
        w_dt=jnp.pad(w_in[:, SSM_INNER + SSM_CONV_DIM:], ((0, 0), (0, LANE - 2 * SSM_HEADS))).astype(BF16),
        conv_w=conv_w, conv_b=conv_b.reshape(1, -1), bias=_pad_lanes(dt_bias), a_log=_pad_lanes(a_log),
        dvec=jnp.repeat(d_skip, SSM_HEAD_DIM).reshape(1, -1), nw=norm_w.reshape(1, -1), w_out=w_out.astype(BF16),
    )


def _ssd_layer_fwd(h, w, li):
    nm = lambda t: f"b{li}_{t}"
    z = _mm(h, w["w_z"], name=nm("z"))
    xraw = _mm(h, w["w_xbc"], name=nm("xbc"))
    dt_raw = _mm(h, w["w_dt"], name=nm("dt"))
    pre, xbc = _conv_fwd(xraw, w["conv_w"], w["conv_b"], nm("conv"))
    dt, cum, cumt = _dt_prep(dt_raw, w["bias"], w["a_log"], nm("dtprep"))
    nc = cumt.shape[0]
    dtk, cumk = _by_group(dt), _by_group(cum)
    cumtk = cumt[:, :2 * SSM_HEADS].reshape(nc, 2 * SSM_GROUPS, SSM_REP, T).transpose(1, 0, 2, 3)
    yf, stf = _ssd_scan(xbc, dtk, cumk, cumtk, False, nm("scan_f"))
    yb, stb = _ssd_scan(xbc, dtk, cumk, cumtk, True, nm("scan_b"))
    yn = _ssd_post(yf, yb, xbc, z, w["dvec"], w["nw"], nm("post"))
    out = _mm(yn, w["w_out"], name=nm("out"))
    return out, (z, xraw, dt_raw, pre, xbc, dt, dtk, cumk, cumtk, yf, stf, yb, stb, yn)


def _ssd_layer_bwd(dy, h, saved, w, li):
    nm = lambda t: f"b{li}_{t}"
    z, xraw, dt_raw, pre, xbc, dt, dtk, cumk, cumtk, yf, stf, yb, stb, yn = saved
    g_w_out = _mm(yn, dy, ta=True, out_dtype=BF16, name=nm("dwout"))
    dyn = _mm(dy, w["w_out"], tb=True, name=nm("dyn"))
    dys, dz, pred = _ssd_post_bwd(dyn, yf, yb, xbc, z, w["dvec"], w["nw"], nm("postbwd"))
    dx1, db1, dc1, ddt_f, dda_f = _ssd_scan_bwd(xbc, dtk, cumk, cumtk, dys, stf, w["dvec"], None, False, nm("scanbwd_f"))
    dx, db, dc, ddt_b, dda_b = _ssd_scan_bwd(xbc, dtk, cumk, cumtk, dys, stb, w["dvec"], (dx1, db1, dc1), True, nm("scanbwd_b"))
    dact = jnp.concatenate([dx, db, dc], axis=1)
    dxraw, cred = _conv_bwd(dact, pre, xraw, w["conv_w"], nm("convbwd"))
    draw, dred = _dt_bwd(dt_raw, w["bias"], w["a_log"], dt, _from_group(ddt_f, ddt_b), _from_group(dda_f, dda_b), nm("dtbwd"))
    dh = _mm(dz, w["w_z"], tb=True, name=nm("dh_z"))
    dh = _mm(dxraw, w["w_xbc"], tb=True, add=dh, name=nm("dh_xbc"))
    dh = _mm(draw, w["w_dt"], tb=True, add=dh, name=nm("dh_dt"))
    g_w_in = jnp.concatenate([_mm(h, dz, ta=True, out_dtype=BF16, name=nm("dwz")), _mm(h, dxraw, ta=True, out_dtype=BF16, name=nm("dwxbc")),
                              _mm(h, draw, ta=True, out_dtype=BF16, name=nm("dwdt"))[:, :2 * SSM_HEADS]], axis=1)
    grads = (g_w_in, cred[:SSM_CONV], cred[SSM_CONV], dred[0, :2 * SSM_HEADS].reshape(2, SSM_HEADS),
             dred[1, :2 * SSM_HEADS].reshape(2, SSM_HEADS), pred[1].reshape(SSM_HEADS, SSM_HEAD_DIM).sum(axis=1), pred[0], g_w_out)
    return dh, grads


B_GRAD_NAMES = ("b_w_in", "b_conv_w", "b_conv_b", "b_dt_bias", "b_a_log", "b_d", "b_norm_w", "b_w_out")


def _local_step(x, tgt, mod, w):
    d = x.shape[1]
    qkv_cols = QKV_COLS
    layers = []
    for i in range(DEPTH):
        j = i // 2
        if i % 2 == 0:
            layers.append((w["a_w_in"][j][:, :qkv_cols].astype(BF16), w["a_w_in"][j][:, qkv_cols:].astype(BF16), w["a_w_out"][j].astype(BF16)))
        else:
            layers.append(_ssd_prep_weights(w["b_w_in"][j], w["b_conv_w"][j], w["b_conv_b"][j], w["b_dt_bias"][j], w["b_a_log"][j],
                                            w["b_d"][j], w["b_norm_w"][j], w["b_w_out"][j]))
    saved = []
    for i in range(DEPTH):
        shift, scale, gate = mod[i:i + 1, :d], mod[i:i + 1, d:2 * d], mod[i:i + 1, 2 * d:]
        h = _modulate(x, scale, shift, f"l{i}_mod")
        if i % 2 == 0:
            out, sv = _attn_layer_fwd(h, *layers[i], i)
        else:
            out, sv = _ssd_layer_fwd(h, layers[i], i)
        xn = _resid_ln(x, out, gate, w["ln_g"][i:i + 1], w["ln_b"][i:i + 1], f"l{i}_ln")
        saved.append((x, h, out, sv))
        x = xn
    dx, lred = _loss_grad(x, tgt, "loss")
    loss = 0.5 * jnp.sum(lred[0]) / d
    dmod, g_ln_g, g_ln_b = [None] * DEPTH, [None] * DEPTH, [None] * DEPTH
    ga_in, ga_out = [None, None], [None, None]
    gb = [None, None]
    for i in reversed(range(DEPTH)):
        j = i // 2
        xi, h, out, sv = saved[i]
        scale, gate = mod[i:i + 1, d:2 * d], mod[i:i + 1, 2 * d:]
        du, dy, red = _resid_ln_bwd(xi, out, dx, gate, w["ln_g"][i:i + 1], f"l{i}_lnbwd")
        g_ln_g[i], g_ln_b[i] = red[1], red[2]
        if i % 2 == 0:
            dhs, ga_in[j], ga_out[j] = _attn_layer_bwd(dy, h, sv, *layers[i], i)
        else:
            dh, gb[j] = _ssd_layer_bwd(dy, h, sv, layers[i], i)
            dhs = [dh]
        dx, red2 = _modulate_bwd(du, dhs, xi, scale, f"l{i}_modbwd")
        dmod[i] = jnp.concatenate([red2[1], red2[0], red[0]])
    grads = {"ln_g": jnp.stack(g_ln_g), "ln_b": jnp.stack(g_ln_b), "a_w_in": jnp.stack(ga_in), "a_w_out": jnp.stack(ga_out)}
    for k, n in enumerate(B_GRAD_NAMES):
        grads[n] = jnp.stack([gb[0][k], gb[1][k]])
    return loss, dx, jnp.stack(dmod), grads


MESH = pl.DeviceIdType.MESH
ANY = pl.BlockSpec(memory_space=pl.ANY)
N_DEV = 8
N_SHARD = 4


def _flip(v, bit):
    return 1 - v if bit else v


def _all_gather8(v, name):
    def body(v_ref, o_ref, send_sems, recv_sems, local_sem):
        x, y, c = lax.axis_index("x"), lax.axis_index("y"), lax.axis_index("c")
        me = 4 * x + 2 * y + c
        local = pltpu.make_async_copy(v_ref, o_ref.at[me], local_sem)
        local.start()
        copies = []
        for k in range(1, N_DEV):
            peer = (_flip(x, k & 4), _flip(y, k & 2), _flip(c, k & 1))
            copies.append(pltpu.make_async_remote_copy(
                src_ref=v_ref, dst_ref=o_ref.at[me], send_sem=send_sems.at[k - 1], recv_sem=recv_sems.at[k - 1],
                device_id=peer, device_id_type=MESH))
        for cp in copies:
            cp.start()
        for cp in copies:
            cp.wait()
        local.wait()

    return pl.pallas_call(
        body, name=name, in_specs=[ANY], out_specs=ANY, out_shape=jax.ShapeDtypeStruct((N_DEV,) + v.shape, v.dtype),
        scratch_shapes=[pltpu.SemaphoreType.DMA((N_DEV - 1,)), pltpu.SemaphoreType.DMA((N_DEV - 1,)), pltpu.SemaphoreType.DMA],
    )(v)


def _transpose_shards(srcs, name):
    n = len(srcs)
    n_rem = (N_SHARD - 1) * n

    def body(*refs):
        s_refs, o_refs = refs[:n], refs[n:2 * n]
        send_sems, recv_sems, local_sems = refs[2 * n:]
        x, y, c = lax.axis_index("x"), lax.axis_index("y"), lax.axis_index("c")
        m = 2 * x + y
        local = [pltpu.make_async_copy(s_refs[a].at[m], o_refs[a].at[m], local_sems.at[a]) for a in range(n)]
        remote = []
        for k in range(1, N_SHARD):
            px, py = _flip(x, k & 2), _flip(y, k & 1)
            for a in range(n):
                i = (k - 1) * n + a
                remote.append(pltpu.make_async_remote_copy(
                    src_ref=s_refs[a].at[2 * px + py], dst_ref=o_refs[a].at[m], send_sem=send_sems.at[i], recv_sem=recv_sems.at[i],
                    device_id=(px, py, c), device_id_type=MESH))
        for cp in local + remote:
            cp.start()
        for cp in remote + local:
            cp.wait()

    return pl.pallas_call(
        body, name=name, in_specs=[ANY] * n, out_specs=[ANY] * n, out_shape=[jax.ShapeDtypeStruct(s.shape, s.dtype) for s in srcs],
        scratch_shapes=[pltpu.SemaphoreType.DMA((n_rem,)), pltpu.SemaphoreType.DMA((n_rem,)), pltpu.SemaphoreType.DMA((n,))],
    )(*srcs)


def _gather_shards(src, name):
    rows = src.shape[0]
    half = rows // 2
    n_ici = N_SHARD - 1

    def body(s_ref, o_ref, send_sems, recv_sems, local_sem):
        x, y, c = lax.axis_index("x"), lax.axis_index("y"), lax.axis_index("c")
        m = 2 * x + y
        sibling = (x, y, 1 - c)
        my_half = pl.ds(pl.multiple_of(c * half, 16), half)
        its_half = pl.ds(pl.multiple_of((1 - c) * half, 16), half)
        local = pltpu.make_async_copy(s_ref, o_ref.at[m], local_sem)
        local.start()
        chips = [(_flip(x, k & 2), _flip(y, k & 1)) for k in range(1, N_SHARD)]

        def copy(sem, src_ref, dst_ref, to):
            return pltpu.make_async_remote_copy(src_ref=src_ref, dst_ref=dst_ref, send_sem=send_sems.at[sem],
                                                recv_sem=recv_sems.at[sem], device_id=to, device_id_type=MESH)

        first = [copy(i, s_ref.at[my_half], o_ref.at[m, my_half], (px, py, c)) for i, (px, py) in enumerate(chips)]
        for cp in first:
            cp.start()
        passed = []
        for i, (px, py) in enumerate(chips):
            landed = o_ref.at[2 * px + py, my_half]
            copy(i, landed, landed, (px, py, c)).wait_recv()
            passed.append(copy(n_ici + i, landed, landed, sibling))
            passed[-1].start()
        for i, (px, py) in enumerate(chips):
            from_sibling = o_ref.at[2 * px + py, its_half]
            copy(n_ici + i, from_sibling, from_sibling, sibling).wait_recv()
        for cp in first + passed:
            cp.wait_send()
        local.wait()

    return pl.pallas_call(
        body, name=name, in_specs=[ANY], out_specs=ANY, out_shape=jax.ShapeDtypeStruct((N_SHARD,) + src.shape, src.dtype),
        scratch_shapes=[pltpu.SemaphoreType.DMA((2 * n_ici,)), pltpu.SemaphoreType.DMA((2 * n_ici,)), pltpu.SemaphoreType.DMA],
    )(src)


def _swap_sibling(vs, name):
    n = len(vs)

    def body(*refs):
        v_refs, o_refs, (send_sems, recv_sems) = refs[:n], refs[n:2 * n], refs[2 * n:]
        x, y, c = lax.axis_index("x"), lax.axis_index("y"), lax.axis_index("c")
        copies = [pltpu.make_async_remote_copy(src_ref=v_refs[a], dst_ref=o_refs[a], send_sem=send_sems.at[a], recv_sem=recv_sems.at[a],
                                               device_id=(x, y, 1 - c), device_id_type=MESH) for a in range(n)]
        for cp in copies:
            cp.start()
        for cp in copies:
            cp.wait()

    return pl.pallas_call(
        body, name=name, in_specs=[ANY] * n, out_specs=[ANY] * n, out_shape=[jax.ShapeDtypeStruct(v.shape, v.dtype) for v in vs],
        scratch_shapes=[pltpu.SemaphoreType.DMA((n,)), pltpu.SemaphoreType.DMA((n,))],
    )(*vs)


def _row_tile(r, elems, step):
    ok = [t for t in range(step, r + 1, step) if r % t == 0 and t <= elems]
    return max(ok) if ok else r


def _sum_slots(a, name):
    n, r, cdim = a.shape
    tm = _row_tile(r, (4 << 20) // (cdim * 4 * (n + 1)), 16)

    def body(a_ref, o_ref):
        acc = a_ref[0].astype(F32)
        for k in range(1, n):
            acc = acc + a_ref[k].astype(F32)
        o_ref[...] = acc

    return pl.pallas_call(
        body, name=name, grid=(r // tm,), in_specs=[pl.BlockSpec((n, tm, cdim), lambda i: (0, i, 0))],
        out_specs=pl.BlockSpec((tm, cdim), lambda i: (i, 0)), out_shape=jax.ShapeDtypeStruct((r, cdim), F32),
        compiler_params=_cp("parallel"),
    )(a)


def _silu_rows(v, name):
    def body(v_ref, o_ref):
        o_ref[...] = _silu(v_ref[...])

    return pl.pallas_call(body, name=name, out_shape=jax.ShapeDtypeStruct(v.shape, F32))(v)


PACK_COLS = 1024


def _adamw(w, gs, m, v, name):
    r, cdim = w.shape
    tm = _row_tile(r, (1 << 18) // cdim, 8)
    c1 = 1.0 / (1.0 - ADAM_B1 ** ADAM_STEP)
    c2 = 1.0 / (1.0 - ADAM_B2 ** ADAM_STEP)
    ng = len(gs)

    def body(*refs):
        w_ref, g_refs, (m_ref, v_ref, g_ref, d_ref, nm_ref, nv_ref) = refs[0], refs[1:1 + ng], refs[1 + ng:]
        g = g_refs[0][...]
        for t in g_refs[1:]:
            g = g + t[...]
        mn = ADAM_B1 * m_ref[...] + (1.0 - ADAM_B1) * g
        vn = ADAM_B2 * v_ref[...] + (1.0 - ADAM_B2) * (g * g)
        g_ref[...] = g
        nm_ref[...] = mn
        nv_ref[...] = vn
        d_ref[...] = -ADAM_LR * ((mn * c1) / (jnp.sqrt(vn * c2) + ADAM_EPS) + ADAM_WD * w_ref[...])

    spec = pl.BlockSpec((tm, cdim), lambda i: (i, 0))
    return pl.pallas_call(
        body, name=name, grid=(r // tm,), in_specs=[spec] * (3 + ng), out_specs=[spec] * 4,
        out_shape=[jax.ShapeDtypeStruct(w.shape, F32)] * 4, compiler_params=_cp("parallel"),
    )(w, *gs, m, v)


def _rows(a):
    f = a.reshape(-1)
    pad = (-f.shape[0]) % PACK_COLS
    if pad:
        f = jnp.pad(f, (0, pad))
    return f.reshape(-1, PACK_COLS)


def _nrows(shape):
    return -(-int(np.prod(shape)) // PACK_COLS)


def _pack(parts, total_rows=None):
    p = jnp.concatenate([_rows(a) for a in parts], axis=0)
    if total_rows is not None and total_rows > p.shape[0]:
        p = jnp.pad(p, ((0, total_rows - p.shape[0]), (0, 0)))
    return p


def _unpack(p, shapes):
    out, r0 = [], 0
    for shp in shapes:
        n = int(np.prod(shp))
        nr = _nrows(shp)
        out.append(p[r0:r0 + nr].reshape(-1)[:n].reshape(shp))
        r0 += nr
    return out


def _unshard_cols(g):
    return jnp.concatenate([g[k] for k in range(N_SHARD)], axis=-1)


def _shard_cols(a):
    n = a.shape[-1] // N_SHARD
    return jnp.stack([a[..., k * n:(k + 1) * n] for k in range(N_SHARD)])


def _unshard_rows(g):
    return jnp.concatenate([g[k] for k in range(N_SHARD)], axis=1)


def _shard_rows(a):
    n = a.shape[1] // N_SHARD
    return jnp.stack([a[:, k * n:(k + 1) * n] for k in range(N_SHARD)])


W_NAMES = ("ada_w", "ada_b", "ln_g", "ln_b", "a_w_in", "a_w_out", "b_w_in", "b_conv_w", "b_conv_b", "b_dt_bias", "b_a_log", "b_d",
           "b_norm_w", "b_w_out")
BIG = ("a_w_in", "a_w_out", "b_w_in", "b_w_out")
SMALL = ("ada_b", "ln_g", "ln_b", "b_conv_w", "b_conv_b", "b_dt_bias", "b_a_log", "b_d", "b_norm_w")


def kernel(x, c, ada_w, ada_b, ln_g, ln_b, a_w_in, a_w_out, b_w_in, b_conv_w, b_conv_b, b_dt_bias, b_a_log, b_d, b_norm_w, b_w_out, loss_target, m_ada_w, m_ada_b, m_ln_g, m_ln_b, m_a_w_in, m_a_w_out, m_b_w_in, m_b_conv_w, m_b_conv_b, m_b_dt_bias, m_b_a_log, m_b_d, m_b_norm_w, m_b_w_out, v_ada_w, v_ada_b, v_ln_g, v_ln_b, v_a_w_in, v_a_w_out, v_b_w_in, v_b_conv_w, v_b_conv_b, v_b_dt_bias, v_b_a_log, v_b_d, v_b_norm_w, v_b_w_out):
    w = dict(ada_w=ada_w, ada_b=ada_b, ln_g=ln_g, ln_b=ln_b, a_w_in=a_w_in, a_w_out=a_w_out, b_w_in=b_w_in, b_conv_w=b_conv_w,
             b_conv_b=b_conv_b, b_dt_bias=b_dt_bias, b_a_log=b_a_log, b_d=b_d, b_norm_w=b_norm_w, b_w_out=b_w_out)
    mom = dict(ada_w=m_ada_w, ada_b=m_ada_b, ln_g=m_ln_g, ln_b=m_ln_b, a_w_in=m_a_w_in, a_w_out=m_a_w_out, b_w_in=m_b_w_in,
               b_conv_w=m_b_conv_w, b_conv_b=m_b_conv_b, b_dt_bias=m_b_dt_bias, b_a_log=m_b_a_log, b_d=m_b_d, b_norm_w=m_b_norm_w,
               b_w_out=m_b_w_out)
    var = dict(ada_w=v_ada_w, ada_b=v_ada_b, ln_g=v_ln_g, ln_b=v_ln_b, a_w_in=v_a_w_in, a_w_out=v_a_w_out, b_w_in=v_b_w_in,
               b_conv_w=v_b_conv_w, b_conv_b=v_b_conv_b, b_dt_bias=v_b_dt_bias, b_a_log=v_b_a_log, b_d=v_b_d, b_norm_w=v_b_norm_w,
               b_w_out=v_b_w_out)
    ax, ay, ac = lax.axis_index("x"), lax.axis_index("y"), lax.axis_index("c")
    me = 4 * ax + 2 * ay + ac
    shard = 2 * ax + ay
    d = x.shape[-1]
    dsh = ada_w.shape[-1]

    small_in = (c, b_conv_w, b_conv_b, b_norm_w)
    g0 = _all_gather8(_pack(small_in).reshape(-1, LANE), "gather_small_in").reshape(N_DEV, -1, PACK_COLS)
    per_dev = [_unpack(g0[k], [a.shape for a in small_in]) for k in range(N_DEV)]
    c_all = jnp.concatenate([p[0] for p in per_dev], axis=0)
    conv_w_full, conv_b_full, norm_w_full = (_unshard_cols([per_dev[2 * k][t] for k in range(N_SHARD)]) for t in (1, 2, 3))

    cond = _silu_rows(jnp.pad(c_all, ((0, 8), (0, 0))), "cond")
    bias = lax.dynamic_slice_in_dim(ada_b, shard * dsh, dsh, axis=1)
    part = jnp.stack([_mm(cond, ada_w[i], add=jnp.broadcast_to(bias[i], (16, dsh)), name=f"mod{i}")[:N_DEV] for i in range(DEPTH)])
    g1 = _all_gather8(part.reshape(-1, LANE), "gather_mod").reshape(N_DEV, DEPTH, N_DEV, dsh)
    mod_all = _unshard_cols([g1[2 * k] for k in range(N_SHARD)])
    mod = lax.dynamic_index_in_dim(mod_all, me, axis=1, keepdims=False)

    gw = _gather_shards(_pack([w[n] for n in BIG]).astype(BF16), "gather_weights")
    big_sh = [_unpack(gw[k], [w[n].shape for n in BIG]) for k in range(N_SHARD)]
    full = dict(
        ln_g=ln_g, ln_b=ln_b, b_dt_bias=b_dt_bias, b_a_log=b_a_log, b_d=b_d,
        b_conv_w=conv_w_full, b_conv_b=conv_b_full, b_norm_w=norm_w_full,
        a_w_in=_unshard_cols([s[0] for s in big_sh]), a_w_out=_unshard_rows([s[1] for s in big_sh]),
        b_w_in=_unshard_cols([s[2] for s in big_sh]), b_w_out=_unshard_rows([s[3] for s in big_sh]),
    )

    loss, grad_x, dmod, g = _local_step(x[0], loss_target[0], mod, full)

    gsh = (_shard_cols(g["a_w_in"]), _shard_rows(g["a_w_out"]), _shard_cols(g["b_w_in"]), _shard_rows(g["b_w_out"]))
    to_send = [t.reshape(N_SHARD, -1, t.shape[-1]) for t in gsh]
    arrived = _transpose_shards(to_send, "scatter_grads")
    mine = [_sum_slots(t, f"sum_{n}") for n, t in zip(BIG, arrived)]
    theirs = _swap_sibling(mine, "swap_grads")

    small_g = (dmod, g["ln_g"], g["ln_b"], g["b_dt_bias"], g["b_a_log"], g["b_d"], g["b_conv_w"], g["b_conv_b"], g["b_norm_w"],
               loss.reshape(1))
    g2 = _all_gather8(_pack(small_g).reshape(-1, LANE), "gather_small_grads")
    tot = _unpack(_sum_slots(g2, "sum_small").reshape(-1, PACK_COLS), [a.shape for a in small_g])
    g_ada_b, g_ln_g, g_ln_b, g_dt_bias, g_a_log, g_d, g_conv_w, g_conv_b, g_norm_w, loss_sum = tot
    dmod_all = g2.reshape(N_DEV, -1)[:, :dmod.size].reshape(N_DEV, DEPTH, 3 * d)
    dmod_mine = lax.dynamic_slice_in_dim(dmod_all, shard * dsh, dsh, axis=2)
    g_ada_w = jnp.stack([_mm(cond, jnp.pad(dmod_mine[:, i], ((0, 8), (0, 0))), ta=True, name=f"dada{i}") for i in range(DEPTH)])
    csh = g_conv_w.shape[-1] // N_SHARD
    nsh = g_norm_w.shape[-1] // N_SHARD
    small_grads = dict(
        ada_w=g_ada_w, ada_b=g_ada_b, ln_g=g_ln_g, ln_b=g_ln_b, b_dt_bias=g_dt_bias, b_a_log=g_a_log, b_d=g_d,
        b_conv_w=lax.dynamic_slice_in_dim(g_conv_w, shard * csh, csh, axis=2),
        b_conv_b=lax.dynamic_slice_in_dim(g_conv_b, shard * csh, csh, axis=1),
        b_norm_w=lax.dynamic_slice_in_dim(g_norm_w, shard * nsh, nsh, axis=1),
    )

    by_name = [{}, {}, {}, {}]

    def update(n, gs):
        two_d = lambda t: t.reshape(-1, t.shape[-1])
        outs = _adamw(two_d(w[n]), [two_d(t) for t in gs], two_d(mom[n]), two_d(var[n]), f"adamw_{n}")
        for t, o in zip(by_name, outs):
            t[n] = o.reshape(w[n].shape)

    for i, n in enumerate(BIG):
        update(n, [mine[i], theirs[i]])
    update("ada_w", [small_grads["ada_w"]])
    rest = SMALL
    rows = -(-sum(_nrows(w[n].shape) for n in rest) // 8) * 8
    packed = _adamw(_pack([w[n] for n in rest], rows), [_pack([small_grads[n] for n in rest], rows)],
                    _pack([mom[n] for n in rest], rows), _pack([var[n] for n in rest], rows), "adamw_small")
    for t, p in zip(by_name, packed):
        t.update(zip(rest, _unpack(p, [w[n].shape for n in rest])))
    return (loss_sum.reshape(()), grad_x[None], *[t[n] for t in by_name for n in W_NAMES])
```

```python
import jax
import jax.numpy as jnp
import numpy as np
from jax import lax
from jax.experimental import pallas as pl
from jax.experimental.pallas import tpu as pltpu

F32 = jnp.float32
BF16 = jnp.bfloat16

DEPTH = 4
A_HEADS = 16
A_HEAD_DIM = 64
A_WIDTH = A_HEADS * A_HEAD_DIM
DILATIONS = (1, 4, 16)
A_RADIUS = 64
A_QBLOCK = 128
SSM_HEADS = 32
SSM_HEAD_DIM = 64
SSM_STATE = 128
SSM_GROUPS = 4
SSM_REP = SSM_HEADS // SSM_GROUPS
SSM_CONV = 5
SSM_CHUNK = 128
DEEPNORM_ALPHA = (2 * DEPTH) ** 0.25
LN_EPS = 1e-5
RMS_EPS = 1e-5
ADAM_LR, ADAM_B1, ADAM_B2, ADAM_EPS, ADAM_WD, ADAM_STEP = 0.001, 0.9, 0.999, 1e-08, 0.01, 10
VMEM_LIMIT = 56 * 1024 * 1024
LANE = 128


def _cp(*sem):
    return pltpu.CompilerParams(dimension_semantics=sem, vmem_limit_bytes=VMEM_LIMIT)


def _tile(dim, target):
    if dim <= target:
        return dim
    t = (target // LANE) * LANE
    while dim % t:
        t -= LANE
    return t


def _sigmoid(x):
    return 1.0 / (1.0 + jnp.exp(-x))


def _silu(x):
    return x * _sigmoid(x)


def _dsilu(x):
    s = _sigmoid(x)
    return s * (1.0 + x * (1.0 - s))


def _split3(x):
    a = x.astype(BF16)
    r = x - a.astype(F32)
    b = r.astype(BF16)
    c = (r - b.astype(F32)).astype(BF16)
    return a, b, c


def _dot(a, b, ca=1, cb=0):
    return lax.dot_general(a, b, (((ca,), (cb,)), ((), ())), preferred_element_type=F32)


def _dot_exact(m01, x):
    a, b, c = _split3(x)
    return _dot(m01, a) + _dot(m01, b) + _dot(m01, c)


def _mm(a, b, *, ta=False, tb=False, add=None, out_dtype=F32, name, tm=1024, tn=1024, tk=1024):
    m, k = (a.shape[1], a.shape[0]) if ta else a.shape
    n = b.shape[0] if tb else b.shape[1]
    assert (b.shape[1] if tb else b.shape[0]) == k
    tm, tn, tk = _tile(m, tm), _tile(n, tn), _tile(k, tk)
    nk = k // tk
    has_add = add is not None

    def body(*refs):
        if has_add:
            a_ref, b_ref, c_ref, o_ref, acc = refs
        else:
            a_ref, b_ref, o_ref, acc = refs
        kk = pl.program_id(2)
        part = _dot(a_ref[...].astype(BF16), b_ref[...].astype(BF16), 0 if ta else 1, 1 if tb else 0)

        def finish(r):
            if has_add:
                r = r + c_ref[...]
            o_ref[...] = r.astype(o_ref.dtype)

        if nk == 1:
            finish(part)
            return

        @pl.when(kk == 0)
        def _():
            acc[...] = part

        @pl.when((kk > 0) & (kk < nk - 1))
        def _():
            acc[...] += part

        @pl.when(kk == nk - 1)
        def _():
            finish(acc[...] + part)

    a_spec = pl.BlockSpec((tk, tm), lambda i, j, kk: (kk, i)) if ta else pl.BlockSpec((tm, tk), lambda i, j, kk: (i, kk))
    b_spec = pl.BlockSpec((tn, tk), lambda i, j, kk: (j, kk)) if tb else pl.BlockSpec((tk, tn), lambda i, j, kk: (kk, j))
    in_specs = [a_spec, b_spec]
    args = [a, b]
    if has_add:
        in_specs.append(pl.BlockSpec((tm, tn), lambda i, j, kk: (i, j)))
        args.append(add)
    return pl.pallas_call(
        body, name=name, grid=(m // tm, n // tn, nk), in_specs=in_specs,
        out_specs=pl.BlockSpec((tm, tn), lambda i, j, kk: (i, j)),
        out_shape=jax.ShapeDtypeStruct((m, n), out_dtype),
        scratch_shapes=[pltpu.VMEM((tm, tn) if nk > 1 else (8, LANE), F32)],
        compiler_params=_cp("parallel", "parallel", "arbitrary"),
    )(*args)


ROWS = 512


def _row_spec(tm, d):
    return pl.BlockSpec((tm, d), lambda i: (i, 0))


def _vec_spec(d, rows=1):
    return pl.BlockSpec((rows, d), lambda i: (0, 0))


def _modulate(x, scale, shift, name):
    s, d = x.shape
    tm = min(ROWS, s)

    def body(x_ref, sc_ref, sh_ref, o_ref):
        o_ref[...] = (x_ref[...] * (1.0 + sc_ref[...]) + sh_ref[...]).astype(BF16)

    return pl.pallas_call(
        body, name=name, grid=(s // tm,), in_specs=[_row_spec(tm, d), _vec_spec(d), _vec_spec(d)],
        out_specs=_row_spec(tm, d), out_shape=jax.ShapeDtypeStruct((s, d), BF16), compiler_params=_cp("parallel"),
    )(x, scale, shift)


def _resid_ln(x, y, gate, g, b, name):
    s, d = x.shape
    tm = min(ROWS, s)

    def body(x_ref, y_ref, gt_ref, g_ref, b_ref, o_ref):
        u = DEEPNORM_ALPHA * x_ref[...] + gt_ref[...] * y_ref[...]
        mu = jnp.mean(u, axis=1, keepdims=True)
        uc = u - mu
        var = jnp.mean(uc * uc, axis=1, keepdims=True)
        o_ref[...] = uc * lax.rsqrt(var + LN_EPS) * g_ref[...] + b_ref[...]

    return pl.pallas_call(
        body, name=name, grid=(s // tm,),
        in_specs=[_row_spec(tm, d), _row_spec(tm, d), _vec_spec(d), _vec_spec(d), _vec_spec(d)],
        out_specs=_row_spec(tm, d), out_shape=jax.ShapeDtypeStruct((s, d), F32), compiler_params=_cp("parallel"),
    )(x, y, gate, g, b)


def _resid_ln_bwd(x, y, dxn, gate, g, name):
    s, d = x.shape
    tm = min(ROWS, s)

    def body(x_ref, y_ref, dxn_ref, gt_ref, g_ref, du_ref, dy_ref, red_ref):
        @pl.when(pl.program_id(0) == 0)
        def _():
            red_ref[...] = jnp.zeros_like(red_ref)

        yv = y_ref[...]
        u = DEEPNORM_ALPHA * x_ref[...] + gt_ref[...] * yv
        mu = jnp.mean(u, axis=1, keepdims=True)
        uc = u - mu
        var = jnp.mean(uc * uc, axis=1, keepdims=True)
        rstd = lax.rsqrt(var + LN_EPS)
        xhat = uc * rstd
        dxnv = dxn_ref[...]
        dxh = dxnv * g_ref[...]
        du = rstd * (dxh - jnp.mean(dxh, axis=1, keepdims=True) - xhat * jnp.mean(dxh * xhat, axis=1, keepdims=True))
        du_ref[...] = du
        dy_ref[...] = (du * gt_ref[...]).astype(BF16)
        red_ref[0:1, :] += jnp.sum(du * yv, axis=0, keepdims=True)
        red_ref[1:2, :] += jnp.sum(dxnv * xhat, axis=0, keepdims=True)
        red_ref[2:3, :] += jnp.sum(dxnv, axis=0, keepdims=True)

    return pl.pallas_call(
        body, name=name, grid=(s // tm,),
        in_specs=[_row_spec(tm, d), _row_spec(tm, d), _row_spec(tm, d), _vec_spec(d), _vec_spec(d)],
        out_specs=[_row_spec(tm, d), _row_spec(tm, d), _vec_spec(d, 8)],
        out_shape=[jax.ShapeDtypeStruct((s, d), F32), jax.ShapeDtypeStruct((s, d), BF16), jax.ShapeDtypeStruct((8, d), F32)],
        compiler_params=_cp("arbitrary"),
    )(x, y, dxn, gate, g)


def _modulate_bwd(du, dhs, x, scale, name):
    s, d = x.shape
    tm = min(ROWS, s)
    n = len(dhs)

    def body(*refs):
        du_ref, dh_refs, (x_ref, sc_ref, dx_ref, red_ref) = refs[0], refs[1:1 + n], refs[1 + n:]

        @pl.when(pl.program_id(0) == 0)
        def _():
            red_ref[...] = jnp.zeros_like(red_ref)

        dhv = dh_refs[0][...]
        for t in dh_refs[1:]:
            dhv = dhv + t[...]
        dx_ref[...] = DEEPNORM_ALPHA * du_ref[...] + dhv * (1.0 + sc_ref[...])
        red_ref[0:1, :] += jnp.sum(dhv * x_ref[...], axis=0, keepdims=True)
        red_ref[1:2, :] += jnp.sum(dhv, axis=0, keepdims=True)

    return pl.pallas_call(
        body, name=name, grid=(s // tm,),
        in_specs=[_row_spec(tm, d)] * (n + 2) + [_vec_spec(d)],
        out_specs=[_row_spec(tm, d), _vec_spec(d, 8)],
        out_shape=[jax.ShapeDtypeStruct((s, d), F32), jax.ShapeDtypeStruct((8, d), F32)],
        compiler_params=_cp("arbitrary"),
    )(du, *dhs, x, scale)


def _loss_grad(xf, tgt, name):
    s, d = xf.shape
    tm = min(ROWS, s)

    def body(x_ref, t_ref, dx_ref, red_ref):
        @pl.when(pl.program_id(0) == 0)
        def _():
            red_ref[...] = jnp.zeros_like(red_ref)

        e = x_ref[...] - t_ref[...]
        dx_ref[...] = e * (1.0 / d)
        red_ref[0:1, :] += jnp.sum(e * e, axis=0, keepdims=True)

    return pl.pallas_call(
        body, name=name, grid=(s // tm,), in_specs=[_row_spec(tm, d), _row_spec(tm, d)],
        out_specs=[_row_spec(tm, d), _vec_spec(d, 8)],
        out_shape=[jax.ShapeDtypeStruct((s, d), F32), jax.ShapeDtypeStruct((8, d), F32)],
        compiler_params=_cp("arbitrary"),
    )(xf, tgt)


QKV_COLS = 3 * 3 * A_WIDTH


SLOPES = tuple(float(2.0 ** (-8.0 * (h + 1.0) / A_HEADS)) for h in range(A_HEADS))
FAR = 1e30
HEAD_COLS = tuple(slice(h * A_HEAD_DIM, (h + 1) * A_HEAD_DIM) for h in range(A_HEADS))


def _band_dist(n, length, dil, span_rows):
    shape = (2 * A_QBLOCK, A_QBLOCK) if span_rows else (A_QBLOCK, 2 * A_QBLOCK)
    r = lax.broadcasted_iota(jnp.int32, shape, 0)
    c = lax.broadcasted_iota(jnp.int32, shape, 1)
    sp, ce = (r, c) if span_rows else (c, r)
    delta = sp - A_RADIUS - ce
    pos = n * A_QBLOCK - A_RADIUS + sp
    valid = (jnp.abs(delta) <= A_RADIUS) & (pos >= 0) & (pos < length)
    return jnp.where(valid, jnp.abs(delta).astype(F32) * float(dil), FAR)


def _span_specs(col, nb64):
    def mk(i):
        return pl.BlockSpec((64, A_WIDTH), lambda r, n: (r * nb64 + jnp.clip(2 * n - 1 + i, 0, nb64 - 1), col))
    return [mk(i) for i in range(4)]


def _to_residue(t, dil):
    if dil == 1:
        return t
    s, c = t.shape
    return t.reshape(s // dil, dil, c).transpose(1, 0, 2).reshape(s, c)


def _from_residue(t, dil):
    if dil == 1:
        return t
    s, c = t.shape
    return t.reshape(dil, s // dil, c).transpose(1, 0, 2).reshape(s, c)


def _cat(refs):
    return jnp.concatenate([t[...] for t in refs], axis=0)


def _head_expander():
    r = lax.broadcasted_iota(jnp.int32, (A_HEADS, A_WIDTH), 0)
    c = lax.broadcasted_iota(jnp.int32, (A_HEADS, A_WIDTH), 1)
    return ((c >= r * A_HEAD_DIM) & (c < (r + 1) * A_HEAD_DIM)).astype(BF16)


def _to_lanes(x16, e):
    a, b, c = _split3(x16)
    return _dot(a, e) + _dot(b, e) + _dot(c, e)


def _per_head_sum(x, e):
    a, b, c = _split3(x)
    return _dot(a, e, 1, 1) + _dot(b, e, 1, 1) + _dot(c, e, 1, 1)


SUB = 128


def _pair_low_lanes():
    return lax.broadcasted_iota(jnp.int32, (A_QBLOCK, LANE), 1) < A_HEAD_DIM


def _one_head(pair, low, j):
    zero = jnp.zeros_like(pair)
    return jnp.where(low, pair, zero) if j == 0 else jnp.where(low, zero, pair)


def _attn_fwd(qkv, g, name):
    s = qkv.shape[0]
    dil = DILATIONS[g]
    length = s // dil
    nblk = length // A_QBLOCK

    def body(q_ref, k0, k1, k2, k3, v0, v1, v2, v3, o_ref, l_ref):
        dist = _band_dist(pl.program_id(1), length, dil, False)
        kk = _cat((k0, k1, k2, k3))
        vv = _cat((v0, v1, v2, v3))
        low = _pair_low_lanes()
        for hp in range(A_HEADS // 2):
            ls = slice(hp * LANE, (hp + 1) * LANE)
            qp, kp, vp = q_ref[:, ls], kk[:, ls], vv[:, ls]
            outs = []
            for j in range(2):
                h = 2 * hp + j
                sc = _dot(_one_head(qp, low, j), kp, 1, 1) * 0.125 - SLOPES[h] * dist
                m = jnp.max(sc, axis=1, keepdims=True)
                p = jnp.exp(sc - m)
                z = jnp.sum(p, axis=1, keepdims=True)
                outs.append(_dot(p.astype(BF16), vp) / z)
                l_ref[:, h:h + 1] = m + jnp.log(z)
            o_ref[:, ls] = jnp.where(low, outs[0], outs[1])

    qspec = pl.BlockSpec((A_QBLOCK, A_WIDTH), lambda r, n: (r * nblk + n, 0))
    lspec = pl.BlockSpec((A_QBLOCK, A_HEADS), lambda r, n: (r * nblk + n, 0))
    return pl.pallas_call(
        body, name=name, grid=(dil, nblk), in_specs=[qspec] + _span_specs(1, 2 * nblk) + _span_specs(2, 2 * nblk),
        out_specs=[qspec, lspec],
        out_shape=[jax.ShapeDtypeStruct((s, A_WIDTH), F32), jax.ShapeDtypeStruct((s, A_HEADS), F32)],
        compiler_params=_cp("parallel", "parallel"),
    )(*([qkv] * 9))


def _attn_merge(os_, ls_, gate, name):
    s, w = gate.shape
    tm = min(ROWS, s)

    def body(o0, o1, o2, l0, l1, l2, g_ref, y_ref, o_ref, l_ref):
        a, b, c = l0[...], l1[...], l2[...]
        m = jnp.maximum(jnp.maximum(a, b), c)
        ea, eb, ec = jnp.exp(a - m), jnp.exp(b - m), jnp.exp(c - m)
        z = ea + eb + ec
        l_ref[...] = m + jnp.log(z)
        e = _head_expander()
        o = _to_lanes(ea / z, e) * o0[...] + _to_lanes(eb / z, e) * o1[...] + _to_lanes(ec / z, e) * o2[...]
        o_ref[...] = o
        y_ref[...] = (o * _silu(g_ref[...])).astype(BF16)

    rs = _row_spec(tm, w)
    ls = _row_spec(tm, A_HEADS)
    return pl.pallas_call(
        body, name=name, grid=(s // tm,), in_specs=[rs] * 3 + [ls] * 3 + [rs], out_specs=[rs, rs, ls],
        out_shape=[jax.ShapeDtypeStruct((s, w), BF16), jax.ShapeDtypeStruct((s, w), F32), jax.ShapeDtypeStruct((s, A_HEADS), F32)],
        compiler_params=_cp("parallel"),
    )(*os_, *ls_, gate)


def _attn_gate_bwd(dyy, o, gate, name):
    s, w = gate.shape
    tm = min(ROWS, s)

    def body(dy_ref, o_ref, g_ref, do_ref, dg_ref, dl_ref):
        dyv, ov, gv = dy_ref[...], o_ref[...], g_ref[...]
        do = dyv * _silu(gv)
        do_ref[...] = do.astype(BF16)
        dg_ref[...] = (dyv * ov * _dsilu(gv)).astype(BF16)
        dl_ref[...] = _per_head_sum(do * ov, _head_expander())

    rs = _row_spec(tm, w)
    return pl.pallas_call(
        body, name=name, grid=(s // tm,), in_specs=[rs] * 3, out_specs=[rs, rs, _row_spec(tm, A_HEADS)],
        out_shape=[jax.ShapeDtypeStruct((s, w), BF16), jax.ShapeDtypeStruct((s, w), BF16), jax.ShapeDtypeStruct((s, A_HEADS), F32)],
        compiler_params=_cp("parallel"),
    )(dyy, o, gate)


def _attn_bwd(qkv, do, lse, delta, g, name):
    s = qkv.shape[0]
    dil = DILATIONS[g]
    length = s // dil
    nblk = length // A_QBLOCK

    def rows(t16):
        return jnp.pad(t16.reshape(dil, length, A_HEADS).transpose(0, 2, 1), ((0, 0), (0, 0), (A_RADIUS, A_RADIUS)))

    def body(q0, q1, q2, q3, k0, k1, k2, k3, v0, v1, v2, v3, d0, d1, d2, d3, lc_ref, ec_ref, la, lb, ea, eb, o_ref):
        dist = _band_dist(pl.program_id(1), length, dil, False)
        qq, kk, vv, dd = _cat((q0, q1, q2, q3)), _cat((k0, k1, k2, k3)), _cat((v0, v1, v2, v3)), _cat((d0, d1, d2, d3))
        lse_r = jnp.concatenate([la[...], lb[...]], axis=1)
        dlt_r = jnp.concatenate([ea[...], eb[...]], axis=1)
        low = _pair_low_lanes()
        centre = slice(A_RADIUS, A_RADIUS + A_QBLOCK)
        for hp in range(A_HEADS // 2):
            ls = slice(hp * LANE, (hp + 1) * LANE)
            qs, ks, vs, ds_ = qq[:, ls], kk[:, ls], vv[:, ls], dd[:, ls]
            qn, kn, vn, dn = qs[centre], ks[centre], vs[centre], ds_[centre]
            for part in range(A_QBLOCK // SUB):
                rs = slice(part * SUB, (part + 1) * SUB)
                lo = low[rs]
                dq, dk, dv = [], [], []
                for j in range(2):
                    h = 2 * hp + j
                    bias = SLOPES[h] * dist[rs]
                    p = jnp.exp(_dot(_one_head(qn[rs], lo, j), ks, 1, 1) * 0.125 - bias - lc_ref[rs, h:h + 1])
                    dsc = p * (_dot(_one_head(dn[rs], lo, j), vs, 1, 1) - ec_ref[rs, h:h + 1])
                    dq.append(_dot(dsc.astype(BF16), ks))
                    pt = jnp.exp(_dot(_one_head(kn[rs], lo, j), qs, 1, 1) * 0.125 - bias - lse_r[h:h + 1, :])
                    dst = pt * (_dot(_one_head(vn[rs], lo, j), ds_, 1, 1) - dlt_r[h:h + 1, :])
                    dk.append(_dot(dst.astype(BF16), qs))
                    dv.append(_dot(pt.astype(BF16), ds_))
                o_ref[rs, ls] = (jnp.where(lo, dq[0], dq[1]) * 0.125).astype(BF16)
                o_ref[rs, A_WIDTH + hp * LANE:A_WIDTH + (hp + 1) * LANE] = (jnp.where(lo, dk[0], dk[1]) * 0.125).astype(BF16)
                o_ref[rs, 2 * A_WIDTH + hp * LANE:2 * A_WIDTH + (hp + 1) * LANE] = jnp.where(lo, dv[0], dv[1]).astype(BF16)

    nb64 = 2 * nblk
    dspecs = _span_specs(0, nb64)
    cspec = pl.BlockSpec((A_QBLOCK, A_HEADS), lambda r, n: (r * nblk + n, 0))
    rspecs = [pl.BlockSpec((None, A_HEADS, A_QBLOCK), lambda r, n: (r, 0, n)), pl.BlockSpec((None, A_HEADS, A_QBLOCK), lambda r, n: (r, 0, n + 1))]
    lse_r, dlt_r = rows(lse), rows(delta)
    return pl.pallas_call(
        body, name=name, grid=(dil, nblk),
        in_specs=_span_specs(0, nb64) + _span_specs(1, nb64) + _span_specs(2, nb64) + dspecs + [cspec, cspec] + rspecs * 2,
        out_specs=pl.BlockSpec((A_QBLOCK, 3 * A_WIDTH), lambda r, n: (r * nblk + n, 0)),
        out_shape=jax.ShapeDtypeStruct((s, 3 * A_WIDTH), BF16),
        compiler_params=_cp("parallel", "parallel"),
    )(*([qkv] * 12), *([do] * 4), lse, delta, lse_r, lse_r, dlt_r, dlt_r)


def _attn_layer_fwd(h, w_qkv, w_gate, w_out, li):
    nm = lambda t: f"a{li}_{t}"
    gate = _mm(h, w_gate, name=nm("gate"))
    hs, qkvs, os_, ls_ = [], [], [], []
    for g, dil in enumerate(DILATIONS):
        hg = _to_residue(h, dil)
        qkv = _mm(hg, w_qkv[:, g * 3 * A_WIDTH:(g + 1) * 3 * A_WIDTH], out_dtype=BF16, name=nm(f"qkv{g}"))
        o, l = _attn_fwd(qkv, g, nm(f"attn{g}"))
        hs.append(hg)
        qkvs.append(qkv)
        os_.append(_from_residue(o, dil))
        ls_.append(_from_residue(l, dil))
    y, o, lse = _attn_merge(os_, ls_, gate, nm("merge"))
    out = _mm(y, w_out, name=nm("out"))
    return out, (hs, qkvs, gate, y, o, lse)


def _attn_layer_bwd(dy, h, saved, w_qkv, w_gate, w_out, li):
    nm = lambda t: f"a{li}_{t}"
    hs, qkvs, gate, y, o, lse = saved
    g_w_out = _mm(y, dy, ta=True, out_dtype=BF16, name=nm("dwout"))
    dyy = _mm(dy, w_out, tb=True, name=nm("dyy"))
    do, dgate, delta = _attn_gate_bwd(dyy, o, gate, nm("gatebwd"))
    dhs, dws = [], []
    for g, dil in enumerate(DILATIONS):
        dqkv = _attn_bwd(qkvs[g], _to_residue(do, dil), _to_residue(lse, dil), _to_residue(delta, dil), g, nm(f"attnbwd{g}"))
        wg = w_qkv[:, g * 3 * A_WIDTH:(g + 1) * 3 * A_WIDTH]
        dws.append(_mm(hs[g], dqkv, ta=True, out_dtype=BF16, name=nm(f"dwqkv{g}")))
        add = _mm(dgate, w_gate, tb=True, name=nm("dh_gate")) if g == 0 else None
        dhs.append(_from_residue(_mm(dqkv, wg, tb=True, add=add, name=nm(f"dh_qkv{g}")), dil))
    g_w_in = jnp.concatenate(dws + [_mm(h, dgate, ta=True, out_dtype=BF16, name=nm("dwgate"))], axis=1)
    return dhs, g_w_in, g_w_out


SSM_INNER = SSM_HEADS * SSM_HEAD_DIM
SSM_BC = SSM_GROUPS * SSM_STATE
SSM_CONV_DIM = SSM_INNER + 2 * SSM_BC
GW = SSM_REP * SSM_HEAD_DIM
T = SSM_CHUNK
HALO = 8


def _conv_specs(tm, tn, s, col=lambda j: j):
    nb8 = s // HALO
    cur = pl.BlockSpec((tm, tn), lambda j, i: (i, col(j)))
    prev = pl.BlockSpec((HALO, tn), lambda j, i: (jnp.maximum(i * (tm // HALO) - 1, 0), col(j)))
    nxt = pl.BlockSpec((HALO, tn), lambda j, i: (jnp.minimum((i + 1) * (tm // HALO), nb8 - 1), col(j)))
    return [prev, cur, nxt]


def _extend(prev_ref, cur_ref, nxt_ref, i, nrow):
    p = jnp.where(i == 0, 0.0, prev_ref[...])
    n = jnp.where(i == nrow - 1, 0.0, nxt_ref[...])
    return jnp.concatenate([p, cur_ref[...], n], axis=0)


def _shift_rows(ext, off, tm):
    rows = ext.shape[0]
    return pltpu.roll(ext, (-off) % rows, 0)[HALO:HALO + tm]


def _conv_fwd(xraw, w, b, name):
    s, cdim = xraw.shape
    tm, tn = min(256, s), 1024
    nrow = s // tm

    def body(p_ref, c_ref, n_ref, w_ref, b_ref, pre_ref, act_ref):
        ext = _extend(p_ref, c_ref, n_ref, pl.program_id(1), nrow)
        acc = jnp.broadcast_to(b_ref[...], (tm, tn))
        for k in range(SSM_CONV):
            acc = acc + w_ref[k:k + 1, :] * _shift_rows(ext, k - SSM_CONV // 2, tm)
        pre_ref[...] = acc
        act_ref[...] = _silu(acc)

    prev, cur, nxt = _conv_specs(tm, tn, s)
    return pl.pallas_call(
        body, name=name, grid=(cdim // tn, nrow),
        in_specs=[prev, cur, nxt, pl.BlockSpec((SSM_CONV, tn), lambda j, i: (0, j)), pl.BlockSpec((1, tn), lambda j, i: (0, j))],
        out_specs=[cur, cur], out_shape=[jax.ShapeDtypeStruct((s, cdim), F32)] * 2,
        compiler_params=_cp("parallel", "parallel"),
    )(xraw, xraw, xraw, w, b)


def _conv_bwd(dx, db, dc, pre, xraw, w, name):
    s, cdim = xraw.shape
    tm, tn = min(256, s), 1024
    nrow = s // tm
    nx = dx.shape[1] // tn

    def body(xp, xc, xn, bp, bc, bn, cp, cc, cn, pp, pc, pn, x_ref, w_ref, o_ref, red_ref):
        j, i = pl.program_id(0), pl.program_id(1)

        @pl.when(i == 0)
        def _():
            red_ref[...] = jnp.zeros_like(red_ref)

        bcext = jnp.concatenate([_extend(bp, bc, bn, i, nrow), _extend(cp, cc, cn, i, nrow)], axis=1)
        dact = jnp.where(j < nx, _extend(xp, xc, xn, i, nrow), bcext)
        dpre = dact * _dsilu(_extend(pp, pc, pn, i, nrow))
        xv = x_ref[...]
        acc = jnp.zeros((tm, tn), F32)
        for k in range(SSM_CONV):
            sk = _shift_rows(dpre, SSM_CONV // 2 - k, tm)
            acc = acc + w_ref[k:k + 1, :] * sk
            red_ref[k:k + 1, :] += jnp.sum(sk * xv, axis=0, keepdims=True)
        red_ref[SSM_CONV:SSM_CONV + 1, :] += jnp.sum(dpre[HALO:HALO + tm], axis=0, keepdims=True)
        o_ref[...] = acc.astype(BF16)

    half = tn // 2
    cur = pl.BlockSpec((tm, tn), lambda j, i: (i, j))
    return pl.pallas_call(
        body, name=name, grid=(cdim // tn, nrow),
        in_specs=_conv_specs(tm, tn, s, lambda j: jnp.minimum(j, nx - 1)) + _conv_specs(tm, half, s, lambda j: 0) * 2
        + _conv_specs(tm, tn, s) + [cur, pl.BlockSpec((SSM_CONV, tn), lambda j, i: (0, j))],
        out_specs=[cur, pl.BlockSpec((8, tn), lambda j, i: (0, j))],
        out_shape=[jax.ShapeDtypeStruct((s, cdim), BF16), jax.ShapeDtypeStruct((8, cdim), F32)],
        compiler_params=_cp("parallel", "arbitrary"),
    )(dx, dx, dx, db, db, db, dc, dc, dc, pre, pre, pre, xraw, w)


def _tri(lower):
    r = lax.broadcasted_iota(jnp.int32, (T, T), 0)
    c = lax.broadcasted_iota(jnp.int32, (T, T), 1)
    return (r >= c) if lower else (r <= c)


def _softplus(x):
    return jnp.maximum(x, 0.0) + jnp.log(1.0 + jnp.exp(-jnp.abs(x)))


def _dt_prep(dt_raw, bias, a_log, name):
    s = dt_raw.shape[0]
    nc = s // T

    def body(r_ref, b_ref, a_ref, dt_ref, cum_ref, cumt_ref):
        dt = _softplus(r_ref[...] + b_ref[...])
        da = dt * (-jnp.exp(a_ref[...]))
        pre = _dot_exact(_tri(True).astype(BF16), da)
        suf = _dot_exact(_tri(False).astype(BF16), da)
        lane = lax.broadcasted_iota(jnp.int32, (T, LANE), 1)
        cum = jnp.where(lane < SSM_HEADS, pre, suf)
        dt_ref[...] = dt
        cum_ref[...] = cum
        cumt_ref[...] = cum.T

    blk = pl.BlockSpec((T, LANE), lambda c: (c, 0))
    vec = pl.BlockSpec((1, LANE), lambda c: (0, 0))
    return pl.pallas_call(
        body, name=name, grid=(nc,), in_specs=[blk, vec, vec],
        out_specs=[blk, blk, pl.BlockSpec((None, LANE, T), lambda c: (c, 0, 0))],
        out_shape=[jax.ShapeDtypeStruct((s, LANE), F32), jax.ShapeDtypeStruct((s, LANE), F32), jax.ShapeDtypeStruct((nc, LANE, T), F32)],
        compiler_params=_cp("parallel"),
    )(dt_raw, bias, a_log)


def _by_group(t):
    s = t.shape[0]
    return t[:, :2 * SSM_HEADS].reshape(s, 2 * SSM_GROUPS, SSM_REP).transpose(1, 0, 2)


def _from_group(tf, tb):
    s = tf.shape[1]
    t = jnp.concatenate([tf, tb], axis=0).transpose(1, 0, 2).reshape(s, 2 * SSM_HEADS)
    return jnp.pad(t, ((0, 0), (0, LANE - 2 * SSM_HEADS)))


def _decay_mats(acol, arow, rev):
    after = _tri(not rev)
    return jnp.where(after, jnp.exp(jnp.where(after, acol - arow, 0.0)), 0.0)


PAIRS = SSM_REP // 2


def _low_lanes():
    return lax.broadcasted_iota(jnp.int32, (T, LANE), 1) < SSM_HEAD_DIM


def _block_diag(v, low):
    zero = jnp.zeros_like(v)
    return jnp.concatenate([jnp.where(low, v, zero), jnp.where(low, zero, v)], axis=0)


CPS = 2
TB = CPS * T


def _scan_specs(rev, ci):
    nxb = SSM_INNER // LANE
    kofs = SSM_GROUPS if rev else 0
    return [
        pl.BlockSpec((TB, GW), lambda g, c: (ci(c), g)),
        pl.BlockSpec((TB, LANE), lambda g, c: (ci(c), nxb + g)),
        pl.BlockSpec((TB, LANE), lambda g, c: (ci(c), nxb + SSM_GROUPS + g)),
        pl.BlockSpec((None, TB, SSM_REP), lambda g, c: (kofs + g, ci(c), 0)),
        pl.BlockSpec((None, TB, SSM_REP), lambda g, c: (kofs + g, ci(c), 0)),
        pl.BlockSpec((None, CPS, SSM_REP, T), lambda g, c: (kofs + g, ci(c), 0, 0)),
    ]


def _chunk_rows(q):
    return pl.ds(q * T, T)


def _pair_lanes(ref, p, low):
    return jnp.where(low, ref[:, 2 * p:2 * p + 1], ref[:, 2 * p + 1:2 * p + 2])


def _ssd_scan(xbc, dtk, cumk, cumtk, rev, name):
    s = xbc.shape[0]
    nc = s // T
    nb = nc // CPS
    last = 0 if rev else T - 1
    ci = (lambda c: nb - 1 - c) if rev else (lambda c: c)

    def body(x_ref, b_ref, c_ref, dt_ref, cum_ref, cumt_ref, y_ref, st_ref, state):
        @pl.when(pl.program_id(1) == 0)
        def _():
            state[...] = jnp.zeros_like(state)

        for q in (reversed(range(CPS)) if rev else range(CPS)):
            rows = _chunk_rows(q)
            chunk(x_ref.at[rows], b_ref.at[rows], c_ref.at[rows], dt_ref.at[rows], cum_ref.at[rows], cumt_ref.at[q],
                  y_ref.at[rows], st_ref.at[q], state)

    def chunk(x_ref, b_ref, c_ref, dt_ref, cum_ref, cumt_ref, y_ref, st_ref, state):
        bm = b_ref[...]
        cm = c_ref[...].astype(BF16)
        cb = _dot(cm, bm.astype(BF16), 1, 1)
        bt = bm.T.astype(BF16)
        low = _low_lanes()
        for p in range(PAIRS):
            ls = slice(p * LANE, (p + 1) * LANE)
            acum = _pair_lanes(cum_ref, p, low)
            u = x_ref[:, ls] * _pair_lanes(dt_ref, p, low)
            tot = acum[last:last + 1, :]
            m = [(cb * _decay_mats(cum_ref[:, r:r + 1], cumt_ref[r:r + 1, :], rev)).astype(BF16) for r in (2 * p, 2 * p + 1)]
            st = state[p]
            st_ref[p] = st
            yd = _dot(jnp.concatenate(m, axis=1), _block_diag(u.astype(BF16), low))
            yo = jnp.exp(acum) * _dot(cm, st.astype(BF16))
            y_ref[:, ls] = yd + yo
            state[p] = jnp.exp(tot) * st + _dot(bt, (jnp.exp(tot - acum) * u).astype(BF16))

    return pl.pallas_call(
        body, name=name, grid=(SSM_GROUPS, nb), in_specs=_scan_specs(rev, ci),
        out_specs=[
            pl.BlockSpec((TB, GW), lambda g, c: (ci(c), g)),
            pl.BlockSpec((CPS, PAIRS, SSM_STATE, LANE), lambda g, c: (ci(c), g, 0, 0)),
        ],
        out_shape=[jax.ShapeDtypeStruct((s, SSM_INNER), F32), jax.ShapeDtypeStruct((nc, SSM_HEADS // 2, SSM_STATE, LANE), F32)],
        scratch_shapes=[pltpu.VMEM((PAIRS, SSM_STATE, LANE), F32)],
        compiler_params=_cp("parallel", "arbitrary"),
    )(xbc, xbc, xbc, dtk, cumk, cumtk)


def _ssd_scan_bwd(xbc, dtk, cumk, cumtk, dy, states, dvec, prev, rev, name):
    s = xbc.shape[0]
    nc = s // T
    nb = nc // CPS
    last = 0 if rev else T - 1
    ci = (lambda c: c) if rev else (lambda c: nb - 1 - c)
    has_prev = prev is not None
    n_in = 12 if has_prev else 9

    def body(*refs):
        ins, outs, scratch = refs[:n_in], refs[n_in:n_in + 5], refs[n_in + 5:]

        @pl.when(pl.program_id(1) == 0)
        def _():
            scratch[0][...] = jnp.zeros_like(scratch[0])

        for q in (range(CPS) if rev else reversed(range(CPS))):
            rows = _chunk_rows(q)
            cut = lambda t: t.at[rows]
            x_ref, b_ref, c_ref, dt_ref, cum_ref, cumt_ref, dy_ref, st_ref, dv_ref = ins[:9]
            sub = [cut(x_ref), cut(b_ref), cut(c_ref), cut(dt_ref), cut(cum_ref), cumt_ref.at[q], cut(dy_ref), st_ref.at[q], dv_ref]
            chunk(*sub, *[cut(t) for t in ins[9:]], *[cut(t) for t in outs], *scratch)

    def chunk(*refs):
        x_ref, b_ref, c_ref, dt_ref, cum_ref, cumt_ref, dy_ref, st_ref, dv_ref = refs[:9]
        refs = refs[9:]
        if has_prev:
            pdx, pdb, pdc = refs[:3]
            refs = refs[3:]
        dx_ref, db_ref, dc_ref, ddt_ref, dda_ref, dstate, rs_buf, in_buf, k_buf = refs
        rs_buf[...] = jnp.zeros_like(rs_buf)
        in_buf[...] = jnp.zeros_like(in_buf)
        k_buf[...] = jnp.zeros_like(k_buf)
        bm = b_ref[...].astype(BF16)
        cm = c_ref[...].astype(BF16)
        cbt = _dot(bm, cm, 1, 1)
        cb = _dot(cm, bm, 1, 1)
        ct = c_ref[...].T.astype(BF16)
        after = _tri(not rev)
        before = _tri(rev)
        from_k = before.astype(BF16)
        ri = lax.broadcasted_iota(jnp.int32, (T, T), 0)
        cj = lax.broadcasted_iota(jnp.int32, (T, T), 1)
        strictly_before = (cj > ri) if rev else (cj < ri)
        dcb = jnp.zeros((T, T), F32)
        dc_acc = jnp.zeros((T, SSM_STATE), F32)
        db_acc = jnp.zeros((T, SSM_STATE), F32)
        low = _low_lanes()
        ri2 = lax.broadcasted_iota(jnp.int32, (LANE, LANE), 0)
        cj2 = lax.broadcasted_iota(jnp.int32, (LANE, LANE), 1)
        halves = ((ri2 < SSM_HEAD_DIM) == (cj2 == 0)) & (cj2 < 2)
        halves = halves.astype(BF16)

        def head_sums(v):
            hi = v.astype(BF16)
            lo = (v - hi.astype(F32)).astype(BF16)
            return _dot(hi, halves) + _dot(lo, halves)

        for p in range(PAIRS):
            ls = slice(p * LANE, (p + 1) * LANE)
            c2 = slice(2 * p, 2 * p + 2)
            lm, lmt = [], []
            for r in (2 * p, 2 * p + 1):
                acol = cum_ref[:, r:r + 1]
                arow = cumt_ref[r:r + 1, :]
                lm.append(jnp.where(after, jnp.exp(jnp.where(after, acol - arow, 0.0)), 0.0))
                lmt.append(jnp.where(before, jnp.exp(jnp.where(before, arow - acol, 0.0)), 0.0))
            acum = _pair_lanes(cum_ref, p, low)
            tot = acum[last:last + 1, :]
            dtl = _pair_lanes(dt_ref, p, low)
            xl = x_ref[:, ls]
            u = xl * dtl
            ub = u.astype(BF16)
            dyl = dy_ref[:, ls]
            dyb = dyl.astype(BF16)
            st = st_ref[p]
            stb = st.astype(BF16)
            dst = dstate[p]
            dstb = dst.astype(BF16)
            dec = jnp.exp(tot - acum)
            eac = jnp.exp(acum)
            etot = jnp.exp(tot)
            du_off = dec * _dot(bm, dstb)
            mt = jnp.concatenate([(cbt * lmt[0]).astype(BF16), (cbt * lmt[1]).astype(BF16)], axis=1)
            du = _dot(mt, _block_diag(dyb, low)) + du_off
            zero = jnp.zeros_like(dyb)
            gl = [_dot(jnp.where(low, dyb, zero), ub, 1, 1) * lm[0], _dot(jnp.where(low, zero, dyb), ub, 1, 1) * lm[1]]
            dcb = dcb + gl[0] + gl[1]
            dc_acc = dc_acc + _dot((eac * dyl).astype(BF16), stb, 1, 1)
            db_acc = db_acc + _dot((dec * u).astype(BF16), dstb, 1, 1)
            w = jnp.concatenate([(gl[0] * cb).astype(BF16), (gl[1] * cb).astype(BF16)], axis=1)
            crossing = _dot(from_k, w)
            for j in range(2):
                cr = jnp.where(strictly_before, crossing[:, j * T:(j + 1) * T], 0.0)
                in_buf[:, 2 * p + j:2 * p + j + 1] = jnp.sum(cr, axis=1, keepdims=True)
            y_off = eac * _dot(cm, stb)
            udu = u * du_off
            rs_buf[:, c2] = head_sums(dyl * y_off - udu)[:, 0:2]
            col = jnp.sum(dst * (etot * st) + udu, axis=0, keepdims=True)
            k_buf[0:1, c2] = head_sums(jnp.broadcast_to(col, (8, LANE)))[0:1, 0:2]
            ddt_ref[:, c2] = head_sums(du * xl)[:, 0:2]
            dx = du * dtl
            if has_prev:
                dx = dx + pdx[:, ls]
            else:
                dx = dx + dyl * dv_ref[:, ls]
            dx_ref[:, ls] = dx
            dstate[p] = etot * dst + _dot(ct, (eac * dyl).astype(BF16))
        dda = in_buf[...] + _dot_exact(from_k, rs_buf[...]) + k_buf[0:1, :]
        dda_ref[...] = dda[:, :SSM_REP]
        dcbb = dcb.astype(BF16)
        dc = dc_acc + _dot(dcbb, bm)
        db = db_acc + _dot(dcbb, cm, 0, 0)
        if has_prev:
            dc = dc + pdc[...]
            db = db + pdb[...]
        dc_ref[...] = dc
        db_ref[...] = db

    xspec = pl.BlockSpec((TB, GW), lambda g, c: (ci(c), g))
    gspec = pl.BlockSpec((TB, LANE), lambda g, c: (ci(c), g))
    in_specs = _scan_specs(rev, ci) + [
        xspec,
        pl.BlockSpec((CPS, PAIRS, SSM_STATE, LANE), lambda g, c: (ci(c), g, 0, 0)),
        pl.BlockSpec((1, GW), lambda g, c: (0, g)),
    ]
    args = [xbc, xbc, xbc, dtk, cumk, cumtk, dy, states, dvec]
    if has_prev:
        in_specs += [xspec, gspec, gspec]
        args += list(prev)
    ospec8 = pl.BlockSpec((None, TB, SSM_REP), lambda g, c: (g, ci(c), 0))
    return pl.pallas_call(
        body, name=name, grid=(SSM_GROUPS, nb), in_specs=in_specs,
        out_specs=[xspec, gspec, gspec, ospec8, ospec8],
        out_shape=[jax.ShapeDtypeStruct((s, SSM_INNER), F32), jax.ShapeDtypeStruct((s, SSM_BC), F32), jax.ShapeDtypeStruct((s, SSM_BC), F32),
                   jax.ShapeDtypeStruct((SSM_GROUPS, s, SSM_REP), F32), jax.ShapeDtypeStruct((SSM_GROUPS, s, SSM_REP), F32)],
        scratch_shapes=[pltpu.VMEM((PAIRS, SSM_STATE, LANE), F32), pltpu.VMEM((T, LANE), F32), pltpu.VMEM((T, LANE), F32),
                        pltpu.VMEM((8, LANE), F32)],
        compiler_params=_cp("parallel", "arbitrary"),
    )(*args)


def _ssd_post(yf, yb, xbc, z, dvec, nw, name):
    s = z.shape[0]
    tm = min(256, s)

    def body(yf_ref, yb_ref, x_ref, z_ref, dv_ref, nw_ref, o_ref):
        ys = yf_ref[...] + yb_ref[...] + dv_ref[...] * x_ref[...]
        yg = ys * _silu(z_ref[...])
        ms = jnp.mean(yg * yg, axis=1, keepdims=True)
        o_ref[...] = (yg * lax.rsqrt(ms + RMS_EPS) * nw_ref[...]).astype(BF16)

    rs = _row_spec(tm, SSM_INNER)
    vs = _vec_spec(SSM_INNER)
    return pl.pallas_call(
        body, name=name, grid=(s // tm,), in_specs=[rs, rs, rs, rs, vs, vs], out_specs=rs,
        out_shape=jax.ShapeDtypeStruct((s, SSM_INNER), BF16), compiler_params=_cp("parallel"),
    )(yf, yb, xbc, z, dvec, nw)


def _ssd_post_bwd(dyn, yf, yb, xbc, z, dvec, nw, name):
    s = z.shape[0]
    tm = min(256, s)

    def body(dyn_ref, yf_ref, yb_ref, x_ref, z_ref, dv_ref, nw_ref, dys_ref, dz_ref, red_ref):
        @pl.when(pl.program_id(0) == 0)
        def _():
            red_ref[...] = jnp.zeros_like(red_ref)

        xv, zv = x_ref[...], z_ref[...]
        ys = yf_ref[...] + yb_ref[...] + dv_ref[...] * xv
        sz = _silu(zv)
        yg = ys * sz
        rstd = lax.rsqrt(jnp.mean(yg * yg, axis=1, keepdims=True) + RMS_EPS)
        yhat = yg * rstd
        dynv = dyn_ref[...]
        dyh = dynv * nw_ref[...]
        dyg = rstd * (dyh - yhat * jnp.mean(dyh * yhat, axis=1, keepdims=True))
        dys = dyg * sz
        dys_ref[...] = dys
        dz_ref[...] = (dyg * ys * _dsilu(zv)).astype(BF16)
        red_ref[0:1, :] += jnp.sum(dynv * yhat, axis=0, keepdims=True)
        red_ref[1:2, :] += jnp.sum(dys * xv, axis=0, keepdims=True)

    rs = _row_spec(tm, SSM_INNER)
    vs = _vec_spec(SSM_INNER)
    return pl.pallas_call(
        body, name=name, grid=(s // tm,), in_specs=[rs, rs, rs, rs, rs, vs, vs],
        out_specs=[rs, rs, _vec_spec(SSM_INNER, 8)],
        out_shape=[jax.ShapeDtypeStruct((s, SSM_INNER), F32), jax.ShapeDtypeStruct((s, SSM_INNER), BF16), jax.ShapeDtypeStruct((8, SSM_INNER), F32)],
        compiler_params=_cp("arbitrary"),
    )(dyn, yf, yb, xbc, z, dvec, nw)


def _dt_bwd(dt_raw, bias, a_log, dt, ddt, dda, name):
    s = dt_raw.shape[0]
    tm = min(1024, s)

    def body(r_ref, b_ref, a_ref, dt_ref, ddt_ref, dda_ref, o_ref, red_ref):
        @pl.when(pl.program_id(0) == 0)
        def _():
            red_ref[...] = jnp.zeros_like(red_ref)

        a = -jnp.exp(a_ref[...])
        ddav = dda_ref[...]
        draw = (ddt_ref[...] + a * ddav) * _sigmoid(r_ref[...] + b_ref[...])
        o_ref[...] = draw.astype(BF16)
        red_ref[0:1, :] += jnp.sum(draw, axis=0, keepdims=True)
        red_ref[1:2, :] += a * jnp.sum(ddav * dt_ref[...], axis=0, keepdims=True)

    rs = _row_spec(tm, LANE)
    vs = _vec_spec(LANE)
    return pl.pallas_call(
        body, name=name, grid=(s // tm,), in_specs=[rs, vs, vs, rs, rs, rs], out_specs=[rs, _vec_spec(LANE, 8)],
        out_shape=[jax.ShapeDtypeStruct((s, LANE), BF16), jax.ShapeDtypeStruct((8, LANE), F32)],
        compiler_params=_cp("arbitrary"),
    )(dt_raw, bias, a_log, dt, ddt, dda)


def _pad_lanes(v):
    v = v.reshape(1, -1)
    return jnp.pad(v, ((0, 0), (0, LANE - v.shape[1])))


def _ssd_prep_weights(w_in, conv_w, conv_b, dt_bias, a_log, d_skip, norm_w, w_out):
    return dict(
        w_z=w_in[:, :SSM_INNER].astype(BF16),
        w_xbc=w_in[:, SSM_INNER:SSM_INNER + SSM_CONV_DIM].astype(BF16),
        w_dt=jnp.pad(w_in[:, SSM_INNER + SSM_CONV_DIM:], ((0, 0), (0, LANE - 2 * SSM_HEADS))).astype(BF16),
        conv_w=conv_w, conv_b=conv_b.reshape(1, -1), bias=_pad_lanes(dt_bias), a_log=_pad_lanes(a_log),
        dvec=jnp.repeat(d_skip, SSM_HEAD_DIM).reshape(1, -1), nw=norm_w.reshape(1, -1), w_out=w_out.astype(BF16),
    )


def _ssd_layer_fwd(h, w, li):
    nm = lambda t: f"b{li}_{t}"
    z = _mm(h, w["w_z"], name=nm("z"))
    xraw = _mm(h, w["w_xbc"], name=nm("xbc"))
    dt_raw = _mm(h, w["w_dt"], name=nm("dt"))
    pre, xbc = _conv_fwd(xraw, w["conv_w"], w["conv_b"], nm("conv"))
    dt, cum, cumt = _dt_prep(dt_raw, w["bias"], w["a_log"], nm("dtprep"))
    nc = cumt.shape[0]
    dtk, cumk = _by_group(dt), _by_group(cum)
    cumtk = cumt[:, :2 * SSM_HEADS].reshape(nc, 2 * SSM_GROUPS, SSM_REP, T).transpose(1, 0, 2, 3)
    yf, stf = _ssd_scan(xbc, dtk, cumk, cumtk, False, nm("scan_f"))
    yb, stb = _ssd_scan(xbc, dtk, cumk, cumtk, True, nm("scan_b"))
    yn = _ssd_post(yf, yb, xbc, z, w["dvec"], w["nw"], nm("post"))
    out = _mm(yn, w["w_out"], name=nm("out"))
    return out, (z, xraw, dt_raw, pre, xbc, dt, dtk, cumk, cumtk, yf, stf, yb, stb, yn)


def _ssd_layer_bwd(dy, h, saved, w, li):
    nm = lambda t: f"b{li}_{t}"
    z, xraw, dt_raw, pre, xbc, dt, dtk, cumk, cumtk, yf, stf, yb, stb, yn = saved
    g_w_out = _mm(yn, dy, ta=True, out_dtype=BF16, name=nm("dwout"))
    dyn = _mm(dy, w["w_out"], tb=True, name=nm("dyn"))
    dys, dz, pred = _ssd_post_bwd(dyn, yf, yb, xbc, z, w["dvec"], w["nw"], nm("postbwd"))
    dx1, db1, dc1, ddt_f, dda_f = _ssd_scan_bwd(xbc, dtk, cumk, cumtk, dys, stf, w["dvec"], None, False, nm("scanbwd_f"))
    dx, db, dc, ddt_b, dda_b = _ssd_scan_bwd(xbc, dtk, cumk, cumtk, dys, stb, w["dvec"], (dx1, db1, dc1), True, nm("scanbwd_b"))
    dxraw, cred = _conv_bwd(dx, db, dc, pre, xraw, w["conv_w"], nm("convbwd"))
    draw, dred = _dt_bwd(dt_raw, w["bias"], w["a_log"], dt, _from_group(ddt_f, ddt_b), _from_group(dda_f, dda_b), nm("dtbwd"))
    dh = _mm(dz, w["w_z"], tb=True, name=nm("dh_z"))
    dh = _mm(dxraw, w["w_xbc"], tb=True, add=dh, name=nm("dh_xbc"))
    dh = _mm(draw, w["w_dt"], tb=True, add=dh, name=nm("dh_dt"))
    g_w_in = jnp.concatenate([_mm(h, dz, ta=True, out_dtype=BF16, name=nm("dwz")), _mm(h, dxraw, ta=True, out_dtype=BF16, name=nm("dwxbc")),
                              _mm(h, draw, ta=True, out_dtype=BF16, name=nm("dwdt"))[:, :2 * SSM_HEADS]], axis=1)
    grads = (g_w_in, cred[:SSM_CONV], cred[SSM_CONV], dred[0, :2 * SSM_HEADS].reshape(2, SSM_HEADS),
             dred[1, :2 * SSM_HEADS].reshape(2, SSM_HEADS), pred[1].reshape(SSM_HEADS, SSM_HEAD_DIM).sum(axis=1), pred[0], g_w_out)
    return dh, grads


B_GRAD_NAMES = ("b_w_in", "b_conv_w", "b_conv_b", "b_dt_bias", "b_a_log", "b_d", "b_norm_w", "b_w_out")


def _local_step(x, tgt, mod, w):
    d = x.shape[1]
    qkv_cols = QKV_COLS
    layers = []
    for i in range(DEPTH):
        j = i // 2
        if i % 2 == 0:
            layers.append((w["a_w_in"][j][:, :qkv_cols].astype(BF16), w["a_w_in"][j][:, qkv_cols:].astype(BF16), w["a_w_out"][j].astype(BF16)))
        else:
            layers.append(_ssd_prep_weights(w["b_w_in"][j], w["b_conv_w"][j], w["b_conv_b"][j], w["b_dt_bias"][j], w["b_a_log"][j],
                                            w["b_d"][j], w["b_norm_w"][j], w["b_w_out"][j]))
    saved = []
    for i in range(DEPTH):
        shift, scale, gate = mod[i:i + 1, :d], mod[i:i + 1, d:2 * d], mod[i:i + 1, 2 * d:]
        h = _modulate(x, scale, shift, f"l{i}_mod")
        if i % 2 == 0:
            out, sv = _attn_layer_fwd(h, *layers[i], i)
        else:
            out, sv = _ssd_layer_fwd(h, layers[i], i)
        xn = _resid_ln(x, out, gate, w["ln_g"][i:i + 1], w["ln_b"][i:i + 1], f"l{i}_ln")
        saved.append((x, h, out, sv))
        x = xn
    dx, lred = _loss_grad(x, tgt, "loss")
    loss = 0.5 * jnp.sum(lred[0]) / d
    dmod, g_ln_g, g_ln_b = [None] * DEPTH, [None] * DEPTH, [None] * DEPTH
    ga_in, ga_out = [None, None], [None, None]
    gb = [None, None]
    for i in reversed(range(DEPTH)):
        j = i // 2
        xi, h, out, sv = saved[i]
        scale, gate = mod[i:i + 1, d:2 * d], mod[i:i + 1, 2 * d:]
        du, dy, red = _resid_ln_bwd(xi, out, dx, gate, w["ln_g"][i:i + 1], f"l{i}_lnbwd")
        g_ln_g[i], g_ln_b[i] = red[1], red[2]
        if i % 2 == 0:
            dhs, ga_in[j], ga_out[j] = _attn_layer_bwd(dy, h, sv, *layers[i], i)
        else:
            dh, gb[j] = _ssd_layer_bwd(dy, h, sv, layers[i], i)
            dhs = [dh]
        dx, red2 = _modulate_bwd(du, dhs, xi, scale, f"l{i}_modbwd")
        dmod[i] = jnp.concatenate([red2[1], red2[0], red[0]])
    grads = {"ln_g": jnp.stack(g_ln_g), "ln_b": jnp.stack(g_ln_b), "a_w_in": jnp.stack(ga_in), "a_w_out": jnp.stack(ga_out)}
    for k, n in enumerate(B_GRAD_NAMES):
        grads[n] = jnp.stack([gb[0][k], gb[1][k]])
    return loss, dx, jnp.stack(dmod), grads


MESH = pl.DeviceIdType.MESH
ANY = pl.BlockSpec(memory_space=pl.ANY)
N_DEV = 8
N_SHARD = 4


def _flip(v, bit):
    return 1 - v if bit else v


def _all_gather8(v, name):
    def body(v_ref, o_ref, send_sems, recv_sems, local_sem):
        x, y, c = lax.axis_index("x"), lax.axis_index("y"), lax.axis_index("c")
        me = 4 * x + 2 * y + c
        local = pltpu.make_async_copy(v_ref, o_ref.at[me], local_sem)
        local.start()
        copies = []
        for k in range(1, N_DEV):
            peer = (_flip(x, k & 4), _flip(y, k & 2), _flip(c, k & 1))
            copies.append(pltpu.make_async_remote_copy(
                src_ref=v_ref, dst_ref=o_ref.at[me], send_sem=send_sems.at[k - 1], recv_sem=recv_sems.at[k - 1],
                device_id=peer, device_id_type=MESH))
        for cp in copies:
            cp.start()
        for cp in copies:
            cp.wait()
        local.wait()

    return pl.pallas_call(
        body, name=name, in_specs=[ANY], out_specs=ANY, out_shape=jax.ShapeDtypeStruct((N_DEV,) + v.shape, v.dtype),
        scratch_shapes=[pltpu.SemaphoreType.DMA((N_DEV - 1,)), pltpu.SemaphoreType.DMA((N_DEV - 1,)), pltpu.SemaphoreType.DMA],
    )(v)


def _transpose_shards(srcs, name):
    n = len(srcs)
    n_rem = (N_SHARD - 1) * n

    def body(*refs):
        s_refs, o_refs = refs[:n], refs[n:2 * n]
        send_sems, recv_sems, local_sems = refs[2 * n:]
        x, y, c = lax.axis_index("x"), lax.axis_index("y"), lax.axis_index("c")
        m = 2 * x + y
        local = [pltpu.make_async_copy(s_refs[a].at[m], o_refs[a].at[m], local_sems.at[a]) for a in range(n)]
        remote = []
        for k in range(1, N_SHARD):
            px, py = _flip(x, k & 2), _flip(y, k & 1)
            for a in range(n):
                i = (k - 1) * n + a
                remote.append(pltpu.make_async_remote_copy(
                    src_ref=s_refs[a].at[2 * px + py], dst_ref=o_refs[a].at[m], send_sem=send_sems.at[i], recv_sem=recv_sems.at[i],
                    device_id=(px, py, c), device_id_type=MESH))
        for cp in local + remote:
            cp.start()
        for cp in remote + local:
            cp.wait()

    return pl.pallas_call(
        body, name=name, in_specs=[ANY] * n, out_specs=[ANY] * n, out_shape=[jax.ShapeDtypeStruct(s.shape, s.dtype) for s in srcs],
        scratch_shapes=[pltpu.SemaphoreType.DMA((n_rem,)), pltpu.SemaphoreType.DMA((n_rem,)), pltpu.SemaphoreType.DMA((n,))],
    )(*srcs)


def _gather_shards(src, name):
    rows = src.shape[0]
    half = rows // 2
    n_ici = N_SHARD - 1

    def body(s_ref, o_ref, send_sems, recv_sems, local_sem):
        x, y, c = lax.axis_index("x"), lax.axis_index("y"), lax.axis_index("c")
        m = 2 * x + y
        sibling = (x, y, 1 - c)
        my_half = pl.ds(pl.multiple_of(c * half, 16), half)
        its_half = pl.ds(pl.multiple_of((1 - c) * half, 16), half)
        local = pltpu.make_async_copy(s_ref, o_ref.at[m], local_sem)
        local.start()
        chips = [(_flip(x, k & 2), _flip(y, k & 1)) for k in range(1, N_SHARD)]

        def copy(sem, src_ref, dst_ref, to):
            return pltpu.make_async_remote_copy(src_ref=src_ref, dst_ref=dst_ref, send_sem=send_sems.at[sem],
                                                recv_sem=recv_sems.at[sem], device_id=to, device_id_type=MESH)

        first = [copy(i, s_ref.at[my_half], o_ref.at[m, my_half], (px, py, c)) for i, (px, py) in enumerate(chips)]
        for cp in first:
            cp.start()
        passed = []
        for i, (px, py) in enumerate(chips):
            landed = o_ref.at[2 * px + py, my_half]
            copy(i, landed, landed, (px, py, c)).wait_recv()
            passed.append(copy(n_ici + i, landed, landed, sibling))
            passed[-1].start()
        for i, (px, py) in enumerate(chips):
            from_sibling = o_ref.at[2 * px + py, its_half]
            copy(n_ici + i, from_sibling, from_sibling, sibling).wait_recv()
        for cp in first + passed:
            cp.wait_send()
        local.wait()

    return pl.pallas_call(
        body, name=name, in_specs=[ANY], out_specs=ANY, out_shape=jax.ShapeDtypeStruct((N_SHARD,) + src.shape, src.dtype),
        scratch_shapes=[pltpu.SemaphoreType.DMA((2 * n_ici,)), pltpu.SemaphoreType.DMA((2 * n_ici,)), pltpu.SemaphoreType.DMA],
    )(src)


def _swap_sibling(vs, name):
    n = len(vs)

    def body(*refs):
        v_refs, o_refs, (send_sems, recv_sems) = refs[:n], refs[n:2 * n], refs[2 * n:]
        x, y, c = lax.axis_index("x"), lax.axis_index("y"), lax.axis_index("c")
        copies = [pltpu.make_async_remote_copy(src_ref=v_refs[a], dst_ref=o_refs[a], send_sem=send_sems.at[a], recv_sem=recv_sems.at[a],
                                               device_id=(x, y, 1 - c), device_id_type=MESH) for a in range(n)]
        for cp in copies:
            cp.start()
        for cp in copies:
            cp.wait()

    return pl.pallas_call(
        body, name=name, in_specs=[ANY] * n, out_specs=[ANY] * n, out_shape=[jax.ShapeDtypeStruct(v.shape, v.dtype) for v in vs],
        scratch_shapes=[pltpu.SemaphoreType.DMA((n,)), pltpu.SemaphoreType.DMA((n,))],
    )(*vs)


def _row_tile(r, elems, step):
    ok = [t for t in range(step, r + 1, step) if r % t == 0 and t <= elems]
    return max(ok) if ok else r


def _sum_slots(a, name):
    n, r, cdim = a.shape
    tm = _row_tile(r, (4 << 20) // (cdim * 4 * (n + 1)), 16)

    def body(a_ref, o_ref):
        acc = a_ref[0].astype(F32)
        for k in range(1, n):
            acc = acc + a_ref[k].astype(F32)
        o_ref[...] = acc

    return pl.pallas_call(
        body, name=name, grid=(r // tm,), in_specs=[pl.BlockSpec((n, tm, cdim), lambda i: (0, i, 0))],
        out_specs=pl.BlockSpec((tm, cdim), lambda i: (i, 0)), out_shape=jax.ShapeDtypeStruct((r, cdim), F32),
        compiler_params=_cp("parallel"),
    )(a)


def _silu_rows(v, name):
    def body(v_ref, o_ref):
        o_ref[...] = _silu(v_ref[...])

    return pl.pallas_call(body, name=name, out_shape=jax.ShapeDtypeStruct(v.shape, F32))(v)


PACK_COLS = 1024


def _adamw(w, gs, m, v, name):
    r, cdim = w.shape
    tm = _row_tile(r, (1 << 18) // cdim, 8)
    c1 = 1.0 / (1.0 - ADAM_B1 ** ADAM_STEP)
    c2 = 1.0 / (1.0 - ADAM_B2 ** ADAM_STEP)
    ng = len(gs)

    def body(*refs):
        w_ref, g_refs, (m_ref, v_ref, g_ref, d_ref, nm_ref, nv_ref) = refs[0], refs[1:1 + ng], refs[1 + ng:]
        g = g_refs[0][...]
        for t in g_refs[1:]:
            g = g + t[...]
        mn = ADAM_B1 * m_ref[...] + (1.0 - ADAM_B1) * g
        vn = ADAM_B2 * v_ref[...] + (1.0 - ADAM_B2) * (g * g)
        g_ref[...] = g
        nm_ref[...] = mn
        nv_ref[...] = vn
        d_ref[...] = -ADAM_LR * ((mn * c1) / (jnp.sqrt(vn * c2) + ADAM_EPS) + ADAM_WD * w_ref[...])

    spec = pl.BlockSpec((tm, cdim), lambda i: (i, 0))
    return pl.pallas_call(
        body, name=name, grid=(r // tm,), in_specs=[spec] * (3 + ng), out_specs=[spec] * 4,
        out_shape=[jax.ShapeDtypeStruct(w.shape, F32)] * 4, compiler_params=_cp("parallel"),
    )(w, *gs, m, v)


def _rows(a):
    f = a.reshape(-1)
    pad = (-f.shape[0]) % PACK_COLS
    if pad:
        f = jnp.pad(f, (0, pad))
    return f.reshape(-1, PACK_COLS)


def _nrows(shape):
    return -(-int(np.prod(shape)) // PACK_COLS)


def _pack(parts, total_rows=None):
    p = jnp.concatenate([_rows(a) for a in parts], axis=0)
    if total_rows is not None and total_rows > p.shape[0]:
        p = jnp.pad(p, ((0, total_rows - p.shape[0]), (0, 0)))
    return p


def _unpack(p, shapes):
    out, r0 = [], 0
    for shp in shapes:
        n = int(np.prod(shp))
        nr = _nrows(shp)
        out.append(p[r0:r0 + nr].reshape(-1)[:n].reshape(shp))
        r0 += nr
    return out


def _unshard_cols(g):
    return jnp.concatenate([g[k] for k in range(N_SHARD)], axis=-1)


def _shard_cols(a):
    n = a.shape[-1] // N_SHARD
    return jnp.stack([a[..., k * n:(k + 1) * n] for k in range(N_SHARD)])


def _unshard_rows(g):
    return jnp.concatenate([g[k] for k in range(N_SHARD)], axis=1)


def _shard_rows(a):
    n = a.shape[1] // N_SHARD
    return jnp.stack([a[:, k * n:(k + 1) * n] for k in range(N_SHARD)])


W_NAMES = ("ada_w", "ada_b", "ln_g", "ln_b", "a_w_in", "a_w_out", "b_w_in", "b_conv_w", "b_conv_b", "b_dt_bias", "b_a_log", "b_d",
           "b_norm_w", "b_w_out")
BIG = ("a_w_in", "a_w_out", "b_w_in", "b_w_out")
SMALL = ("ada_b", "ln_g", "ln_b", "b_conv_w", "b_conv_b", "b_dt_bias", "b_a_log", "b_d", "b_norm_w")


def kernel(x, c, ada_w, ada_b, ln_g, ln_b, a_w_in, a_w_out, b_w_in, b_conv_w, b_conv_b, b_dt_bias, b_a_log, b_d, b_norm_w, b_w_out, loss_target, m_ada_w, m_ada_b, m_ln_g, m_ln_b, m_a_w_in, m_a_w_out, m_b_w_in, m_b_conv_w, m_b_conv_b, m_b_dt_bias, m_b_a_log, m_b_d, m_b_norm_w, m_b_w_out, v_ada_w, v_ada_b, v_ln_g, v_ln_b, v_a_w_in, v_a_w_out, v_b_w_in, v_b_conv_w, v_b_conv_b, v_b_dt_bias, v_b_a_log, v_b_d, v_b_norm_w, v_b_w_out):
    w = dict(ada_w=ada_w, ada_b=ada_b, ln_g=ln_g, ln_b=ln_b, a_w_in=a_w_in, a_w_out=a_w_out, b_w_in=b_w_in, b_conv_w=b_conv_w,
             b_conv_b=b_conv_b, b_dt_bias=b_dt_bias, b_a_log=b_a_log, b_d=b_d, b_norm_w=b_norm_w, b_w_out=b_w_out)
    mom = dict(ada_w=m_ada_w, ada_b=m_ada_b, ln_g=m_ln_g, ln_b=m_ln_b, a_w_in=m_a_w_in, a_w_out=m_a_w_out, b_w_in=m_b_w_in,
               b_conv_w=m_b_conv_w, b_conv_b=m_b_conv_b, b_dt_bias=m_b_dt_bias, b_a_log=m_b_a_log, b_d=m_b_d, b_norm_w=m_b_norm_w,
               b_w_out=m_b_w_out)
    var = dict(ada_w=v_ada_w, ada_b=v_ada_b, ln_g=v_ln_g, ln_b=v_ln_b, a_w_in=v_a_w_in, a_w_out=v_a_w_out, b_w_in=v_b_w_in,
               b_conv_w=v_b_conv_w, b_conv_b=v_b_conv_b, b_dt_bias=v_b_dt_bias, b_a_log=v_b_a_log, b_d=v_b_d, b_norm_w=v_b_norm_w,
               b_w_out=v_b_w_out)
    ax, ay, ac = lax.axis_index("x"), lax.axis_index("y"), lax.axis_index("c")
    me = 4 * ax + 2 * ay + ac
    shard = 2 * ax + ay
    d = x.shape[-1]
    dsh = ada_w.shape[-1]

    small_in = (c, b_conv_w, b_conv_b, b_norm_w)
    g0 = _all_gather8(_pack(small_in).reshape(-1, LANE), "gather_small_in").reshape(N_DEV, -1, PACK_COLS)
    per_dev = [_unpack(g0[k], [a.shape for a in small_in]) for k in range(N_DEV)]
    c_all = jnp.concatenate([p[0] for p in per_dev], axis=0)
    conv_w_full, conv_b_full, norm_w_full = (_unshard_cols([per_dev[2 * k][t] for k in range(N_SHARD)]) for t in (1, 2, 3))

    cond = _silu_rows(jnp.pad(c_all, ((0, 8), (0, 0))), "cond")
    bias = lax.dynamic_slice_in_dim(ada_b, shard * dsh, dsh, axis=1)
    part = jnp.stack([_mm(cond, ada_w[i], add=jnp.broadcast_to(bias[i], (16, dsh)), name=f"mod{i}")[:N_DEV] for i in range(DEPTH)])
    g1 = _all_gather8(part.reshape(-1, LANE), "gather_mod").reshape(N_DEV, DEPTH, N_DEV, dsh)
    mod_all = _unshard_cols([g1[2 * k] for k in range(N_SHARD)])
    mod = lax.dynamic_index_in_dim(mod_all, me, axis=1, keepdims=False)

    gw = _gather_shards(_pack([w[n] for n in BIG]).astype(BF16), "gather_weights")
    big_sh = [_unpack(gw[k], [w[n].shape for n in BIG]) for k in range(N_SHARD)]
    full = dict(
        ln_g=ln_g, ln_b=ln_b, b_dt_bias=b_dt_bias, b_a_log=b_a_log, b_d=b_d,
        b_conv_w=conv_w_full, b_conv_b=conv_b_full, b_norm_w=norm_w_full,
        a_w_in=_unshard_cols([s[0] for s in big_sh]), a_w_out=_unshard_rows([s[1] for s in big_sh]),
        b_w_in=_unshard_cols([s[2] for s in big_sh]), b_w_out=_unshard_rows([s[3] for s in big_sh]),
    )

    loss, grad_x, dmod, g = _local_step(x[0], loss_target[0], mod, full)

    gsh = (_shard_cols(g["a_w_in"]), _shard_rows(g["a_w_out"]), _shard_cols(g["b_w_in"]), _shard_rows(g["b_w_out"]))
    to_send = [t.reshape(N_SHARD, -1, t.shape[-1]) for t in gsh]
    arrived = _transpose_shards(to_send, "scatter_grads")
    mine = [_sum_slots(t, f"sum_{n}") for n, t in zip(BIG, arrived)]
    theirs = _swap_sibling(mine, "swap_grads")

    small_g = (dmod, g["ln_g"], g["ln_b"], g["b_dt_bias"], g["b_a_log"], g["b_d"], g["b_conv_w"], g["b_conv_b"], g["b_norm_w"],
               loss.reshape(1))
    g2 = _all_gather8(_pack(small_g).reshape(-1, LANE), "gather_small_grads")
    tot = _unpack(_sum_slots(g2, "sum_small").reshape(-1, PACK_COLS), [a.shape for a in small_g])
    g_ada_b, g_ln_g, g_ln_b, g_dt_bias, g_a_log, g_d, g_conv_w, g_conv_b, g_norm_w, loss_sum = tot
    dmod_all = g2.reshape(N_DEV, -1)[:, :dmod.size].reshape(N_DEV, DEPTH, 3 * d)
    dmod_mine = lax.dynamic_slice_in_dim(dmod_all, shard * dsh, dsh, axis=2)
    g_ada_w = jnp.stack([_mm(cond, jnp.pad(dmod_mine[:, i], ((0, 8), (0, 0))), ta=True, name=f"dada{i}") for i in range(DEPTH)])
    csh = g_conv_w.shape[-1] // N_SHARD
    nsh = g_norm_w.shape[-1] // N_SHARD
    small_grads = dict(
        ada_w=g_ada_w, ada_b=g_ada_b, ln_g=g_ln_g, ln_b=g_ln_b, b_dt_bias=g_dt_bias, b_a_log=g_a_log, b_d=g_d,
        b_conv_w=lax.dynamic_slice_in_dim(g_conv_w, shard * csh, csh, axis=2),
        b_conv_b=lax.dynamic_slice_in_dim(g_conv_b, shard * csh, csh, axis=1),
        b_norm_w=lax.dynamic_slice_in_dim(g_norm_w, shard * nsh, nsh, axis=1),
    )

    by_name = [{}, {}, {}, {}]

    def update(n, gs):
        two_d = lambda t: t.reshape(-1, t.shape[-1])
        outs = _adamw(two_d(w[n]), [two_d(t) for t in gs], two_d(mom[n]), two_d(var[n]), f"adamw_{n}")
        for t, o in zip(by_name, outs):
            t[n] = o.reshape(w[n].shape)

    for i, n in enumerate(BIG):
        update(n, [mine[i], theirs[i]])
    update("ada_w", [small_grads["ada_w"]])
    rest = SMALL
    rows = -(-sum(_nrows(w[n].shape) for n in rest) // 8) * 8
    packed = _adamw(_pack([w[n] for n in rest], rows), [_pack([small_grads[n] for n in rest], rows)],
                    _pack([mom[n] for n in rest], rows), _pack([var[n] for n in rest], rows), "adamw_small")
    for t, p in zip(by_name, packed):
        t.update(zip(rest, _unpack(p, [w[n].shape for n in rest])))
    return (loss_sum.reshape(()), grad_x[None], *[t[n] for t in by_name for n in W_NAMES])
```

```python
import jax
import jax.numpy as jnp
import numpy as np
from jax import lax
from jax.experimental import pallas as pl
from jax.experimental.pallas import tpu as pltpu

F32 = jnp.float32
BF16 = jnp.bfloat16

DEPTH = 4
A_HEADS = 16
A_HEAD_DIM = 64
A_WIDTH = A_HEADS * A_HEAD_DIM
DILATIONS = (1, 4, 16)
A_RADIUS = 64
A_QBLOCK = 128
SSM_HEADS = 32
SSM_HEAD_DIM = 64
SSM_STATE = 128
SSM_GROUPS = 4
SSM_REP = SSM_HEADS // SSM_GROUPS
SSM_CONV = 5
SSM_CHUNK = 128
DEEPNORM_ALPHA = (2 * DEPTH) ** 0.25
LN_EPS = 1e-5
RMS_EPS = 1e-5
ADAM_LR, ADAM_B1, ADAM_B2, ADAM_EPS, ADAM_WD, ADAM_STEP = 0.001, 0.9, 0.999, 1e-08, 0.01, 10
VMEM_LIMIT = 56 * 1024 * 1024
LANE = 128


def _cp(*sem):
    return pltpu.CompilerParams(dimension_semantics=sem, vmem_limit_bytes=VMEM_LIMIT)


def _tile(dim, target):
    if dim <= target:
        return dim
    t = (target // LANE) * LANE
    while dim % t:
        t -= LANE
    return t


def _sigmoid(x):
    return 1.0 / (1.0 + jnp.exp(-x))


def _silu(x):
    return x * _sigmoid(x)


def _dsilu(x):
    s = _sigmoid(x)
    return s * (1.0 + x * (1.0 - s))


def _split3(x):
    a = x.astype(BF16)
    r = x - a.astype(F32)
    b = r.astype(BF16)
    c = (r - b.astype(F32)).astype(BF16)
    return a, b, c


def _dot(a, b, ca=1, cb=0):
    return lax.dot_general(a, b, (((ca,), (cb,)), ((), ())), preferred_element_type=F32)


def _dot_exact(m01, x):
    a, b, c = _split3(x)
    return _dot(m01, a) + _dot(m01, b) + _dot(m01, c)


def _mm(a, b, *, ta=False, tb=False, add=None, out_dtype=F32, name, tm=1024, tn=1024, tk=1024):
    m, k = (a.shape[1], a.shape[0]) if ta else a.shape
    n = b.shape[0] if tb else b.shape[1]
    assert (b.shape[1] if tb else b.shape[0]) == k
    tm, tn, tk = _tile(m, tm), _tile(n, tn), _tile(k, tk)
    nk = k // tk
    has_add = add is not None

    def body(*refs):
        if has_add:
            a_ref, b_ref, c_ref, o_ref, acc = refs
        else:
            a_ref, b_ref, o_ref, acc = refs
        kk = pl.program_id(2)
        part = _dot(a_ref[...].astype(BF16), b_ref[...].astype(BF16), 0 if ta else 1, 1 if tb else 0)

        def finish(r):
            if has_add:
                r = r + c_ref[...]
            o_ref[...] = r.astype(o_ref.dtype)

        if nk == 1:
            finish(part)
            return

        @pl.when(kk == 0)
        def _():
            acc[...] = part

        @pl.when((kk > 0) & (kk < nk - 1))
        def _():
            acc[...] += part

        @pl.when(kk == nk - 1)
        def _():
            finish(acc[...] + part)

    a_spec = pl.BlockSpec((tk, tm), lambda i, j, kk: (kk, i)) if ta else pl.BlockSpec((tm, tk), lambda i, j, kk: (i, kk))
    b_spec = pl.BlockSpec((tn, tk), lambda i, j, kk: (j, kk)) if tb else pl.BlockSpec((tk, tn), lambda i, j, kk: (kk, j))
    in_specs = [a_spec, b_spec]
    args = [a, b]
    if has_add:
        in_specs.append(pl.BlockSpec((tm, tn), lambda i, j, kk: (i, j)))
        args.append(add)
    return pl.pallas_call(
        body, name=name, grid=(m // tm, n // tn, nk), in_specs=in_specs,
        out_specs=pl.BlockSpec((tm, tn), lambda i, j, kk: (i, j)),
        out_shape=jax.ShapeDtypeStruct((m, n), out_dtype),
        scratch_shapes=[pltpu.VMEM((tm, tn) if nk > 1 else (8, LANE), F32)],
        compiler_params=_cp("parallel", "parallel", "arbitrary"),
    )(*args)


ROWS = 512


def _row_spec(tm, d):
    return pl.BlockSpec((tm, d), lambda i: (i, 0))


def _vec_spec(d, rows=1):
    return pl.BlockSpec((rows, d), lambda i: (0, 0))


def _modulate(x, scale, shift, name):
    s, d = x.shape
    tm = min(ROWS, s)

    def body(x_ref, sc_ref, sh_ref, o_ref):
        o_ref[...] = (x_ref[...] * (1.0 + sc_ref[...]) + sh_ref[...]).astype(BF16)

    return pl.pallas_call(
        body, name=name, grid=(s // tm,), in_specs=[_row_spec(tm, d), _vec_spec(d), _vec_spec(d)],
        out_specs=_row_spec(tm, d), out_shape=jax.ShapeDtypeStruct((s, d), BF16), compiler_params=_cp("parallel"),
    )(x, scale, shift)


def _resid_ln(x, y, gate, g, b, name):
    s, d = x.shape
    tm = min(ROWS, s)

    def body(x_ref, y_ref, gt_ref, g_ref, b_ref, o_ref):
        u = DEEPNORM_ALPHA * x_ref[...] + gt_ref[...] * y_ref[...]
        mu = jnp.mean(u, axis=1, keepdims=True)
        uc = u - mu
        var = jnp.mean(uc * uc, axis=1, keepdims=True)
        o_ref[...] = uc * lax.rsqrt(var + LN_EPS) * g_ref[...] + b_ref[...]

    return pl.pallas_call(
        body, name=name, grid=(s // tm,),
        in_specs=[_row_spec(tm, d), _row_spec(tm, d), _vec_spec(d), _vec_spec(d), _vec_spec(d)],
        out_specs=_row_spec(tm, d), out_shape=jax.ShapeDtypeStruct((s, d), F32), compiler_params=_cp("parallel"),
    )(x, y, gate, g, b)


def _resid_ln_bwd(x, y, dxn, gate, g, name):
    s, d = x.shape
    tm = min(ROWS, s)

    def body(x_ref, y_ref, dxn_ref, gt_ref, g_ref, du_ref, dy_ref, red_ref):
        @pl.when(pl.program_id(0) == 0)
        def _():
            red_ref[...] = jnp.zeros_like(red_ref)

        yv = y_ref[...]
        u = DEEPNORM_ALPHA * x_ref[...] + gt_ref[...] * yv
        mu = jnp.mean(u, axis=1, keepdims=True)
        uc = u - mu
        var = jnp.mean(uc * uc, axis=1, keepdims=True)
        rstd = lax.rsqrt(var + LN_EPS)
        xhat = uc * rstd
        dxnv = dxn_ref[...]
        dxh = dxnv * g_ref[...]
        du = rstd * (dxh - jnp.mean(dxh, axis=1, keepdims=True) - xhat * jnp.mean(dxh * xhat, axis=1, keepdims=True))
        du_ref[...] = du
        dy_ref[...] = (du * gt_ref[...]).astype(BF16)
        red_ref[0:1, :] += jnp.sum(du * yv, axis=0, keepdims=True)
        red_ref[1:2, :] += jnp.sum(dxnv * xhat, axis=0, keepdims=True)
        red_ref[2:3, :] += jnp.sum(dxnv, axis=0, keepdims=True)

    return pl.pallas_call(
        body, name=name, grid=(s // tm,),
        in_specs=[_row_spec(tm, d), _row_spec(tm, d), _row_spec(tm, d), _vec_spec(d), _vec_spec(d)],
        out_specs=[_row_spec(tm, d), _row_spec(tm, d), _vec_spec(d, 8)],
        out_shape=[jax.ShapeDtypeStruct((s, d), F32), jax.ShapeDtypeStruct((s, d), BF16), jax.ShapeDtypeStruct((8, d), F32)],
        compiler_params=_cp("arbitrary"),
    )(x, y, dxn, gate, g)


def _modulate_bwd(du, dhs, x, scale, name):
    s, d = x.shape
    tm = min(ROWS, s)
    n = len(dhs)

    def body(*refs):
        du_ref, dh_refs, (x_ref, sc_ref, dx_ref, red_ref) = refs[0], refs[1:1 + n], refs[1 + n:]

        @pl.when(pl.program_id(0) == 0)
        def _():
            red_ref[...] = jnp.zeros_like(red_ref)

        dhv = dh_refs[0][...]
        for t in dh_refs[1:]:
            dhv = dhv + t[...]
        dx_ref[...] = DEEPNORM_ALPHA * du_ref[...] + dhv * (1.0 + sc_ref[...])
        red_ref[0:1, :] += jnp.sum(dhv * x_ref[...], axis=0, keepdims=True)
        red_ref[1:2, :] += jnp.sum(dhv, axis=0, keepdims=True)

    return pl.pallas_call(
        body, name=name, grid=(s // tm,),
        in_specs=[_row_spec(tm, d)] * (n + 2) + [_vec_spec(d)],
        out_specs=[_row_spec(tm, d), _vec_spec(d, 8)],
        out_shape=[jax.ShapeDtypeStruct((s, d), F32), jax.ShapeDtypeStruct((8, d), F32)],
        compiler_params=_cp("arbitrary"),
    )(du, *dhs, x, scale)


def _loss_grad(xf, tgt, name):
    s, d = xf.shape
    tm = min(ROWS, s)

    def body(x_ref, t_ref, dx_ref, red_ref):
        @pl.when(pl.program_id(0) == 0)
        def _():
            red_ref[...] = jnp.zeros_like(red_ref)

        e = x_ref[...] - t_ref[...]
        dx_ref[...] = e * (1.0 / d)
        red_ref[0:1, :] += jnp.sum(e * e, axis=0, keepdims=True)

    return pl.pallas_call(
        body, name=name, grid=(s // tm,), in_specs=[_row_spec(tm, d), _row_spec(tm, d)],
        out_specs=[_row_spec(tm, d), _vec_spec(d, 8)],
        out_shape=[jax.ShapeDtypeStruct((s, d), F32), jax.ShapeDtypeStruct((8, d), F32)],
        compiler_params=_cp("arbitrary"),
    )(xf, tgt)


QKV_COLS = 3 * 3 * A_WIDTH


SLOPES = tuple(float(2.0 ** (-8.0 * (h + 1.0) / A_HEADS)) for h in range(A_HEADS))
FAR = 1e30
HEAD_COLS = tuple(slice(h * A_HEAD_DIM, (h + 1) * A_HEAD_DIM) for h in range(A_HEADS))


def _band_dist(n, length, dil, span_rows):
    shape = (2 * A_QBLOCK, A_QBLOCK) if span_rows else (A_QBLOCK, 2 * A_QBLOCK)
    r = lax.broadcasted_iota(jnp.int32, shape, 0)
    c = lax.broadcasted_iota(jnp.int32, shape, 1)
    sp, ce = (r, c) if span_rows else (c, r)
    delta = sp - A_RADIUS - ce
    pos = n * A_QBLOCK - A_RADIUS + sp
    valid = (jnp.abs(delta) <= A_RADIUS) & (pos >= 0) & (pos < length)
    return jnp.where(valid, jnp.abs(delta).astype(F32) * float(dil), FAR)


def _span_specs(col, nb64):
    def mk(i):
        return pl.BlockSpec((64, A_WIDTH), lambda r, n: (r * nb64 + jnp.clip(2 * n - 1 + i, 0, nb64 - 1), col))
    return [mk(i) for i in range(4)]


def _to_residue(t, dil):
    if dil == 1:
        return t
    s, c = t.shape
    return t.reshape(s // dil, dil, c).transpose(1, 0, 2).reshape(s, c)


def _from_residue(t, dil):
    if dil == 1:
        return t
    s, c = t.shape
    return t.reshape(dil, s // dil, c).transpose(1, 0, 2).reshape(s, c)


def _cat(refs):
    return jnp.concatenate([t[...] for t in refs], axis=0)


def _head_expander():
    r = lax.broadcasted_iota(jnp.int32, (A_HEADS, A_WIDTH), 0)
    c = lax.broadcasted_iota(jnp.int32, (A_HEADS, A_WIDTH), 1)
    return ((c >= r * A_HEAD_DIM) & (c < (r + 1) * A_HEAD_DIM)).astype(BF16)


def _to_lanes(x16, e):
    a, b, c = _split3(x16)
    return _dot(a, e) + _dot(b, e) + _dot(c, e)


def _per_head_sum(x, e):
    a, b, c = _split3(x)
    return _dot(a, e, 1, 1) + _dot(b, e, 1, 1) + _dot(c, e, 1, 1)


def _pair_low_lanes():
    return lax.broadcasted_iota(jnp.int32, (A_QBLOCK, LANE), 1) < A_HEAD_DIM


def _top_rows():
    return lax.broadcasted_iota(jnp.int32, (2 * A_QBLOCK, 1), 0) < A_QBLOCK


def _block_diag(v, low):
    zero = jnp.zeros_like(v)
    return jnp.concatenate([jnp.where(low, v, zero), jnp.where(low, zero, v)], axis=0)


def _attn_fwd(qkv, g, name):
    s = qkv.shape[0]
    dil = DILATIONS[g]
    length = s // dil
    nblk = length // A_QBLOCK

    def body(q_ref, k0, k1, k2, k3, v0, v1, v2, v3, o_ref, l_ref):
        dist = _band_dist(pl.program_id(1), length, dil, False)
        kk = _cat((k0, k1, k2, k3))
        vv = _cat((v0, v1, v2, v3))
        low = _pair_low_lanes()
        top = _top_rows()
        dist2 = jnp.concatenate([dist, dist], axis=0)
        for hp in range(A_HEADS // 2):
            ls = slice(hp * LANE, (hp + 1) * LANE)
            qp, kp, vp = q_ref[:, ls], kk[:, ls], vv[:, ls]
            sc = _dot(_block_diag(qp, low), kp, 1, 1) * 0.125 - jnp.where(top, SLOPES[2 * hp], SLOPES[2 * hp + 1]) * dist2
            m = jnp.max(sc, axis=1, keepdims=True)
            p = jnp.exp(sc - m)
            z = jnp.sum(p, axis=1, keepdims=True)
            o2 = _dot(p.astype(BF16), vp) / z
            lse2 = m + jnp.log(z)
            l_ref[:, 2 * hp:2 * hp + 1] = lse2[:A_QBLOCK]
            l_ref[:, 2 * hp + 1:2 * hp + 2] = lse2[A_QBLOCK:]
            o_ref[:, ls] = jnp.where(low, o2[:A_QBLOCK], o2[A_QBLOCK:])

    qspec = pl.BlockSpec((A_QBLOCK, A_WIDTH), lambda r, n: (r * nblk + n, 0))
    lspec = pl.BlockSpec((A_QBLOCK, A_HEADS), lambda r, n: (r * nblk + n, 0))
    return pl.pallas_call(
        body, name=name, grid=(dil, nblk), in_specs=[qspec] + _span_specs(1, 2 * nblk) + _span_specs(2, 2 * nblk),
        out_specs=[qspec, lspec],
        out_shape=[jax.ShapeDtypeStruct((s, A_WIDTH), F32), jax.ShapeDtypeStruct((s, A_HEADS), F32)],
        compiler_params=_cp("parallel", "parallel"),
    )(*([qkv] * 9))


def _attn_merge(os_, ls_, gate, name):
    s, w = gate.shape
    tm = min(ROWS, s)

    def body(o0, o1, o2, l0, l1, l2, g_ref, y_ref, o_ref, l_ref):
        a, b, c = l0[...], l1[...], l2[...]
        m = jnp.maximum(jnp.maximum(a, b), c)
        ea, eb, ec = jnp.exp(a - m), jnp.exp(b - m), jnp.exp(c - m)
        z = ea + eb + ec
        l_ref[...] = m + jnp.log(z)
        e = _head_expander()
        o = _to_lanes(ea / z, e) * o0[...] + _to_lanes(eb / z, e) * o1[...] + _to_lanes(ec / z, e) * o2[...]
        o_ref[...] = o
        y_ref[...] = (o * _silu(g_ref[...])).astype(BF16)

    rs = _row_spec(tm, w)
    ls = _row_spec(tm, A_HEADS)
    return pl.pallas_call(
        body, name=name, grid=(s // tm,), in_specs=[rs] * 3 + [ls] * 3 + [rs], out_specs=[rs, rs, ls],
        out_shape=[jax.ShapeDtypeStruct((s, w), BF16), jax.ShapeDtypeStruct((s, w), F32), jax.ShapeDtypeStruct((s, A_HEADS), F32)],
        compiler_params=_cp("parallel"),
    )(*os_, *ls_, gate)


def _attn_gate_bwd(dyy, o, gate, name):
    s, w = gate.shape
    tm = min(ROWS, s)

    def body(dy_ref, o_ref, g_ref, do_ref, dg_ref, dl_ref):
        dyv, ov, gv = dy_ref[...], o_ref[...], g_ref[...]
        do = dyv * _silu(gv)
        do_ref[...] = do.astype(BF16)
        dg_ref[...] = (dyv * ov * _dsilu(gv)).astype(BF16)
        dl_ref[...] = _per_head_sum(do * ov, _head_expander())

    rs = _row_spec(tm, w)
    return pl.pallas_call(
        body, name=name, grid=(s // tm,), in_specs=[rs] * 3, out_specs=[rs, rs, _row_spec(tm, A_HEADS)],
        out_shape=[jax.ShapeDtypeStruct((s, w), BF16), jax.ShapeDtypeStruct((s, w), BF16), jax.ShapeDtypeStruct((s, A_HEADS), F32)],
        compiler_params=_cp("parallel"),
    )(dyy, o, gate)


def _attn_bwd(qkv, do, lse, delta, g, name):
    s = qkv.shape[0]
    dil = DILATIONS[g]
    length = s // dil
    nblk = length // A_QBLOCK

    def rows(t16):
        return jnp.pad(t16.reshape(dil, length, A_HEADS).transpose(0, 2, 1), ((0, 0), (0, 0), (A_RADIUS, A_RADIUS)))

    def body(q0, q1, q2, q3, k0, k1, k2, k3, v0, v1, v2, v3, d0, d1, d2, d3, lc_ref, ec_ref, la, lb, ea, eb, o_ref):
        dist = _band_dist(pl.program_id(1), length, dil, False)
        qq, kk, vv, dd = _cat((q0, q1, q2, q3)), _cat((k0, k1, k2, k3)), _cat((v0, v1, v2, v3)), _cat((d0, d1, d2, d3))
        lse_r = jnp.concatenate([la[...], lb[...]], axis=1)
        dlt_r = jnp.concatenate([ea[...], eb[...]], axis=1)
        low = _pair_low_lanes()
        top = _top_rows()
        dist2 = jnp.concatenate([dist, dist], axis=0)
        centre = slice(A_RADIUS, A_RADIUS + A_QBLOCK)
        for hp in range(A_HEADS // 2):
            ls = slice(hp * LANE, (hp + 1) * LANE)
            qs, ks, vs, ds_ = qq[:, ls], kk[:, ls], vv[:, ls], dd[:, ls]
            qn, kn, vn, dn = qs[centre], ks[centre], vs[centre], ds_[centre]
            h0, h1 = 2 * hp, 2 * hp + 1
            bias = jnp.where(top, SLOPES[h0], SLOPES[h1]) * dist2
            lc = jnp.concatenate([lc_ref[:, h0:h0 + 1], lc_ref[:, h1:h1 + 1]], axis=0)
            ec = jnp.concatenate([ec_ref[:, h0:h0 + 1], ec_ref[:, h1:h1 + 1]], axis=0)
            lr = jnp.where(top, lse_r[h0:h0 + 1, :], lse_r[h1:h1 + 1, :])
            er = jnp.where(top, dlt_r[h0:h0 + 1, :], dlt_r[h1:h1 + 1, :])
            p = jnp.exp(_dot(_block_diag(qn, low), ks, 1, 1) * 0.125 - bias - lc)
            dsc = p * (_dot(_block_diag(dn, low), vs, 1, 1) - ec)
            dq = _dot(dsc.astype(BF16), ks)
            pt = jnp.exp(_dot(_block_diag(kn, low), qs, 1, 1) * 0.125 - bias - lr)
            dst = pt * (_dot(_block_diag(vn, low), ds_, 1, 1) - er)
            dk = _dot(dst.astype(BF16), qs)
            dv = _dot(pt.astype(BF16), ds_)
            merge = lambda t: jnp.where(low, t[:A_QBLOCK], t[A_QBLOCK:])
            o_ref[:, ls] = (merge(dq) * 0.125).astype(BF16)
            o_ref[:, A_WIDTH + hp * LANE:A_WIDTH + (hp + 1) * LANE] = (merge(dk) * 0.125).astype(BF16)
            o_ref[:, 2 * A_WIDTH + hp * LANE:2 * A_WIDTH + (hp + 1) * LANE] = merge(dv).astype(BF16)

    nb64 = 2 * nblk
    dspecs = _span_specs(0, nb64)
    cspec = pl.BlockSpec((A_QBLOCK, A_HEADS), lambda r, n: (r * nblk + n, 0))
    rspecs = [pl.BlockSpec((None, A_HEADS, A_QBLOCK), lambda r, n: (r, 0, n)), pl.BlockSpec((None, A_HEADS, A_QBLOCK), lambda r, n: (r, 0, n + 1))]
    lse_r, dlt_r = rows(lse), rows(delta)
    return pl.pallas_call(
        body, name=name, grid=(dil, nblk),
        in_specs=_span_specs(0, nb64) + _span_specs(1, nb64) + _span_specs(2, nb64) + dspecs + [cspec, cspec] + rspecs * 2,
        out_specs=pl.BlockSpec((A_QBLOCK, 3 * A_WIDTH), lambda r, n: (r * nblk + n, 0)),
        out_shape=jax.ShapeDtypeStruct((s, 3 * A_WIDTH), BF16),
        compiler_params=_cp("parallel", "parallel"),
    )(*([qkv] * 12), *([do] * 4), lse, delta, lse_r, lse_r, dlt_r, dlt_r)


def _attn_layer_fwd(h, w_qkv, w_gate, w_out, li):
    nm = lambda t: f"a{li}_{t}"
    gate = _mm(h, w_gate, name=nm("gate"))
    hs, qkvs, os_, ls_ = [], [], [], []
    for g, dil in enumerate(DILATIONS):
        hg = _to_residue(h, dil)
        qkv = _mm(hg, w_qkv[:, g * 3 * A_WIDTH:(g + 1) * 3 * A_WIDTH], out_dtype=BF16, name=nm(f"qkv{g}"))
        o, l = _attn_fwd(qkv, g, nm(f"attn{g}"))
        hs.append(hg)
        qkvs.append(qkv)
        os_.append(_from_residue(o, dil))
        ls_.append(_from_residue(l, dil))
    y, o, lse = _attn_merge(os_, ls_, gate, nm("merge"))
    out = _mm(y, w_out, name=nm("out"))
    return out, (hs, qkvs, gate, y, o, lse)


def _attn_layer_bwd(dy, h, saved, w_qkv, w_gate, w_out, li):
    nm = lambda t: f"a{li}_{t}"
    hs, qkvs, gate, y, o, lse = saved
    g_w_out = _mm(y, dy, ta=True, out_dtype=BF16, name=nm("dwout"))
    dyy = _mm(dy, w_out, tb=True, name=nm("dyy"))
    do, dgate, delta = _attn_gate_bwd(dyy, o, gate, nm("gatebwd"))
    dhs, dws = [], []
    for g, dil in enumerate(DILATIONS):
        dqkv = _attn_bwd(qkvs[g], _to_residue(do, dil), _to_residue(lse, dil), _to_residue(delta, dil), g, nm(f"attnbwd{g}"))
        wg = w_qkv[:, g * 3 * A_WIDTH:(g + 1) * 3 * A_WIDTH]
        dws.append(_mm(hs[g], dqkv, ta=True, out_dtype=BF16, name=nm(f"dwqkv{g}")))
        add = _mm(dgate, w_gate, tb=True, name=nm("dh_gate")) if g == 0 else None
        dhs.append(_from_residue(_mm(dqkv, wg, tb=True, add=add, name=nm(f"dh_qkv{g}")), dil))
    g_w_in = jnp.concatenate(dws + [_mm(h, dgate, ta=True, out_dtype=BF16, name=nm("dwgate"))], axis=1)
    return dhs, g_w_in, g_w_out


SSM_INNER = SSM_HEADS * SSM_HEAD_DIM
SSM_BC = SSM_GROUPS * SSM_STATE
SSM_CONV_DIM = SSM_INNER + 2 * SSM_BC
GW = SSM_REP * SSM_HEAD_DIM
T = SSM_CHUNK
HALO = 8


def _conv_specs(tm, tn, s, col=lambda j: j):
    nb8 = s // HALO
    cur = pl.BlockSpec((tm, tn), lambda j, i: (i, col(j)))
    prev = pl.BlockSpec((HALO, tn), lambda j, i: (jnp.maximum(i * (tm // HALO) - 1, 0), col(j)))
    nxt = pl.BlockSpec((HALO, tn), lambda j, i: (jnp.minimum((i + 1) * (tm // HALO), nb8 - 1), col(j)))
    return [prev, cur, nxt]


def _extend(prev_ref, cur_ref, nxt_ref, i, nrow):
    p = jnp.where(i == 0, 0.0, prev_ref[...])
    n = jnp.where(i == nrow - 1, 0.0, nxt_ref[...])
    return jnp.concatenate([p, cur_ref[...], n], axis=0)


def _shift_rows(ext, off, tm):
    rows = ext.shape[0]
    return pltpu.roll(ext, (-off) % rows, 0)[HALO:HALO + tm]


def _conv_fwd(xraw, w, b, name):
    s, cdim = xraw.shape
    tm, tn = min(256, s), 1024
    nrow = s // tm

    def body(p_ref, c_ref, n_ref, w_ref, b_ref, pre_ref, act_ref):
        ext = _extend(p_ref, c_ref, n_ref, pl.program_id(1), nrow)
        acc = jnp.broadcast_to(b_ref[...], (tm, tn))
        for k in range(SSM_CONV):
            acc = acc + w_ref[k:k + 1, :] * _shift_rows(ext, k - SSM_CONV // 2, tm)
        pre_ref[...] = acc
        act_ref[...] = _silu(acc)

    prev, cur, nxt = _conv_specs(tm, tn, s)
    return pl.pallas_call(
        body, name=name, grid=(cdim // tn, nrow),
        in_specs=[prev, cur, nxt, pl.BlockSpec((SSM_CONV, tn), lambda j, i: (0, j)), pl.BlockSpec((1, tn), lambda j, i: (0, j))],
        out_specs=[cur, cur], out_shape=[jax.ShapeDtypeStruct((s, cdim), F32)] * 2,
        compiler_params=_cp("parallel", "parallel"),
    )(xraw, xraw, xraw, w, b)


def _conv_bwd(dx, db, dc, pre, xraw, w, name):
    s, cdim = xraw.shape
    tm, tn = min(256, s), 1024
    nrow = s // tm
    nx = dx.shape[1] // tn

    def body(xp, xc, xn, bp, bc, bn, cp, cc, cn, pp, pc, pn, x_ref, w_ref, o_ref, red_ref):
        j, i = pl.program_id(0), pl.program_id(1)

        @pl.when(i == 0)
        def _():
            red_ref[...] = jnp.zeros_like(red_ref)

        bcext = jnp.concatenate([_extend(bp, bc, bn, i, nrow), _extend(cp, cc, cn, i, nrow)], axis=1)
        dact = jnp.where(j < nx, _extend(xp, xc, xn, i, nrow), bcext)
        dpre = dact * _dsilu(_extend(pp, pc, pn, i, nrow))
        xv = x_ref[...]
        acc = jnp.zeros((tm, tn), F32)
        for k in range(SSM_CONV):
            sk = _shift_rows(dpre, SSM_CONV // 2 - k, tm)
            acc = acc + w_ref[k:k + 1, :] * sk
            red_ref[k:k + 1, :] += jnp.sum(sk * xv, axis=0, keepdims=True)
        red_ref[SSM_CONV:SSM_CONV + 1, :] += jnp.sum(dpre[HALO:HALO + tm], axis=0, keepdims=True)
        o_ref[...] = acc.astype(BF16)

    half = tn // 2
    cur = pl.BlockSpec((tm, tn), lambda j, i: (i, j))
    return pl.pallas_call(
        body, name=name, grid=(cdim // tn, nrow),
        in_specs=_conv_specs(tm, tn, s, lambda j: jnp.minimum(j, nx - 1)) + _conv_specs(tm, half, s, lambda j: 0) * 2
        + _conv_specs(tm, tn, s) + [cur, pl.BlockSpec((SSM_CONV, tn), lambda j, i: (0, j))],
        out_specs=[cur, pl.BlockSpec((8, tn), lambda j, i: (0, j))],
        out_shape=[jax.ShapeDtypeStruct((s, cdim), BF16), jax.ShapeDtypeStruct((8, cdim), F32)],
        compiler_params=_cp("parallel", "arbitrary"),
    )(dx, dx, dx, db, db, db, dc, dc, dc, pre, pre, pre, xraw, w)


def _tri(lower):
    r = lax.broadcasted_iota(jnp.int32, (T, T), 0)
    c = lax.broadcasted_iota(jnp.int32, (T, T), 1)
    return (r >= c) if lower else (r <= c)


def _softplus(x):
    return jnp.maximum(x, 0.0) + jnp.log(1.0 + jnp.exp(-jnp.abs(x)))


def _dt_prep(dt_raw, bias, a_log, name):
    s = dt_raw.shape[0]
    nc = s // T

    def body(r_ref, b_ref, a_ref, dt_ref, cum_ref, cumt_ref):
        dt = _softplus(r_ref[...] + b_ref[...])
        da = dt * (-jnp.exp(a_ref[...]))
        pre = _dot_exact(_tri(True).astype(BF16), da)
        suf = _dot_exact(_tri(False).astype(BF16), da)
        lane = lax.broadcasted_iota(jnp.int32, (T, LANE), 1)
        cum = jnp.where(lane < SSM_HEADS, pre, suf)
        dt_ref[...] = dt
        cum_ref[...] = cum
        cumt_ref[...] = cum.T

    blk = pl.BlockSpec((T, LANE), lambda c: (c, 0))
    vec = pl.BlockSpec((1, LANE), lambda c: (0, 0))
    return pl.pallas_call(
        body, name=name, grid=(nc,), in_specs=[blk, vec, vec],
        out_specs=[blk, blk, pl.BlockSpec((None, LANE, T), lambda c: (c, 0, 0))],
        out_shape=[jax.ShapeDtypeStruct((s, LANE), F32), jax.ShapeDtypeStruct((s, LANE), F32), jax.ShapeDtypeStruct((nc, LANE, T), F32)],
        compiler_params=_cp("parallel"),
    )(dt_raw, bias, a_log)


def _by_group(t):
    s = t.shape[0]
    return t[:, :2 * SSM_HEADS].reshape(s, 2 * SSM_GROUPS, SSM_REP).transpose(1, 0, 2)


def _from_group(tf, tb):
    s = tf.shape[1]
    t = jnp.concatenate([tf, tb], axis=0).transpose(1, 0, 2).reshape(s, 2 * SSM_HEADS)
    return jnp.pad(t, ((0, 0), (0, LANE - 2 * SSM_HEADS)))


def _decay_mats(acol, arow, rev):
    after = _tri(not rev)
    return jnp.where(after, jnp.exp(jnp.where(after, acol - arow, 0.0)), 0.0)


PAIRS = SSM_REP // 2


def _low_lanes():
    return lax.broadcasted_iota(jnp.int32, (T, LANE), 1) < SSM_HEAD_DIM


CPS = 4
TB = CPS * T


def _scan_specs(rev, ci):
    nxb = SSM_INNER // LANE
    kofs = SSM_GROUPS if rev else 0
    return [
        pl.BlockSpec((TB, GW), lambda g, c: (ci(c), g)),
        pl.BlockSpec((TB, LANE), lambda g, c: (ci(c), nxb + g)),
        pl.BlockSpec((TB, LANE), lambda g, c: (ci(c), nxb + SSM_GROUPS + g)),
        pl.BlockSpec((None, TB, SSM_REP), lambda g, c: (kofs + g, ci(c), 0)),
        pl.BlockSpec((None, TB, SSM_REP), lambda g, c: (kofs + g, ci(c), 0)),
        pl.BlockSpec((None, CPS, SSM_REP, T), lambda g, c: (kofs + g, ci(c), 0, 0)),
    ]


def _chunk_rows(q):
    return pl.ds(q * T, T)


def _pair_lanes(ref, p, low):
    return jnp.where(low, ref[:, 2 * p:2 * p + 1], ref[:, 2 * p + 1:2 * p + 2])


def _ssd_scan(xbc, dtk, cumk, cumtk, rev, name):
    s = xbc.shape[0]
    nc = s // T
    nb = nc // CPS
    last = 0 if rev else T - 1
    ci = (lambda c: nb - 1 - c) if rev else (lambda c: c)

    def body(x_ref, b_ref, c_ref, dt_ref, cum_ref, cumt_ref, y_ref, st_ref, state):
        @pl.when(pl.program_id(1) == 0)
        def _():
            state[...] = jnp.zeros_like(state)

        for q in (reversed(range(CPS)) if rev else range(CPS)):
            rows = _chunk_rows(q)
            chunk(x_ref.at[rows], b_ref.at[rows], c_ref.at[rows], dt_ref.at[rows], cum_ref.at[rows], cumt_ref.at[q],
                  y_ref.at[rows], st_ref.at[q], state)

    def chunk(x_ref, b_ref, c_ref, dt_ref, cum_ref, cumt_ref, y_ref, st_ref, state):
        bm = b_ref[...]
        cm = c_ref[...].astype(BF16)
        cb = _dot(cm, bm.astype(BF16), 1, 1)
        bt = bm.T.astype(BF16)
        low = _low_lanes()
        for p in range(PAIRS):
            ls = slice(p * LANE, (p + 1) * LANE)
            acum = _pair_lanes(cum_ref, p, low)
            u = x_ref[:, ls] * _pair_lanes(dt_ref, p, low)
            tot = acum[last:last + 1, :]
            m = [(cb * _decay_mats(cum_ref[:, r:r + 1], cumt_ref[r:r + 1, :], rev)).astype(BF16) for r in (2 * p, 2 * p + 1)]
            st = state[p]
            st_ref[p] = st
            yd = _dot(jnp.concatenate(m, axis=1), _block_diag(u.astype(BF16), low))
            yo = jnp.exp(acum) * _dot(cm, st.astype(BF16))
            y_ref[:, ls] = yd + yo
            state[p] = jnp.exp(tot) * st + _dot(bt, (jnp.exp(tot - acum) * u).astype(BF16))

    return pl.pallas_call(
        body, name=name, grid=(SSM_GROUPS, nb), in_specs=_scan_specs(rev, ci),
        out_specs=[
            pl.BlockSpec((TB, GW), lambda g, c: (ci(c), g)),
            pl.BlockSpec((CPS, PAIRS, SSM_STATE, LANE), lambda g, c: (ci(c), g, 0, 0)),
        ],
        out_shape=[jax.ShapeDtypeStruct((s, SSM_INNER), F32), jax.ShapeDtypeStruct((nc, SSM_HEADS // 2, SSM_STATE, LANE), F32)],
        scratch_shapes=[pltpu.VMEM((PAIRS, SSM_STATE, LANE), F32)],
        compiler_params=_cp("parallel", "arbitrary"),
    )(xbc, xbc, xbc, dtk, cumk, cumtk)


def _ssd_scan_bwd(xbc, dtk, cumk, cumtk, dy, states, dvec, prev, rev, name):
    s = xbc.shape[0]
    nc = s // T
    nb = nc // CPS
    last = 0 if rev else T - 1
    ci = (lambda c: c) if rev else (lambda c: nb - 1 - c)
    has_prev = prev is not None
    n_in = 12 if has_prev else 9

    def body(*refs):
        ins, outs, scratch = refs[:n_in], refs[n_in:n_in + 5], refs[n_in + 5:]

        @pl.when(pl.program_id(1) == 0)
        def _():
            scratch[0][...] = jnp.zeros_like(scratch[0])

        for q in (range(CPS) if rev else reversed(range(CPS))):
            rows = _chunk_rows(q)
            cut = lambda t: t.at[rows]
            x_ref, b_ref, c_ref, dt_ref, cum_ref, cumt_ref, dy_ref, st_ref, dv_ref = ins[:9]
            sub = [cut(x_ref), cut(b_ref), cut(c_ref), cut(dt_ref), cut(cum_ref), cumt_ref.at[q], cut(dy_ref), st_ref.at[q], dv_ref]
            chunk(*sub, *[cut(t) for t in ins[9:]], *[cut(t) for t in outs], *scratch)

    def chunk(*refs):
        x_ref, b_ref, c_ref, dt_ref, cum_ref, cumt_ref, dy_ref, st_ref, dv_ref = refs[:9]
        refs = refs[9:]
        if has_prev:
            pdx, pdb, pdc = refs[:3]
            refs = refs[3:]
        dx_ref, db_ref, dc_ref, ddt_ref, dda_ref, dstate, rs_buf, in_buf, k_buf = refs
        rs_buf[...] = jnp.zeros_like(rs_buf)
        in_buf[...] = jnp.zeros_like(in_buf)
        k_buf[...] = jnp.zeros_like(k_buf)
        bm = b_ref[...].astype(BF16)
        cm = c_ref[...].astype(BF16)
        cbt = _dot(bm, cm, 1, 1)
        cb = _dot(cm, bm, 1, 1)
        ct = c_ref[...].T.astype(BF16)
        after = _tri(not rev)
        before = _tri(rev)
        from_k = before.astype(BF16)
        ri = lax.broadcasted_iota(jnp.int32, (T, T), 0)
        cj = lax.broadcasted_iota(jnp.int32, (T, T), 1)
        strictly_before = (cj > ri) if rev else (cj < ri)
        dcb = jnp.zeros((T, T), F32)
        dc_acc = jnp.zeros((T, SSM_STATE), F32)
        db_acc = jnp.zeros((T, SSM_STATE), F32)
        low = _low_lanes()
        ri2 = lax.broadcasted_iota(jnp.int32, (LANE, LANE), 0)
        cj2 = lax.broadcasted_iota(jnp.int32, (LANE, LANE), 1)
        halves = ((ri2 < SSM_HEAD_DIM) == (cj2 == 0)) & (cj2 < 2)
        halves = halves.astype(BF16)

        def head_sums(v):
            hi = v.astype(BF16)
            lo = (v - hi.astype(F32)).astype(BF16)
            return _dot(hi, halves) + _dot(lo, halves)

        for p in range(PAIRS):
            ls = slice(p * LANE, (p + 1) * LANE)
            c2 = slice(2 * p, 2 * p + 2)
            lm, lmt = [], []
            for r in (2 * p, 2 * p + 1):
                acol = cum_ref[:, r:r + 1]
                arow = cumt_ref[r:r + 1, :]
                lm.append(jnp.where(after, jnp.exp(jnp.where(after, acol - arow, 0.0)), 0.0))
                lmt.append(jnp.where(before, jnp.exp(jnp.where(before, arow - acol, 0.0)), 0.0))
            acum = _pair_lanes(cum_ref, p, low)
            tot = acum[last:last + 1, :]
            dtl = _pair_lanes(dt_ref, p, low)
            xl = x_ref[:, ls]
            u = xl * dtl
            ub = u.astype(BF16)
            dyl = dy_ref[:, ls]
            dyb = dyl.astype(BF16)
            st = st_ref[p]
            stb = st.astype(BF16)
            dst = dstate[p]
            dstb = dst.astype(BF16)
            dec = jnp.exp(tot - acum)
            eac = jnp.exp(acum)
            etot = jnp.exp(tot)
            du_off = dec * _dot(bm, dstb)
            mt = jnp.concatenate([(cbt * lmt[0]).astype(BF16), (cbt * lmt[1]).astype(BF16)], axis=1)
            du = _dot(mt, _block_diag(dyb, low)) + du_off
            zero = jnp.zeros_like(dyb)
            gl = [_dot(jnp.where(low, dyb, zero), ub, 1, 1) * lm[0], _dot(jnp.where(low, zero, dyb), ub, 1, 1) * lm[1]]
            dcb = dcb + gl[0] + gl[1]
            dc_acc = dc_acc + _dot((eac * dyl).astype(BF16), stb, 1, 1)
            db_acc = db_acc + _dot((dec * u).astype(BF16), dstb, 1, 1)
            w = jnp.concatenate([(gl[0] * cb).astype(BF16), (gl[1] * cb).astype(BF16)], axis=1)
            crossing = _dot(from_k, w)
            for j in range(2):
                cr = jnp.where(strictly_before, crossing[:, j * T:(j + 1) * T], 0.0)
                in_buf[:, 2 * p + j:2 * p + j + 1] = jnp.sum(cr, axis=1, keepdims=True)
            y_off = eac * _dot(cm, stb)
            udu = u * du_off
            rs_buf[:, c2] = head_sums(dyl * y_off - udu)[:, 0:2]
            col = jnp.sum(dst * (etot * st) + udu, axis=0, keepdims=True)
            k_buf[0:1, c2] = head_sums(jnp.broadcast_to(col, (8, LANE)))[0:1, 0:2]
            ddt_ref[:, c2] = head_sums(du * xl)[:, 0:2]
            dx = du * dtl
            if has_prev:
                dx = dx + pdx[:, ls]
            else:
                dx = dx + dyl * dv_ref[:, ls]
            dx_ref[:, ls] = dx
            dstate[p] = etot * dst + _dot(ct, (eac * dyl).astype(BF16))
        dda = in_buf[...] + _dot_exact(from_k, rs_buf[...]) + k_buf[0:1, :]
        dda_ref[...] = dda[:, :SSM_REP]
        dcbb = dcb.astype(BF16)
        dc = dc_acc + _dot(dcbb, bm)
        db = db_acc + _dot(dcbb, cm, 0, 0)
        if has_prev:
            dc = dc + pdc[...]
            db = db + pdb[...]
        dc_ref[...] = dc
        db_ref[...] = db

    xspec = pl.BlockSpec((TB, GW), lambda g, c: (ci(c), g))
    gspec = pl.BlockSpec((TB, LANE), lambda g, c: (ci(c), g))
    in_specs = _scan_specs(rev, ci) + [
        xspec,
        pl.BlockSpec((CPS, PAIRS, SSM_STATE, LANE), lambda g, c: (ci(c), g, 0, 0)),
        pl.BlockSpec((1, GW), lambda g, c: (0, g)),
    ]
    args = [xbc, xbc, xbc, dtk, cumk, cumtk, dy, states, dvec]
    if has_prev:
        in_specs += [xspec, gspec, gspec]
        args += list(prev)
    ospec8 = pl.BlockSpec((None, TB, SSM_REP), lambda g, c: (g, ci(c), 0))
    return pl.pallas_call(
        body, name=name, grid=(SSM_GROUPS, nb), in_specs=in_specs,
        out_specs=[xspec, gspec, gspec, ospec8, ospec8],
        out_shape=[jax.ShapeDtypeStruct((s, SSM_INNER), F32), jax.ShapeDtypeStruct((s, SSM_BC), F32), jax.ShapeDtypeStruct((s, SSM_BC), F32),
                   jax.ShapeDtypeStruct((SSM_GROUPS, s, SSM_REP), F32), jax.ShapeDtypeStruct((SSM_GROUPS, s, SSM_REP), F32)],
        scratch_shapes=[pltpu.VMEM((PAIRS, SSM_STATE, LANE), F32), pltpu.VMEM((T, LANE), F32), pltpu.VMEM((T, LANE), F32),
                        pltpu.VMEM((8, LANE), F32)],
        compiler_params=_cp("parallel", "arbitrary"),
    )(*args)


def _ssd_post(yf, yb, xbc, z, dvec, nw, name):
    s = z.shape[0]
    tm = min(256, s)

    def body(yf_ref, yb_ref, x_ref, z_ref, dv_ref, nw_ref, o_ref):
        ys = yf_ref[...] + yb_ref[...] + dv_ref[...] * x_ref[...]
        yg = ys * _silu(z_ref[...])
        ms = jnp.mean(yg * yg, axis=1, keepdims=True)
        o_ref[...] = (yg * lax.rsqrt(ms + RMS_EPS) * nw_ref[...]).astype(BF16)

    rs = _row_spec(tm, SSM_INNER)
    vs = _vec_spec(SSM_INNER)
    return pl.pallas_call(
        body, name=name, grid=(s // tm,), in_specs=[rs, rs, rs, rs, vs, vs], out_specs=rs,
        out_shape=jax.ShapeDtypeStruct((s, SSM_INNER), BF16), compiler_params=_cp("parallel"),
    )(yf, yb, xbc, z, dvec, nw)


def _ssd_post_bwd(dyn, yf, yb, xbc, z, dvec, nw, name):
    s = z.shape[0]
    tm = min(256, s)

    def body(dyn_ref, yf_ref, yb_ref, x_ref, z_ref, dv_ref, nw_ref, dys_ref, dz_ref, red_ref):
        @pl.when(pl.program_id(0) == 0)
        def _():
            red_ref[...] = jnp.zeros_like(red_ref)

        xv, zv = x_ref[...], z_ref[...]
        ys = yf_ref[...] + yb_ref[...] + dv_ref[...] * xv
        sz = _silu(zv)
        yg = ys * sz
        rstd = lax.rsqrt(jnp.mean(yg * yg, axis=1, keepdims=True) + RMS_EPS)
        yhat = yg * rstd
        dynv = dyn_ref[...]
        dyh = dynv * nw_ref[...]
        dyg = rstd * (dyh - yhat * jnp.mean(dyh * yhat, axis=1, keepdims=True))
        dys = dyg * sz
        dys_ref[...] = dys
        dz_ref[...] = (dyg * ys * _dsilu(zv)).astype(BF16)
        red_ref[0:1, :] += jnp.sum(dynv * yhat, axis=0, keepdims=True)
        red_ref[1:2, :] += jnp.sum(dys * xv, axis=0, keepdims=True)

    rs = _row_spec(tm, SSM_INNER)
    vs = _vec_spec(SSM_INNER)
    return pl.pallas_call(
        body, name=name, grid=(s // tm,), in_specs=[rs, rs, rs, rs, rs, vs, vs],
        out_specs=[rs, rs, _vec_spec(SSM_INNER, 8)],
        out_shape=[jax.ShapeDtypeStruct((s, SSM_INNER), F32), jax.ShapeDtypeStruct((s, SSM_INNER), BF16), jax.ShapeDtypeStruct((8, SSM_INNER), F32)],
        compiler_params=_cp("arbitrary"),
    )(dyn, yf, yb, xbc, z, dvec, nw)


def _dt_bwd(dt_raw, bias, a_log, dt, ddt, dda, name):
    s = dt_raw.shape[0]
    tm = min(1024, s)

    def body(r_ref, b_ref, a_ref, dt_ref, ddt_ref, dda_ref, o_ref, red_ref):
        @pl.when(pl.program_id(0) == 0)
        def _():
            red_ref[...] = jnp.zeros_like(red_ref)

        a = -jnp.exp(a_ref[...])
        ddav = dda_ref[...]
        draw = (ddt_ref[...] + a * ddav) * _sigmoid(r_ref[...] + b_ref[...])
        o_ref[...] = draw.astype(BF16)
        red_ref[0:1, :] += jnp.sum(draw, axis=0, keepdims=True)
        red_ref[1:2, :] += a * jnp.sum(ddav * dt_ref[...], axis=0, keepdims=True)

    rs = _row_spec(tm, LANE)
    vs = _vec_spec(LANE)
    return pl.pallas_call(
        body, name=name, grid=(s // tm,), in_specs=[rs, vs, vs, rs, rs, rs], out_specs=[rs, _vec_spec(LANE, 8)],
        out_shape=[jax.ShapeDtypeStruct((s, LANE), BF16), jax.ShapeDtypeStruct((8, LANE), F32)],
        compiler_params=_cp("arbitrary"),
    )(dt_raw, bias, a_log, dt, ddt, dda)


def _pad_lanes(v):
    v = v.reshape(1, -1)
    return jnp.pad(v, ((0, 0), (0, LANE - v.shape[1])))


def _ssd_prep_weights(w_in, conv_w, conv_b, dt_bias, a_log, d_skip, norm_w, w_out):
    return dict(
        w_z=w_in[:, :SSM_INNER].astype(BF16),
        w_xbc=w_in[:, SSM_INNER:SSM_INNER + SSM_CONV_DIM].astype(BF16),
        w_dt=jnp.pad(w_in[:, SSM_INNER + SSM_CONV_DIM:], ((0, 0), (0, LANE - 2 * SSM_HEADS))).astype(BF16),
        conv_w=conv_w, conv_b=conv_b.reshape(1, -1), bias=_pad_lanes(dt_bias), a_log=_pad_lanes(a_log),
        dvec=jnp.repeat(d_skip, SSM_HEAD_DIM).reshape(1, -1), nw=norm_w.reshape(1, -1), w_out=w_out.astype(BF16),
    )


def _ssd_layer_fwd(h, w, li):
    nm = lambda t: f"b{li}_{t}"
    z = _mm(h, w["w_z"], name=nm("z"))
    xraw = _mm(h, w["w_xbc"], name=nm("xbc"))
    dt_raw = _mm(h, w["w_dt"], name=nm("dt"))
    pre, xbc = _conv_fwd(xraw, w["conv_w"], w["conv_b"], nm("conv"))
    dt, cum, cumt = _dt_prep(dt_raw, w["bias"], w["a_log"], nm("dtprep"))
    nc = cumt.shape[0]
    dtk, cumk = _by_group(dt), _by_group(cum)
    cumtk = cumt[:, :2 * SSM_HEADS].reshape(nc, 2 * SSM_GROUPS, SSM_REP, T).transpose(1, 0, 2, 3)
    yf, stf = _ssd_scan(xbc, dtk, cumk, cumtk, False, nm("scan_f"))
    yb, stb = _ssd_scan(xbc, dtk, cumk, cumtk, True, nm("scan_b"))
    yn = _ssd_post(yf, yb, xbc, z, w["dvec"], w["nw"], nm("post"))
    out = _mm(yn, w["w_out"], name=nm("out"))
    return out, (z, xraw, dt_raw, pre, xbc, dt, dtk, cumk, cumtk, yf, stf, yb, stb, yn)


def _ssd_layer_bwd(dy, h, saved, w, li):
    nm = lambda t: f"b{li}_{t}"
    z, xraw, dt_raw, pre, xbc, dt, dtk, cumk, cumtk, yf, stf, yb, stb, yn = saved
    g_w_out = _mm(yn, dy, ta=True, out_dtype=BF16, name=nm("dwout"))
    dyn = _mm(dy, w["w_out"], tb=True, name=nm("dyn"))
    dys, dz, pred = _ssd_post_bwd(dyn, yf, yb, xbc, z, w["dvec"], w["nw"], nm("postbwd"))
    dx1, db1, dc1, ddt_f, dda_f = _ssd_scan_bwd(xbc, dtk, cumk, cumtk, dys, stf, w["dvec"], None, False, nm("scanbwd_f"))
    dx, db, dc, ddt_b, dda_b = _ssd_scan_bwd(xbc, dtk, cumk, cumtk, dys, stb, w["dvec"], (dx1, db1, dc1), True, nm("scanbwd_b"))
    dxraw, cred = _conv_bwd(dx, db, dc, pre, xraw, w["conv_w"], nm("convbwd"))
    draw, dred = _dt_bwd(dt_raw, w["bias"], w["a_log"], dt, _from_group(ddt_f, ddt_b), _from_group(dda_f, dda_b), nm("dtbwd"))
    dh = _mm(dz, w["w_z"], tb=True, name=nm("dh_z"))
    dh = _mm(dxraw, w["w_xbc"], tb=True, add=dh, name=nm("dh_xbc"))
    dh = _mm(draw, w["w_dt"], tb=True, add=dh, name=nm("dh_dt"))
    g_w_in = jnp.concatenate([_mm(h, dz, ta=True, out_dtype=BF16, name=nm("dwz")), _mm(h, dxraw, ta=True, out_dtype=BF16, name=nm("dwxbc")),
                              _mm(h, draw, ta=True, out_dtype=BF16, name=nm("dwdt"))[:, :2 * SSM_HEADS]], axis=1)
    grads = (g_w_in, cred[:SSM_CONV], cred[SSM_CONV], dred[0, :2 * SSM_HEADS].reshape(2, SSM_HEADS),
             dred[1, :2 * SSM_HEADS].reshape(2, SSM_HEADS), pred[1].reshape(SSM_HEADS, SSM_HEAD_DIM).sum(axis=1), pred[0], g_w_out)
    return dh, grads


B_GRAD_NAMES = ("b_w_in", "b_conv_w", "b_conv_b", "b_dt_bias", "b_a_log", "b_d", "b_norm_w", "b_w_out")


def _local_step(x, tgt, mod, w):
    d = x.shape[1]
    qkv_cols = QKV_COLS
    layers = []
    for i in range(DEPTH):
        j = i // 2
        if i % 2 == 0:
            layers.append((w["a_w_in"][j][:, :qkv_cols].astype(BF16), w["a_w_in"][j][:, qkv_cols:].astype(BF16), w["a_w_out"][j].astype(BF16)))
        else:
            layers.append(_ssd_prep_weights(w["b_w_in"][j], w["b_conv_w"][j], w["b_conv_b"][j], w["b_dt_bias"][j], w["b_a_log"][j],
                                            w["b_d"][j], w["b_norm_w"][j], w["b_w_out"][j]))
    saved = []
    for i in range(DEPTH):
        shift, scale, gate = mod[i:i + 1, :d], mod[i:i + 1, d:2 * d], mod[i:i + 1, 2 * d:]
        h = _modulate(x, scale, shift, f"l{i}_mod")
        if i % 2 == 0:
            out, sv = _attn_layer_fwd(h, *layers[i], i)
        else:
            out, sv = _ssd_layer_fwd(h, layers[i], i)
        xn = _resid_ln(x, out, gate, w["ln_g"][i:i + 1], w["ln_b"][i:i + 1], f"l{i}_ln")
        saved.append((x, h, out, sv))
        x = xn
    dx, lred = _loss_grad(x, tgt, "loss")
    loss = 0.5 * jnp.sum(lred[0]) / d
    dmod, g_ln_g, g_ln_b = [None] * DEPTH, [None] * DEPTH, [None] * DEPTH
    ga_in, ga_out = [None, None], [None, None]
    gb = [None, None]
    for i in reversed(range(DEPTH)):
        j = i // 2
        xi, h, out, sv = saved[i]
        scale, gate = mod[i:i + 1, d:2 * d], mod[i:i + 1, 2 * d:]
        du, dy, red = _resid_ln_bwd(xi, out, dx, gate, w["ln_g"][i:i + 1], f"l{i}_lnbwd")
        g_ln_g[i], g_ln_b[i] = red[1], red[2]
        if i % 2 == 0:
            dhs, ga_in[j], ga_out[j] = _attn_layer_bwd(dy, h, sv, *layers[i], i)
        else:
            dh, gb[j] = _ssd_layer_bwd(dy, h, sv, layers[i], i)
            dhs = [dh]
        dx, red2 = _modulate_bwd(du, dhs, xi, scale, f"l{i}_modbwd")
        dmod[i] = jnp.concatenate([red2[1], red2[0], red[0]])
    grads = {"ln_g": jnp.stack(g_ln_g), "ln_b": jnp.stack(g_ln_b), "a_w_in": jnp.stack(ga_in), "a_w_out": jnp.stack(ga_out)}
    for k, n in enumerate(B_GRAD_NAMES):
        grads[n] = jnp.stack([gb[0][k], gb[1][k]])
    return loss, dx, jnp.stack(dmod), grads


MESH = pl.DeviceIdType.MESH
ANY = pl.BlockSpec(memory_space=pl.ANY)
N_DEV = 8
N_SHARD = 4


def _flip(v, bit):
    return 1 - v if bit else v


def _all_gather8(v, name):
    def body(v_ref, o_ref, send_sems, recv_sems, local_sem):
        x, y, c = lax.axis_index("x"), lax.axis_index("y"), lax.axis_index("c")
        me = 4 * x + 2 * y + c
        local = pltpu.make_async_copy(v_ref, o_ref.at[me], local_sem)
        local.start()
        copies = []
        for k in range(1, N_DEV):
            peer = (_flip(x, k & 4), _flip(y, k & 2), _flip(c, k & 1))
            copies.append(pltpu.make_async_remote_copy(
                src_ref=v_ref, dst_ref=o_ref.at[me], send_sem=send_sems.at[k - 1], recv_sem=recv_sems.at[k - 1],
                device_id=peer, device_id_type=MESH))
        for cp in copies:
            cp.start()
        for cp in copies:
            cp.wait()
        local.wait()

    return pl.pallas_call(
        body, name=name, in_specs=[ANY], out_specs=ANY, out_shape=jax.ShapeDtypeStruct((N_DEV,) + v.shape, v.dtype),
        scratch_shapes=[pltpu.SemaphoreType.DMA((N_DEV - 1,)), pltpu.SemaphoreType.DMA((N_DEV - 1,)), pltpu.SemaphoreType.DMA],
    )(v)


def _transpose_shards(srcs, name):
    n = len(srcs)
    n_rem = (N_SHARD - 1) * n

    def body(*refs):
        s_refs, o_refs = refs[:n], refs[n:2 * n]
        send_sems, recv_sems, local_sems = refs[2 * n:]
        x, y, c = lax.axis_index("x"), lax.axis_index("y"), lax.axis_index("c")
        m = 2 * x + y
        local = [pltpu.make_async_copy(s_refs[a].at[m], o_refs[a].at[m], local_sems.at[a]) for a in range(n)]
        remote = []
        for k in range(1, N_SHARD):
            px, py = _flip(x, k & 2), _flip(y, k & 1)
            for a in range(n):
                i = (k - 1) * n + a
                remote.append(pltpu.make_async_remote_copy(
                    src_ref=s_refs[a].at[2 * px + py], dst_ref=o_refs[a].at[m], send_sem=send_sems.at[i], recv_sem=recv_sems.at[i],
                    device_id=(px, py, c), device_id_type=MESH))
        for cp in local + remote:
            cp.start()
        for cp in remote + local:
            cp.wait()

    return pl.pallas_call(
        body, name=name, in_specs=[ANY] * n, out_specs=[ANY] * n, out_shape=[jax.ShapeDtypeStruct(s.shape, s.dtype) for s in srcs],
        scratch_shapes=[pltpu.SemaphoreType.DMA((n_rem,)), pltpu.SemaphoreType.DMA((n_rem,)), pltpu.SemaphoreType.DMA((n,))],
    )(*srcs)


def _gather_shards(src, name):
    rows = src.shape[0]
    half = rows // 2
    n_ici = N_SHARD - 1

    def body(s_ref, o_ref, send_sems, recv_sems, local_sem):
        x, y, c = lax.axis_index("x"), lax.axis_index("y"), lax.axis_index("c")
        m = 2 * x + y
        sibling = (x, y, 1 - c)
        my_half = pl.ds(pl.multiple_of(c * half, 16), half)
        its_half = pl.ds(pl.multiple_of((1 - c) * half, 16), half)
        local = pltpu.make_async_copy(s_ref, o_ref.at[m], local_sem)
        local.start()
        chips = [(_flip(x, k & 2), _flip(y, k & 1)) for k in range(1, N_SHARD)]

        def copy(sem, src_ref, dst_ref, to):
            return pltpu.make_async_remote_copy(src_ref=src_ref, dst_ref=dst_ref, send_sem=send_sems.at[sem],
                                                recv_sem=recv_sems.at[sem], device_id=to, device_id_type=MESH)

        first = [copy(i, s_ref.at[my_half], o_ref.at[m, my_half], (px, py, c)) for i, (px, py) in enumerate(chips)]
        for cp in first:
            cp.start()
        passed = []
        for i, (px, py) in enumerate(chips):
            landed = o_ref.at[2 * px + py, my_half]
            copy(i, landed, landed, (px, py, c)).wait_recv()
            passed.append(copy(n_ici + i, landed, landed, sibling))
            passed[-1].start()
        for i, (px, py) in enumerate(chips):
            from_sibling = o_ref.at[2 * px + py, its_half]
            copy(n_ici + i, from_sibling, from_sibling, sibling).wait_recv()
        for cp in first + passed:
            cp.wait_send()
        local.wait()

    return pl.pallas_call(
        body, name=name, in_specs=[ANY], out_specs=ANY, out_shape=jax.ShapeDtypeStruct((N_SHARD,) + src.shape, src.dtype),
        scratch_shapes=[pltpu.SemaphoreType.DMA((2 * n_ici,)), pltpu.SemaphoreType.DMA((2 * n_ici,)), pltpu.SemaphoreType.DMA],
    )(src)


def _swap_sibling(vs, name):
    n = len(vs)

    def body(*refs):
        v_refs, o_refs, (send_sems, recv_sems) = refs[:n], refs[n:2 * n], refs[2 * n:]
        x, y, c = lax.axis_index("x"), lax.axis_index("y"), lax.axis_index("c")
        copies = [pltpu.make_async_remote_copy(src_ref=v_refs[a], dst_ref=o_refs[a], send_sem=send_sems.at[a], recv_sem=recv_sems.at[a],
                                               device_id=(x, y, 1 - c), device_id_type=MESH) for a in range(n)]
        for cp in copies:
            cp.start()
        for cp in copies:
            cp.wait()

    return pl.pallas_call(
        body, name=name, in_specs=[ANY] * n, out_specs=[ANY] * n, out_shape=[jax.ShapeDtypeStruct(v.shape, v.dtype) for v in vs],
        scratch_shapes=[pltpu.SemaphoreType.DMA((n,)), pltpu.SemaphoreType.DMA((n,))],
    )(*vs)


def _row_tile(r, elems, step):
    ok = [t for t in range(step, r + 1, step) if r % t == 0 and t <= elems]
    return max(ok) if ok else r


def _sum_slots(a, name):
    n, r, cdim = a.shape
    tm = _row_tile(r, (4 << 20) // (cdim * 4 * (n + 1)), 16)

    def body(a_ref, o_ref):
        acc = a_ref[0].astype(F32)
        for k in range(1, n):
            acc = acc + a_ref[k].astype(F32)
        o_ref[...] = acc

    return pl.pallas_call(
        body, name=name, grid=(r // tm,), in_specs=[pl.BlockSpec((n, tm, cdim), lambda i: (0, i, 0))],
        out_specs=pl.BlockSpec((tm, cdim), lambda i: (i, 0)), out_shape=jax.ShapeDtypeStruct((r, cdim), F32),
        compiler_params=_cp("parallel"),
    )(a)


def _silu_rows(v, name):
    def body(v_ref, o_ref):
        o_ref[...] = _silu(v_ref[...])

    return pl.pallas_call(body, name=name, out_shape=jax.ShapeDtypeStruct(v.shape, F32))(v)


PACK_COLS = 1024


def _adamw(w, gs, m, v, name):
    r, cdim = w.shape
    tm = _row_tile(r, (1 << 18) // cdim, 8)
    c1 = 1.0 / (1.0 - ADAM_B1 ** ADAM_STEP)
    c2 = 1.0 / (1.0 - ADAM_B2 ** ADAM_STEP)
    ng = len(gs)

    def body(*refs):
        w_ref, g_refs, (m_ref, v_ref, g_ref, d_ref, nm_ref, nv_ref) = refs[0], refs[1:1 + ng], refs[1 + ng:]
        g = g_refs[0][...]
        for t in g_refs[1:]:
            g = g + t[...]
        mn = ADAM_B1 * m_ref[...] + (1.0 - ADAM_B1) * g
        vn = ADAM_B2 * v_ref[...] + (1.0 - ADAM_B2) * (g * g)
        g_ref[...] = g
        nm_ref[...] = mn
        nv_ref[...] = vn
        d_ref[...] = -ADAM_LR * ((mn * c1) / (jnp.sqrt(vn * c2) + ADAM_EPS) + ADAM_WD * w_ref[...])

    spec = pl.BlockSpec((tm, cdim), lambda i: (i, 0))
    return pl.pallas_call(
        body, name=name, grid=(r // tm,), in_specs=[spec] * (3 + ng), out_specs=[spec] * 4,
        out_shape=[jax.ShapeDtypeStruct(w.shape, F32)] * 4, compiler_params=_cp("parallel"),
    )(w, *gs, m, v)


def _rows(a):
    f = a.reshape(-1)
    pad = (-f.shape[0]) % PACK_COLS
    if pad:
        f = jnp.pad(f, (0, pad))
    return f.reshape(-1, PACK_COLS)


def _nrows(shape):
    return -(-int(np.prod(shape)) // PACK_COLS)


def _pack(parts, total_rows=None):
    p = jnp.concatenate([_rows(a) for a in parts], axis=0)
    if total_rows is not None and total_rows > p.shape[0]:
        p = jnp.pad(p, ((0, total_rows - p.shape[0]), (0, 0)))
    return p


def _unpack(p, shapes):
    out, r0 = [], 0
    for shp in shapes:
        n = int(np.prod(shp))
        nr = _nrows(shp)
        out.append(p[r0:r0 + nr].reshape(-1)[:n].reshape(shp))
        r0 += nr
    return out


def _unshard_cols(g):
    return jnp.concatenate([g[k] for k in range(N_SHARD)], axis=-1)


def _shard_cols(a):
    n = a.shape[-1] // N_SHARD
    return jnp.stack([a[..., k * n:(k + 1) * n] for k in range(N_SHARD)])


def _unshard_rows(g):
    return jnp.concatenate([g[k] for k in range(N_SHARD)], axis=1)


def _shard_rows(a):
    n = a.shape[1] // N_SHARD
    return jnp.stack([a[:, k * n:(k + 1) * n] for k in range(N_SHARD)])


W_NAMES = ("ada_w", "ada_b", "ln_g", "ln_b", "a_w_in", "a_w_out", "b_w_in", "b_conv_w", "b_conv_b", "b_dt_bias", "b_a_log", "b_d",
           "b_norm_w", "b_w_out")
BIG = ("a_w_in", "a_w_out", "b_w_in", "b_w_out")
SMALL = ("ada_b", "ln_g", "ln_b", "b_conv_w", "b_conv_b", "b_dt_bias", "b_a_log", "b_d", "b_norm_w")


def kernel(x, c, ada_w, ada_b, ln_g, ln_b, a_w_in, a_w_out, b_w_in, b_conv_w, b_conv_b, b_dt_bias, b_a_log, b_d, b_norm_w, b_w_out, loss_target, m_ada_w, m_ada_b, m_ln_g, m_ln_b, m_a_w_in, m_a_w_out, m_b_w_in, m_b_conv_w, m_b_conv_b, m_b_dt_bias, m_b_a_log, m_b_d, m_b_norm_w, m_b_w_out, v_ada_w, v_ada_b, v_ln_g, v_ln_b, v_a_w_in, v_a_w_out, v_b_w_in, v_b_conv_w, v_b_conv_b, v_b_dt_bias, v_b_a_log, v_b_d, v_b_norm_w, v_b_w_out):
    w = dict(ada_w=ada_w, ada_b=ada_b, ln_g=ln_g, ln_b=ln_b, a_w_in=a_w_in, a_w_out=a_w_out, b_w_in=b_w_in, b_conv_w=b_conv_w,
             b_conv_b=b_conv_b, b_dt_bias=b_dt_bias, b_a_log=b_a_log, b_d=b_d, b_norm_w=b_norm_w, b_w_out=b_w_out)
    mom = dict(ada_w=m_ada_w, ada_b=m_ada_b, ln_g=m_ln_g, ln_b=m_ln_b, a_w_in=m_a_w_in, a_w_out=m_a_w_out, b_w_in=m_b_w_in,
               b_conv_w=m_b_conv_w, b_conv_b=m_b_conv_b, b_dt_bias=m_b_dt_bias, b_a_log=m_b_a_log, b_d=m_b_d, b_norm_w=m_b_norm_w,
               b_w_out=m_b_w_out)
    var = dict(ada_w=v_ada_w, ada_b=v_ada_b, ln_g=v_ln_g, ln_b=v_ln_b, a_w_in=v_a_w_in, a_w_out=v_a_w_out, b_w_in=v_b_w_in,
               b_conv_w=v_b_conv_w, b_conv_b=v_b_conv_b, b_dt_bias=v_b_dt_bias, b_a_log=v_b_a_log, b_d=v_b_d, b_norm_w=v_b_norm_w,
               b_w_out=v_b_w_out)
    ax, ay, ac = lax.axis_index("x"), lax.axis_index("y"), lax.axis_index("c")
    me = 4 * ax + 2 * ay + ac
    shard = 2 * ax + ay
    d = x.shape[-1]
    dsh = ada_w.shape[-1]

    small_in = (c, b_conv_w, b_conv_b, b_norm_w)
    g0 = _all_gather8(_pack(small_in).reshape(-1, LANE), "gather_small_in").reshape(N_DEV, -1, PACK_COLS)
    per_dev = [_unpack(g0[k], [a.shape for a in small_in]) for k in range(N_DEV)]
    c_all = jnp.concatenate([p[0] for p in per_dev], axis=0)
    conv_w_full, conv_b_full, norm_w_full = (_unshard_cols([per_dev[2 * k][t] for k in range(N_SHARD)]) for t in (1, 2, 3))

    cond = _silu_rows(jnp.pad(c_all, ((0, 8), (0, 0))), "cond")
    bias = lax.dynamic_slice_in_dim(ada_b, shard * dsh, dsh, axis=1)
    part = jnp.stack([_mm(cond, ada_w[i], add=jnp.broadcast_to(bias[i], (16, dsh)), name=f"mod{i}")[:N_DEV] for i in range(DEPTH)])
    g1 = _all_gather8(part.reshape(-1, LANE), "gather_mod").reshape(N_DEV, DEPTH, N_DEV, dsh)
    mod_all = _unshard_cols([g1[2 * k] for k in range(N_SHARD)])
    mod = lax.dynamic_index_in_dim(mod_all, me, axis=1, keepdims=False)

    gw = _gather_shards(_pack([w[n] for n in BIG]).astype(BF16), "gather_weights")
    big_sh = [_unpack(gw[k], [w[n].shape for n in BIG]) for k in range(N_SHARD)]
    full = dict(
        ln_g=ln_g, ln_b=ln_b, b_dt_bias=b_dt_bias, b_a_log=b_a_log, b_d=b_d,
        b_conv_w=conv_w_full, b_conv_b=conv_b_full, b_norm_w=norm_w_full,
        a_w_in=_unshard_cols([s[0] for s in big_sh]), a_w_out=_unshard_rows([s[1] for s in big_sh]),
        b_w_in=_unshard_cols([s[2] for s in big_sh]), b_w_out=_unshard_rows([s[3] for s in big_sh]),
    )

    loss, grad_x, dmod, g = _local_step(x[0], loss_target[0], mod, full)

    gsh = (_shard_cols(g["a_w_in"]), _shard_rows(g["a_w_out"]), _shard_cols(g["b_w_in"]), _shard_rows(g["b_w_out"]))
    to_send = [t.reshape(N_SHARD, -1, t.shape[-1]) for t in gsh]
    arrived = _transpose_shards(to_send, "scatter_grads")
    mine = [_sum_slots(t, f"sum_{n}") for n, t in zip(BIG, arrived)]
    theirs = _swap_sibling(mine, "swap_grads")

    small_g = (dmod, g["ln_g"], g["ln_b"], g["b_dt_bias"], g["b_a_log"], g["b_d"], g["b_conv_w"], g["b_conv_b"], g["b_norm_w"],
               loss.reshape(1))
    g2 = _all_gather8(_pack(small_g).reshape(-1, LANE), "gather_small_grads")
    tot = _unpack(_sum_slots(g2, "sum_small").reshape(-1, PACK_COLS), [a.shape for a in small_g])
    g_ada_b, g_ln_g, g_ln_b, g_dt_bias, g_a_log, g_d, g_conv_w, g_conv_b, g_norm_w, loss_sum = tot
    dmod_all = g2.reshape(N_DEV, -1)[:, :dmod.size].reshape(N_DEV, DEPTH, 3 * d)
    dmod_mine = lax.dynamic_slice_in_dim(dmod_all, shard * dsh, dsh, axis=2)
    g_ada_w = jnp.stack([_mm(cond, jnp.pad(dmod_mine[:, i], ((0, 8), (0, 0))), ta=True, name=f"dada{i}") for i in range(DEPTH)])
    csh = g_conv_w.shape[-1] // N_SHARD
    nsh = g_norm_w.shape[-1] // N_SHARD
    small_grads = dict(
        ada_w=g_ada_w, ada_b=g_ada_b, ln_g=g_ln_g, ln_b=g_ln_b, b_dt_bias=g_dt_bias, b_a_log=g_a_log, b_d=g_d,
        b_conv_w=lax.dynamic_slice_in_dim(g_conv_w, shard * csh, csh, axis=2),
        b_conv_b=lax.dynamic_slice_in_dim(g_conv_b, shard * csh, csh, axis=1),
        b_norm_w=lax.dynamic_slice_in_dim(g_norm_w, shard * nsh, nsh, axis=1),
    )

    by_name = [{}, {}, {}, {}]

    def update(n, gs):
        two_d = lambda t: t.reshape(-1, t.shape[-1])
        outs = _adamw(two_d(w[n]), [two_d(t) for t in gs], two_d(mom[n]), two_d(var[n]), f"adamw_{n}")
        for t, o in zip(by_name, outs):
            t[n] = o.reshape(w[n].shape)

    for i, n in enumerate(BIG):
        update(n, [mine[i], theirs[i]])
    update("ada_w", [small_grads["ada_w"]])
    rest = SMALL
    rows = -(-sum(_nrows(w[n].shape) for n in rest) // 8) * 8
    packed = _adamw(_pack([w[n] for n in rest], rows), [_pack([small_grads[n] for n in rest], rows)],
                    _pack([mom[n] for n in rest], rows), _pack([var[n] for n in rest], rows), "adamw_small")
    for t, p in zip(by_name, packed):
        t.update(zip(rest, _unpack(p, [w[n].shape for n in rest])))
    return (loss_sum.reshape(()), grad_x[None], *[t[n] for t in by_name for n in W_NAMES])
```

```python
import jax
import jax.numpy as jnp
import numpy as np
from jax import lax
from jax.experimental import pallas as pl
from jax.experimental.pallas import tpu as pltpu

F32 = jnp.float32
BF16 = jnp.bfloat16

DEPTH = 4
A_HEADS = 16
A_HEAD_DIM = 64
A_WIDTH = A_HEADS * A_HEAD_DIM
DILATIONS = (1, 4, 16)
A_RADIUS = 64
A_QBLOCK = 128
SSM_HEADS = 32
SSM_HEAD_DIM = 64
SSM_STATE = 128
SSM_GROUPS = 4
SSM_REP = SSM_HEADS // SSM_GROUPS
SSM_CONV = 5
SSM_CHUNK = 128
DEEPNORM_ALPHA = (2 * DEPTH) ** 0.25
LN_EPS = 1e-5
RMS_EPS = 1e-5
ADAM_LR, ADAM_B1, ADAM_B2, ADAM_EPS, ADAM_WD, ADAM_STEP = 0.001, 0.9, 0.999, 1e-08, 0.01, 10
VMEM_LIMIT = 56 * 1024 * 1024
LANE = 128


def _cp(*sem):
    return pltpu.CompilerParams(dimension_semantics=sem, vmem_limit_bytes=VMEM_LIMIT)


def _tile(dim, target):
    if dim <= target:
        return dim
    t = (target // LANE) * LANE
    while dim % t:
        t -= LANE
    return t


def _sigmoid(x):
    return 1.0 / (1.0 + jnp.exp(-x))


def _silu(x):
    return x * _sigmoid(x)


def _dsilu(x):
    s = _sigmoid(x)
    return s * (1.0 + x * (1.0 - s))


def _split3(x):
    a = x.astype(BF16)
    r = x - a.astype(F32)
    b = r.astype(BF16)
    c = (r - b.astype(F32)).astype(BF16)
    return a, b, c


def _dot(a, b, ca=1, cb=0):
    return lax.dot_general(a, b, (((ca,), (cb,)), ((), ())), preferred_element_type=F32)


def _dot_exact(m01, x):
    a, b, c = _split3(x)
    return _dot(m01, a) + _dot(m01, b) + _dot(m01, c)


def _mm(a, b, *, ta=False, tb=False, add=None, out_dtype=F32, name, tm=1024, tn=1024, tk=1024):
    m, k = (a.shape[1], a.shape[0]) if ta else a.shape
    n = b.shape[0] if tb else b.shape[1]
    assert (b.shape[1] if tb else b.shape[0]) == k
    tm, tn, tk = _tile(m, tm), _tile(n, tn), _tile(k, tk)
    nk = k // tk
    has_add = add is not None

    def body(*refs):
        if has_add:
            a_ref, b_ref, c_ref, o_ref, acc = refs
        else:
            a_ref, b_ref, o_ref, acc = refs
        kk = pl.program_id(2)
        part = _dot(a_ref[...].astype(BF16), b_ref[...].astype(BF16), 0 if ta else 1, 1 if tb else 0)

        def finish(r):
            if has_add:
                r = r + c_ref[...]
            o_ref[...] = r.astype(o_ref.dtype)

        if nk == 1:
            finish(part)
            return

        @pl.when(kk == 0)
        def _():
            acc[...] = part

        @pl.when((kk > 0) & (kk < nk - 1))
        def _():
            acc[...] += part

        @pl.when(kk == nk - 1)
        def _():
            finish(acc[...] + part)

    a_spec = pl.BlockSpec((tk, tm), lambda i, j, kk: (kk, i)) if ta else pl.BlockSpec((tm, tk), lambda i, j, kk: (i, kk))
    b_spec = pl.BlockSpec((tn, tk), lambda i, j, kk: (j, kk)) if tb else pl.BlockSpec((tk, tn), lambda i, j, kk: (kk, j))
    in_specs = [a_spec, b_spec]
    args = [a, b]
    if has_add:
        in_specs.append(pl.BlockSpec((tm, tn), lambda i, j, kk: (i, j)))
        args.append(add)
    return pl.pallas_call(
        body, name=name, grid=(m // tm, n // tn, nk), in_specs=in_specs,
        out_specs=pl.BlockSpec((tm, tn), lambda i, j, kk: (i, j)),
        out_shape=jax.ShapeDtypeStruct((m, n), out_dtype),
        scratch_shapes=[pltpu.VMEM((tm, tn) if nk > 1 else (8, LANE), F32)],
        compiler_params=_cp("parallel", "parallel", "arbitrary"),
    )(*args)


ROWS = 512


def _row_spec(tm, d):
    return pl.BlockSpec((tm, d), lambda i: (i, 0))


def _vec_spec(d, rows=1):
    return pl.BlockSpec((rows, d), lambda i: (0, 0))


def _modulate(x, scale, shift, name):
    s, d = x.shape
    tm = min(ROWS, s)

    def body(x_ref, sc_ref, sh_ref, o_ref):
        o_ref[...] = (x_ref[...] * (1.0 + sc_ref[...]) + sh_ref[...]).astype(BF16)

    return pl.pallas_call(
        body, name=name, grid=(s // tm,), in_specs=[_row_spec(tm, d), _vec_spec(d), _vec_spec(d)],
        out_specs=_row_spec(tm, d), out_shape=jax.ShapeDtypeStruct((s, d), BF16), compiler_params=_cp("parallel"),
    )(x, scale, shift)


def _resid_ln(x, y, gate, g, b, name):
    s, d = x.shape
    tm = min(ROWS, s)

    def body(x_ref, y_ref, gt_ref, g_ref, b_ref, o_ref):
        u = DEEPNORM_ALPHA * x_ref[...] + gt_ref[...] * y_ref[...]
        mu = jnp.mean(u, axis=1, keepdims=True)
        uc = u - mu
        var = jnp.mean(uc * uc, axis=1, keepdims=True)
        o_ref[...] = uc * lax.rsqrt(var + LN_EPS) * g_ref[...] + b_ref[...]

    return pl.pallas_call(
        body, name=name, grid=(s // tm,),
        in_specs=[_row_spec(tm, d), _row_spec(tm, d), _vec_spec(d), _vec_spec(d), _vec_spec(d)],
        out_specs=_row_spec(tm, d), out_shape=jax.ShapeDtypeStruct((s, d), F32), compiler_params=_cp("parallel"),
    )(x, y, gate, g, b)


def _resid_ln_bwd(x, y, dxn, gate, g, name):
    s, d = x.shape
    tm = min(ROWS, s)

    def body(x_ref, y_ref, dxn_ref, gt_ref, g_ref, du_ref, dy_ref, red_ref):
        @pl.when(pl.program_id(0) == 0)
        def _():
            red_ref[...] = jnp.zeros_like(red_ref)

        yv = y_ref[...]
        u = DEEPNORM_ALPHA * x_ref[...] + gt_ref[...] * yv
        mu = jnp.mean(u, axis=1, keepdims=True)
        uc = u - mu
        var = jnp.mean(uc * uc, axis=1, keepdims=True)
        rstd = lax.rsqrt(var + LN_EPS)
        xhat = uc * rstd
        dxnv = dxn_ref[...]
        dxh = dxnv * g_ref[...]
        du = rstd * (dxh - jnp.mean(dxh, axis=1, keepdims=True) - xhat * jnp.mean(dxh * xhat, axis=1, keepdims=True))
        du_ref[...] = du
        dy_ref[...] = (du * gt_ref[...]).astype(BF16)
        red_ref[0:1, :] += jnp.sum(du * yv, axis=0, keepdims=True)
        red_ref[1:2, :] += jnp.sum(dxnv * xhat, axis=0, keepdims=True)
        red_ref[2:3, :] += jnp.sum(dxnv, axis=0, keepdims=True)

    return pl.pallas_call(
        body, name=name, grid=(s // tm,),
        in_specs=[_row_spec(tm, d), _row_spec(tm, d), _row_spec(tm, d), _vec_spec(d), _vec_spec(d)],
        out_specs=[_row_spec(tm, d), _row_spec(tm, d), _vec_spec(d, 8)],
        out_shape=[jax.ShapeDtypeStruct((s, d), F32), jax.ShapeDtypeStruct((s, d), BF16), jax.ShapeDtypeStruct((8, d), F32)],
        compiler_params=_cp("arbitrary"),
    )(x, y, dxn, gate, g)


def _modulate_bwd(du, dhs, x, scale, name):
    s, d = x.shape
    tm = min(ROWS, s)
    n = len(dhs)

    def body(*refs):
        du_ref, dh_refs, (x_ref, sc_ref, dx_ref, red_ref) = refs[0], refs[1:1 + n], refs[1 + n:]

        @pl.when(pl.program_id(0) == 0)
        def _():
            red_ref[...] = jnp.zeros_like(red_ref)

        dhv = dh_refs[0][...]
        for t in dh_refs[1:]:
            dhv = dhv + t[...]
        dx_ref[...] = DEEPNORM_ALPHA * du_ref[...] + dhv * (1.0 + sc_ref[...])
        red_ref[0:1, :] += jnp.sum(dhv * x_ref[...], axis=0, keepdims=True)
        red_ref[1:2, :] += jnp.sum(dhv, axis=0, keepdims=True)

    return pl.pallas_call(
        body, name=name, grid=(s // tm,),
        in_specs=[_row_spec(tm, d)] * (n + 2) + [_vec_spec(d)],
        out_specs=[_row_spec(tm, d), _vec_spec(d, 8)],
        out_shape=[jax.ShapeDtypeStruct((s, d), F32), jax.ShapeDtypeStruct((8, d), F32)],
        compiler_params=_cp("arbitrary"),
    )(du, *dhs, x, scale)


def _loss_grad(xf, tgt, name):
    s, d = xf.shape
    tm = min(ROWS, s)

    def body(x_ref, t_ref, dx_ref, red_ref):
        @pl.when(pl.program_id(0) == 0)
        def _():
            red_ref[...] = jnp.zeros_like(red_ref)

        e = x_ref[...] - t_ref[...]
        dx_ref[...] = e * (1.0 / d)
        red_ref[0:1, :] += jnp.sum(e * e, axis=0, keepdims=True)

    return pl.pallas_call(
        body, name=name, grid=(s // tm,), in_specs=[_row_spec(tm, d), _row_spec(tm, d)],
        out_specs=[_row_spec(tm, d), _vec_spec(d, 8)],
        out_shape=[jax.ShapeDtypeStruct((s, d), F32), jax.ShapeDtypeStruct((8, d), F32)],
        compiler_params=_cp("arbitrary"),
    )(xf, tgt)


QKV_COLS = 3 * 3 * A_WIDTH


SLOPES = tuple(float(2.0 ** (-8.0 * (h + 1.0) / A_HEADS)) for h in range(A_HEADS))
FAR = 1e30
HEAD_COLS = tuple(slice(h * A_HEAD_DIM, (h + 1) * A_HEAD_DIM) for h in range(A_HEADS))


def _band_dist(n, length, dil, span_rows):
    shape = (2 * A_QBLOCK, A_QBLOCK) if span_rows else (A_QBLOCK, 2 * A_QBLOCK)
    r = lax.broadcasted_iota(jnp.int32, shape, 0)
    c = lax.broadcasted_iota(jnp.int32, shape, 1)
    sp, ce = (r, c) if span_rows else (c, r)
    delta = sp - A_RADIUS - ce
    pos = n * A_QBLOCK - A_RADIUS + sp
    valid = (jnp.abs(delta) <= A_RADIUS) & (pos >= 0) & (pos < length)
    return jnp.where(valid, jnp.abs(delta).astype(F32) * float(dil), FAR)


def _span_specs(col, nb64):
    def mk(i):
        return pl.BlockSpec((64, A_WIDTH), lambda r, n: (r * nb64 + jnp.clip(2 * n - 1 + i, 0, nb64 - 1), col))
    return [mk(i) for i in range(4)]


def _to_residue(t, dil):
    if dil == 1:
        return t
    s, c = t.shape
    return t.reshape(s // dil, dil, c).transpose(1, 0, 2).reshape(s, c)


def _from_residue(t, dil):
    if dil == 1:
        return t
    s, c = t.shape
    return t.reshape(dil, s // dil, c).transpose(1, 0, 2).reshape(s, c)


def _cat(refs):
    return jnp.concatenate([t[...] for t in refs], axis=0)


def _head_expander():
    r = lax.broadcasted_iota(jnp.int32, (A_HEADS, A_WIDTH), 0)
    c = lax.broadcasted_iota(jnp.int32, (A_HEADS, A_WIDTH), 1)
    return ((c >= r * A_HEAD_DIM) & (c < (r + 1) * A_HEAD_DIM)).astype(BF16)


def _to_lanes(x16, e):
    a, b, c = _split3(x16)
    return _dot(a, e) + _dot(b, e) + _dot(c, e)


def _per_head_sum(x, e):
    a, b, c = _split3(x)
    return _dot(a, e, 1, 1) + _dot(b, e, 1, 1) + _dot(c, e, 1, 1)


def _pair_low_lanes():
    return lax.broadcasted_iota(jnp.int32, (A_QBLOCK, LANE), 1) < A_HEAD_DIM


def _top_rows():
    return lax.broadcasted_iota(jnp.int32, (2 * A_QBLOCK, 1), 0) < A_QBLOCK


def _block_diag(v, low):
    zero = jnp.zeros_like(v)
    return jnp.concatenate([jnp.where(low, v, zero), jnp.where(low, zero, v)], axis=0)


def _attn_fwd(qkv, g, name):
    s = qkv.shape[0]
    dil = DILATIONS[g]
    length = s // dil
    nblk = length // A_QBLOCK

    def body(q_ref, k0, k1, k2, k3, v0, v1, v2, v3, o_ref, l_ref):
        dist = _band_dist(pl.program_id(1), length, dil, False)
        kk = _cat((k0, k1, k2, k3))
        vv = _cat((v0, v1, v2, v3))
        low = _pair_low_lanes()
        top = _top_rows()
        dist2 = jnp.concatenate([dist, dist], axis=0)
        for hp in range(A_HEADS // 2):
            ls = slice(hp * LANE, (hp + 1) * LANE)
            qp, kp, vp = q_ref[:, ls], kk[:, ls], vv[:, ls]
            sc = _dot(_block_diag(qp, low), kp, 1, 1) * 0.125 - jnp.where(top, SLOPES[2 * hp], SLOPES[2 * hp + 1]) * dist2
            m = jnp.max(sc, axis=1, keepdims=True)
            p = jnp.exp(sc - m)
            z = jnp.sum(p, axis=1, keepdims=True)
            o2 = _dot(p.astype(BF16), vp) / z
            lse2 = m + jnp.log(z)
            l_ref[:, 2 * hp:2 * hp + 1] = lse2[:A_QBLOCK]
            l_ref[:, 2 * hp + 1:2 * hp + 2] = lse2[A_QBLOCK:]
            o_ref[:, ls] = jnp.where(low, o2[:A_QBLOCK], o2[A_QBLOCK:])

    qspec = pl.BlockSpec((A_QBLOCK, A_WIDTH), lambda r, n: (r * nblk + n, 0))
    lspec = pl.BlockSpec((A_QBLOCK, A_HEADS), lambda r, n: (r * nblk + n, 0))
    return pl.pallas_call(
        body, name=name, grid=(dil, nblk), in_specs=[qspec] + _span_specs(1, 2 * nblk) + _span_specs(2, 2 * nblk),
        out_specs=[qspec, lspec],
        out_shape=[jax.ShapeDtypeStruct((s, A_WIDTH), F32), jax.ShapeDtypeStruct((s, A_HEADS), F32)],
        compiler_params=_cp("parallel", "parallel"),
    )(*([qkv] * 9))


def _attn_merge(os_, ls_, gate, name):
    s, w = gate.shape
    tm = min(ROWS, s)

    def body(o0, o1, o2, l0, l1, l2, g_ref, y_ref, o_ref, l_ref):
        a, b, c = l0[...], l1[...], l2[...]
        m = jnp.maximum(jnp.maximum(a, b), c)
        ea, eb, ec = jnp.exp(a - m), jnp.exp(b - m), jnp.exp(c - m)
        z = ea + eb + ec
        l_ref[...] = m + jnp.log(z)
        e = _head_expander()
        o = _to_lanes(ea / z, e) * o0[...] + _to_lanes(eb / z, e) * o1[...] + _to_lanes(ec / z, e) * o2[...]
        o_ref[...] = o
        y_ref[...] = (o * _silu(g_ref[...])).astype(BF16)

    rs = _row_spec(tm, w)
    ls = _row_spec(tm, A_HEADS)
    return pl.pallas_call(
        body, name=name, grid=(s // tm,), in_specs=[rs] * 3 + [ls] * 3 + [rs], out_specs=[rs, rs, ls],
        out_shape=[jax.ShapeDtypeStruct((s, w), BF16), jax.ShapeDtypeStruct((s, w), F32), jax.ShapeDtypeStruct((s, A_HEADS), F32)],
        compiler_params=_cp("parallel"),
    )(*os_, *ls_, gate)


def _attn_gate_bwd(dyy, o, gate, name):
    s, w = gate.shape
    tm = min(ROWS, s)

    def body(dy_ref, o_ref, g_ref, do_ref, dg_ref, dl_ref):
        dyv, ov, gv = dy_ref[...], o_ref[...], g_ref[...]
        do = dyv * _silu(gv)
        do_ref[...] = do.astype(BF16)
        dg_ref[...] = (dyv * ov * _dsilu(gv)).astype(BF16)
        dl_ref[...] = _per_head_sum(do * ov, _head_expander())

    rs = _row_spec(tm, w)
    return pl.pallas_call(
        body, name=name, grid=(s // tm,), in_specs=[rs] * 3, out_specs=[rs, rs, _row_spec(tm, A_HEADS)],
        out_shape=[jax.ShapeDtypeStruct((s, w), BF16), jax.ShapeDtypeStruct((s, w), BF16), jax.ShapeDtypeStruct((s, A_HEADS), F32)],
        compiler_params=_cp("parallel"),
    )(dyy, o, gate)


def _attn_bwd(qkv, do, lse, delta, g, name):
    s = qkv.shape[0]
    dil = DILATIONS[g]
    length = s // dil
    nblk = length // A_QBLOCK

    def rows(t16):
        return jnp.pad(t16.reshape(dil, length, A_HEADS).transpose(0, 2, 1), ((0, 0), (0, 0), (A_RADIUS, A_RADIUS)))

    def body(q0, q1, q2, q3, k0, k1, k2, k3, v0, v1, v2, v3, d0, d1, d2, d3, lc_ref, ec_ref, la, lb, ea, eb, o_ref):
        dist = _band_dist(pl.program_id(1), length, dil, False)
        qq, kk, vv, dd = _cat((q0, q1, q2, q3)), _cat((k0, k1, k2, k3)), _cat((v0, v1, v2, v3)), _cat((d0, d1, d2, d3))
        lse_r = jnp.concatenate([la[...], lb[...]], axis=1)
        dlt_r = jnp.concatenate([ea[...], eb[...]], axis=1)
        low = _pair_low_lanes()
        top = _top_rows()
        dist2 = jnp.concatenate([dist, dist], axis=0)
        centre = slice(A_RADIUS, A_RADIUS + A_QBLOCK)
        for hp in range(A_HEADS // 2):
            ls = slice(hp * LANE, (hp + 1) * LANE)
            qs, ks, vs, ds_ = qq[:, ls], kk[:, ls], vv[:, ls], dd[:, ls]
            qn, kn, vn, dn = qs[centre], ks[centre], vs[centre], ds_[centre]
            h0, h1 = 2 * hp, 2 * hp + 1
            bias = jnp.where(top, SLOPES[h0], SLOPES[h1]) * dist2
            lc = jnp.concatenate([lc_ref[:, h0:h0 + 1], lc_ref[:, h1:h1 + 1]], axis=0)
            ec = jnp.concatenate([ec_ref[:, h0:h0 + 1], ec_ref[:, h1:h1 + 1]], axis=0)
            lr = jnp.where(top, lse_r[h0:h0 + 1, :], lse_r[h1:h1 + 1, :])
            er = jnp.where(top, dlt_r[h0:h0 + 1, :], dlt_r[h1:h1 + 1, :])
            p = jnp.exp(_dot(_block_diag(qn, low), ks, 1, 1) * 0.125 - bias - lc)
            dsc = p * (_dot(_block_diag(dn, low), vs, 1, 1) - ec)
            dq = _dot(dsc.astype(BF16), ks)
            pt = jnp.exp(_dot(_block_diag(kn, low), qs, 1, 1) * 0.125 - bias - lr)
            dst = pt * (_dot(_block_diag(vn, low), ds_, 1, 1) - er)
            dk = _dot(dst.astype(BF16), qs)
            dv = _dot(pt.astype(BF16), ds_)
            merge = lambda t: jnp.where(low, t[:A_QBLOCK], t[A_QBLOCK:])
            o_ref[:, ls] = (merge(dq) * 0.125).astype(BF16)
            o_ref[:, A_WIDTH + hp * LANE:A_WIDTH + (hp + 1) * LANE] = (merge(dk) * 0.125).astype(BF16)
            o_ref[:, 2 * A_WIDTH + hp * LANE:2 * A_WIDTH + (hp + 1) * LANE] = merge(dv).astype(BF16)

    nb64 = 2 * nblk
    dspecs = _span_specs(0, nb64)
    cspec = pl.BlockSpec((A_QBLOCK, A_HEADS), lambda r, n: (r * nblk + n, 0))
    rspecs = [pl.BlockSpec((None, A_HEADS, A_QBLOCK), lambda r, n: (r, 0, n)), pl.BlockSpec((None, A_HEADS, A_QBLOCK), lambda r, n: (r, 0, n + 1))]
    lse_r, dlt_r = rows(lse), rows(delta)
    return pl.pallas_call(
        body, name=name, grid=(dil, nblk),
        in_specs=_span_specs(0, nb64) + _span_specs(1, nb64) + _span_specs(2, nb64) + dspecs + [cspec, cspec] + rspecs * 2,
        out_specs=pl.BlockSpec((A_QBLOCK, 3 * A_WIDTH), lambda r, n: (r * nblk + n, 0)),
        out_shape=jax.ShapeDtypeStruct((s, 3 * A_WIDTH), BF16),
        compiler_params=_cp("parallel", "parallel"),
    )(*([qkv] * 12), *([do] * 4), lse, delta, lse_r, lse_r, dlt_r, dlt_r)


def _attn_layer_fwd(h, w_qkv, w_gate, w_out, li):
    nm = lambda t: f"a{li}_{t}"
    gate = _mm(h, w_gate, name=nm("gate"))
    hs, qkvs, os_, ls_ = [], [], [], []
    for g, dil in enumerate(DILATIONS):
        hg = _to_residue(h, dil)
        qkv = _mm(hg, w_qkv[:, g * 3 * A_WIDTH:(g + 1) * 3 * A_WIDTH], out_dtype=BF16, name=nm(f"qkv{g}"))
        o, l = _attn_fwd(qkv, g, nm(f"attn{g}"))
        hs.append(hg)
        qkvs.append(qkv)
        os_.append(_from_residue(o, dil))
        ls_.append(_from_residue(l, dil))
    y, o, lse = _attn_merge(os_, ls_, gate, nm("merge"))
    out = _mm(y, w_out, name=nm("out"))
    return out, (hs, qkvs, gate, y, o, lse)


def _attn_layer_bwd(dy, h, saved, w_qkv, w_gate, w_out, li):
    nm = lambda t: f"a{li}_{t}"
    hs, qkvs, gate, y, o, lse = saved
    g_w_out = _mm(y, dy, ta=True, out_dtype=BF16, name=nm("dwout"))
    dyy = _mm(dy, w_out, tb=True, name=nm("dyy"))
    do, dgate, delta = _attn_gate_bwd(dyy, o, gate, nm("gatebwd"))
    dhs, dws = [], []
    for g, dil in enumerate(DILATIONS):
        dqkv = _attn_bwd(qkvs[g], _to_residue(do, dil), _to_residue(lse, dil), _to_residue(delta, dil), g, nm(f"attnbwd{g}"))
        wg = w_qkv[:, g * 3 * A_WIDTH:(g + 1) * 3 * A_WIDTH]
        dws.append(_mm(hs[g], dqkv, ta=True, out_dtype=BF16, name=nm(f"dwqkv{g}")))
        add = _mm(dgate, w_gate, tb=True, name=nm("dh_gate")) if g == 0 else None
        dhs.append(_from_residue(_mm(dqkv, wg, tb=True, add=add, name=nm(f"dh_qkv{g}")), dil))
    g_w_in = jnp.concatenate(dws + [_mm(h, dgate, ta=True, out_dtype=BF16, name=nm("dwgate"))], axis=1)
    return dhs, g_w_in, g_w_out


SSM_INNER = SSM_HEADS * SSM_HEAD_DIM
SSM_BC = SSM_GROUPS * SSM_STATE
SSM_CONV_DIM = SSM_INNER + 2 * SSM_BC
GW = SSM_REP * SSM_HEAD_DIM
T = SSM_CHUNK
HALO = 8


def _conv_specs(tm, tn, s, col=lambda j: j):
    nb8 = s // HALO
    cur = pl.BlockSpec((tm, tn), lambda j, i: (i, col(j)))
    prev = pl.BlockSpec((HALO, tn), lambda j, i: (jnp.maximum(i * (tm // HALO) - 1, 0), col(j)))
    nxt = pl.BlockSpec((HALO, tn), lambda j, i: (jnp.minimum((i + 1) * (tm // HALO), nb8 - 1), col(j)))
    return [prev, cur, nxt]


def _extend(prev_ref, cur_ref, nxt_ref, i, nrow):
    p = jnp.where(i == 0, 0.0, prev_ref[...])
    n = jnp.where(i == nrow - 1, 0.0, nxt_ref[...])
    return jnp.concatenate([p, cur_ref[...], n], axis=0)


def _shift_rows(ext, off, tm):
    rows = ext.shape[0]
    return pltpu.roll(ext, (-off) % rows, 0)[HALO:HALO + tm]


def _conv_fwd(xraw, w, b, name):
    s, cdim = xraw.shape
    tm, tn = min(256, s), 1024
    nrow = s // tm

    def body(p_ref, c_ref, n_ref, w_ref, b_ref, pre_ref, act_ref):
        ext = _extend(p_ref, c_ref, n_ref, pl.program_id(1), nrow)
        acc = jnp.broadcast_to(b_ref[...], (tm, tn))
        for k in range(SSM_CONV):
            acc = acc + w_ref[k:k + 1, :] * _shift_rows(ext, k - SSM_CONV // 2, tm)
        pre_ref[...] = acc
        act_ref[...] = _silu(acc)

    prev, cur, nxt = _conv_specs(tm, tn, s)
    return pl.pallas_call(
        body, name=name, grid=(cdim // tn, nrow),
        in_specs=[prev, cur, nxt, pl.BlockSpec((SSM_CONV, tn), lambda j, i: (0, j)), pl.BlockSpec((1, tn), lambda j, i: (0, j))],
        out_specs=[cur, cur], out_shape=[jax.ShapeDtypeStruct((s, cdim), F32)] * 2,
        compiler_params=_cp("parallel", "parallel"),
    )(xraw, xraw, xraw, w, b)


def _conv_bwd(dx, db, dc, pre, xraw, w, name):
    s, cdim = xraw.shape
    tm, tn = min(256, s), 1024
    nrow = s // tm
    nx = dx.shape[1] // tn

    def body(xp, xc, xn, bp, bc, bn, cp, cc, cn, pp, pc, pn, x_ref, w_ref, o_ref, red_ref):
        j, i = pl.program_id(0), pl.program_id(1)

        @pl.when(i == 0)
        def _():
            red_ref[...] = jnp.zeros_like(red_ref)

        bcext = jnp.concatenate([_extend(bp, bc, bn, i, nrow), _extend(cp, cc, cn, i, nrow)], axis=1)
        dact = jnp.where(j < nx, _extend(xp, xc, xn, i, nrow), bcext)
        dpre = dact * _dsilu(_extend(pp, pc, pn, i, nrow))
        xv = x_ref[...]
        acc = jnp.zeros((tm, tn), F32)
        for k in range(SSM_CONV):
            sk = _shift_rows(dpre, SSM_CONV // 2 - k, tm)
            acc = acc + w_ref[k:k + 1, :] * sk
            red_ref[k:k + 1, :] += jnp.sum(sk * xv, axis=0, keepdims=True)
        red_ref[SSM_CONV:SSM_CONV + 1, :] += jnp.sum(dpre[HALO:HALO + tm], axis=0, keepdims=True)
        o_ref[...] = acc.astype(BF16)

    half = tn // 2
    cur = pl.BlockSpec((tm, tn), lambda j, i: (i, j))
    return pl.pallas_call(
        body, name=name, grid=(cdim // tn, nrow),
        in_specs=_conv_specs(tm, tn, s, lambda j: jnp.minimum(j, nx - 1)) + _conv_specs(tm, half, s, lambda j: 0) * 2
        + _conv_specs(tm, tn, s) + [cur, pl.BlockSpec((SSM_CONV, tn), lambda j, i: (0, j))],
        out_specs=[cur, pl.BlockSpec((8, tn), lambda j, i: (0, j))],
        out_shape=[jax.ShapeDtypeStruct((s, cdim), BF16), jax.ShapeDtypeStruct((8, cdim), F32)],
        compiler_params=_cp("parallel", "arbitrary"),
    )(dx, dx, dx, db, db, db, dc, dc, dc, pre, pre, pre, xraw, w)


def _tri(lower):
    r = lax.broadcasted_iota(jnp.int32, (T, T), 0)
    c = lax.broadcasted_iota(jnp.int32, (T, T), 1)
    return (r >= c) if lower else (r <= c)


def _softplus(x):
    return jnp.maximum(x, 0.0) + jnp.log(1.0 + jnp.exp(-jnp.abs(x)))


def _dt_prep(dt_raw, bias, a_log, name):
    s = dt_raw.shape[0]
    nc = s // T

    def body(r_ref, b_ref, a_ref, dt_ref, cum_ref, cumt_ref):
        dt = _softplus(r_ref[...] + b_ref[...])
        da = dt * (-jnp.exp(a_ref[...]))
        pre = _dot_exact(_tri(True).astype(BF16), da)
        suf = _dot_exact(_tri(False).astype(BF16), da)
        lane = lax.broadcasted_iota(jnp.int32, (T, LANE), 1)
        cum = jnp.where(lane < SSM_HEADS, pre, suf)
        dt_ref[...] = dt
        cum_ref[...] = cum
        cumt_ref[...] = cum.T

    blk = pl.BlockSpec((T, LANE), lambda c: (c, 0))
    vec = pl.BlockSpec((1, LANE), lambda c: (0, 0))
    return pl.pallas_call(
        body, name=name, grid=(nc,), in_specs=[blk, vec, vec],
        out_specs=[blk, blk, pl.BlockSpec((None, LANE, T), lambda c: (c, 0, 0))],
        out_shape=[jax.ShapeDtypeStruct((s, LANE), F32), jax.ShapeDtypeStruct((s, LANE), F32), jax.ShapeDtypeStruct((nc, LANE, T), F32)],
        compiler_params=_cp("parallel"),
    )(dt_raw, bias, a_log)


def _by_group(t):
    s = t.shape[0]
    return t[:, :2 * SSM_HEADS].reshape(s, 2 * SSM_GROUPS, SSM_REP).transpose(1, 0, 2)


def _from_group(tf, tb):
    s = tf.shape[1]
    t = jnp.concatenate([tf, tb], axis=0).transpose(1, 0, 2).reshape(s, 2 * SSM_HEADS)
    return jnp.pad(t, ((0, 0), (0, LANE - 2 * SSM_HEADS)))


def _decay_mats(acol, arow, rev):
    after = _tri(not rev)
    return jnp.where(after, jnp.exp(jnp.where(after, acol - arow, 0.0)), 0.0)


PAIRS = SSM_REP // 2


def _low_lanes():
    return lax.broadcasted_iota(jnp.int32, (T, LANE), 1) < SSM_HEAD_DIM


CPS = 4
TB = CPS * T


def _scan_specs(rev, ci):
    nxb = SSM_INNER // LANE
    kofs = SSM_GROUPS if rev else 0
    return [
        pl.BlockSpec((TB, GW), lambda g, c: (ci(c), g)),
        pl.BlockSpec((TB, LANE), lambda g, c: (ci(c), nxb + g)),
        pl.BlockSpec((TB, LANE), lambda g, c: (ci(c), nxb + SSM_GROUPS + g)),
        pl.BlockSpec((None, TB, SSM_REP), lambda g, c: (kofs + g, ci(c), 0)),
        pl.BlockSpec((None, TB, SSM_REP), lambda g, c: (kofs + g, ci(c), 0)),
        pl.BlockSpec((None, CPS, SSM_REP, T), lambda g, c: (kofs + g, ci(c), 0, 0)),
    ]


def _chunk_rows(q):
    return pl.ds(q * T, T)


def _pair_lanes(ref, p, low):
    return jnp.where(low, ref[:, 2 * p:2 * p + 1], ref[:, 2 * p + 1:2 * p + 2])


def _ssd_scan(xbc, dtk, cumk, cumtk, rev, name):
    s = xbc.shape[0]
    nc = s // T
    nb = nc // CPS
    last = 0 if rev else T - 1
    ci = (lambda c: nb - 1 - c) if rev else (lambda c: c)

    def body(x_ref, b_ref, c_ref, dt_ref, cum_ref, cumt_ref, y_ref, st_ref, state):
        @pl.when(pl.program_id(1) == 0)
        def _():
            state[...] = jnp.zeros_like(state)

        for q in (reversed(range(CPS)) if rev else range(CPS)):
            rows = _chunk_rows(q)
            chunk(x_ref.at[rows], b_ref.at[rows], c_ref.at[rows], dt_ref.at[rows], cum_ref.at[rows], cumt_ref.at[q],
                  y_ref.at[rows], st_ref.at[q], state)

    def chunk(x_ref, b_ref, c_ref, dt_ref, cum_ref, cumt_ref, y_ref, st_ref, state):
        bm = b_ref[...]
        cm = c_ref[...].astype(BF16)
        cb = _dot(cm, bm.astype(BF16), 1, 1)
        bt = bm.T.astype(BF16)
        low = _low_lanes()
        for p in range(PAIRS):
            ls = slice(p * LANE, (p + 1) * LANE)
            acum = _pair_lanes(cum_ref, p, low)
            u = x_ref[:, ls] * _pair_lanes(dt_ref, p, low)
            tot = acum[last:last + 1, :]
            m = [(cb * _decay_mats(cum_ref[:, r:r + 1], cumt_ref[r:r + 1, :], rev)).astype(BF16) for r in (2 * p, 2 * p + 1)]
            st = state[p]
            st_ref[p] = st
            yd = _dot(jnp.concatenate(m, axis=1), _block_diag(u.astype(BF16), low))
            yo = jnp.exp(acum) * _dot(cm, st.astype(BF16))
            y_ref[:, ls] = yd + yo
            state[p] = jnp.exp(tot) * st + _dot(bt, (jnp.exp(tot - acum) * u).astype(BF16))

    return pl.pallas_call(
        body, name=name, grid=(SSM_GROUPS, nb), in_specs=_scan_specs(rev, ci),
        out_specs=[
            pl.BlockSpec((TB, GW), lambda g, c: (ci(c), g)),
            pl.BlockSpec((CPS, PAIRS, SSM_STATE, LANE), lambda g, c: (ci(c), g, 0, 0)),
        ],
        out_shape=[jax.ShapeDtypeStruct((s, SSM_INNER), F32), jax.ShapeDtypeStruct((nc, SSM_HEADS // 2, SSM_STATE, LANE), F32)],
        scratch_shapes=[pltpu.VMEM((PAIRS, SSM_STATE, LANE), F32)],
        compiler_params=_cp("parallel", "arbitrary"),
    )(xbc, xbc, xbc, dtk, cumk, cumtk)


def _ssd_scan_bwd(xbc, dtk, cumk, cumtk, dy, states, dvec, prev, rev, name, exch=()):
    s = xbc.shape[0]
    nc = s // T
    nb = nc // CPS
    last = 0 if rev else T - 1
    ci = (lambda c: c) if rev else (lambda c: nb - 1 - c)
    has_prev = prev is not None
    n_in = 12 if has_prev else 9
    n_ex = len(exch)

    def body(*refs):
        ins, ex_in, refs = refs[:n_in], refs[n_in:n_in + n_ex], refs[n_in + n_ex:]
        outs, ex_out, scratch = refs[:5], refs[5:5 + n_ex], refs[5 + n_ex:]
        if n_ex:
            scratch, sems = scratch[:4], scratch[4:]
            g, c = pl.program_id(0), pl.program_id(1)

            @pl.when((g == 0) & (c == 0))
            def _():
                for cp in _shard_exchange(ex_in, ex_out, *sems):
                    cp.start()

            @pl.when((g == SSM_GROUPS - 1) & (c == nb - 1))
            def _():
                for cp in _shard_exchange(ex_in, ex_out, *sems):
                    cp.wait()

        @pl.when(pl.program_id(1) == 0)
        def _():
            scratch[0][...] = jnp.zeros_like(scratch[0])

        for q in (range(CPS) if rev else reversed(range(CPS))):
            rows = _chunk_rows(q)
            cut = lambda t: t.at[rows]
            x_ref, b_ref, c_ref, dt_ref, cum_ref, cumt_ref, dy_ref, st_ref, dv_ref = ins[:9]
            sub = [cut(x_ref), cut(b_ref), cut(c_ref), cut(dt_ref), cut(cum_ref), cumt_ref.at[q], cut(dy_ref), st_ref.at[q], dv_ref]
            chunk(*sub, *[cut(t) for t in ins[9:]], *[cut(t) for t in outs], *scratch)

    def chunk(*refs):
        x_ref, b_ref, c_ref, dt_ref, cum_ref, cumt_ref, dy_ref, st_ref, dv_ref = refs[:9]
        refs = refs[9:]
        if has_prev:
            pdx, pdb, pdc = refs[:3]
            refs = refs[3:]
        dx_ref, db_ref, dc_ref, ddt_ref, dda_ref, dstate, rs_buf, in_buf, k_buf = refs
        rs_buf[...] = jnp.zeros_like(rs_buf)
        in_buf[...] = jnp.zeros_like(in_buf)
        k_buf[...] = jnp.zeros_like(k_buf)
        bm = b_ref[...].astype(BF16)
        cm = c_ref[...].astype(BF16)
        cbt = _dot(bm, cm, 1, 1)
        cb = _dot(cm, bm, 1, 1)
        ct = c_ref[...].T.astype(BF16)
        after = _tri(not rev)
        before = _tri(rev)
        from_k = before.astype(BF16)
        ri = lax.broadcasted_iota(jnp.int32, (T, T), 0)
        cj = lax.broadcasted_iota(jnp.int32, (T, T), 1)
        strictly_before = (cj > ri) if rev else (cj < ri)
        dcb = jnp.zeros((T, T), F32)
        dc_acc = jnp.zeros((T, SSM_STATE), F32)
        db_acc = jnp.zeros((T, SSM_STATE), F32)
        low = _low_lanes()
        ri2 = lax.broadcasted_iota(jnp.int32, (LANE, LANE), 0)
        cj2 = lax.broadcasted_iota(jnp.int32, (LANE, LANE), 1)
        halves = ((ri2 < SSM_HEAD_DIM) == (cj2 == 0)) & (cj2 < 2)
        halves = halves.astype(BF16)

        def head_sums(v):
            hi = v.astype(BF16)
            lo = (v - hi.astype(F32)).astype(BF16)
            return _dot(hi, halves) + _dot(lo, halves)

        for p in range(PAIRS):
            ls = slice(p * LANE, (p + 1) * LANE)
            c2 = slice(2 * p, 2 * p + 2)
            lm, lmt = [], []
            for r in (2 * p, 2 * p + 1):
                acol = cum_ref[:, r:r + 1]
                arow = cumt_ref[r:r + 1, :]
                lm.append(jnp.where(after, jnp.exp(jnp.where(after, acol - arow, 0.0)), 0.0))
                lmt.append(jnp.where(before, jnp.exp(jnp.where(before, arow - acol, 0.0)), 0.0))
            acum = _pair_lanes(cum_ref, p, low)
            tot = acum[last:last + 1, :]
            dtl = _pair_lanes(dt_ref, p, low)
            xl = x_ref[:, ls]
            u = xl * dtl
            ub = u.astype(BF16)
            dyl = dy_ref[:, ls]
            dyb = dyl.astype(BF16)
            st = st_ref[p]
            stb = st.astype(BF16)
            dst = dstate[p]
            dstb = dst.astype(BF16)
            dec = jnp.exp(tot - acum)
            eac = jnp.exp(acum)
            etot = jnp.exp(tot)
            du_off = dec * _dot(bm, dstb)
            mt = jnp.concatenate([(cbt * lmt[0]).astype(BF16), (cbt * lmt[1]).astype(BF16)], axis=1)
            du = _dot(mt, _block_diag(dyb, low)) + du_off
            zero = jnp.zeros_like(dyb)
            gl = [_dot(jnp.where(low, dyb, zero), ub, 1, 1) * lm[0], _dot(jnp.where(low, zero, dyb), ub, 1, 1) * lm[1]]
            dcb = dcb + gl[0] + gl[1]
            dc_acc = dc_acc + _dot((eac * dyl).astype(BF16), stb, 1, 1)
            db_acc = db_acc + _dot((dec * u).astype(BF16), dstb, 1, 1)
            w = jnp.concatenate([(gl[0] * cb).astype(BF16), (gl[1] * cb).astype(BF16)], axis=1)
            crossing = _dot(from_k, w)
            for j in range(2):
                cr = jnp.where(strictly_before, crossing[:, j * T:(j + 1) * T], 0.0)
                in_buf[:, 2 * p + j:2 * p + j + 1] = jnp.sum(cr, axis=1, keepdims=True)
            y_off = eac * _dot(cm, stb)
            udu = u * du_off
            rs_buf[:, c2] = head_sums(dyl * y_off - udu)[:, 0:2]
            col = jnp.sum(dst * (etot * st) + udu, axis=0, keepdims=True)
            k_buf[0:1, c2] = head_sums(jnp.broadcast_to(col, (8, LANE)))[0:1, 0:2]
            ddt_ref[:, c2] = head_sums(du * xl)[:, 0:2]
            dx = du * dtl
            if has_prev:
                dx = dx + pdx[:, ls]
            else:
                dx = dx + dyl * dv_ref[:, ls]
            dx_ref[:, ls] = dx
            dstate[p] = etot * dst + _dot(ct, (eac * dyl).astype(BF16))
        dda = in_buf[...] + _dot_exact(from_k, rs_buf[...]) + k_buf[0:1, :]
        dda_ref[...] = dda[:, :SSM_REP]
        dcbb = dcb.astype(BF16)
        dc = dc_acc + _dot(dcbb, bm)
        db = db_acc + _dot(dcbb, cm, 0, 0)
        if has_prev:
            dc = dc + pdc[...]
            db = db + pdb[...]
        dc_ref[...] = dc
        db_ref[...] = db

    xspec = pl.BlockSpec((TB, GW), lambda g, c: (ci(c), g))
    gspec = pl.BlockSpec((TB, LANE), lambda g, c: (ci(c), g))
    in_specs = _scan_specs(rev, ci) + [
        xspec,
        pl.BlockSpec((CPS, PAIRS, SSM_STATE, LANE), lambda g, c: (ci(c), g, 0, 0)),
        pl.BlockSpec((1, GW), lambda g, c: (0, g)),
    ]
    args = [xbc, xbc, xbc, dtk, cumk, cumtk, dy, states, dvec]
    if has_prev:
        in_specs += [xspec, gspec, gspec]
        args += list(prev)
    ospec8 = pl.BlockSpec((None, TB, SSM_REP), lambda g, c: (g, ci(c), 0))
    n_rem = (N_SHARD - 1) * n_ex
    sems = [pltpu.SemaphoreType.DMA((n_rem,)), pltpu.SemaphoreType.DMA((n_rem,)), pltpu.SemaphoreType.DMA((n_ex,))] if n_ex else []
    outs = pl.pallas_call(
        body, name=name, grid=(SSM_GROUPS, nb), in_specs=in_specs + [ANY] * n_ex,
        out_specs=[xspec, gspec, gspec, ospec8, ospec8] + [ANY] * n_ex,
        out_shape=[jax.ShapeDtypeStruct((s, SSM_INNER), F32), jax.ShapeDtypeStruct((s, SSM_BC), F32), jax.ShapeDtypeStruct((s, SSM_BC), F32),
                   jax.ShapeDtypeStruct((SSM_GROUPS, s, SSM_REP), F32), jax.ShapeDtypeStruct((SSM_GROUPS, s, SSM_REP), F32)]
        + [jax.ShapeDtypeStruct(t.shape, t.dtype) for t in exch],
        scratch_shapes=[pltpu.VMEM((PAIRS, SSM_STATE, LANE), F32), pltpu.VMEM((T, LANE), F32), pltpu.VMEM((T, LANE), F32),
                        pltpu.VMEM((8, LANE), F32)] + sems,
        compiler_params=_cp("arbitrary" if n_ex else "parallel", "arbitrary"),
    )(*args, *exch)
    return (outs[:5], outs[5:]) if n_ex else outs


def _ssd_post(yf, yb, xbc, z, dvec, nw, name):
    s = z.shape[0]
    tm = min(256, s)

    def body(yf_ref, yb_ref, x_ref, z_ref, dv_ref, nw_ref, o_ref):
        ys = yf_ref[...] + yb_ref[...] + dv_ref[...] * x_ref[...]
        yg = ys * _silu(z_ref[...])
        ms = jnp.mean(yg * yg, axis=1, keepdims=True)
        o_ref[...] = (yg * lax.rsqrt(ms + RMS_EPS) * nw_ref[...]).astype(BF16)

    rs = _row_spec(tm, SSM_INNER)
    vs = _vec_spec(SSM_INNER)
    return pl.pallas_call(
        body, name=name, grid=(s // tm,), in_specs=[rs, rs, rs, rs, vs, vs], out_specs=rs,
        out_shape=jax.ShapeDtypeStruct((s, SSM_INNER), BF16), compiler_params=_cp("parallel"),
    )(yf, yb, xbc, z, dvec, nw)


def _ssd_post_bwd(dyn, yf, yb, xbc, z, dvec, nw, name):
    s = z.shape[0]
    tm = min(256, s)

    def body(dyn_ref, yf_ref, yb_ref, x_ref, z_ref, dv_ref, nw_ref, dys_ref, dz_ref, red_ref):
        @pl.when(pl.program_id(0) == 0)
        def _():
            red_ref[...] = jnp.zeros_like(red_ref)

        xv, zv = x_ref[...], z_ref[...]
        ys = yf_ref[...] + yb_ref[...] + dv_ref[...] * xv
        sz = _silu(zv)
        yg = ys * sz
        rstd = lax.rsqrt(jnp.mean(yg * yg, axis=1, keepdims=True) + RMS_EPS)
        yhat = yg * rstd
        dynv = dyn_ref[...]
        dyh = dynv * nw_ref[...]
        dyg = rstd * (dyh - yhat * jnp.mean(dyh * yhat, axis=1, keepdims=True))
        dys = dyg * sz
        dys_ref[...] = dys
        dz_ref[...] = (dyg * ys * _dsilu(zv)).astype(BF16)
        red_ref[0:1, :] += jnp.sum(dynv * yhat, axis=0, keepdims=True)
        red_ref[1:2, :] += jnp.sum(dys * xv, axis=0, keepdims=True)

    rs = _row_spec(tm, SSM_INNER)
    vs = _vec_spec(SSM_INNER)
    return pl.pallas_call(
        body, name=name, grid=(s // tm,), in_specs=[rs, rs, rs, rs, rs, vs, vs],
        out_specs=[rs, rs, _vec_spec(SSM_INNER, 8)],
        out_shape=[jax.ShapeDtypeStruct((s, SSM_INNER), F32), jax.ShapeDtypeStruct((s, SSM_INNER), BF16), jax.ShapeDtypeStruct((8, SSM_INNER), F32)],
        compiler_params=_cp("arbitrary"),
    )(dyn, yf, yb, xbc, z, dvec, nw)


def _dt_bwd(dt_raw, bias, a_log, dt, ddt, dda, name):
    s = dt_raw.shape[0]
    tm = min(1024, s)

    def body(r_ref, b_ref, a_ref, dt_ref, ddt_ref, dda_ref, o_ref, red_ref):
        @pl.when(pl.program_id(0) == 0)
        def _():
            red_ref[...] = jnp.zeros_like(red_ref)

        a = -jnp.exp(a_ref[...])
        ddav = dda_ref[...]
        draw = (ddt_ref[...] + a * ddav) * _sigmoid(r_ref[...] + b_ref[...])
        o_ref[...] = draw.astype(BF16)
        red_ref[0:1, :] += jnp.sum(draw, axis=0, keepdims=True)
        red_ref[1:2, :] += a * jnp.sum(ddav * dt_ref[...], axis=0, keepdims=True)

    rs = _row_spec(tm, LANE)
    vs = _vec_spec(LANE)
    return pl.pallas_call(
        body, name=name, grid=(s // tm,), in_specs=[rs, vs, vs, rs, rs, rs], out_specs=[rs, _vec_spec(LANE, 8)],
        out_shape=[jax.ShapeDtypeStruct((s, LANE), BF16), jax.ShapeDtypeStruct((8, LANE), F32)],
        compiler_params=_cp("arbitrary"),
    )(dt_raw, bias, a_log, dt, ddt, dda)


def _pad_lanes(v):
    v = v.reshape(1, -1)
    return jnp.pad(v, ((0, 0), (0, LANE - v.shape[1])))


def _ssd_prep_weights(w_in, conv_w, conv_b, dt_bias, a_log, d_skip, norm_w, w_out):
    return dict(
        w_z=w_in[:, :SSM_INNER].astype(BF16),
        w_xbc=w_in[:, SSM_INNER:SSM_INNER + SSM_CONV_DIM].astype(BF16),
        w_dt=jnp.pad(w_in[:, SSM_INNER + SSM_CONV_DIM:], ((0, 0), (0, LANE - 2 * SSM_HEADS))).astype(BF16),
        conv_w=conv_w, conv_b=conv_b.reshape(1, -1), bias=_pad_lanes(dt_bias), a_log=_pad_lanes(a_log),
        dvec=jnp.repeat(d_skip, SSM_HEAD_DIM).reshape(1, -1), nw=norm_w.reshape(1, -1), w_out=w_out.astype(BF16),
    )


def _ssd_layer_fwd(h, w, li):
    nm = lambda t: f"b{li}_{t}"
    z = _mm(h, w["w_z"], name=nm("z"))
    xraw = _mm(h, w["w_xbc"], name=nm("xbc"))
    dt_raw = _mm(h, w["w_dt"], name=nm("dt"))
    pre, xbc = _conv_fwd(xraw, w["conv_w"], w["conv_b"], nm("conv"))
    dt, cum, cumt = _dt_prep(dt_raw, w["bias"], w["a_log"], nm("dtprep"))
    nc = cumt.shape[0]
    dtk, cumk = _by_group(dt), _by_group(cum)
    cumtk = cumt[:, :2 * SSM_HEADS].reshape(nc, 2 * SSM_GROUPS, SSM_REP, T).transpose(1, 0, 2, 3)
    yf, stf = _ssd_scan(xbc, dtk, cumk, cumtk, False, nm("scan_f"))
    yb, stb = _ssd_scan(xbc, dtk, cumk, cumtk, True, nm("scan_b"))
    yn = _ssd_post(yf, yb, xbc, z, w["dvec"], w["nw"], nm("post"))
    out = _mm(yn, w["w_out"], name=nm("out"))
    return out, (z, xraw, dt_raw, pre, xbc, dt, dtk, cumk, cumtk, yf, stf, yb, stb, yn)


def _ssd_layer_bwd(dy, h, saved, w, li, exch=((), ())):
    nm = lambda t: f"b{li}_{t}"
    z, xraw, dt_raw, pre, xbc, dt, dtk, cumk, cumtk, yf, stf, yb, stb, yn = saved
    g_w_out = _mm(yn, dy, ta=True, out_dtype=BF16, name=nm("dwout"))
    dyn = _mm(dy, w["w_out"], tb=True, name=nm("dyn"))
    dys, dz, pred = _ssd_post_bwd(dyn, yf, yb, xbc, z, w["dvec"], w["nw"], nm("postbwd"))
    arrived = [(), ()]
    res = _ssd_scan_bwd(xbc, dtk, cumk, cumtk, dys, stf, w["dvec"], None, False, nm("scanbwd_f"), exch[0])
    if exch[0]:
        res, arrived[0] = res
    dx1, db1, dc1, ddt_f, dda_f = res
    res = _ssd_scan_bwd(xbc, dtk, cumk, cumtk, dys, stb, w["dvec"], (dx1, db1, dc1), True, nm("scanbwd_b"), exch[1])
    if exch[1]:
        res, arrived[1] = res
    dx, db, dc, ddt_b, dda_b = res
    dxraw, cred = _conv_bwd(dx, db, dc, pre, xraw, w["conv_w"], nm("convbwd"))
    draw, dred = _dt_bwd(dt_raw, w["bias"], w["a_log"], dt, _from_group(ddt_f, ddt_b), _from_group(dda_f, dda_b), nm("dtbwd"))
    dh = _mm(dz, w["w_z"], tb=True, name=nm("dh_z"))
    dh = _mm(dxraw, w["w_xbc"], tb=True, add=dh, name=nm("dh_xbc"))
    dh = _mm(draw, w["w_dt"], tb=True, add=dh, name=nm("dh_dt"))
    g_w_in = jnp.concatenate([_mm(h, dz, ta=True, out_dtype=BF16, name=nm("dwz")), _mm(h, dxraw, ta=True, out_dtype=BF16, name=nm("dwxbc")),
                              _mm(h, draw, ta=True, out_dtype=BF16, name=nm("dwdt"))[:, :2 * SSM_HEADS]], axis=1)
    grads = (g_w_in, cred[:SSM_CONV], cred[SSM_CONV], dred[0, :2 * SSM_HEADS].reshape(2, SSM_HEADS),
             dred[1, :2 * SSM_HEADS].reshape(2, SSM_HEADS), pred[1].reshape(SSM_HEADS, SSM_HEAD_DIM).sum(axis=1), pred[0], g_w_out)
    return dh, grads, arrived


B_GRAD_NAMES = ("b_w_in", "b_conv_w", "b_conv_b", "b_dt_bias", "b_a_log", "b_d", "b_norm_w", "b_w_out")


def _shards_of(big, j):
    return [_shard_cols(big["a_w_in"][j]), big["a_w_out"][j].reshape(N_SHARD, -1, big["a_w_out"][j].shape[-1]),
            _shard_cols(big["b_w_in"][j]), big["b_w_out"][j].reshape(N_SHARD, -1, big["b_w_out"][j].shape[-1])]


def _local_step(x, tgt, mod, w, early=False):
    d = x.shape[1]
    qkv_cols = QKV_COLS
    layers = []
    for i in range(DEPTH):
        j = i // 2
        if i % 2 == 0:
            layers.append((w["a_w_in"][j][:, :qkv_cols].astype(BF16), w["a_w_in"][j][:, qkv_cols:].astype(BF16), w["a_w_out"][j].astype(BF16)))
        else:
            layers.append(_ssd_prep_weights(w["b_w_in"][j], w["b_conv_w"][j], w["b_conv_b"][j], w["b_dt_bias"][j], w["b_a_log"][j],
                                            w["b_d"][j], w["b_norm_w"][j], w["b_w_out"][j]))
    saved = []
    for i in range(DEPTH):
        shift, scale, gate = mod[i:i + 1, :d], mod[i:i + 1, d:2 * d], mod[i:i + 1, 2 * d:]
        h = _modulate(x, scale, shift, f"l{i}_mod")
        if i % 2 == 0:
            out, sv = _attn_layer_fwd(h, *layers[i], i)
        else:
            out, sv = _ssd_layer_fwd(h, layers[i], i)
        xn = _resid_ln(x, out, gate, w["ln_g"][i:i + 1], w["ln_b"][i:i + 1], f"l{i}_ln")
        saved.append((x, h, out, sv))
        x = xn
    dx, lred = _loss_grad(x, tgt, "loss")
    loss = 0.5 * jnp.sum(lred[0]) / d
    dmod, g_ln_g, g_ln_b = [None] * DEPTH, [None] * DEPTH, [None] * DEPTH
    ga_in, ga_out = [None, None], [None, None]
    gb = [None, None]
    arrived = None
    for i in reversed(range(DEPTH)):
        j = i // 2
        xi, h, out, sv = saved[i]
        scale, gate = mod[i:i + 1, d:2 * d], mod[i:i + 1, 2 * d:]
        du, dy, red = _resid_ln_bwd(xi, out, dx, gate, w["ln_g"][i:i + 1], f"l{i}_lnbwd")
        g_ln_g[i], g_ln_b[i] = red[1], red[2]
        if i % 2 == 0:
            dhs, ga_in[j], ga_out[j] = _attn_layer_bwd(dy, h, sv, *layers[i], i)
        else:
            exch = ((), ())
            if early and j == 0:
                sh = _shards_of(dict(a_w_in=ga_in, a_w_out=ga_out, b_w_in=[None, gb[1][0]], b_w_out=[None, gb[1][7]]), 1)
                exch = (sh[:2], sh[2:])
            dh, gb[j], got = _ssd_layer_bwd(dy, h, sv, layers[i], i, exch)
            if early and j == 0:
                arrived = list(got[0]) + list(got[1])
            dhs = [dh]
        dx, red2 = _modulate_bwd(du, dhs, xi, scale, f"l{i}_modbwd")
        dmod[i] = jnp.concatenate([red2[1], red2[0], red[0]])
    grads = {"ln_g": jnp.stack(g_ln_g), "ln_b": jnp.stack(g_ln_b), "a_w_in": jnp.stack(ga_in), "a_w_out": jnp.stack(ga_out)}
    for k, n in enumerate(B_GRAD_NAMES):
        grads[n] = jnp.stack([gb[0][k], gb[1][k]])
    big = dict(a_w_in=ga_in, a_w_out=ga_out, b_w_in=[gb[0][0], gb[1][0]], b_w_out=[gb[0][7], gb[1][7]])
    return loss, dx, jnp.stack(dmod), grads, big, arrived


MESH = pl.DeviceIdType.MESH
ANY = pl.BlockSpec(memory_space=pl.ANY)
N_DEV = 8
N_SHARD = 4


def _flip(v, bit):
    return 1 - v if bit else v


def _all_gather8(v, name):
    def body(v_ref, o_ref, send_sems, recv_sems, local_sem):
        x, y, c = lax.axis_index("x"), lax.axis_index("y"), lax.axis_index("c")
        me = 4 * x + 2 * y + c
        local = pltpu.make_async_copy(v_ref, o_ref.at[me], local_sem)
        local.start()
        copies = []
        for k in range(1, N_DEV):
            peer = (_flip(x, k & 4), _flip(y, k & 2), _flip(c, k & 1))
            copies.append(pltpu.make_async_remote_copy(
                src_ref=v_ref, dst_ref=o_ref.at[me], send_sem=send_sems.at[k - 1], recv_sem=recv_sems.at[k - 1],
                device_id=peer, device_id_type=MESH))
        for cp in copies:
            cp.start()
        for cp in copies:
            cp.wait()
        local.wait()

    return pl.pallas_call(
        body, name=name, in_specs=[ANY], out_specs=ANY, out_shape=jax.ShapeDtypeStruct((N_DEV,) + v.shape, v.dtype),
        scratch_shapes=[pltpu.SemaphoreType.DMA((N_DEV - 1,)), pltpu.SemaphoreType.DMA((N_DEV - 1,)), pltpu.SemaphoreType.DMA],
    )(v)


def _shard_exchange(s_refs, o_refs, send_sems, recv_sems, local_sems):
    n = len(s_refs)
    x, y, c = lax.axis_index("x"), lax.axis_index("y"), lax.axis_index("c")
    m = 2 * x + y
    copies = [pltpu.make_async_copy(s_refs[a].at[m], o_refs[a].at[m], local_sems.at[a]) for a in range(n)]
    for k in range(1, N_SHARD):
        px, py = _flip(x, k & 2), _flip(y, k & 1)
        for a in range(n):
            i = (k - 1) * n + a
            copies.append(pltpu.make_async_remote_copy(
                src_ref=s_refs[a].at[2 * px + py], dst_ref=o_refs[a].at[m], send_sem=send_sems.at[i], recv_sem=recv_sems.at[i],
                device_id=(px, py, c), device_id_type=MESH))
    return copies


def _transpose_shards(srcs, name):
    n = len(srcs)
    n_rem = (N_SHARD - 1) * n

    def body(*refs):
        copies = _shard_exchange(refs[:n], refs[n:2 * n], *refs[2 * n:])
        for cp in copies:
            cp.start()
        for cp in copies:
            cp.wait()

    return pl.pallas_call(
        body, name=name, in_specs=[ANY] * n, out_specs=[ANY] * n, out_shape=[jax.ShapeDtypeStruct(s.shape, s.dtype) for s in srcs],
        scratch_shapes=[pltpu.SemaphoreType.DMA((n_rem,)), pltpu.SemaphoreType.DMA((n_rem,)), pltpu.SemaphoreType.DMA((n,))],
    )(*srcs)


def _gather_shards(src, name):
    rows = src.shape[0]
    half = rows // 2
    n_ici = N_SHARD - 1

    def body(s_ref, o_ref, send_sems, recv_sems, local_sem):
        x, y, c = lax.axis_index("x"), lax.axis_index("y"), lax.axis_index("c")
        m = 2 * x + y
        sibling = (x, y, 1 - c)
        my_half = pl.ds(pl.multiple_of(c * half, 16), half)
        its_half = pl.ds(pl.multiple_of((1 - c) * half, 16), half)
        local = pltpu.make_async_copy(s_ref, o_ref.at[m], local_sem)
        local.start()
        chips = [(_flip(x, k & 2), _flip(y, k & 1)) for k in range(1, N_SHARD)]

        def copy(sem, src_ref, dst_ref, to):
            return pltpu.make_async_remote_copy(src_ref=src_ref, dst_ref=dst_ref, send_sem=send_sems.at[sem],
                                                recv_sem=recv_sems.at[sem], device_id=to, device_id_type=MESH)

        first = [copy(i, s_ref.at[my_half], o_ref.at[m, my_half], (px, py, c)) for i, (px, py) in enumerate(chips)]
        for cp in first:
            cp.start()
        passed = []
        for i, (px, py) in enumerate(chips):
            landed = o_ref.at[2 * px + py, my_half]
            copy(i, landed, landed, (px, py, c)).wait_recv()
            passed.append(copy(n_ici + i, landed, landed, sibling))
            passed[-1].start()
        for i, (px, py) in enumerate(chips):
            from_sibling = o_ref.at[2 * px + py, its_half]
            copy(n_ici + i, from_sibling, from_sibling, sibling).wait_recv()
        for cp in first + passed:
            cp.wait_send()
        local.wait()

    return pl.pallas_call(
        body, name=name, in_specs=[ANY], out_specs=ANY, out_shape=jax.ShapeDtypeStruct((N_SHARD,) + src.shape, src.dtype),
        scratch_shapes=[pltpu.SemaphoreType.DMA((2 * n_ici,)), pltpu.SemaphoreType.DMA((2 * n_ici,)), pltpu.SemaphoreType.DMA],
    )(src)


def _swap_sibling(vs, name):
    n = len(vs)

    def body(*refs):
        v_refs, o_refs, (send_sems, recv_sems) = refs[:n], refs[n:2 * n], refs[2 * n:]
        x, y, c = lax.axis_index("x"), lax.axis_index("y"), lax.axis_index("c")
        copies = [pltpu.make_async_remote_copy(src_ref=v_refs[a], dst_ref=o_refs[a], send_sem=send_sems.at[a], recv_sem=recv_sems.at[a],
                                               device_id=(x, y, 1 - c), device_id_type=MESH) for a in range(n)]
        for cp in copies:
            cp.start()
        for cp in copies:
            cp.wait()

    return pl.pallas_call(
        body, name=name, in_specs=[ANY] * n, out_specs=[ANY] * n, out_shape=[jax.ShapeDtypeStruct(v.shape, v.dtype) for v in vs],
        scratch_shapes=[pltpu.SemaphoreType.DMA((n,)), pltpu.SemaphoreType.DMA((n,))],
    )(*vs)


def _row_tile(r, elems, step):
    ok = [t for t in range(step, r + 1, step) if r % t == 0 and t <= elems]
    return max(ok) if ok else r


def _sum_slots(a, name):
    n, r, cdim = a.shape
    tm = _row_tile(r, (4 << 20) // (cdim * 4 * (n + 1)), 16)

    def body(a_ref, o_ref):
        acc = a_ref[0].astype(F32)
        for k in range(1, n):
            acc = acc + a_ref[k].astype(F32)
        o_ref[...] = acc

    return pl.pallas_call(
        body, name=name, grid=(r // tm,), in_specs=[pl.BlockSpec((n, tm, cdim), lambda i: (0, i, 0))],
        out_specs=pl.BlockSpec((tm, cdim), lambda i: (i, 0)), out_shape=jax.ShapeDtypeStruct((r, cdim), F32),
        compiler_params=_cp("parallel"),
    )(a)


def _silu_rows(v, name):
    def body(v_ref, o_ref):
        o_ref[...] = _silu(v_ref[...])

    return pl.pallas_call(body, name=name, out_shape=jax.ShapeDtypeStruct(v.shape, F32))(v)


PACK_COLS = 1024


def _adamw(w, gs, m, v, name):
    r, cdim = w.shape
    tm = _row_tile(r, (1 << 18) // cdim, 8)
    c1 = 1.0 / (1.0 - ADAM_B1 ** ADAM_STEP)
    c2 = 1.0 / (1.0 - ADAM_B2 ** ADAM_STEP)
    ng = len(gs)

    def body(*refs):
        w_ref, g_refs, (m_ref, v_ref, g_ref, d_ref, nm_ref, nv_ref) = refs[0], refs[1:1 + ng], refs[1 + ng:]
        g = g_refs[0][...]
        for t in g_refs[1:]:
            g = g + t[...]
        mn = ADAM_B1 * m_ref[...] + (1.0 - ADAM_B1) * g
        vn = ADAM_B2 * v_ref[...] + (1.0 - ADAM_B2) * (g * g)
        g_ref[...] = g
        nm_ref[...] = mn
        nv_ref[...] = vn
        d_ref[...] = -ADAM_LR * ((mn * c1) / (jnp.sqrt(vn * c2) + ADAM_EPS) + ADAM_WD * w_ref[...])

    spec = pl.BlockSpec((tm, cdim), lambda i: (i, 0))
    return pl.pallas_call(
        body, name=name, grid=(r // tm,), in_specs=[spec] * (3 + ng), out_specs=[spec] * 4,
        out_shape=[jax.ShapeDtypeStruct(w.shape, F32)] * 4, compiler_params=_cp("parallel"),
    )(w, *gs, m, v)


def _rows(a):
    f = a.reshape(-1)
    pad = (-f.shape[0]) % PACK_COLS
    if pad:
        f = jnp.pad(f, (0, pad))
    return f.reshape(-1, PACK_COLS)


def _nrows(shape):
    return -(-int(np.prod(shape)) // PACK_COLS)


def _pack(parts, total_rows=None):
    p = jnp.concatenate([_rows(a) for a in parts], axis=0)
    if total_rows is not None and total_rows > p.shape[0]:
        p = jnp.pad(p, ((0, total_rows - p.shape[0]), (0, 0)))
    return p


def _unpack(p, shapes):
    out, r0 = [], 0
    for shp in shapes:
        n = int(np.prod(shp))
        nr = _nrows(shp)
        out.append(p[r0:r0 + nr].reshape(-1)[:n].reshape(shp))
        r0 += nr
    return out


def _unshard_cols(g):
    return jnp.concatenate([g[k] for k in range(N_SHARD)], axis=-1)


def _shard_cols(a):
    n = a.shape[-1] // N_SHARD
    return jnp.stack([a[..., k * n:(k + 1) * n] for k in range(N_SHARD)])


def _unshard_rows(g):
    return jnp.concatenate([g[k] for k in range(N_SHARD)], axis=1)


def _shard_rows(a):
    n = a.shape[1] // N_SHARD
    return jnp.stack([a[:, k * n:(k + 1) * n] for k in range(N_SHARD)])


W_NAMES = ("ada_w", "ada_b", "ln_g", "ln_b", "a_w_in", "a_w_out", "b_w_in", "b_conv_w", "b_conv_b", "b_dt_bias", "b_a_log", "b_d",
           "b_norm_w", "b_w_out")
BIG = ("a_w_in", "a_w_out", "b_w_in", "b_w_out")
SMALL = ("ada_b", "ln_g", "ln_b", "b_conv_w", "b_conv_b", "b_dt_bias", "b_a_log", "b_d", "b_norm_w")


def kernel(x, c, ada_w, ada_b, ln_g, ln_b, a_w_in, a_w_out, b_w_in, b_conv_w, b_conv_b, b_dt_bias, b_a_log, b_d, b_norm_w, b_w_out, loss_target, m_ada_w, m_ada_b, m_ln_g, m_ln_b, m_a_w_in, m_a_w_out, m_b_w_in, m_b_conv_w, m_b_conv_b, m_b_dt_bias, m_b_a_log, m_b_d, m_b_norm_w, m_b_w_out, v_ada_w, v_ada_b, v_ln_g, v_ln_b, v_a_w_in, v_a_w_out, v_b_w_in, v_b_conv_w, v_b_conv_b, v_b_dt_bias, v_b_a_log, v_b_d, v_b_norm_w, v_b_w_out):
    w = dict(ada_w=ada_w, ada_b=ada_b, ln_g=ln_g, ln_b=ln_b, a_w_in=a_w_in, a_w_out=a_w_out, b_w_in=b_w_in, b_conv_w=b_conv_w,
             b_conv_b=b_conv_b, b_dt_bias=b_dt_bias, b_a_log=b_a_log, b_d=b_d, b_norm_w=b_norm_w, b_w_out=b_w_out)
    mom = dict(ada_w=m_ada_w, ada_b=m_ada_b, ln_g=m_ln_g, ln_b=m_ln_b, a_w_in=m_a_w_in, a_w_out=m_a_w_out, b_w_in=m_b_w_in,
               b_conv_w=m_b_conv_w, b_conv_b=m_b_conv_b, b_dt_bias=m_b_dt_bias, b_a_log=m_b_a_log, b_d=m_b_d, b_norm_w=m_b_norm_w,
               b_w_out=m_b_w_out)
    var = dict(ada_w=v_ada_w, ada_b=v_ada_b, ln_g=v_ln_g, ln_b=v_ln_b, a_w_in=v_a_w_in, a_w_out=v_a_w_out, b_w_in=v_b_w_in,
               b_conv_w=v_b_conv_w, b_conv_b=v_b_conv_b, b_dt_bias=v_b_dt_bias, b_a_log=v_b_a_log, b_d=v_b_d, b_norm_w=v_b_norm_w,
               b_w_out=v_b_w_out)
    ax, ay, ac = lax.axis_index("x"), lax.axis_index("y"), lax.axis_index("c")
    me = 4 * ax + 2 * ay + ac
    shard = 2 * ax + ay
    d = x.shape[-1]
    dsh = ada_w.shape[-1]

    small_in = (c, b_conv_w, b_conv_b, b_norm_w)
    g0 = _all_gather8(_pack(small_in).reshape(-1, LANE), "gather_small_in").reshape(N_DEV, -1, PACK_COLS)
    per_dev = [_unpack(g0[k], [a.shape for a in small_in]) for k in range(N_DEV)]
    c_all = jnp.concatenate([p[0] for p in per_dev], axis=0)
    conv_w_full, conv_b_full, norm_w_full = (_unshard_cols([per_dev[2 * k][t] for k in range(N_SHARD)]) for t in (1, 2, 3))

    cond = _silu_rows(jnp.pad(c_all, ((0, 8), (0, 0))), "cond")
    bias = lax.dynamic_slice_in_dim(ada_b, shard * dsh, dsh, axis=1)
    part = jnp.stack([_mm(cond, ada_w[i], add=jnp.broadcast_to(bias[i], (16, dsh)), name=f"mod{i}")[:N_DEV] for i in range(DEPTH)])
    g1 = _all_gather8(part.reshape(-1, LANE), "gather_mod").reshape(N_DEV, DEPTH, N_DEV, dsh)
    mod_all = _unshard_cols([g1[2 * k] for k in range(N_SHARD)])
    mod = lax.dynamic_index_in_dim(mod_all, me, axis=1, keepdims=False)

    gw = _gather_shards(_pack([w[n] for n in BIG]).astype(BF16), "gather_weights")
    big_sh = [_unpack(gw[k], [w[n].shape for n in BIG]) for k in range(N_SHARD)]
    full = dict(
        ln_g=ln_g, ln_b=ln_b, b_dt_bias=b_dt_bias, b_a_log=b_a_log, b_d=b_d,
        b_conv_w=conv_w_full, b_conv_b=conv_b_full, b_norm_w=norm_w_full,
        a_w_in=_unshard_cols([s[0] for s in big_sh]), a_w_out=_unshard_rows([s[1] for s in big_sh]),
        b_w_in=_unshard_cols([s[2] for s in big_sh]), b_w_out=_unshard_rows([s[3] for s in big_sh]),
    )

    loss, grad_x, dmod, g, big, arrived1 = _local_step(x[0], loss_target[0], mod, full, early=True)

    arrived0 = _transpose_shards(_shards_of(big, 0), "scatter_grads")
    mine = [jnp.concatenate([_sum_slots(arrived0[a], f"sum0_{n}"), _sum_slots(arrived1[a], f"sum1_{n}")], axis=0)
            for a, n in enumerate(BIG)]
    theirs = _swap_sibling(mine, "swap_grads")

    small_g = (dmod, g["ln_g"], g["ln_b"], g["b_dt_bias"], g["b_a_log"], g["b_d"], g["b_conv_w"], g["b_conv_b"], g["b_norm_w"],
               loss.reshape(1))
    g2 = _all_gather8(_pack(small_g).reshape(-1, LANE), "gather_small_grads")
    tot = _unpack(_sum_slots(g2, "sum_small").reshape(-1, PACK_COLS), [a.shape for a in small_g])
    g_ada_b, g_ln_g, g_ln_b, g_dt_bias, g_a_log, g_d, g_conv_w, g_conv_b, g_norm_w, loss_sum = tot
    dmod_all = g2.reshape(N_DEV, -1)[:, :dmod.size].reshape(N_DEV, DEPTH, 3 * d)
    dmod_mine = lax.dynamic_slice_in_dim(dmod_all, shard * dsh, dsh, axis=2)
    g_ada_w = jnp.stack([_mm(cond, jnp.pad(dmod_mine[:, i], ((0, 8), (0, 0))), ta=True, name=f"dada{i}") for i in range(DEPTH)])
    csh = g_conv_w.shape[-1] // N_SHARD
    nsh = g_norm_w.shape[-1] // N_SHARD
    small_grads = dict(
        ada_w=g_ada_w, ada_b=g_ada_b, ln_g=g_ln_g, ln_b=g_ln_b, b_dt_bias=g_dt_bias, b_a_log=g_a_log, b_d=g_d,
        b_conv_w=lax.dynamic_slice_in_dim(g_conv_w, shard * csh, csh, axis=2),
        b_conv_b=lax.dynamic_slice_in_dim(g_conv_b, shard * csh, csh, axis=1),
        b_norm_w=lax.dynamic_slice_in_dim(g_norm_w, shard * nsh, nsh, axis=1),
    )

    by_name = [{}, {}, {}, {}]

    def update(n, gs):
        two_d = lambda t: t.reshape(-1, t.shape[-1])
        outs = _adamw(two_d(w[n]), [two_d(t) for t in gs], two_d(mom[n]), two_d(var[n]), f"adamw_{n}")
        for t, o in zip(by_name, outs):
            t[n] = o.reshape(w[n].shape)

    for i, n in enumerate(BIG):
        update(n, [mine[i], theirs[i]])
    update("ada_w", [small_grads["ada_w"]])
    rest = SMALL
    rows = -(-sum(_nrows(w[n].shape) for n in rest) // 8) * 8
    packed = _adamw(_pack([w[n] for n in rest], rows), [_pack([small_grads[n] for n in rest], rows)],
                    _pack([mom[n] for n in rest], rows), _pack([var[n] for n in rest], rows), "adamw_small")
    for t, p in zip(by_name, packed):
        t.update(zip(rest, _unpack(p, [w[n].shape for n in rest])))
    return (loss_sum.reshape(()), grad_x[None], *[t[n] for t in by_name for n in W_NAMES])
```

```python
import jax
import jax.numpy as jnp
import numpy as np
from jax import lax
from jax.experimental import pallas as pl
from jax.experimental.pallas import tpu as pltpu

F32 = jnp.float32
BF16 = jnp.bfloat16

DEPTH = 4
A_HEADS = 16
A_HEAD_DIM = 64
A_WIDTH = A_HEADS * A_HEAD_DIM
DILATIONS = (1, 4, 16)
A_RADIUS = 64
A_QBLOCK = 128
SSM_HEADS = 32
SSM_HEAD_DIM = 64
SSM_STATE = 128
SSM_GROUPS = 4
SSM_REP = SSM_HEADS // SSM_GROUPS
SSM_CONV = 5
SSM_CHUNK = 128
DEEPNORM_ALPHA = (2 * DEPTH) ** 0.25
LN_EPS = 1e-5
RMS_EPS = 1e-5
ADAM_LR, ADAM_B1, ADAM_B2, ADAM_EPS, ADAM_WD, ADAM_STEP = 0.001, 0.9, 0.999, 1e-08, 0.01, 10
VMEM_LIMIT = 56 * 1024 * 1024
LANE = 128


def _cp(*sem):
    return pltpu.CompilerParams(dimension_semantics=sem, vmem_limit_bytes=VMEM_LIMIT)


def _tile(dim, target):
    if dim <= target:
        return dim
    t = (target // LANE) * LANE
    while dim % t:
        t -= LANE
    return t


def _sigmoid(x):
    return 1.0 / (1.0 + jnp.exp(-x))


def _silu(x):
    return x * _sigmoid(x)


def _dsilu(x):
    s = _sigmoid(x)
    return s * (1.0 + x * (1.0 - s))


def _split3(x):
    a = x.astype(BF16)
    r = x - a.astype(F32)
    b = r.astype(BF16)
    c = (r - b.astype(F32)).astype(BF16)
    return a, b, c


def _dot(a, b, ca=1, cb=0):
    return lax.dot_general(a, b, (((ca,), (cb,)), ((), ())), preferred_element_type=F32)


def _dot_exact(m01, x):
    a, b, c = _split3(x)
    return _dot(m01, a) + _dot(m01, b) + _dot(m01, c)


def _mm(a, b, *, ta=False, tb=False, add=None, out_dtype=F32, name, tm=1024, tn=1024, tk=1024):
    m, k = (a.shape[1], a.shape[0]) if ta else a.shape
    n = b.shape[0] if tb else b.shape[1]
    assert (b.shape[1] if tb else b.shape[0]) == k
    tm, tn, tk = _tile(m, tm), _tile(n, tn), _tile(k, tk)
    nk = k // tk
    has_add = add is not None

    def body(*refs):
        if has_add:
            a_ref, b_ref, c_ref, o_ref, acc = refs
        else:
            a_ref, b_ref, o_ref, acc = refs
        kk = pl.program_id(2)
        part = _dot(a_ref[...].astype(BF16), b_ref[...].astype(BF16), 0 if ta else 1, 1 if tb else 0)

        def finish(r):
            if has_add:
                r = r + c_ref[...]
            o_ref[...] = r.astype(o_ref.dtype)

        if nk == 1:
            finish(part)
            return

        @pl.when(kk == 0)
        def _():
            acc[...] = part

        @pl.when((kk > 0) & (kk < nk - 1))
        def _():
            acc[...] += part

        @pl.when(kk == nk - 1)
        def _():
            finish(acc[...] + part)

    a_spec = pl.BlockSpec((tk, tm), lambda i, j, kk: (kk, i)) if ta else pl.BlockSpec((tm, tk), lambda i, j, kk: (i, kk))
    b_spec = pl.BlockSpec((tn, tk), lambda i, j, kk: (j, kk)) if tb else pl.BlockSpec((tk, tn), lambda i, j, kk: (kk, j))
    in_specs = [a_spec, b_spec]
    args = [a, b]
    if has_add:
        in_specs.append(pl.BlockSpec((tm, tn), lambda i, j, kk: (i, j)))
        args.append(add)
    return pl.pallas_call(
        body, name=name, grid=(m // tm, n // tn, nk), in_specs=in_specs,
        out_specs=pl.BlockSpec((tm, tn), lambda i, j, kk: (i, j)),
        out_shape=jax.ShapeDtypeStruct((m, n), out_dtype),
        scratch_shapes=[pltpu.VMEM((tm, tn) if nk > 1 else (8, LANE), F32)],
        compiler_params=_cp("parallel", "parallel", "arbitrary"),
    )(*args)


ROWS = 512


def _row_spec(tm, d):
    return pl.BlockSpec((tm, d), lambda i: (i, 0))


def _vec_spec(d, rows=1):
    return pl.BlockSpec((rows, d), lambda i: (0, 0))


def _modulate(x, scale, shift, name):
    s, d = x.shape
    tm = min(ROWS, s)

    def body(x_ref, sc_ref, sh_ref, o_ref):
        o_ref[...] = (x_ref[...] * (1.0 + sc_ref[...]) + sh_ref[...]).astype(BF16)

    return pl.pallas_call(
        body, name=name, grid=(s // tm,), in_specs=[_row_spec(tm, d), _vec_spec(d), _vec_spec(d)],
        out_specs=_row_spec(tm, d), out_shape=jax.ShapeDtypeStruct((s, d), BF16), compiler_params=_cp("parallel"),
    )(x, scale, shift)


def _resid_ln(x, y, gate, g, b, name):
    s, d = x.shape
    tm = min(ROWS, s)

    def body(x_ref, y_ref, gt_ref, g_ref, b_ref, o_ref):
        u = DEEPNORM_ALPHA * x_ref[...] + gt_ref[...] * y_ref[...]
        mu = jnp.mean(u, axis=1, keepdims=True)
        uc = u - mu
        var = jnp.mean(uc * uc, axis=1, keepdims=True)
        o_ref[...] = uc * lax.rsqrt(var + LN_EPS) * g_ref[...] + b_ref[...]

    return pl.pallas_call(
        body, name=name, grid=(s // tm,),
        in_specs=[_row_spec(tm, d), _row_spec(tm, d), _vec_spec(d), _vec_spec(d), _vec_spec(d)],
        out_specs=_row_spec(tm, d), out_shape=jax.ShapeDtypeStruct((s, d), F32), compiler_params=_cp("parallel"),
    )(x, y, gate, g, b)


def _resid_ln_bwd(x, y, dxn, gate, g, name):
    s, d = x.shape
    tm = min(ROWS, s)

    def body(x_ref, y_ref, dxn_ref, gt_ref, g_ref, du_ref, dy_ref, red_ref):
        @pl.when(pl.program_id(0) == 0)
        def _():
            red_ref[...] = jnp.zeros_like(red_ref)

        yv = y_ref[...]
        u = DEEPNORM_ALPHA * x_ref[...] + gt_ref[...] * yv
        mu = jnp.mean(u, axis=1, keepdims=True)
        uc = u - mu
        var = jnp.mean(uc * uc, axis=1, keepdims=True)
        rstd = lax.rsqrt(var + LN_EPS)
        xhat = uc * rstd
        dxnv = dxn_ref[...]
        dxh = dxnv * g_ref[...]
        du = rstd * (dxh - jnp.mean(dxh, axis=1, keepdims=True) - xhat * jnp.mean(dxh * xhat, axis=1, keepdims=True))
        du_ref[...] = du
        dy_ref[...] = (du * gt_ref[...]).astype(BF16)
        red_ref[0:1, :] += jnp.sum(du * yv, axis=0, keepdims=True)
        red_ref[1:2, :] += jnp.sum(dxnv * xhat, axis=0, keepdims=True)
        red_ref[2:3, :] += jnp.sum(dxnv, axis=0, keepdims=True)

    return pl.pallas_call(
        body, name=name, grid=(s // tm,),
        in_specs=[_row_spec(tm, d), _row_spec(tm, d), _row_spec(tm, d), _vec_spec(d), _vec_spec(d)],
        out_specs=[_row_spec(tm, d), _row_spec(tm, d), _vec_spec(d, 8)],
        out_shape=[jax.ShapeDtypeStruct((s, d), F32), jax.ShapeDtypeStruct((s, d), BF16), jax.ShapeDtypeStruct((8, d), F32)],
        compiler_params=_cp("arbitrary"),
    )(x, y, dxn, gate, g)


def _modulate_bwd(du, dhs, x, scale, name):
    s, d = x.shape
    tm = min(ROWS, s)
    n = len(dhs)

    def body(*refs):
        du_ref, dh_refs, (x_ref, sc_ref, dx_ref, red_ref) = refs[0], refs[1:1 + n], refs[1 + n:]

        @pl.when(pl.program_id(0) == 0)
        def _():
            red_ref[...] = jnp.zeros_like(red_ref)

        dhv = dh_refs[0][...]
        for t in dh_refs[1:]:
            dhv = dhv + t[...]
        dx_ref[...] = DEEPNORM_ALPHA * du_ref[...] + dhv * (1.0 + sc_ref[...])
        red_ref[0:1, :] += jnp.sum(dhv * x_ref[...], axis=0, keepdims=True)
        red_ref[1:2, :] += jnp.sum(dhv, axis=0, keepdims=True)

    return pl.pallas_call(
        body, name=name, grid=(s // tm,),
        in_specs=[_row_spec(tm, d)] * (n + 2) + [_vec_spec(d)],
        out_specs=[_row_spec(tm, d), _vec_spec(d, 8)],
        out_shape=[jax.ShapeDtypeStruct((s, d), F32), jax.ShapeDtypeStruct((8, d), F32)],
        compiler_params=_cp("arbitrary"),
    )(du, *dhs, x, scale)


def _loss_grad(xf, tgt, name):
    s, d = xf.shape
    tm = min(ROWS, s)

    def body(x_ref, t_ref, dx_ref, red_ref):
        @pl.when(pl.program_id(0) == 0)
        def _():
            red_ref[...] = jnp.zeros_like(red_ref)

        e = x_ref[...] - t_ref[...]
        dx_ref[...] = e * (1.0 / d)
        red_ref[0:1, :] += jnp.sum(e * e, axis=0, keepdims=True)

    return pl.pallas_call(
        body, name=name, grid=(s // tm,), in_specs=[_row_spec(tm, d), _row_spec(tm, d)],
        out_specs=[_row_spec(tm, d), _vec_spec(d, 8)],
        out_shape=[jax.ShapeDtypeStruct((s, d), F32), jax.ShapeDtypeStruct((8, d), F32)],
        compiler_params=_cp("arbitrary"),
    )(xf, tgt)


QKV_COLS = 3 * 3 * A_WIDTH


SLOPES = tuple(float(2.0 ** (-8.0 * (h + 1.0) / A_HEADS)) for h in range(A_HEADS))
FAR = 1e30
HEAD_COLS = tuple(slice(h * A_HEAD_DIM, (h + 1) * A_HEAD_DIM) for h in range(A_HEADS))


def _band_dist(n, length, dil, span_rows):
    shape = (2 * A_QBLOCK, A_QBLOCK) if span_rows else (A_QBLOCK, 2 * A_QBLOCK)
    r = lax.broadcasted_iota(jnp.int32, shape, 0)
    c = lax.broadcasted_iota(jnp.int32, shape, 1)
    sp, ce = (r, c) if span_rows else (c, r)
    delta = sp - A_RADIUS - ce
    pos = n * A_QBLOCK - A_RADIUS + sp
    valid = (jnp.abs(delta) <= A_RADIUS) & (pos >= 0) & (pos < length)
    return jnp.where(valid, jnp.abs(delta).astype(F32) * float(dil), FAR)


def _span_specs(col, nb64):
    def mk(i):
        return pl.BlockSpec((64, A_WIDTH), lambda r, n: (r * nb64 + jnp.clip(2 * n - 1 + i, 0, nb64 - 1), col))
    return [mk(i) for i in range(4)]


def _to_residue(t, dil):
    if dil == 1:
        return t
    s, c = t.shape
    return t.reshape(s // dil, dil, c).transpose(1, 0, 2).reshape(s, c)


def _from_residue(t, dil):
    if dil == 1:
        return t
    s, c = t.shape
    return t.reshape(dil, s // dil, c).transpose(1, 0, 2).reshape(s, c)


def _cat(refs):
    return jnp.concatenate([t[...] for t in refs], axis=0)


def _head_expander():
    r = lax.broadcasted_iota(jnp.int32, (A_HEADS, A_WIDTH), 0)
    c = lax.broadcasted_iota(jnp.int32, (A_HEADS, A_WIDTH), 1)
    return ((c >= r * A_HEAD_DIM) & (c < (r + 1) * A_HEAD_DIM)).astype(BF16)


def _to_lanes(x16, e):
    a, b, c = _split3(x16)
    return _dot(a, e) + _dot(b, e) + _dot(c, e)


def _per_head_sum(x, e):
    a, b, c = _split3(x)
    return _dot(a, e, 1, 1) + _dot(b, e, 1, 1) + _dot(c, e, 1, 1)


def _pair_low_lanes():
    return lax.broadcasted_iota(jnp.int32, (A_QBLOCK, LANE), 1) < A_HEAD_DIM


def _top_rows():
    return lax.broadcasted_iota(jnp.int32, (2 * A_QBLOCK, 1), 0) < A_QBLOCK


def _block_diag(v, low):
    zero = jnp.zeros_like(v)
    return jnp.concatenate([jnp.where(low, v, zero), jnp.where(low, zero, v)], axis=0)


def _attn_fwd(qkv, g, name):
    s = qkv.shape[0]
    dil = DILATIONS[g]
    length = s // dil
    nblk = length // A_QBLOCK

    def body(q_ref, k0, k1, k2, k3, v0, v1, v2, v3, o_ref, l_ref):
        dist = _band_dist(pl.program_id(1), length, dil, False)
        kk = _cat((k0, k1, k2, k3))
        vv = _cat((v0, v1, v2, v3))
        low = _pair_low_lanes()
        top = _top_rows()
        dist2 = jnp.concatenate([dist, dist], axis=0)
        for hp in range(A_HEADS // 2):
            ls = slice(hp * LANE, (hp + 1) * LANE)
            qp, kp, vp = q_ref[:, ls], kk[:, ls], vv[:, ls]
            sc = _dot(_block_diag(qp, low), kp, 1, 1) * 0.125 - jnp.where(top, SLOPES[2 * hp], SLOPES[2 * hp + 1]) * dist2
            m = jnp.max(sc, axis=1, keepdims=True)
            p = jnp.exp(sc - m)
            z = jnp.sum(p, axis=1, keepdims=True)
            o2 = _dot(p.astype(BF16), vp) / z
            lse2 = m + jnp.log(z)
            l_ref[:, 2 * hp:2 * hp + 1] = lse2[:A_QBLOCK]
            l_ref[:, 2 * hp + 1:2 * hp + 2] = lse2[A_QBLOCK:]
            o_ref[:, ls] = jnp.where(low, o2[:A_QBLOCK], o2[A_QBLOCK:])

    qspec = pl.BlockSpec((A_QBLOCK, A_WIDTH), lambda r, n: (r * nblk + n, 0))
    lspec = pl.BlockSpec((A_QBLOCK, A_HEADS), lambda r, n: (r * nblk + n, 0))
    return pl.pallas_call(
        body, name=name, grid=(dil, nblk), in_specs=[qspec] + _span_specs(1, 2 * nblk) + _span_specs(2, 2 * nblk),
        out_specs=[qspec, lspec],
        out_shape=[jax.ShapeDtypeStruct((s, A_WIDTH), F32), jax.ShapeDtypeStruct((s, A_HEADS), F32)],
        compiler_params=_cp("parallel", "parallel"),
    )(*([qkv] * 9))


def _attn_merge(os_, ls_, gate, name):
    s, w = gate.shape
    tm = min(ROWS, s)

    def body(o0, o1, o2, l0, l1, l2, g_ref, y_ref, o_ref, l_ref):
        a, b, c = l0[...], l1[...], l2[...]
        m = jnp.maximum(jnp.maximum(a, b), c)
        ea, eb, ec = jnp.exp(a - m), jnp.exp(b - m), jnp.exp(c - m)
        z = ea + eb + ec
        l_ref[...] = m + jnp.log(z)
        e = _head_expander()
        o = _to_lanes(ea / z, e) * o0[...] + _to_lanes(eb / z, e) * o1[...] + _to_lanes(ec / z, e) * o2[...]
        o_ref[...] = o
        y_ref[...] = (o * _silu(g_ref[...])).astype(BF16)

    rs = _row_spec(tm, w)
    ls = _row_spec(tm, A_HEADS)
    return pl.pallas_call(
        body, name=name, grid=(s // tm,), in_specs=[rs] * 3 + [ls] * 3 + [rs], out_specs=[rs, rs, ls],
        out_shape=[jax.ShapeDtypeStruct((s, w), BF16), jax.ShapeDtypeStruct((s, w), F32), jax.ShapeDtypeStruct((s, A_HEADS), F32)],
        compiler_params=_cp("parallel"),
    )(*os_, *ls_, gate)


def _attn_gate_bwd(dyy, o, gate, name):
    s, w = gate.shape
    tm = min(ROWS, s)

    def body(dy_ref, o_ref, g_ref, do_ref, dg_ref, dl_ref):
        dyv, ov, gv = dy_ref[...], o_ref[...], g_ref[...]
        do = dyv * _silu(gv)
        do_ref[...] = do.astype(BF16)
        dg_ref[...] = (dyv * ov * _dsilu(gv)).astype(BF16)
        dl_ref[...] = _per_head_sum(do * ov, _head_expander())

    rs = _row_spec(tm, w)
    return pl.pallas_call(
        body, name=name, grid=(s // tm,), in_specs=[rs] * 3, out_specs=[rs, rs, _row_spec(tm, A_HEADS)],
        out_shape=[jax.ShapeDtypeStruct((s, w), BF16), jax.ShapeDtypeStruct((s, w), BF16), jax.ShapeDtypeStruct((s, A_HEADS), F32)],
        compiler_params=_cp("parallel"),
    )(dyy, o, gate)


def _ride_along(ex_in, ex_out, sems, first, last):
    @pl.when(first)
    def _():
        for cp in _shard_exchange(ex_in, ex_out, *sems):
            cp.start()

    @pl.when(last)
    def _():
        for cp in _shard_exchange(ex_in, ex_out, *sems):
            cp.wait()


def _ride_along_specs(exch):
    n = len(exch)
    n_rem = (N_SHARD - 1) * n
    sems = [pltpu.SemaphoreType.DMA((n_rem,)), pltpu.SemaphoreType.DMA((n_rem,)), pltpu.SemaphoreType.DMA((n,))] if n else []
    return [ANY] * n, [jax.ShapeDtypeStruct(t.shape, t.dtype) for t in exch], sems


def _attn_bwd(qkv, do, lse, delta, g, name, exch=()):
    s = qkv.shape[0]
    dil = DILATIONS[g]
    length = s // dil
    nblk = length // A_QBLOCK
    n_ex = len(exch)

    def rows(t16):
        return jnp.pad(t16.reshape(dil, length, A_HEADS).transpose(0, 2, 1), ((0, 0), (0, 0), (A_RADIUS, A_RADIUS)))

    def body(*refs):
        (q0, q1, q2, q3, k0, k1, k2, k3, v0, v1, v2, v3, d0, d1, d2, d3, lc_ref, ec_ref, la, lb, ea, eb), refs = refs[:22], refs[22:]
        o_ref = refs[n_ex]
        if n_ex:
            r, n = pl.program_id(0), pl.program_id(1)
            _ride_along(refs[:n_ex], refs[n_ex + 1:2 * n_ex + 1], refs[2 * n_ex + 1:], (r == 0) & (n == 0), (r == dil - 1) & (n == nblk - 1))
        dist = _band_dist(pl.program_id(1), length, dil, False)
        qq, kk, vv, dd = _cat((q0, q1, q2, q3)), _cat((k0, k1, k2, k3)), _cat((v0, v1, v2, v3)), _cat((d0, d1, d2, d3))
        lse_r = jnp.concatenate([la[...], lb[...]], axis=1)
        dlt_r = jnp.concatenate([ea[...], eb[...]], axis=1)
        low = _pair_low_lanes()
        top = _top_rows()
        dist2 = jnp.concatenate([dist, dist], axis=0)
        centre = slice(A_RADIUS, A_RADIUS + A_QBLOCK)
        for hp in range(A_HEADS // 2):
            ls = slice(hp * LANE, (hp + 1) * LANE)
            qs, ks, vs, ds_ = qq[:, ls], kk[:, ls], vv[:, ls], dd[:, ls]
            qn, kn, vn, dn = qs[centre], ks[centre], vs[centre], ds_[centre]
            h0, h1 = 2 * hp, 2 * hp + 1
            bias = jnp.where(top, SLOPES[h0], SLOPES[h1]) * dist2
            lc = jnp.concatenate([lc_ref[:, h0:h0 + 1], lc_ref[:, h1:h1 + 1]], axis=0)
            ec = jnp.concatenate([ec_ref[:, h0:h0 + 1], ec_ref[:, h1:h1 + 1]], axis=0)
            lr = jnp.where(top, lse_r[h0:h0 + 1, :], lse_r[h1:h1 + 1, :])
            er = jnp.where(top, dlt_r[h0:h0 + 1, :], dlt_r[h1:h1 + 1, :])
            p = jnp.exp(_dot(_block_diag(qn, low), ks, 1, 1) * 0.125 - bias - lc)
            dsc = p * (_dot(_block_diag(dn, low), vs, 1, 1) - ec)
            dq = _dot(dsc.astype(BF16), ks)
            pt = jnp.exp(_dot(_block_diag(kn, low), qs, 1, 1) * 0.125 - bias - lr)
            dst = pt * (_dot(_block_diag(vn, low), ds_, 1, 1) - er)
            dk = _dot(dst.astype(BF16), qs)
            dv = _dot(pt.astype(BF16), ds_)
            merge = lambda t: jnp.where(low, t[:A_QBLOCK], t[A_QBLOCK:])
            o_ref[:, ls] = (merge(dq) * 0.125).astype(BF16)
            o_ref[:, A_WIDTH + hp * LANE:A_WIDTH + (hp + 1) * LANE] = (merge(dk) * 0.125).astype(BF16)
            o_ref[:, 2 * A_WIDTH + hp * LANE:2 * A_WIDTH + (hp + 1) * LANE] = merge(dv).astype(BF16)

    nb64 = 2 * nblk
    dspecs = _span_specs(0, nb64)
    cspec = pl.BlockSpec((A_QBLOCK, A_HEADS), lambda r, n: (r * nblk + n, 0))
    rspecs = [pl.BlockSpec((None, A_HEADS, A_QBLOCK), lambda r, n: (r, 0, n)), pl.BlockSpec((None, A_HEADS, A_QBLOCK), lambda r, n: (r, 0, n + 1))]
    lse_r, dlt_r = rows(lse), rows(delta)
    ex_specs, ex_shapes, sems = _ride_along_specs(exch)
    outs = pl.pallas_call(
        body, name=name, grid=(dil, nblk),
        in_specs=_span_specs(0, nb64) + _span_specs(1, nb64) + _span_specs(2, nb64) + dspecs + [cspec, cspec] + rspecs * 2 + ex_specs,
        out_specs=[pl.BlockSpec((A_QBLOCK, 3 * A_WIDTH), lambda r, n: (r * nblk + n, 0))] + ex_specs,
        out_shape=[jax.ShapeDtypeStruct((s, 3 * A_WIDTH), BF16)] + ex_shapes,
        scratch_shapes=sems,
        compiler_params=_cp(*(("arbitrary", "arbitrary") if n_ex else ("parallel", "parallel"))),
    )(*([qkv] * 12), *([do] * 4), lse, delta, lse_r, lse_r, dlt_r, dlt_r, *exch)
    return (outs[0], outs[1:]) if n_ex else outs[0]


def _attn_layer_fwd(h, w_qkv, w_gate, w_out, li):
    nm = lambda t: f"a{li}_{t}"
    gate = _mm(h, w_gate, name=nm("gate"))
    hs, qkvs, os_, ls_ = [], [], [], []
    for g, dil in enumerate(DILATIONS):
        hg = _to_residue(h, dil)
        qkv = _mm(hg, w_qkv[:, g * 3 * A_WIDTH:(g + 1) * 3 * A_WIDTH], out_dtype=BF16, name=nm(f"qkv{g}"))
        o, l = _attn_fwd(qkv, g, nm(f"attn{g}"))
        hs.append(hg)
        qkvs.append(qkv)
        os_.append(_from_residue(o, dil))
        ls_.append(_from_residue(l, dil))
    y, o, lse = _attn_merge(os_, ls_, gate, nm("merge"))
    out = _mm(y, w_out, name=nm("out"))
    return out, (hs, qkvs, gate, y, o, lse)


def _attn_layer_bwd(dy, h, saved, w_qkv, w_gate, w_out, li, exch=()):
    nm = lambda t: f"a{li}_{t}"
    hs, qkvs, gate, y, o, lse = saved
    g_w_out = _mm(y, dy, ta=True, out_dtype=BF16, name=nm("dwout"))
    dyy = _mm(dy, w_out, tb=True, name=nm("dyy"))
    do, dgate, delta = _attn_gate_bwd(dyy, o, gate, nm("gatebwd"))
    dhs, dws, arrived = [], [], ()
    for g, dil in enumerate(DILATIONS):
        dqkv = _attn_bwd(qkvs[g], _to_residue(do, dil), _to_residue(lse, dil), _to_residue(delta, dil), g, nm(f"attnbwd{g}"),
                         exch if g == 0 else ())
        if g == 0 and exch:
            dqkv, arrived = dqkv
        wg = w_qkv[:, g * 3 * A_WIDTH:(g + 1) * 3 * A_WIDTH]
        dws.append(_mm(hs[g], dqkv, ta=True, out_dtype=BF16, name=nm(f"dwqkv{g}")))
        add = _mm(dgate, w_gate, tb=True, name=nm("dh_gate")) if g == 0 else None
        dhs.append(_from_residue(_mm(dqkv, wg, tb=True, add=add, name=nm(f"dh_qkv{g}")), dil))
    g_w_in = jnp.concatenate(dws + [_mm(h, dgate, ta=True, out_dtype=BF16, name=nm("dwgate"))], axis=1)
    return dhs, g_w_in, g_w_out, arrived


SSM_INNER = SSM_HEADS * SSM_HEAD_DIM
SSM_BC = SSM_GROUPS * SSM_STATE
SSM_CONV_DIM = SSM_INNER + 2 * SSM_BC
GW = SSM_REP * SSM_HEAD_DIM
T = SSM_CHUNK
HALO = 8


def _conv_specs(tm, tn, s, col=lambda j: j):
    nb8 = s // HALO
    cur = pl.BlockSpec((tm, tn), lambda j, i: (i, col(j)))
    prev = pl.BlockSpec((HALO, tn), lambda j, i: (jnp.maximum(i * (tm // HALO) - 1, 0), col(j)))
    nxt = pl.BlockSpec((HALO, tn), lambda j, i: (jnp.minimum((i + 1) * (tm // HALO), nb8 - 1), col(j)))
    return [prev, cur, nxt]


def _extend(prev_ref, cur_ref, nxt_ref, i, nrow):
    p = jnp.where(i == 0, 0.0, prev_ref[...])
    n = jnp.where(i == nrow - 1, 0.0, nxt_ref[...])
    return jnp.concatenate([p, cur_ref[...], n], axis=0)


def _shift_rows(ext, off, tm):
    rows = ext.shape[0]
    return pltpu.roll(ext, (-off) % rows, 0)[HALO:HALO + tm]


def _conv_fwd(xraw, w, b, name):
    s, cdim = xraw.shape
    tm, tn = min(256, s), 1024
    nrow = s // tm

    def body(p_ref, c_ref, n_ref, w_ref, b_ref, pre_ref, act_ref):
        ext = _extend(p_ref, c_ref, n_ref, pl.program_id(1), nrow)
        acc = jnp.broadcast_to(b_ref[...], (tm, tn))
        for k in range(SSM_CONV):
            acc = acc + w_ref[k:k + 1, :] * _shift_rows(ext, k - SSM_CONV // 2, tm)
        pre_ref[...] = acc
        act_ref[...] = _silu(acc)

    prev, cur, nxt = _conv_specs(tm, tn, s)
    return pl.pallas_call(
        body, name=name, grid=(cdim // tn, nrow),
        in_specs=[prev, cur, nxt, pl.BlockSpec((SSM_CONV, tn), lambda j, i: (0, j)), pl.BlockSpec((1, tn), lambda j, i: (0, j))],
        out_specs=[cur, cur], out_shape=[jax.ShapeDtypeStruct((s, cdim), F32)] * 2,
        compiler_params=_cp("parallel", "parallel"),
    )(xraw, xraw, xraw, w, b)


def _conv_bwd(dx, db, dc, pre, xraw, w, name):
    s, cdim = xraw.shape
    tm, tn = min(256, s), 1024
    nrow = s // tm
    nx = dx.shape[1] // tn

    def body(xp, xc, xn, bp, bc, bn, cp, cc, cn, pp, pc, pn, x_ref, w_ref, o_ref, red_ref):
        j, i = pl.program_id(0), pl.program_id(1)

        @pl.when(i == 0)
        def _():
            red_ref[...] = jnp.zeros_like(red_ref)

        bcext = jnp.concatenate([_extend(bp, bc, bn, i, nrow), _extend(cp, cc, cn, i, nrow)], axis=1)
        dact = jnp.where(j < nx, _extend(xp, xc, xn, i, nrow), bcext)
        dpre = dact * _dsilu(_extend(pp, pc, pn, i, nrow))
        xv = x_ref[...]
        acc = jnp.zeros((tm, tn), F32)
        for k in range(SSM_CONV):
            sk = _shift_rows(dpre, SSM_CONV // 2 - k, tm)
            acc = acc + w_ref[k:k + 1, :] * sk
            red_ref[k:k + 1, :] += jnp.sum(sk * xv, axis=0, keepdims=True)
        red_ref[SSM_CONV:SSM_CONV + 1, :] += jnp.sum(dpre[HALO:HALO + tm], axis=0, keepdims=True)
        o_ref[...] = acc.astype(BF16)

    half = tn // 2
    cur = pl.BlockSpec((tm, tn), lambda j, i: (i, j))
    return pl.pallas_call(
        body, name=name, grid=(cdim // tn, nrow),
        in_specs=_conv_specs(tm, tn, s, lambda j: jnp.minimum(j, nx - 1)) + _conv_specs(tm, half, s, lambda j: 0) * 2
        + _conv_specs(tm, tn, s) + [cur, pl.BlockSpec((SSM_CONV, tn), lambda j, i: (0, j))],
        out_specs=[cur, pl.BlockSpec((8, tn), lambda j, i: (0, j))],
        out_shape=[jax.ShapeDtypeStruct((s, cdim), BF16), jax.ShapeDtypeStruct((8, cdim), F32)],
        compiler_params=_cp("parallel", "arbitrary"),
    )(dx, dx, dx, db, db, db, dc, dc, dc, pre, pre, pre, xraw, w)


def _tri(lower):
    r = lax.broadcasted_iota(jnp.int32, (T, T), 0)
    c = lax.broadcasted_iota(jnp.int32, (T, T), 1)
    return (r >= c) if lower else (r <= c)


def _softplus(x):
    return jnp.maximum(x, 0.0) + jnp.log(1.0 + jnp.exp(-jnp.abs(x)))


def _dt_prep(dt_raw, bias, a_log, name):
    s = dt_raw.shape[0]
    nc = s // T

    def body(r_ref, b_ref, a_ref, dt_ref, cum_ref, cumt_ref):
        dt = _softplus(r_ref[...] + b_ref[...])
        da = dt * (-jnp.exp(a_ref[...]))
        pre = _dot_exact(_tri(True).astype(BF16), da)
        suf = _dot_exact(_tri(False).astype(BF16), da)
        lane = lax.broadcasted_iota(jnp.int32, (T, LANE), 1)
        cum = jnp.where(lane < SSM_HEADS, pre, suf)
        dt_ref[...] = dt
        cum_ref[...] = cum
        cumt_ref[...] = cum.T

    blk = pl.BlockSpec((T, LANE), lambda c: (c, 0))
    vec = pl.BlockSpec((1, LANE), lambda c: (0, 0))
    return pl.pallas_call(
        body, name=name, grid=(nc,), in_specs=[blk, vec, vec],
        out_specs=[blk, blk, pl.BlockSpec((None, LANE, T), lambda c: (c, 0, 0))],
        out_shape=[jax.ShapeDtypeStruct((s, LANE), F32), jax.ShapeDtypeStruct((s, LANE), F32), jax.ShapeDtypeStruct((nc, LANE, T), F32)],
        compiler_params=_cp("parallel"),
    )(dt_raw, bias, a_log)


def _by_group(t):
    s = t.shape[0]
    return t[:, :2 * SSM_HEADS].reshape(s, 2 * SSM_GROUPS, SSM_REP).transpose(1, 0, 2)


def _from_group(tf, tb):
    s = tf.shape[1]
    t = jnp.concatenate([tf, tb], axis=0).transpose(1, 0, 2).reshape(s, 2 * SSM_HEADS)
    return jnp.pad(t, ((0, 0), (0, LANE - 2 * SSM_HEADS)))


def _decay_mats(acol, arow, rev):
    after = _tri(not rev)
    return jnp.where(after, jnp.exp(jnp.where(after, acol - arow, 0.0)), 0.0)


PAIRS = SSM_REP // 2


def _low_lanes():
    return lax.broadcasted_iota(jnp.int32, (T, LANE), 1) < SSM_HEAD_DIM


CPS = 4
TB = CPS * T


def _scan_specs(rev, ci):
    nxb = SSM_INNER // LANE
    kofs = SSM_GROUPS if rev else 0
    return [
        pl.BlockSpec((TB, GW), lambda g, c: (ci(c), g)),
        pl.BlockSpec((TB, LANE), lambda g, c: (ci(c), nxb + g)),
        pl.BlockSpec((TB, LANE), lambda g, c: (ci(c), nxb + SSM_GROUPS + g)),
        pl.BlockSpec((None, TB, SSM_REP), lambda g, c: (kofs + g, ci(c), 0)),
        pl.BlockSpec((None, TB, SSM_REP), lambda g, c: (kofs + g, ci(c), 0)),
        pl.BlockSpec((None, CPS, SSM_REP, T), lambda g, c: (kofs + g, ci(c), 0, 0)),
    ]


def _chunk_rows(q):
    return pl.ds(q * T, T)


def _pair_lanes(ref, p, low):
    return jnp.where(low, ref[:, 2 * p:2 * p + 1], ref[:, 2 * p + 1:2 * p + 2])


def _ssd_scan(xbc, dtk, cumk, cumtk, rev, name):
    s = xbc.shape[0]
    nc = s // T
    nb = nc // CPS
    last = 0 if rev else T - 1
    ci = (lambda c: nb - 1 - c) if rev else (lambda c: c)

    def body(x_ref, b_ref, c_ref, dt_ref, cum_ref, cumt_ref, y_ref, st_ref, state):
        @pl.when(pl.program_id(1) == 0)
        def _():
            state[...] = jnp.zeros_like(state)

        for q in (reversed(range(CPS)) if rev else range(CPS)):
            rows = _chunk_rows(q)
            chunk(x_ref.at[rows], b_ref.at[rows], c_ref.at[rows], dt_ref.at[rows], cum_ref.at[rows], cumt_ref.at[q],
                  y_ref.at[rows], st_ref.at[q], state)

    def chunk(x_ref, b_ref, c_ref, dt_ref, cum_ref, cumt_ref, y_ref, st_ref, state):
        bm = b_ref[...]
        cm = c_ref[...].astype(BF16)
        cb = _dot(cm, bm.astype(BF16), 1, 1)
        bt = bm.T.astype(BF16)
        low = _low_lanes()
        for p in range(PAIRS):
            ls = slice(p * LANE, (p + 1) * LANE)
            acum = _pair_lanes(cum_ref, p, low)
            u = x_ref[:, ls] * _pair_lanes(dt_ref, p, low)
            tot = acum[last:last + 1, :]
            m = [(cb * _decay_mats(cum_ref[:, r:r + 1], cumt_ref[r:r + 1, :], rev)).astype(BF16) for r in (2 * p, 2 * p + 1)]
            st = state[p]
            st_ref[p] = st
            yd = _dot(jnp.concatenate(m, axis=1), _block_diag(u.astype(BF16), low))
            yo = jnp.exp(acum) * _dot(cm, st.astype(BF16))
            y_ref[:, ls] = yd + yo
            state[p] = jnp.exp(tot) * st + _dot(bt, (jnp.exp(tot - acum) * u).astype(BF16))

    return pl.pallas_call(
        body, name=name, grid=(SSM_GROUPS, nb), in_specs=_scan_specs(rev, ci),
        out_specs=[
            pl.BlockSpec((TB, GW), lambda g, c: (ci(c), g)),
            pl.BlockSpec((CPS, PAIRS, SSM_STATE, LANE), lambda g, c: (ci(c), g, 0, 0)),
        ],
        out_shape=[jax.ShapeDtypeStruct((s, SSM_INNER), F32), jax.ShapeDtypeStruct((nc, SSM_HEADS // 2, SSM_STATE, LANE), F32)],
        scratch_shapes=[pltpu.VMEM((PAIRS, SSM_STATE, LANE), F32)],
        compiler_params=_cp("parallel", "arbitrary"),
    )(xbc, xbc, xbc, dtk, cumk, cumtk)


def _ssd_scan_bwd(xbc, dtk, cumk, cumtk, dy, states, dvec, prev, rev, name, exch=()):
    s = xbc.shape[0]
    nc = s // T
    nb = nc // CPS
    last = 0 if rev else T - 1
    ci = (lambda c: c) if rev else (lambda c: nb - 1 - c)
    has_prev = prev is not None
    n_in = 12 if has_prev else 9
    n_ex = len(exch)

    def body(*refs):
        ins, ex_in, refs = refs[:n_in], refs[n_in:n_in + n_ex], refs[n_in + n_ex:]
        outs, ex_out, scratch = refs[:5], refs[5:5 + n_ex], refs[5 + n_ex:]
        if n_ex:
            scratch, sems = scratch[:4], scratch[4:]
            g, c = pl.program_id(0), pl.program_id(1)
            _ride_along(ex_in, ex_out, sems, (g == 0) & (c == 0), (g == SSM_GROUPS - 1) & (c == nb - 1))

        @pl.when(pl.program_id(1) == 0)
        def _():
            scratch[0][...] = jnp.zeros_like(scratch[0])

        for q in (range(CPS) if rev else reversed(range(CPS))):
            rows = _chunk_rows(q)
            cut = lambda t: t.at[rows]
            x_ref, b_ref, c_ref, dt_ref, cum_ref, cumt_ref, dy_ref, st_ref, dv_ref = ins[:9]
            sub = [cut(x_ref), cut(b_ref), cut(c_ref), cut(dt_ref), cut(cum_ref), cumt_ref.at[q], cut(dy_ref), st_ref.at[q], dv_ref]
            chunk(*sub, *[cut(t) for t in ins[9:]], *[cut(t) for t in outs], *scratch)

    def chunk(*refs):
        x_ref, b_ref, c_ref, dt_ref, cum_ref, cumt_ref, dy_ref, st_ref, dv_ref = refs[:9]
        refs = refs[9:]
        if has_prev:
            pdx, pdb, pdc = refs[:3]
            refs = refs[3:]
        dx_ref, db_ref, dc_ref, ddt_ref, dda_ref, dstate, rs_buf, in_buf, k_buf = refs
        rs_buf[...] = jnp.zeros_like(rs_buf)
        in_buf[...] = jnp.zeros_like(in_buf)
        k_buf[...] = jnp.zeros_like(k_buf)
        bm = b_ref[...].astype(BF16)
        cm = c_ref[...].astype(BF16)
        cbt = _dot(bm, cm, 1, 1)
        cb = _dot(cm, bm, 1, 1)
        ct = c_ref[...].T.astype(BF16)
        after = _tri(not rev)
        before = _tri(rev)
        from_k = before.astype(BF16)
        ri = lax.broadcasted_iota(jnp.int32, (T, T), 0)
        cj = lax.broadcasted_iota(jnp.int32, (T, T), 1)
        strictly_before = (cj > ri) if rev else (cj < ri)
        dcb = jnp.zeros((T, T), F32)
        dc_acc = jnp.zeros((T, SSM_STATE), F32)
        db_acc = jnp.zeros((T, SSM_STATE), F32)
        low = _low_lanes()
        ri2 = lax.broadcasted_iota(jnp.int32, (LANE, LANE), 0)
        cj2 = lax.broadcasted_iota(jnp.int32, (LANE, LANE), 1)
        halves = ((ri2 < SSM_HEAD_DIM) == (cj2 == 0)) & (cj2 < 2)
        halves = halves.astype(BF16)

        def head_sums(v):
            hi = v.astype(BF16)
            lo = (v - hi.astype(F32)).astype(BF16)
            return _dot(hi, halves) + _dot(lo, halves)

        for p in range(PAIRS):
            ls = slice(p * LANE, (p + 1) * LANE)
            c2 = slice(2 * p, 2 * p + 2)
            lm, lmt = [], []
            for r in (2 * p, 2 * p + 1):
                acol = cum_ref[:, r:r + 1]
                arow = cumt_ref[r:r + 1, :]
                lm.append(jnp.where(after, jnp.exp(jnp.where(after, acol - arow, 0.0)), 0.0))
                lmt.append(jnp.where(before, jnp.exp(jnp.where(before, arow - acol, 0.0)), 0.0))
            acum = _pair_lanes(cum_ref, p, low)
            tot = acum[last:last + 1, :]
            dtl = _pair_lanes(dt_ref, p, low)
            xl = x_ref[:, ls]
            u = xl * dtl
            ub = u.astype(BF16)
            dyl = dy_ref[:, ls]
            dyb = dyl.astype(BF16)
            st = st_ref[p]
            stb = st.astype(BF16)
            dst = dstate[p]
            dstb = dst.astype(BF16)
            dec = jnp.exp(tot - acum)
            eac = jnp.exp(acum)
            etot = jnp.exp(tot)
            du_off = dec * _dot(bm, dstb)
            mt = jnp.concatenate([(cbt * lmt[0]).astype(BF16), (cbt * lmt[1]).astype(BF16)], axis=1)
            du = _dot(mt, _block_diag(dyb, low)) + du_off
            zero = jnp.zeros_like(dyb)
            gl = [_dot(jnp.where(low, dyb, zero), ub, 1, 1) * lm[0], _dot(jnp.where(low, zero, dyb), ub, 1, 1) * lm[1]]
            dcb = dcb + gl[0] + gl[1]
            dc_acc = dc_acc + _dot((eac * dyl).astype(BF16), stb, 1, 1)
            db_acc = db_acc + _dot((dec * u).astype(BF16), dstb, 1, 1)
            w = jnp.concatenate([(gl[0] * cb).astype(BF16), (gl[1] * cb).astype(BF16)], axis=1)
            crossing = _dot(from_k, w)
            for j in range(2):
                cr = jnp.where(strictly_before, crossing[:, j * T:(j + 1) * T], 0.0)
                in_buf[:, 2 * p + j:2 * p + j + 1] = jnp.sum(cr, axis=1, keepdims=True)
            y_off = eac * _dot(cm, stb)
            udu = u * du_off
            rs_buf[:, c2] = head_sums(dyl * y_off - udu)[:, 0:2]
            col = jnp.sum(dst * (etot * st) + udu, axis=0, keepdims=True)
            k_buf[0:1, c2] = head_sums(jnp.broadcast_to(col, (8, LANE)))[0:1, 0:2]
            ddt_ref[:, c2] = head_sums(du * xl)[:, 0:2]
            dx = du * dtl
            if has_prev:
                dx = dx + pdx[:, ls]
            else:
                dx = dx + dyl * dv_ref[:, ls]
            dx_ref[:, ls] = dx
            dstate[p] = etot * dst + _dot(ct, (eac * dyl).astype(BF16))
        dda = in_buf[...] + _dot_exact(from_k, rs_buf[...]) + k_buf[0:1, :]
        dda_ref[...] = dda[:, :SSM_REP]
        dcbb = dcb.astype(BF16)
        dc = dc_acc + _dot(dcbb, bm)
        db = db_acc + _dot(dcbb, cm, 0, 0)
        if has_prev:
            dc = dc + pdc[...]
            db = db + pdb[...]
        dc_ref[...] = dc
        db_ref[...] = db

    xspec = pl.BlockSpec((TB, GW), lambda g, c: (ci(c), g))
    gspec = pl.BlockSpec((TB, LANE), lambda g, c: (ci(c), g))
    in_specs = _scan_specs(rev, ci) + [
        xspec,
        pl.BlockSpec((CPS, PAIRS, SSM_STATE, LANE), lambda g, c: (ci(c), g, 0, 0)),
        pl.BlockSpec((1, GW), lambda g, c: (0, g)),
    ]
    args = [xbc, xbc, xbc, dtk, cumk, cumtk, dy, states, dvec]
    if has_prev:
        in_specs += [xspec, gspec, gspec]
        args += list(prev)
    ospec8 = pl.BlockSpec((None, TB, SSM_REP), lambda g, c: (g, ci(c), 0))
    ex_specs, ex_shapes, sems = _ride_along_specs(exch)
    outs = pl.pallas_call(
        body, name=name, grid=(SSM_GROUPS, nb), in_specs=in_specs + ex_specs,
        out_specs=[xspec, gspec, gspec, ospec8, ospec8] + ex_specs,
        out_shape=[jax.ShapeDtypeStruct((s, SSM_INNER), F32), jax.ShapeDtypeStruct((s, SSM_BC), F32), jax.ShapeDtypeStruct((s, SSM_BC), F32),
                   jax.ShapeDtypeStruct((SSM_GROUPS, s, SSM_REP), F32), jax.ShapeDtypeStruct((SSM_GROUPS, s, SSM_REP), F32)]
        + ex_shapes,
        scratch_shapes=[pltpu.VMEM((PAIRS, SSM_STATE, LANE), F32), pltpu.VMEM((T, LANE), F32), pltpu.VMEM((T, LANE), F32),
                        pltpu.VMEM((8, LANE), F32)] + sems,
        compiler_params=_cp("arbitrary" if n_ex else "parallel", "arbitrary"),
    )(*args, *exch)
    return (outs[:5], outs[5:]) if n_ex else outs


def _ssd_post(yf, yb, xbc, z, dvec, nw, name):
    s = z.shape[0]
    tm = min(256, s)

    def body(yf_ref, yb_ref, x_ref, z_ref, dv_ref, nw_ref, o_ref):
        ys = yf_ref[...] + yb_ref[...] + dv_ref[...] * x_ref[...]
        yg = ys * _silu(z_ref[...])
        ms = jnp.mean(yg * yg, axis=1, keepdims=True)
        o_ref[...] = (yg * lax.rsqrt(ms + RMS_EPS) * nw_ref[...]).astype(BF16)

    rs = _row_spec(tm, SSM_INNER)
    vs = _vec_spec(SSM_INNER)
    return pl.pallas_call(
        body, name=name, grid=(s // tm,), in_specs=[rs, rs, rs, rs, vs, vs], out_specs=rs,
        out_shape=jax.ShapeDtypeStruct((s, SSM_INNER), BF16), compiler_params=_cp("parallel"),
    )(yf, yb, xbc, z, dvec, nw)


def _ssd_post_bwd(dyn, yf, yb, xbc, z, dvec, nw, name):
    s = z.shape[0]
    tm = min(256, s)

    def body(dyn_ref, yf_ref, yb_ref, x_ref, z_ref, dv_ref, nw_ref, dys_ref, dz_ref, red_ref):
        @pl.when(pl.program_id(0) == 0)
        def _():
            red_ref[...] = jnp.zeros_like(red_ref)

        xv, zv = x_ref[...], z_ref[...]
        ys = yf_ref[...] + yb_ref[...] + dv_ref[...] * xv
        sz = _silu(zv)
        yg = ys * sz
        rstd = lax.rsqrt(jnp.mean(yg * yg, axis=1, keepdims=True) + RMS_EPS)
        yhat = yg * rstd
        dynv = dyn_ref[...]
        dyh = dynv * nw_ref[...]
        dyg = rstd * (dyh - yhat * jnp.mean(dyh * yhat, axis=1, keepdims=True))
        dys = dyg * sz
        dys_ref[...] = dys
        dz_ref[...] = (dyg * ys * _dsilu(zv)).astype(BF16)
        red_ref[0:1, :] += jnp.sum(dynv * yhat, axis=0, keepdims=True)
        red_ref[1:2, :] += jnp.sum(dys * xv, axis=0, keepdims=True)

    rs = _row_spec(tm, SSM_INNER)
    vs = _vec_spec(SSM_INNER)
    return pl.pallas_call(
        body, name=name, grid=(s // tm,), in_specs=[rs, rs, rs, rs, rs, vs, vs],
        out_specs=[rs, rs, _vec_spec(SSM_INNER, 8)],
        out_shape=[jax.ShapeDtypeStruct((s, SSM_INNER), F32), jax.ShapeDtypeStruct((s, SSM_INNER), BF16), jax.ShapeDtypeStruct((8, SSM_INNER), F32)],
        compiler_params=_cp("arbitrary"),
    )(dyn, yf, yb, xbc, z, dvec, nw)


def _dt_bwd(dt_raw, bias, a_log, dt, ddt, dda, name):
    s = dt_raw.shape[0]
    tm = min(1024, s)

    def body(r_ref, b_ref, a_ref, dt_ref, ddt_ref, dda_ref, o_ref, red_ref):
        @pl.when(pl.program_id(0) == 0)
        def _():
            red_ref[...] = jnp.zeros_like(red_ref)

        a = -jnp.exp(a_ref[...])
        ddav = dda_ref[...]
        draw = (ddt_ref[...] + a * ddav) * _sigmoid(r_ref[...] + b_ref[...])
        o_ref[...] = draw.astype(BF16)
        red_ref[0:1, :] += jnp.sum(draw, axis=0, keepdims=True)
        red_ref[1:2, :] += a * jnp.sum(ddav * dt_ref[...], axis=0, keepdims=True)

    rs = _row_spec(tm, LANE)
    vs = _vec_spec(LANE)
    return pl.pallas_call(
        body, name=name, grid=(s // tm,), in_specs=[rs, vs, vs, rs, rs, rs], out_specs=[rs, _vec_spec(LANE, 8)],
        out_shape=[jax.ShapeDtypeStruct((s, LANE), BF16), jax.ShapeDtypeStruct((8, LANE), F32)],
        compiler_params=_cp("arbitrary"),
    )(dt_raw, bias, a_log, dt, ddt, dda)


def _pad_lanes(v):
    v = v.reshape(1, -1)
    return jnp.pad(v, ((0, 0), (0, LANE - v.shape[1])))


def _ssd_prep_weights(w_in, conv_w, conv_b, dt_bias, a_log, d_skip, norm_w, w_out):
    return dict(
        w_z=w_in[:, :SSM_INNER].astype(BF16),
        w_xbc=w_in[:, SSM_INNER:SSM_INNER + SSM_CONV_DIM].astype(BF16),
        w_dt=jnp.pad(w_in[:, SSM_INNER + SSM_CONV_DIM:], ((0, 0), (0, LANE - 2 * SSM_HEADS))).astype(BF16),
        conv_w=conv_w, conv_b=conv_b.reshape(1, -1), bias=_pad_lanes(dt_bias), a_log=_pad_lanes(a_log),
        dvec=jnp.repeat(d_skip, SSM_HEAD_DIM).reshape(1, -1), nw=norm_w.reshape(1, -1), w_out=w_out.astype(BF16),
    )


def _ssd_layer_fwd(h, w, li):
    nm = lambda t: f"b{li}_{t}"
    z = _mm(h, w["w_z"], name=nm("z"))
    xraw = _mm(h, w["w_xbc"], name=nm("xbc"))
    dt_raw = _mm(h, w["w_dt"], name=nm("dt"))
    pre, xbc = _conv_fwd(xraw, w["conv_w"], w["conv_b"], nm("conv"))
    dt, cum, cumt = _dt_prep(dt_raw, w["bias"], w["a_log"], nm("dtprep"))
    nc = cumt.shape[0]
    dtk, cumk = _by_group(dt), _by_group(cum)
    cumtk = cumt[:, :2 * SSM_HEADS].reshape(nc, 2 * SSM_GROUPS, SSM_REP, T).transpose(1, 0, 2, 3)
    yf, stf = _ssd_scan(xbc, dtk, cumk, cumtk, False, nm("scan_f"))
    yb, stb = _ssd_scan(xbc, dtk, cumk, cumtk, True, nm("scan_b"))
    yn = _ssd_post(yf, yb, xbc, z, w["dvec"], w["nw"], nm("post"))
    out = _mm(yn, w["w_out"], name=nm("out"))
    return out, (z, xraw, dt_raw, pre, xbc, dt, dtk, cumk, cumtk, yf, stf, yb, stb, yn)


def _ssd_layer_bwd(dy, h, saved, w, li, exch=((), ())):
    nm = lambda t: f"b{li}_{t}"
    z, xraw, dt_raw, pre, xbc, dt, dtk, cumk, cumtk, yf, stf, yb, stb, yn = saved
    g_w_out = _mm(yn, dy, ta=True, out_dtype=BF16, name=nm("dwout"))
    dyn = _mm(dy, w["w_out"], tb=True, name=nm("dyn"))
    dys, dz, pred = _ssd_post_bwd(dyn, yf, yb, xbc, z, w["dvec"], w["nw"], nm("postbwd"))
    arrived = [(), ()]
    res = _ssd_scan_bwd(xbc, dtk, cumk, cumtk, dys, stf, w["dvec"], None, False, nm("scanbwd_f"), exch[0])
    if exch[0]:
        res, arrived[0] = res
    dx1, db1, dc1, ddt_f, dda_f = res
    res = _ssd_scan_bwd(xbc, dtk, cumk, cumtk, dys, stb, w["dvec"], (dx1, db1, dc1), True, nm("scanbwd_b"), exch[1])
    if exch[1]:
        res, arrived[1] = res
    dx, db, dc, ddt_b, dda_b = res
    dxraw, cred = _conv_bwd(dx, db, dc, pre, xraw, w["conv_w"], nm("convbwd"))
    draw, dred = _dt_bwd(dt_raw, w["bias"], w["a_log"], dt, _from_group(ddt_f, ddt_b), _from_group(dda_f, dda_b), nm("dtbwd"))
    dh = _mm(dz, w["w_z"], tb=True, name=nm("dh_z"))
    dh = _mm(dxraw, w["w_xbc"], tb=True, add=dh, name=nm("dh_xbc"))
    dh = _mm(draw, w["w_dt"], tb=True, add=dh, name=nm("dh_dt"))
    g_w_in = jnp.concatenate([_mm(h, dz, ta=True, out_dtype=BF16, name=nm("dwz")), _mm(h, dxraw, ta=True, out_dtype=BF16, name=nm("dwxbc")),
                              _mm(h, draw, ta=True, out_dtype=BF16, name=nm("dwdt"))[:, :2 * SSM_HEADS]], axis=1)
    grads = (g_w_in, cred[:SSM_CONV], cred[SSM_CONV], dred[0, :2 * SSM_HEADS].reshape(2, SSM_HEADS),
             dred[1, :2 * SSM_HEADS].reshape(2, SSM_HEADS), pred[1].reshape(SSM_HEADS, SSM_HEAD_DIM).sum(axis=1), pred[0], g_w_out)
    return dh, grads, arrived


B_GRAD_NAMES = ("b_w_in", "b_conv_w", "b_conv_b", "b_dt_bias", "b_a_log", "b_d", "b_norm_w", "b_w_out")


def _shards_of(big, j):
    return [_shard_cols(big["a_w_in"][j]), big["a_w_out"][j].reshape(N_SHARD, -1, big["a_w_out"][j].shape[-1]),
            _shard_cols(big["b_w_in"][j]), big["b_w_out"][j].reshape(N_SHARD, -1, big["b_w_out"][j].shape[-1])]


def _local_step(x, tgt, mod, w, early=False):
    d = x.shape[1]
    qkv_cols = QKV_COLS
    layers = []
    for i in range(DEPTH):
        j = i // 2
        if i % 2 == 0:
            layers.append((w["a_w_in"][j][:, :qkv_cols].astype(BF16), w["a_w_in"][j][:, qkv_cols:].astype(BF16), w["a_w_out"][j].astype(BF16)))
        else:
            layers.append(_ssd_prep_weights(w["b_w_in"][j], w["b_conv_w"][j], w["b_conv_b"][j], w["b_dt_bias"][j], w["b_a_log"][j],
                                            w["b_d"][j], w["b_norm_w"][j], w["b_w_out"][j]))
    saved = []
    for i in range(DEPTH):
        shift, scale, gate = mod[i:i + 1, :d], mod[i:i + 1, d:2 * d], mod[i:i + 1, 2 * d:]
        h = _modulate(x, scale, shift, f"l{i}_mod")
        if i % 2 == 0:
            out, sv = _attn_layer_fwd(h, *layers[i], i)
        else:
            out, sv = _ssd_layer_fwd(h, layers[i], i)
        xn = _resid_ln(x, out, gate, w["ln_g"][i:i + 1], w["ln_b"][i:i + 1], f"l{i}_ln")
        saved.append((x, h, out, sv))
        x = xn
    dx, lred = _loss_grad(x, tgt, "loss")
    loss = 0.5 * jnp.sum(lred[0]) / d
    dmod, g_ln_g, g_ln_b = [None] * DEPTH, [None] * DEPTH, [None] * DEPTH
    ga_in, ga_out = [None, None], [None, None]
    gb = [None, None]
    arrived = None
    for i in reversed(range(DEPTH)):
        j = i // 2
        xi, h, out, sv = saved[i]
        scale, gate = mod[i:i + 1, d:2 * d], mod[i:i + 1, 2 * d:]
        du, dy, red = _resid_ln_bwd(xi, out, dx, gate, w["ln_g"][i:i + 1], f"l{i}_lnbwd")
        g_ln_g[i], g_ln_b[i] = red[1], red[2]
        if i % 2 == 0:
            exch = ()
            if early and j == 0:
                exch = [_shard_cols(gb[0][0]), gb[0][7].reshape(N_SHARD, -1, gb[0][7].shape[-1])]
            dhs, ga_in[j], ga_out[j], got = _attn_layer_bwd(dy, h, sv, *layers[i], i, exch)
            if early and j == 0:
                arrived = (list(got), arrived)
        else:
            exch = ((), ())
            if early and j == 0:
                sh = _shards_of(dict(a_w_in=ga_in, a_w_out=ga_out, b_w_in=[None, gb[1][0]], b_w_out=[None, gb[1][7]]), 1)
                exch = (sh[:2], sh[2:])
            dh, gb[j], got = _ssd_layer_bwd(dy, h, sv, layers[i], i, exch)
            if early and j == 0:
                arrived = list(got[0]) + list(got[1])
            dhs = [dh]
        dx, red2 = _modulate_bwd(du, dhs, xi, scale, f"l{i}_modbwd")
        dmod[i] = jnp.concatenate([red2[1], red2[0], red[0]])
    grads = {"ln_g": jnp.stack(g_ln_g), "ln_b": jnp.stack(g_ln_b), "a_w_in": jnp.stack(ga_in), "a_w_out": jnp.stack(ga_out)}
    for k, n in enumerate(B_GRAD_NAMES):
        grads[n] = jnp.stack([gb[0][k], gb[1][k]])
    big = dict(a_w_in=ga_in, a_w_out=ga_out, b_w_in=[gb[0][0], gb[1][0]], b_w_out=[gb[0][7], gb[1][7]])
    return loss, dx, jnp.stack(dmod), grads, big, arrived


MESH = pl.DeviceIdType.MESH
ANY = pl.BlockSpec(memory_space=pl.ANY)
N_DEV = 8
N_SHARD = 4


def _flip(v, bit):
    return 1 - v if bit else v


def _all_gather8(v, name):
    def body(v_ref, o_ref, send_sems, recv_sems, local_sem):
        x, y, c = lax.axis_index("x"), lax.axis_index("y"), lax.axis_index("c")
        me = 4 * x + 2 * y + c
        local = pltpu.make_async_copy(v_ref, o_ref.at[me], local_sem)
        local.start()
        copies = []
        for k in range(1, N_DEV):
            peer = (_flip(x, k & 4), _flip(y, k & 2), _flip(c, k & 1))
            copies.append(pltpu.make_async_remote_copy(
                src_ref=v_ref, dst_ref=o_ref.at[me], send_sem=send_sems.at[k - 1], recv_sem=recv_sems.at[k - 1],
                device_id=peer, device_id_type=MESH))
        for cp in copies:
            cp.start()
        for cp in copies:
            cp.wait()
        local.wait()

    return pl.pallas_call(
        body, name=name, in_specs=[ANY], out_specs=ANY, out_shape=jax.ShapeDtypeStruct((N_DEV,) + v.shape, v.dtype),
        scratch_shapes=[pltpu.SemaphoreType.DMA((N_DEV - 1,)), pltpu.SemaphoreType.DMA((N_DEV - 1,)), pltpu.SemaphoreType.DMA],
    )(v)


def _shard_exchange(s_refs, o_refs, send_sems, recv_sems, local_sems):
    n = len(s_refs)
    x, y, c = lax.axis_index("x"), lax.axis_index("y"), lax.axis_index("c")
    m = 2 * x + y
    copies = [pltpu.make_async_copy(s_refs[a].at[m], o_refs[a].at[m], local_sems.at[a]) for a in range(n)]
    for k in range(1, N_SHARD):
        px, py = _flip(x, k & 2), _flip(y, k & 1)
        for a in range(n):
            i = (k - 1) * n + a
            copies.append(pltpu.make_async_remote_copy(
                src_ref=s_refs[a].at[2 * px + py], dst_ref=o_refs[a].at[m], send_sem=send_sems.at[i], recv_sem=recv_sems.at[i],
                device_id=(px, py, c), device_id_type=MESH))
    return copies


def _transpose_shards(srcs, name):
    n = len(srcs)
    n_rem = (N_SHARD - 1) * n

    def body(*refs):
        copies = _shard_exchange(refs[:n], refs[n:2 * n], *refs[2 * n:])
        for cp in copies:
            cp.start()
        for cp in copies:
            cp.wait()

    return pl.pallas_call(
        body, name=name, in_specs=[ANY] * n, out_specs=[ANY] * n, out_shape=[jax.ShapeDtypeStruct(s.shape, s.dtype) for s in srcs],
        scratch_shapes=[pltpu.SemaphoreType.DMA((n_rem,)), pltpu.SemaphoreType.DMA((n_rem,)), pltpu.SemaphoreType.DMA((n,))],
    )(*srcs)


def _gather_shards(src, name):
    rows = src.shape[0]
    half = rows // 2
    n_ici = N_SHARD - 1

    def body(s_ref, o_ref, send_sems, recv_sems, local_sem):
        x, y, c = lax.axis_index("x"), lax.axis_index("y"), lax.axis_index("c")
        m = 2 * x + y
        sibling = (x, y, 1 - c)
        my_half = pl.ds(pl.multiple_of(c * half, 16), half)
        its_half = pl.ds(pl.multiple_of((1 - c) * half, 16), half)
        local = pltpu.make_async_copy(s_ref, o_ref.at[m], local_sem)
        local.start()
        chips = [(_flip(x, k & 2), _flip(y, k & 1)) for k in range(1, N_SHARD)]

        def copy(sem, src_ref, dst_ref, to):
            return pltpu.make_async_remote_copy(src_ref=src_ref, dst_ref=dst_ref, send_sem=send_sems.at[sem],
                                                recv_sem=recv_sems.at[sem], device_id=to, device_id_type=MESH)

        first = [copy(i, s_ref.at[my_half], o_ref.at[m, my_half], (px, py, c)) for i, (px, py) in enumerate(chips)]
        for cp in first:
            cp.start()
        passed = []
        for i, (px, py) in enumerate(chips):
            landed = o_ref.at[2 * px + py, my_half]
            copy(i, landed, landed, (px, py, c)).wait_recv()
            passed.append(copy(n_ici + i, landed, landed, sibling))
            passed[-1].start()
        for i, (px, py) in enumerate(chips):
            from_sibling = o_ref.at[2 * px + py, its_half]
            copy(n_ici + i, from_sibling, from_sibling, sibling).wait_recv()
        for cp in first + passed:
            cp.wait_send()
        local.wait()

    return pl.pallas_call(
        body, name=name, in_specs=[ANY], out_specs=ANY, out_shape=jax.ShapeDtypeStruct((N_SHARD,) + src.shape, src.dtype),
        scratch_shapes=[pltpu.SemaphoreType.DMA((2 * n_ici,)), pltpu.SemaphoreType.DMA((2 * n_ici,)), pltpu.SemaphoreType.DMA],
    )(src)


def _swap_sibling(vs, name):
    n = len(vs)

    def body(*refs):
        v_refs, o_refs, (send_sems, recv_sems) = refs[:n], refs[n:2 * n], refs[2 * n:]
        x, y, c = lax.axis_index("x"), lax.axis_index("y"), lax.axis_index("c")
        copies = [pltpu.make_async_remote_copy(src_ref=v_refs[a], dst_ref=o_refs[a], send_sem=send_sems.at[a], recv_sem=recv_sems.at[a],
                                               device_id=(x, y, 1 - c), device_id_type=MESH) for a in range(n)]
        for cp in copies:
            cp.start()
        for cp in copies:
            cp.wait()

    return pl.pallas_call(
        body, name=name, in_specs=[ANY] * n, out_specs=[ANY] * n, out_shape=[jax.ShapeDtypeStruct(v.shape, v.dtype) for v in vs],
        scratch_shapes=[pltpu.SemaphoreType.DMA((n,)), pltpu.SemaphoreType.DMA((n,))],
    )(*vs)


def _row_tile(r, elems, step):
    ok = [t for t in range(step, r + 1, step) if r % t == 0 and t <= elems]
    return max(ok) if ok else r


def _sum_slots(a, name):
    n, r, cdim = a.shape
    tm = _row_tile(r, (4 << 20) // (cdim * 4 * (n + 1)), 16)

    def body(a_ref, o_ref):
        acc = a_ref[0].astype(F32)
        for k in range(1, n):
            acc = acc + a_ref[k].astype(F32)
        o_ref[...] = acc

    return pl.pallas_call(
        body, name=name, grid=(r // tm,), in_specs=[pl.BlockSpec((n, tm, cdim), lambda i: (0, i, 0))],
        out_specs=pl.BlockSpec((tm, cdim), lambda i: (i, 0)), out_shape=jax.ShapeDtypeStruct((r, cdim), F32),
        compiler_params=_cp("parallel"),
    )(a)


def _silu_rows(v, name):
    def body(v_ref, o_ref):
        o_ref[...] = _silu(v_ref[...])

    return pl.pallas_call(body, name=name, out_shape=jax.ShapeDtypeStruct(v.shape, F32))(v)


PACK_COLS = 1024


def _adamw(w, gs, m, v, name):
    r, cdim = w.shape
    tm = _row_tile(r, (1 << 18) // cdim, 8)
    c1 = 1.0 / (1.0 - ADAM_B1 ** ADAM_STEP)
    c2 = 1.0 / (1.0 - ADAM_B2 ** ADAM_STEP)
    ng = len(gs)

    def body(*refs):
        w_ref, g_refs, (m_ref, v_ref, g_ref, d_ref, nm_ref, nv_ref) = refs[0], refs[1:1 + ng], refs[1 + ng:]
        g = g_refs[0][...]
        for t in g_refs[1:]:
            g = g + t[...]
        mn = ADAM_B1 * m_ref[...] + (1.0 - ADAM_B1) * g
        vn = ADAM_B2 * v_ref[...] + (1.0 - ADAM_B2) * (g * g)
        g_ref[...] = g
        nm_ref[...] = mn
        nv_ref[...] = vn
        d_ref[...] = -ADAM_LR * ((mn * c1) / (jnp.sqrt(vn * c2) + ADAM_EPS) + ADAM_WD * w_ref[...])

    spec = pl.BlockSpec((tm, cdim), lambda i: (i, 0))
    return pl.pallas_call(
        body, name=name, grid=(r // tm,), in_specs=[spec] * (3 + ng), out_specs=[spec] * 4,
        out_shape=[jax.ShapeDtypeStruct(w.shape, F32)] * 4, compiler_params=_cp("parallel"),
    )(w, *gs, m, v)


def _rows(a):
    f = a.reshape(-1)
    pad = (-f.shape[0]) % PACK_COLS
    if pad:
        f = jnp.pad(f, (0, pad))
    return f.reshape(-1, PACK_COLS)


def _nrows(shape):
    return -(-int(np.prod(shape)) // PACK_COLS)


def _pack(parts, total_rows=None):
    p = jnp.concatenate([_rows(a) for a in parts], axis=0)
    if total_rows is not None and total_rows > p.shape[0]:
        p = jnp.pad(p, ((0, total_rows - p.shape[0]), (0, 0)))
    return p


def _unpack(p, shapes):
    out, r0 = [], 0
    for shp in shapes:
        n = int(np.prod(shp))
        nr = _nrows(shp)
        out.append(p[r0:r0 + nr].reshape(-1)[:n].reshape(shp))
        r0 += nr
    return out


def _unshard_cols(g):
    return jnp.concatenate([g[k] for k in range(N_SHARD)], axis=-1)


def _shard_cols(a):
    n = a.shape[-1] // N_SHARD
    return jnp.stack([a[..., k * n:(k + 1) * n] for k in range(N_SHARD)])


def _unshard_rows(g):
    return jnp.concatenate([g[k] for k in range(N_SHARD)], axis=1)


def _shard_rows(a):
    n = a.shape[1] // N_SHARD
    return jnp.stack([a[:, k * n:(k + 1) * n] for k in range(N_SHARD)])


W_NAMES = ("ada_w", "ada_b", "ln_g", "ln_b", "a_w_in", "a_w_out", "b_w_in", "b_conv_w", "b_conv_b", "b_dt_bias", "b_a_log", "b_d",
           "b_norm_w", "b_w_out")
BIG = ("a_w_in", "a_w_out", "b_w_in", "b_w_out")
SMALL = ("ada_b", "ln_g", "ln_b", "b_conv_w", "b_conv_b", "b_dt_bias", "b_a_log", "b_d", "b_norm_w")


def kernel(x, c, ada_w, ada_b, ln_g, ln_b, a_w_in, a_w_out, b_w_in, b_conv_w, b_conv_b, b_dt_bias, b_a_log, b_d, b_norm_w, b_w_out, loss_target, m_ada_w, m_ada_b, m_ln_g, m_ln_b, m_a_w_in, m_a_w_out, m_b_w_in, m_b_conv_w, m_b_conv_b, m_b_dt_bias, m_b_a_log, m_b_d, m_b_norm_w, m_b_w_out, v_ada_w, v_ada_b, v_ln_g, v_ln_b, v_a_w_in, v_a_w_out, v_b_w_in, v_b_conv_w, v_b_conv_b, v_b_dt_bias, v_b_a_log, v_b_d, v_b_norm_w, v_b_w_out):
    w = dict(ada_w=ada_w, ada_b=ada_b, ln_g=ln_g, ln_b=ln_b, a_w_in=a_w_in, a_w_out=a_w_out, b_w_in=b_w_in, b_conv_w=b_conv_w,
             b_conv_b=b_conv_b, b_dt_bias=b_dt_bias, b_a_log=b_a_log, b_d=b_d, b_norm_w=b_norm_w, b_w_out=b_w_out)
    mom = dict(ada_w=m_ada_w, ada_b=m_ada_b, ln_g=m_ln_g, ln_b=m_ln_b, a_w_in=m_a_w_in, a_w_out=m_a_w_out, b_w_in=m_b_w_in,
               b_conv_w=m_b_conv_w, b_conv_b=m_b_conv_b, b_dt_bias=m_b_dt_bias, b_a_log=m_b_a_log, b_d=m_b_d, b_norm_w=m_b_norm_w,
               b_w_out=m_b_w_out)
    var = dict(ada_w=v_ada_w, ada_b=v_ada_b, ln_g=v_ln_g, ln_b=v_ln_b, a_w_in=v_a_w_in, a_w_out=v_a_w_out, b_w_in=v_b_w_in,
               b_conv_w=v_b_conv_w, b_conv_b=v_b_conv_b, b_dt_bias=v_b_dt_bias, b_a_log=v_b_a_log, b_d=v_b_d, b_norm_w=v_b_norm_w,
               b_w_out=v_b_w_out)
    ax, ay, ac = lax.axis_index("x"), lax.axis_index("y"), lax.axis_index("c")
    me = 4 * ax + 2 * ay + ac
    shard = 2 * ax + ay
    d = x.shape[-1]
    dsh = ada_w.shape[-1]

    small_in = (c, b_conv_w, b_conv_b, b_norm_w)
    g0 = _all_gather8(_pack(small_in).reshape(-1, LANE), "gather_small_in").reshape(N_DEV, -1, PACK_COLS)
    per_dev = [_unpack(g0[k], [a.shape for a in small_in]) for k in range(N_DEV)]
    c_all = jnp.concatenate([p[0] for p in per_dev], axis=0)
    conv_w_full, conv_b_full, norm_w_full = (_unshard_cols([per_dev[2 * k][t] for k in range(N_SHARD)]) for t in (1, 2, 3))

    cond = _silu_rows(jnp.pad(c_all, ((0, 8), (0, 0))), "cond")
    bias = lax.dynamic_slice_in_dim(ada_b, shard * dsh, dsh, axis=1)
    part = jnp.stack([_mm(cond, ada_w[i], add=jnp.broadcast_to(bias[i], (16, dsh)), name=f"mod{i}")[:N_DEV] for i in range(DEPTH)])
    g1 = _all_gather8(part.reshape(-1, LANE), "gather_mod").reshape(N_DEV, DEPTH, N_DEV, dsh)
    mod_all = _unshard_cols([g1[2 * k] for k in range(N_SHARD)])
    mod = lax.dynamic_index_in_dim(mod_all, me, axis=1, keepdims=False)

    gw = _gather_shards(_pack([w[n] for n in BIG]).astype(BF16), "gather_weights")
    big_sh = [_unpack(gw[k], [w[n].shape for n in BIG]) for k in range(N_SHARD)]
    full = dict(
        ln_g=ln_g, ln_b=ln_b, b_dt_bias=b_dt_bias, b_a_log=b_a_log, b_d=b_d,
        b_conv_w=conv_w_full, b_conv_b=conv_b_full, b_norm_w=norm_w_full,
        a_w_in=_unshard_cols([s[0] for s in big_sh]), a_w_out=_unshard_rows([s[1] for s in big_sh]),
        b_w_in=_unshard_cols([s[2] for s in big_sh]), b_w_out=_unshard_rows([s[3] for s in big_sh]),
    )

    loss, grad_x, dmod, g, big, (arrived_b0, arrived1) = _local_step(x[0], loss_target[0], mod, full, early=True)

    arrived0 = list(_transpose_shards(_shards_of(big, 0)[:2], "scatter_grads")) + arrived_b0
    mine = [jnp.concatenate([_sum_slots(arrived0[a], f"sum0_{n}"), _sum_slots(arrived1[a], f"sum1_{n}")], axis=0)
            for a, n in enumerate(BIG)]
    theirs = _swap_sibling(mine, "swap_grads")

    small_g = (dmod, g["ln_g"], g["ln_b"], g["b_dt_bias"], g["b_a_log"], g["b_d"], g["b_conv_w"], g["b_conv_b"], g["b_norm_w"],
               loss.reshape(1))
    g2 = _all_gather8(_pack(small_g).reshape(-1, LANE), "gather_small_grads")
    tot = _unpack(_sum_slots(g2, "sum_small").reshape(-1, PACK_COLS), [a.shape for a in small_g])
    g_ada_b, g_ln_g, g_ln_b, g_dt_bias, g_a_log, g_d, g_conv_w, g_conv_b, g_norm_w, loss_sum = tot
    dmod_all = g2.reshape(N_DEV, -1)[:, :dmod.size].reshape(N_DEV, DEPTH, 3 * d)
    dmod_mine = lax.dynamic_slice_in_dim(dmod_all, shard * dsh, dsh, axis=2)
    g_ada_w = jnp.stack([_mm(cond, jnp.pad(dmod_mine[:, i], ((0, 8), (0, 0))), ta=True, name=f"dada{i}") for i in range(DEPTH)])
    csh = g_conv_w.shape[-1] // N_SHARD
    nsh = g_norm_w.shape[-1] // N_SHARD
    small_grads = dict(
        ada_w=g_ada_w, ada_b=g_ada_b, ln_g=g_ln_g, ln_b=g_ln_b, b_dt_bias=g_dt_bias, b_a_log=g_a_log, b_d=g_d,
        b_conv_w=lax.dynamic_slice_in_dim(g_conv_w, shard * csh, csh, axis=2),
        b_conv_b=lax.dynamic_slice_in_dim(g_conv_b, shard * csh, csh, axis=1),
        b_norm_w=lax.dynamic_slice_in_dim(g_norm_w, shard * nsh, nsh, axis=1),
    )

    by_name = [{}, {}, {}, {}]

    def update(n, gs):
        two_d = lambda t: t.reshape(-1, t.shape[-1])
        outs = _adamw(two_d(w[n]), [two_d(t) for t in gs], two_d(mom[n]), two_d(var[n]), f"adamw_{n}")
        for t, o in zip(by_name, outs):
            t[n] = o.reshape(w[n].shape)

    for i, n in enumerate(BIG):
        update(n, [mine[i], theirs[i]])
    update("ada_w", [small_grads["ada_w"]])
    rest = SMALL
    rows = -(-sum(_nrows(w[n].shape) for n in rest) // 8) * 8
    packed = _adamw(_pack([w[n] for n in rest], rows), [_pack([small_grads[n] for n in rest], rows)],
                    _pack([mom[n] for n in rest], rows), _pack([var[n] for n in rest], rows), "adamw_small")
    for t, p in zip(by_name, packed):
        t.update(zip(rest, _unpack(p, [w[n].shape for n in rest])))
    return (loss_sum.reshape(()), grad_x[None], *[t[n] for t in by_name for n in W_NAMES])
```

```python
import jax
import jax.numpy as jnp
import numpy as np
from jax import lax
from jax.experimental import pallas as pl
from jax.experimental.pallas import tpu as pltpu

F32 = jnp.float32
BF16 = jnp.bfloat16

DEPTH = 4
A_HEADS = 16
A_HEAD_DIM = 64
A_WIDTH = A_HEADS * A_HEAD_DIM
DILATIONS = (1, 4, 16)
A_RADIUS = 64
A_QBLOCK = 128
SSM_HEADS = 32
SSM_HEAD_DIM = 64
SSM_STATE = 128
SSM_GROUPS = 4
SSM_REP = SSM_HEADS // SSM_GROUPS
SSM_CONV = 5
SSM_CHUNK = 128
DEEPNORM_ALPHA = (2 * DEPTH) ** 0.25
LN_EPS = 1e-5
RMS_EPS = 1e-5
ADAM_LR, ADAM_B1, ADAM_B2, ADAM_EPS, ADAM_WD, ADAM_STEP = 0.001, 0.9, 0.999, 1e-08, 0.01, 10
VMEM_LIMIT = 56 * 1024 * 1024
LANE = 128


def _cp(*sem):
    return pltpu.CompilerParams(dimension_semantics=sem, vmem_limit_bytes=VMEM_LIMIT)


def _tile(dim, target):
    if dim <= target:
        return dim
    t = (target // LANE) * LANE
    while dim % t:
        t -= LANE
    return t


def _sigmoid(x):
    return 1.0 / (1.0 + jnp.exp(-x))


def _silu(x):
    return x * _sigmoid(x)


def _dsilu(x):
    s = _sigmoid(x)
    return s * (1.0 + x * (1.0 - s))


def _split3(x):
    a = x.astype(BF16)
    r = x - a.astype(F32)
    b = r.astype(BF16)
    c = (r - b.astype(F32)).astype(BF16)
    return a, b, c


def _dot(a, b, ca=1, cb=0):
    return lax.dot_general(a, b, (((ca,), (cb,)), ((), ())), preferred_element_type=F32)


def _dot_exact(m01, x):
    a, b, c = _split3(x)
    return _dot(m01, a) + _dot(m01, b) + _dot(m01, c)


def _mm(a, b, *, ta=False, tb=False, add=None, out_dtype=F32, name, tm=1024, tn=1024, tk=1024):
    m, k = (a.shape[1], a.shape[0]) if ta else a.shape
    n = b.shape[0] if tb else b.shape[1]
    assert (b.shape[1] if tb else b.shape[0]) == k
    tm, tn, tk = _tile(m, tm), _tile(n, tn), _tile(k, tk)
    nk = k // tk
    has_add = add is not None

    def body(*refs):
        if has_add:
            a_ref, b_ref, c_ref, o_ref, acc = refs
        else:
            a_ref, b_ref, o_ref, acc = refs
        kk = pl.program_id(2)
        part = _dot(a_ref[...].astype(BF16), b_ref[...].astype(BF16), 0 if ta else 1, 1 if tb else 0)

        def finish(r):
            if has_add:
                r = r + c_ref[...]
            o_ref[...] = r.astype(o_ref.dtype)

        if nk == 1:
            finish(part)
            return

        @pl.when(kk == 0)
        def _():
            acc[...] = part

        @pl.when((kk > 0) & (kk < nk - 1))
        def _():
            acc[...] += part

        @pl.when(kk == nk - 1)
        def _():
            finish(acc[...] + part)

    a_spec = pl.BlockSpec((tk, tm), lambda i, j, kk: (kk, i)) if ta else pl.BlockSpec((tm, tk), lambda i, j, kk: (i, kk))
    b_spec = pl.BlockSpec((tn, tk), lambda i, j, kk: (j, kk)) if tb else pl.BlockSpec((tk, tn), lambda i, j, kk: (kk, j))
    in_specs = [a_spec, b_spec]
    args = [a, b]
    if has_add:
        in_specs.append(pl.BlockSpec((tm, tn), lambda i, j, kk: (i, j)))
        args.append(add)
    return pl.pallas_call(
        body, name=name, grid=(m // tm, n // tn, nk), in_specs=in_specs,
        out_specs=pl.BlockSpec((tm, tn), lambda i, j, kk: (i, j)),
        out_shape=jax.ShapeDtypeStruct((m, n), out_dtype),
        scratch_shapes=[pltpu.VMEM((tm, tn) if nk > 1 else (8, LANE), F32)],
        compiler_params=_cp("parallel", "parallel", "arbitrary"),
    )(*args)


ROWS = 512


def _row_spec(tm, d):
    return pl.BlockSpec((tm, d), lambda i: (i, 0))


def _vec_spec(d, rows=1):
    return pl.BlockSpec((rows, d), lambda i: (0, 0))


def _modulate(x, scale, shift, name):
    s, d = x.shape
    tm = min(ROWS, s)

    def body(x_ref, sc_ref, sh_ref, o_ref):
        o_ref[...] = (x_ref[...] * (1.0 + sc_ref[...]) + sh_ref[...]).astype(BF16)

    return pl.pallas_call(
        body, name=name, grid=(s // tm,), in_specs=[_row_spec(tm, d), _vec_spec(d), _vec_spec(d)],
        out_specs=_row_spec(tm, d), out_shape=jax.ShapeDtypeStruct((s, d), BF16), compiler_params=_cp("parallel"),
    )(x, scale, shift)


def _resid_ln(x, y, gate, g, b, name):
    s, d = x.shape
    tm = min(ROWS, s)

    def body(x_ref, y_ref, gt_ref, g_ref, b_ref, o_ref):
        u = DEEPNORM_ALPHA * x_ref[...] + gt_ref[...] * y_ref[...]
        mu = jnp.mean(u, axis=1, keepdims=True)
        uc = u - mu
        var = jnp.mean(uc * uc, axis=1, keepdims=True)
        o_ref[...] = uc * lax.rsqrt(var + LN_EPS) * g_ref[...] + b_ref[...]

    return pl.pallas_call(
        body, name=name, grid=(s // tm,),
        in_specs=[_row_spec(tm, d), _row_spec(tm, d), _vec_spec(d), _vec_spec(d), _vec_spec(d)],
        out_specs=_row_spec(tm, d), out_shape=jax.ShapeDtypeStruct((s, d), F32), compiler_params=_cp("parallel"),
    )(x, y, gate, g, b)


def _resid_ln_bwd(x, y, dxn, gate, g, name):
    s, d = x.shape
    tm = min(ROWS, s)

    def body(x_ref, y_ref, dxn_ref, gt_ref, g_ref, du_ref, dy_ref, red_ref):
        @pl.when(pl.program_id(0) == 0)
        def _():
            red_ref[...] = jnp.zeros_like(red_ref)

        yv = y_ref[...]
        u = DEEPNORM_ALPHA * x_ref[...] + gt_ref[...] * yv
        mu = jnp.mean(u, axis=1, keepdims=True)
        uc = u - mu
        var = jnp.mean(uc * uc, axis=1, keepdims=True)
        rstd = lax.rsqrt(var + LN_EPS)
        xhat = uc * rstd
        dxnv = dxn_ref[...]
        dxh = dxnv * g_ref[...]
        du = rstd * (dxh - jnp.mean(dxh, axis=1, keepdims=True) - xhat * jnp.mean(dxh * xhat, axis=1, keepdims=True))
        du_ref[...] = du
        dy_ref[...] = (du * gt_ref[...]).astype(BF16)
        red_ref[0:1, :] += jnp.sum(du * yv, axis=0, keepdims=True)
        red_ref[1:2, :] += jnp.sum(dxnv * xhat, axis=0, keepdims=True)
        red_ref[2:3, :] += jnp.sum(dxnv, axis=0, keepdims=True)

    return pl.pallas_call(
        body, name=name, grid=(s // tm,),
        in_specs=[_row_spec(tm, d), _row_spec(tm, d), _row_spec(tm, d), _vec_spec(d), _vec_spec(d)],
        out_specs=[_row_spec(tm, d), _row_spec(tm, d), _vec_spec(d, 8)],
        out_shape=[jax.ShapeDtypeStruct((s, d), F32), jax.ShapeDtypeStruct((s, d), BF16), jax.ShapeDtypeStruct((8, d), F32)],
        compiler_params=_cp("arbitrary"),
    )(x, y, dxn, gate, g)


def _modulate_bwd(du, dhs, x, scale, name):
    s, d = x.shape
    tm = min(ROWS, s)
    n = len(dhs)

    def body(*refs):
        du_ref, dh_refs, (x_ref, sc_ref, dx_ref, red_ref) = refs[0], refs[1:1 + n], refs[1 + n:]

        @pl.when(pl.program_id(0) == 0)
        def _():
            red_ref[...] = jnp.zeros_like(red_ref)

        dhv = dh_refs[0][...]
        for t in dh_refs[1:]:
            dhv = dhv + t[...]
        dx_ref[...] = DEEPNORM_ALPHA * du_ref[...] + dhv * (1.0 + sc_ref[...])
        red_ref[0:1, :] += jnp.sum(dhv * x_ref[...], axis=0, keepdims=True)
        red_ref[1:2, :] += jnp.sum(dhv, axis=0, keepdims=True)

    return pl.pallas_call(
        body, name=name, grid=(s // tm,),
        in_specs=[_row_spec(tm, d)] * (n + 2) + [_vec_spec(d)],
        out_specs=[_row_spec(tm, d), _vec_spec(d, 8)],
        out_shape=[jax.ShapeDtypeStruct((s, d), F32), jax.ShapeDtypeStruct((8, d), F32)],
        compiler_params=_cp("arbitrary"),
    )(du, *dhs, x, scale)


def _loss_grad(xf, tgt, name):
    s, d = xf.shape
    tm = min(ROWS, s)

    def body(x_ref, t_ref, dx_ref, red_ref):
        @pl.when(pl.program_id(0) == 0)
        def _():
            red_ref[...] = jnp.zeros_like(red_ref)

        e = x_ref[...] - t_ref[...]
        dx_ref[...] = e * (1.0 / d)
        red_ref[0:1, :] += jnp.sum(e * e, axis=0, keepdims=True)

    return pl.pallas_call(
        body, name=name, grid=(s // tm,), in_specs=[_row_spec(tm, d), _row_spec(tm, d)],
        out_specs=[_row_spec(tm, d), _vec_spec(d, 8)],
        out_shape=[jax.ShapeDtypeStruct((s, d), F32), jax.ShapeDtypeStruct((8, d), F32)],
        compiler_params=_cp("arbitrary"),
    )(xf, tgt)


QKV_COLS = 3 * 3 * A_WIDTH


SLOPES = tuple(float(2.0 ** (-8.0 * (h + 1.0) / A_HEADS)) for h in range(A_HEADS))
FAR = 1e30
HEAD_COLS = tuple(slice(h * A_HEAD_DIM, (h + 1) * A_HEAD_DIM) for h in range(A_HEADS))


def _band_dist(n, length, dil, span_rows):
    shape = (2 * A_QBLOCK, A_QBLOCK) if span_rows else (A_QBLOCK, 2 * A_QBLOCK)
    r = lax.broadcasted_iota(jnp.int32, shape, 0)
    c = lax.broadcasted_iota(jnp.int32, shape, 1)
    sp, ce = (r, c) if span_rows else (c, r)
    delta = sp - A_RADIUS - ce
    pos = n * A_QBLOCK - A_RADIUS + sp
    valid = (jnp.abs(delta) <= A_RADIUS) & (pos >= 0) & (pos < length)
    return jnp.where(valid, jnp.abs(delta).astype(F32) * float(dil), FAR)


def _span_specs(col, nb64):
    def mk(i):
        return pl.BlockSpec((64, A_WIDTH), lambda r, n: (r * nb64 + jnp.clip(2 * n - 1 + i, 0, nb64 - 1), col))
    return [mk(i) for i in range(4)]


def _to_residue(t, dil):
    if dil == 1:
        return t
    s, c = t.shape
    return t.reshape(s // dil, dil, c).transpose(1, 0, 2).reshape(s, c)


def _from_residue(t, dil):
    if dil == 1:
        return t
    s, c = t.shape
    return t.reshape(dil, s // dil, c).transpose(1, 0, 2).reshape(s, c)


def _cat(refs):
    return jnp.concatenate([t[...] for t in refs], axis=0)


def _head_expander():
    r = lax.broadcasted_iota(jnp.int32, (A_HEADS, A_WIDTH), 0)
    c = lax.broadcasted_iota(jnp.int32, (A_HEADS, A_WIDTH), 1)
    return ((c >= r * A_HEAD_DIM) & (c < (r + 1) * A_HEAD_DIM)).astype(BF16)


def _to_lanes(x16, e):
    a, b, c = _split3(x16)
    return _dot(a, e) + _dot(b, e) + _dot(c, e)


def _per_head_sum(x, e):
    a, b, c = _split3(x)
    return _dot(a, e, 1, 1) + _dot(b, e, 1, 1) + _dot(c, e, 1, 1)


def _pair_low_lanes():
    return lax.broadcasted_iota(jnp.int32, (A_QBLOCK, LANE), 1) < A_HEAD_DIM


def _top_rows():
    return lax.broadcasted_iota(jnp.int32, (2 * A_QBLOCK, 1), 0) < A_QBLOCK


def _block_diag(v, low):
    zero = jnp.zeros_like(v)
    return jnp.concatenate([jnp.where(low, v, zero), jnp.where(low, zero, v)], axis=0)


def _attn_fwd(qkv, g, name):
    s = qkv.shape[0]
    dil = DILATIONS[g]
    length = s // dil
    nblk = length // A_QBLOCK

    def body(q_ref, k0, k1, k2, k3, v0, v1, v2, v3, o_ref, l_ref):
        dist = _band_dist(pl.program_id(1), length, dil, False)
        kk = _cat((k0, k1, k2, k3))
        vv = _cat((v0, v1, v2, v3))
        low = _pair_low_lanes()
        top = _top_rows()
        dist2 = jnp.concatenate([dist, dist], axis=0)
        for hp in range(A_HEADS // 2):
            ls = slice(hp * LANE, (hp + 1) * LANE)
            qp, kp, vp = q_ref[:, ls], kk[:, ls], vv[:, ls]
            sc = _dot(_block_diag(qp, low), kp, 1, 1) * 0.125 - jnp.where(top, SLOPES[2 * hp], SLOPES[2 * hp + 1]) * dist2
            m = jnp.max(sc, axis=1, keepdims=True)
            p = jnp.exp(sc - m)
            z = jnp.sum(p, axis=1, keepdims=True)
            o2 = _dot(p.astype(BF16), vp) / z
            lse2 = m + jnp.log(z)
            l_ref[:, 2 * hp:2 * hp + 1] = lse2[:A_QBLOCK]
            l_ref[:, 2 * hp + 1:2 * hp + 2] = lse2[A_QBLOCK:]
            o_ref[:, ls] = jnp.where(low, o2[:A_QBLOCK], o2[A_QBLOCK:])

    qspec = pl.BlockSpec((A_QBLOCK, A_WIDTH), lambda r, n: (r * nblk + n, 0))
    lspec = pl.BlockSpec((A_QBLOCK, A_HEADS), lambda r, n: (r * nblk + n, 0))
    return pl.pallas_call(
        body, name=name, grid=(dil, nblk), in_specs=[qspec] + _span_specs(1, 2 * nblk) + _span_specs(2, 2 * nblk),
        out_specs=[qspec, lspec],
        out_shape=[jax.ShapeDtypeStruct((s, A_WIDTH), F32), jax.ShapeDtypeStruct((s, A_HEADS), F32)],
        compiler_params=_cp("parallel", "parallel"),
    )(*([qkv] * 9))


def _attn_merge(os_, ls_, gate, name):
    s, w = gate.shape
    tm = min(ROWS, s)

    def body(o0, o1, o2, l0, l1, l2, g_ref, y_ref, o_ref, l_ref):
        a, b, c = l0[...], l1[...], l2[...]
        m = jnp.maximum(jnp.maximum(a, b), c)
        ea, eb, ec = jnp.exp(a - m), jnp.exp(b - m), jnp.exp(c - m)
        z = ea + eb + ec
        l_ref[...] = m + jnp.log(z)
        e = _head_expander()
        o = _to_lanes(ea / z, e) * o0[...] + _to_lanes(eb / z, e) * o1[...] + _to_lanes(ec / z, e) * o2[...]
        o_ref[...] = o
        y_ref[...] = (o * _silu(g_ref[...])).astype(BF16)

    rs = _row_spec(tm, w)
    ls = _row_spec(tm, A_HEADS)
    return pl.pallas_call(
        body, name=name, grid=(s // tm,), in_specs=[rs] * 3 + [ls] * 3 + [rs], out_specs=[rs, rs, ls],
        out_shape=[jax.ShapeDtypeStruct((s, w), BF16), jax.ShapeDtypeStruct((s, w), F32), jax.ShapeDtypeStruct((s, A_HEADS), F32)],
        compiler_params=_cp("parallel"),
    )(*os_, *ls_, gate)


def _attn_gate_bwd(dyy, o, gate, name):
    s, w = gate.shape
    tm = min(ROWS, s)

    def body(dy_ref, o_ref, g_ref, do_ref, dg_ref, dl_ref):
        dyv, ov, gv = dy_ref[...], o_ref[...], g_ref[...]
        do = dyv * _silu(gv)
        do_ref[...] = do.astype(BF16)
        dg_ref[...] = (dyv * ov * _dsilu(gv)).astype(BF16)
        dl_ref[...] = _per_head_sum(do * ov, _head_expander())

    rs = _row_spec(tm, w)
    return pl.pallas_call(
        body, name=name, grid=(s // tm,), in_specs=[rs] * 3, out_specs=[rs, rs, _row_spec(tm, A_HEADS)],
        out_shape=[jax.ShapeDtypeStruct((s, w), BF16), jax.ShapeDtypeStruct((s, w), BF16), jax.ShapeDtypeStruct((s, A_HEADS), F32)],
        compiler_params=_cp("parallel"),
    )(dyy, o, gate)


def _ride_along(ex_in, ex_out, sems, first, last):
    @pl.when(first)
    def _():
        for cp in _shard_exchange(ex_in, ex_out, *sems):
            cp.start()

    @pl.when(last)
    def _():
        for cp in _shard_exchange(ex_in, ex_out, *sems):
            cp.wait()


def _ride_along_specs(exch):
    n = len(exch)
    n_rem = (N_SHARD - 1) * n
    sems = [pltpu.SemaphoreType.DMA((n_rem,)), pltpu.SemaphoreType.DMA((n_rem,)), pltpu.SemaphoreType.DMA((n,))] if n else []
    return [ANY] * n, [jax.ShapeDtypeStruct(t.shape, t.dtype) for t in exch], sems


def _attn_bwd(qkv, do, lse, delta, g, name, exch=()):
    s = qkv.shape[0]
    dil = DILATIONS[g]
    length = s // dil
    nblk = length // A_QBLOCK
    n_ex = len(exch)

    def rows(t16):
        return jnp.pad(t16.reshape(dil, length, A_HEADS).transpose(0, 2, 1), ((0, 0), (0, 0), (A_RADIUS, A_RADIUS)))

    def body(*refs):
        (q0, q1, q2, q3, k0, k1, k2, k3, v0, v1, v2, v3, d0, d1, d2, d3, lc_ref, ec_ref, la, lb, ea, eb), refs = refs[:22], refs[22:]
        o_ref = refs[n_ex]
        if n_ex:
            r, n = pl.program_id(0), pl.program_id(1)
            _ride_along(refs[:n_ex], refs[n_ex + 1:2 * n_ex + 1], refs[2 * n_ex + 1:], (r == 0) & (n == 0), (r == dil - 1) & (n == nblk - 1))
        dist = _band_dist(pl.program_id(1), length, dil, False)
        qq, kk, vv, dd = _cat((q0, q1, q2, q3)), _cat((k0, k1, k2, k3)), _cat((v0, v1, v2, v3)), _cat((d0, d1, d2, d3))
        lse_r = jnp.concatenate([la[...], lb[...]], axis=1)
        dlt_r = jnp.concatenate([ea[...], eb[...]], axis=1)
        low = _pair_low_lanes()
        top = _top_rows()
        dist2 = jnp.concatenate([dist, dist], axis=0)
        centre = slice(A_RADIUS, A_RADIUS + A_QBLOCK)
        for hp in range(A_HEADS // 2):
            ls = slice(hp * LANE, (hp + 1) * LANE)
            qs, ks, vs, ds_ = qq[:, ls], kk[:, ls], vv[:, ls], dd[:, ls]
            qn, kn, vn, dn = qs[centre], ks[centre], vs[centre], ds_[centre]
            h0, h1 = 2 * hp, 2 * hp + 1
            bias = jnp.where(top, SLOPES[h0], SLOPES[h1]) * dist2
            lc = jnp.concatenate([lc_ref[:, h0:h0 + 1], lc_ref[:, h1:h1 + 1]], axis=0)
            ec = jnp.concatenate([ec_ref[:, h0:h0 + 1], ec_ref[:, h1:h1 + 1]], axis=0)
            lr = jnp.where(top, lse_r[h0:h0 + 1, :], lse_r[h1:h1 + 1, :])
            er = jnp.where(top, dlt_r[h0:h0 + 1, :], dlt_r[h1:h1 + 1, :])
            p = jnp.exp(_dot(_block_diag(qn, low), ks, 1, 1) * 0.125 - bias - lc)
            dsc = p * (_dot(_block_diag(dn, low), vs, 1, 1) - ec)
            dq = _dot(dsc.astype(BF16), ks)
            pt = jnp.exp(_dot(_block_diag(kn, low), qs, 1, 1) * 0.125 - bias - lr)
            dst = pt * (_dot(_block_diag(vn, low), ds_, 1, 1) - er)
            dk = _dot(dst.astype(BF16), qs)
            dv = _dot(pt.astype(BF16), ds_)
            merge = lambda t: jnp.where(low, t[:A_QBLOCK], t[A_QBLOCK:])
            o_ref[:, ls] = (merge(dq) * 0.125).astype(BF16)
            o_ref[:, A_WIDTH + hp * LANE:A_WIDTH + (hp + 1) * LANE] = (merge(dk) * 0.125).astype(BF16)
            o_ref[:, 2 * A_WIDTH + hp * LANE:2 * A_WIDTH + (hp + 1) * LANE] = merge(dv).astype(BF16)

    nb64 = 2 * nblk
    dspecs = _span_specs(0, nb64)
    cspec = pl.BlockSpec((A_QBLOCK, A_HEADS), lambda r, n: (r * nblk + n, 0))
    rspecs = [pl.BlockSpec((None, A_HEADS, A_QBLOCK), lambda r, n: (r, 0, n)), pl.BlockSpec((None, A_HEADS, A_QBLOCK), lambda r, n: (r, 0, n + 1))]
    lse_r, dlt_r = rows(lse), rows(delta)
    ex_specs, ex_shapes, sems = _ride_along_specs(exch)
    outs = pl.pallas_call(
        body, name=name, grid=(dil, nblk),
        in_specs=_span_specs(0, nb64) + _span_specs(1, nb64) + _span_specs(2, nb64) + dspecs + [cspec, cspec] + rspecs * 2 + ex_specs,
        out_specs=[pl.BlockSpec((A_QBLOCK, 3 * A_WIDTH), lambda r, n: (r * nblk + n, 0))] + ex_specs,
        out_shape=[jax.ShapeDtypeStruct((s, 3 * A_WIDTH), BF16)] + ex_shapes,
        scratch_shapes=sems,
        compiler_params=_cp(*(("arbitrary", "arbitrary") if n_ex else ("parallel", "parallel"))),
    )(*([qkv] * 12), *([do] * 4), lse, delta, lse_r, lse_r, dlt_r, dlt_r, *exch)
    return (outs[0], outs[1:]) if n_ex else outs[0]


def _attn_layer_fwd(h, w_qkv, w_gate, w_out, li):
    nm = lambda t: f"a{li}_{t}"
    gate = _mm(h, w_gate, name=nm("gate"))
    hs, qkvs, os_, ls_ = [], [], [], []
    for g, dil in enumerate(DILATIONS):
        hg = _to_residue(h, dil)
        qkv = _mm(hg, w_qkv[:, g * 3 * A_WIDTH:(g + 1) * 3 * A_WIDTH], out_dtype=BF16, name=nm(f"qkv{g}"))
        o, l = _attn_fwd(qkv, g, nm(f"attn{g}"))
        hs.append(hg)
        qkvs.append(qkv)
        os_.append(_from_residue(o, dil))
        ls_.append(_from_residue(l, dil))
    y, o, lse = _attn_merge(os_, ls_, gate, nm("merge"))
    out = _mm(y, w_out, name=nm("out"))
    return out, (hs, qkvs, gate, y, o, lse)


def _attn_layer_bwd(dy, h, saved, w_qkv, w_gate, w_out, li, exch=()):
    nm = lambda t: f"a{li}_{t}"
    hs, qkvs, gate, y, o, lse = saved
    g_w_out = _mm(y, dy, ta=True, out_dtype=BF16, name=nm("dwout"))
    dyy = _mm(dy, w_out, tb=True, name=nm("dyy"))
    do, dgate, delta = _attn_gate_bwd(dyy, o, gate, nm("gatebwd"))
    dhs, dws, arrived = [], [], ()
    for g, dil in enumerate(DILATIONS):
        dqkv = _attn_bwd(qkvs[g], _to_residue(do, dil), _to_residue(lse, dil), _to_residue(delta, dil), g, nm(f"attnbwd{g}"),
                         exch if g == 0 else ())
        if g == 0 and exch:
            dqkv, arrived = dqkv
        wg = w_qkv[:, g * 3 * A_WIDTH:(g + 1) * 3 * A_WIDTH]
        dws.append(_mm(hs[g], dqkv, ta=True, out_dtype=BF16, name=nm(f"dwqkv{g}")))
        add = _mm(dgate, w_gate, tb=True, name=nm("dh_gate")) if g == 0 else None
        dhs.append(_from_residue(_mm(dqkv, wg, tb=True, add=add, name=nm(f"dh_qkv{g}")), dil))
    g_w_in = jnp.concatenate(dws + [_mm(h, dgate, ta=True, out_dtype=BF16, name=nm("dwgate"))], axis=1)
    return dhs, g_w_in, g_w_out, arrived


SSM_INNER = SSM_HEADS * SSM_HEAD_DIM
SSM_BC = SSM_GROUPS * SSM_STATE
SSM_CONV_DIM = SSM_INNER + 2 * SSM_BC
GW = SSM_REP * SSM_HEAD_DIM
T = SSM_CHUNK
HALO = 8


def _conv_specs(tm, tn, s, col=lambda j: j):
    nb8 = s // HALO
    cur = pl.BlockSpec((tm, tn), lambda j, i: (i, col(j)))
    prev = pl.BlockSpec((HALO, tn), lambda j, i: (jnp.maximum(i * (tm // HALO) - 1, 0), col(j)))
    nxt = pl.BlockSpec((HALO, tn), lambda j, i: (jnp.minimum((i + 1) * (tm // HALO), nb8 - 1), col(j)))
    return [prev, cur, nxt]


def _extend(prev_ref, cur_ref, nxt_ref, i, nrow):
    p = jnp.where(i == 0, 0.0, prev_ref[...])
    n = jnp.where(i == nrow - 1, 0.0, nxt_ref[...])
    return jnp.concatenate([p, cur_ref[...], n], axis=0)


def _shift_rows(ext, off, tm):
    rows = ext.shape[0]
    return pltpu.roll(ext, (-off) % rows, 0)[HALO:HALO + tm]


def _conv_fwd(xraw, w, b, name):
    s, cdim = xraw.shape
    tm, tn = min(256, s), 1024
    nrow = s // tm

    def body(p_ref, c_ref, n_ref, w_ref, b_ref, pre_ref, act_ref):
        ext = _extend(p_ref, c_ref, n_ref, pl.program_id(1), nrow)
        acc = jnp.broadcast_to(b_ref[...], (tm, tn))
        for k in range(SSM_CONV):
            acc = acc + w_ref[k:k + 1, :] * _shift_rows(ext, k - SSM_CONV // 2, tm)
        pre_ref[...] = acc
        act_ref[...] = _silu(acc)

    prev, cur, nxt = _conv_specs(tm, tn, s)
    return pl.pallas_call(
        body, name=name, grid=(cdim // tn, nrow),
        in_specs=[prev, cur, nxt, pl.BlockSpec((SSM_CONV, tn), lambda j, i: (0, j)), pl.BlockSpec((1, tn), lambda j, i: (0, j))],
        out_specs=[cur, cur], out_shape=[jax.ShapeDtypeStruct((s, cdim), F32)] * 2,
        compiler_params=_cp("parallel", "parallel"),
    )(xraw, xraw, xraw, w, b)


def _conv_bwd(dx, db, dc, pre, xraw, w, name):
    s, cdim = xraw.shape
    tm, tn = min(256, s), 1024
    nrow = s // tm
    nx = dx.shape[1] // tn

    def body(xp, xc, xn, bp, bc, bn, cp, cc, cn, pp, pc, pn, x_ref, w_ref, o_ref, red_ref):
        j, i = pl.program_id(0), pl.program_id(1)

        @pl.when(i == 0)
        def _():
            red_ref[...] = jnp.zeros_like(red_ref)

        bcext = jnp.concatenate([_extend(bp, bc, bn, i, nrow), _extend(cp, cc, cn, i, nrow)], axis=1)
        dact = jnp.where(j < nx, _extend(xp, xc, xn, i, nrow), bcext)
        dpre = dact * _dsilu(_extend(pp, pc, pn, i, nrow))
        xv = x_ref[...]
        acc = jnp.zeros((tm, tn), F32)
        for k in range(SSM_CONV):
            sk = _shift_rows(dpre, SSM_CONV // 2 - k, tm)
            acc = acc + w_ref[k:k + 1, :] * sk
            red_ref[k:k + 1, :] += jnp.sum(sk * xv, axis=0, keepdims=True)
        red_ref[SSM_CONV:SSM_CONV + 1, :] += jnp.sum(dpre[HALO:HALO + tm], axis=0, keepdims=True)
        o_ref[...] = acc.astype(BF16)

    half = tn // 2
    cur = pl.BlockSpec((tm, tn), lambda j, i: (i, j))
    return pl.pallas_call(
        body, name=name, grid=(cdim // tn, nrow),
        in_specs=_conv_specs(tm, tn, s, lambda j: jnp.minimum(j, nx - 1)) + _conv_specs(tm, half, s, lambda j: 0) * 2
        + _conv_specs(tm, tn, s) + [cur, pl.BlockSpec((SSM_CONV, tn), lambda j, i: (0, j))],
        out_specs=[cur, pl.BlockSpec((8, tn), lambda j, i: (0, j))],
        out_shape=[jax.ShapeDtypeStruct((s, cdim), BF16), jax.ShapeDtypeStruct((8, cdim), F32)],
        compiler_params=_cp("parallel", "arbitrary"),
    )(dx, dx, dx, db, db, db, dc, dc, dc, pre, pre, pre, xraw, w)


def _tri(lower):
    r = lax.broadcasted_iota(jnp.int32, (T, T), 0)
    c = lax.broadcasted_iota(jnp.int32, (T, T), 1)
    return (r >= c) if lower else (r <= c)


def _softplus(x):
    return jnp.maximum(x, 0.0) + jnp.log(1.0 + jnp.exp(-jnp.abs(x)))


def _dt_prep(dt_raw, bias, a_log, name):
    s = dt_raw.shape[0]
    nc = s // T

    def body(r_ref, b_ref, a_ref, dt_ref, cum_ref, cumt_ref):
        dt = _softplus(r_ref[...] + b_ref[...])
        da = dt * (-jnp.exp(a_ref[...]))
        pre = _dot_exact(_tri(True).astype(BF16), da)
        suf = _dot_exact(_tri(False).astype(BF16), da)
        lane = lax.broadcasted_iota(jnp.int32, (T, LANE), 1)
        cum = jnp.where(lane < SSM_HEADS, pre, suf)
        dt_ref[...] = dt
        cum_ref[...] = cum
        cumt_ref[...] = cum.T

    blk = pl.BlockSpec((T, LANE), lambda c: (c, 0))
    vec = pl.BlockSpec((1, LANE), lambda c: (0, 0))
    return pl.pallas_call(
        body, name=name, grid=(nc,), in_specs=[blk, vec, vec],
        out_specs=[blk, blk, pl.BlockSpec((None, LANE, T), lambda c: (c, 0, 0))],
        out_shape=[jax.ShapeDtypeStruct((s, LANE), F32), jax.ShapeDtypeStruct((s, LANE), F32), jax.ShapeDtypeStruct((nc, LANE, T), F32)],
        compiler_params=_cp("parallel"),
    )(dt_raw, bias, a_log)


def _by_group(t):
    s = t.shape[0]
    return t[:, :2 * SSM_HEADS].reshape(s, 2 * SSM_GROUPS, SSM_REP).transpose(1, 0, 2)


def _from_group(tf, tb):
    s = tf.shape[1]
    t = jnp.concatenate([tf, tb], axis=0).transpose(1, 0, 2).reshape(s, 2 * SSM_HEADS)
    return jnp.pad(t, ((0, 0), (0, LANE - 2 * SSM_HEADS)))


def _decay_mats(acol, arow, rev):
    after = _tri(not rev)
    return jnp.where(after, jnp.exp(jnp.where(after, acol - arow, 0.0)), 0.0)


PAIRS = SSM_REP // 2


def _low_lanes():
    return lax.broadcasted_iota(jnp.int32, (T, LANE), 1) < SSM_HEAD_DIM


CPS = 8
TB = CPS * T


def _scan_specs(rev, ci):
    nxb = SSM_INNER // LANE
    kofs = SSM_GROUPS if rev else 0
    return [
        pl.BlockSpec((TB, GW), lambda g, c: (ci(c), g)),
        pl.BlockSpec((TB, LANE), lambda g, c: (ci(c), nxb + g)),
        pl.BlockSpec((TB, LANE), lambda g, c: (ci(c), nxb + SSM_GROUPS + g)),
        pl.BlockSpec((None, TB, SSM_REP), lambda g, c: (kofs + g, ci(c), 0)),
        pl.BlockSpec((None, TB, SSM_REP), lambda g, c: (kofs + g, ci(c), 0)),
        pl.BlockSpec((None, CPS, SSM_REP, T), lambda g, c: (kofs + g, ci(c), 0, 0)),
    ]


def _chunk_rows(q):
    return pl.ds(q * T, T)


def _pair_lanes(ref, p, low):
    return jnp.where(low, ref[:, 2 * p:2 * p + 1], ref[:, 2 * p + 1:2 * p + 2])


def _ssd_scan(xbc, dtk, cumk, cumtk, rev, name, add=None):
    s = xbc.shape[0]
    nc = s // T
    nb = nc // CPS
    last = 0 if rev else T - 1
    ci = (lambda c: nb - 1 - c) if rev else (lambda c: c)
    has_add = add is not None

    def body(*refs):
        x_ref, b_ref, c_ref, dt_ref, cum_ref, cumt_ref = refs[:6]
        a_ref = refs[6] if has_add else None
        y_ref, st_ref, state = refs[-3:]

        @pl.when(pl.program_id(1) == 0)
        def _():
            state[...] = jnp.zeros_like(state)

        for q in (reversed(range(CPS)) if rev else range(CPS)):
            rows = _chunk_rows(q)
            chunk(x_ref.at[rows], b_ref.at[rows], c_ref.at[rows], dt_ref.at[rows], cum_ref.at[rows], cumt_ref.at[q],
                  a_ref.at[rows] if has_add else None, y_ref.at[rows], st_ref.at[q], state)

    def chunk(x_ref, b_ref, c_ref, dt_ref, cum_ref, cumt_ref, a_ref, y_ref, st_ref, state):
        bm = b_ref[...]
        cm = c_ref[...].astype(BF16)
        cb = _dot(cm, bm.astype(BF16), 1, 1)
        bt = bm.T.astype(BF16)
        low = _low_lanes()
        for p in range(PAIRS):
            ls = slice(p * LANE, (p + 1) * LANE)
            acum = _pair_lanes(cum_ref, p, low)
            u = x_ref[:, ls] * _pair_lanes(dt_ref, p, low)
            tot = acum[last:last + 1, :]
            m = [(cb * _decay_mats(cum_ref[:, r:r + 1], cumt_ref[r:r + 1, :], rev)).astype(BF16) for r in (2 * p, 2 * p + 1)]
            st = state[p]
            st_ref[p] = st
            yd = _dot(jnp.concatenate(m, axis=1), _block_diag(u.astype(BF16), low))
            yo = jnp.exp(acum) * _dot(cm, st.astype(BF16))
            y_ref[:, ls] = yd + yo + a_ref[:, ls] if has_add else yd + yo
            state[p] = jnp.exp(tot) * st + _dot(bt, (jnp.exp(tot - acum) * u).astype(BF16))

    yspec = pl.BlockSpec((TB, GW), lambda g, c: (ci(c), g))
    return pl.pallas_call(
        body, name=name, grid=(SSM_GROUPS, nb), in_specs=_scan_specs(rev, ci) + ([yspec] if has_add else []),
        out_specs=[yspec, pl.BlockSpec((CPS, PAIRS, SSM_STATE, LANE), lambda g, c: (ci(c), g, 0, 0))],
        out_shape=[jax.ShapeDtypeStruct((s, SSM_INNER), F32), jax.ShapeDtypeStruct((nc, SSM_HEADS // 2, SSM_STATE, LANE), F32)],
        scratch_shapes=[pltpu.VMEM((PAIRS, SSM_STATE, LANE), F32)],
        compiler_params=_cp("parallel", "arbitrary"),
    )(xbc, xbc, xbc, dtk, cumk, cumtk, *([add] if has_add else []))


def _ssd_scan_bwd(xbc, dtk, cumk, cumtk, dy, states, dvec, prev, rev, name, exch=()):
    s = xbc.shape[0]
    nc = s // T
    nb = nc // CPS
    last = 0 if rev else T - 1
    ci = (lambda c: c) if rev else (lambda c: nb - 1 - c)
    has_prev = prev is not None
    n_in = 12 if has_prev else 9
    n_ex = len(exch)

    def body(*refs):
        ins, ex_in, refs = refs[:n_in], refs[n_in:n_in + n_ex], refs[n_in + n_ex:]
        outs, ex_out, scratch = refs[:5], refs[5:5 + n_ex], refs[5 + n_ex:]
        if n_ex:
            scratch, sems = scratch[:4], scratch[4:]
            g, c = pl.program_id(0), pl.program_id(1)
            _ride_along(ex_in, ex_out, sems, (g == 0) & (c == 0), (g == SSM_GROUPS - 1) & (c == nb - 1))

        @pl.when(pl.program_id(1) == 0)
        def _():
            scratch[0][...] = jnp.zeros_like(scratch[0])

        for q in (range(CPS) if rev else reversed(range(CPS))):
            rows = _chunk_rows(q)
            cut = lambda t: t.at[rows]
            x_ref, b_ref, c_ref, dt_ref, cum_ref, cumt_ref, dy_ref, st_ref, dv_ref = ins[:9]
            sub = [cut(x_ref), cut(b_ref), cut(c_ref), cut(dt_ref), cut(cum_ref), cumt_ref.at[q], cut(dy_ref), st_ref.at[q], dv_ref]
            chunk(*sub, *[cut(t) for t in ins[9:]], *[cut(t) for t in outs], *scratch)

    def chunk(*refs):
        x_ref, b_ref, c_ref, dt_ref, cum_ref, cumt_ref, dy_ref, st_ref, dv_ref = refs[:9]
        refs = refs[9:]
        if has_prev:
            pdx, pdb, pdc = refs[:3]
            refs = refs[3:]
        dx_ref, db_ref, dc_ref, ddt_ref, dda_ref, dstate, rs_buf, in_buf, k_buf = refs
        rs_buf[...] = jnp.zeros_like(rs_buf)
        in_buf[...] = jnp.zeros_like(in_buf)
        k_buf[...] = jnp.zeros_like(k_buf)
        bm = b_ref[...].astype(BF16)
        cm = c_ref[...].astype(BF16)
        cbt = _dot(bm, cm, 1, 1)
        cb = _dot(cm, bm, 1, 1)
        ct = c_ref[...].T.astype(BF16)
        after = _tri(not rev)
        before = _tri(rev)
        from_k = before.astype(BF16)
        ri = lax.broadcasted_iota(jnp.int32, (T, T), 0)
        cj = lax.broadcasted_iota(jnp.int32, (T, T), 1)
        strictly_before = (cj > ri) if rev else (cj < ri)
        dcb = jnp.zeros((T, T), F32)
        dc_acc = jnp.zeros((T, SSM_STATE), F32)
        db_acc = jnp.zeros((T, SSM_STATE), F32)
        low = _low_lanes()
        ri2 = lax.broadcasted_iota(jnp.int32, (LANE, LANE), 0)
        cj2 = lax.broadcasted_iota(jnp.int32, (LANE, LANE), 1)
        halves = ((ri2 < SSM_HEAD_DIM) == (cj2 == 0)) & (cj2 < 2)
        halves = halves.astype(BF16)

        def head_sums(v):
            hi = v.astype(BF16)
            lo = (v - hi.astype(F32)).astype(BF16)
            return _dot(hi, halves) + _dot(lo, halves)

        for p in range(PAIRS):
            ls = slice(p * LANE, (p + 1) * LANE)
            c2 = slice(2 * p, 2 * p + 2)
            lm, lmt = [], []
            for r in (2 * p, 2 * p + 1):
                acol = cum_ref[:, r:r + 1]
                arow = cumt_ref[r:r + 1, :]
                lm.append(jnp.where(after, jnp.exp(jnp.where(after, acol - arow, 0.0)), 0.0))
                lmt.append(jnp.where(before, jnp.exp(jnp.where(before, arow - acol, 0.0)), 0.0))
            acum = _pair_lanes(cum_ref, p, low)
            tot = acum[last:last + 1, :]
            dtl = _pair_lanes(dt_ref, p, low)
            xl = x_ref[:, ls]
            u = xl * dtl
            ub = u.astype(BF16)
            dyl = dy_ref[:, ls]
            dyb = dyl.astype(BF16)
            st = st_ref[p]
            stb = st.astype(BF16)
            dst = dstate[p]
            dstb = dst.astype(BF16)
            dec = jnp.exp(tot - acum)
            eac = jnp.exp(acum)
            etot = jnp.exp(tot)
            du_off = dec * _dot(bm, dstb)
            mt = jnp.concatenate([(cbt * lmt[0]).astype(BF16), (cbt * lmt[1]).astype(BF16)], axis=1)
            du = _dot(mt, _block_diag(dyb, low)) + du_off
            zero = jnp.zeros_like(dyb)
            gl = [_dot(jnp.where(low, dyb, zero), ub, 1, 1) * lm[0], _dot(jnp.where(low, zero, dyb), ub, 1, 1) * lm[1]]
            dcb = dcb + gl[0] + gl[1]
            dc_acc = dc_acc + _dot((eac * dyl).astype(BF16), stb, 1, 1)
            db_acc = db_acc + _dot((dec * u).astype(BF16), dstb, 1, 1)
            w = jnp.concatenate([(gl[0] * cb).astype(BF16), (gl[1] * cb).astype(BF16)], axis=1)
            crossing = _dot(from_k, w)
            for j in range(2):
                cr = jnp.where(strictly_before, crossing[:, j * T:(j + 1) * T], 0.0)
                in_buf[:, 2 * p + j:2 * p + j + 1] = jnp.sum(cr, axis=1, keepdims=True)
            y_off = eac * _dot(cm, stb)
            udu = u * du_off
            rs_buf[:, c2] = head_sums(dyl * y_off - udu)[:, 0:2]
            col = jnp.sum(dst * (etot * st) + udu, axis=0, keepdims=True)
            k_buf[0:1, c2] = head_sums(jnp.broadcast_to(col, (8, LANE)))[0:1, 0:2]
            ddt_ref[:, c2] = head_sums(du * xl)[:, 0:2]
            dx = du * dtl
            if has_prev:
                dx = dx + pdx[:, ls]
            else:
                dx = dx + dyl * dv_ref[:, ls]
            dx_ref[:, ls] = dx
            dstate[p] = etot * dst + _dot(ct, (eac * dyl).astype(BF16))
        dda = in_buf[...] + _dot_exact(from_k, rs_buf[...]) + k_buf[0:1, :]
        dda_ref[...] = dda[:, :SSM_REP]
        dcbb = dcb.astype(BF16)
        dc = dc_acc + _dot(dcbb, bm)
        db = db_acc + _dot(dcbb, cm, 0, 0)
        if has_prev:
            dc = dc + pdc[...]
            db = db + pdb[...]
        dc_ref[...] = dc
        db_ref[...] = db

    xspec = pl.BlockSpec((TB, GW), lambda g, c: (ci(c), g))
    gspec = pl.BlockSpec((TB, LANE), lambda g, c: (ci(c), g))
    in_specs = _scan_specs(rev, ci) + [
        xspec,
        pl.BlockSpec((CPS, PAIRS, SSM_STATE, LANE), lambda g, c: (ci(c), g, 0, 0)),
        pl.BlockSpec((1, GW), lambda g, c: (0, g)),
    ]
    args = [xbc, xbc, xbc, dtk, cumk, cumtk, dy, states, dvec]
    if has_prev:
        in_specs += [xspec, gspec, gspec]
        args += list(prev)
    ospec8 = pl.BlockSpec((None, TB, SSM_REP), lambda g, c: (g, ci(c), 0))
    ex_specs, ex_shapes, sems = _ride_along_specs(exch)
    outs = pl.pallas_call(
        body, name=name, grid=(SSM_GROUPS, nb), in_specs=in_specs + ex_specs,
        out_specs=[xspec, gspec, gspec, ospec8, ospec8] + ex_specs,
        out_shape=[jax.ShapeDtypeStruct((s, SSM_INNER), F32), jax.ShapeDtypeStruct((s, SSM_BC), F32), jax.ShapeDtypeStruct((s, SSM_BC), F32),
                   jax.ShapeDtypeStruct((SSM_GROUPS, s, SSM_REP), F32), jax.ShapeDtypeStruct((SSM_GROUPS, s, SSM_REP), F32)]
        + ex_shapes,
        scratch_shapes=[pltpu.VMEM((PAIRS, SSM_STATE, LANE), F32), pltpu.VMEM((T, LANE), F32), pltpu.VMEM((T, LANE), F32),
                        pltpu.VMEM((8, LANE), F32)] + sems,
        compiler_params=_cp("arbitrary" if n_ex else "parallel", "arbitrary"),
    )(*args, *exch)
    return (outs[:5], outs[5:]) if n_ex else outs


def _ssd_post(y, xbc, z, dvec, nw, name):
    s = z.shape[0]
    tm = min(256, s)

    def body(y_ref, x_ref, z_ref, dv_ref, nw_ref, o_ref):
        ys = y_ref[...] + dv_ref[...] * x_ref[...]
        yg = ys * _silu(z_ref[...])
        ms = jnp.mean(yg * yg, axis=1, keepdims=True)
        o_ref[...] = (yg * lax.rsqrt(ms + RMS_EPS) * nw_ref[...]).astype(BF16)

    rs = _row_spec(tm, SSM_INNER)
    vs = _vec_spec(SSM_INNER)
    return pl.pallas_call(
        body, name=name, grid=(s // tm,), in_specs=[rs, rs, rs, vs, vs], out_specs=rs,
        out_shape=jax.ShapeDtypeStruct((s, SSM_INNER), BF16), compiler_params=_cp("parallel"),
    )(y, xbc, z, dvec, nw)


def _ssd_post_bwd(dyn, y, xbc, z, dvec, nw, name):
    s = z.shape[0]
    tm = min(256, s)

    def body(dyn_ref, y_ref, x_ref, z_ref, dv_ref, nw_ref, dys_ref, dz_ref, red_ref):
        @pl.when(pl.program_id(0) == 0)
        def _():
            red_ref[...] = jnp.zeros_like(red_ref)

        xv, zv = x_ref[...], z_ref[...]
        ys = y_ref[...] + dv_ref[...] * xv
        sz = _silu(zv)
        yg = ys * sz
        rstd = lax.rsqrt(jnp.mean(yg * yg, axis=1, keepdims=True) + RMS_EPS)
        yhat = yg * rstd
        dynv = dyn_ref[...]
        dyh = dynv * nw_ref[...]
        dyg = rstd * (dyh - yhat * jnp.mean(dyh * yhat, axis=1, keepdims=True))
        dys = dyg * sz
        dys_ref[...] = dys
        dz_ref[...] = (dyg * ys * _dsilu(zv)).astype(BF16)
        red_ref[0:1, :] += jnp.sum(dynv * yhat, axis=0, keepdims=True)
        red_ref[1:2, :] += jnp.sum(dys * xv, axis=0, keepdims=True)

    rs = _row_spec(tm, SSM_INNER)
    vs = _vec_spec(SSM_INNER)
    return pl.pallas_call(
        body, name=name, grid=(s // tm,), in_specs=[rs, rs, rs, rs, vs, vs],
        out_specs=[rs, rs, _vec_spec(SSM_INNER, 8)],
        out_shape=[jax.ShapeDtypeStruct((s, SSM_INNER), F32), jax.ShapeDtypeStruct((s, SSM_INNER), BF16), jax.ShapeDtypeStruct((8, SSM_INNER), F32)],
        compiler_params=_cp("arbitrary"),
    )(dyn, y, xbc, z, dvec, nw)


def _dt_bwd(dt_raw, bias, a_log, dt, ddt, dda, name):
    s = dt_raw.shape[0]
    tm = min(1024, s)

    def body(r_ref, b_ref, a_ref, dt_ref, ddt_ref, dda_ref, o_ref, red_ref):
        @pl.when(pl.program_id(0) == 0)
        def _():
            red_ref[...] = jnp.zeros_like(red_ref)

        a = -jnp.exp(a_ref[...])
        ddav = dda_ref[...]
        draw = (ddt_ref[...] + a * ddav) * _sigmoid(r_ref[...] + b_ref[...])
        o_ref[...] = draw.astype(BF16)
        red_ref[0:1, :] += jnp.sum(draw, axis=0, keepdims=True)
        red_ref[1:2, :] += a * jnp.sum(ddav * dt_ref[...], axis=0, keepdims=True)

    rs = _row_spec(tm, LANE)
    vs = _vec_spec(LANE)
    return pl.pallas_call(
        body, name=name, grid=(s // tm,), in_specs=[rs, vs, vs, rs, rs, rs], out_specs=[rs, _vec_spec(LANE, 8)],
        out_shape=[jax.ShapeDtypeStruct((s, LANE), BF16), jax.ShapeDtypeStruct((8, LANE), F32)],
        compiler_params=_cp("arbitrary"),
    )(dt_raw, bias, a_log, dt, ddt, dda)


def _pad_lanes(v):
    v = v.reshape(1, -1)
    return jnp.pad(v, ((0, 0), (0, LANE - v.shape[1])))


def _ssd_prep_weights(w_in, conv_w, conv_b, dt_bias, a_log, d_skip, norm_w, w_out):
    return dict(
        w_z=w_in[:, :SSM_INNER].astype(BF16),
        w_xbc=w_in[:, SSM_INNER:SSM_INNER + SSM_CONV_DIM].astype(BF16),
        w_dt=jnp.pad(w_in[:, SSM_INNER + SSM_CONV_DIM:], ((0, 0), (0, LANE - 2 * SSM_HEADS))).astype(BF16),
        conv_w=conv_w, conv_b=conv_b.reshape(1, -1), bias=_pad_lanes(dt_bias), a_log=_pad_lanes(a_log),
        dvec=jnp.repeat(d_skip, SSM_HEAD_DIM).reshape(1, -1), nw=norm_w.reshape(1, -1), w_out=w_out.astype(BF16),
    )


def _ssd_layer_fwd(h, w, li):
    nm = lambda t: f"b{li}_{t}"
    z = _mm(h, w["w_z"], name=nm("z"))
    xraw = _mm(h, w["w_xbc"], name=nm("xbc"))
    dt_raw = _mm(h, w["w_dt"], name=nm("dt"))
    pre, xbc = _conv_fwd(xraw, w["conv_w"], w["conv_b"], nm("conv"))
    dt, cum, cumt = _dt_prep(dt_raw, w["bias"], w["a_log"], nm("dtprep"))
    nc = cumt.shape[0]
    dtk, cumk = _by_group(dt), _by_group(cum)
    cumtk = cumt[:, :2 * SSM_HEADS].reshape(nc, 2 * SSM_GROUPS, SSM_REP, T).transpose(1, 0, 2, 3)
    yf, stf = _ssd_scan(xbc, dtk, cumk, cumtk, False, nm("scan_f"))
    y, stb = _ssd_scan(xbc, dtk, cumk, cumtk, True, nm("scan_b"), add=yf)
    yn = _ssd_post(y, xbc, z, w["dvec"], w["nw"], nm("post"))
    out = _mm(yn, w["w_out"], name=nm("out"))
    return out, (z, xraw, dt_raw, pre, xbc, dt, dtk, cumk, cumtk, y, stf, stb, yn)


def _ssd_layer_bwd(dy, h, saved, w, li, exch=((), ())):
    nm = lambda t: f"b{li}_{t}"
    z, xraw, dt_raw, pre, xbc, dt, dtk, cumk, cumtk, y, stf, stb, yn = saved
    g_w_out = _mm(yn, dy, ta=True, out_dtype=BF16, name=nm("dwout"))
    dyn = _mm(dy, w["w_out"], tb=True, name=nm("dyn"))
    dys, dz, pred = _ssd_post_bwd(dyn, y, xbc, z, w["dvec"], w["nw"], nm("postbwd"))
    arrived = [(), ()]
    res = _ssd_scan_bwd(xbc, dtk, cumk, cumtk, dys, stf, w["dvec"], None, False, nm("scanbwd_f"), exch[0])
    if exch[0]:
        res, arrived[0] = res
    dx1, db1, dc1, ddt_f, dda_f = res
    res = _ssd_scan_bwd(xbc, dtk, cumk, cumtk, dys, stb, w["dvec"], (dx1, db1, dc1), True, nm("scanbwd_b"), exch[1])
    if exch[1]:
        res, arrived[1] = res
    dx, db, dc, ddt_b, dda_b = res
    dxraw, cred = _conv_bwd(dx, db, dc, pre, xraw, w["conv_w"], nm("convbwd"))
    draw, dred = _dt_bwd(dt_raw, w["bias"], w["a_log"], dt, _from_group(ddt_f, ddt_b), _from_group(dda_f, dda_b), nm("dtbwd"))
    dh = _mm(dz, w["w_z"], tb=True, name=nm("dh_z"))
    dh = _mm(dxraw, w["w_xbc"], tb=True, add=dh, name=nm("dh_xbc"))
    dh = _mm(draw, w["w_dt"], tb=True, add=dh, name=nm("dh_dt"))
    g_w_in = jnp.concatenate([_mm(h, dz, ta=True, out_dtype=BF16, name=nm("dwz")), _mm(h, dxraw, ta=True, out_dtype=BF16, name=nm("dwxbc")),
                              _mm(h, draw, ta=True, out_dtype=BF16, name=nm("dwdt"))[:, :2 * SSM_HEADS]], axis=1)
    grads = (g_w_in, cred[:SSM_CONV], cred[SSM_CONV], dred[0, :2 * SSM_HEADS].reshape(2, SSM_HEADS),
             dred[1, :2 * SSM_HEADS].reshape(2, SSM_HEADS), pred[1].reshape(SSM_HEADS, SSM_HEAD_DIM).sum(axis=1), pred[0], g_w_out)
    return dh, grads, arrived


B_GRAD_NAMES = ("b_w_in", "b_conv_w", "b_conv_b", "b_dt_bias", "b_a_log", "b_d", "b_norm_w", "b_w_out")


def _shards_of(big, j):
    return [_shard_cols(big["a_w_in"][j]), big["a_w_out"][j].reshape(N_SHARD, -1, big["a_w_out"][j].shape[-1]),
            _shard_cols(big["b_w_in"][j]), big["b_w_out"][j].reshape(N_SHARD, -1, big["b_w_out"][j].shape[-1])]


def _local_step(x, tgt, mod, w, early=False):
    d = x.shape[1]
    qkv_cols = QKV_COLS
    layers = []
    for i in range(DEPTH):
        j = i // 2
        if i % 2 == 0:
            layers.append((w["a_w_in"][j][:, :qkv_cols].astype(BF16), w["a_w_in"][j][:, qkv_cols:].astype(BF16), w["a_w_out"][j].astype(BF16)))
        else:
            layers.append(_ssd_prep_weights(w["b_w_in"][j], w["b_conv_w"][j], w["b_conv_b"][j], w["b_dt_bias"][j], w["b_a_log"][j],
                                            w["b_d"][j], w["b_norm_w"][j], w["b_w_out"][j]))
    saved = []
    for i in range(DEPTH):
        shift, scale, gate = mod[i:i + 1, :d], mod[i:i + 1, d:2 * d], mod[i:i + 1, 2 * d:]
        h = _modulate(x, scale, shift, f"l{i}_mod")
        if i % 2 == 0:
            out, sv = _attn_layer_fwd(h, *layers[i], i)
        else:
            out, sv = _ssd_layer_fwd(h, layers[i], i)
        xn = _resid_ln(x, out, gate, w["ln_g"][i:i + 1], w["ln_b"][i:i + 1], f"l{i}_ln")
        saved.append((x, h, out, sv))
        x = xn
    dx, lred = _loss_grad(x, tgt, "loss")
    loss = 0.5 * jnp.sum(lred[0]) / d
    dmod, g_ln_g, g_ln_b = [None] * DEPTH, [None] * DEPTH, [None] * DEPTH
    ga_in, ga_out = [None, None], [None, None]
    gb = [None, None]
    arrived = None
    for i in reversed(range(DEPTH)):
        j = i // 2
        xi, h, out, sv = saved[i]
        scale, gate = mod[i:i + 1, d:2 * d], mod[i:i + 1, 2 * d:]
        du, dy, red = _resid_ln_bwd(xi, out, dx, gate, w["ln_g"][i:i + 1], f"l{i}_lnbwd")
        g_ln_g[i], g_ln_b[i] = red[1], red[2]
        if i % 2 == 0:
            exch = ()
            if early and j == 0:
                exch = [_shard_cols(gb[0][0]), gb[0][7].reshape(N_SHARD, -1, gb[0][7].shape[-1])]
            dhs, ga_in[j], ga_out[j], got = _attn_layer_bwd(dy, h, sv, *layers[i], i, exch)
            if early and j == 0:
                arrived = (list(got), arrived)
        else:
            exch = ((), ())
            if early and j == 0:
                sh = _shards_of(dict(a_w_in=ga_in, a_w_out=ga_out, b_w_in=[None, gb[1][0]], b_w_out=[None, gb[1][7]]), 1)
                exch = (sh[:2], sh[2:])
            dh, gb[j], got = _ssd_layer_bwd(dy, h, sv, layers[i], i, exch)
            if early and j == 0:
                arrived = list(got[0]) + list(got[1])
            dhs = [dh]
        dx, red2 = _modulate_bwd(du, dhs, xi, scale, f"l{i}_modbwd")
        dmod[i] = jnp.concatenate([red2[1], red2[0], red[0]])
    grads = {"ln_g": jnp.stack(g_ln_g), "ln_b": jnp.stack(g_ln_b), "a_w_in": jnp.stack(ga_in), "a_w_out": jnp.stack(ga_out)}
    for k, n in enumerate(B_GRAD_NAMES):
        grads[n] = jnp.stack([gb[0][k], gb[1][k]])
    big = dict(a_w_in=ga_in, a_w_out=ga_out, b_w_in=[gb[0][0], gb[1][0]], b_w_out=[gb[0][7], gb[1][7]])
    return loss, dx, jnp.stack(dmod), grads, big, arrived


MESH = pl.DeviceIdType.MESH
ANY = pl.BlockSpec(memory_space=pl.ANY)
N_DEV = 8
N_SHARD = 4


def _flip(v, bit):
    return 1 - v if bit else v


def _all_gather8(v, name):
    def body(v_ref, o_ref, send_sems, recv_sems, local_sem):
        x, y, c = lax.axis_index("x"), lax.axis_index("y"), lax.axis_index("c")
        me = 4 * x + 2 * y + c
        local = pltpu.make_async_copy(v_ref, o_ref.at[me], local_sem)
        local.start()
        copies = []
        for k in range(1, N_DEV):
            peer = (_flip(x, k & 4), _flip(y, k & 2), _flip(c, k & 1))
            copies.append(pltpu.make_async_remote_copy(
                src_ref=v_ref, dst_ref=o_ref.at[me], send_sem=send_sems.at[k - 1], recv_sem=recv_sems.at[k - 1],
                device_id=peer, device_id_type=MESH))
        for cp in copies:
            cp.start()
        for cp in copies:
            cp.wait()
        local.wait()

    return pl.pallas_call(
        body, name=name, in_specs=[ANY], out_specs=ANY, out_shape=jax.ShapeDtypeStruct((N_DEV,) + v.shape, v.dtype),
        scratch_shapes=[pltpu.SemaphoreType.DMA((N_DEV - 1,)), pltpu.SemaphoreType.DMA((N_DEV - 1,)), pltpu.SemaphoreType.DMA],
    )(v)


def _shard_exchange(s_refs, o_refs, send_sems, recv_sems, local_sems):
    n = len(s_refs)
    x, y, c = lax.axis_index("x"), lax.axis_index("y"), lax.axis_index("c")
    m = 2 * x + y
    copies = [pltpu.make_async_copy(s_refs[a].at[m], o_refs[a].at[m], local_sems.at[a]) for a in range(n)]
    for k in range(1, N_SHARD):
        px, py = _flip(x, k & 2), _flip(y, k & 1)
        for a in range(n):
            i = (k - 1) * n + a
            copies.append(pltpu.make_async_remote_copy(
                src_ref=s_refs[a].at[2 * px + py], dst_ref=o_refs[a].at[m], send_sem=send_sems.at[i], recv_sem=recv_sems.at[i],
                device_id=(px, py, c), device_id_type=MESH))
    return copies


def _transpose_shards(srcs, name):
    n = len(srcs)
    n_rem = (N_SHARD - 1) * n

    def body(*refs):
        copies = _shard_exchange(refs[:n], refs[n:2 * n], *refs[2 * n:])
        for cp in copies:
            cp.start()
        for cp in copies:
            cp.wait()

    return pl.pallas_call(
        body, name=name, in_specs=[ANY] * n, out_specs=[ANY] * n, out_shape=[jax.ShapeDtypeStruct(s.shape, s.dtype) for s in srcs],
        scratch_shapes=[pltpu.SemaphoreType.DMA((n_rem,)), pltpu.SemaphoreType.DMA((n_rem,)), pltpu.SemaphoreType.DMA((n,))],
    )(*srcs)


def _gather_shards(src, name):
    rows = src.shape[0]
    half = rows // 2
    n_ici = N_SHARD - 1

    def body(s_ref, o_ref, send_sems, recv_sems, local_sem):
        x, y, c = lax.axis_index("x"), lax.axis_index("y"), lax.axis_index("c")
        m = 2 * x + y
        sibling = (x, y, 1 - c)
        my_half = pl.ds(pl.multiple_of(c * half, 16), half)
        its_half = pl.ds(pl.multiple_of((1 - c) * half, 16), half)
        local = pltpu.make_async_copy(s_ref, o_ref.at[m], local_sem)
        local.start()
        chips = [(_flip(x, k & 2), _flip(y, k & 1)) for k in range(1, N_SHARD)]

        def copy(sem, src_ref, dst_ref, to):
            return pltpu.make_async_remote_copy(src_ref=src_ref, dst_ref=dst_ref, send_sem=send_sems.at[sem],
                                                recv_sem=recv_sems.at[sem], device_id=to, device_id_type=MESH)

        first = [copy(i, s_ref.at[my_half], o_ref.at[m, my_half], (px, py, c)) for i, (px, py) in enumerate(chips)]
        for cp in first:
            cp.start()
        passed = []
        for i, (px, py) in enumerate(chips):
            landed = o_ref.at[2 * px + py, my_half]
            copy(i, landed, landed, (px, py, c)).wait_recv()
            passed.append(copy(n_ici + i, landed, landed, sibling))
            passed[-1].start()
        for i, (px, py) in enumerate(chips):
            from_sibling = o_ref.at[2 * px + py, its_half]
            copy(n_ici + i, from_sibling, from_sibling, sibling).wait_recv()
        for cp in first + passed:
            cp.wait_send()
        local.wait()

    return pl.pallas_call(
        body, name=name, in_specs=[ANY], out_specs=ANY, out_shape=jax.ShapeDtypeStruct((N_SHARD,) + src.shape, src.dtype),
        scratch_shapes=[pltpu.SemaphoreType.DMA((2 * n_ici,)), pltpu.SemaphoreType.DMA((2 * n_ici,)), pltpu.SemaphoreType.DMA],
    )(src)


def _swap_sibling(vs, name):
    n = len(vs)

    def body(*refs):
        v_refs, o_refs, (send_sems, recv_sems) = refs[:n], refs[n:2 * n], refs[2 * n:]
        x, y, c = lax.axis_index("x"), lax.axis_index("y"), lax.axis_index("c")
        copies = [pltpu.make_async_remote_copy(src_ref=v_refs[a], dst_ref=o_refs[a], send_sem=send_sems.at[a], recv_sem=recv_sems.at[a],
                                               device_id=(x, y, 1 - c), device_id_type=MESH) for a in range(n)]
        for cp in copies:
            cp.start()
        for cp in copies:
            cp.wait()

    return pl.pallas_call(
        body, name=name, in_specs=[ANY] * n, out_specs=[ANY] * n, out_shape=[jax.ShapeDtypeStruct(v.shape, v.dtype) for v in vs],
        scratch_shapes=[pltpu.SemaphoreType.DMA((n,)), pltpu.SemaphoreType.DMA((n,))],
    )(*vs)


def _row_tile(r, elems, step):
    ok = [t for t in range(step, r + 1, step) if r % t == 0 and t <= elems]
    return max(ok) if ok else r


def _sum_slots(a, name):
    n, r, cdim = a.shape
    tm = _row_tile(r, (4 << 20) // (cdim * 4 * (n + 1)), 16)

    def body(a_ref, o_ref):
        acc = a_ref[0].astype(F32)
        for k in range(1, n):
            acc = acc + a_ref[k].astype(F32)
        o_ref[...] = acc

    return pl.pallas_call(
        body, name=name, grid=(r // tm,), in_specs=[pl.BlockSpec((n, tm, cdim), lambda i: (0, i, 0))],
        out_specs=pl.BlockSpec((tm, cdim), lambda i: (i, 0)), out_shape=jax.ShapeDtypeStruct((r, cdim), F32),
        compiler_params=_cp("parallel"),
    )(a)


def _silu_rows(v, name):
    def body(v_ref, o_ref):
        o_ref[...] = _silu(v_ref[...])

    return pl.pallas_call(body, name=name, out_shape=jax.ShapeDtypeStruct(v.shape, F32))(v)


PACK_COLS = 1024


def _adamw(w, gs, m, v, name):
    r, cdim = w.shape
    tm = _row_tile(r, (1 << 18) // cdim, 8)
    c1 = 1.0 / (1.0 - ADAM_B1 ** ADAM_STEP)
    c2 = 1.0 / (1.0 - ADAM_B2 ** ADAM_STEP)
    ng = len(gs)

    def body(*refs):
        w_ref, g_refs, (m_ref, v_ref, g_ref, d_ref, nm_ref, nv_ref) = refs[0], refs[1:1 + ng], refs[1 + ng:]
        g = g_refs[0][...]
        for t in g_refs[1:]:
            g = g + t[...]
        mn = ADAM_B1 * m_ref[...] + (1.0 - ADAM_B1) * g
        vn = ADAM_B2 * v_ref[...] + (1.0 - ADAM_B2) * (g * g)
        g_ref[...] = g
        nm_ref[...] = mn
        nv_ref[...] = vn
        d_ref[...] = -ADAM_LR * ((mn * c1) / (jnp.sqrt(vn * c2) + ADAM_EPS) + ADAM_WD * w_ref[...])

    spec = pl.BlockSpec((tm, cdim), lambda i: (i, 0))
    return pl.pallas_call(
        body, name=name, grid=(r // tm,), in_specs=[spec] * (3 + ng), out_specs=[spec] * 4,
        out_shape=[jax.ShapeDtypeStruct(w.shape, F32)] * 4, compiler_params=_cp("parallel"),
    )(w, *gs, m, v)


def _rows(a):
    f = a.reshape(-1)
    pad = (-f.shape[0]) % PACK_COLS
    if pad:
        f = jnp.pad(f, (0, pad))
    return f.reshape(-1, PACK_COLS)


def _nrows(shape):
    return -(-int(np.prod(shape)) // PACK_COLS)


def _pack(parts, total_rows=None):
    p = jnp.concatenate([_rows(a) for a in parts], axis=0)
    if total_rows is not None and total_rows > p.shape[0]:
        p = jnp.pad(p, ((0, total_rows - p.shape[0]), (0, 0)))
    return p


def _unpack(p, shapes):
    out, r0 = [], 0
    for shp in shapes:
        n = int(np.prod(shp))
        nr = _nrows(shp)
        out.append(p[r0:r0 + nr].reshape(-1)[:n].reshape(shp))
        r0 += nr
    return out


def _unshard_cols(g):
    return jnp.concatenate([g[k] for k in range(N_SHARD)], axis=-1)


def _shard_cols(a):
    n = a.shape[-1] // N_SHARD
    return jnp.stack([a[..., k * n:(k + 1) * n] for k in range(N_SHARD)])


def _unshard_rows(g):
    return jnp.concatenate([g[k] for k in range(N_SHARD)], axis=1)


def _shard_rows(a):
    n = a.shape[1] // N_SHARD
    return jnp.stack([a[:, k * n:(k + 1) * n] for k in range(N_SHARD)])


W_NAMES = ("ada_w", "ada_b", "ln_g", "ln_b", "a_w_in", "a_w_out", "b_w_in", "b_conv_w", "b_conv_b", "b_dt_bias", "b_a_log", "b_d",
           "b_norm_w", "b_w_out")
BIG = ("a_w_in", "a_w_out", "b_w_in", "b_w_out")
SMALL = ("ada_b", "ln_g", "ln_b", "b_conv_w", "b_conv_b", "b_dt_bias", "b_a_log", "b_d", "b_norm_w")


def kernel(x, c, ada_w, ada_b, ln_g, ln_b, a_w_in, a_w_out, b_w_in, b_conv_w, b_conv_b, b_dt_bias, b_a_log, b_d, b_norm_w, b_w_out, loss_target, m_ada_w, m_ada_b, m_ln_g, m_ln_b, m_a_w_in, m_a_w_out, m_b_w_in, m_b_conv_w, m_b_conv_b, m_b_dt_bias, m_b_a_log, m_b_d, m_b_norm_w, m_b_w_out, v_ada_w, v_ada_b, v_ln_g, v_ln_b, v_a_w_in, v_a_w_out, v_b_w_in, v_b_conv_w, v_b_conv_b, v_b_dt_bias, v_b_a_log, v_b_d, v_b_norm_w, v_b_w_out):
    w = dict(ada_w=ada_w, ada_b=ada_b, ln_g=ln_g, ln_b=ln_b, a_w_in=a_w_in, a_w_out=a_w_out, b_w_in=b_w_in, b_conv_w=b_conv_w,
             b_conv_b=b_conv_b, b_dt_bias=b_dt_bias, b_a_log=b_a_log, b_d=b_d, b_norm_w=b_norm_w, b_w_out=b_w_out)
    mom = dict(ada_w=m_ada_w, ada_b=m_ada_b, ln_g=m_ln_g, ln_b=m_ln_b, a_w_in=m_a_w_in, a_w_out=m_a_w_out, b_w_in=m_b_w_in,
               b_conv_w=m_b_conv_w, b_conv_b=m_b_conv_b, b_dt_bias=m_b_dt_bias, b_a_log=m_b_a_log, b_d=m_b_d, b_norm_w=m_b_norm_w,
               b_w_out=m_b_w_out)
    var = dict(ada_w=v_ada_w, ada_b=v_ada_b, ln_g=v_ln_g, ln_b=v_ln_b, a_w_in=v_a_w_in, a_w_out=v_a_w_out, b_w_in=v_b_w_in,
               b_conv_w=v_b_conv_w, b_conv_b=v_b_conv_b, b_dt_bias=v_b_dt_bias, b_a_log=v_b_a_log, b_d=v_b_d, b_norm_w=v_b_norm_w,
               b_w_out=v_b_w_out)
    ax, ay, ac = lax.axis_index("x"), lax.axis_index("y"), lax.axis_index("c")
    me = 4 * ax + 2 * ay + ac
    shard = 2 * ax + ay
    d = x.shape[-1]
    dsh = ada_w.shape[-1]

    small_in = (c, b_conv_w, b_conv_b, b_norm_w)
    g0 = _all_gather8(_pack(small_in).reshape(-1, LANE), "gather_small_in").reshape(N_DEV, -1, PACK_COLS)
    per_dev = [_unpack(g0[k], [a.shape for a in small_in]) for k in range(N_DEV)]
    c_all = jnp.concatenate([p[0] for p in per_dev], axis=0)
    conv_w_full, conv_b_full, norm_w_full = (_unshard_cols([per_dev[2 * k][t] for k in range(N_SHARD)]) for t in (1, 2, 3))

    cond = _silu_rows(jnp.pad(c_all, ((0, 8), (0, 0))), "cond")
    bias = lax.dynamic_slice_in_dim(ada_b, shard * dsh, dsh, axis=1)
    part = jnp.stack([_mm(cond, ada_w[i], add=jnp.broadcast_to(bias[i], (16, dsh)), name=f"mod{i}")[:N_DEV] for i in range(DEPTH)])
    g1 = _all_gather8(part.reshape(-1, LANE), "gather_mod").reshape(N_DEV, DEPTH, N_DEV, dsh)
    mod_all = _unshard_cols([g1[2 * k] for k in range(N_SHARD)])
    mod = lax.dynamic_index_in_dim(mod_all, me, axis=1, keepdims=False)

    gw = _gather_shards(_pack([w[n] for n in BIG]).astype(BF16), "gather_weights")
    big_sh = [_unpack(gw[k], [w[n].shape for n in BIG]) for k in range(N_SHARD)]
    full = dict(
        ln_g=ln_g, ln_b=ln_b, b_dt_bias=b_dt_bias, b_a_log=b_a_log, b_d=b_d,
        b_conv_w=conv_w_full, b_conv_b=conv_b_full, b_norm_w=norm_w_full,
        a_w_in=_unshard_cols([s[0] for s in big_sh]), a_w_out=_unshard_rows([s[1] for s in big_sh]),
        b_w_in=_unshard_cols([s[2] for s in big_sh]), b_w_out=_unshard_rows([s[3] for s in big_sh]),
    )

    loss, grad_x, dmod, g, big, (arrived_b0, arrived1) = _local_step(x[0], loss_target[0], mod, full, early=True)

    arrived0 = list(_transpose_shards(_shards_of(big, 0)[:2], "scatter_grads")) + arrived_b0
    mine = [jnp.concatenate([_sum_slots(arrived0[a], f"sum0_{n}"), _sum_slots(arrived1[a], f"sum1_{n}")], axis=0)
            for a, n in enumerate(BIG)]
    theirs = _swap_sibling(mine, "swap_grads")

    small_g = (dmod, g["ln_g"], g["ln_b"], g["b_dt_bias"], g["b_a_log"], g["b_d"], g["b_conv_w"], g["b_conv_b"], g["b_norm_w"],
               loss.reshape(1))
    g2 = _all_gather8(_pack(small_g).reshape(-1, LANE), "gather_small_grads")
    tot = _unpack(_sum_slots(g2, "sum_small").reshape(-1, PACK_COLS), [a.shape for a in small_g])
    g_ada_b, g_ln_g, g_ln_b, g_dt_bias, g_a_log, g_d, g_conv_w, g_conv_b, g_norm_w, loss_sum = tot
    dmod_all = g2.reshape(N_DEV, -1)[:, :dmod.size].reshape(N_DEV, DEPTH, 3 * d)
    dmod_mine = lax.dynamic_slice_in_dim(dmod_all, shard * dsh, dsh, axis=2)
    g_ada_w = jnp.stack([_mm(cond, jnp.pad(dmod_mine[:, i], ((0, 8), (0, 0))), ta=True, name=f"dada{i}") for i in range(DEPTH)])
    csh = g_conv_w.shape[-1] // N_SHARD
    nsh = g_norm_w.shape[-1] // N_SHARD
    small_grads = dict(
        ada_w=g_ada_w, ada_b=g_ada_b, ln_g=g_ln_g, ln_b=g_ln_b, b_dt_bias=g_dt_bias, b_a_log=g_a_log, b_d=g_d,
        b_conv_w=lax.dynamic_slice_in_dim(g_conv_w, shard * csh, csh, axis=2),
        b_conv_b=lax.dynamic_slice_in_dim(g_conv_b, shard * csh, csh, axis=1),
        b_norm_w=lax.dynamic_slice_in_dim(g_norm_w, shard * nsh, nsh, axis=1),
    )

    by_name = [{}, {}, {}, {}]

    def update(n, gs):
        two_d = lambda t: t.reshape(-1, t.shape[-1])
        outs = _adamw(two_d(w[n]), [two_d(t) for t in gs], two_d(mom[n]), two_d(var[n]), f"adamw_{n}")
        for t, o in zip(by_name, outs):
            t[n] = o.reshape(w[n].shape)

    for i, n in enumerate(BIG):
        update(n, [mine[i], theirs[i]])
    update("ada_w", [small_grads["ada_w"]])
    rest = SMALL
    rows = -(-sum(_nrows(w[n].shape) for n in rest) // 8) * 8
    packed = _adamw(_pack([w[n] for n in rest], rows), [_pack([small_grads[n] for n in rest], rows)],
                    _pack([mom[n] for n in rest], rows), _pack([var[n] for n in rest], rows), "adamw_small")
    for t, p in zip(by_name, packed):
        t.update(zip(rest, _unpack(p, [w[n].shape for n in rest])))
    return (loss_sum.reshape(()), grad_x[None], *[t[n] for t in by_name for n in W_NAMES])
```

```python
import jax
import jax.numpy as jnp
import numpy as np
from jax import lax
from jax.experimental import pallas as pl
from jax.experimental.pallas import tpu as pltpu

F32 = jnp.float32
BF16 = jnp.bfloat16

DEPTH = 4
A_HEADS = 16
A_HEAD_DIM = 64
A_WIDTH = A_HEADS * A_HEAD_DIM
DILATIONS = (1, 4, 16)
A_RADIUS = 64
A_QBLOCK = 128
SSM_HEADS = 32
SSM_HEAD_DIM = 64
SSM_STATE = 128
SSM_GROUPS = 4
SSM_REP = SSM_HEADS // SSM_GROUPS
SSM_CONV = 5
SSM_CHUNK = 128
DEEPNORM_ALPHA = (2 * DEPTH) ** 0.25
LN_EPS = 1e-5
RMS_EPS = 1e-5
ADAM_LR, ADAM_B1, ADAM_B2, ADAM_EPS, ADAM_WD, ADAM_STEP = 0.001, 0.9, 0.999, 1e-08, 0.01, 10
VMEM_LIMIT = 56 * 1024 * 1024
LANE = 128


def _cp(*sem):
    return pltpu.CompilerParams(dimension_semantics=sem, vmem_limit_bytes=VMEM_LIMIT)


def _tile(dim, target):
    if dim <= target:
        return dim
    t = (target // LANE) * LANE
    while dim % t:
        t -= LANE
    return t


def _sigmoid(x):
    return 1.0 / (1.0 + jnp.exp(-x))


def _silu(x):
    return x * _sigmoid(x)


def _dsilu(x):
    s = _sigmoid(x)
    return s * (1.0 + x * (1.0 - s))


def _split3(x):
    a = x.astype(BF16)
    r = x - a.astype(F32)
    b = r.astype(BF16)
    c = (r - b.astype(F32)).astype(BF16)
    return a, b, c


def _dot(a, b, ca=1, cb=0):
    return lax.dot_general(a, b, (((ca,), (cb,)), ((), ())), preferred_element_type=F32)


def _dot_exact(m01, x):
    a, b, c = _split3(x)
    return _dot(m01, a) + _dot(m01, b) + _dot(m01, c)


def _mm(a, b, *, ta=False, tb=False, add=None, out_dtype=F32, name, tm=1024, tn=1024, tk=2048):
    m, k = (a.shape[1], a.shape[0]) if ta else a.shape
    n = b.shape[0] if tb else b.shape[1]
    assert (b.shape[1] if tb else b.shape[0]) == k
    tm, tn, tk = _tile(m, tm), _tile(n, tn), _tile(k, tk)
    nk = k // tk
    has_add = add is not None

    def body(*refs):
        if has_add:
            a_ref, b_ref, c_ref, o_ref, acc = refs
        else:
            a_ref, b_ref, o_ref, acc = refs
        kk = pl.program_id(2)
        part = _dot(a_ref[...].astype(BF16), b_ref[...].astype(BF16), 0 if ta else 1, 1 if tb else 0)

        def finish(r):
            if has_add:
                r = r + c_ref[...]
            o_ref[...] = r.astype(o_ref.dtype)

        if nk == 1:
            finish(part)
            return

        @pl.when(kk == 0)
        def _():
            acc[...] = part

        @pl.when((kk > 0) & (kk < nk - 1))
        def _():
            acc[...] += part

        @pl.when(kk == nk - 1)
        def _():
            finish(acc[...] + part)

    a_spec = pl.BlockSpec((tk, tm), lambda i, j, kk: (kk, i)) if ta else pl.BlockSpec((tm, tk), lambda i, j, kk: (i, kk))
    b_spec = pl.BlockSpec((tn, tk), lambda i, j, kk: (j, kk)) if tb else pl.BlockSpec((tk, tn), lambda i, j, kk: (kk, j))
    in_specs = [a_spec, b_spec]
    args = [a, b]
    if has_add:
        in_specs.append(pl.BlockSpec((tm, tn), lambda i, j, kk: (i, j)))
        args.append(add)
    return pl.pallas_call(
        body, name=name, grid=(m // tm, n // tn, nk), in_specs=in_specs,
        out_specs=pl.BlockSpec((tm, tn), lambda i, j, kk: (i, j)),
        out_shape=jax.ShapeDtypeStruct((m, n), out_dtype),
        scratch_shapes=[pltpu.VMEM((tm, tn) if nk > 1 else (8, LANE), F32)],
        compiler_params=_cp("parallel", "parallel", "arbitrary"),
    )(*args)


ROWS = 512


def _row_spec(tm, d):
    return pl.BlockSpec((tm, d), lambda i: (i, 0))


def _vec_spec(d, rows=1):
    return pl.BlockSpec((rows, d), lambda i: (0, 0))


def _modulate(x, scale, shift, name):
    s, d = x.shape
    tm = min(ROWS, s)

    def body(x_ref, sc_ref, sh_ref, o_ref):
        o_ref[...] = (x_ref[...] * (1.0 + sc_ref[...]) + sh_ref[...]).astype(BF16)

    return pl.pallas_call(
        body, name=name, grid=(s // tm,), in_specs=[_row_spec(tm, d), _vec_spec(d), _vec_spec(d)],
        out_specs=_row_spec(tm, d), out_shape=jax.ShapeDtypeStruct((s, d), BF16), compiler_params=_cp("parallel"),
    )(x, scale, shift)


def _resid_ln(x, y, gate, g, b, name):
    s, d = x.shape
    tm = min(ROWS, s)

    def body(x_ref, y_ref, gt_ref, g_ref, b_ref, o_ref):
        u = DEEPNORM_ALPHA * x_ref[...] + gt_ref[...] * y_ref[...]
        mu = jnp.mean(u, axis=1, keepdims=True)
        uc = u - mu
        var = jnp.mean(uc * uc, axis=1, keepdims=True)
        o_ref[...] = uc * lax.rsqrt(var + LN_EPS) * g_ref[...] + b_ref[...]

    return pl.pallas_call(
        body, name=name, grid=(s // tm,),
        in_specs=[_row_spec(tm, d), _row_spec(tm, d), _vec_spec(d), _vec_spec(d), _vec_spec(d)],
        out_specs=_row_spec(tm, d), out_shape=jax.ShapeDtypeStruct((s, d), F32), compiler_params=_cp("parallel"),
    )(x, y, gate, g, b)


def _resid_ln_bwd(x, y, dxn, gate, g, name):
    s, d = x.shape
    tm = min(ROWS, s)

    def body(x_ref, y_ref, dxn_ref, gt_ref, g_ref, du_ref, dy_ref, red_ref):
        @pl.when(pl.program_id(0) == 0)
        def _():
            red_ref[...] = jnp.zeros_like(red_ref)

        yv = y_ref[...]
        u = DEEPNORM_ALPHA * x_ref[...] + gt_ref[...] * yv
        mu = jnp.mean(u, axis=1, keepdims=True)
        uc = u - mu
        var = jnp.mean(uc * uc, axis=1, keepdims=True)
        rstd = lax.rsqrt(var + LN_EPS)
        xhat = uc * rstd
        dxnv = dxn_ref[...]
        dxh = dxnv * g_ref[...]
        du = rstd * (dxh - jnp.mean(dxh, axis=1, keepdims=True) - xhat * jnp.mean(dxh * xhat, axis=1, keepdims=True))
        du_ref[...] = du
        dy_ref[...] = (du * gt_ref[...]).astype(BF16)
        red_ref[0:1, :] += jnp.sum(du * yv, axis=0, keepdims=True)
        red_ref[1:2, :] += jnp.sum(dxnv * xhat, axis=0, keepdims=True)
        red_ref[2:3, :] += jnp.sum(dxnv, axis=0, keepdims=True)

    return pl.pallas_call(
        body, name=name, grid=(s // tm,),
        in_specs=[_row_spec(tm, d), _row_spec(tm, d), _row_spec(tm, d), _vec_spec(d), _vec_spec(d)],
        out_specs=[_row_spec(tm, d), _row_spec(tm, d), _vec_spec(d, 8)],
        out_shape=[jax.ShapeDtypeStruct((s, d), F32), jax.ShapeDtypeStruct((s, d), BF16), jax.ShapeDtypeStruct((8, d), F32)],
        compiler_params=_cp("arbitrary"),
    )(x, y, dxn, gate, g)


def _modulate_bwd(du, dhs, x, scale, name):
    s, d = x.shape
    tm = min(ROWS, s)
    n = len(dhs)

    def body(*refs):
        du_ref, dh_refs, (x_ref, sc_ref, dx_ref, red_ref) = refs[0], refs[1:1 + n], refs[1 + n:]

        @pl.when(pl.program_id(0) == 0)
        def _():
            red_ref[...] = jnp.zeros_like(red_ref)

        dhv = dh_refs[0][...]
        for t in dh_refs[1:]:
            dhv = dhv + t[...]
        dx_ref[...] = DEEPNORM_ALPHA * du_ref[...] + dhv * (1.0 + sc_ref[...])
        red_ref[0:1, :] += jnp.sum(dhv * x_ref[...], axis=0, keepdims=True)
        red_ref[1:2, :] += jnp.sum(dhv, axis=0, keepdims=True)

    return pl.pallas_call(
        body, name=name, grid=(s // tm,),
        in_specs=[_row_spec(tm, d)] * (n + 2) + [_vec_spec(d)],
        out_specs=[_row_spec(tm, d), _vec_spec(d, 8)],
        out_shape=[jax.ShapeDtypeStruct((s, d), F32), jax.ShapeDtypeStruct((8, d), F32)],
        compiler_params=_cp("arbitrary"),
    )(du, *dhs, x, scale)


def _loss_grad(xf, tgt, name):
    s, d = xf.shape
    tm = min(ROWS, s)

    def body(x_ref, t_ref, dx_ref, red_ref):
        @pl.when(pl.program_id(0) == 0)
        def _():
            red_ref[...] = jnp.zeros_like(red_ref)

        e = x_ref[...] - t_ref[...]
        dx_ref[...] = e * (1.0 / d)
        red_ref[0:1, :] += jnp.sum(e * e, axis=0, keepdims=True)

    return pl.pallas_call(
        body, name=name, grid=(s // tm,), in_specs=[_row_spec(tm, d), _row_spec(tm, d)],
        out_specs=[_row_spec(tm, d), _vec_spec(d, 8)],
        out_shape=[jax.ShapeDtypeStruct((s, d), F32), jax.ShapeDtypeStruct((8, d), F32)],
        compiler_params=_cp("arbitrary"),
    )(xf, tgt)


QKV_COLS = 3 * 3 * A_WIDTH


SLOPES = tuple(float(2.0 ** (-8.0 * (h + 1.0) / A_HEADS)) for h in range(A_HEADS))
FAR = 1e30
HEAD_COLS = tuple(slice(h * A_HEAD_DIM, (h + 1) * A_HEAD_DIM) for h in range(A_HEADS))


def _band_dist(n, length, dil, span_rows):
    shape = (2 * A_QBLOCK, A_QBLOCK) if span_rows else (A_QBLOCK, 2 * A_QBLOCK)
    r = lax.broadcasted_iota(jnp.int32, shape, 0)
    c = lax.broadcasted_iota(jnp.int32, shape, 1)
    sp, ce = (r, c) if span_rows else (c, r)
    delta = sp - A_RADIUS - ce
    pos = n * A_QBLOCK - A_RADIUS + sp
    valid = (jnp.abs(delta) <= A_RADIUS) & (pos >= 0) & (pos < length)
    return jnp.where(valid, jnp.abs(delta).astype(F32) * float(dil), FAR)


def _span_specs(col, nb64):
    def mk(i):
        return pl.BlockSpec((64, A_WIDTH), lambda r, n: (r * nb64 + jnp.clip(2 * n - 1 + i, 0, nb64 - 1), col))
    return [mk(i) for i in range(4)]


def _to_residue(t, dil):
    if dil == 1:
        return t
    s, c = t.shape
    return t.reshape(s // dil, dil, c).transpose(1, 0, 2).reshape(s, c)


def _from_residue(t, dil):
    if dil == 1:
        return t
    s, c = t.shape
    return t.reshape(dil, s // dil, c).transpose(1, 0, 2).reshape(s, c)


def _cat(refs):
    return jnp.concatenate([t[...] for t in refs], axis=0)


def _head_expander():
    r = lax.broadcasted_iota(jnp.int32, (A_HEADS, A_WIDTH), 0)
    c = lax.broadcasted_iota(jnp.int32, (A_HEADS, A_WIDTH), 1)
    return ((c >= r * A_HEAD_DIM) & (c < (r + 1) * A_HEAD_DIM)).astype(BF16)


def _to_lanes(x16, e):
    a, b, c = _split3(x16)
    return _dot(a, e) + _dot(b, e) + _dot(c, e)


def _per_head_sum(x, e):
    a, b, c = _split3(x)
    return _dot(a, e, 1, 1) + _dot(b, e, 1, 1) + _dot(c, e, 1, 1)


def _pair_low_lanes():
    return lax.broadcasted_iota(jnp.int32, (A_QBLOCK, LANE), 1) < A_HEAD_DIM


def _top_rows():
    return lax.broadcasted_iota(jnp.int32, (2 * A_QBLOCK, 1), 0) < A_QBLOCK


def _block_diag(v, low):
    zero = jnp.zeros_like(v)
    return jnp.concatenate([jnp.where(low, v, zero), jnp.where(low, zero, v)], axis=0)


def _attn_fwd(qkv, g, name):
    s = qkv.shape[0]
    dil = DILATIONS[g]
    length = s // dil
    nblk = length // A_QBLOCK

    def body(q_ref, k0, k1, k2, k3, v0, v1, v2, v3, o_ref, l_ref):
        dist = _band_dist(pl.program_id(1), length, dil, False)
        kk = _cat((k0, k1, k2, k3))
        vv = _cat((v0, v1, v2, v3))
        low = _pair_low_lanes()
        top = _top_rows()
        dist2 = jnp.concatenate([dist, dist], axis=0)
        for hp in range(A_HEADS // 2):
            ls = slice(hp * LANE, (hp + 1) * LANE)
            qp, kp, vp = q_ref[:, ls], kk[:, ls], vv[:, ls]
            sc = _dot(_block_diag(qp, low), kp, 1, 1) * 0.125 - jnp.where(top, SLOPES[2 * hp], SLOPES[2 * hp + 1]) * dist2
            m = jnp.max(sc, axis=1, keepdims=True)
            p = jnp.exp(sc - m)
            z = jnp.sum(p, axis=1, keepdims=True)
            o2 = _dot(p.astype(BF16), vp) / z
            lse2 = m + jnp.log(z)
            l_ref[:, 2 * hp:2 * hp + 1] = lse2[:A_QBLOCK]
            l_ref[:, 2 * hp + 1:2 * hp + 2] = lse2[A_QBLOCK:]
            o_ref[:, ls] = jnp.where(low, o2[:A_QBLOCK], o2[A_QBLOCK:])

    qspec = pl.BlockSpec((A_QBLOCK, A_WIDTH), lambda r, n: (r * nblk + n, 0))
    lspec = pl.BlockSpec((A_QBLOCK, A_HEADS), lambda r, n: (r * nblk + n, 0))
    return pl.pallas_call(
        body, name=name, grid=(dil, nblk), in_specs=[qspec] + _span_specs(1, 2 * nblk) + _span_specs(2, 2 * nblk),
        out_specs=[qspec, lspec],
        out_shape=[jax.ShapeDtypeStruct((s, A_WIDTH), F32), jax.ShapeDtypeStruct((s, A_HEADS), F32)],
        compiler_params=_cp("parallel", "parallel"),
    )(*([qkv] * 9))


def _attn_merge(os_, ls_, gate, name):
    s, w = gate.shape
    tm = min(ROWS, s)

    def body(o0, o1, o2, l0, l1, l2, g_ref, y_ref, o_ref, l_ref):
        a, b, c = l0[...], l1[...], l2[...]
        m = jnp.maximum(jnp.maximum(a, b), c)
        ea, eb, ec = jnp.exp(a - m), jnp.exp(b - m), jnp.exp(c - m)
        z = ea + eb + ec
        l_ref[...] = m + jnp.log(z)
        e = _head_expander()
        o = _to_lanes(ea / z, e) * o0[...] + _to_lanes(eb / z, e) * o1[...] + _to_lanes(ec / z, e) * o2[...]
        o_ref[...] = o
        y_ref[...] = (o * _silu(g_ref[...])).astype(BF16)

    rs = _row_spec(tm, w)
    ls = _row_spec(tm, A_HEADS)
    return pl.pallas_call(
        body, name=name, grid=(s // tm,), in_specs=[rs] * 3 + [ls] * 3 + [rs], out_specs=[rs, rs, ls],
        out_shape=[jax.ShapeDtypeStruct((s, w), BF16), jax.ShapeDtypeStruct((s, w), F32), jax.ShapeDtypeStruct((s, A_HEADS), F32)],
        compiler_params=_cp("parallel"),
    )(*os_, *ls_, gate)


def _attn_gate_bwd(dyy, o, gate, name):
    s, w = gate.shape
    tm = min(ROWS, s)

    def body(dy_ref, o_ref, g_ref, do_ref, dg_ref, dl_ref):
        dyv, ov, gv = dy_ref[...], o_ref[...], g_ref[...]
        do = dyv * _silu(gv)
        do_ref[...] = do.astype(BF16)
        dg_ref[...] = (dyv * ov * _dsilu(gv)).astype(BF16)
        dl_ref[...] = _per_head_sum(do * ov, _head_expander())

    rs = _row_spec(tm, w)
    return pl.pallas_call(
        body, name=name, grid=(s // tm,), in_specs=[rs] * 3, out_specs=[rs, rs, _row_spec(tm, A_HEADS)],
        out_shape=[jax.ShapeDtypeStruct((s, w), BF16), jax.ShapeDtypeStruct((s, w), BF16), jax.ShapeDtypeStruct((s, A_HEADS), F32)],
        compiler_params=_cp("parallel"),
    )(dyy, o, gate)


def _ride_along(ex_in, ex_out, sems, first, last):
    @pl.when(first)
    def _():
        for cp in _shard_exchange(ex_in, ex_out, *sems):
            cp.start()

    @pl.when(last)
    def _():
        for cp in _shard_exchange(ex_in, ex_out, *sems):
            cp.wait()


def _ride_along_specs(exch):
    n = len(exch)
    n_rem = (N_SHARD - 1) * n
    sems = [pltpu.SemaphoreType.DMA((n_rem,)), pltpu.SemaphoreType.DMA((n_rem,)), pltpu.SemaphoreType.DMA((n,))] if n else []
    return [ANY] * n, [jax.ShapeDtypeStruct(t.shape, t.dtype) for t in exch], sems


def _attn_bwd(qkv, do, lse, delta, g, name, exch=()):
    s = qkv.shape[0]
    dil = DILATIONS[g]
    length = s // dil
    nblk = length // A_QBLOCK
    n_ex = len(exch)

    def rows(t16):
        return jnp.pad(t16.reshape(dil, length, A_HEADS).transpose(0, 2, 1), ((0, 0), (0, 0), (A_RADIUS, A_RADIUS)))

    def body(*refs):
        (q0, q1, q2, q3, k0, k1, k2, k3, v0, v1, v2, v3, d0, d1, d2, d3, lc_ref, ec_ref, la, lb, ea, eb), refs = refs[:22], refs[22:]
        o_ref = refs[n_ex]
        if n_ex:
            r, n = pl.program_id(0), pl.program_id(1)
            _ride_along(refs[:n_ex], refs[n_ex + 1:2 * n_ex + 1], refs[2 * n_ex + 1:], (r == 0) & (n == 0), (r == dil - 1) & (n == nblk - 1))
        dist = _band_dist(pl.program_id(1), length, dil, False)
        qq, kk, vv, dd = _cat((q0, q1, q2, q3)), _cat((k0, k1, k2, k3)), _cat((v0, v1, v2, v3)), _cat((d0, d1, d2, d3))
        lse_r = jnp.concatenate([la[...], lb[...]], axis=1)
        dlt_r = jnp.concatenate([ea[...], eb[...]], axis=1)
        low = _pair_low_lanes()
        top = _top_rows()
        dist2 = jnp.concatenate([dist, dist], axis=0)
        centre = slice(A_RADIUS, A_RADIUS + A_QBLOCK)
        for hp in range(A_HEADS // 2):
            ls = slice(hp * LANE, (hp + 1) * LANE)
            qs, ks, vs, ds_ = qq[:, ls], kk[:, ls], vv[:, ls], dd[:, ls]
            qn, kn, vn, dn = qs[centre], ks[centre], vs[centre], ds_[centre]
            h0, h1 = 2 * hp, 2 * hp + 1
            bias = jnp.where(top, SLOPES[h0], SLOPES[h1]) * dist2
            lc = jnp.concatenate([lc_ref[:, h0:h0 + 1], lc_ref[:, h1:h1 + 1]], axis=0)
            ec = jnp.concatenate([ec_ref[:, h0:h0 + 1], ec_ref[:, h1:h1 + 1]], axis=0)
            lr = jnp.where(top, lse_r[h0:h0 + 1, :], lse_r[h1:h1 + 1, :])
            er = jnp.where(top, dlt_r[h0:h0 + 1, :], dlt_r[h1:h1 + 1, :])
            p = jnp.exp(_dot(_block_diag(qn, low), ks, 1, 1) * 0.125 - bias - lc)
            dsc = p * (_dot(_block_diag(dn, low), vs, 1, 1) - ec)
            dq = _dot(dsc.astype(BF16), ks)
            pt = jnp.exp(_dot(_block_diag(kn, low), qs, 1, 1) * 0.125 - bias - lr)
            dst = pt * (_dot(_block_diag(vn, low), ds_, 1, 1) - er)
            dk = _dot(dst.astype(BF16), qs)
            dv = _dot(pt.astype(BF16), ds_)
            merge = lambda t: jnp.where(low, t[:A_QBLOCK], t[A_QBLOCK:])
            o_ref[:, ls] = (merge(dq) * 0.125).astype(BF16)
            o_ref[:, A_WIDTH + hp * LANE:A_WIDTH + (hp + 1) * LANE] = (merge(dk) * 0.125).astype(BF16)
            o_ref[:, 2 * A_WIDTH + hp * LANE:2 * A_WIDTH + (hp + 1) * LANE] = merge(dv).astype(BF16)

    nb64 = 2 * nblk
    dspecs = _span_specs(0, nb64)
    cspec = pl.BlockSpec((A_QBLOCK, A_HEADS), lambda r, n: (r * nblk + n, 0))
    rspecs = [pl.BlockSpec((None, A_HEADS, A_QBLOCK), lambda r, n: (r, 0, n)), pl.BlockSpec((None, A_HEADS, A_QBLOCK), lambda r, n: (r, 0, n + 1))]
    lse_r, dlt_r = rows(lse), rows(delta)
    ex_specs, ex_shapes, sems = _ride_along_specs(exch)
    outs = pl.pallas_call(
        body, name=name, grid=(dil, nblk),
        in_specs=_span_specs(0, nb64) + _span_specs(1, nb64) + _span_specs(2, nb64) + dspecs + [cspec, cspec] + rspecs * 2 + ex_specs,
        out_specs=[pl.BlockSpec((A_QBLOCK, 3 * A_WIDTH), lambda r, n: (r * nblk + n, 0))] + ex_specs,
        out_shape=[jax.ShapeDtypeStruct((s, 3 * A_WIDTH), BF16)] + ex_shapes,
        scratch_shapes=sems,
        compiler_params=_cp(*(("arbitrary", "arbitrary") if n_ex else ("parallel", "parallel"))),
    )(*([qkv] * 12), *([do] * 4), lse, delta, lse_r, lse_r, dlt_r, dlt_r, *exch)
    return (outs[0], outs[1:]) if n_ex else outs[0]


def _attn_layer_fwd(h, w_qkv, w_gate, w_out, li):
    nm = lambda t: f"a{li}_{t}"
    gate = _mm(h, w_gate, name=nm("gate"))
    hs, qkvs, os_, ls_ = [], [], [], []
    for g, dil in enumerate(DILATIONS):
        hg = _to_residue(h, dil)
        qkv = _mm(hg, w_qkv[:, g * 3 * A_WIDTH:(g + 1) * 3 * A_WIDTH], out_dtype=BF16, name=nm(f"qkv{g}"))
        o, l = _attn_fwd(qkv, g, nm(f"attn{g}"))
        hs.append(hg)
        qkvs.append(qkv)
        os_.append(_from_residue(o, dil))
        ls_.append(_from_residue(l, dil))
    y, o, lse = _attn_merge(os_, ls_, gate, nm("merge"))
    out = _mm(y, w_out, name=nm("out"))
    return out, (hs, qkvs, gate, y, o, lse)


def _attn_layer_bwd(dy, h, saved, w_qkv, w_gate, w_out, li, exch=()):
    nm = lambda t: f"a{li}_{t}"
    hs, qkvs, gate, y, o, lse = saved
    g_w_out = _mm(y, dy, ta=True, out_dtype=BF16, name=nm("dwout"))
    dyy = _mm(dy, w_out, tb=True, name=nm("dyy"))
    do, dgate, delta = _attn_gate_bwd(dyy, o, gate, nm("gatebwd"))
    dhs, dws, arrived = [], [], ()
    for g, dil in enumerate(DILATIONS):
        dqkv = _attn_bwd(qkvs[g], _to_residue(do, dil), _to_residue(lse, dil), _to_residue(delta, dil), g, nm(f"attnbwd{g}"),
                         exch if g == 0 else ())
        if g == 0 and exch:
            dqkv, arrived = dqkv
        wg = w_qkv[:, g * 3 * A_WIDTH:(g + 1) * 3 * A_WIDTH]
        dws.append(_mm(hs[g], dqkv, ta=True, out_dtype=BF16, name=nm(f"dwqkv{g}")))
        add = _mm(dgate, w_gate, tb=True, name=nm("dh_gate")) if g == 0 else None
        dhs.append(_from_residue(_mm(dqkv, wg, tb=True, add=add, name=nm(f"dh_qkv{g}")), dil))
    g_w_in = jnp.concatenate(dws + [_mm(h, dgate, ta=True, out_dtype=BF16, name=nm("dwgate"))], axis=1)
    return dhs, g_w_in, g_w_out, arrived


SSM_INNER = SSM_HEADS * SSM_HEAD_DIM
SSM_BC = SSM_GROUPS * SSM_STATE
SSM_CONV_DIM = SSM_INNER + 2 * SSM_BC
GW = SSM_REP * SSM_HEAD_DIM
T = SSM_CHUNK
HALO = 8


def _conv_specs(tm, tn, s, col=lambda j: j):
    nb8 = s // HALO
    cur = pl.BlockSpec((tm, tn), lambda j, i: (i, col(j)))
    prev = pl.BlockSpec((HALO, tn), lambda j, i: (jnp.maximum(i * (tm // HALO) - 1, 0), col(j)))
    nxt = pl.BlockSpec((HALO, tn), lambda j, i: (jnp.minimum((i + 1) * (tm // HALO), nb8 - 1), col(j)))
    return [prev, cur, nxt]


def _extend(prev_ref, cur_ref, nxt_ref, i, nrow):
    p = jnp.where(i == 0, 0.0, prev_ref[...])
    n = jnp.where(i == nrow - 1, 0.0, nxt_ref[...])
    return jnp.concatenate([p, cur_ref[...], n], axis=0)


def _shift_rows(ext, off, tm):
    rows = ext.shape[0]
    return pltpu.roll(ext, (-off) % rows, 0)[HALO:HALO + tm]


def _conv_fwd(xraw, w, b, name):
    s, cdim = xraw.shape
    tm, tn = min(256, s), 1024
    nrow = s // tm

    def body(p_ref, c_ref, n_ref, w_ref, b_ref, pre_ref, act_ref):
        ext = _extend(p_ref, c_ref, n_ref, pl.program_id(1), nrow)
        acc = jnp.broadcast_to(b_ref[...], (tm, tn))
        for k in range(SSM_CONV):
            acc = acc + w_ref[k:k + 1, :] * _shift_rows(ext, k - SSM_CONV // 2, tm)
        pre_ref[...] = acc
        act_ref[...] = _silu(acc)

    prev, cur, nxt = _conv_specs(tm, tn, s)
    return pl.pallas_call(
        body, name=name, grid=(cdim // tn, nrow),
        in_specs=[prev, cur, nxt, pl.BlockSpec((SSM_CONV, tn), lambda j, i: (0, j)), pl.BlockSpec((1, tn), lambda j, i: (0, j))],
        out_specs=[cur, cur], out_shape=[jax.ShapeDtypeStruct((s, cdim), F32)] * 2,
        compiler_params=_cp("parallel", "parallel"),
    )(xraw, xraw, xraw, w, b)


def _conv_bwd(dx, db, dc, pre, xraw, w, name):
    s, cdim = xraw.shape
    tm, tn = min(256, s), 1024
    nrow = s // tm
    nx = dx.shape[1] // tn

    def body(xp, xc, xn, bp, bc, bn, cp, cc, cn, pp, pc, pn, x_ref, w_ref, o_ref, red_ref):
        j, i = pl.program_id(0), pl.program_id(1)

        @pl.when(i == 0)
        def _():
            red_ref[...] = jnp.zeros_like(red_ref)

        bcext = jnp.concatenate([_extend(bp, bc, bn, i, nrow), _extend(cp, cc, cn, i, nrow)], axis=1)
        dact = jnp.where(j < nx, _extend(xp, xc, xn, i, nrow), bcext)
        dpre = dact * _dsilu(_extend(pp, pc, pn, i, nrow))
        xv = x_ref[...]
        acc = jnp.zeros((tm, tn), F32)
        for k in range(SSM_CONV):
            sk = _shift_rows(dpre, SSM_CONV // 2 - k, tm)
            acc = acc + w_ref[k:k + 1, :] * sk
            red_ref[k:k + 1, :] += jnp.sum(sk * xv, axis=0, keepdims=True)
        red_ref[SSM_CONV:SSM_CONV + 1, :] += jnp.sum(dpre[HALO:HALO + tm], axis=0, keepdims=True)
        o_ref[...] = acc.astype(BF16)

    half = tn // 2
    cur = pl.BlockSpec((tm, tn), lambda j, i: (i, j))
    return pl.pallas_call(
        body, name=name, grid=(cdim // tn, nrow),
        in_specs=_conv_specs(tm, tn, s, lambda j: jnp.minimum(j, nx - 1)) + _conv_specs(tm, half, s, lambda j: 0) * 2
        + _conv_specs(tm, tn, s) + [cur, pl.BlockSpec((SSM_CONV, tn), lambda j, i: (0, j))],
        out_specs=[cur, pl.BlockSpec((8, tn), lambda j, i: (0, j))],
        out_shape=[jax.ShapeDtypeStruct((s, cdim), BF16), jax.ShapeDtypeStruct((8, cdim), F32)],
        compiler_params=_cp("parallel", "arbitrary"),
    )(dx, dx, dx, db, db, db, dc, dc, dc, pre, pre, pre, xraw, w)


def _tri(lower):
    r = lax.broadcasted_iota(jnp.int32, (T, T), 0)
    c = lax.broadcasted_iota(jnp.int32, (T, T), 1)
    return (r >= c) if lower else (r <= c)


def _softplus(x):
    return jnp.maximum(x, 0.0) + jnp.log(1.0 + jnp.exp(-jnp.abs(x)))


def _dt_prep(dt_raw, bias, a_log, name):
    s = dt_raw.shape[0]
    nc = s // T

    def body(r_ref, b_ref, a_ref, dt_ref, cum_ref, cumt_ref):
        dt = _softplus(r_ref[...] + b_ref[...])
        da = dt * (-jnp.exp(a_ref[...]))
        pre = _dot_exact(_tri(True).astype(BF16), da)
        suf = _dot_exact(_tri(False).astype(BF16), da)
        lane = lax.broadcasted_iota(jnp.int32, (T, LANE), 1)
        cum = jnp.where(lane < SSM_HEADS, pre, suf)
        dt_ref[...] = dt
        cum_ref[...] = cum
        cumt_ref[...] = cum.T

    blk = pl.BlockSpec((T, LANE), lambda c: (c, 0))
    vec = pl.BlockSpec((1, LANE), lambda c: (0, 0))
    return pl.pallas_call(
        body, name=name, grid=(nc,), in_specs=[blk, vec, vec],
        out_specs=[blk, blk, pl.BlockSpec((None, LANE, T), lambda c: (c, 0, 0))],
        out_shape=[jax.ShapeDtypeStruct((s, LANE), F32), jax.ShapeDtypeStruct((s, LANE), F32), jax.ShapeDtypeStruct((nc, LANE, T), F32)],
        compiler_params=_cp("parallel"),
    )(dt_raw, bias, a_log)


def _by_group(t):
    s = t.shape[0]
    return t[:, :2 * SSM_HEADS].reshape(s, 2 * SSM_GROUPS, SSM_REP).transpose(1, 0, 2)


def _from_group(tf, tb):
    s = tf.shape[1]
    t = jnp.concatenate([tf, tb], axis=0).transpose(1, 0, 2).reshape(s, 2 * SSM_HEADS)
    return jnp.pad(t, ((0, 0), (0, LANE - 2 * SSM_HEADS)))


def _decay_mats(acol, arow, rev):
    after = _tri(not rev)
    return jnp.where(after, jnp.exp(jnp.where(after, acol - arow, 0.0)), 0.0)


PAIRS = SSM_REP // 2


def _low_lanes():
    return lax.broadcasted_iota(jnp.int32, (T, LANE), 1) < SSM_HEAD_DIM


CPS = 8
TB = CPS * T


def _scan_specs(rev, ci):
    nxb = SSM_INNER // LANE
    kofs = SSM_GROUPS if rev else 0
    return [
        pl.BlockSpec((TB, GW), lambda g, c: (ci(c), g)),
        pl.BlockSpec((TB, LANE), lambda g, c: (ci(c), nxb + g)),
        pl.BlockSpec((TB, LANE), lambda g, c: (ci(c), nxb + SSM_GROUPS + g)),
        pl.BlockSpec((None, TB, SSM_REP), lambda g, c: (kofs + g, ci(c), 0)),
        pl.BlockSpec((None, TB, SSM_REP), lambda g, c: (kofs + g, ci(c), 0)),
        pl.BlockSpec((None, CPS, SSM_REP, T), lambda g, c: (kofs + g, ci(c), 0, 0)),
    ]


def _chunk_rows(q):
    return pl.ds(q * T, T)


def _pair_lanes(ref, p, low):
    return jnp.where(low, ref[:, 2 * p:2 * p + 1], ref[:, 2 * p + 1:2 * p + 2])


def _ssd_scan(xbc, dtk, cumk, cumtk, rev, name, add=None):
    s = xbc.shape[0]
    nc = s // T
    nb = nc // CPS
    last = 0 if rev else T - 1
    ci = (lambda c: nb - 1 - c) if rev else (lambda c: c)
    has_add = add is not None

    def body(*refs):
        x_ref, b_ref, c_ref, dt_ref, cum_ref, cumt_ref = refs[:6]
        a_ref = refs[6] if has_add else None
        y_ref, st_ref, state = refs[-3:]

        @pl.when(pl.program_id(1) == 0)
        def _():
            state[...] = jnp.zeros_like(state)

        for q in (reversed(range(CPS)) if rev else range(CPS)):
            rows = _chunk_rows(q)
            chunk(x_ref.at[rows], b_ref.at[rows], c_ref.at[rows], dt_ref.at[rows], cum_ref.at[rows], cumt_ref.at[q],
                  a_ref.at[rows] if has_add else None, y_ref.at[rows], st_ref.at[q], state)

    def chunk(x_ref, b_ref, c_ref, dt_ref, cum_ref, cumt_ref, a_ref, y_ref, st_ref, state):
        bm = b_ref[...]
        cm = c_ref[...].astype(BF16)
        cb = _dot(cm, bm.astype(BF16), 1, 1)
        bt = bm.T.astype(BF16)
        low = _low_lanes()
        for p in range(PAIRS):
            ls = slice(p * LANE, (p + 1) * LANE)
            acum = _pair_lanes(cum_ref, p, low)
            u = x_ref[:, ls] * _pair_lanes(dt_ref, p, low)
            tot = acum[last:last + 1, :]
            m = [(cb * _decay_mats(cum_ref[:, r:r + 1], cumt_ref[r:r + 1, :], rev)).astype(BF16) for r in (2 * p, 2 * p + 1)]
            st = state[p]
            st_ref[p] = st
            yd = _dot(jnp.concatenate(m, axis=1), _block_diag(u.astype(BF16), low))
            yo = jnp.exp(acum) * _dot(cm, st.astype(BF16))
            y_ref[:, ls] = yd + yo + a_ref[:, ls] if has_add else yd + yo
            state[p] = jnp.exp(tot) * st + _dot(bt, (jnp.exp(tot - acum) * u).astype(BF16))

    yspec = pl.BlockSpec((TB, GW), lambda g, c: (ci(c), g))
    return pl.pallas_call(
        body, name=name, grid=(SSM_GROUPS, nb), in_specs=_scan_specs(rev, ci) + ([yspec] if has_add else []),
        out_specs=[yspec, pl.BlockSpec((CPS, PAIRS, SSM_STATE, LANE), lambda g, c: (ci(c), g, 0, 0))],
        out_shape=[jax.ShapeDtypeStruct((s, SSM_INNER), F32), jax.ShapeDtypeStruct((nc, SSM_HEADS // 2, SSM_STATE, LANE), F32)],
        scratch_shapes=[pltpu.VMEM((PAIRS, SSM_STATE, LANE), F32)],
        compiler_params=_cp("parallel", "arbitrary"),
    )(xbc, xbc, xbc, dtk, cumk, cumtk, *([add] if has_add else []))


def _ssd_scan_bwd(xbc, dtk, cumk, cumtk, dy, states, dvec, prev, rev, name, exch=()):
    s = xbc.shape[0]
    nc = s // T
    nb = nc // CPS
    last = 0 if rev else T - 1
    ci = (lambda c: c) if rev else (lambda c: nb - 1 - c)
    has_prev = prev is not None
    n_in = 12 if has_prev else 9
    n_ex = len(exch)

    def body(*refs):
        ins, ex_in, refs = refs[:n_in], refs[n_in:n_in + n_ex], refs[n_in + n_ex:]
        outs, ex_out, scratch = refs[:5], refs[5:5 + n_ex], refs[5 + n_ex:]
        if n_ex:
            scratch, sems = scratch[:4], scratch[4:]
            g, c = pl.program_id(0), pl.program_id(1)
            _ride_along(ex_in, ex_out, sems, (g == 0) & (c == 0), (g == SSM_GROUPS - 1) & (c == nb - 1))

        @pl.when(pl.program_id(1) == 0)
        def _():
            scratch[0][...] = jnp.zeros_like(scratch[0])

        for q in (range(CPS) if rev else reversed(range(CPS))):
            rows = _chunk_rows(q)
            cut = lambda t: t.at[rows]
            x_ref, b_ref, c_ref, dt_ref, cum_ref, cumt_ref, dy_ref, st_ref, dv_ref = ins[:9]
            sub = [cut(x_ref), cut(b_ref), cut(c_ref), cut(dt_ref), cut(cum_ref), cumt_ref.at[q], cut(dy_ref), st_ref.at[q], dv_ref]
            chunk(*sub, *[cut(t) for t in ins[9:]], *[cut(t) for t in outs], *scratch)

    def chunk(*refs):
        x_ref, b_ref, c_ref, dt_ref, cum_ref, cumt_ref, dy_ref, st_ref, dv_ref = refs[:9]
        refs = refs[9:]
        if has_prev:
            pdx, pdb, pdc = refs[:3]
            refs = refs[3:]
        dx_ref, db_ref, dc_ref, ddt_ref, dda_ref, dstate, rs_buf, in_buf, k_buf = refs
        rs_buf[...] = jnp.zeros_like(rs_buf)
        in_buf[...] = jnp.zeros_like(in_buf)
        k_buf[...] = jnp.zeros_like(k_buf)
        bm = b_ref[...].astype(BF16)
        cm = c_ref[...].astype(BF16)
        cbt = _dot(bm, cm, 1, 1)
        cb = _dot(cm, bm, 1, 1)
        ct = c_ref[...].T.astype(BF16)
        after = _tri(not rev)
        before = _tri(rev)
        from_k = before.astype(BF16)
        ri = lax.broadcasted_iota(jnp.int32, (T, T), 0)
        cj = lax.broadcasted_iota(jnp.int32, (T, T), 1)
        strictly_before = (cj > ri) if rev else (cj < ri)
        dcb = jnp.zeros((T, T), F32)
        dc_acc = jnp.zeros((T, SSM_STATE), F32)
        db_acc = jnp.zeros((T, SSM_STATE), F32)
        low = _low_lanes()
        ri2 = lax.broadcasted_iota(jnp.int32, (LANE, LANE), 0)
        cj2 = lax.broadcasted_iota(jnp.int32, (LANE, LANE), 1)
        halves = ((ri2 < SSM_HEAD_DIM) == (cj2 == 0)) & (cj2 < 2)
        halves = halves.astype(BF16)

        def head_sums(v):
            hi = v.astype(BF16)
            lo = (v - hi.astype(F32)).astype(BF16)
            return _dot(hi, halves) + _dot(lo, halves)

        for p in range(PAIRS):
            ls = slice(p * LANE, (p + 1) * LANE)
            c2 = slice(2 * p, 2 * p + 2)
            lm, lmt = [], []
            for r in (2 * p, 2 * p + 1):
                acol = cum_ref[:, r:r + 1]
                arow = cumt_ref[r:r + 1, :]
                lm.append(jnp.where(after, jnp.exp(jnp.where(after, acol - arow, 0.0)), 0.0))
                lmt.append(jnp.where(before, jnp.exp(jnp.where(before, arow - acol, 0.0)), 0.0))
            acum = _pair_lanes(cum_ref, p, low)
            tot = acum[last:last + 1, :]
            dtl = _pair_lanes(dt_ref, p, low)
            xl = x_ref[:, ls]
            u = xl * dtl
            ub = u.astype(BF16)
            dyl = dy_ref[:, ls]
            dyb = dyl.astype(BF16)
            st = st_ref[p]
            stb = st.astype(BF16)
            dst = dstate[p]
            dstb = dst.astype(BF16)
            dec = jnp.exp(tot - acum)
            eac = jnp.exp(acum)
            etot = jnp.exp(tot)
            du_off = dec * _dot(bm, dstb)
            mt = jnp.concatenate([(cbt * lmt[0]).astype(BF16), (cbt * lmt[1]).astype(BF16)], axis=1)
            dyd = _block_diag(dyb, low)
            du = _dot(mt, dyd) + du_off
            g2 = _dot(dyd, ub, 1, 1)
            gl = [g2[:T] * lm[0], g2[T:] * lm[1]]
            dcb = dcb + gl[0] + gl[1]
            dc_acc = dc_acc + _dot((eac * dyl).astype(BF16), stb, 1, 1)
            db_acc = db_acc + _dot((dec * u).astype(BF16), dstb, 1, 1)
            w = jnp.concatenate([(gl[0] * cb).astype(BF16), (gl[1] * cb).astype(BF16)], axis=1)
            crossing = _dot(from_k, w)
            for j in range(2):
                cr = jnp.where(strictly_before, crossing[:, j * T:(j + 1) * T], 0.0)
                in_buf[:, 2 * p + j:2 * p + j + 1] = jnp.sum(cr, axis=1, keepdims=True)
            y_off = eac * _dot(cm, stb)
            udu = u * du_off
            rs_buf[:, c2] = head_sums(dyl * y_off - udu)[:, 0:2]
            col = jnp.sum(dst * (etot * st) + udu, axis=0, keepdims=True)
            k_buf[0:1, c2] = head_sums(jnp.broadcast_to(col, (8, LANE)))[0:1, 0:2]
            ddt_ref[:, c2] = head_sums(du * xl)[:, 0:2]
            dx = du * dtl
            if has_prev:
                dx = dx + pdx[:, ls]
            else:
                dx = dx + dyl * dv_ref[:, ls]
            dx_ref[:, ls] = dx
            dstate[p] = etot * dst + _dot(ct, (eac * dyl).astype(BF16))
        dda = in_buf[...] + _dot_exact(from_k, rs_buf[...]) + k_buf[0:1, :]
        dda_ref[...] = dda[:, :SSM_REP]
        dcbb = dcb.astype(BF16)
        dc = dc_acc + _dot(dcbb, bm)
        db = db_acc + _dot(dcbb, cm, 0, 0)
        if has_prev:
            dc = dc + pdc[...]
            db = db + pdb[...]
        dc_ref[...] = dc
        db_ref[...] = db

    xspec = pl.BlockSpec((TB, GW), lambda g, c: (ci(c), g))
    gspec = pl.BlockSpec((TB, LANE), lambda g, c: (ci(c), g))
    in_specs = _scan_specs(rev, ci) + [
        xspec,
        pl.BlockSpec((CPS, PAIRS, SSM_STATE, LANE), lambda g, c: (ci(c), g, 0, 0)),
        pl.BlockSpec((1, GW), lambda g, c: (0, g)),
    ]
    args = [xbc, xbc, xbc, dtk, cumk, cumtk, dy, states, dvec]
    if has_prev:
        in_specs += [xspec, gspec, gspec]
        args += list(prev)
    ospec8 = pl.BlockSpec((None, TB, SSM_REP), lambda g, c: (g, ci(c), 0))
    ex_specs, ex_shapes, sems = _ride_along_specs(exch)
    outs = pl.pallas_call(
        body, name=name, grid=(SSM_GROUPS, nb), in_specs=in_specs + ex_specs,
        out_specs=[xspec, gspec, gspec, ospec8, ospec8] + ex_specs,
        out_shape=[jax.ShapeDtypeStruct((s, SSM_INNER), F32), jax.ShapeDtypeStruct((s, SSM_BC), F32), jax.ShapeDtypeStruct((s, SSM_BC), F32),
                   jax.ShapeDtypeStruct((SSM_GROUPS, s, SSM_REP), F32), jax.ShapeDtypeStruct((SSM_GROUPS, s, SSM_REP), F32)]
        + ex_shapes,
        scratch_shapes=[pltpu.VMEM((PAIRS, SSM_STATE, LANE), F32), pltpu.VMEM((T, LANE), F32), pltpu.VMEM((T, LANE), F32),
                        pltpu.VMEM((8, LANE), F32)] + sems,
        compiler_params=_cp("arbitrary" if n_ex else "parallel", "arbitrary"),
    )(*args, *exch)
    return (outs[:5], outs[5:]) if n_ex else outs


def _ssd_post(y, xbc, z, dvec, nw, name):
    s = z.shape[0]
    tm = min(256, s)

    def body(y_ref, x_ref, z_ref, dv_ref, nw_ref, o_ref):
        ys = y_ref[...] + dv_ref[...] * x_ref[...]
        yg = ys * _silu(z_ref[...])
        ms = jnp.mean(yg * yg, axis=1, keepdims=True)
        o_ref[...] = (yg * lax.rsqrt(ms + RMS_EPS) * nw_ref[...]).astype(BF16)

    rs = _row_spec(tm, SSM_INNER)
    vs = _vec_spec(SSM_INNER)
    return pl.pallas_call(
        body, name=name, grid=(s // tm,), in_specs=[rs, rs, rs, vs, vs], out_specs=rs,
        out_shape=jax.ShapeDtypeStruct((s, SSM_INNER), BF16), compiler_params=_cp("parallel"),
    )(y, xbc, z, dvec, nw)


def _ssd_post_bwd(dyn, y, xbc, z, dvec, nw, name):
    s = z.shape[0]
    tm = min(256, s)

    def body(dyn_ref, y_ref, x_ref, z_ref, dv_ref, nw_ref, dys_ref, dz_ref, red_ref):
        @pl.when(pl.program_id(0) == 0)
        def _():
            red_ref[...] = jnp.zeros_like(red_ref)

        xv, zv = x_ref[...], z_ref[...]
        ys = y_ref[...] + dv_ref[...] * xv
        sz = _silu(zv)
        yg = ys * sz
        rstd = lax.rsqrt(jnp.mean(yg * yg, axis=1, keepdims=True) + RMS_EPS)
        yhat = yg * rstd
        dynv = dyn_ref[...]
        dyh = dynv * nw_ref[...]
        dyg = rstd * (dyh - yhat * jnp.mean(dyh * yhat, axis=1, keepdims=True))
        dys = dyg * sz
        dys_ref[...] = dys
        dz_ref[...] = (dyg * ys * _dsilu(zv)).astype(BF16)
        red_ref[0:1, :] += jnp.sum(dynv * yhat, axis=0, keepdims=True)
        red_ref[1:2, :] += jnp.sum(dys * xv, axis=0, keepdims=True)

    rs = _row_spec(tm, SSM_INNER)
    vs = _vec_spec(SSM_INNER)
    return pl.pallas_call(
        body, name=name, grid=(s // tm,), in_specs=[rs, rs, rs, rs, vs, vs],
        out_specs=[rs, rs, _vec_spec(SSM_INNER, 8)],
        out_shape=[jax.ShapeDtypeStruct((s, SSM_INNER), F32), jax.ShapeDtypeStruct((s, SSM_INNER), BF16), jax.ShapeDtypeStruct((8, SSM_INNER), F32)],
        compiler_params=_cp("arbitrary"),
    )(dyn, y, xbc, z, dvec, nw)


def _dt_bwd(dt_raw, bias, a_log, dt, ddt, dda, name):
    s = dt_raw.shape[0]
    tm = min(1024, s)

    def body(r_ref, b_ref, a_ref, dt_ref, ddt_ref, dda_ref, o_ref, red_ref):
        @pl.when(pl.program_id(0) == 0)
        def _():
            red_ref[...] = jnp.zeros_like(red_ref)

        a = -jnp.exp(a_ref[...])
        ddav = dda_ref[...]
        draw = (ddt_ref[...] + a * ddav) * _sigmoid(r_ref[...] + b_ref[...])
        o_ref[...] = draw.astype(BF16)
        red_ref[0:1, :] += jnp.sum(draw, axis=0, keepdims=True)
        red_ref[1:2, :] += a * jnp.sum(ddav * dt_ref[...], axis=0, keepdims=True)

    rs = _row_spec(tm, LANE)
    vs = _vec_spec(LANE)
    return pl.pallas_call(
        body, name=name, grid=(s // tm,), in_specs=[rs, vs, vs, rs, rs, rs], out_specs=[rs, _vec_spec(LANE, 8)],
        out_shape=[jax.ShapeDtypeStruct((s, LANE), BF16), jax.ShapeDtypeStruct((8, LANE), F32)],
        compiler_params=_cp("arbitrary"),
    )(dt_raw, bias, a_log, dt, ddt, dda)


def _pad_lanes(v):
    v = v.reshape(1, -1)
    return jnp.pad(v, ((0, 0), (0, LANE - v.shape[1])))


def _ssd_prep_weights(w_in, conv_w, conv_b, dt_bias, a_log, d_skip, norm_w, w_out):
    return dict(
        w_z=w_in[:, :SSM_INNER].astype(BF16),
        w_xbc=w_in[:, SSM_INNER:SSM_INNER + SSM_CONV_DIM].astype(BF16),
        w_dt=jnp.pad(w_in[:, SSM_INNER + SSM_CONV_DIM:], ((0, 0), (0, LANE - 2 * SSM_HEADS))).astype(BF16),
        conv_w=conv_w, conv_b=conv_b.reshape(1, -1), bias=_pad_lanes(dt_bias), a_log=_pad_lanes(a_log),
        dvec=jnp.repeat(d_skip, SSM_HEAD_DIM).reshape(1, -1), nw=norm_w.reshape(1, -1), w_out=w_out.astype(BF16),
    )


def _ssd_layer_fwd(h, w, li):
    nm = lambda t: f"b{li}_{t}"
    z = _mm(h, w["w_z"], name=nm("z"))
    xraw = _mm(h, w["w_xbc"], name=nm("xbc"))
    dt_raw = _mm(h, w["w_dt"], name=nm("dt"))
    pre, xbc = _conv_fwd(xraw, w["conv_w"], w["conv_b"], nm("conv"))
    dt, cum, cumt = _dt_prep(dt_raw, w["bias"], w["a_log"], nm("dtprep"))
    nc = cumt.shape[0]
    dtk, cumk = _by_group(dt), _by_group(cum)
    cumtk = cumt[:, :2 * SSM_HEADS].reshape(nc, 2 * SSM_GROUPS, SSM_REP, T).transpose(1, 0, 2, 3)
    yf, stf = _ssd_scan(xbc, dtk, cumk, cumtk, False, nm("scan_f"))
    y, stb = _ssd_scan(xbc, dtk, cumk, cumtk, True, nm("scan_b"), add=yf)
    yn = _ssd_post(y, xbc, z, w["dvec"], w["nw"], nm("post"))
    out = _mm(yn, w["w_out"], name=nm("out"))
    return out, (z, xraw, dt_raw, pre, xbc, dt, dtk, cumk, cumtk, y, stf, stb, yn)


def _ssd_layer_bwd(dy, h, saved, w, li, exch=((), ())):
    nm = lambda t: f"b{li}_{t}"
    z, xraw, dt_raw, pre, xbc, dt, dtk, cumk, cumtk, y, stf, stb, yn = saved
    g_w_out = _mm(yn, dy, ta=True, out_dtype=BF16, name=nm("dwout"))
    dyn = _mm(dy, w["w_out"], tb=True, name=nm("dyn"))
    dys, dz, pred = _ssd_post_bwd(dyn, y, xbc, z, w["dvec"], w["nw"], nm("postbwd"))
    arrived = [(), ()]
    res = _ssd_scan_bwd(xbc, dtk, cumk, cumtk, dys, stf, w["dvec"], None, False, nm("scanbwd_f"), exch[0])
    if exch[0]:
        res, arrived[0] = res
    dx1, db1, dc1, ddt_f, dda_f = res
    res = _ssd_scan_bwd(xbc, dtk, cumk, cumtk, dys, stb, w["dvec"], (dx1, db1, dc1), True, nm("scanbwd_b"), exch[1])
    if exch[1]:
        res, arrived[1] = res
    dx, db, dc, ddt_b, dda_b = res
    dxraw, cred = _conv_bwd(dx, db, dc, pre, xraw, w["conv_w"], nm("convbwd"))
    draw, dred = _dt_bwd(dt_raw, w["bias"], w["a_log"], dt, _from_group(ddt_f, ddt_b), _from_group(dda_f, dda_b), nm("dtbwd"))
    dh = _mm(dz, w["w_z"], tb=True, name=nm("dh_z"))
    dh = _mm(dxraw, w["w_xbc"], tb=True, add=dh, name=nm("dh_xbc"))
    dh = _mm(draw, w["w_dt"], tb=True, add=dh, name=nm("dh_dt"))
    g_w_in = jnp.concatenate([_mm(h, dz, ta=True, out_dtype=BF16, name=nm("dwz")), _mm(h, dxraw, ta=True, out_dtype=BF16, name=nm("dwxbc")),
                              _mm(h, draw, ta=True, out_dtype=BF16, name=nm("dwdt"))[:, :2 * SSM_HEADS]], axis=1)
    grads = (g_w_in, cred[:SSM_CONV], cred[SSM_CONV], dred[0, :2 * SSM_HEADS].reshape(2, SSM_HEADS),
             dred[1, :2 * SSM_HEADS].reshape(2, SSM_HEADS), pred[1].reshape(SSM_HEADS, SSM_HEAD_DIM).sum(axis=1), pred[0], g_w_out)
    return dh, grads, arrived


B_GRAD_NAMES = ("b_w_in", "b_conv_w", "b_conv_b", "b_dt_bias", "b_a_log", "b_d", "b_norm_w", "b_w_out")


def _shards_of(big, j):
    return [_shard_cols(big["a_w_in"][j]), big["a_w_out"][j].reshape(N_SHARD, -1, big["a_w_out"][j].shape[-1]),
            _shard_cols(big["b_w_in"][j]), big["b_w_out"][j].reshape(N_SHARD, -1, big["b_w_out"][j].shape[-1])]


def _local_step(x, tgt, mod, w, early=False):
    d = x.shape[1]
    qkv_cols = QKV_COLS
    layers = []
    for i in range(DEPTH):
        j = i // 2
        if i % 2 == 0:
            layers.append((w["a_w_in"][j][:, :qkv_cols].astype(BF16), w["a_w_in"][j][:, qkv_cols:].astype(BF16), w["a_w_out"][j].astype(BF16)))
        else:
            layers.append(_ssd_prep_weights(w["b_w_in"][j], w["b_conv_w"][j], w["b_conv_b"][j], w["b_dt_bias"][j], w["b_a_log"][j],
                                            w["b_d"][j], w["b_norm_w"][j], w["b_w_out"][j]))
    saved = []
    for i in range(DEPTH):
        shift, scale, gate = mod[i:i + 1, :d], mod[i:i + 1, d:2 * d], mod[i:i + 1, 2 * d:]
        h = _modulate(x, scale, shift, f"l{i}_mod")
        if i % 2 == 0:
            out, sv = _attn_layer_fwd(h, *layers[i], i)
        else:
            out, sv = _ssd_layer_fwd(h, layers[i], i)
        xn = _resid_ln(x, out, gate, w["ln_g"][i:i + 1], w["ln_b"][i:i + 1], f"l{i}_ln")
        saved.append((x, h, out, sv))
        x = xn
    dx, lred = _loss_grad(x, tgt, "loss")
    loss = 0.5 * jnp.sum(lred[0]) / d
    dmod, g_ln_g, g_ln_b = [None] * DEPTH, [None] * DEPTH, [None] * DEPTH
    ga_in, ga_out = [None, None], [None, None]
    gb = [None, None]
    arrived = None
    for i in reversed(range(DEPTH)):
        j = i // 2
        xi, h, out, sv = saved[i]
        scale, gate = mod[i:i + 1, d:2 * d], mod[i:i + 1, 2 * d:]
        du, dy, red = _resid_ln_bwd(xi, out, dx, gate, w["ln_g"][i:i + 1], f"l{i}_lnbwd")
        g_ln_g[i], g_ln_b[i] = red[1], red[2]
        if i % 2 == 0:
            exch = ()
            if early and j == 0:
                exch = [_shard_cols(gb[0][0]), gb[0][7].reshape(N_SHARD, -1, gb[0][7].shape[-1])]
            dhs, ga_in[j], ga_out[j], got = _attn_layer_bwd(dy, h, sv, *layers[i], i, exch)
            if early and j == 0:
                arrived = (list(got), arrived)
        else:
            exch = ((), ())
            if early and j == 0:
                sh = _shards_of(dict(a_w_in=ga_in, a_w_out=ga_out, b_w_in=[None, gb[1][0]], b_w_out=[None, gb[1][7]]), 1)
                exch = (sh[:2], sh[2:])
            dh, gb[j], got = _ssd_layer_bwd(dy, h, sv, layers[i], i, exch)
            if early and j == 0:
                arrived = list(got[0]) + list(got[1])
            dhs = [dh]
        dx, red2 = _modulate_bwd(du, dhs, xi, scale, f"l{i}_modbwd")
        dmod[i] = jnp.concatenate([red2[1], red2[0], red[0]])
    grads = {"ln_g": jnp.stack(g_ln_g), "ln_b": jnp.stack(g_ln_b), "a_w_in": jnp.stack(ga_in), "a_w_out": jnp.stack(ga_out)}
    for k, n in enumerate(B_GRAD_NAMES):
        grads[n] = jnp.stack([gb[0][k], gb[1][k]])
    big = dict(a_w_in=ga_in, a_w_out=ga_out, b_w_in=[gb[0][0], gb[1][0]], b_w_out=[gb[0][7], gb[1][7]])
    return loss, dx, jnp.stack(dmod), grads, big, arrived


MESH = pl.DeviceIdType.MESH
ANY = pl.BlockSpec(memory_space=pl.ANY)
N_DEV = 8
N_SHARD = 4


def _flip(v, bit):
    return 1 - v if bit else v


def _all_gather8(v, name):
    def body(v_ref, o_ref, send_sems, recv_sems, local_sem):
        x, y, c = lax.axis_index("x"), lax.axis_index("y"), lax.axis_index("c")
        me = 4 * x + 2 * y + c
        local = pltpu.make_async_copy(v_ref, o_ref.at[me], local_sem)
        local.start()
        copies = []
        for k in range(1, N_DEV):
            peer = (_flip(x, k & 4), _flip(y, k & 2), _flip(c, k & 1))
            copies.append(pltpu.make_async_remote_copy(
                src_ref=v_ref, dst_ref=o_ref.at[me], send_sem=send_sems.at[k - 1], recv_sem=recv_sems.at[k - 1],
                device_id=peer, device_id_type=MESH))
        for cp in copies:
            cp.start()
        for cp in copies:
            cp.wait()
        local.wait()

    return pl.pallas_call(
        body, name=name, in_specs=[ANY], out_specs=ANY, out_shape=jax.ShapeDtypeStruct((N_DEV,) + v.shape, v.dtype),
        scratch_shapes=[pltpu.SemaphoreType.DMA((N_DEV - 1,)), pltpu.SemaphoreType.DMA((N_DEV - 1,)), pltpu.SemaphoreType.DMA],
    )(v)


def _shard_exchange(s_refs, o_refs, send_sems, recv_sems, local_sems):
    n = len(s_refs)
    x, y, c = lax.axis_index("x"), lax.axis_index("y"), lax.axis_index("c")
    m = 2 * x + y
    copies = [pltpu.make_async_copy(s_refs[a].at[m], o_refs[a].at[m], local_sems.at[a]) for a in range(n)]
    for k in range(1, N_SHARD):
        px, py = _flip(x, k & 2), _flip(y, k & 1)
        for a in range(n):
            i = (k - 1) * n + a
            copies.append(pltpu.make_async_remote_copy(
                src_ref=s_refs[a].at[2 * px + py], dst_ref=o_refs[a].at[m], send_sem=send_sems.at[i], recv_sem=recv_sems.at[i],
                device_id=(px, py, c), device_id_type=MESH))
    return copies


def _transpose_shards(srcs, name):
    n = len(srcs)
    n_rem = (N_SHARD - 1) * n

    def body(*refs):
        copies = _shard_exchange(refs[:n], refs[n:2 * n], *refs[2 * n:])
        for cp in copies:
            cp.start()
        for cp in copies:
            cp.wait()

    return pl.pallas_call(
        body, name=name, in_specs=[ANY] * n, out_specs=[ANY] * n, out_shape=[jax.ShapeDtypeStruct(s.shape, s.dtype) for s in srcs],
        scratch_shapes=[pltpu.SemaphoreType.DMA((n_rem,)), pltpu.SemaphoreType.DMA((n_rem,)), pltpu.SemaphoreType.DMA((n,))],
    )(*srcs)


def _gather_shards(src, name):
    rows = src.shape[0]
    half = rows // 2
    n_ici = N_SHARD - 1

    def body(s_ref, o_ref, send_sems, recv_sems, local_sem):
        x, y, c = lax.axis_index("x"), lax.axis_index("y"), lax.axis_index("c")
        m = 2 * x + y
        sibling = (x, y, 1 - c)
        my_half = pl.ds(pl.multiple_of(c * half, 16), half)
        its_half = pl.ds(pl.multiple_of((1 - c) * half, 16), half)
        local = pltpu.make_async_copy(s_ref, o_ref.at[m], local_sem)
        local.start()
        chips = [(_flip(x, k & 2), _flip(y, k & 1)) for k in range(1, N_SHARD)]

        def copy(sem, src_ref, dst_ref, to):
            return pltpu.make_async_remote_copy(src_ref=src_ref, dst_ref=dst_ref, send_sem=send_sems.at[sem],
                                                recv_sem=recv_sems.at[sem], device_id=to, device_id_type=MESH)

        first = [copy(i, s_ref.at[my_half], o_ref.at[m, my_half], (px, py, c)) for i, (px, py) in enumerate(chips)]
        for cp in first:
            cp.start()
        passed = []
        for i, (px, py) in enumerate(chips):
            landed = o_ref.at[2 * px + py, my_half]
            copy(i, landed, landed, (px, py, c)).wait_recv()
            passed.append(copy(n_ici + i, landed, landed, sibling))
            passed[-1].start()
        for i, (px, py) in enumerate(chips):
            from_sibling = o_ref.at[2 * px + py, its_half]
            copy(n_ici + i, from_sibling, from_sibling, sibling).wait_recv()
        for cp in first + passed:
            cp.wait_send()
        local.wait()

    return pl.pallas_call(
        body, name=name, in_specs=[ANY], out_specs=ANY, out_shape=jax.ShapeDtypeStruct((N_SHARD,) + src.shape, src.dtype),
        scratch_shapes=[pltpu.SemaphoreType.DMA((2 * n_ici,)), pltpu.SemaphoreType.DMA((2 * n_ici,)), pltpu.SemaphoreType.DMA],
    )(src)


def _swap_sibling(vs, name):
    n = len(vs)

    def body(*refs):
        v_refs, o_refs, (send_sems, recv_sems) = refs[:n], refs[n:2 * n], refs[2 * n:]
        x, y, c = lax.axis_index("x"), lax.axis_index("y"), lax.axis_index("c")
        copies = [pltpu.make_async_remote_copy(src_ref=v_refs[a], dst_ref=o_refs[a], send_sem=send_sems.at[a], recv_sem=recv_sems.at[a],
                                               device_id=(x, y, 1 - c), device_id_type=MESH) for a in range(n)]
        for cp in copies:
            cp.start()
        for cp in copies:
            cp.wait()

    return pl.pallas_call(
        body, name=name, in_specs=[ANY] * n, out_specs=[ANY] * n, out_shape=[jax.ShapeDtypeStruct(v.shape, v.dtype) for v in vs],
        scratch_shapes=[pltpu.SemaphoreType.DMA((n,)), pltpu.SemaphoreType.DMA((n,))],
    )(*vs)


def _row_tile(r, elems, step):
    ok = [t for t in range(step, r + 1, step) if r % t == 0 and t <= elems]
    return max(ok) if ok else r


def _sum_slots(a, name):
    n, r, cdim = a.shape
    tm = _row_tile(r, (4 << 20) // (cdim * 4 * (n + 1)), 16)

    def body(a_ref, o_ref):
        acc = a_ref[0].astype(F32)
        for k in range(1, n):
            acc = acc + a_ref[k].astype(F32)
        o_ref[...] = acc

    return pl.pallas_call(
        body, name=name, grid=(r // tm,), in_specs=[pl.BlockSpec((n, tm, cdim), lambda i: (0, i, 0))],
        out_specs=pl.BlockSpec((tm, cdim), lambda i: (i, 0)), out_shape=jax.ShapeDtypeStruct((r, cdim), F32),
        compiler_params=_cp("parallel"),
    )(a)


def _silu_rows(v, name):
    def body(v_ref, o_ref):
        o_ref[...] = _silu(v_ref[...])

    return pl.pallas_call(body, name=name, out_shape=jax.ShapeDtypeStruct(v.shape, F32))(v)


PACK_COLS = 1024


def _adamw(w, gs, m, v, name):
    r, cdim = w.shape
    tm = _row_tile(r, (1 << 18) // cdim, 8)
    c1 = 1.0 / (1.0 - ADAM_B1 ** ADAM_STEP)
    c2 = 1.0 / (1.0 - ADAM_B2 ** ADAM_STEP)
    ng = len(gs)

    def body(*refs):
        w_ref, g_refs, (m_ref, v_ref, g_ref, d_ref, nm_ref, nv_ref) = refs[0], refs[1:1 + ng], refs[1 + ng:]
        g = g_refs[0][...]
        for t in g_refs[1:]:
            g = g + t[...]
        mn = ADAM_B1 * m_ref[...] + (1.0 - ADAM_B1) * g
        vn = ADAM_B2 * v_ref[...] + (1.0 - ADAM_B2) * (g * g)
        g_ref[...] = g
        nm_ref[...] = mn
        nv_ref[...] = vn
        d_ref[...] = -ADAM_LR * ((mn * c1) / (jnp.sqrt(vn * c2) + ADAM_EPS) + ADAM_WD * w_ref[...])

    spec = pl.BlockSpec((tm, cdim), lambda i: (i, 0))
    return pl.pallas_call(
        body, name=name, grid=(r // tm,), in_specs=[spec] * (3 + ng), out_specs=[spec] * 4,
        out_shape=[jax.ShapeDtypeStruct(w.shape, F32)] * 4, compiler_params=_cp("parallel"),
    )(w, *gs, m, v)


def _rows(a):
    f = a.reshape(-1)
    pad = (-f.shape[0]) % PACK_COLS
    if pad:
        f = jnp.pad(f, (0, pad))
    return f.reshape(-1, PACK_COLS)


def _nrows(shape):
    return -(-int(np.prod(shape)) // PACK_COLS)


def _pack(parts, total_rows=None):
    p = jnp.concatenate([_rows(a) for a in parts], axis=0)
    if total_rows is not None and total_rows > p.shape[0]:
        p = jnp.pad(p, ((0, total_rows - p.shape[0]), (0, 0)))
    return p


def _unpack(p, shapes):
    out, r0 = [], 0
    for shp in shapes:
        n = int(np.prod(shp))
        nr = _nrows(shp)
        out.append(p[r0:r0 + nr].reshape(-1)[:n].reshape(shp))
        r0 += nr
    return out


def _unshard_cols(g):
    return jnp.concatenate([g[k] for k in range(N_SHARD)], axis=-1)


def _shard_cols(a):
    n = a.shape[-1] // N_SHARD
    return jnp.stack([a[..., k * n:(k + 1) * n] for k in range(N_SHARD)])


def _unshard_rows(g):
    return jnp.concatenate([g[k] for k in range(N_SHARD)], axis=1)


def _shard_rows(a):
    n = a.shape[1] // N_SHARD
    return jnp.stack([a[:, k * n:(k + 1) * n] for k in range(N_SHARD)])


W_NAMES = ("ada_w", "ada_b", "ln_g", "ln_b", "a_w_in", "a_w_out", "b_w_in", "b_conv_w", "b_conv_b", "b_dt_bias", "b_a_log", "b_d",
           "b_norm_w", "b_w_out")
BIG = ("a_w_in", "a_w_out", "b_w_in", "b_w_out")
SMALL = ("ada_b", "ln_g", "ln_b", "b_conv_w", "b_conv_b", "b_dt_bias", "b_a_log", "b_d", "b_norm_w")


def kernel(x, c, ada_w, ada_b, ln_g, ln_b, a_w_in, a_w_out, b_w_in, b_conv_w, b_conv_b, b_dt_bias, b_a_log, b_d, b_norm_w, b_w_out, loss_target, m_ada_w, m_ada_b, m_ln_g, m_ln_b, m_a_w_in, m_a_w_out, m_b_w_in, m_b_conv_w, m_b_conv_b, m_b_dt_bias, m_b_a_log, m_b_d, m_b_norm_w, m_b_w_out, v_ada_w, v_ada_b, v_ln_g, v_ln_b, v_a_w_in, v_a_w_out, v_b_w_in, v_b_conv_w, v_b_conv_b, v_b_dt_bias, v_b_a_log, v_b_d, v_b_norm_w, v_b_w_out):
    w = dict(ada_w=ada_w, ada_b=ada_b, ln_g=ln_g, ln_b=ln_b, a_w_in=a_w_in, a_w_out=a_w_out, b_w_in=b_w_in, b_conv_w=b_conv_w,
             b_conv_b=b_conv_b, b_dt_bias=b_dt_bias, b_a_log=b_a_log, b_d=b_d, b_norm_w=b_norm_w, b_w_out=b_w_out)
    mom = dict(ada_w=m_ada_w, ada_b=m_ada_b, ln_g=m_ln_g, ln_b=m_ln_b, a_w_in=m_a_w_in, a_w_out=m_a_w_out, b_w_in=m_b_w_in,
               b_conv_w=m_b_conv_w, b_conv_b=m_b_conv_b, b_dt_bias=m_b_dt_bias, b_a_log=m_b_a_log, b_d=m_b_d, b_norm_w=m_b_norm_w,
               b_w_out=m_b_w_out)
    var = dict(ada_w=v_ada_w, ada_b=v_ada_b, ln_g=v_ln_g, ln_b=v_ln_b, a_w_in=v_a_w_in, a_w_out=v_a_w_out, b_w_in=v_b_w_in,
               b_conv_w=v_b_conv_w, b_conv_b=v_b_conv_b, b_dt_bias=v_b_dt_bias, b_a_log=v_b_a_log, b_d=v_b_d, b_norm_w=v_b_norm_w,
               b_w_out=v_b_w_out)
    ax, ay, ac = lax.axis_index("x"), lax.axis_index("y"), lax.axis_index("c")
    me = 4 * ax + 2 * ay + ac
    shard = 2 * ax + ay
    d = x.shape[-1]
    dsh = ada_w.shape[-1]

    small_in = (c, b_conv_w, b_conv_b, b_norm_w)
    g0 = _all_gather8(_pack(small_in).reshape(-1, LANE), "gather_small_in").reshape(N_DEV, -1, PACK_COLS)
    per_dev = [_unpack(g0[k], [a.shape for a in small_in]) for k in range(N_DEV)]
    c_all = jnp.concatenate([p[0] for p in per_dev], axis=0)
    conv_w_full, conv_b_full, norm_w_full = (_unshard_cols([per_dev[2 * k][t] for k in range(N_SHARD)]) for t in (1, 2, 3))

    cond = _silu_rows(jnp.pad(c_all, ((0, 8), (0, 0))), "cond")
    bias = lax.dynamic_slice_in_dim(ada_b, shard * dsh, dsh, axis=1)
    part = jnp.stack([_mm(cond, ada_w[i], add=jnp.broadcast_to(bias[i], (16, dsh)), name=f"mod{i}")[:N_DEV] for i in range(DEPTH)])
    g1 = _all_gather8(part.reshape(-1, LANE), "gather_mod").reshape(N_DEV, DEPTH, N_DEV, dsh)
    mod_all = _unshard_cols([g1[2 * k] for k in range(N_SHARD)])
    mod = lax.dynamic_index_in_dim(mod_all, me, axis=1, keepdims=False)

    gw = _gather_shards(_pack([w[n] for n in BIG]).astype(BF16), "gather_weights")
    big_sh = [_unpack(gw[k], [w[n].shape for n in BIG]) for k in range(N_SHARD)]
    full = dict(
        ln_g=ln_g, ln_b=ln_b, b_dt_bias=b_dt_bias, b_a_log=b_a_log, b_d=b_d,
        b_conv_w=conv_w_full, b_conv_b=conv_b_full, b_norm_w=norm_w_full,
        a_w_in=_unshard_cols([s[0] for s in big_sh]), a_w_out=_unshard_rows([s[1] for s in big_sh]),
        b_w_in=_unshard_cols([s[2] for s in big_sh]), b_w_out=_unshard_rows([s[3] for s in big_sh]),
    )

    loss, grad_x, dmod, g, big, (arrived_b0, arrived1) = _local_step(x[0], loss_target[0], mod, full, early=True)

    arrived0 = list(_transpose_shards(_shards_of(big, 0)[:2], "scatter_grads")) + arrived_b0
    mine = [jnp.concatenate([_sum_slots(arrived0[a], f"sum0_{n}"), _sum_slots(arrived1[a], f"sum1_{n}")], axis=0)
            for a, n in enumerate(BIG)]
    theirs = _swap_sibling(mine, "swap_grads")

    small_g = (dmod, g["ln_g"], g["ln_b"], g["b_dt_bias"], g["b_a_log"], g["b_d"], g["b_conv_w"], g["b_conv_b"], g["b_norm_w"],
               loss.reshape(1))
    g2 = _all_gather8(_pack(small_g).reshape(-1, LANE), "gather_small_grads")
    tot = _unpack(_sum_slots(g2, "sum_small").reshape(-1, PACK_COLS), [a.shape for a in small_g])
    g_ada_b, g_ln_g, g_ln_b, g_dt_bias, g_a_log, g_d, g_conv_w, g_conv_b, g_norm_w, loss_sum = tot
    dmod_all = g2.reshape(N_DEV, -1)[:, :dmod.size].reshape(N_DEV, DEPTH, 3 * d)
    dmod_mine = lax.dynamic_slice_in_dim(dmod_all, shard * dsh, dsh, axis=2)
    g_ada_w = jnp.stack([_mm(cond, jnp.pad(dmod_mine[:, i], ((0, 8), (0, 0))), ta=True, name=f"dada{i}") for i in range(DEPTH)])
    csh = g_conv_w.shape[-1] // N_SHARD
    nsh = g_norm_w.shape[-1] // N_SHARD
    small_grads = dict(
        ada_w=g_ada_w, ada_b=g_ada_b, ln_g=g_ln_g, ln_b=g_ln_b, b_dt_bias=g_dt_bias, b_a_log=g_a_log, b_d=g_d,
        b_conv_w=lax.dynamic_slice_in_dim(g_conv_w, shard * csh, csh, axis=2),
        b_conv_b=lax.dynamic_slice_in_dim(g_conv_b, shard * csh, csh, axis=1),
        b_norm_w=lax.dynamic_slice_in_dim(g_norm_w, shard * nsh, nsh, axis=1),
    )

    by_name = [{}, {}, {}, {}]

    def update(n, gs):
        two_d = lambda t: t.reshape(-1, t.shape[-1])
        outs = _adamw(two_d(w[n]), [two_d(t) for t in gs], two_d(mom[n]), two_d(var[n]), f"adamw_{n}")
        for t, o in zip(by_name, outs):
            t[n] = o.reshape(w[n].shape)

    for i, n in enumerate(BIG):
        update(n, [mine[i], theirs[i]])
    update("ada_w", [small_grads["ada_w"]])
    rest = SMALL
    rows = -(-sum(_nrows(w[n].shape) for n in rest) // 8) * 8
    packed = _adamw(_pack([w[n] for n in rest], rows), [_pack([small_grads[n] for n in rest], rows)],
                    _pack([mom[n] for n in rest], rows), _pack([var[n] for n in rest], rows), "adamw_small")
    for t, p in zip(by_name, packed):
        t.update(zip(rest, _unpack(p, [w[n].shape for n in rest])))
    return (loss_sum.reshape(()), grad_x[None], *[t[n] for t in by_name for n in W_NAMES])
```

```python
import jax
import jax.numpy as jnp
import numpy as np
from jax import lax
from jax.experimental import pallas as pl
from jax.experimental.pallas import tpu as pltpu

F32 = jnp.float32
BF16 = jnp.bfloat16

DEPTH = 4
A_HEADS = 16
A_HEAD_DIM = 64
A_WIDTH = A_HEADS * A_HEAD_DIM
DILATIONS = (1, 4, 16)
A_RADIUS = 64
A_QBLOCK = 128
SSM_HEADS = 32
SSM_HEAD_DIM = 64
SSM_STATE = 128
SSM_GROUPS = 4
SSM_REP = SSM_HEADS // SSM_GROUPS
SSM_CONV = 5
SSM_CHUNK = 128
DEEPNORM_ALPHA = (2 * DEPTH) ** 0.25
LN_EPS = 1e-5
RMS_EPS = 1e-5
ADAM_LR, ADAM_B1, ADAM_B2, ADAM_EPS, ADAM_WD, ADAM_STEP = 0.001, 0.9, 0.999, 1e-08, 0.01, 10
VMEM_LIMIT = 56 * 1024 * 1024
LANE = 128


def _cp(*sem):
    return pltpu.CompilerParams(dimension_semantics=sem, vmem_limit_bytes=VMEM_LIMIT)


def _tile(dim, target):
    if dim <= target:
        return dim
    t = (target // LANE) * LANE
    while dim % t:
        t -= LANE
    return t


def _sigmoid(x):
    return 1.0 / (1.0 + jnp.exp(-x))


def _silu(x):
    return x * _sigmoid(x)


def _dsilu(x):
    s = _sigmoid(x)
    return s * (1.0 + x * (1.0 - s))


def _split3(x):
    a = x.astype(BF16)
    r = x - a.astype(F32)
    b = r.astype(BF16)
    c = (r - b.astype(F32)).astype(BF16)
    return a, b, c


def _dot(a, b, ca=1, cb=0):
    return lax.dot_general(a, b, (((ca,), (cb,)), ((), ())), preferred_element_type=F32)


def _dot_exact(m01, x):
    a, b, c = _split3(x)
    return _dot(m01, a) + _dot(m01, b) + _dot(m01, c)


def _mm(a, b, *, ta=False, tb=False, add=None, out_dtype=F32, name, tm=1024, tn=1024, tk=2048):
    m, k = (a.shape[1], a.shape[0]) if ta else a.shape
    n = b.shape[0] if tb else b.shape[1]
    assert (b.shape[1] if tb else b.shape[0]) == k
    tm, tn, tk = _tile(m, tm), _tile(n, tn), _tile(k, tk)
    nk = k // tk
    has_add = add is not None

    def body(*refs):
        if has_add:
            a_ref, b_ref, c_ref, o_ref, acc = refs
        else:
            a_ref, b_ref, o_ref, acc = refs
        kk = pl.program_id(2)
        part = _dot(a_ref[...].astype(BF16), b_ref[...].astype(BF16), 0 if ta else 1, 1 if tb else 0)

        def finish(r):
            if has_add:
                r = r + c_ref[...]
            o_ref[...] = r.astype(o_ref.dtype)

        if nk == 1:
            finish(part)
            return

        @pl.when(kk == 0)
        def _():
            acc[...] = part

        @pl.when((kk > 0) & (kk < nk - 1))
        def _():
            acc[...] += part

        @pl.when(kk == nk - 1)
        def _():
            finish(acc[...] + part)

    a_spec = pl.BlockSpec((tk, tm), lambda i, j, kk: (kk, i)) if ta else pl.BlockSpec((tm, tk), lambda i, j, kk: (i, kk))
    b_spec = pl.BlockSpec((tn, tk), lambda i, j, kk: (j, kk)) if tb else pl.BlockSpec((tk, tn), lambda i, j, kk: (kk, j))
    in_specs = [a_spec, b_spec]
    args = [a, b]
    if has_add:
        in_specs.append(pl.BlockSpec((tm, tn), lambda i, j, kk: (i, j)))
        args.append(add)
    return pl.pallas_call(
        body, name=name, grid=(m // tm, n // tn, nk), in_specs=in_specs,
        out_specs=pl.BlockSpec((tm, tn), lambda i, j, kk: (i, j)),
        out_shape=jax.ShapeDtypeStruct((m, n), out_dtype),
        scratch_shapes=[pltpu.VMEM((tm, tn) if nk > 1 else (8, LANE), F32)],
        compiler_params=_cp("parallel", "parallel", "arbitrary"),
    )(*args)


ROWS = 512


def _row_spec(tm, d):
    return pl.BlockSpec((tm, d), lambda i: (i, 0))


def _vec_spec(d, rows=1):
    return pl.BlockSpec((rows, d), lambda i: (0, 0))


def _modulate(x, scale, shift, name):
    s, d = x.shape
    tm = min(ROWS, s)

    def body(x_ref, sc_ref, sh_ref, o_ref):
        o_ref[...] = (x_ref[...] * (1.0 + sc_ref[...]) + sh_ref[...]).astype(BF16)

    return pl.pallas_call(
        body, name=name, grid=(s // tm,), in_specs=[_row_spec(tm, d), _vec_spec(d), _vec_spec(d)],
        out_specs=_row_spec(tm, d), out_shape=jax.ShapeDtypeStruct((s, d), BF16), compiler_params=_cp("parallel"),
    )(x, scale, shift)


def _resid_ln(x, y, gate, g, b, name):
    s, d = x.shape
    tm = min(ROWS, s)

    def body(x_ref, y_ref, gt_ref, g_ref, b_ref, o_ref):
        u = DEEPNORM_ALPHA * x_ref[...] + gt_ref[...] * y_ref[...]
        mu = jnp.mean(u, axis=1, keepdims=True)
        uc = u - mu
        var = jnp.mean(uc * uc, axis=1, keepdims=True)
        o_ref[...] = uc * lax.rsqrt(var + LN_EPS) * g_ref[...] + b_ref[...]

    return pl.pallas_call(
        body, name=name, grid=(s // tm,),
        in_specs=[_row_spec(tm, d), _row_spec(tm, d), _vec_spec(d), _vec_spec(d), _vec_spec(d)],
        out_specs=_row_spec(tm, d), out_shape=jax.ShapeDtypeStruct((s, d), F32), compiler_params=_cp("parallel"),
    )(x, y, gate, g, b)


def _resid_ln_bwd(x, y, dxn, gate, g, name):
    s, d = x.shape
    tm = min(ROWS, s)

    def body(x_ref, y_ref, dxn_ref, gt_ref, g_ref, du_ref, dy_ref, red_ref):
        @pl.when(pl.program_id(0) == 0)
        def _():
            red_ref[...] = jnp.zeros_like(red_ref)

        yv = y_ref[...]
        u = DEEPNORM_ALPHA * x_ref[...] + gt_ref[...] * yv
        mu = jnp.mean(u, axis=1, keepdims=True)
        uc = u - mu
        var = jnp.mean(uc * uc, axis=1, keepdims=True)
        rstd = lax.rsqrt(var + LN_EPS)
        xhat = uc * rstd
        dxnv = dxn_ref[...]
        dxh = dxnv * g_ref[...]
        du = rstd * (dxh - jnp.mean(dxh, axis=1, keepdims=True) - xhat * jnp.mean(dxh * xhat, axis=1, keepdims=True))
        du_ref[...] = du
        dy_ref[...] = (du * gt_ref[...]).astype(BF16)
        red_ref[0:1, :] += jnp.sum(du * yv, axis=0, keepdims=True)
        red_ref[1:2, :] += jnp.sum(dxnv * xhat, axis=0, keepdims=True)
        red_ref[2:3, :] += jnp.sum(dxnv, axis=0, keepdims=True)

    return pl.pallas_call(
        body, name=name, grid=(s // tm,),
        in_specs=[_row_spec(tm, d), _row_spec(tm, d), _row_spec(tm, d), _vec_spec(d), _vec_spec(d)],
        out_specs=[_row_spec(tm, d), _row_spec(tm, d), _vec_spec(d, 8)],
        out_shape=[jax.ShapeDtypeStruct((s, d), F32), jax.ShapeDtypeStruct((s, d), BF16), jax.ShapeDtypeStruct((8, d), F32)],
        compiler_params=_cp("arbitrary"),
    )(x, y, dxn, gate, g)


def _modulate_bwd(du, dhs, x, scale, name):
    s, d = x.shape
    tm = min(ROWS, s)
    n = len(dhs)

    def body(*refs):
        du_ref, dh_refs, (x_ref, sc_ref, dx_ref, red_ref) = refs[0], refs[1:1 + n], refs[1 + n:]

        @pl.when(pl.program_id(0) == 0)
        def _():
            red_ref[...] = jnp.zeros_like(red_ref)

        dhv = dh_refs[0][...]
        for t in dh_refs[1:]:
            dhv = dhv + t[...]
        dx_ref[...] = DEEPNORM_ALPHA * du_ref[...] + dhv * (1.0 + sc_ref[...])
        red_ref[0:1, :] += jnp.sum(dhv * x_ref[...], axis=0, keepdims=True)
        red_ref[1:2, :] += jnp.sum(dhv, axis=0, keepdims=True)

    return pl.pallas_call(
        body, name=name, grid=(s // tm,),
        in_specs=[_row_spec(tm, d)] * (n + 2) + [_vec_spec(d)],
        out_specs=[_row_spec(tm, d), _vec_spec(d, 8)],
        out_shape=[jax.ShapeDtypeStruct((s, d), F32), jax.ShapeDtypeStruct((8, d), F32)],
        compiler_params=_cp("arbitrary"),
    )(du, *dhs, x, scale)


def _loss_grad(xf, tgt, name):
    s, d = xf.shape
    tm = min(ROWS, s)

    def body(x_ref, t_ref, dx_ref, red_ref):
        @pl.when(pl.program_id(0) == 0)
        def _():
            red_ref[...] = jnp.zeros_like(red_ref)

        e = x_ref[...] - t_ref[...]
        dx_ref[...] = e * (1.0 / d)
        red_ref[0:1, :] += jnp.sum(e * e, axis=0, keepdims=True)

    return pl.pallas_call(
        body, name=name, grid=(s // tm,), in_specs=[_row_spec(tm, d), _row_spec(tm, d)],
        out_specs=[_row_spec(tm, d), _vec_spec(d, 8)],
        out_shape=[jax.ShapeDtypeStruct((s, d), F32), jax.ShapeDtypeStruct((8, d), F32)],
        compiler_params=_cp("arbitrary"),
    )(xf, tgt)


QKV_COLS = 3 * 3 * A_WIDTH


SLOPES = tuple(float(2.0 ** (-8.0 * (h + 1.0) / A_HEADS)) for h in range(A_HEADS))
FAR = 1e30
HEAD_COLS = tuple(slice(h * A_HEAD_DIM, (h + 1) * A_HEAD_DIM) for h in range(A_HEADS))


def _band_dist(n, length, dil, span_rows):
    shape = (2 * A_QBLOCK, A_QBLOCK) if span_rows else (A_QBLOCK, 2 * A_QBLOCK)
    r = lax.broadcasted_iota(jnp.int32, shape, 0)
    c = lax.broadcasted_iota(jnp.int32, shape, 1)
    sp, ce = (r, c) if span_rows else (c, r)
    delta = sp - A_RADIUS - ce
    pos = n * A_QBLOCK - A_RADIUS + sp
    valid = (jnp.abs(delta) <= A_RADIUS) & (pos >= 0) & (pos < length)
    return jnp.where(valid, jnp.abs(delta).astype(F32) * float(dil), FAR)


def _span_specs(col, nb64):
    def mk(i):
        return pl.BlockSpec((64, A_WIDTH), lambda r, n: (r * nb64 + jnp.clip(2 * n - 1 + i, 0, nb64 - 1), col))
    return [mk(i) for i in range(4)]


def _to_residue(t, dil):
    if dil == 1:
        return t
    s, c = t.shape
    return t.reshape(s // dil, dil, c).transpose(1, 0, 2).reshape(s, c)


def _from_residue(t, dil):
    if dil == 1:
        return t
    s, c = t.shape
    return t.reshape(dil, s // dil, c).transpose(1, 0, 2).reshape(s, c)


def _cat(refs):
    return jnp.concatenate([t[...] for t in refs], axis=0)


def _head_expander():
    r = lax.broadcasted_iota(jnp.int32, (A_HEADS, A_WIDTH), 0)
    c = lax.broadcasted_iota(jnp.int32, (A_HEADS, A_WIDTH), 1)
    return ((c >= r * A_HEAD_DIM) & (c < (r + 1) * A_HEAD_DIM)).astype(BF16)


def _to_lanes(x16, e):
    a, b, c = _split3(x16)
    return _dot(a, e) + _dot(b, e) + _dot(c, e)


def _per_head_sum(x, e):
    a, b, c = _split3(x)
    return _dot(a, e, 1, 1) + _dot(b, e, 1, 1) + _dot(c, e, 1, 1)


def _pair_low_lanes():
    return lax.broadcasted_iota(jnp.int32, (A_QBLOCK, LANE), 1) < A_HEAD_DIM


def _top_rows():
    return lax.broadcasted_iota(jnp.int32, (2 * A_QBLOCK, 1), 0) < A_QBLOCK


def _block_diag(v, low):
    zero = jnp.zeros_like(v)
    return jnp.concatenate([jnp.where(low, v, zero), jnp.where(low, zero, v)], axis=0)


def _attn_fwd(qkv, g, name):
    s = qkv.shape[0]
    dil = DILATIONS[g]
    length = s // dil
    nblk = length // A_QBLOCK

    def body(q_ref, k0, k1, k2, k3, v0, v1, v2, v3, o_ref, l_ref):
        dist = _band_dist(pl.program_id(1), length, dil, False)
        kk = _cat((k0, k1, k2, k3))
        vv = _cat((v0, v1, v2, v3))
        low = _pair_low_lanes()
        top = _top_rows()
        dist2 = jnp.concatenate([dist, dist], axis=0)
        for hp in range(A_HEADS // 2):
            ls = slice(hp * LANE, (hp + 1) * LANE)
            qp, kp, vp = q_ref[:, ls], kk[:, ls], vv[:, ls]
            sc = _dot(_block_diag(qp, low), kp, 1, 1) * 0.125 - jnp.where(top, SLOPES[2 * hp], SLOPES[2 * hp + 1]) * dist2
            m = jnp.max(sc, axis=1, keepdims=True)
            p = jnp.exp(sc - m)
            z = jnp.sum(p, axis=1, keepdims=True)
            o2 = _dot(p.astype(BF16), vp) / z
            lse2 = m + jnp.log(z)
            l_ref[:, 2 * hp:2 * hp + 1] = lse2[:A_QBLOCK]
            l_ref[:, 2 * hp + 1:2 * hp + 2] = lse2[A_QBLOCK:]
            o_ref[:, ls] = jnp.where(low, o2[:A_QBLOCK], o2[A_QBLOCK:])

    qspec = pl.BlockSpec((A_QBLOCK, A_WIDTH), lambda r, n: (r * nblk + n, 0))
    lspec = pl.BlockSpec((A_QBLOCK, A_HEADS), lambda r, n: (r * nblk + n, 0))
    return pl.pallas_call(
        body, name=name, grid=(dil, nblk), in_specs=[qspec] + _span_specs(1, 2 * nblk) + _span_specs(2, 2 * nblk),
        out_specs=[qspec, lspec],
        out_shape=[jax.ShapeDtypeStruct((s, A_WIDTH), F32), jax.ShapeDtypeStruct((s, A_HEADS), F32)],
        compiler_params=_cp("parallel", "parallel"),
    )(*([qkv] * 9))


def _attn_merge(os_, ls_, gate, name):
    s, w = gate.shape
    tm = min(ROWS, s)

    def body(o0, o1, o2, l0, l1, l2, g_ref, y_ref, o_ref, l_ref):
        a, b, c = l0[...], l1[...], l2[...]
        m = jnp.maximum(jnp.maximum(a, b), c)
        ea, eb, ec = jnp.exp(a - m), jnp.exp(b - m), jnp.exp(c - m)
        z = ea + eb + ec
        l_ref[...] = m + jnp.log(z)
        e = _head_expander()
        o = _to_lanes(ea / z, e) * o0[...] + _to_lanes(eb / z, e) * o1[...] + _to_lanes(ec / z, e) * o2[...]
        o_ref[...] = o
        y_ref[...] = (o * _silu(g_ref[...])).astype(BF16)

    rs = _row_spec(tm, w)
    ls = _row_spec(tm, A_HEADS)
    return pl.pallas_call(
        body, name=name, grid=(s // tm,), in_specs=[rs] * 3 + [ls] * 3 + [rs], out_specs=[rs, rs, ls],
        out_shape=[jax.ShapeDtypeStruct((s, w), BF16), jax.ShapeDtypeStruct((s, w), F32), jax.ShapeDtypeStruct((s, A_HEADS), F32)],
        compiler_params=_cp("parallel"),
    )(*os_, *ls_, gate)


def _attn_gate_bwd(dyy, o, gate, name):
    s, w = gate.shape
    tm = min(ROWS, s)

    def body(dy_ref, o_ref, g_ref, do_ref, dg_ref, dl_ref):
        dyv, ov, gv = dy_ref[...], o_ref[...], g_ref[...]
        do = dyv * _silu(gv)
        do_ref[...] = do.astype(BF16)
        dg_ref[...] = (dyv * ov * _dsilu(gv)).astype(BF16)
        dl_ref[...] = _per_head_sum(do * ov, _head_expander())

    rs = _row_spec(tm, w)
    return pl.pallas_call(
        body, name=name, grid=(s // tm,), in_specs=[rs] * 3, out_specs=[rs, rs, _row_spec(tm, A_HEADS)],
        out_shape=[jax.ShapeDtypeStruct((s, w), BF16), jax.ShapeDtypeStruct((s, w), BF16), jax.ShapeDtypeStruct((s, A_HEADS), F32)],
        compiler_params=_cp("parallel"),
    )(dyy, o, gate)


def _ride_along(ex_in, ex_out, sems, first, last):
    @pl.when(first)
    def _():
        for cp in _shard_exchange(ex_in, ex_out, *sems):
            cp.start()

    @pl.when(last)
    def _():
        for cp in _shard_exchange(ex_in, ex_out, *sems):
            cp.wait()


def _ride_along_specs(exch):
    n = len(exch)
    n_rem = (N_SHARD - 1) * n
    sems = [pltpu.SemaphoreType.DMA((n_rem,)), pltpu.SemaphoreType.DMA((n_rem,)), pltpu.SemaphoreType.DMA((n,))] if n else []
    return [ANY] * n, [jax.ShapeDtypeStruct(t.shape, t.dtype) for t in exch], sems


def _attn_bwd(qkv, do, lse, delta, g, name, exch=()):
    s = qkv.shape[0]
    dil = DILATIONS[g]
    length = s // dil
    nblk = length // A_QBLOCK
    n_ex = len(exch)

    def rows(t16):
        return jnp.pad(t16.reshape(dil, length, A_HEADS).transpose(0, 2, 1), ((0, 0), (0, 0), (A_RADIUS, A_RADIUS)))

    def body(*refs):
        (q0, q1, q2, q3, k0, k1, k2, k3, v0, v1, v2, v3, d0, d1, d2, d3, lc_ref, ec_ref, la, lb, ea, eb), refs = refs[:22], refs[22:]
        o_ref = refs[n_ex]
        if n_ex:
            r, n = pl.program_id(0), pl.program_id(1)
            _ride_along(refs[:n_ex], refs[n_ex + 1:2 * n_ex + 1], refs[2 * n_ex + 1:], (r == 0) & (n == 0), (r == dil - 1) & (n == nblk - 1))
        dist = _band_dist(pl.program_id(1), length, dil, False)
        qq, kk, vv, dd = _cat((q0, q1, q2, q3)), _cat((k0, k1, k2, k3)), _cat((v0, v1, v2, v3)), _cat((d0, d1, d2, d3))
        lse_r = jnp.concatenate([la[...], lb[...]], axis=1)
        dlt_r = jnp.concatenate([ea[...], eb[...]], axis=1)
        low = _pair_low_lanes()
        top = _top_rows()
        dist2 = jnp.concatenate([dist, dist], axis=0)
        centre = slice(A_RADIUS, A_RADIUS + A_QBLOCK)
        for hp in range(A_HEADS // 2):
            ls = slice(hp * LANE, (hp + 1) * LANE)
            qs, ks, vs, ds_ = qq[:, ls], kk[:, ls], vv[:, ls], dd[:, ls]
            qn, kn, vn, dn = qs[centre], ks[centre], vs[centre], ds_[centre]
            h0, h1 = 2 * hp, 2 * hp + 1
            bias = jnp.where(top, SLOPES[h0], SLOPES[h1]) * dist2
            lc = jnp.concatenate([lc_ref[:, h0:h0 + 1], lc_ref[:, h1:h1 + 1]], axis=0)
            ec = jnp.concatenate([ec_ref[:, h0:h0 + 1], ec_ref[:, h1:h1 + 1]], axis=0)
            lr = jnp.where(top, lse_r[h0:h0 + 1, :], lse_r[h1:h1 + 1, :])
            er = jnp.where(top, dlt_r[h0:h0 + 1, :], dlt_r[h1:h1 + 1, :])
            p = jnp.exp(_dot(_block_diag(qn, low), ks, 1, 1) * 0.125 - bias - lc)
            dsc = p * (_dot(_block_diag(dn, low), vs, 1, 1) - ec)
            dq = _dot(dsc.astype(BF16), ks)
            pt = jnp.exp(_dot(_block_diag(kn, low), qs, 1, 1) * 0.125 - bias - lr)
            dst = pt * (_dot(_block_diag(vn, low), ds_, 1, 1) - er)
            dk = _dot(dst.astype(BF16), qs)
            dv = _dot(pt.astype(BF16), ds_)
            merge = lambda t: jnp.where(low, t[:A_QBLOCK], t[A_QBLOCK:])
            o_ref[:, ls] = (merge(dq) * 0.125).astype(BF16)
            o_ref[:, A_WIDTH + hp * LANE:A_WIDTH + (hp + 1) * LANE] = (merge(dk) * 0.125).astype(BF16)
            o_ref[:, 2 * A_WIDTH + hp * LANE:2 * A_WIDTH + (hp + 1) * LANE] = merge(dv).astype(BF16)

    nb64 = 2 * nblk
    dspecs = _span_specs(0, nb64)
    cspec = pl.BlockSpec((A_QBLOCK, A_HEADS), lambda r, n: (r * nblk + n, 0))
    rspecs = [pl.BlockSpec((None, A_HEADS, A_QBLOCK), lambda r, n: (r, 0, n)), pl.BlockSpec((None, A_HEADS, A_QBLOCK), lambda r, n: (r, 0, n + 1))]
    lse_r, dlt_r = rows(lse), rows(delta)
    ex_specs, ex_shapes, sems = _ride_along_specs(exch)
    outs = pl.pallas_call(
        body, name=name, grid=(dil, nblk),
        in_specs=_span_specs(0, nb64) + _span_specs(1, nb64) + _span_specs(2, nb64) + dspecs + [cspec, cspec] + rspecs * 2 + ex_specs,
        out_specs=[pl.BlockSpec((A_QBLOCK, 3 * A_WIDTH), lambda r, n: (r * nblk + n, 0))] + ex_specs,
        out_shape=[jax.ShapeDtypeStruct((s, 3 * A_WIDTH), BF16)] + ex_shapes,
        scratch_shapes=sems,
        compiler_params=_cp(*(("arbitrary", "arbitrary") if n_ex else ("parallel", "parallel"))),
    )(*([qkv] * 12), *([do] * 4), lse, delta, lse_r, lse_r, dlt_r, dlt_r, *exch)
    return (outs[0], outs[1:]) if n_ex else outs[0]


def _attn_layer_fwd(h, w_qkv, w_gate, w_out, li):
    nm = lambda t: f"a{li}_{t}"
    gate = _mm(h, w_gate, name=nm("gate"))
    hs, qkvs, os_, ls_ = [], [], [], []
    for g, dil in enumerate(DILATIONS):
        hg = _to_residue(h, dil)
        qkv = _mm(hg, w_qkv[:, g * 3 * A_WIDTH:(g + 1) * 3 * A_WIDTH], out_dtype=BF16, name=nm(f"qkv{g}"))
        o, l = _attn_fwd(qkv, g, nm(f"attn{g}"))
        hs.append(hg)
        qkvs.append(qkv)
        os_.append(_from_residue(o, dil))
        ls_.append(_from_residue(l, dil))
    y, o, lse = _attn_merge(os_, ls_, gate, nm("merge"))
    out = _mm(y, w_out, name=nm("out"))
    return out, (hs, qkvs, gate, y, o, lse)


def _attn_layer_bwd(dy, h, saved, w_qkv, w_gate, w_out, li, exch=()):
    nm = lambda t: f"a{li}_{t}"
    hs, qkvs, gate, y, o, lse = saved
    g_w_out = _mm(y, dy, ta=True, out_dtype=BF16, name=nm("dwout"))
    dyy = _mm(dy, w_out, tb=True, name=nm("dyy"))
    do, dgate, delta = _attn_gate_bwd(dyy, o, gate, nm("gatebwd"))
    dhs, dws, arrived = [], [], ()
    for g, dil in enumerate(DILATIONS):
        dqkv = _attn_bwd(qkvs[g], _to_residue(do, dil), _to_residue(lse, dil), _to_residue(delta, dil), g, nm(f"attnbwd{g}"),
                         exch if g == 0 else ())
        if g == 0 and exch:
            dqkv, arrived = dqkv
        wg = w_qkv[:, g * 3 * A_WIDTH:(g + 1) * 3 * A_WIDTH]
        dws.append(_mm(hs[g], dqkv, ta=True, out_dtype=BF16, name=nm(f"dwqkv{g}")))
        add = _mm(dgate, w_gate, tb=True, name=nm("dh_gate")) if g == 0 else None
        dhs.append(_from_residue(_mm(dqkv, wg, tb=True, add=add, name=nm(f"dh_qkv{g}")), dil))
    g_w_in = jnp.concatenate(dws + [_mm(h, dgate, ta=True, out_dtype=BF16, name=nm("dwgate"))], axis=1)
    return dhs, g_w_in, g_w_out, arrived


SSM_INNER = SSM_HEADS * SSM_HEAD_DIM
SSM_BC = SSM_GROUPS * SSM_STATE
SSM_CONV_DIM = SSM_INNER + 2 * SSM_BC
GW = SSM_REP * SSM_HEAD_DIM
T = SSM_CHUNK
HALO = 8


def _conv_specs(tm, tn, s, col=lambda j: j):
    nb8 = s // HALO
    cur = pl.BlockSpec((tm, tn), lambda j, i: (i, col(j)))
    prev = pl.BlockSpec((HALO, tn), lambda j, i: (jnp.maximum(i * (tm // HALO) - 1, 0), col(j)))
    nxt = pl.BlockSpec((HALO, tn), lambda j, i: (jnp.minimum((i + 1) * (tm // HALO), nb8 - 1), col(j)))
    return [prev, cur, nxt]


def _extend(prev_ref, cur_ref, nxt_ref, i, nrow):
    p = jnp.where(i == 0, 0.0, prev_ref[...])
    n = jnp.where(i == nrow - 1, 0.0, nxt_ref[...])
    return jnp.concatenate([p, cur_ref[...], n], axis=0)


def _shift_rows(ext, off, tm):
    rows = ext.shape[0]
    return pltpu.roll(ext, (-off) % rows, 0)[HALO:HALO + tm]


def _conv_fwd(xraw, w, b, name):
    s, cdim = xraw.shape
    tm, tn = min(512, s), 1024
    nrow = s // tm

    def body(p_ref, c_ref, n_ref, w_ref, b_ref, pre_ref, act_ref):
        ext = _extend(p_ref, c_ref, n_ref, pl.program_id(1), nrow)
        acc = jnp.broadcast_to(b_ref[...], (tm, tn))
        for k in range(SSM_CONV):
            acc = acc + w_ref[k:k + 1, :] * _shift_rows(ext, k - SSM_CONV // 2, tm)
        pre_ref[...] = acc
        act_ref[...] = _silu(acc)

    prev, cur, nxt = _conv_specs(tm, tn, s)
    return pl.pallas_call(
        body, name=name, grid=(cdim // tn, nrow),
        in_specs=[prev, cur, nxt, pl.BlockSpec((SSM_CONV, tn), lambda j, i: (0, j)), pl.BlockSpec((1, tn), lambda j, i: (0, j))],
        out_specs=[cur, cur], out_shape=[jax.ShapeDtypeStruct((s, cdim), F32)] * 2,
        compiler_params=_cp("parallel", "parallel"),
    )(xraw, xraw, xraw, w, b)


def _conv_bwd(dx, db, dc, pre, xraw, w, name):
    s, cdim = xraw.shape
    tm, tn = min(512, s), 1024
    nrow = s // tm
    nx = dx.shape[1] // tn

    def body(xp, xc, xn, bp, bc, bn, cp, cc, cn, pp, pc, pn, x_ref, w_ref, o_ref, red_ref):
        j, i = pl.program_id(0), pl.program_id(1)

        @pl.when(i == 0)
        def _():
            red_ref[...] = jnp.zeros_like(red_ref)

        bcext = jnp.concatenate([_extend(bp, bc, bn, i, nrow), _extend(cp, cc, cn, i, nrow)], axis=1)
        dact = jnp.where(j < nx, _extend(xp, xc, xn, i, nrow), bcext)
        dpre = dact * _dsilu(_extend(pp, pc, pn, i, nrow))
        xv = x_ref[...]
        acc = jnp.zeros((tm, tn), F32)
        for k in range(SSM_CONV):
            sk = _shift_rows(dpre, SSM_CONV // 2 - k, tm)
            acc = acc + w_ref[k:k + 1, :] * sk
            red_ref[k:k + 1, :] += jnp.sum(sk * xv, axis=0, keepdims=True)
        red_ref[SSM_CONV:SSM_CONV + 1, :] += jnp.sum(dpre[HALO:HALO + tm], axis=0, keepdims=True)
        o_ref[...] = acc.astype(BF16)

    half = tn // 2
    cur = pl.BlockSpec((tm, tn), lambda j, i: (i, j))
    return pl.pallas_call(
        body, name=name, grid=(cdim // tn, nrow),
        in_specs=_conv_specs(tm, tn, s, lambda j: jnp.minimum(j, nx - 1)) + _conv_specs(tm, half, s, lambda j: 0) * 2
        + _conv_specs(tm, tn, s) + [cur, pl.BlockSpec((SSM_CONV, tn), lambda j, i: (0, j))],
        out_specs=[cur, pl.BlockSpec((8, tn), lambda j, i: (0, j))],
        out_shape=[jax.ShapeDtypeStruct((s, cdim), BF16), jax.ShapeDtypeStruct((8, cdim), F32)],
        compiler_params=_cp("parallel", "arbitrary"),
    )(dx, dx, dx, db, db, db, dc, dc, dc, pre, pre, pre, xraw, w)


def _tri(lower):
    r = lax.broadcasted_iota(jnp.int32, (T, T), 0)
    c = lax.broadcasted_iota(jnp.int32, (T, T), 1)
    return (r >= c) if lower else (r <= c)


def _softplus(x):
    return jnp.maximum(x, 0.0) + jnp.log(1.0 + jnp.exp(-jnp.abs(x)))


def _dt_prep(dt_raw, bias, a_log, name):
    s = dt_raw.shape[0]
    nc = s // T

    def body(r_ref, b_ref, a_ref, dt_ref, cum_ref, cumt_ref):
        dt = _softplus(r_ref[...] + b_ref[...])
        da = dt * (-jnp.exp(a_ref[...]))
        pre = _dot_exact(_tri(True).astype(BF16), da)
        suf = _dot_exact(_tri(False).astype(BF16), da)
        lane = lax.broadcasted_iota(jnp.int32, (T, LANE), 1)
        cum = jnp.where(lane < SSM_HEADS, pre, suf)
        dt_ref[...] = dt
        cum_ref[...] = cum
        cumt_ref[...] = cum.T

    blk = pl.BlockSpec((T, LANE), lambda c: (c, 0))
    vec = pl.BlockSpec((1, LANE), lambda c: (0, 0))
    return pl.pallas_call(
        body, name=name, grid=(nc,), in_specs=[blk, vec, vec],
        out_specs=[blk, blk, pl.BlockSpec((None, LANE, T), lambda c: (c, 0, 0))],
        out_shape=[jax.ShapeDtypeStruct((s, LANE), F32), jax.ShapeDtypeStruct((s, LANE), F32), jax.ShapeDtypeStruct((nc, LANE, T), F32)],
        compiler_params=_cp("parallel"),
    )(dt_raw, bias, a_log)


def _by_group(t):
    s = t.shape[0]
    return t[:, :2 * SSM_HEADS].reshape(s, 2 * SSM_GROUPS, SSM_REP).transpose(1, 0, 2)


def _from_group(tf, tb):
    s = tf.shape[1]
    t = jnp.concatenate([tf, tb], axis=0).transpose(1, 0, 2).reshape(s, 2 * SSM_HEADS)
    return jnp.pad(t, ((0, 0), (0, LANE - 2 * SSM_HEADS)))


def _decay_mats(acol, arow, rev):
    after = _tri(not rev)
    return jnp.where(after, jnp.exp(jnp.where(after, acol - arow, 0.0)), 0.0)


PAIRS = SSM_REP // 2


def _low_lanes():
    return lax.broadcasted_iota(jnp.int32, (T, LANE), 1) < SSM_HEAD_DIM


CPS = 8
TB = CPS * T


def _scan_specs(rev, ci):
    nxb = SSM_INNER // LANE
    kofs = SSM_GROUPS if rev else 0
    return [
        pl.BlockSpec((TB, GW), lambda g, c: (ci(c), g)),
        pl.BlockSpec((TB, LANE), lambda g, c: (ci(c), nxb + g)),
        pl.BlockSpec((TB, LANE), lambda g, c: (ci(c), nxb + SSM_GROUPS + g)),
        pl.BlockSpec((None, TB, SSM_REP), lambda g, c: (kofs + g, ci(c), 0)),
        pl.BlockSpec((None, TB, SSM_REP), lambda g, c: (kofs + g, ci(c), 0)),
        pl.BlockSpec((None, CPS, SSM_REP, T), lambda g, c: (kofs + g, ci(c), 0, 0)),
    ]


def _chunk_rows(q):
    return pl.ds(q * T, T)


def _pair_lanes(ref, p, low):
    return jnp.where(low, ref[:, 2 * p:2 * p + 1], ref[:, 2 * p + 1:2 * p + 2])


def _ssd_scan(xbc, dtk, cumk, cumtk, rev, name, add=None):
    s = xbc.shape[0]
    nc = s // T
    nb = nc // CPS
    last = 0 if rev else T - 1
    ci = (lambda c: nb - 1 - c) if rev else (lambda c: c)
    has_add = add is not None

    def body(*refs):
        x_ref, b_ref, c_ref, dt_ref, cum_ref, cumt_ref = refs[:6]
        a_ref = refs[6] if has_add else None
        y_ref, st_ref, state = refs[-3:]

        @pl.when(pl.program_id(1) == 0)
        def _():
            state[...] = jnp.zeros_like(state)

        for q in (reversed(range(CPS)) if rev else range(CPS)):
            rows = _chunk_rows(q)
            chunk(x_ref.at[rows], b_ref.at[rows], c_ref.at[rows], dt_ref.at[rows], cum_ref.at[rows], cumt_ref.at[q],
                  a_ref.at[rows] if has_add else None, y_ref.at[rows], st_ref.at[q], state)

    def chunk(x_ref, b_ref, c_ref, dt_ref, cum_ref, cumt_ref, a_ref, y_ref, st_ref, state):
        bm = b_ref[...]
        cm = c_ref[...].astype(BF16)
        cb = _dot(cm, bm.astype(BF16), 1, 1)
        bt = bm.T.astype(BF16)
        low = _low_lanes()
        for p in range(PAIRS):
            ls = slice(p * LANE, (p + 1) * LANE)
            acum = _pair_lanes(cum_ref, p, low)
            u = x_ref[:, ls] * _pair_lanes(dt_ref, p, low)
            tot = acum[last:last + 1, :]
            m = [(cb * _decay_mats(cum_ref[:, r:r + 1], cumt_ref[r:r + 1, :], rev)).astype(BF16) for r in (2 * p, 2 * p + 1)]
            st = state[p]
            st_ref[p] = st
            yd = _dot(jnp.concatenate(m, axis=1), _block_diag(u.astype(BF16), low))
            yo = jnp.exp(acum) * _dot(cm, st.astype(BF16))
            y_ref[:, ls] = yd + yo + a_ref[:, ls] if has_add else yd + yo
            state[p] = jnp.exp(tot) * st + _dot(bt, (jnp.exp(tot - acum) * u).astype(BF16))

    yspec = pl.BlockSpec((TB, GW), lambda g, c: (ci(c), g))
    return pl.pallas_call(
        body, name=name, grid=(SSM_GROUPS, nb), in_specs=_scan_specs(rev, ci) + ([yspec] if has_add else []),
        out_specs=[yspec, pl.BlockSpec((CPS, PAIRS, SSM_STATE, LANE), lambda g, c: (ci(c), g, 0, 0))],
        out_shape=[jax.ShapeDtypeStruct((s, SSM_INNER), F32), jax.ShapeDtypeStruct((nc, SSM_HEADS // 2, SSM_STATE, LANE), F32)],
        scratch_shapes=[pltpu.VMEM((PAIRS, SSM_STATE, LANE), F32)],
        compiler_params=_cp("parallel", "arbitrary"),
    )(xbc, xbc, xbc, dtk, cumk, cumtk, *([add] if has_add else []))


def _ssd_scan_bwd(xbc, dtk, cumk, cumtk, dy, states, dvec, prev, rev, name, exch=()):
    s = xbc.shape[0]
    nc = s // T
    nb = nc // CPS
    last = 0 if rev else T - 1
    ci = (lambda c: c) if rev else (lambda c: nb - 1 - c)
    has_prev = prev is not None
    n_in = 12 if has_prev else 9
    n_ex = len(exch)

    def body(*refs):
        ins, ex_in, refs = refs[:n_in], refs[n_in:n_in + n_ex], refs[n_in + n_ex:]
        outs, ex_out, scratch = refs[:5], refs[5:5 + n_ex], refs[5 + n_ex:]
        if n_ex:
            scratch, sems = scratch[:4], scratch[4:]
            g, c = pl.program_id(0), pl.program_id(1)
            _ride_along(ex_in, ex_out, sems, (g == 0) & (c == 0), (g == SSM_GROUPS - 1) & (c == nb - 1))

        @pl.when(pl.program_id(1) == 0)
        def _():
            scratch[0][...] = jnp.zeros_like(scratch[0])

        for q in (range(CPS) if rev else reversed(range(CPS))):
            rows = _chunk_rows(q)
            cut = lambda t: t.at[rows]
            x_ref, b_ref, c_ref, dt_ref, cum_ref, cumt_ref, dy_ref, st_ref, dv_ref = ins[:9]
            sub = [cut(x_ref), cut(b_ref), cut(c_ref), cut(dt_ref), cut(cum_ref), cumt_ref.at[q], cut(dy_ref), st_ref.at[q], dv_ref]
            chunk(*sub, *[cut(t) for t in ins[9:]], *[cut(t) for t in outs], *scratch)

    def chunk(*refs):
        x_ref, b_ref, c_ref, dt_ref, cum_ref, cumt_ref, dy_ref, st_ref, dv_ref = refs[:9]
        refs = refs[9:]
        if has_prev:
            pdx, pdb, pdc = refs[:3]
            refs = refs[3:]
        dx_ref, db_ref, dc_ref, ddt_ref, dda_ref, dstate, rs_buf, in_buf, k_buf = refs
        rs_buf[...] = jnp.zeros_like(rs_buf)
        in_buf[...] = jnp.zeros_like(in_buf)
        k_buf[...] = jnp.zeros_like(k_buf)
        bm = b_ref[...].astype(BF16)
        cm = c_ref[...].astype(BF16)
        cbt = _dot(bm, cm, 1, 1)
        cb = _dot(cm, bm, 1, 1)
        ct = c_ref[...].T.astype(BF16)
        after = _tri(not rev)
        before = _tri(rev)
        from_k = before.astype(BF16)
        ri = lax.broadcasted_iota(jnp.int32, (T, T), 0)
        cj = lax.broadcasted_iota(jnp.int32, (T, T), 1)
        strictly_before = (cj > ri) if rev else (cj < ri)
        dcb = jnp.zeros((T, T), F32)
        dc_acc = jnp.zeros((T, SSM_STATE), F32)
        db_acc = jnp.zeros((T, SSM_STATE), F32)
        low = _low_lanes()
        ri2 = lax.broadcasted_iota(jnp.int32, (LANE, LANE), 0)
        cj2 = lax.broadcasted_iota(jnp.int32, (LANE, LANE), 1)
        halves = ((ri2 < SSM_HEAD_DIM) == (cj2 == 0)) & (cj2 < 2)
        halves = halves.astype(BF16)

        def head_sums(v):
            hi = v.astype(BF16)
            lo = (v - hi.astype(F32)).astype(BF16)
            return _dot(hi, halves) + _dot(lo, halves)

        for p in range(PAIRS):
            ls = slice(p * LANE, (p + 1) * LANE)
            c2 = slice(2 * p, 2 * p + 2)
            lm, lmt = [], []
            for r in (2 * p, 2 * p + 1):
                acol = cum_ref[:, r:r + 1]
                arow = cumt_ref[r:r + 1, :]
                lm.append(jnp.where(after, jnp.exp(jnp.where(after, acol - arow, 0.0)), 0.0))
                lmt.append(jnp.where(before, jnp.exp(jnp.where(before, arow - acol, 0.0)), 0.0))
            acum = _pair_lanes(cum_ref, p, low)
            tot = acum[last:last + 1, :]
            dtl = _pair_lanes(dt_ref, p, low)
            xl = x_ref[:, ls]
            u = xl * dtl
            ub = u.astype(BF16)
            dyl = dy_ref[:, ls]
            dyb = dyl.astype(BF16)
            st = st_ref[p]
            stb = st.astype(BF16)
            dst = dstate[p]
            dstb = dst.astype(BF16)
            dec = jnp.exp(tot - acum)
            eac = jnp.exp(acum)
            etot = jnp.exp(tot)
            du_off = dec * _dot(bm, dstb)
            mt = jnp.concatenate([(cbt * lmt[0]).astype(BF16), (cbt * lmt[1]).astype(BF16)], axis=1)
            dyd = _block_diag(dyb, low)
            du = _dot(mt, dyd) + du_off
            g2 = _dot(dyd, ub, 1, 1)
            gl = [g2[:T] * lm[0], g2[T:] * lm[1]]
            dcb = dcb + gl[0] + gl[1]
            dc_acc = dc_acc + _dot((eac * dyl).astype(BF16), stb, 1, 1)
            db_acc = db_acc + _dot((dec * u).astype(BF16), dstb, 1, 1)
            w = jnp.concatenate([(gl[0] * cb).astype(BF16), (gl[1] * cb).astype(BF16)], axis=1)
            crossing = _dot(from_k, w)
            for j in range(2):
                cr = jnp.where(strictly_before, crossing[:, j * T:(j + 1) * T], 0.0)
                in_buf[:, 2 * p + j:2 * p + j + 1] = jnp.sum(cr, axis=1, keepdims=True)
            y_off = eac * _dot(cm, stb)
            udu = u * du_off
            rs_buf[:, c2] = head_sums(dyl * y_off - udu)[:, 0:2]
            col = jnp.sum(dst * (etot * st) + udu, axis=0, keepdims=True)
            k_buf[0:1, c2] = head_sums(jnp.broadcast_to(col, (8, LANE)))[0:1, 0:2]
            ddt_ref[:, c2] = head_sums(du * xl)[:, 0:2]
            dx = du * dtl
            if has_prev:
                dx = dx + pdx[:, ls]
            else:
                dx = dx + dyl * dv_ref[:, ls]
            dx_ref[:, ls] = dx
            dstate[p] = etot * dst + _dot(ct, (eac * dyl).astype(BF16))
        dda = in_buf[...] + _dot_exact(from_k, rs_buf[...]) + k_buf[0:1, :]
        dda_ref[...] = dda[:, :SSM_REP]
        dcbb = dcb.astype(BF16)
        dc = dc_acc + _dot(dcbb, bm)
        db = db_acc + _dot(dcbb, cm, 0, 0)
        if has_prev:
            dc = dc + pdc[...]
            db = db + pdb[...]
        dc_ref[...] = dc
        db_ref[...] = db

    xspec = pl.BlockSpec((TB, GW), lambda g, c: (ci(c), g))
    gspec = pl.BlockSpec((TB, LANE), lambda g, c: (ci(c), g))
    in_specs = _scan_specs(rev, ci) + [
        xspec,
        pl.BlockSpec((CPS, PAIRS, SSM_STATE, LANE), lambda g, c: (ci(c), g, 0, 0)),
        pl.BlockSpec((1, GW), lambda g, c: (0, g)),
    ]
    args = [xbc, xbc, xbc, dtk, cumk, cumtk, dy, states, dvec]
    if has_prev:
        in_specs += [xspec, gspec, gspec]
        args += list(prev)
    ospec8 = pl.BlockSpec((None, TB, SSM_REP), lambda g, c: (g, ci(c), 0))
    ex_specs, ex_shapes, sems = _ride_along_specs(exch)
    outs = pl.pallas_call(
        body, name=name, grid=(SSM_GROUPS, nb), in_specs=in_specs + ex_specs,
        out_specs=[xspec, gspec, gspec, ospec8, ospec8] + ex_specs,
        out_shape=[jax.ShapeDtypeStruct((s, SSM_INNER), F32), jax.ShapeDtypeStruct((s, SSM_BC), F32), jax.ShapeDtypeStruct((s, SSM_BC), F32),
                   jax.ShapeDtypeStruct((SSM_GROUPS, s, SSM_REP), F32), jax.ShapeDtypeStruct((SSM_GROUPS, s, SSM_REP), F32)]
        + ex_shapes,
        scratch_shapes=[pltpu.VMEM((PAIRS, SSM_STATE, LANE), F32), pltpu.VMEM((T, LANE), F32), pltpu.VMEM((T, LANE), F32),
                        pltpu.VMEM((8, LANE), F32)] + sems,
        compiler_params=_cp("arbitrary" if n_ex else "parallel", "arbitrary"),
    )(*args, *exch)
    return (outs[:5], outs[5:]) if n_ex else outs


def _ssd_post(y, xbc, z, dvec, nw, name):
    s = z.shape[0]
    tm = min(256, s)

    def body(y_ref, x_ref, z_ref, dv_ref, nw_ref, o_ref):
        ys = y_ref[...] + dv_ref[...] * x_ref[...]
        yg = ys * _silu(z_ref[...])
        ms = jnp.mean(yg * yg, axis=1, keepdims=True)
        o_ref[...] = (yg * lax.rsqrt(ms + RMS_EPS) * nw_ref[...]).astype(BF16)

    rs = _row_spec(tm, SSM_INNER)
    vs = _vec_spec(SSM_INNER)
    return pl.pallas_call(
        body, name=name, grid=(s // tm,), in_specs=[rs, rs, rs, vs, vs], out_specs=rs,
        out_shape=jax.ShapeDtypeStruct((s, SSM_INNER), BF16), compiler_params=_cp("parallel"),
    )(y, xbc, z, dvec, nw)


def _ssd_post_bwd(dyn, y, xbc, z, dvec, nw, name):
    s = z.shape[0]
    tm = min(256, s)

    def body(dyn_ref, y_ref, x_ref, z_ref, dv_ref, nw_ref, dys_ref, dz_ref, red_ref):
        @pl.when(pl.program_id(0) == 0)
        def _():
            red_ref[...] = jnp.zeros_like(red_ref)

        xv, zv = x_ref[...], z_ref[...]
        ys = y_ref[...] + dv_ref[...] * xv
        sz = _silu(zv)
        yg = ys * sz
        rstd = lax.rsqrt(jnp.mean(yg * yg, axis=1, keepdims=True) + RMS_EPS)
        yhat = yg * rstd
        dynv = dyn_ref[...]
        dyh = dynv * nw_ref[...]
        dyg = rstd * (dyh - yhat * jnp.mean(dyh * yhat, axis=1, keepdims=True))
        dys = dyg * sz
        dys_ref[...] = dys
        dz_ref[...] = (dyg * ys * _dsilu(zv)).astype(BF16)
        red_ref[0:1, :] += jnp.sum(dynv * yhat, axis=0, keepdims=True)
        red_ref[1:2, :] += jnp.sum(dys * xv, axis=0, keepdims=True)

    rs = _row_spec(tm, SSM_INNER)
    vs = _vec_spec(SSM_INNER)
    return pl.pallas_call(
        body, name=name, grid=(s // tm,), in_specs=[rs, rs, rs, rs, vs, vs],
        out_specs=[rs, rs, _vec_spec(SSM_INNER, 8)],
        out_shape=[jax.ShapeDtypeStruct((s, SSM_INNER), F32), jax.ShapeDtypeStruct((s, SSM_INNER), BF16), jax.ShapeDtypeStruct((8, SSM_INNER), F32)],
        compiler_params=_cp("arbitrary"),
    )(dyn, y, xbc, z, dvec, nw)


def _dt_bwd(dt_raw, bias, a_log, dt, ddt, dda, name):
    s = dt_raw.shape[0]
    tm = min(1024, s)

    def body(r_ref, b_ref, a_ref, dt_ref, ddt_ref, dda_ref, o_ref, red_ref):
        @pl.when(pl.program_id(0) == 0)
        def _():
            red_ref[...] = jnp.zeros_like(red_ref)

        a = -jnp.exp(a_ref[...])
        ddav = dda_ref[...]
        draw = (ddt_ref[...] + a * ddav) * _sigmoid(r_ref[...] + b_ref[...])
        o_ref[...] = draw.astype(BF16)
        red_ref[0:1, :] += jnp.sum(draw, axis=0, keepdims=True)
        red_ref[1:2, :] += a * jnp.sum(ddav * dt_ref[...], axis=0, keepdims=True)

    rs = _row_spec(tm, LANE)
    vs = _vec_spec(LANE)
    return pl.pallas_call(
        body, name=name, grid=(s // tm,), in_specs=[rs, vs, vs, rs, rs, rs], out_specs=[rs, _vec_spec(LANE, 8)],
        out_shape=[jax.ShapeDtypeStruct((s, LANE), BF16), jax.ShapeDtypeStruct((8, LANE), F32)],
        compiler_params=_cp("arbitrary"),
    )(dt_raw, bias, a_log, dt, ddt, dda)


def _pad_lanes(v):
    v = v.reshape(1, -1)
    return jnp.pad(v, ((0, 0), (0, LANE - v.shape[1])))


def _ssd_prep_weights(w_in, conv_w, conv_b, dt_bias, a_log, d_skip, norm_w, w_out):
    return dict(
        w_z=w_in[:, :SSM_INNER].astype(BF16),
        w_xbc=w_in[:, SSM_INNER:SSM_INNER + SSM_CONV_DIM].astype(BF16),
        w_dt=jnp.pad(w_in[:, SSM_INNER + SSM_CONV_DIM:], ((0, 0), (0, LANE - 2 * SSM_HEADS))).astype(BF16),
        conv_w=conv_w, conv_b=conv_b.reshape(1, -1), bias=_pad_lanes(dt_bias), a_log=_pad_lanes(a_log),
        dvec=jnp.repeat(d_skip, SSM_HEAD_DIM).reshape(1, -1), nw=norm_w.reshape(1, -1), w_out=w_out.astype(BF16),
    )


def _ssd_layer_fwd(h, w, li):
    nm = lambda t: f"b{li}_{t}"
    z = _mm(h, w["w_z"], name=nm("z"))
    xraw = _mm(h, w["w_xbc"], name=nm("xbc"))
    dt_raw = _mm(h, w["w_dt"], name=nm("dt"))
    pre, xbc = _conv_fwd(xraw, w["conv_w"], w["conv_b"], nm("conv"))
    dt, cum, cumt = _dt_prep(dt_raw, w["bias"], w["a_log"], nm("dtprep"))
    nc = cumt.shape[0]
    dtk, cumk = _by_group(dt), _by_group(cum)
    cumtk = cumt[:, :2 * SSM_HEADS].reshape(nc, 2 * SSM_GROUPS, SSM_REP, T).transpose(1, 0, 2, 3)
    yf, stf = _ssd_scan(xbc, dtk, cumk, cumtk, False, nm("scan_f"))
    y, stb = _ssd_scan(xbc, dtk, cumk, cumtk, True, nm("scan_b"), add=yf)
    yn = _ssd_post(y, xbc, z, w["dvec"], w["nw"], nm("post"))
    out = _mm(yn, w["w_out"], name=nm("out"))
    return out, (z, xraw, dt_raw, pre, xbc, dt, dtk, cumk, cumtk, y, stf, stb, yn)


def _ssd_layer_bwd(dy, h, saved, w, li, exch=((), ())):
    nm = lambda t: f"b{li}_{t}"
    z, xraw, dt_raw, pre, xbc, dt, dtk, cumk, cumtk, y, stf, stb, yn = saved
    g_w_out = _mm(yn, dy, ta=True, out_dtype=BF16, name=nm("dwout"))
    dyn = _mm(dy, w["w_out"], tb=True, name=nm("dyn"))
    dys, dz, pred = _ssd_post_bwd(dyn, y, xbc, z, w["dvec"], w["nw"], nm("postbwd"))
    arrived = [(), ()]
    res = _ssd_scan_bwd(xbc, dtk, cumk, cumtk, dys, stf, w["dvec"], None, False, nm("scanbwd_f"), exch[0])
    if exch[0]:
        res, arrived[0] = res
    dx1, db1, dc1, ddt_f, dda_f = res
    res = _ssd_scan_bwd(xbc, dtk, cumk, cumtk, dys, stb, w["dvec"], (dx1, db1, dc1), True, nm("scanbwd_b"), exch[1])
    if exch[1]:
        res, arrived[1] = res
    dx, db, dc, ddt_b, dda_b = res
    dxraw, cred = _conv_bwd(dx, db, dc, pre, xraw, w["conv_w"], nm("convbwd"))
    draw, dred = _dt_bwd(dt_raw, w["bias"], w["a_log"], dt, _from_group(ddt_f, ddt_b), _from_group(dda_f, dda_b), nm("dtbwd"))
    dh = _mm(dz, w["w_z"], tb=True, name=nm("dh_z"))
    dh = _mm(dxraw, w["w_xbc"], tb=True, add=dh, name=nm("dh_xbc"))
    dh = _mm(draw, w["w_dt"], tb=True, add=dh, name=nm("dh_dt"))
    g_w_in = jnp.concatenate([_mm(h, dz, ta=True, out_dtype=BF16, name=nm("dwz")), _mm(h, dxraw, ta=True, out_dtype=BF16, name=nm("dwxbc")),
                              _mm(h, draw, ta=True, out_dtype=BF16, name=nm("dwdt"))[:, :2 * SSM_HEADS]], axis=1)
    grads = (g_w_in, cred[:SSM_CONV], cred[SSM_CONV], dred[0, :2 * SSM_HEADS].reshape(2, SSM_HEADS),
             dred[1, :2 * SSM_HEADS].reshape(2, SSM_HEADS), pred[1].reshape(SSM_HEADS, SSM_HEAD_DIM).sum(axis=1), pred[0], g_w_out)
    return dh, grads, arrived


B_GRAD_NAMES = ("b_w_in", "b_conv_w", "b_conv_b", "b_dt_bias", "b_a_log", "b_d", "b_norm_w", "b_w_out")


def _shards_of(big, j):
    return [_shard_cols(big["a_w_in"][j]), big["a_w_out"][j].reshape(N_SHARD, -1, big["a_w_out"][j].shape[-1]),
            _shard_cols(big["b_w_in"][j]), big["b_w_out"][j].reshape(N_SHARD, -1, big["b_w_out"][j].shape[-1])]


def _local_step(x, tgt, mod, w, early=False):
    d = x.shape[1]
    qkv_cols = QKV_COLS
    layers = []
    for i in range(DEPTH):
        j = i // 2
        if i % 2 == 0:
            layers.append((w["a_w_in"][j][:, :qkv_cols].astype(BF16), w["a_w_in"][j][:, qkv_cols:].astype(BF16), w["a_w_out"][j].astype(BF16)))
        else:
            layers.append(_ssd_prep_weights(w["b_w_in"][j], w["b_conv_w"][j], w["b_conv_b"][j], w["b_dt_bias"][j], w["b_a_log"][j],
                                            w["b_d"][j], w["b_norm_w"][j], w["b_w_out"][j]))
    saved = []
    for i in range(DEPTH):
        shift, scale, gate = mod[i:i + 1, :d], mod[i:i + 1, d:2 * d], mod[i:i + 1, 2 * d:]
        h = _modulate(x, scale, shift, f"l{i}_mod")
        if i % 2 == 0:
            out, sv = _attn_layer_fwd(h, *layers[i], i)
        else:
            out, sv = _ssd_layer_fwd(h, layers[i], i)
        xn = _resid_ln(x, out, gate, w["ln_g"][i:i + 1], w["ln_b"][i:i + 1], f"l{i}_ln")
        saved.append((x, h, out, sv))
        x = xn
    dx, lred = _loss_grad(x, tgt, "loss")
    loss = 0.5 * jnp.sum(lred[0]) / d
    dmod, g_ln_g, g_ln_b = [None] * DEPTH, [None] * DEPTH, [None] * DEPTH
    ga_in, ga_out = [None, None], [None, None]
    gb = [None, None]
    arrived = None
    for i in reversed(range(DEPTH)):
        j = i // 2
        xi, h, out, sv = saved[i]
        scale, gate = mod[i:i + 1, d:2 * d], mod[i:i + 1, 2 * d:]
        du, dy, red = _resid_ln_bwd(xi, out, dx, gate, w["ln_g"][i:i + 1], f"l{i}_lnbwd")
        g_ln_g[i], g_ln_b[i] = red[1], red[2]
        if i % 2 == 0:
            exch = ()
            if early and j == 0:
                exch = [_shard_cols(gb[0][0]), gb[0][7].reshape(N_SHARD, -1, gb[0][7].shape[-1])]
            dhs, ga_in[j], ga_out[j], got = _attn_layer_bwd(dy, h, sv, *layers[i], i, exch)
            if early and j == 0:
                arrived = (list(got), arrived)
        else:
            exch = ((), ())
            if early and j == 0:
                sh = _shards_of(dict(a_w_in=ga_in, a_w_out=ga_out, b_w_in=[None, gb[1][0]], b_w_out=[None, gb[1][7]]), 1)
                exch = (sh[:2], sh[2:])
            dh, gb[j], got = _ssd_layer_bwd(dy, h, sv, layers[i], i, exch)
            if early and j == 0:
                arrived = list(got[0]) + list(got[1])
            dhs = [dh]
        dx, red2 = _modulate_bwd(du, dhs, xi, scale, f"l{i}_modbwd")
        dmod[i] = jnp.concatenate([red2[1], red2[0], red[0]])
    grads = {"ln_g": jnp.stack(g_ln_g), "ln_b": jnp.stack(g_ln_b), "a_w_in": jnp.stack(ga_in), "a_w_out": jnp.stack(ga_out)}
    for k, n in enumerate(B_GRAD_NAMES):
        grads[n] = jnp.stack([gb[0][k], gb[1][k]])
    big = dict(a_w_in=ga_in, a_w_out=ga_out, b_w_in=[gb[0][0], gb[1][0]], b_w_out=[gb[0][7], gb[1][7]])
    return loss, dx, jnp.stack(dmod), grads, big, arrived


MESH = pl.DeviceIdType.MESH
ANY = pl.BlockSpec(memory_space=pl.ANY)
N_DEV = 8
N_SHARD = 4


def _flip(v, bit):
    return 1 - v if bit else v


def _all_gather8(v, name):
    def body(v_ref, o_ref, send_sems, recv_sems, local_sem):
        x, y, c = lax.axis_index("x"), lax.axis_index("y"), lax.axis_index("c")
        me = 4 * x + 2 * y + c
        local = pltpu.make_async_copy(v_ref, o_ref.at[me], local_sem)
        local.start()
        copies = []
        for k in range(1, N_DEV):
            peer = (_flip(x, k & 4), _flip(y, k & 2), _flip(c, k & 1))
            copies.append(pltpu.make_async_remote_copy(
                src_ref=v_ref, dst_ref=o_ref.at[me], send_sem=send_sems.at[k - 1], recv_sem=recv_sems.at[k - 1],
                device_id=peer, device_id_type=MESH))
        for cp in copies:
            cp.start()
        for cp in copies:
            cp.wait()
        local.wait()

    return pl.pallas_call(
        body, name=name, in_specs=[ANY], out_specs=ANY, out_shape=jax.ShapeDtypeStruct((N_DEV,) + v.shape, v.dtype),
        scratch_shapes=[pltpu.SemaphoreType.DMA((N_DEV - 1,)), pltpu.SemaphoreType.DMA((N_DEV - 1,)), pltpu.SemaphoreType.DMA],
    )(v)


def _shard_exchange(s_refs, o_refs, send_sems, recv_sems, local_sems):
    n = len(s_refs)
    x, y, c = lax.axis_index("x"), lax.axis_index("y"), lax.axis_index("c")
    m = 2 * x + y
    copies = [pltpu.make_async_copy(s_refs[a].at[m], o_refs[a].at[m], local_sems.at[a]) for a in range(n)]
    for k in range(1, N_SHARD):
        px, py = _flip(x, k & 2), _flip(y, k & 1)
        for a in range(n):
            i = (k - 1) * n + a
            copies.append(pltpu.make_async_remote_copy(
                src_ref=s_refs[a].at[2 * px + py], dst_ref=o_refs[a].at[m], send_sem=send_sems.at[i], recv_sem=recv_sems.at[i],
                device_id=(px, py, c), device_id_type=MESH))
    return copies


def _transpose_shards(srcs, name):
    n = len(srcs)
    n_rem = (N_SHARD - 1) * n

    def body(*refs):
        copies = _shard_exchange(refs[:n], refs[n:2 * n], *refs[2 * n:])
        for cp in copies:
            cp.start()
        for cp in copies:
            cp.wait()

    return pl.pallas_call(
        body, name=name, in_specs=[ANY] * n, out_specs=[ANY] * n, out_shape=[jax.ShapeDtypeStruct(s.shape, s.dtype) for s in srcs],
        scratch_shapes=[pltpu.SemaphoreType.DMA((n_rem,)), pltpu.SemaphoreType.DMA((n_rem,)), pltpu.SemaphoreType.DMA((n,))],
    )(*srcs)


def _gather_shards(src, name):
    rows = src.shape[0]
    half = rows // 2
    n_ici = N_SHARD - 1

    def body(s_ref, o_ref, send_sems, recv_sems, local_sem):
        x, y, c = lax.axis_index("x"), lax.axis_index("y"), lax.axis_index("c")
        m = 2 * x + y
        sibling = (x, y, 1 - c)
        my_half = pl.ds(pl.multiple_of(c * half, 16), half)
        its_half = pl.ds(pl.multiple_of((1 - c) * half, 16), half)
        local = pltpu.make_async_copy(s_ref, o_ref.at[m], local_sem)
        local.start()
        chips = [(_flip(x, k & 2), _flip(y, k & 1)) for k in range(1, N_SHARD)]

        def copy(sem, src_ref, dst_ref, to):
            return pltpu.make_async_remote_copy(src_ref=src_ref, dst_ref=dst_ref, send_sem=send_sems.at[sem],
                                                recv_sem=recv_sems.at[sem], device_id=to, device_id_type=MESH)

        first = [copy(i, s_ref.at[my_half], o_ref.at[m, my_half], (px, py, c)) for i, (px, py) in enumerate(chips)]
        for cp in first:
            cp.start()
        passed = []
        for i, (px, py) in enumerate(chips):
            landed = o_ref.at[2 * px + py, my_half]
            copy(i, landed, landed, (px, py, c)).wait_recv()
            passed.append(copy(n_ici + i, landed, landed, sibling))
            passed[-1].start()
        for i, (px, py) in enumerate(chips):
            from_sibling = o_ref.at[2 * px + py, its_half]
            copy(n_ici + i, from_sibling, from_sibling, sibling).wait_recv()
        for cp in first + passed:
            cp.wait_send()
        local.wait()

    return pl.pallas_call(
        body, name=name, in_specs=[ANY], out_specs=ANY, out_shape=jax.ShapeDtypeStruct((N_SHARD,) + src.shape, src.dtype),
        scratch_shapes=[pltpu.SemaphoreType.DMA((2 * n_ici,)), pltpu.SemaphoreType.DMA((2 * n_ici,)), pltpu.SemaphoreType.DMA],
    )(src)


def _swap_sibling(vs, name):
    n = len(vs)

    def body(*refs):
        v_refs, o_refs, (send_sems, recv_sems) = refs[:n], refs[n:2 * n], refs[2 * n:]
        x, y, c = lax.axis_index("x"), lax.axis_index("y"), lax.axis_index("c")
        copies = [pltpu.make_async_remote_copy(src_ref=v_refs[a], dst_ref=o_refs[a], send_sem=send_sems.at[a], recv_sem=recv_sems.at[a],
                                               device_id=(x, y, 1 - c), device_id_type=MESH) for a in range(n)]
        for cp in copies:
            cp.start()
        for cp in copies:
            cp.wait()

    return pl.pallas_call(
        body, name=name, in_specs=[ANY] * n, out_specs=[ANY] * n, out_shape=[jax.ShapeDtypeStruct(v.shape, v.dtype) for v in vs],
        scratch_shapes=[pltpu.SemaphoreType.DMA((n,)), pltpu.SemaphoreType.DMA((n,))],
    )(*vs)


def _row_tile(r, elems, step):
    ok = [t for t in range(step, r + 1, step) if r % t == 0 and t <= elems]
    return max(ok) if ok else r


def _sum_slots(a, name):
    n, r, cdim = a.shape
    tm = _row_tile(r, (4 << 20) // (cdim * 4 * (n + 1)), 16)

    def body(a_ref, o_ref):
        acc = a_ref[0].astype(F32)
        for k in range(1, n):
            acc = acc + a_ref[k].astype(F32)
        o_ref[...] = acc

    return pl.pallas_call(
        body, name=name, grid=(r // tm,), in_specs=[pl.BlockSpec((n, tm, cdim), lambda i: (0, i, 0))],
        out_specs=pl.BlockSpec((tm, cdim), lambda i: (i, 0)), out_shape=jax.ShapeDtypeStruct((r, cdim), F32),
        compiler_params=_cp("parallel"),
    )(a)


def _silu_rows(v, name):
    def body(v_ref, o_ref):
        o_ref[...] = _silu(v_ref[...])

    return pl.pallas_call(body, name=name, out_shape=jax.ShapeDtypeStruct(v.shape, F32))(v)


PACK_COLS = 1024


def _adamw(w, gs, m, v, name):
    r, cdim = w.shape
    tm = _row_tile(r, (1 << 18) // cdim, 8)
    c1 = 1.0 / (1.0 - ADAM_B1 ** ADAM_STEP)
    c2 = 1.0 / (1.0 - ADAM_B2 ** ADAM_STEP)
    ng = len(gs)

    def body(*refs):
        w_ref, g_refs, (m_ref, v_ref, g_ref, d_ref, nm_ref, nv_ref) = refs[0], refs[1:1 + ng], refs[1 + ng:]
        g = g_refs[0][...]
        for t in g_refs[1:]:
            g = g + t[...]
        mn = ADAM_B1 * m_ref[...] + (1.0 - ADAM_B1) * g
        vn = ADAM_B2 * v_ref[...] + (1.0 - ADAM_B2) * (g * g)
        g_ref[...] = g
        nm_ref[...] = mn
        nv_ref[...] = vn
        d_ref[...] = -ADAM_LR * ((mn * c1) / (jnp.sqrt(vn * c2) + ADAM_EPS) + ADAM_WD * w_ref[...])

    spec = pl.BlockSpec((tm, cdim), lambda i: (i, 0))
    return pl.pallas_call(
        body, name=name, grid=(r // tm,), in_specs=[spec] * (3 + ng), out_specs=[spec] * 4,
        out_shape=[jax.ShapeDtypeStruct(w.shape, F32)] * 4, compiler_params=_cp("parallel"),
    )(w, *gs, m, v)


def _rows(a):
    f = a.reshape(-1)
    pad = (-f.shape[0]) % PACK_COLS
    if pad:
        f = jnp.pad(f, (0, pad))
    return f.reshape(-1, PACK_COLS)


def _nrows(shape):
    return -(-int(np.prod(shape)) // PACK_COLS)


def _pack(parts, total_rows=None):
    p = jnp.concatenate([_rows(a) for a in parts], axis=0)
    if total_rows is not None and total_rows > p.shape[0]:
        p = jnp.pad(p, ((0, total_rows - p.shape[0]), (0, 0)))
    return p


def _unpack(p, shapes):
    out, r0 = [], 0
    for shp in shapes:
        n = int(np.prod(shp))
        nr = _nrows(shp)
        out.append(p[r0:r0 + nr].reshape(-1)[:n].reshape(shp))
        r0 += nr
    return out


def _unshard_cols(g):
    return jnp.concatenate([g[k] for k in range(N_SHARD)], axis=-1)


def _shard_cols(a):
    n = a.shape[-1] // N_SHARD
    return jnp.stack([a[..., k * n:(k + 1) * n] for k in range(N_SHARD)])


def _unshard_rows(g):
    return jnp.concatenate([g[k] for k in range(N_SHARD)], axis=1)


def _shard_rows(a):
    n = a.shape[1] // N_SHARD
    return jnp.stack([a[:, k * n:(k + 1) * n] for k in range(N_SHARD)])


W_NAMES = ("ada_w", "ada_b", "ln_g", "ln_b", "a_w_in", "a_w_out", "b_w_in", "b_conv_w", "b_conv_b", "b_dt_bias", "b_a_log", "b_d",
           "b_norm_w", "b_w_out")
BIG = ("a_w_in", "a_w_out", "b_w_in", "b_w_out")
SMALL = ("ada_b", "ln_g", "ln_b", "b_conv_w", "b_conv_b", "b_dt_bias", "b_a_log", "b_d", "b_norm_w")


def kernel(x, c, ada_w, ada_b, ln_g, ln_b, a_w_in, a_w_out, b_w_in, b_conv_w, b_conv_b, b_dt_bias, b_a_log, b_d, b_norm_w, b_w_out, loss_target, m_ada_w, m_ada_b, m_ln_g, m_ln_b, m_a_w_in, m_a_w_out, m_b_w_in, m_b_conv_w, m_b_conv_b, m_b_dt_bias, m_b_a_log, m_b_d, m_b_norm_w, m_b_w_out, v_ada_w, v_ada_b, v_ln_g, v_ln_b, v_a_w_in, v_a_w_out, v_b_w_in, v_b_conv_w, v_b_conv_b, v_b_dt_bias, v_b_a_log, v_b_d, v_b_norm_w, v_b_w_out):
    w = dict(ada_w=ada_w, ada_b=ada_b, ln_g=ln_g, ln_b=ln_b, a_w_in=a_w_in, a_w_out=a_w_out, b_w_in=b_w_in, b_conv_w=b_conv_w,
             b_conv_b=b_conv_b, b_dt_bias=b_dt_bias, b_a_log=b_a_log, b_d=b_d, b_norm_w=b_norm_w, b_w_out=b_w_out)
    mom = dict(ada_w=m_ada_w, ada_b=m_ada_b, ln_g=m_ln_g, ln_b=m_ln_b, a_w_in=m_a_w_in, a_w_out=m_a_w_out, b_w_in=m_b_w_in,
               b_conv_w=m_b_conv_w, b_conv_b=m_b_conv_b, b_dt_bias=m_b_dt_bias, b_a_log=m_b_a_log, b_d=m_b_d, b_norm_w=m_b_norm_w,
               b_w_out=m_b_w_out)
    var = dict(ada_w=v_ada_w, ada_b=v_ada_b, ln_g=v_ln_g, ln_b=v_ln_b, a_w_in=v_a_w_in, a_w_out=v_a_w_out, b_w_in=v_b_w_in,
               b_conv_w=v_b_conv_w, b_conv_b=v_b_conv_b, b_dt_bias=v_b_dt_bias, b_a_log=v_b_a_log, b_d=v_b_d, b_norm_w=v_b_norm_w,
               b_w_out=v_b_w_out)
    ax, ay, ac = lax.axis_index("x"), lax.axis_index("y"), lax.axis_index("c")
    me = 4 * ax + 2 * ay + ac
    shard = 2 * ax + ay
    d = x.shape[-1]
    dsh = ada_w.shape[-1]

    small_in = (c, b_conv_w, b_conv_b, b_norm_w)
    g0 = _all_gather8(_pack(small_in).reshape(-1, LANE), "gather_small_in").reshape(N_DEV, -1, PACK_COLS)
    per_dev = [_unpack(g0[k], [a.shape for a in small_in]) for k in range(N_DEV)]
    c_all = jnp.concatenate([p[0] for p in per_dev], axis=0)
    conv_w_full, conv_b_full, norm_w_full = (_unshard_cols([per_dev[2 * k][t] for k in range(N_SHARD)]) for t in (1, 2, 3))

    cond = _silu_rows(jnp.pad(c_all, ((0, 8), (0, 0))), "cond")
    bias = lax.dynamic_slice_in_dim(ada_b, shard * dsh, dsh, axis=1)
    part = jnp.stack([_mm(cond, ada_w[i], add=jnp.broadcast_to(bias[i], (16, dsh)), name=f"mod{i}")[:N_DEV] for i in range(DEPTH)])
    g1 = _all_gather8(part.reshape(-1, LANE), "gather_mod").reshape(N_DEV, DEPTH, N_DEV, dsh)
    mod_all = _unshard_cols([g1[2 * k] for k in range(N_SHARD)])
    mod = lax.dynamic_index_in_dim(mod_all, me, axis=1, keepdims=False)

    gw = _gather_shards(_pack([w[n] for n in BIG]).astype(BF16), "gather_weights")
    big_sh = [_unpack(gw[k], [w[n].shape for n in BIG]) for k in range(N_SHARD)]
    full = dict(
        ln_g=ln_g, ln_b=ln_b, b_dt_bias=b_dt_bias, b_a_log=b_a_log, b_d=b_d,
        b_conv_w=conv_w_full, b_conv_b=conv_b_full, b_norm_w=norm_w_full,
        a_w_in=_unshard_cols([s[0] for s in big_sh]), a_w_out=_unshard_rows([s[1] for s in big_sh]),
        b_w_in=_unshard_cols([s[2] for s in big_sh]), b_w_out=_unshard_rows([s[3] for s in big_sh]),
    )

    loss, grad_x, dmod, g, big, (arrived_b0, arrived1) = _local_step(x[0], loss_target[0], mod, full, early=True)

    arrived0 = list(_transpose_shards(_shards_of(big, 0)[:2], "scatter_grads")) + arrived_b0
    mine = [jnp.concatenate([_sum_slots(arrived0[a], f"sum0_{n}"), _sum_slots(arrived1[a], f"sum1_{n}")], axis=0)
            for a, n in enumerate(BIG)]
    theirs = _swap_sibling(mine, "swap_grads")

    small_g = (dmod, g["ln_g"], g["ln_b"], g["b_dt_bias"], g["b_a_log"], g["b_d"], g["b_conv_w"], g["b_conv_b"], g["b_norm_w"],
               loss.reshape(1))
    g2 = _all_gather8(_pack(small_g).reshape(-1, LANE), "gather_small_grads")
    tot = _unpack(_sum_slots(g2, "sum_small").reshape(-1, PACK_COLS), [a.shape for a in small_g])
    g_ada_b, g_ln_g, g_ln_b, g_dt_bias, g_a_log, g_d, g_conv_w, g_conv_b, g_norm_w, loss_sum = tot
    dmod_all = g2.reshape(N_DEV, -1)[:, :dmod.size].reshape(N_DEV, DEPTH, 3 * d)
    dmod_mine = lax.dynamic_slice_in_dim(dmod_all, shard * dsh, dsh, axis=2)
    g_ada_w = jnp.stack([_mm(cond, jnp.pad(dmod_mine[:, i], ((0, 8), (0, 0))), ta=True, name=f"dada{i}") for i in range(DEPTH)])
    csh = g_conv_w.shape[-1] // N_SHARD
    nsh = g_norm_w.shape[-1] // N_SHARD
    small_grads = dict(
        ada_w=g_ada_w, ada_b=g_ada_b, ln_g=g_ln_g, ln_b=g_ln_b, b_dt_bias=g_dt_bias, b_a_log=g_a_log, b_d=g_d,
        b_conv_w=lax.dynamic_slice_in_dim(g_conv_w, shard * csh, csh, axis=2),
        b_conv_b=lax.dynamic_slice_in_dim(g_conv_b, shard * csh, csh, axis=1),
        b_norm_w=lax.dynamic_slice_in_dim(g_norm_w, shard * nsh, nsh, axis=1),
    )

    by_name = [{}, {}, {}, {}]

    def update(n, gs):
        two_d = lambda t: t.reshape(-1, t.shape[-1])
        outs = _adamw(two_d(w[n]), [two_d(t) for t in gs], two_d(mom[n]), two_d(var[n]), f"adamw_{n}")
        for t, o in zip(by_name, outs):
            t[n] = o.reshape(w[n].shape)

    for i, n in enumerate(BIG):
        update(n, [mine[i], theirs[i]])
    update("ada_w", [small_grads["ada_w"]])
    rest = SMALL
    rows = -(-sum(_nrows(w[n].shape) for n in rest) // 8) * 8
    packed = _adamw(_pack([w[n] for n in rest], rows), [_pack([small_grads[n] for n in rest], rows)],
                    _pack([mom[n] for n in rest], rows), _pack([var[n] for n in rest], rows), "adamw_small")
    for t, p in zip(by_name, packed):
        t.update(zip(rest, _unpack(p, [w[n].shape for n in rest])))
    return (loss_sum.reshape(()), grad_x[None], *[t[n] for t in by_name for n in W_NAMES])
```

```python
import jax
import jax.numpy as jnp
import numpy as np
from jax import lax
from jax.experimental import pallas as pl
from jax.experimental.pallas import tpu as pltpu

F32 = jnp.float32
BF16 = jnp.bfloat16

DEPTH = 4
A_HEADS = 16
A_HEAD_DIM = 64
A_WIDTH = A_HEADS * A_HEAD_DIM
DILATIONS = (1, 4, 16)
A_RADIUS = 64
A_QBLOCK = 128
SSM_HEADS = 32
SSM_HEAD_DIM = 64
SSM_STATE = 128
SSM_GROUPS = 4
SSM_REP = SSM_HEADS // SSM_GROUPS
SSM_CONV = 5
SSM_CHUNK = 128
DEEPNORM_ALPHA = (2 * DEPTH) ** 0.25
LN_EPS = 1e-5
RMS_EPS = 1e-5
ADAM_LR, ADAM_B1, ADAM_B2, ADAM_EPS, ADAM_WD, ADAM_STEP = 0.001, 0.9, 0.999, 1e-08, 0.01, 10
VMEM_LIMIT = 56 * 1024 * 1024
LANE = 128


def _cp(*sem):
    return pltpu.CompilerParams(dimension_semantics=sem, vmem_limit_bytes=VMEM_LIMIT)


def _tile(dim, target):
    if dim <= target:
        return dim
    t = (target // LANE) * LANE
    while dim % t:
        t -= LANE
    return t


def _sigmoid(x):
    return 1.0 / (1.0 + jnp.exp(-x))


def _silu(x):
    return x * _sigmoid(x)


def _dsilu(x):
    s = _sigmoid(x)
    return s * (1.0 + x * (1.0 - s))


def _split3(x):
    a = x.astype(BF16)
    r = x - a.astype(F32)
    b = r.astype(BF16)
    c = (r - b.astype(F32)).astype(BF16)
    return a, b, c


def _dot(a, b, ca=1, cb=0):
    return lax.dot_general(a, b, (((ca,), (cb,)), ((), ())), preferred_element_type=F32)


def _dot_exact(m01, x):
    a, b, c = _split3(x)
    return _dot(m01, a) + _dot(m01, b) + _dot(m01, c)


def _mm(a, b, *, ta=False, tb=False, add=None, out_dtype=F32, name, tm=1024, tn=1024, tk=2048):
    m, k = (a.shape[1], a.shape[0]) if ta else a.shape
    n = b.shape[0] if tb else b.shape[1]
    assert (b.shape[1] if tb else b.shape[0]) == k
    tm, tn, tk = _tile(m, tm), _tile(n, tn), _tile(k, tk)
    nk = k // tk
    has_add = add is not None

    def body(*refs):
        if has_add:
            a_ref, b_ref, c_ref, o_ref, acc = refs
        else:
            a_ref, b_ref, o_ref, acc = refs
        kk = pl.program_id(2)
        part = _dot(a_ref[...].astype(BF16), b_ref[...].astype(BF16), 0 if ta else 1, 1 if tb else 0)

        def finish(r):
            if has_add:
                r = r + c_ref[...]
            o_ref[...] = r.astype(o_ref.dtype)

        if nk == 1:
            finish(part)
            return

        @pl.when(kk == 0)
        def _():
            acc[...] = part

        @pl.when((kk > 0) & (kk < nk - 1))
        def _():
            acc[...] += part

        @pl.when(kk == nk - 1)
        def _():
            finish(acc[...] + part)

    a_spec = pl.BlockSpec((tk, tm), lambda i, j, kk: (kk, i)) if ta else pl.BlockSpec((tm, tk), lambda i, j, kk: (i, kk))
    b_spec = pl.BlockSpec((tn, tk), lambda i, j, kk: (j, kk)) if tb else pl.BlockSpec((tk, tn), lambda i, j, kk: (kk, j))
    in_specs = [a_spec, b_spec]
    args = [a, b]
    if has_add:
        in_specs.append(pl.BlockSpec((tm, tn), lambda i, j, kk: (i, j)))
        args.append(add)
    return pl.pallas_call(
        body, name=name, grid=(m // tm, n // tn, nk), in_specs=in_specs,
        out_specs=pl.BlockSpec((tm, tn), lambda i, j, kk: (i, j)),
        out_shape=jax.ShapeDtypeStruct((m, n), out_dtype),
        scratch_shapes=[pltpu.VMEM((tm, tn) if nk > 1 else (8, LANE), F32)],
        compiler_params=_cp("parallel", "parallel", "arbitrary"),
    )(*args)


ROWS = 512


def _row_spec(tm, d):
    return pl.BlockSpec((tm, d), lambda i: (i, 0))


def _vec_spec(d, rows=1):
    return pl.BlockSpec((rows, d), lambda i: (0, 0))


def _modulate(x, scale, shift, name):
    s, d = x.shape
    tm = min(ROWS, s)

    def body(x_ref, sc_ref, sh_ref, o_ref):
        o_ref[...] = (x_ref[...] * (1.0 + sc_ref[...]) + sh_ref[...]).astype(BF16)

    return pl.pallas_call(
        body, name=name, grid=(s // tm,), in_specs=[_row_spec(tm, d), _vec_spec(d), _vec_spec(d)],
        out_specs=_row_spec(tm, d), out_shape=jax.ShapeDtypeStruct((s, d), BF16), compiler_params=_cp("parallel"),
    )(x, scale, shift)


def _resid_ln(x, y, gate, g, b, name, nxt=None):
    s, d = x.shape
    tm = min(ROWS, s)

    def body(x_ref, y_ref, gt_ref, g_ref, b_ref, *rest):
        u = DEEPNORM_ALPHA * x_ref[...] + gt_ref[...] * y_ref[...]
        mu = jnp.mean(u, axis=1, keepdims=True)
        uc = u - mu
        var = jnp.mean(uc * uc, axis=1, keepdims=True)
        xn = uc * lax.rsqrt(var + LN_EPS) * g_ref[...] + b_ref[...]
        if nxt is None:
            rest[0][...] = xn
        else:
            sc_ref, sh_ref, o_ref, h_ref = rest
            o_ref[...] = xn
            h_ref[...] = (xn * (1.0 + sc_ref[...]) + sh_ref[...]).astype(BF16)

    rs, vs = _row_spec(tm, d), _vec_spec(d)
    if nxt is None:
        return pl.pallas_call(
            body, name=name, grid=(s // tm,), in_specs=[rs, rs, vs, vs, vs],
            out_specs=rs, out_shape=jax.ShapeDtypeStruct((s, d), F32), compiler_params=_cp("parallel"),
        )(x, y, gate, g, b)
    return pl.pallas_call(
        body, name=name, grid=(s // tm,), in_specs=[rs, rs, vs, vs, vs, vs, vs], out_specs=[rs, rs],
        out_shape=[jax.ShapeDtypeStruct((s, d), F32), jax.ShapeDtypeStruct((s, d), BF16)], compiler_params=_cp("parallel"),
    )(x, y, gate, g, b, *nxt)


def _resid_ln_bwd(x, y, dxn, gate, g, name):
    s, d = x.shape
    tm = min(ROWS, s)

    def body(x_ref, y_ref, dxn_ref, gt_ref, g_ref, du_ref, dy_ref, red_ref):
        @pl.when(pl.program_id(0) == 0)
        def _():
            red_ref[...] = jnp.zeros_like(red_ref)

        yv = y_ref[...]
        u = DEEPNORM_ALPHA * x_ref[...] + gt_ref[...] * yv
        mu = jnp.mean(u, axis=1, keepdims=True)
        uc = u - mu
        var = jnp.mean(uc * uc, axis=1, keepdims=True)
        rstd = lax.rsqrt(var + LN_EPS)
        xhat = uc * rstd
        dxnv = dxn_ref[...]
        dxh = dxnv * g_ref[...]
        du = rstd * (dxh - jnp.mean(dxh, axis=1, keepdims=True) - xhat * jnp.mean(dxh * xhat, axis=1, keepdims=True))
        du_ref[...] = du
        dy_ref[...] = (du * gt_ref[...]).astype(BF16)
        red_ref[0:1, :] += jnp.sum(du * yv, axis=0, keepdims=True)
        red_ref[1:2, :] += jnp.sum(dxnv * xhat, axis=0, keepdims=True)
        red_ref[2:3, :] += jnp.sum(dxnv, axis=0, keepdims=True)

    return pl.pallas_call(
        body, name=name, grid=(s // tm,),
        in_specs=[_row_spec(tm, d), _row_spec(tm, d), _row_spec(tm, d), _vec_spec(d), _vec_spec(d)],
        out_specs=[_row_spec(tm, d), _row_spec(tm, d), _vec_spec(d, 8)],
        out_shape=[jax.ShapeDtypeStruct((s, d), F32), jax.ShapeDtypeStruct((s, d), BF16), jax.ShapeDtypeStruct((8, d), F32)],
        compiler_params=_cp("arbitrary"),
    )(x, y, dxn, gate, g)


def _modulate_bwd(du, dhs, x, scale, name):
    s, d = x.shape
    tm = min(ROWS, s)
    n = len(dhs)

    def body(*refs):
        du_ref, dh_refs, (x_ref, sc_ref, dx_ref, red_ref) = refs[0], refs[1:1 + n], refs[1 + n:]

        @pl.when(pl.program_id(0) == 0)
        def _():
            red_ref[...] = jnp.zeros_like(red_ref)

        dhv = dh_refs[0][...]
        for t in dh_refs[1:]:
            dhv = dhv + t[...]
        dx_ref[...] = DEEPNORM_ALPHA * du_ref[...] + dhv * (1.0 + sc_ref[...])
        red_ref[0:1, :] += jnp.sum(dhv * x_ref[...], axis=0, keepdims=True)
        red_ref[1:2, :] += jnp.sum(dhv, axis=0, keepdims=True)

    return pl.pallas_call(
        body, name=name, grid=(s // tm,),
        in_specs=[_row_spec(tm, d)] * (n + 2) + [_vec_spec(d)],
        out_specs=[_row_spec(tm, d), _vec_spec(d, 8)],
        out_shape=[jax.ShapeDtypeStruct((s, d), F32), jax.ShapeDtypeStruct((8, d), F32)],
        compiler_params=_cp("arbitrary"),
    )(du, *dhs, x, scale)


def _loss_grad(xf, tgt, name):
    s, d = xf.shape
    tm = min(ROWS, s)

    def body(x_ref, t_ref, dx_ref, red_ref):
        @pl.when(pl.program_id(0) == 0)
        def _():
            red_ref[...] = jnp.zeros_like(red_ref)

        e = x_ref[...] - t_ref[...]
        dx_ref[...] = e * (1.0 / d)
        red_ref[0:1, :] += jnp.sum(e * e, axis=0, keepdims=True)

    return pl.pallas_call(
        body, name=name, grid=(s // tm,), in_specs=[_row_spec(tm, d), _row_spec(tm, d)],
        out_specs=[_row_spec(tm, d), _vec_spec(d, 8)],
        out_shape=[jax.ShapeDtypeStruct((s, d), F32), jax.ShapeDtypeStruct((8, d), F32)],
        compiler_params=_cp("arbitrary"),
    )(xf, tgt)


QKV_COLS = 3 * 3 * A_WIDTH


SLOPES = tuple(float(2.0 ** (-8.0 * (h + 1.0) / A_HEADS)) for h in range(A_HEADS))
FAR = 1e30
HEAD_COLS = tuple(slice(h * A_HEAD_DIM, (h + 1) * A_HEAD_DIM) for h in range(A_HEADS))


def _band_dist(n, length, dil, span_rows):
    shape = (2 * A_QBLOCK, A_QBLOCK) if span_rows else (A_QBLOCK, 2 * A_QBLOCK)
    r = lax.broadcasted_iota(jnp.int32, shape, 0)
    c = lax.broadcasted_iota(jnp.int32, shape, 1)
    sp, ce = (r, c) if span_rows else (c, r)
    delta = sp - A_RADIUS - ce
    pos = n * A_QBLOCK - A_RADIUS + sp
    valid = (jnp.abs(delta) <= A_RADIUS) & (pos >= 0) & (pos < length)
    return jnp.where(valid, jnp.abs(delta).astype(F32) * float(dil), FAR)


def _span_specs(col, nb64):
    def mk(i):
        return pl.BlockSpec((64, A_WIDTH), lambda r, n: (r * nb64 + jnp.clip(2 * n - 1 + i, 0, nb64 - 1), col))
    return [mk(i) for i in range(4)]


def _to_residue(t, dil):
    if dil == 1:
        return t
    s, c = t.shape
    return t.reshape(s // dil, dil, c).transpose(1, 0, 2).reshape(s, c)


def _from_residue(t, dil):
    if dil == 1:
        return t
    s, c = t.shape
    return t.reshape(dil, s // dil, c).transpose(1, 0, 2).reshape(s, c)


def _cat(refs):
    return jnp.concatenate([t[...] for t in refs], axis=0)


def _head_expander():
    r = lax.broadcasted_iota(jnp.int32, (A_HEADS, A_WIDTH), 0)
    c = lax.broadcasted_iota(jnp.int32, (A_HEADS, A_WIDTH), 1)
    return ((c >= r * A_HEAD_DIM) & (c < (r + 1) * A_HEAD_DIM)).astype(BF16)


def _to_lanes(x16, e):
    a, b, c = _split3(x16)
    return _dot(a, e) + _dot(b, e) + _dot(c, e)


def _per_head_sum(x, e):
    a, b, c = _split3(x)
    return _dot(a, e, 1, 1) + _dot(b, e, 1, 1) + _dot(c, e, 1, 1)


def _pair_low_lanes():
    return lax.broadcasted_iota(jnp.int32, (A_QBLOCK, LANE), 1) < A_HEAD_DIM


def _top_rows():
    return lax.broadcasted_iota(jnp.int32, (2 * A_QBLOCK, 1), 0) < A_QBLOCK


def _block_diag(v, low):
    zero = jnp.zeros_like(v)
    return jnp.concatenate([jnp.where(low, v, zero), jnp.where(low, zero, v)], axis=0)


def _attn_fwd(qkv, g, name):
    s = qkv.shape[0]
    dil = DILATIONS[g]
    length = s // dil
    nblk = length // A_QBLOCK

    def body(q_ref, k0, k1, k2, k3, v0, v1, v2, v3, o_ref, l_ref):
        dist = _band_dist(pl.program_id(1), length, dil, False)
        kk = _cat((k0, k1, k2, k3))
        vv = _cat((v0, v1, v2, v3))
        low = _pair_low_lanes()
        top = _top_rows()
        dist2 = jnp.concatenate([dist, dist], axis=0)
        for hp in range(A_HEADS // 2):
            ls = slice(hp * LANE, (hp + 1) * LANE)
            qp, kp, vp = q_ref[:, ls], kk[:, ls], vv[:, ls]
            sc = _dot(_block_diag(qp, low), kp, 1, 1) * 0.125 - jnp.where(top, SLOPES[2 * hp], SLOPES[2 * hp + 1]) * dist2
            m = jnp.max(sc, axis=1, keepdims=True)
            p = jnp.exp(sc - m)
            z = jnp.sum(p, axis=1, keepdims=True)
            o2 = _dot(p.astype(BF16), vp) / z
            lse2 = m + jnp.log(z)
            l_ref[:, 2 * hp:2 * hp + 1] = lse2[:A_QBLOCK]
            l_ref[:, 2 * hp + 1:2 * hp + 2] = lse2[A_QBLOCK:]
            o_ref[:, ls] = jnp.where(low, o2[:A_QBLOCK], o2[A_QBLOCK:])

    qspec = pl.BlockSpec((A_QBLOCK, A_WIDTH), lambda r, n: (r * nblk + n, 0))
    lspec = pl.BlockSpec((A_QBLOCK, A_HEADS), lambda r, n: (r * nblk + n, 0))
    return pl.pallas_call(
        body, name=name, grid=(dil, nblk), in_specs=[qspec] + _span_specs(1, 2 * nblk) + _span_specs(2, 2 * nblk),
        out_specs=[qspec, lspec],
        out_shape=[jax.ShapeDtypeStruct((s, A_WIDTH), F32), jax.ShapeDtypeStruct((s, A_HEADS), F32)],
        compiler_params=_cp("parallel", "parallel"),
    )(*([qkv] * 9))


def _attn_merge(os_, ls_, gate, name):
    s, w = gate.shape
    tm = min(ROWS, s)

    def body(o0, o1, o2, l0, l1, l2, g_ref, y_ref, o_ref, l_ref):
        a, b, c = l0[...], l1[...], l2[...]
        m = jnp.maximum(jnp.maximum(a, b), c)
        ea, eb, ec = jnp.exp(a - m), jnp.exp(b - m), jnp.exp(c - m)
        z = ea + eb + ec
        l_ref[...] = m + jnp.log(z)
        e = _head_expander()
        o = _to_lanes(ea / z, e) * o0[...] + _to_lanes(eb / z, e) * o1[...] + _to_lanes(ec / z, e) * o2[...]
        o_ref[...] = o
        y_ref[...] = (o * _silu(g_ref[...])).astype(BF16)

    rs = _row_spec(tm, w)
    ls = _row_spec(tm, A_HEADS)
    return pl.pallas_call(
        body, name=name, grid=(s // tm,), in_specs=[rs] * 3 + [ls] * 3 + [rs], out_specs=[rs, rs, ls],
        out_shape=[jax.ShapeDtypeStruct((s, w), BF16), jax.ShapeDtypeStruct((s, w), F32), jax.ShapeDtypeStruct((s, A_HEADS), F32)],
        compiler_params=_cp("parallel"),
    )(*os_, *ls_, gate)


def _attn_gate_bwd(dyy, o, gate, name):
    s, w = gate.shape
    tm = min(ROWS, s)

    def body(dy_ref, o_ref, g_ref, do_ref, dg_ref, dl_ref):
        dyv, ov, gv = dy_ref[...], o_ref[...], g_ref[...]
        do = dyv * _silu(gv)
        do_ref[...] = do.astype(BF16)
        dg_ref[...] = (dyv * ov * _dsilu(gv)).astype(BF16)
        dl_ref[...] = _per_head_sum(do * ov, _head_expander())

    rs = _row_spec(tm, w)
    return pl.pallas_call(
        body, name=name, grid=(s // tm,), in_specs=[rs] * 3, out_specs=[rs, rs, _row_spec(tm, A_HEADS)],
        out_shape=[jax.ShapeDtypeStruct((s, w), BF16), jax.ShapeDtypeStruct((s, w), BF16), jax.ShapeDtypeStruct((s, A_HEADS), F32)],
        compiler_params=_cp("parallel"),
    )(dyy, o, gate)


def _ride_along(ex_in, ex_out, sems, first, last):
    @pl.when(first)
    def _():
        for cp in _shard_exchange(ex_in, ex_out, *sems):
            cp.start()

    @pl.when(last)
    def _():
        for cp in _shard_exchange(ex_in, ex_out, *sems):
            cp.wait()


def _ride_along_specs(exch):
    n = len(exch)
    n_rem = (N_SHARD - 1) * n
    sems = [pltpu.SemaphoreType.DMA((n_rem,)), pltpu.SemaphoreType.DMA((n_rem,)), pltpu.SemaphoreType.DMA((n,))] if n else []
    return [ANY] * n, [jax.ShapeDtypeStruct(t.shape, t.dtype) for t in exch], sems


def _attn_bwd(qkv, do, lse, delta, g, name, exch=()):
    s = qkv.shape[0]
    dil = DILATIONS[g]
    length = s // dil
    nblk = length // A_QBLOCK
    n_ex = len(exch)

    def rows(t16):
        return jnp.pad(t16.reshape(dil, length, A_HEADS).transpose(0, 2, 1), ((0, 0), (0, 0), (A_RADIUS, A_RADIUS)))

    def body(*refs):
        (q0, q1, q2, q3, k0, k1, k2, k3, v0, v1, v2, v3, d0, d1, d2, d3, lc_ref, ec_ref, la, lb, ea, eb), refs = refs[:22], refs[22:]
        o_ref = refs[n_ex]
        if n_ex:
            r, n = pl.program_id(0), pl.program_id(1)
            _ride_along(refs[:n_ex], refs[n_ex + 1:2 * n_ex + 1], refs[2 * n_ex + 1:], (r == 0) & (n == 0), (r == dil - 1) & (n == nblk - 1))
        dist = _band_dist(pl.program_id(1), length, dil, False)
        qq, kk, vv, dd = _cat((q0, q1, q2, q3)), _cat((k0, k1, k2, k3)), _cat((v0, v1, v2, v3)), _cat((d0, d1, d2, d3))
        lse_r = jnp.concatenate([la[...], lb[...]], axis=1)
        dlt_r = jnp.concatenate([ea[...], eb[...]], axis=1)
        low = _pair_low_lanes()
        top = _top_rows()
        dist2 = jnp.concatenate([dist, dist], axis=0)
        centre = slice(A_RADIUS, A_RADIUS + A_QBLOCK)
        for hp in range(A_HEADS // 2):
            ls = slice(hp * LANE, (hp + 1) * LANE)
            qs, ks, vs, ds_ = qq[:, ls], kk[:, ls], vv[:, ls], dd[:, ls]
            qn, kn, vn, dn = qs[centre], ks[centre], vs[centre], ds_[centre]
            h0, h1 = 2 * hp, 2 * hp + 1
            bias = jnp.where(top, SLOPES[h0], SLOPES[h1]) * dist2
            lc = jnp.concatenate([lc_ref[:, h0:h0 + 1], lc_ref[:, h1:h1 + 1]], axis=0)
            ec = jnp.concatenate([ec_ref[:, h0:h0 + 1], ec_ref[:, h1:h1 + 1]], axis=0)
            lr = jnp.where(top, lse_r[h0:h0 + 1, :], lse_r[h1:h1 + 1, :])
            er = jnp.where(top, dlt_r[h0:h0 + 1, :], dlt_r[h1:h1 + 1, :])
            p = jnp.exp(_dot(_block_diag(qn, low), ks, 1, 1) * 0.125 - bias - lc)
            dsc = p * (_dot(_block_diag(dn, low), vs, 1, 1) - ec)
            dq = _dot(dsc.astype(BF16), ks)
            pt = jnp.exp(_dot(_block_diag(kn, low), qs, 1, 1) * 0.125 - bias - lr)
            dst = pt * (_dot(_block_diag(vn, low), ds_, 1, 1) - er)
            dk = _dot(dst.astype(BF16), qs)
            dv = _dot(pt.astype(BF16), ds_)
            merge = lambda t: jnp.where(low, t[:A_QBLOCK], t[A_QBLOCK:])
            o_ref[:, ls] = (merge(dq) * 0.125).astype(BF16)
            o_ref[:, A_WIDTH + hp * LANE:A_WIDTH + (hp + 1) * LANE] = (merge(dk) * 0.125).astype(BF16)
            o_ref[:, 2 * A_WIDTH + hp * LANE:2 * A_WIDTH + (hp + 1) * LANE] = merge(dv).astype(BF16)

    nb64 = 2 * nblk
    dspecs = _span_specs(0, nb64)
    cspec = pl.BlockSpec((A_QBLOCK, A_HEADS), lambda r, n: (r * nblk + n, 0))
    rspecs = [pl.BlockSpec((None, A_HEADS, A_QBLOCK), lambda r, n: (r, 0, n)), pl.BlockSpec((None, A_HEADS, A_QBLOCK), lambda r, n: (r, 0, n + 1))]
    lse_r, dlt_r = rows(lse), rows(delta)
    ex_specs, ex_shapes, sems = _ride_along_specs(exch)
    outs = pl.pallas_call(
        body, name=name, grid=(dil, nblk),
        in_specs=_span_specs(0, nb64) + _span_specs(1, nb64) + _span_specs(2, nb64) + dspecs + [cspec, cspec] + rspecs * 2 + ex_specs,
        out_specs=[pl.BlockSpec((A_QBLOCK, 3 * A_WIDTH), lambda r, n: (r * nblk + n, 0))] + ex_specs,
        out_shape=[jax.ShapeDtypeStruct((s, 3 * A_WIDTH), BF16)] + ex_shapes,
        scratch_shapes=sems,
        compiler_params=_cp(*(("arbitrary", "arbitrary") if n_ex else ("parallel", "parallel"))),
    )(*([qkv] * 12), *([do] * 4), lse, delta, lse_r, lse_r, dlt_r, dlt_r, *exch)
    return (outs[0], outs[1:]) if n_ex else outs[0]


def _attn_layer_fwd(h, w_qkv, w_gate, w_out, li):
    nm = lambda t: f"a{li}_{t}"
    gate = _mm(h, w_gate, name=nm("gate"))
    hs, qkvs, os_, ls_ = [], [], [], []
    for g, dil in enumerate(DILATIONS):
        hg = _to_residue(h, dil)
        qkv = _mm(hg, w_qkv[:, g * 3 * A_WIDTH:(g + 1) * 3 * A_WIDTH], out_dtype=BF16, name=nm(f"qkv{g}"))
        o, l = _attn_fwd(qkv, g, nm(f"attn{g}"))
        hs.append(hg)
        qkvs.append(qkv)
        os_.append(_from_residue(o, dil))
        ls_.append(_from_residue(l, dil))
    y, o, lse = _attn_merge(os_, ls_, gate, nm("merge"))
    out = _mm(y, w_out, name=nm("out"))
    return out, (hs, qkvs, gate, y, o, lse)


def _attn_layer_bwd(dy, h, saved, w_qkv, w_gate, w_out, li, exch=()):
    nm = lambda t: f"a{li}_{t}"
    hs, qkvs, gate, y, o, lse = saved
    g_w_out = _mm(y, dy, ta=True, out_dtype=BF16, name=nm("dwout"))
    dyy = _mm(dy, w_out, tb=True, name=nm("dyy"))
    do, dgate, delta = _attn_gate_bwd(dyy, o, gate, nm("gatebwd"))
    dhs, dws, arrived = [], [], ()
    for g, dil in enumerate(DILATIONS):
        dqkv = _attn_bwd(qkvs[g], _to_residue(do, dil), _to_residue(lse, dil), _to_residue(delta, dil), g, nm(f"attnbwd{g}"),
                         exch if g == 0 else ())
        if g == 0 and exch:
            dqkv, arrived = dqkv
        wg = w_qkv[:, g * 3 * A_WIDTH:(g + 1) * 3 * A_WIDTH]
        dws.append(_mm(hs[g], dqkv, ta=True, out_dtype=BF16, name=nm(f"dwqkv{g}")))
        add = _mm(dgate, w_gate, tb=True, name=nm("dh_gate")) if g == 0 else None
        dhs.append(_from_residue(_mm(dqkv, wg, tb=True, add=add, name=nm(f"dh_qkv{g}")), dil))
    g_w_in = jnp.concatenate(dws + [_mm(h, dgate, ta=True, out_dtype=BF16, name=nm("dwgate"))], axis=1)
    return dhs, g_w_in, g_w_out, arrived


SSM_INNER = SSM_HEADS * SSM_HEAD_DIM
SSM_BC = SSM_GROUPS * SSM_STATE
SSM_CONV_DIM = SSM_INNER + 2 * SSM_BC
GW = SSM_REP * SSM_HEAD_DIM
T = SSM_CHUNK
HALO = 8


def _conv_specs(tm, tn, s, col=lambda j: j):
    nb8 = s // HALO
    cur = pl.BlockSpec((tm, tn), lambda j, i: (i, col(j)))
    prev = pl.BlockSpec((HALO, tn), lambda j, i: (jnp.maximum(i * (tm // HALO) - 1, 0), col(j)))
    nxt = pl.BlockSpec((HALO, tn), lambda j, i: (jnp.minimum((i + 1) * (tm // HALO), nb8 - 1), col(j)))
    return [prev, cur, nxt]


def _extend(prev_ref, cur_ref, nxt_ref, i, nrow):
    p = jnp.where(i == 0, 0.0, prev_ref[...])
    n = jnp.where(i == nrow - 1, 0.0, nxt_ref[...])
    return jnp.concatenate([p, cur_ref[...], n], axis=0)


def _shift_rows(ext, off, tm):
    rows = ext.shape[0]
    return pltpu.roll(ext, (-off) % rows, 0)[HALO:HALO + tm]


def _conv_fwd(xraw, w, b, name):
    s, cdim = xraw.shape
    tm, tn = min(512, s), 1024
    nrow = s // tm

    def body(p_ref, c_ref, n_ref, w_ref, b_ref, pre_ref, act_ref):
        ext = _extend(p_ref, c_ref, n_ref, pl.program_id(1), nrow)
        acc = jnp.broadcast_to(b_ref[...], (tm, tn))
        for k in range(SSM_CONV):
            acc = acc + w_ref[k:k + 1, :] * _shift_rows(ext, k - SSM_CONV // 2, tm)
        pre_ref[...] = acc
        act_ref[...] = _silu(acc)

    prev, cur, nxt = _conv_specs(tm, tn, s)
    return pl.pallas_call(
        body, name=name, grid=(cdim // tn, nrow),
        in_specs=[prev, cur, nxt, pl.BlockSpec((SSM_CONV, tn), lambda j, i: (0, j)), pl.BlockSpec((1, tn), lambda j, i: (0, j))],
        out_specs=[cur, cur], out_shape=[jax.ShapeDtypeStruct((s, cdim), F32)] * 2,
        compiler_params=_cp("parallel", "parallel"),
    )(xraw, xraw, xraw, w, b)


def _conv_bwd(dx, db, dc, pre, xraw, w, name):
    s, cdim = xraw.shape
    tm, tn = min(512, s), 1024
    nrow = s // tm
    nx = dx.shape[1] // tn

    def body(xp, xc, xn, bp, bc, bn, cp, cc, cn, pp, pc, pn, x_ref, w_ref, o_ref, red_ref):
        j, i = pl.program_id(0), pl.program_id(1)

        @pl.when(i == 0)
        def _():
            red_ref[...] = jnp.zeros_like(red_ref)

        bcext = jnp.concatenate([_extend(bp, bc, bn, i, nrow), _extend(cp, cc, cn, i, nrow)], axis=1)
        dact = jnp.where(j < nx, _extend(xp, xc, xn, i, nrow), bcext)
        dpre = dact * _dsilu(_extend(pp, pc, pn, i, nrow))
        xv = x_ref[...]
        acc = jnp.zeros((tm, tn), F32)
        for k in range(SSM_CONV):
            sk = _shift_rows(dpre, SSM_CONV // 2 - k, tm)
            acc = acc + w_ref[k:k + 1, :] * sk
            red_ref[k:k + 1, :] += jnp.sum(sk * xv, axis=0, keepdims=True)
        red_ref[SSM_CONV:SSM_CONV + 1, :] += jnp.sum(dpre[HALO:HALO + tm], axis=0, keepdims=True)
        o_ref[...] = acc.astype(BF16)

    half = tn // 2
    cur = pl.BlockSpec((tm, tn), lambda j, i: (i, j))
    return pl.pallas_call(
        body, name=name, grid=(cdim // tn, nrow),
        in_specs=_conv_specs(tm, tn, s, lambda j: jnp.minimum(j, nx - 1)) + _conv_specs(tm, half, s, lambda j: 0) * 2
        + _conv_specs(tm, tn, s) + [cur, pl.BlockSpec((SSM_CONV, tn), lambda j, i: (0, j))],
        out_specs=[cur, pl.BlockSpec((8, tn), lambda j, i: (0, j))],
        out_shape=[jax.ShapeDtypeStruct((s, cdim), BF16), jax.ShapeDtypeStruct((8, cdim), F32)],
        compiler_params=_cp("parallel", "arbitrary"),
    )(dx, dx, dx, db, db, db, dc, dc, dc, pre, pre, pre, xraw, w)


def _tri(lower):
    r = lax.broadcasted_iota(jnp.int32, (T, T), 0)
    c = lax.broadcasted_iota(jnp.int32, (T, T), 1)
    return (r >= c) if lower else (r <= c)


def _softplus(x):
    return jnp.maximum(x, 0.0) + jnp.log(1.0 + jnp.exp(-jnp.abs(x)))


def _dt_prep(dt_raw, bias, a_log, name):
    s = dt_raw.shape[0]
    nc = s // T

    def body(r_ref, b_ref, a_ref, dt_ref, cum_ref, cumt_ref):
        dt = _softplus(r_ref[...] + b_ref[...])
        da = dt * (-jnp.exp(a_ref[...]))
        pre = _dot_exact(_tri(True).astype(BF16), da)
        suf = _dot_exact(_tri(False).astype(BF16), da)
        lane = lax.broadcasted_iota(jnp.int32, (T, LANE), 1)
        cum = jnp.where(lane < SSM_HEADS, pre, suf)
        dt_ref[...] = dt
        cum_ref[...] = cum
        cumt_ref[...] = cum.T

    blk = pl.BlockSpec((T, LANE), lambda c: (c, 0))
    vec = pl.BlockSpec((1, LANE), lambda c: (0, 0))
    return pl.pallas_call(
        body, name=name, grid=(nc,), in_specs=[blk, vec, vec],
        out_specs=[blk, blk, pl.BlockSpec((None, LANE, T), lambda c: (c, 0, 0))],
        out_shape=[jax.ShapeDtypeStruct((s, LANE), F32), jax.ShapeDtypeStruct((s, LANE), F32), jax.ShapeDtypeStruct((nc, LANE, T), F32)],
        compiler_params=_cp("parallel"),
    )(dt_raw, bias, a_log)


def _by_group(t):
    s = t.shape[0]
    return t[:, :2 * SSM_HEADS].reshape(s, 2 * SSM_GROUPS, SSM_REP).transpose(1, 0, 2)


def _from_group(tf, tb):
    s = tf.shape[1]
    t = jnp.concatenate([tf, tb], axis=0).transpose(1, 0, 2).reshape(s, 2 * SSM_HEADS)
    return jnp.pad(t, ((0, 0), (0, LANE - 2 * SSM_HEADS)))


def _decay_mats(acol, arow, rev):
    after = _tri(not rev)
    return jnp.where(after, jnp.exp(jnp.where(after, acol - arow, 0.0)), 0.0)


PAIRS = SSM_REP // 2


def _low_lanes():
    return lax.broadcasted_iota(jnp.int32, (T, LANE), 1) < SSM_HEAD_DIM


CPS = 8
TB = CPS * T


def _scan_specs(rev, ci):
    nxb = SSM_INNER // LANE
    kofs = SSM_GROUPS if rev else 0
    return [
        pl.BlockSpec((TB, GW), lambda g, c: (ci(c), g)),
        pl.BlockSpec((TB, LANE), lambda g, c: (ci(c), nxb + g)),
        pl.BlockSpec((TB, LANE), lambda g, c: (ci(c), nxb + SSM_GROUPS + g)),
        pl.BlockSpec((None, TB, SSM_REP), lambda g, c: (kofs + g, ci(c), 0)),
        pl.BlockSpec((None, TB, SSM_REP), lambda g, c: (kofs + g, ci(c), 0)),
        pl.BlockSpec((None, CPS, SSM_REP, T), lambda g, c: (kofs + g, ci(c), 0, 0)),
    ]


def _chunk_rows(q):
    return pl.ds(q * T, T)


def _pair_lanes(ref, p, low):
    return jnp.where(low, ref[:, 2 * p:2 * p + 1], ref[:, 2 * p + 1:2 * p + 2])


def _ssd_scan(xbc, dtk, cumk, cumtk, rev, name, add=None):
    s = xbc.shape[0]
    nc = s // T
    nb = nc // CPS
    last = 0 if rev else T - 1
    ci = (lambda c: nb - 1 - c) if rev else (lambda c: c)
    has_add = add is not None

    def body(*refs):
        x_ref, b_ref, c_ref, dt_ref, cum_ref, cumt_ref = refs[:6]
        a_ref = refs[6] if has_add else None
        y_ref, st_ref, state = refs[-3:]

        @pl.when(pl.program_id(1) == 0)
        def _():
            state[...] = jnp.zeros_like(state)

        for q in (reversed(range(CPS)) if rev else range(CPS)):
            rows = _chunk_rows(q)
            chunk(x_ref.at[rows], b_ref.at[rows], c_ref.at[rows], dt_ref.at[rows], cum_ref.at[rows], cumt_ref.at[q],
                  a_ref.at[rows] if has_add else None, y_ref.at[rows], st_ref.at[q], state)

    def chunk(x_ref, b_ref, c_ref, dt_ref, cum_ref, cumt_ref, a_ref, y_ref, st_ref, state):
        bm = b_ref[...]
        cm = c_ref[...].astype(BF16)
        cb = _dot(cm, bm.astype(BF16), 1, 1)
        bt = bm.T.astype(BF16)
        low = _low_lanes()
        for p in range(PAIRS):
            ls = slice(p * LANE, (p + 1) * LANE)
            acum = _pair_lanes(cum_ref, p, low)
            u = x_ref[:, ls] * _pair_lanes(dt_ref, p, low)
            tot = acum[last:last + 1, :]
            m = [(cb * _decay_mats(cum_ref[:, r:r + 1], cumt_ref[r:r + 1, :], rev)).astype(BF16) for r in (2 * p, 2 * p + 1)]
            st = state[p]
            st_ref[p] = st
            yd = _dot(jnp.concatenate(m, axis=1), _block_diag(u.astype(BF16), low))
            yo = jnp.exp(acum) * _dot(cm, st.astype(BF16))
            y_ref[:, ls] = yd + yo + a_ref[:, ls] if has_add else yd + yo
            state[p] = jnp.exp(tot) * st + _dot(bt, (jnp.exp(tot - acum) * u).astype(BF16))

    yspec = pl.BlockSpec((TB, GW), lambda g, c: (ci(c), g))
    return pl.pallas_call(
        body, name=name, grid=(SSM_GROUPS, nb), in_specs=_scan_specs(rev, ci) + ([yspec] if has_add else []),
        out_specs=[yspec, pl.BlockSpec((CPS, PAIRS, SSM_STATE, LANE), lambda g, c: (ci(c), g, 0, 0))],
        out_shape=[jax.ShapeDtypeStruct((s, SSM_INNER), F32), jax.ShapeDtypeStruct((nc, SSM_HEADS // 2, SSM_STATE, LANE), F32)],
        scratch_shapes=[pltpu.VMEM((PAIRS, SSM_STATE, LANE), F32)],
        compiler_params=_cp("parallel", "arbitrary"),
    )(xbc, xbc, xbc, dtk, cumk, cumtk, *([add] if has_add else []))


def _ssd_scan_bwd(xbc, dtk, cumk, cumtk, dy, states, dvec, prev, rev, name, exch=()):
    s = xbc.shape[0]
    nc = s // T
    nb = nc // CPS
    last = 0 if rev else T - 1
    ci = (lambda c: c) if rev else (lambda c: nb - 1 - c)
    has_prev = prev is not None
    n_in = 12 if has_prev else 9
    n_ex = len(exch)

    def body(*refs):
        ins, ex_in, refs = refs[:n_in], refs[n_in:n_in + n_ex], refs[n_in + n_ex:]
        outs, ex_out, scratch = refs[:5], refs[5:5 + n_ex], refs[5 + n_ex:]
        if n_ex:
            scratch, sems = scratch[:4], scratch[4:]
            g, c = pl.program_id(0), pl.program_id(1)
            _ride_along(ex_in, ex_out, sems, (g == 0) & (c == 0), (g == SSM_GROUPS - 1) & (c == nb - 1))

        @pl.when(pl.program_id(1) == 0)
        def _():
            scratch[0][...] = jnp.zeros_like(scratch[0])

        for q in (range(CPS) if rev else reversed(range(CPS))):
            rows = _chunk_rows(q)
            cut = lambda t: t.at[rows]
            x_ref, b_ref, c_ref, dt_ref, cum_ref, cumt_ref, dy_ref, st_ref, dv_ref = ins[:9]
            sub = [cut(x_ref), cut(b_ref), cut(c_ref), cut(dt_ref), cut(cum_ref), cumt_ref.at[q], cut(dy_ref), st_ref.at[q], dv_ref]
            chunk(*sub, *[cut(t) for t in ins[9:]], *[cut(t) for t in outs], *scratch)

    def chunk(*refs):
        x_ref, b_ref, c_ref, dt_ref, cum_ref, cumt_ref, dy_ref, st_ref, dv_ref = refs[:9]
        refs = refs[9:]
        if has_prev:
            pdx, pdb, pdc = refs[:3]
            refs = refs[3:]
        dx_ref, db_ref, dc_ref, ddt_ref, dda_ref, dstate, rs_buf, in_buf, k_buf = refs
        rs_buf[...] = jnp.zeros_like(rs_buf)
        in_buf[...] = jnp.zeros_like(in_buf)
        k_buf[...] = jnp.zeros_like(k_buf)
        bm = b_ref[...].astype(BF16)
        cm = c_ref[...].astype(BF16)
        cbt = _dot(bm, cm, 1, 1)
        cb = _dot(cm, bm, 1, 1)
        ct = c_ref[...].T.astype(BF16)
        after = _tri(not rev)
        before = _tri(rev)
        from_k = before.astype(BF16)
        ri = lax.broadcasted_iota(jnp.int32, (T, T), 0)
        cj = lax.broadcasted_iota(jnp.int32, (T, T), 1)
        strictly_before = (cj > ri) if rev else (cj < ri)
        dcb = jnp.zeros((T, T), F32)
        dc_acc = jnp.zeros((T, SSM_STATE), F32)
        db_acc = jnp.zeros((T, SSM_STATE), F32)
        low = _low_lanes()
        ri2 = lax.broadcasted_iota(jnp.int32, (LANE, LANE), 0)
        cj2 = lax.broadcasted_iota(jnp.int32, (LANE, LANE), 1)
        halves = ((ri2 < SSM_HEAD_DIM) == (cj2 == 0)) & (cj2 < 2)
        halves = halves.astype(BF16)

        def head_sums(v):
            hi = v.astype(BF16)
            lo = (v - hi.astype(F32)).astype(BF16)
            return _dot(hi, halves) + _dot(lo, halves)

        for p in range(PAIRS):
            ls = slice(p * LANE, (p + 1) * LANE)
            c2 = slice(2 * p, 2 * p + 2)
            lm, lmt = [], []
            for r in (2 * p, 2 * p + 1):
                acol = cum_ref[:, r:r + 1]
                arow = cumt_ref[r:r + 1, :]
                lm.append(jnp.where(after, jnp.exp(jnp.where(after, acol - arow, 0.0)), 0.0))
                lmt.append(jnp.where(before, jnp.exp(jnp.where(before, arow - acol, 0.0)), 0.0))
            acum = _pair_lanes(cum_ref, p, low)
            tot = acum[last:last + 1, :]
            dtl = _pair_lanes(dt_ref, p, low)
            xl = x_ref[:, ls]
            u = xl * dtl
            ub = u.astype(BF16)
            dyl = dy_ref[:, ls]
            dyb = dyl.astype(BF16)
            st = st_ref[p]
            stb = st.astype(BF16)
            dst = dstate[p]
            dstb = dst.astype(BF16)
            dec = jnp.exp(tot - acum)
            eac = jnp.exp(acum)
            etot = jnp.exp(tot)
            du_off = dec * _dot(bm, dstb)
            mt = jnp.concatenate([(cbt * lmt[0]).astype(BF16), (cbt * lmt[1]).astype(BF16)], axis=1)
            dyd = _block_diag(dyb, low)
            du = _dot(mt, dyd) + du_off
            g2 = _dot(dyd, ub, 1, 1)
            gl = [g2[:T] * lm[0], g2[T:] * lm[1]]
            dcb = dcb + gl[0] + gl[1]
            dc_acc = dc_acc + _dot((eac * dyl).astype(BF16), stb, 1, 1)
            db_acc = db_acc + _dot((dec * u).astype(BF16), dstb, 1, 1)
            w = jnp.concatenate([(gl[0] * cb).astype(BF16), (gl[1] * cb).astype(BF16)], axis=1)
            crossing = _dot(from_k, w)
            for j in range(2):
                cr = jnp.where(strictly_before, crossing[:, j * T:(j + 1) * T], 0.0)
                in_buf[:, 2 * p + j:2 * p + j + 1] = jnp.sum(cr, axis=1, keepdims=True)
            y_off = eac * _dot(cm, stb)
            udu = u * du_off
            rs_buf[:, c2] = head_sums(dyl * y_off - udu)[:, 0:2]
            col = jnp.sum(dst * (etot * st) + udu, axis=0, keepdims=True)
            k_buf[0:1, c2] = head_sums(jnp.broadcast_to(col, (8, LANE)))[0:1, 0:2]
            ddt_ref[:, c2] = head_sums(du * xl)[:, 0:2]
            dx = du * dtl
            if has_prev:
                dx = dx + pdx[:, ls]
            else:
                dx = dx + dyl * dv_ref[:, ls]
            dx_ref[:, ls] = dx
            dstate[p] = etot * dst + _dot(ct, (eac * dyl).astype(BF16))
        dda = in_buf[...] + _dot_exact(from_k, rs_buf[...]) + k_buf[0:1, :]
        dda_ref[...] = dda[:, :SSM_REP]
        dcbb = dcb.astype(BF16)
        dc = dc_acc + _dot(dcbb, bm)
        db = db_acc + _dot(dcbb, cm, 0, 0)
        if has_prev:
            dc = dc + pdc[...]
            db = db + pdb[...]
        dc_ref[...] = dc
        db_ref[...] = db

    xspec = pl.BlockSpec((TB, GW), lambda g, c: (ci(c), g))
    gspec = pl.BlockSpec((TB, LANE), lambda g, c: (ci(c), g))
    in_specs = _scan_specs(rev, ci) + [
        xspec,
        pl.BlockSpec((CPS, PAIRS, SSM_STATE, LANE), lambda g, c: (ci(c), g, 0, 0)),
        pl.BlockSpec((1, GW), lambda g, c: (0, g)),
    ]
    args = [xbc, xbc, xbc, dtk, cumk, cumtk, dy, states, dvec]
    if has_prev:
        in_specs += [xspec, gspec, gspec]
        args += list(prev)
    ospec8 = pl.BlockSpec((None, TB, SSM_REP), lambda g, c: (g, ci(c), 0))
    ex_specs, ex_shapes, sems = _ride_along_specs(exch)
    outs = pl.pallas_call(
        body, name=name, grid=(SSM_GROUPS, nb), in_specs=in_specs + ex_specs,
        out_specs=[xspec, gspec, gspec, ospec8, ospec8] + ex_specs,
        out_shape=[jax.ShapeDtypeStruct((s, SSM_INNER), F32), jax.ShapeDtypeStruct((s, SSM_BC), F32), jax.ShapeDtypeStruct((s, SSM_BC), F32),
                   jax.ShapeDtypeStruct((SSM_GROUPS, s, SSM_REP), F32), jax.ShapeDtypeStruct((SSM_GROUPS, s, SSM_REP), F32)]
        + ex_shapes,
        scratch_shapes=[pltpu.VMEM((PAIRS, SSM_STATE, LANE), F32), pltpu.VMEM((T, LANE), F32), pltpu.VMEM((T, LANE), F32),
                        pltpu.VMEM((8, LANE), F32)] + sems,
        compiler_params=_cp("arbitrary" if n_ex else "parallel", "arbitrary"),
    )(*args, *exch)
    return (outs[:5], outs[5:]) if n_ex else outs


def _ssd_post(y, xbc, z, dvec, nw, name):
    s = z.shape[0]
    tm = min(256, s)

    def body(y_ref, x_ref, z_ref, dv_ref, nw_ref, o_ref):
        ys = y_ref[...] + dv_ref[...] * x_ref[...]
        yg = ys * _silu(z_ref[...])
        ms = jnp.mean(yg * yg, axis=1, keepdims=True)
        o_ref[...] = (yg * lax.rsqrt(ms + RMS_EPS) * nw_ref[...]).astype(BF16)

    rs = _row_spec(tm, SSM_INNER)
    vs = _vec_spec(SSM_INNER)
    return pl.pallas_call(
        body, name=name, grid=(s // tm,), in_specs=[rs, rs, rs, vs, vs], out_specs=rs,
        out_shape=jax.ShapeDtypeStruct((s, SSM_INNER), BF16), compiler_params=_cp("parallel"),
    )(y, xbc, z, dvec, nw)


def _ssd_post_bwd(dyn, y, xbc, z, dvec, nw, name):
    s = z.shape[0]
    tm = min(256, s)

    def body(dyn_ref, y_ref, x_ref, z_ref, dv_ref, nw_ref, dys_ref, dz_ref, red_ref):
        @pl.when(pl.program_id(0) == 0)
        def _():
            red_ref[...] = jnp.zeros_like(red_ref)

        xv, zv = x_ref[...], z_ref[...]
        ys = y_ref[...] + dv_ref[...] * xv
        sz = _silu(zv)
        yg = ys * sz
        rstd = lax.rsqrt(jnp.mean(yg * yg, axis=1, keepdims=True) + RMS_EPS)
        yhat = yg * rstd
        dynv = dyn_ref[...]
        dyh = dynv * nw_ref[...]
        dyg = rstd * (dyh - yhat * jnp.mean(dyh * yhat, axis=1, keepdims=True))
        dys = dyg * sz
        dys_ref[...] = dys
        dz_ref[...] = (dyg * ys * _dsilu(zv)).astype(BF16)
        red_ref[0:1, :] += jnp.sum(dynv * yhat, axis=0, keepdims=True)
        red_ref[1:2, :] += jnp.sum(dys * xv, axis=0, keepdims=True)

    rs = _row_spec(tm, SSM_INNER)
    vs = _vec_spec(SSM_INNER)
    return pl.pallas_call(
        body, name=name, grid=(s // tm,), in_specs=[rs, rs, rs, rs, vs, vs],
        out_specs=[rs, rs, _vec_spec(SSM_INNER, 8)],
        out_shape=[jax.ShapeDtypeStruct((s, SSM_INNER), F32), jax.ShapeDtypeStruct((s, SSM_INNER), BF16), jax.ShapeDtypeStruct((8, SSM_INNER), F32)],
        compiler_params=_cp("arbitrary"),
    )(dyn, y, xbc, z, dvec, nw)


def _dt_bwd(dt_raw, bias, a_log, dt, ddt, dda, name):
    s = dt_raw.shape[0]
    tm = min(1024, s)

    def body(r_ref, b_ref, a_ref, dt_ref, ddt_ref, dda_ref, o_ref, red_ref):
        @pl.when(pl.program_id(0) == 0)
        def _():
            red_ref[...] = jnp.zeros_like(red_ref)

        a = -jnp.exp(a_ref[...])
        ddav = dda_ref[...]
        draw = (ddt_ref[...] + a * ddav) * _sigmoid(r_ref[...] + b_ref[...])
        o_ref[...] = draw.astype(BF16)
        red_ref[0:1, :] += jnp.sum(draw, axis=0, keepdims=True)
        red_ref[1:2, :] += a * jnp.sum(ddav * dt_ref[...], axis=0, keepdims=True)

    rs = _row_spec(tm, LANE)
    vs = _vec_spec(LANE)
    return pl.pallas_call(
        body, name=name, grid=(s // tm,), in_specs=[rs, vs, vs, rs, rs, rs], out_specs=[rs, _vec_spec(LANE, 8)],
        out_shape=[jax.ShapeDtypeStruct((s, LANE), BF16), jax.ShapeDtypeStruct((8, LANE), F32)],
        compiler_params=_cp("arbitrary"),
    )(dt_raw, bias, a_log, dt, ddt, dda)


def _pad_lanes(v):
    v = v.reshape(1, -1)
    return jnp.pad(v, ((0, 0), (0, LANE - v.shape[1])))


def _ssd_prep_weights(w_in, conv_w, conv_b, dt_bias, a_log, d_skip, norm_w, w_out):
    return dict(
        w_z=w_in[:, :SSM_INNER].astype(BF16),
        w_xbc=w_in[:, SSM_INNER:SSM_INNER + SSM_CONV_DIM].astype(BF16),
        w_dt=jnp.pad(w_in[:, SSM_INNER + SSM_CONV_DIM:], ((0, 0), (0, LANE - 2 * SSM_HEADS))).astype(BF16),
        conv_w=conv_w, conv_b=conv_b.reshape(1, -1), bias=_pad_lanes(dt_bias), a_log=_pad_lanes(a_log),
        dvec=jnp.repeat(d_skip, SSM_HEAD_DIM).reshape(1, -1), nw=norm_w.reshape(1, -1), w_out=w_out.astype(BF16),
    )


def _ssd_layer_fwd(h, w, li):
    nm = lambda t: f"b{li}_{t}"
    z = _mm(h, w["w_z"], name=nm("z"))
    xraw = _mm(h, w["w_xbc"], name=nm("xbc"))
    dt_raw = _mm(h, w["w_dt"], name=nm("dt"))
    pre, xbc = _conv_fwd(xraw, w["conv_w"], w["conv_b"], nm("conv"))
    dt, cum, cumt = _dt_prep(dt_raw, w["bias"], w["a_log"], nm("dtprep"))
    nc = cumt.shape[0]
    dtk, cumk = _by_group(dt), _by_group(cum)
    cumtk = cumt[:, :2 * SSM_HEADS].reshape(nc, 2 * SSM_GROUPS, SSM_REP, T).transpose(1, 0, 2, 3)
    yf, stf = _ssd_scan(xbc, dtk, cumk, cumtk, False, nm("scan_f"))
    y, stb = _ssd_scan(xbc, dtk, cumk, cumtk, True, nm("scan_b"), add=yf)
    yn = _ssd_post(y, xbc, z, w["dvec"], w["nw"], nm("post"))
    out = _mm(yn, w["w_out"], name=nm("out"))
    return out, (z, xraw, dt_raw, pre, xbc, dt, dtk, cumk, cumtk, y, stf, stb, yn)


def _ssd_layer_bwd(dy, h, saved, w, li, exch=((), ())):
    nm = lambda t: f"b{li}_{t}"
    z, xraw, dt_raw, pre, xbc, dt, dtk, cumk, cumtk, y, stf, stb, yn = saved
    g_w_out = _mm(yn, dy, ta=True, out_dtype=BF16, name=nm("dwout"))
    dyn = _mm(dy, w["w_out"], tb=True, name=nm("dyn"))
    dys, dz, pred = _ssd_post_bwd(dyn, y, xbc, z, w["dvec"], w["nw"], nm("postbwd"))
    arrived = [(), ()]
    res = _ssd_scan_bwd(xbc, dtk, cumk, cumtk, dys, stf, w["dvec"], None, False, nm("scanbwd_f"), exch[0])
    if exch[0]:
        res, arrived[0] = res
    dx1, db1, dc1, ddt_f, dda_f = res
    res = _ssd_scan_bwd(xbc, dtk, cumk, cumtk, dys, stb, w["dvec"], (dx1, db1, dc1), True, nm("scanbwd_b"), exch[1])
    if exch[1]:
        res, arrived[1] = res
    dx, db, dc, ddt_b, dda_b = res
    dxraw, cred = _conv_bwd(dx, db, dc, pre, xraw, w["conv_w"], nm("convbwd"))
    draw, dred = _dt_bwd(dt_raw, w["bias"], w["a_log"], dt, _from_group(ddt_f, ddt_b), _from_group(dda_f, dda_b), nm("dtbwd"))
    dh = _mm(dz, w["w_z"], tb=True, name=nm("dh_z"))
    dh = _mm(dxraw, w["w_xbc"], tb=True, add=dh, name=nm("dh_xbc"))
    dh = _mm(draw, w["w_dt"], tb=True, add=dh, name=nm("dh_dt"))
    g_w_in = jnp.concatenate([_mm(h, dz, ta=True, out_dtype=BF16, name=nm("dwz")), _mm(h, dxraw, ta=True, out_dtype=BF16, name=nm("dwxbc")),
                              _mm(h, draw, ta=True, out_dtype=BF16, name=nm("dwdt"))[:, :2 * SSM_HEADS]], axis=1)
    grads = (g_w_in, cred[:SSM_CONV], cred[SSM_CONV], dred[0, :2 * SSM_HEADS].reshape(2, SSM_HEADS),
             dred[1, :2 * SSM_HEADS].reshape(2, SSM_HEADS), pred[1].reshape(SSM_HEADS, SSM_HEAD_DIM).sum(axis=1), pred[0], g_w_out)
    return dh, grads, arrived


B_GRAD_NAMES = ("b_w_in", "b_conv_w", "b_conv_b", "b_dt_bias", "b_a_log", "b_d", "b_norm_w", "b_w_out")


def _shards_of(big, j):
    return [_shard_cols(big["a_w_in"][j]), big["a_w_out"][j].reshape(N_SHARD, -1, big["a_w_out"][j].shape[-1]),
            _shard_cols(big["b_w_in"][j]), big["b_w_out"][j].reshape(N_SHARD, -1, big["b_w_out"][j].shape[-1])]


def _local_step(x, tgt, mod, w, early=False):
    d = x.shape[1]
    qkv_cols = QKV_COLS
    layers = []
    for i in range(DEPTH):
        j = i // 2
        if i % 2 == 0:
            layers.append((w["a_w_in"][j][:, :qkv_cols].astype(BF16), w["a_w_in"][j][:, qkv_cols:].astype(BF16), w["a_w_out"][j].astype(BF16)))
        else:
            layers.append(_ssd_prep_weights(w["b_w_in"][j], w["b_conv_w"][j], w["b_conv_b"][j], w["b_dt_bias"][j], w["b_a_log"][j],
                                            w["b_d"][j], w["b_norm_w"][j], w["b_w_out"][j]))
    saved = []
    h = _modulate(x, mod[0:1, d:2 * d], mod[0:1, :d], "l0_mod")
    for i in range(DEPTH):
        gate = mod[i:i + 1, 2 * d:]
        if i % 2 == 0:
            out, sv = _attn_layer_fwd(h, *layers[i], i)
        else:
            out, sv = _ssd_layer_fwd(h, layers[i], i)
        saved.append((x, h, out, sv))
        if i + 1 < DEPTH:
            x, h = _resid_ln(x, out, gate, w["ln_g"][i:i + 1], w["ln_b"][i:i + 1], f"l{i}_ln",
                             (mod[i + 1:i + 2, d:2 * d], mod[i + 1:i + 2, :d]))
        else:
            x = _resid_ln(x, out, gate, w["ln_g"][i:i + 1], w["ln_b"][i:i + 1], f"l{i}_ln")
    dx, lred = _loss_grad(x, tgt, "loss")
    loss = 0.5 * jnp.sum(lred[0]) / d
    dmod, g_ln_g, g_ln_b = [None] * DEPTH, [None] * DEPTH, [None] * DEPTH
    ga_in, ga_out = [None, None], [None, None]
    gb = [None, None]
    arrived = None
    for i in reversed(range(DEPTH)):
        j = i // 2
        xi, h, out, sv = saved[i]
        scale, gate = mod[i:i + 1, d:2 * d], mod[i:i + 1, 2 * d:]
        du, dy, red = _resid_ln_bwd(xi, out, dx, gate, w["ln_g"][i:i + 1], f"l{i}_lnbwd")
        g_ln_g[i], g_ln_b[i] = red[1], red[2]
        if i % 2 == 0:
            exch = ()
            if early and j == 0:
                exch = [_shard_cols(gb[0][0]), gb[0][7].reshape(N_SHARD, -1, gb[0][7].shape[-1])]
            dhs, ga_in[j], ga_out[j], got = _attn_layer_bwd(dy, h, sv, *layers[i], i, exch)
            if early and j == 0:
                arrived = (list(got), arrived)
        else:
            exch = ((), ())
            if early and j == 0:
                sh = _shards_of(dict(a_w_in=ga_in, a_w_out=ga_out, b_w_in=[None, gb[1][0]], b_w_out=[None, gb[1][7]]), 1)
                exch = (sh[:2], sh[2:])
            dh, gb[j], got = _ssd_layer_bwd(dy, h, sv, layers[i], i, exch)
            if early and j == 0:
                arrived = list(got[0]) + list(got[1])
            dhs = [dh]
        dx, red2 = _modulate_bwd(du, dhs, xi, scale, f"l{i}_modbwd")
        dmod[i] = jnp.concatenate([red2[1], red2[0], red[0]])
    grads = {"ln_g": jnp.stack(g_ln_g), "ln_b": jnp.stack(g_ln_b), "a_w_in": jnp.stack(ga_in), "a_w_out": jnp.stack(ga_out)}
    for k, n in enumerate(B_GRAD_NAMES):
        grads[n] = jnp.stack([gb[0][k], gb[1][k]])
    big = dict(a_w_in=ga_in, a_w_out=ga_out, b_w_in=[gb[0][0], gb[1][0]], b_w_out=[gb[0][7], gb[1][7]])
    return loss, dx, jnp.stack(dmod), grads, big, arrived


MESH = pl.DeviceIdType.MESH
ANY = pl.BlockSpec(memory_space=pl.ANY)
N_DEV = 8
N_SHARD = 4


def _flip(v, bit):
    return 1 - v if bit else v


def _all_gather8(v, name):
    def body(v_ref, o_ref, send_sems, recv_sems, local_sem):
        x, y, c = lax.axis_index("x"), lax.axis_index("y"), lax.axis_index("c")
        me = 4 * x + 2 * y + c
        local = pltpu.make_async_copy(v_ref, o_ref.at[me], local_sem)
        local.start()
        copies = []
        for k in range(1, N_DEV):
            peer = (_flip(x, k & 4), _flip(y, k & 2), _flip(c, k & 1))
            copies.append(pltpu.make_async_remote_copy(
                src_ref=v_ref, dst_ref=o_ref.at[me], send_sem=send_sems.at[k - 1], recv_sem=recv_sems.at[k - 1],
                device_id=peer, device_id_type=MESH))
        for cp in copies:
            cp.start()
        for cp in copies:
            cp.wait()
        local.wait()

    return pl.pallas_call(
        body, name=name, in_specs=[ANY], out_specs=ANY, out_shape=jax.ShapeDtypeStruct((N_DEV,) + v.shape, v.dtype),
        scratch_shapes=[pltpu.SemaphoreType.DMA((N_DEV - 1,)), pltpu.SemaphoreType.DMA((N_DEV - 1,)), pltpu.SemaphoreType.DMA],
    )(v)


def _shard_exchange(s_refs, o_refs, send_sems, recv_sems, local_sems):
    n = len(s_refs)
    x, y, c = lax.axis_index("x"), lax.axis_index("y"), lax.axis_index("c")
    m = 2 * x + y
    copies = [pltpu.make_async_copy(s_refs[a].at[m], o_refs[a].at[m], local_sems.at[a]) for a in range(n)]
    for k in range(1, N_SHARD):
        px, py = _flip(x, k & 2), _flip(y, k & 1)
        for a in range(n):
            i = (k - 1) * n + a
            copies.append(pltpu.make_async_remote_copy(
                src_ref=s_refs[a].at[2 * px + py], dst_ref=o_refs[a].at[m], send_sem=send_sems.at[i], recv_sem=recv_sems.at[i],
                device_id=(px, py, c), device_id_type=MESH))
    return copies


def _transpose_shards(srcs, name):
    n = len(srcs)
    n_rem = (N_SHARD - 1) * n

    def body(*refs):
        copies = _shard_exchange(refs[:n], refs[n:2 * n], *refs[2 * n:])
        for cp in copies:
            cp.start()
        for cp in copies:
            cp.wait()

    return pl.pallas_call(
        body, name=name, in_specs=[ANY] * n, out_specs=[ANY] * n, out_shape=[jax.ShapeDtypeStruct(s.shape, s.dtype) for s in srcs],
        scratch_shapes=[pltpu.SemaphoreType.DMA((n_rem,)), pltpu.SemaphoreType.DMA((n_rem,)), pltpu.SemaphoreType.DMA((n,))],
    )(*srcs)


def _gather_shards(src, name):
    rows = src.shape[0]
    half = rows // 2
    n_ici = N_SHARD - 1

    def body(s_ref, o_ref, send_sems, recv_sems, local_sem):
        x, y, c = lax.axis_index("x"), lax.axis_index("y"), lax.axis_index("c")
        m = 2 * x + y
        sibling = (x, y, 1 - c)
        my_half = pl.ds(pl.multiple_of(c * half, 16), half)
        its_half = pl.ds(pl.multiple_of((1 - c) * half, 16), half)
        local = pltpu.make_async_copy(s_ref, o_ref.at[m], local_sem)
        local.start()
        chips = [(_flip(x, k & 2), _flip(y, k & 1)) for k in range(1, N_SHARD)]

        def copy(sem, src_ref, dst_ref, to):
            return pltpu.make_async_remote_copy(src_ref=src_ref, dst_ref=dst_ref, send_sem=send_sems.at[sem],
                                                recv_sem=recv_sems.at[sem], device_id=to, device_id_type=MESH)

        first = [copy(i, s_ref.at[my_half], o_ref.at[m, my_half], (px, py, c)) for i, (px, py) in enumerate(chips)]
        for cp in first:
            cp.start()
        passed = []
        for i, (px, py) in enumerate(chips):
            landed = o_ref.at[2 * px + py, my_half]
            copy(i, landed, landed, (px, py, c)).wait_recv()
            passed.append(copy(n_ici + i, landed, landed, sibling))
            passed[-1].start()
        for i, (px, py) in enumerate(chips):
            from_sibling = o_ref.at[2 * px + py, its_half]
            copy(n_ici + i, from_sibling, from_sibling, sibling).wait_recv()
        for cp in first + passed:
            cp.wait_send()
        local.wait()

    return pl.pallas_call(
        body, name=name, in_specs=[ANY], out_specs=ANY, out_shape=jax.ShapeDtypeStruct((N_SHARD,) + src.shape, src.dtype),
        scratch_shapes=[pltpu.SemaphoreType.DMA((2 * n_ici,)), pltpu.SemaphoreType.DMA((2 * n_ici,)), pltpu.SemaphoreType.DMA],
    )(src)


def _swap_sibling(vs, name):
    n = len(vs)

    def body(*refs):
        v_refs, o_refs, (send_sems, recv_sems) = refs[:n], refs[n:2 * n], refs[2 * n:]
        x, y, c = lax.axis_index("x"), lax.axis_index("y"), lax.axis_index("c")
        copies = [pltpu.make_async_remote_copy(src_ref=v_refs[a], dst_ref=o_refs[a], send_sem=send_sems.at[a], recv_sem=recv_sems.at[a],
                                               device_id=(x, y, 1 - c), device_id_type=MESH) for a in range(n)]
        for cp in copies:
            cp.start()
        for cp in copies:
            cp.wait()

    return pl.pallas_call(
        body, name=name, in_specs=[ANY] * n, out_specs=[ANY] * n, out_shape=[jax.ShapeDtypeStruct(v.shape, v.dtype) for v in vs],
        scratch_shapes=[pltpu.SemaphoreType.DMA((n,)), pltpu.SemaphoreType.DMA((n,))],
    )(*vs)


def _row_tile(r, elems, step):
    ok = [t for t in range(step, r + 1, step) if r % t == 0 and t <= elems]
    return max(ok) if ok else r


def _sum_slots(a, name):
    n, r, cdim = a.shape
    tm = _row_tile(r, (4 << 20) // (cdim * 4 * (n + 1)), 16)

    def body(a_ref, o_ref):
        acc = a_ref[0].astype(F32)
        for k in range(1, n):
            acc = acc + a_ref[k].astype(F32)
        o_ref[...] = acc

    return pl.pallas_call(
        body, name=name, grid=(r // tm,), in_specs=[pl.BlockSpec((n, tm, cdim), lambda i: (0, i, 0))],
        out_specs=pl.BlockSpec((tm, cdim), lambda i: (i, 0)), out_shape=jax.ShapeDtypeStruct((r, cdim), F32),
        compiler_params=_cp("parallel"),
    )(a)


def _silu_rows(v, name):
    def body(v_ref, o_ref):
        o_ref[...] = _silu(v_ref[...])

    return pl.pallas_call(body, name=name, out_shape=jax.ShapeDtypeStruct(v.shape, F32))(v)


PACK_COLS = 1024


def _adamw(w, gs, m, v, name):
    r, cdim = w.shape
    tm = _row_tile(r, (1 << 18) // cdim, 8)
    c1 = 1.0 / (1.0 - ADAM_B1 ** ADAM_STEP)
    c2 = 1.0 / (1.0 - ADAM_B2 ** ADAM_STEP)
    ng = len(gs)

    def body(*refs):
        w_ref, g_refs, (m_ref, v_ref, g_ref, d_ref, nm_ref, nv_ref) = refs[0], refs[1:1 + ng], refs[1 + ng:]
        g = g_refs[0][...]
        for t in g_refs[1:]:
            g = g + t[...]
        mn = ADAM_B1 * m_ref[...] + (1.0 - ADAM_B1) * g
        vn = ADAM_B2 * v_ref[...] + (1.0 - ADAM_B2) * (g * g)
        g_ref[...] = g
        nm_ref[...] = mn
        nv_ref[...] = vn
        d_ref[...] = -ADAM_LR * ((mn * c1) / (jnp.sqrt(vn * c2) + ADAM_EPS) + ADAM_WD * w_ref[...])

    spec = pl.BlockSpec((tm, cdim), lambda i: (i, 0))
    return pl.pallas_call(
        body, name=name, grid=(r // tm,), in_specs=[spec] * (3 + ng), out_specs=[spec] * 4,
        out_shape=[jax.ShapeDtypeStruct(w.shape, F32)] * 4, compiler_params=_cp("parallel"),
    )(w, *gs, m, v)


def _rows(a):
    f = a.reshape(-1)
    pad = (-f.shape[0]) % PACK_COLS
    if pad:
        f = jnp.pad(f, (0, pad))
    return f.reshape(-1, PACK_COLS)


def _nrows(shape):
    return -(-int(np.prod(shape)) // PACK_COLS)


def _pack(parts, total_rows=None):
    p = jnp.concatenate([_rows(a) for a in parts], axis=0)
    if total_rows is not None and total_rows > p.shape[0]:
        p = jnp.pad(p, ((0, total_rows - p.shape[0]), (0, 0)))
    return p


def _unpack(p, shapes):
    out, r0 = [], 0
    for shp in shapes:
        n = int(np.prod(shp))
        nr = _nrows(shp)
        out.append(p[r0:r0 + nr].reshape(-1)[:n].reshape(shp))
        r0 += nr
    return out


def _unshard_cols(g):
    return jnp.concatenate([g[k] for k in range(N_SHARD)], axis=-1)


def _shard_cols(a):
    n = a.shape[-1] // N_SHARD
    return jnp.stack([a[..., k * n:(k + 1) * n] for k in range(N_SHARD)])


def _unshard_rows(g):
    return jnp.concatenate([g[k] for k in range(N_SHARD)], axis=1)


def _shard_rows(a):
    n = a.shape[1] // N_SHARD
    return jnp.stack([a[:, k * n:(k + 1) * n] for k in range(N_SHARD)])


W_NAMES = ("ada_w", "ada_b", "ln_g", "ln_b", "a_w_in", "a_w_out", "b_w_in", "b_conv_w", "b_conv_b", "b_dt_bias", "b_a_log", "b_d",
           "b_norm_w", "b_w_out")
BIG = ("a_w_in", "a_w_out", "b_w_in", "b_w_out")
SMALL = ("ada_b", "ln_g", "ln_b", "b_conv_w", "b_conv_b", "b_dt_bias", "b_a_log", "b_d", "b_norm_w")


def kernel(x, c, ada_w, ada_b, ln_g, ln_b, a_w_in, a_w_out, b_w_in, b_conv_w, b_conv_b, b_dt_bias, b_a_log, b_d, b_norm_w, b_w_out, loss_target, m_ada_w, m_ada_b, m_ln_g, m_ln_b, m_a_w_in, m_a_w_out, m_b_w_in, m_b_conv_w, m_b_conv_b, m_b_dt_bias, m_b_a_log, m_b_d, m_b_norm_w, m_b_w_out, v_ada_w, v_ada_b, v_ln_g, v_ln_b, v_a_w_in, v_a_w_out, v_b_w_in, v_b_conv_w, v_b_conv_b, v_b_dt_bias, v_b_a_log, v_b_d, v_b_norm_w, v_b_w_out):
    w = dict(ada_w=ada_w, ada_b=ada_b, ln_g=ln_g, ln_b=ln_b, a_w_in=a_w_in, a_w_out=a_w_out, b_w_in=b_w_in, b_conv_w=b_conv_w,
             b_conv_b=b_conv_b, b_dt_bias=b_dt_bias, b_a_log=b_a_log, b_d=b_d, b_norm_w=b_norm_w, b_w_out=b_w_out)
    mom = dict(ada_w=m_ada_w, ada_b=m_ada_b, ln_g=m_ln_g, ln_b=m_ln_b, a_w_in=m_a_w_in, a_w_out=m_a_w_out, b_w_in=m_b_w_in,
               b_conv_w=m_b_conv_w, b_conv_b=m_b_conv_b, b_dt_bias=m_b_dt_bias, b_a_log=m_b_a_log, b_d=m_b_d, b_norm_w=m_b_norm_w,
               b_w_out=m_b_w_out)
    var = dict(ada_w=v_ada_w, ada_b=v_ada_b, ln_g=v_ln_g, ln_b=v_ln_b, a_w_in=v_a_w_in, a_w_out=v_a_w_out, b_w_in=v_b_w_in,
               b_conv_w=v_b_conv_w, b_conv_b=v_b_conv_b, b_dt_bias=v_b_dt_bias, b_a_log=v_b_a_log, b_d=v_b_d, b_norm_w=v_b_norm_w,
               b_w_out=v_b_w_out)
    ax, ay, ac = lax.axis_index("x"), lax.axis_index("y"), lax.axis_index("c")
    me = 4 * ax + 2 * ay + ac
    shard = 2 * ax + ay
    d = x.shape[-1]
    dsh = ada_w.shape[-1]

    small_in = (c, b_conv_w, b_conv_b, b_norm_w)
    g0 = _all_gather8(_pack(small_in).reshape(-1, LANE), "gather_small_in").reshape(N_DEV, -1, PACK_COLS)
    per_dev = [_unpack(g0[k], [a.shape for a in small_in]) for k in range(N_DEV)]
    c_all = jnp.concatenate([p[0] for p in per_dev], axis=0)
    conv_w_full, conv_b_full, norm_w_full = (_unshard_cols([per_dev[2 * k][t] for k in range(N_SHARD)]) for t in (1, 2, 3))

    cond = _silu_rows(jnp.pad(c_all, ((0, 8), (0, 0))), "cond")
    bias = lax.dynamic_slice_in_dim(ada_b, shard * dsh, dsh, axis=1)
    part = jnp.stack([_mm(cond, ada_w[i], add=jnp.broadcast_to(bias[i], (16, dsh)), name=f"mod{i}")[:N_DEV] for i in range(DEPTH)])
    g1 = _all_gather8(part.reshape(-1, LANE), "gather_mod").reshape(N_DEV, DEPTH, N_DEV, dsh)
    mod_all = _unshard_cols([g1[2 * k] for k in range(N_SHARD)])
    mod = lax.dynamic_index_in_dim(mod_all, me, axis=1, keepdims=False)

    gw = _gather_shards(_pack([w[n] for n in BIG]).astype(BF16), "gather_weights")
    big_sh = [_unpack(gw[k], [w[n].shape for n in BIG]) for k in range(N_SHARD)]
    full = dict(
        ln_g=ln_g, ln_b=ln_b, b_dt_bias=b_dt_bias, b_a_log=b_a_log, b_d=b_d,
        b_conv_w=conv_w_full, b_conv_b=conv_b_full, b_norm_w=norm_w_full,
        a_w_in=_unshard_cols([s[0] for s in big_sh]), a_w_out=_unshard_rows([s[1] for s in big_sh]),
        b_w_in=_unshard_cols([s[2] for s in big_sh]), b_w_out=_unshard_rows([s[3] for s in big_sh]),
    )

    loss, grad_x, dmod, g, big, (arrived_b0, arrived1) = _local_step(x[0], loss_target[0], mod, full, early=True)

    arrived0 = list(_transpose_shards(_shards_of(big, 0)[:2], "scatter_grads")) + arrived_b0
    mine = [jnp.concatenate([_sum_slots(arrived0[a], f"sum0_{n}"), _sum_slots(arrived1[a], f"sum1_{n}")], axis=0)
            for a, n in enumerate(BIG)]
    theirs = _swap_sibling(mine, "swap_grads")

    small_g = (dmod, g["ln_g"], g["ln_b"], g["b_dt_bias"], g["b_a_log"], g["b_d"], g["b_conv_w"], g["b_conv_b"], g["b_norm_w"],
               loss.reshape(1))
    g2 = _all_gather8(_pack(small_g).reshape(-1, LANE), "gather_small_grads")
    tot = _unpack(_sum_slots(g2, "sum_small").reshape(-1, PACK_COLS), [a.shape for a in small_g])
    g_ada_b, g_ln_g, g_ln_b, g_dt_bias, g_a_log, g_d, g_conv_w, g_conv_b, g_norm_w, loss_sum = tot
    dmod_all = g2.reshape(N_DEV, -1)[:, :dmod.size].reshape(N_DEV, DEPTH, 3 * d)
    dmod_mine = lax.dynamic_slice_in_dim(dmod_all, shard * dsh, dsh, axis=2)
    g_ada_w = jnp.stack([_mm(cond, jnp.pad(dmod_mine[:, i], ((0, 8), (0, 0))), ta=True, name=f"dada{i}") for i in range(DEPTH)])
    csh = g_conv_w.shape[-1] // N_SHARD
    nsh = g_norm_w.shape[-1] // N_SHARD
    small_grads = dict(
        ada_w=g_ada_w, ada_b=g_ada_b, ln_g=g_ln_g, ln_b=g_ln_b, b_dt_bias=g_dt_bias, b_a_log=g_a_log, b_d=g_d,
        b_conv_w=lax.dynamic_slice_in_dim(g_conv_w, shard * csh, csh, axis=2),
        b_conv_b=lax.dynamic_slice_in_dim(g_conv_b, shard * csh, csh, axis=1),
        b_norm_w=lax.dynamic_slice_in_dim(g_norm_w, shard * nsh, nsh, axis=1),
    )

    by_name = [{}, {}, {}, {}]

    def update(n, gs):
        two_d = lambda t: t.reshape(-1, t.shape[-1])
        outs = _adamw(two_d(w[n]), [two_d(t) for t in gs], two_d(mom[n]), two_d(var[n]), f"adamw_{n}")
        for t, o in zip(by_name, outs):
            t[n] = o.reshape(w[n].shape)

    for i, n in enumerate(BIG):
        update(n, [mine[i], theirs[i]])
    update("ada_w", [small_grads["ada_w"]])
    rest = SMALL
    rows = -(-sum(_nrows(w[n].shape) for n in rest) // 8) * 8
    packed = _adamw(_pack([w[n] for n in rest], rows), [_pack([small_grads[n] for n in rest], rows)],
                    _pack([mom[n] for n in rest], rows), _pack([var[n] for n in rest], rows), "adamw_small")
    for t, p in zip(by_name, packed):
        t.update(zip(rest, _unpack(p, [w[n].shape for n in rest])))
    return (loss_sum.reshape(()), grad_x[None], *[t[n] for t in by_name for n in W_NAMES])
```
